```python
import jax, jax.numpy as jnp
from jax import lax
import numpy as np

D_MODEL = 1024
BATCH = 8
SEQ = 2048
DEPTH = 2

GRID_W = 64
CTX_LEN = 256

N_HEADS = 8
QK_NOPE = 64
QK_ROPE = 32
V_DIM = 64
Q_RANK = 384
KV_RANK = 256
QK_DIM = QK_NOPE + QK_ROPE
ROPE_BASE = 10000.0
Q_BLOCK = 128
CONV_WIDTH = 512
CONV_K = 3
FOURIER_GROUPS = 4
FOURIER_GROUP_CH = 128
FOURIER_WIDTH = FOURIER_GROUPS * FOURIER_GROUP_CH
N_EXPERTS = 32
TOP_K = 4
F_EXPERT = 256
F_SHARED = 256
ROUTED_SCALE = 2.5
N_BRANCHES = 3
EPS = 1e-6

KV_END = KV_RANK + QK_ROPE
Q_END = KV_END + Q_RANK
CB_END = Q_END + CONV_WIDTH
CC_END = CB_END + CONV_WIDTH
CX_END = CC_END + CONV_WIDTH
FOUR_END = CX_END + FOURIER_WIDTH
PROJ_WIDTH = FOUR_END + N_BRANCHES * D_MODEL

kernel_name = "hybrid_mla_conv_fourier_moe_dit"


def rms_norm(x, g):
    xf = x.astype(jnp.float32)
    y = xf * lax.rsqrt(jnp.mean(xf * xf, axis=-1, keepdims=True) + EPS)
    return (y * g.astype(jnp.float32)).astype(x.dtype)


def modulate(h, shift, scale):
    return h * (1 + scale) + shift


def rope_tables(n):
    rows = n // GRID_W
    r, col = jnp.meshgrid(jnp.arange(rows), jnp.arange(GRID_W), indexing="ij")
    r = r.reshape(-1).astype(jnp.float32)
    col = col.reshape(-1).astype(jnp.float32)
    pairs = QK_ROPE // 4
    inv = ROPE_BASE ** (-jnp.arange(pairs, dtype=jnp.float32) / pairs)
    ang = jnp.concatenate([r[:, None] * inv, col[:, None] * inv], axis=-1)
    return jnp.cos(ang), jnp.sin(ang)


def apply_rope(x, cos, sin):
    half = QK_ROPE // 2
    xf = x.astype(jnp.float32)
    x1, x2 = xf[..., :half], xf[..., half:]
    return jnp.concatenate([x1 * cos - x2 * sin, x2 * cos + x1 * sin], axis=-1).astype(x.dtype)


def mla_kv(p_kv, g_kv, w_ukv, rope):
    B, N, _ = p_kv.shape
    ckv = rms_norm(p_kv[..., :KV_RANK], g_kv)
    k_pe = p_kv[..., KV_RANK:]
    if rope is not None:
        k_pe = apply_rope(k_pe, rope[0], rope[1])
    kv = (ckv @ w_ukv).reshape(B, N, N_HEADS, QK_NOPE + V_DIM)
    k_nope, v = kv[..., :QK_NOPE], kv[..., QK_NOPE:]
    k_pe = jnp.broadcast_to(k_pe[:, :, None, :], (B, N, N_HEADS, QK_ROPE))
    return jnp.concatenate([k_nope, k_pe], axis=-1), v


def mla_q(p_q, g_q, w_uq, rope):
    B, N, _ = p_q.shape
    cq = rms_norm(p_q, g_q)
    q = (cq @ w_uq).reshape(B, N, N_HEADS, QK_DIM)
    if rope is not None:
        q_pe = apply_rope(q[..., QK_NOPE:], rope[0][:, None, :], rope[1][:, None, :])
        q = jnp.concatenate([q[..., :QK_NOPE], q_pe], axis=-1)
    return q


def attention(q, k, v):
    B, N, H, _ = q.shape
    nb = N // Q_BLOCK
    qb = q.reshape(B, nb, Q_BLOCK, H, QK_DIM).swapaxes(0, 1)
    scale = QK_DIM ** -0.5

    def one_block(qblk):
        s = jnp.einsum("bqhd,bkhd->bhqk", qblk, k).astype(jnp.float32) * scale
        p = jax.nn.softmax(s, axis=-1)
        return jnp.einsum("bhqk,bkhv->bqhv", p.astype(v.dtype), v)

    o = lax.map(one_block, qb)
    return o.swapaxes(0, 1).reshape(B, N, H * V_DIM)


def short_conv(b_gate, c_gate, u, w_conv):
    N = u.shape[1]
    pad = CONV_K // 2
    inner = jnp.pad(c_gate * u, ((0, 0), (pad, pad), (0, 0)))
    y = sum(inner[:, j:j + N] * w_conv[j] for j in range(CONV_K))
    return b_gate * y


def fourier_mix(u):
    B, N, _ = u.shape
    ug = u.astype(jnp.float32).reshape(B, N, FOURIER_GROUPS, FOURIER_GROUP_CH)
    f = jnp.fft.fft2(ug, axes=(1, 3), norm="ortho").real
    return f.reshape(B, N, FOURIER_WIDTH).astype(u.dtype)


def mix_stream(proj, k_read, v_read, rope, prm):
    q = mla_q(proj[..., KV_END:Q_END], prm["g_q"], prm["w_uq"], rope)
    y_attn = attention(q, k_read, v_read) @ prm["w_mla_out"]
    y_conv = short_conv(proj[..., Q_END:CB_END], proj[..., CB_END:CC_END],
                        proj[..., CC_END:CX_END], prm["conv_w"]) @ prm["w_conv_out"]
    y_four = fourier_mix(proj[..., CX_END:FOUR_END]) @ prm["w_four_out"]
    gates = jax.nn.sigmoid(proj[..., FOUR_END:] + prm["b_gate"])
    g_a, g_c, g_f = jnp.split(gates, N_BRANCHES, axis=-1)
    return (g_a * y_attn + g_c * y_conv + g_f * y_four) @ prm["w_out"]


def moe(h, prm):
    B, N, D = h.shape
    t = h.reshape(B * N, D)
    scores = jax.nn.sigmoid((t @ prm["w_router"]).astype(jnp.float32))
    _, idx = lax.top_k(scores + prm["b_router"].astype(jnp.float32), TOP_K)
    sel = jnp.take_along_axis(scores, idx, axis=-1)
    w = sel / jnp.sum(sel, axis=-1, keepdims=True) * ROUTED_SCALE
    comb = jnp.sum(jax.nn.one_hot(idx, N_EXPERTS, dtype=jnp.float32) * w[..., None], axis=1)
    gate = jnp.einsum("td,edf->tef", t, prm["w_gate_e"])
    up = jnp.einsum("td,edf->tef", t, prm["w_up_e"])
    act = jax.nn.silu(gate) * up * comb[..., None].astype(t.dtype)
    routed = jnp.einsum("tef,efd->td", act, prm["w_down_e"])
    shared = (jax.nn.silu(t @ prm["w_gate_s"]) * (t @ prm["w_up_s"])) @ prm["w_down_s"]
    return (routed + shared).reshape(B, N, D)


def layer(x, ctx, ada_x, ada_c, prm, rope, ctx_out):
    sh1x, sc1x, g1x, sh2x, sc2x, g2x = jnp.split(ada_x[:, None, :], 6, axis=-1)
    sh1c, sc1c, g1c, sh2c, sc2c, g2c = jnp.split(ada_c, 6, axis=-1)
    hx = modulate(rms_norm(x, prm["g_pre_mix"]), sh1x, sc1x)
    hc = modulate(rms_norm(ctx, prm["g_pre_mix"]), sh1c, sc1c)
    px = hx @ prm["w_in"]
    if ctx_out:
        pc = hc @ prm["w_in"]
        pc_kv = pc[..., :KV_END]
    else:
        pc_kv = hc @ prm["w_in"][:, :KV_END]
    kc, vc = mla_kv(pc_kv, prm["g_kv"], prm["w_ukv"], None)
    kx, vx = mla_kv(px[..., :KV_END], prm["g_kv"], prm["w_ukv"], rope)
    k_all = jnp.concatenate([kc, kx], axis=1)
    v_all = jnp.concatenate([vc, vx], axis=1)
    yx = mix_stream(px, k_all, v_all, rope, prm)
    x = x + g1x * rms_norm(yx, prm["g_post_mix"])
    hx2 = modulate(rms_norm(x, prm["g_pre_ffn"]), sh2x, sc2x)
    x = x + g2x * rms_norm(moe(hx2, prm), prm["g_post_ffn"])
    if ctx_out:
        yc = mix_stream(pc, kc, vc, None, prm)
        ctx = ctx + g1c * rms_norm(yc, prm["g_post_mix"])
        hc2 = modulate(rms_norm(ctx, prm["g_pre_ffn"]), sh2c, sc2c)
        ctx = ctx + g2c * rms_norm(moe(hc2, prm), prm["g_post_ffn"])
    return x, ctx


def setup_inputs(seed: int = 0) -> dict:
    key = jax.random.key(seed)
    ks = jax.random.split(key, 29)
    f32 = jnp.float32

    def nrm(k, shape, scale):
        return jax.random.normal(k, shape, f32) * scale

    D, L = D_MODEL, DEPTH
    return {
        "x": nrm(ks[0], (BATCH, SEQ, D), 1.0),
        "c": nrm(ks[1], (BATCH, D), 1.0),
        "ctx": nrm(ks[2], (BATCH, CTX_LEN, D), 1.0),
        "c_ctx": nrm(ks[3], (D,), 1.0),
        "w_ada": nrm(ks[4], (L, D, 6 * D), 0.5 * D ** -0.5),
        "b_ada": nrm(ks[5], (L, 6 * D), 0.02),
        "g_pre_mix": 1.0 + nrm(ks[6], (L, D), 0.02),
        "g_post_mix": 1.0 + nrm(ks[7], (L, D), 0.02),
        "g_pre_ffn": 1.0 + nrm(ks[8], (L, D), 0.02),
        "g_post_ffn": 1.0 + nrm(ks[9], (L, D), 0.02),
        "w_in": nrm(ks[10], (L, D, PROJ_WIDTH), D ** -0.5),
        "b_gate": nrm(ks[11], (L, N_BRANCHES * D), 0.02),
        "g_q": 1.0 + nrm(ks[12], (L, Q_RANK), 0.02),
        "w_uq": nrm(ks[13], (L, Q_RANK, N_HEADS * QK_DIM), Q_RANK ** -0.5),
        "g_kv": 1.0 + nrm(ks[14], (L, KV_RANK), 0.02),
        "w_ukv": nrm(ks[15], (L, KV_RANK, N_HEADS * (QK_NOPE + V_DIM)), KV_RANK ** -0.5),
        "w_mla_out": nrm(ks[16], (L, N_HEADS * V_DIM, D), (N_HEADS * V_DIM) ** -0.5),
        "conv_w": nrm(ks[17], (L, CONV_K, CONV_WIDTH), CONV_K ** -0.5),
        "w_conv_out": nrm(ks[18], (L, CONV_WIDTH, D), CONV_WIDTH ** -0.5),
        "w_four_out": nrm(ks[19], (L, FOURIER_WIDTH, D), FOURIER_WIDTH ** -0.5),
        "w_out": nrm(ks[20], (L, D, D), D ** -0.5),
        "w_router": nrm(ks[21], (L, D, N_EXPERTS), D ** -0.5),
        "b_router": nrm(ks[22], (L, N_EXPERTS), 0.01),
        "w_gate_e": nrm(ks[23], (L, N_EXPERTS, D, F_EXPERT), D ** -0.5),
        "w_up_e": nrm(ks[24], (L, N_EXPERTS, D, F_EXPERT), D ** -0.5),
        "w_down_e": nrm(ks[25], (L, N_EXPERTS, F_EXPERT, D), F_EXPERT ** -0.5),
        "w_gate_s": nrm(ks[26], (L, D, F_SHARED), D ** -0.5),
        "w_up_s": nrm(ks[27], (L, D, F_SHARED), D ** -0.5),
        "w_down_s": nrm(ks[28], (L, F_SHARED, D), F_SHARED ** -0.5),
    }


def reference(x, c, ctx, c_ctx, w_ada, b_ada, g_pre_mix, g_post_mix, g_pre_ffn, g_post_ffn,
              w_in, b_gate, g_q, w_uq, g_kv, w_ukv, w_mla_out, conv_w, w_conv_out, w_four_out,
              w_out, w_router, b_router, w_gate_e, w_up_e, w_down_e, w_gate_s, w_up_s, w_down_s):
    rope = rope_tables(x.shape[1])
    for l in range(DEPTH):
        prm = {
            "g_pre_mix": g_pre_mix[l], "g_post_mix": g_post_mix[l],
            "g_pre_ffn": g_pre_ffn[l], "g_post_ffn": g_post_ffn[l],
            "w_in": w_in[l], "b_gate": b_gate[l],
            "g_q": g_q[l], "w_uq": w_uq[l], "g_kv": g_kv[l], "w_ukv": w_ukv[l],
            "w_mla_out": w_mla_out[l], "conv_w": conv_w[l], "w_conv_out": w_conv_out[l],
            "w_four_out": w_four_out[l], "w_out": w_out[l],
            "w_router": w_router[l], "b_router": b_router[l],
            "w_gate_e": w_gate_e[l], "w_up_e": w_up_e[l], "w_down_e": w_down_e[l],
            "w_gate_s": w_gate_s[l], "w_up_s": w_up_s[l], "w_down_s": w_down_s[l],
        }
        ada_x = jax.nn.silu(c) @ w_ada[l] + b_ada[l]
        ada_c = jax.nn.silu(c_ctx) @ w_ada[l] + b_ada[l]
        x, ctx = layer(x, ctx, ada_x, ada_c, prm, rope, l < DEPTH - 1)
    return x
```

```python
import functools

import numpy as np
import jax
import jax.numpy as jnp
from jax import lax
from jax.experimental import pallas as pl
from jax.experimental.pallas import tpu as pltpu

N_HEADS = 8
QK_NOPE = 64
QK_ROPE = 32
V_DIM = 64
GRID_W = 64
ROPE_BASE = 10000.0
FOURIER_GROUPS = 4
TOP_K = 4
ROUTED_SCALE = 2.5
N_BRANCHES = 3
EPS = 1e-6

LANE = 128
HEAD_PAD = LANE
VMEM_LIMIT = 56 * 1024 * 1024

F32 = jnp.float32
BF16 = jnp.bfloat16


def _rms(x, g):
    return x * lax.rsqrt(jnp.mean(x * x, axis=-1, keepdims=True) + EPS) * g


def _sigmoid(x):
    return 1.0 / (1.0 + jnp.exp(-x))


def _dot(a, b):
    return jnp.dot(a, b, preferred_element_type=F32)


def _resident(shape):
    nd = len(shape)
    return pl.BlockSpec(shape, lambda *_: (0,) * nd, pipeline_mode=pl.Buffered(1))


def _params(n_grid):
    return pltpu.CompilerParams(dimension_semantics=("arbitrary",) * n_grid,
                                vmem_limit_bytes=VMEM_LIMIT)


def _ada_kernel(c_ref, w_ref, b_ref, o_ref):
    c = c_ref[...]
    a = (c * _sigmoid(c)).astype(BF16)
    o_ref[0] = _dot(a, w_ref[0].astype(BF16)) + b_ref[0]


def _ada(c_all, w_ada, b_ada):
    n_layers, d, n_out = w_ada.shape
    rows = c_all.shape[0]
    tn = 1536
    return pl.pallas_call(
        _ada_kernel,
        grid=(n_layers, n_out // tn),
        in_specs=[
            pl.BlockSpec((rows, d), lambda l, j: (0, 0)),
            pl.BlockSpec((1, d, tn), lambda l, j: (l, 0, j)),
            pl.BlockSpec((1, 1, tn), lambda l, j: (l, 0, j)),
        ],
        out_specs=pl.BlockSpec((1, rows, tn), lambda l, j: (l, 0, j)),
        out_shape=jax.ShapeDtypeStruct((n_layers, rows, n_out), F32),
        compiler_params=_params(2),
        name="ada",
    )(c_all, w_ada, b_ada.reshape(n_layers, 1, n_out))


def _inproj_kernel(*refs, kv_only, kv_rank, q_rank, conv_w, four_w, d_model):
    if kv_only:
        (x_ref, mod_ref, gpre_ref, w1_ref, gkv_ref, wuk_ref, wuv_ref, cos_ref, sin_ref,
         k_ref, v_ref) = refs
    else:
        (x_ref, mod_ref, gpre_ref, w1_ref, gkv_ref, wuk_ref, wuv_ref, cos_ref, sin_ref,
         bg_ref, gq_ref, wuq_ref, wuqr_ref, dc_ref,
         k_ref, v_ref, q_ref, cb_ref, cc_ref, cu_ref, ab_ref, gate_ref) = refs

    x = x_ref[...]
    shift = mod_ref[0, 0:1, :]
    scale = mod_ref[0, 1:2, :]
    h = (_rms(x, gpre_ref[...]) * (1.0 + scale) + shift).astype(BF16)
    cos = cos_ref[...]
    sin = sin_ref[...]

    o_kpe = kv_rank
    o_rot = o_kpe + HEAD_PAD
    o_q = o_rot + HEAD_PAD
    p = _dot(h, w1_ref[:, 0:o_q])
    ckv = _rms(p[:, 0:kv_rank], gkv_ref[...]).astype(BF16)
    kpe = p[:, o_kpe:o_rot] * cos + p[:, o_rot:o_q] * sin
    k = _dot(ckv, wuk_ref[...]) + jnp.concatenate([kpe] * N_HEADS, axis=1)
    k_ref[...] = k.astype(k_ref.dtype)
    v_ref[...] = _dot(ckv, wuv_ref[...]).astype(v_ref.dtype)
    if kv_only:
        return

    o_cb = o_q + q_rank
    cq = _rms(_dot(h, w1_ref[:, o_q:o_cb]), gq_ref[...]).astype(BF16)
    cos_h = jnp.concatenate([cos] * N_HEADS, axis=1)
    sin_h = jnp.concatenate([sin] * N_HEADS, axis=1)
    q = _dot(cq, wuq_ref[...]) * cos_h + _dot(cq, wuqr_ref[...]) * sin_h
    q_ref[...] = q.astype(q_ref.dtype)

    o_cc = o_cb + conv_w
    o_cu = o_cc + conv_w
    o_four = o_cu + conv_w
    cb_ref[...] = _dot(h, w1_ref[:, o_cb:o_cc]).astype(cb_ref.dtype)
    cc_ref[...] = _dot(h, w1_ref[:, o_cc:o_cu]).astype(cc_ref.dtype)
    cu_ref[...] = _dot(h, w1_ref[:, o_cu:o_four]).astype(cu_ref.dtype)

    o_gate = o_four + four_w
    uf = _dot(h, w1_ref[:, o_four:o_gate]).astype(BF16)
    ab_ref[...] = _dot(uf, dc_ref[...]).astype(ab_ref.dtype)

    for j in range(N_BRANCHES):
        lo = o_gate + j * d_model
        z = _dot(h, w1_ref[:, lo:lo + d_model]) + bg_ref[:, j * d_model:(j + 1) * d_model]
        gate_ref[:, j * d_model:(j + 1) * d_model] = _sigmoid(z).astype(gate_ref.dtype)


def _inproj(xs, mod, seq_len, lw, tables, *, kv_only, tm):
    t, d = xs.shape
    nb = mod.shape[0]
    tiles_per_seq = seq_len // tm
    cos_t, sin_t = tables
    table_tiles = cos_t.shape[0] // tm
    kv_rank = lw["g_kv"].shape[1]
    q_rank = lw["g_q"].shape[1]
    conv_w = lw["conv_w"].shape[1]
    four_w = lw["dc"].shape[0]
    n_k = N_HEADS * HEAD_PAD
    n_v = N_HEADS * V_DIM

    def row(i):
        return (i, 0)

    def mod_map(i):
        return ((i // tiles_per_seq) % nb, 0, 0)

    def tab_map(i):
        return (i % table_tiles, 0)

    w1 = lw["w1"]
    if kv_only:
        w1 = w1[:, :kv_rank + 2 * HEAD_PAD]
    in_specs = [
        pl.BlockSpec((tm, d), row),
        pl.BlockSpec((1,) + mod.shape[1:], mod_map),
        _resident((1, d)),
        _resident(w1.shape),
        _resident((1, kv_rank)),
        _resident(lw["wuk"].shape),
        _resident(lw["wuv"].shape),
        pl.BlockSpec((tm, HEAD_PAD), tab_map),
        pl.BlockSpec((tm, HEAD_PAD), tab_map),
    ]
    args = [xs, mod, lw["g_pre_mix"], w1, lw["g_kv"], lw["wuk"], lw["wuv"], cos_t, sin_t]
    out_shape = [jax.ShapeDtypeStruct((t, n_k), BF16), jax.ShapeDtypeStruct((t, n_v), BF16)]
    out_specs = [pl.BlockSpec((tm, n_k), row), pl.BlockSpec((tm, n_v), row)]
    if not kv_only:
        in_specs += [
            _resident(lw["b_gate"].shape),
            _resident((1, q_rank)),
            _resident(lw["wuq"].shape),
            _resident(lw["wuq_rot"].shape),
            _resident(lw["dc"].shape),
        ]
        args += [lw["b_gate"], lw["g_q"], lw["wuq"], lw["wuq_rot"], lw["dc"]]
        widths = [n_k, conv_w, conv_w, conv_w, 2 * four_w, N_BRANCHES * d]
        out_shape += [jax.ShapeDtypeStruct((t, w), BF16) for w in widths]
        out_specs += [pl.BlockSpec((tm, w), row) for w in widths]
    outs = pl.pallas_call(
        functools.partial(_inproj_kernel, kv_only=kv_only, kv_rank=kv_rank, q_rank=q_rank,
                          conv_w=conv_w, four_w=four_w, d_model=d),
        grid=(t // tm,),
        in_specs=in_specs,
        out_specs=out_specs,
        out_shape=out_shape,
        compiler_params=_params(1),
        name="inproj_kv" if kv_only else "inproj",
    )(*args)
    names = ["k", "v", "q", "cb", "cc", "cu", "ab", "gate"]
    return dict(zip(names, outs))


def _attn_kernel(*refs, n_seg):
    q_ref = refs[0]
    o_ref = refs[-1]
    nt = (((1,), (1,)), ((), ()))
    outs = []
    for hh in range(2):
        qh = q_ref[:, hh * HEAD_PAD:(hh + 1) * HEAD_PAD]
        s = [lax.dot_general(qh, refs[1 + 2 * i][:, hh * HEAD_PAD:(hh + 1) * HEAD_PAD], nt,
                             preferred_element_type=F32) for i in range(n_seg)]
        m = functools.reduce(jnp.maximum, [jnp.max(si, axis=-1, keepdims=True) for si in s])
        p = [jnp.exp(si - m) for si in s]
        l = functools.reduce(jnp.add, [jnp.sum(pi, axis=-1, keepdims=True) for pi in p])
        acc = functools.reduce(jnp.add, [
            _dot(p[i].astype(BF16), refs[2 + 2 * i][:, hh * V_DIM:(hh + 1) * V_DIM])
            for i in range(n_seg)])
        outs.append(acc / l)
    o_ref[...] = jnp.concatenate(outs, axis=1).astype(o_ref.dtype)


def _attention(q, segs, batch, seq_q, *, tq):
    t = q.shape[0]
    qt = seq_q // tq
    in_specs = [pl.BlockSpec((tq, 2 * HEAD_PAD), lambda b, hp, j: (b * qt + j, hp))]
    args = [q]
    for k, v, m in segs:
        in_specs.append(pl.BlockSpec((m, 2 * HEAD_PAD), lambda b, hp, j: (b, hp)))
        in_specs.append(pl.BlockSpec((m, 2 * V_DIM), lambda b, hp, j: (b, hp)))
        args += [k, v]
    return pl.pallas_call(
        functools.partial(_attn_kernel, n_seg=len(segs)),
        grid=(batch, N_HEADS // 2, qt),
        in_specs=in_specs,
        out_specs=pl.BlockSpec((tq, 2 * V_DIM), lambda b, hp, j: (b * qt + j, hp)),
        out_shape=jax.ShapeDtypeStruct((t, N_HEADS * V_DIM), BF16),
        compiler_params=_params(3),
        name="attention",
    )(*args)


def _four_kernel(cs_ref, ab_ref, o_ref, *, n, fw):
    o = _dot(cs_ref[:, 0:n], ab_ref[:, 0:fw]) + _dot(cs_ref[:, n:2 * n], ab_ref[:, fw:2 * fw])
    o_ref[...] = o.astype(o_ref.dtype)


def _fourier(ab, cs, batch, seq_len, *, tn):
    t, fw2 = ab.shape
    fw = fw2 // 2
    nt = seq_len // tn
    return pl.pallas_call(
        functools.partial(_four_kernel, n=seq_len, fw=fw),
        grid=(batch, nt),
        in_specs=[
            pl.BlockSpec((tn, 2 * seq_len), lambda b, j: (j, 0)),
            pl.BlockSpec((seq_len, fw2), lambda b, j: (b, 0)),
        ],
        out_specs=pl.BlockSpec((tn, fw), lambda b, j: (b * nt + j, 0)),
        out_shape=jax.ShapeDtypeStruct((t, fw), BF16),
        compiler_params=_params(2),
        name="fourier",
    )(cs, ab)


def _merge_kernel(x_ref, mod_ref, o_ref, cb_ref, cc_ref, cu_ref, ccp_ref, cup_ref, ccn_ref,
                  cun_ref, f_ref, gate_ref, convw_ref, wmo_ref, wco_ref, wfo_ref, wout_ref,
                  gpost_ref, out_ref, pad_ref, *, tiles_per_seq, tm, d_model):
    i = pl.program_id(0)
    pos = i % tiles_per_seq
    has_prev = (pos > 0).astype(F32)
    has_next = (pos < tiles_per_seq - 1).astype(F32)
    pad_ref[0:8, :] = ccp_ref[...].astype(F32) * cup_ref[...].astype(F32) * has_prev
    pad_ref[8:8 + tm, :] = cc_ref[...].astype(F32) * cu_ref[...].astype(F32)
    pad_ref[8 + tm:16 + tm, :] = ccn_ref[...].astype(F32) * cun_ref[...].astype(F32) * has_next
    conv = (pad_ref[7:7 + tm, :] * convw_ref[0:1, :] + pad_ref[8:8 + tm, :] * convw_ref[1:2, :]
            + pad_ref[9:9 + tm, :] * convw_ref[2:3, :])
    y_conv = _dot((cb_ref[...].astype(F32) * conv).astype(BF16), wco_ref[...])
    y_attn = _dot(o_ref[...], wmo_ref[...])
    y_four = _dot(f_ref[...], wfo_ref[...])
    d = d_model
    merged = (gate_ref[:, 0:d].astype(F32) * y_attn + gate_ref[:, d:2 * d].astype(F32) * y_conv
              + gate_ref[:, 2 * d:3 * d].astype(F32) * y_four)
    y = _dot(merged.astype(BF16), wout_ref[...])
    g1 = mod_ref[0, 2:3, :]
    out_ref[...] = x_ref[...] + g1 * _rms(y, gpost_ref[...])


def _merge(xs, mod, seq_len, pr, o, four, lw, *, tm):
    t, d = xs.shape
    nb = mod.shape[0]
    tiles_per_seq = seq_len // tm
    cw = lw["conv_w"].shape[1]
    fw = four.shape[1]
    hb = tm // 8
    last_hb = t // 8 - 1

    def row(i):
        return (i, 0)

    def prev_map(i):
        return (jnp.maximum(i * hb - 1, 0), 0)

    def next_map(i):
        return (jnp.minimum((i + 1) * hb, last_hb), 0)

    in_specs = [
        pl.BlockSpec((tm, d), row),
        pl.BlockSpec((1,) + mod.shape[1:], lambda i: ((i // tiles_per_seq) % nb, 0, 0)),
        pl.BlockSpec((tm, o.shape[1]), row),
        pl.BlockSpec((tm, cw), row),
        pl.BlockSpec((tm, cw), row),
        pl.BlockSpec((tm, cw), row),
        pl.BlockSpec((8, cw), prev_map),
        pl.BlockSpec((8, cw), prev_map),
        pl.BlockSpec((8, cw), next_map),
        pl.BlockSpec((8, cw), next_map),
        pl.BlockSpec((tm, fw), row),
        pl.BlockSpec((tm, N_BRANCHES * d), row),
        _resident(lw["conv_w"].shape),
        _resident(lw["w_mla_out"].shape),
        _resident(lw["w_conv_out"].shape),
        _resident(lw["w_four_out"].shape),
        _resident(lw["w_out"].shape),
        _resident((1, d)),
    ]
    return pl.pallas_call(
        functools.partial(_merge_kernel, tiles_per_seq=tiles_per_seq, tm=tm, d_model=d),
        grid=(t // tm,),
        in_specs=in_specs,
        out_specs=pl.BlockSpec((tm, d), row),
        out_shape=jax.ShapeDtypeStruct((t, d), F32),
        scratch_shapes=[pltpu.VMEM((tm + 16, cw), F32)],
        compiler_params=_params(1),
        name="merge",
    )(xs, mod, o, pr["cb"], pr["cc"], pr["cu"], pr["cc"], pr["cu"], pr["cc"], pr["cu"], four,
      pr["gate"], lw["conv_w"], lw["w_mla_out"], lw["w_conv_out"], lw["w_four_out"], lw["w_out"],
      lw["g_post_mix"])


def _moe_kernel(x_ref, mod_ref, gpre_ref, gpost_ref, wr_ref, br_ref, wgs_ref, wus_ref, wds_ref,
                wge_ref, wue_ref, wde_ref, out_ref, t_ref, comb_ref, acc_ref, *, n_experts):
    e = pl.program_id(1)

    @pl.when(e == 0)
    def _():
        shift = mod_ref[0, 3:4, :]
        scale = mod_ref[0, 4:5, :]
        t = _rms(x_ref[...], gpre_ref[...]) * (1.0 + scale) + shift
        t_hi = t.astype(BF16)
        t_lo = (t - t_hi.astype(F32)).astype(BF16)
        t_ref[...] = t_hi
        hh = _dot(t_hi, wr_ref[...])
        logits = hh[:, 0:n_experts] + hh[:, n_experts:2 * n_experts] + _dot(t_lo, wr_ref[:, 0:n_experts])
        scores = _sigmoid(logits)
        work = scores + br_ref[...]
        lane = lax.broadcasted_iota(jnp.int32, scores.shape, 1)
        comb = jnp.zeros_like(scores)
        for _ in range(TOP_K):
            best = jnp.max(work, axis=-1, keepdims=True)
            first = jnp.min(jnp.where(work == best, lane, n_experts), axis=-1, keepdims=True)
            hit = lane == first
            comb = jnp.where(hit, scores, comb)
            work = jnp.where(hit, -jnp.inf, work)
        comb_ref[...] = comb / jnp.sum(comb, axis=-1, keepdims=True) * ROUTED_SCALE
        gate = _dot(t_hi, wgs_ref[...])
        act = (gate * _sigmoid(gate) * _dot(t_hi, wus_ref[...])).astype(BF16)
        acc_ref[...] = _dot(act, wds_ref[...])

    t_hi = t_ref[...]
    gate = _dot(t_hi, wge_ref[0, 0].astype(BF16))
    up = _dot(t_hi, wue_ref[0, 0].astype(BF16))
    lane = lax.broadcasted_iota(jnp.int32, comb_ref.shape, 1)
    w_e = jnp.sum(jnp.where(lane == e, comb_ref[...], 0.0), axis=-1, keepdims=True)
    act = (gate * _sigmoid(gate) * up * w_e).astype(BF16)
    acc_ref[...] += _dot(act, wde_ref[0, 0].astype(BF16))

    @pl.when(e == n_experts - 1)
    def _():
        g2 = mod_ref[0, 5:6, :]
        out_ref[...] = x_ref[...] + g2 * _rms(acc_ref[...], gpost_ref[...])


def _moe(xs, mod, seq_len, lw, w_gate_e, w_up_e, w_down_e, layer, *, tm):
    t, d = xs.shape
    nb = mod.shape[0]
    tiles_per_seq = max(seq_len // tm, 1)
    n_experts, _, f = w_gate_e.shape[1:]
    in_specs = [
        pl.BlockSpec((tm, d), lambda i, e: (i, 0)),
        pl.BlockSpec((1,) + mod.shape[1:], lambda i, e: ((i // tiles_per_seq) % nb, 0, 0)),
        _resident((1, d)),
        _resident((1, d)),
        _resident(lw["w_router"].shape),
        _resident((1, n_experts)),
        _resident(lw["w_gate_s"].shape),
        _resident(lw["w_up_s"].shape),
        _resident(lw["w_down_s"].shape),
        pl.BlockSpec((1, 1, d, f), lambda i, e: (layer, e, 0, 0)),
        pl.BlockSpec((1, 1, d, f), lambda i, e: (layer, e, 0, 0)),
        pl.BlockSpec((1, 1, f, d), lambda i, e: (layer, e, 0, 0)),
    ]
    return pl.pallas_call(
        functools.partial(_moe_kernel, n_experts=n_experts),
        grid=(t // tm, n_experts),
        in_specs=in_specs,
        out_specs=pl.BlockSpec((tm, d), lambda i, e: (i, 0)),
        out_shape=jax.ShapeDtypeStruct((t, d), F32),
        scratch_shapes=[pltpu.VMEM((tm, d), BF16), pltpu.VMEM((tm, n_experts), F32),
                        pltpu.VMEM((tm, d), F32)],
        compiler_params=_params(2),
        name="moe",
    )(xs, mod, lw["g_pre_ffn"], lw["g_post_ffn"], lw["w_router"], lw["b_router"], lw["w_gate_s"],
      lw["w_up_s"], lw["w_down_s"], w_gate_e, w_up_e, w_down_e)


def _rope_tables(n):
    rows = n // GRID_W
    r, col = jnp.meshgrid(jnp.arange(rows), jnp.arange(GRID_W), indexing="ij")
    r = r.reshape(-1).astype(F32)
    col = col.reshape(-1).astype(F32)
    pairs = QK_ROPE // 4
    inv = ROPE_BASE ** (-jnp.arange(pairs, dtype=F32) / pairs)
    ang = jnp.concatenate([r[:, None] * inv, col[:, None] * inv], axis=-1)
    cos, sin = jnp.cos(ang), jnp.sin(ang)
    pad = HEAD_PAD - QK_NOPE - QK_ROPE
    cos_t = jnp.concatenate([jnp.ones((n, QK_NOPE), F32), cos, cos, jnp.zeros((n, pad), F32)], axis=1)
    sin_t = jnp.concatenate([jnp.zeros((n, QK_NOPE), F32), sin, sin, jnp.zeros((n, pad), F32)], axis=1)
    return cos_t, sin_t


def _identity_tables(n):
    pad = HEAD_PAD - QK_NOPE - QK_ROPE
    cos_t = jnp.concatenate([jnp.ones((n, QK_NOPE + QK_ROPE), F32), jnp.zeros((n, pad), F32)], axis=1)
    return cos_t, jnp.zeros((n, HEAD_PAD), F32)


def _position_dft(n):
    idx = (jnp.arange(n, dtype=jnp.int32)[:, None] * jnp.arange(n, dtype=jnp.int32)[None, :]) % n
    ang = idx.astype(F32) * (2.0 * np.pi / n)
    norm = 1.0 / np.sqrt(n)
    return jnp.concatenate([jnp.cos(ang) * norm, jnp.sin(ang) * (-norm)], axis=1).astype(BF16)


def _channel_dft(width):
    gc = width // FOURIER_GROUPS
    idx = (jnp.arange(gc, dtype=jnp.int32)[:, None] * jnp.arange(gc, dtype=jnp.int32)[None, :]) % gc
    ang = idx.astype(F32) * (2.0 * np.pi / gc)
    eye = jnp.eye(FOURIER_GROUPS, dtype=F32)
    norm = 1.0 / np.sqrt(gc)
    return jnp.concatenate([jnp.kron(eye, jnp.cos(ang) * norm), jnp.kron(eye, jnp.sin(ang) * norm)],
                           axis=1).astype(BF16)


def _pad_cols(w, width):
    return jnp.pad(w, ((0, 0), (0, width - w.shape[1])))


def _rot_cols(w):
    half = w.shape[-1] // 2
    return jnp.concatenate([-w[..., half:], w[..., :half]], axis=-1)


def _layer_weights(l, g_pre_mix, g_post_mix, g_pre_ffn, g_post_ffn, w_in, b_gate, g_q, w_uq, g_kv,
                   w_ukv, w_mla_out, conv_w, w_conv_out, w_four_out, w_out, w_router, b_router,
                   w_gate_s, w_up_s, w_down_s, dc):
    d = w_in.shape[1]
    kv_rank = g_kv.shape[1]
    q_rank = g_q.shape[1]
    w = w_in[l]
    kv_end = kv_rank + QK_ROPE
    zeros_nope = jnp.zeros((d, QK_NOPE), F32)
    kpe = w[:, kv_rank:kv_end]
    w1 = jnp.concatenate([
        w[:, :kv_rank],
        _pad_cols(jnp.concatenate([zeros_nope, kpe], axis=1), HEAD_PAD),
        _pad_cols(jnp.concatenate([zeros_nope, _rot_cols(kpe)], axis=1), HEAD_PAD),
        w[:, kv_end:],
    ], axis=1).astype(BF16)

    qk_dim = QK_NOPE + QK_ROPE
    pad = HEAD_PAD - qk_dim
    uq = w_uq[l].reshape(q_rank, N_HEADS, qk_dim) * (qk_dim ** -0.5)
    zq = jnp.zeros((q_rank, N_HEADS, pad), F32)
    wuq = jnp.concatenate([uq, zq], axis=-1).reshape(q_rank, N_HEADS * HEAD_PAD).astype(BF16)
    wuq_rot = jnp.concatenate([jnp.zeros((q_rank, N_HEADS, QK_NOPE), F32), _rot_cols(uq[..., QK_NOPE:]), zq],
                              axis=-1).reshape(q_rank, N_HEADS * HEAD_PAD).astype(BF16)
    ukv = w_ukv[l].reshape(kv_rank, N_HEADS, QK_NOPE + V_DIM)
    wuk = jnp.concatenate([ukv[..., :QK_NOPE], jnp.zeros((kv_rank, N_HEADS, HEAD_PAD - QK_NOPE), F32)],
                          axis=-1).reshape(kv_rank, N_HEADS * HEAD_PAD).astype(BF16)
    wuv = ukv[..., QK_NOPE:].reshape(kv_rank, N_HEADS * V_DIM).astype(BF16)

    wr = w_router[l]
    wr_hi = wr.astype(BF16)
    wr_lo = (wr - wr_hi.astype(F32)).astype(BF16)
    return {
        "g_pre_mix": g_pre_mix[l][None], "g_post_mix": g_post_mix[l][None],
        "g_pre_ffn": g_pre_ffn[l][None], "g_post_ffn": g_post_ffn[l][None],
        "w1": w1, "b_gate": b_gate[l][None], "g_q": g_q[l][None], "g_kv": g_kv[l][None],
        "wuq": wuq, "wuq_rot": wuq_rot, "wuk": wuk, "wuv": wuv, "dc": dc,
        "conv_w": conv_w[l],
        "w_mla_out": w_mla_out[l].astype(BF16), "w_conv_out": w_conv_out[l].astype(BF16),
        "w_four_out": w_four_out[l].astype(BF16), "w_out": w_out[l].astype(BF16),
        "w_router": jnp.concatenate([wr_hi, wr_lo], axis=1), "b_router": b_router[l][None],
        "w_gate_s": w_gate_s[l].astype(BF16), "w_up_s": w_up_s[l].astype(BF16),
        "w_down_s": w_down_s[l].astype(BF16),
    }


def _tile(n, pref):
    return pref if n % pref == 0 else n


def kernel(x, c, ctx, c_ctx, w_ada, b_ada, g_pre_mix, g_post_mix, g_pre_ffn, g_post_ffn, w_in, b_gate,
           g_q, w_uq, g_kv, w_ukv, w_mla_out, conv_w, w_conv_out, w_four_out, w_out, w_router, b_router,
           w_gate_e, w_up_e, w_down_e, w_gate_s, w_up_s, w_down_s):
    batch, seq, d = x.shape
    n_ctx = ctx.shape[1]
    n_layers = w_in.shape[0]
    xs = x.reshape(batch * seq, d)
    cs = ctx.reshape(batch * n_ctx, d)

    mod_rows = 16
    c_all = jnp.concatenate([c, c_ctx[None], jnp.zeros((mod_rows - batch - 1, d), F32)], axis=0)
    ada = _ada(c_all, w_ada, b_ada)

    tab_x = _rope_tables(seq)
    tm_c = _tile(n_ctx, 256)
    tab_c = _identity_tables(tm_c)
    cs_x = _position_dft(seq)
    cs_c = _position_dft(n_ctx)
    dc = _channel_dft(w_four_out.shape[1])

    tm_x = _tile(seq, 512)
    tm_moe_x = _tile(seq, 1024)
    tm_moe_c = _tile(batch * n_ctx, 1024)

    for l in range(n_layers):
        last = l == n_layers - 1
        lw = _layer_weights(l, g_pre_mix, g_post_mix, g_pre_ffn, g_post_ffn, w_in, b_gate, g_q, w_uq,
                            g_kv, w_ukv, w_mla_out, conv_w, w_conv_out, w_four_out, w_out, w_router,
                            b_router, w_gate_s, w_up_s, w_down_s, dc)
        mods = ada[l].reshape(mod_rows, 6, d)
        mod_x = mods[:batch]
        mod_c = mods[batch:batch + 1]

        pc = _inproj(cs, mod_c, n_ctx, lw, tab_c, kv_only=last, tm=tm_c)
        px = _inproj(xs, mod_x, seq, lw, tab_x, kv_only=False, tm=tm_x)
        o_x = _attention(px["q"], [(pc["k"], pc["v"], n_ctx), (px["k"], px["v"], seq)], batch, seq, tq=tm_x)
        f_x = _fourier(px["ab"], cs_x, batch, seq, tn=tm_x)
        x1 = _merge(xs, mod_x, seq, px, o_x, f_x, lw, tm=tm_x)
        xs = _moe(x1, mod_x, seq, lw, w_gate_e, w_up_e, w_down_e, l, tm=tm_moe_x)
        if not last:
            o_c = _attention(pc["q"], [(pc["k"], pc["v"], n_ctx)], batch, n_ctx, tq=tm_c)
            f_c = _fourier(pc["ab"], cs_c, batch, n_ctx, tn=tm_c)
            c1 = _merge(cs, mod_c, n_ctx, pc, o_c, f_c, lw, tm=tm_c)
            cs = _moe(c1, mod_c, batch * n_ctx, lw, w_gate_e, w_up_e, w_down_e, l, tm=tm_moe_c)
    return xs.reshape(batch, seq, d)
```

```python
import functools

import numpy as np
import jax
import jax.numpy as jnp
from jax import lax
from jax.experimental import pallas as pl
from jax.experimental.pallas import tpu as pltpu
from jax.experimental.pallas import tpu_sc as plsc

N_HEADS = 8
QK_NOPE = 64
QK_ROPE = 32
V_DIM = 64
GRID_W = 64
ROPE_BASE = 10000.0
FOURIER_GROUPS = 4
TOP_K = 4
ROUTED_SCALE = 2.5
N_BRANCHES = 3
EPS = 1e-6

LANE = 128
HEAD_PAD = LANE
VMEM_LIMIT = 56 * 1024 * 1024
PACK_W = 256
SC_WINDOW = 128

F32 = jnp.float32
BF16 = jnp.bfloat16


def _rms(x, g):
    return x * lax.rsqrt(jnp.mean(x * x, axis=-1, keepdims=True) + EPS) * g


def _sigmoid(x):
    return 1.0 / (1.0 + jnp.exp(-x))


def _dot(a, b):
    return jnp.dot(a, b, preferred_element_type=F32)


def _resident(shape):
    nd = len(shape)
    return pl.BlockSpec(shape, lambda *_: (0,) * nd, pipeline_mode=pl.Buffered(1))


def _params(n_grid):
    return pltpu.CompilerParams(dimension_semantics=("arbitrary",) * n_grid,
                                vmem_limit_bytes=VMEM_LIMIT)


def _ada_kernel(c_ref, w_ref, b_ref, o_ref):
    c = c_ref[...]
    a = (c * _sigmoid(c)).astype(BF16)
    o_ref[0] = _dot(a, w_ref[0].astype(BF16)) + b_ref[0]


def _ada(c_all, w_ada, b_ada):
    n_layers, d, n_out = w_ada.shape
    rows = c_all.shape[0]
    tn = 1536
    return pl.pallas_call(
        _ada_kernel,
        grid=(n_layers, n_out // tn),
        in_specs=[
            pl.BlockSpec((rows, d), lambda l, j: (0, 0)),
            pl.BlockSpec((1, d, tn), lambda l, j: (l, 0, j)),
            pl.BlockSpec((1, 1, tn), lambda l, j: (l, 0, j)),
        ],
        out_specs=pl.BlockSpec((1, rows, tn), lambda l, j: (l, 0, j)),
        out_shape=jax.ShapeDtypeStruct((n_layers, rows, n_out), F32),
        compiler_params=_params(2),
        name="ada",
    )(c_all, w_ada, b_ada.reshape(n_layers, 1, n_out))


def _inproj_kernel(*refs, kv_only, kv_rank, q_rank, conv_w, four_w, d_model):
    if kv_only:
        (x_ref, mod_ref, gpre_ref, w1_ref, gkv_ref, wuk_ref, wuv_ref, cos_ref, sin_ref,
         k_ref, v_ref) = refs
    else:
        (x_ref, mod_ref, gpre_ref, w1_ref, gkv_ref, wuk_ref, wuv_ref, cos_ref, sin_ref,
         bg_ref, gq_ref, wuq_ref, wuqr_ref, dc_ref,
         k_ref, v_ref, q_ref, cb_ref, cc_ref, cu_ref, ab_ref, gate_ref) = refs

    x = x_ref[...]
    shift = mod_ref[0, 0:1, :]
    scale = mod_ref[0, 1:2, :]
    h = (_rms(x, gpre_ref[...]) * (1.0 + scale) + shift).astype(BF16)
    cos = cos_ref[...]
    sin = sin_ref[...]

    o_kpe = kv_rank
    o_rot = o_kpe + HEAD_PAD
    o_q = o_rot + HEAD_PAD
    p = _dot(h, w1_ref[:, 0:o_q])
    ckv = _rms(p[:, 0:kv_rank], gkv_ref[...]).astype(BF16)
    kpe = p[:, o_kpe:o_rot] * cos + p[:, o_rot:o_q] * sin
    k = _dot(ckv, wuk_ref[...]) + jnp.concatenate([kpe] * N_HEADS, axis=1)
    k_ref[...] = k.astype(k_ref.dtype)
    v_ref[...] = _dot(ckv, wuv_ref[...]).astype(v_ref.dtype)
    if kv_only:
        return

    o_cb = o_q + q_rank
    cq = _rms(_dot(h, w1_ref[:, o_q:o_cb]), gq_ref[...]).astype(BF16)
    cos_h = jnp.concatenate([cos] * N_HEADS, axis=1)
    sin_h = jnp.concatenate([sin] * N_HEADS, axis=1)
    q = _dot(cq, wuq_ref[...]) * cos_h + _dot(cq, wuqr_ref[...]) * sin_h
    q_ref[...] = q.astype(q_ref.dtype)

    o_cc = o_cb + conv_w
    o_cu = o_cc + conv_w
    o_four = o_cu + conv_w
    cb_ref[...] = _dot(h, w1_ref[:, o_cb:o_cc]).astype(cb_ref.dtype)
    cc_ref[...] = _dot(h, w1_ref[:, o_cc:o_cu]).astype(cc_ref.dtype)
    cu_ref[...] = _dot(h, w1_ref[:, o_cu:o_four]).astype(cu_ref.dtype)

    o_gate = o_four + four_w
    uf = _dot(h, w1_ref[:, o_four:o_gate]).astype(BF16)
    ab_ref[...] = _dot(uf, dc_ref[...]).astype(ab_ref.dtype)

    for j in range(N_BRANCHES):
        lo = o_gate + j * d_model
        z = _dot(h, w1_ref[:, lo:lo + d_model]) + bg_ref[:, j * d_model:(j + 1) * d_model]
        gate_ref[:, j * d_model:(j + 1) * d_model] = _sigmoid(z).astype(gate_ref.dtype)


def _inproj(xs, mod, seq_len, lw, tables, *, kv_only, tm):
    t, d = xs.shape
    nb = mod.shape[0]
    tiles_per_seq = seq_len // tm
    cos_t, sin_t = tables
    table_tiles = cos_t.shape[0] // tm
    kv_rank = lw["g_kv"].shape[1]
    q_rank = lw["g_q"].shape[1]
    conv_w = lw["conv_w"].shape[1]
    four_w = lw["dc"].shape[0]
    n_k = N_HEADS * HEAD_PAD
    n_v = N_HEADS * V_DIM

    def row(i):
        return (i, 0)

    def mod_map(i):
        return ((i // tiles_per_seq) % nb, 0, 0)

    def tab_map(i):
        return (i % table_tiles, 0)

    w1 = lw["w1"]
    if kv_only:
        w1 = w1[:, :kv_rank + 2 * HEAD_PAD]
    in_specs = [
        pl.BlockSpec((tm, d), row),
        pl.BlockSpec((1,) + mod.shape[1:], mod_map),
        _resident((1, d)),
        _resident(w1.shape),
        _resident((1, kv_rank)),
        _resident(lw["wuk"].shape),
        _resident(lw["wuv"].shape),
        pl.BlockSpec((tm, HEAD_PAD), tab_map),
        pl.BlockSpec((tm, HEAD_PAD), tab_map),
    ]
    args = [xs, mod, lw["g_pre_mix"], w1, lw["g_kv"], lw["wuk"], lw["wuv"], cos_t, sin_t]
    out_shape = [jax.ShapeDtypeStruct((t, n_k), BF16), jax.ShapeDtypeStruct((t, n_v), BF16)]
    out_specs = [pl.BlockSpec((tm, n_k), row), pl.BlockSpec((tm, n_v), row)]
    if not kv_only:
        in_specs += [
            _resident(lw["b_gate"].shape),
            _resident((1, q_rank)),
            _resident(lw["wuq"].shape),
            _resident(lw["wuq_rot"].shape),
            _resident(lw["dc"].shape),
        ]
        args += [lw["b_gate"], lw["g_q"], lw["wuq"], lw["wuq_rot"], lw["dc"]]
        widths = [n_k, conv_w, conv_w, conv_w, 2 * four_w, N_BRANCHES * d]
        out_shape += [jax.ShapeDtypeStruct((t, w), BF16) for w in widths]
        out_specs += [pl.BlockSpec((tm, w), row) for w in widths]
    outs = pl.pallas_call(
        functools.partial(_inproj_kernel, kv_only=kv_only, kv_rank=kv_rank, q_rank=q_rank,
                          conv_w=conv_w, four_w=four_w, d_model=d),
        grid=(t // tm,),
        in_specs=in_specs,
        out_specs=out_specs,
        out_shape=out_shape,
        compiler_params=_params(1),
        name="inproj_kv" if kv_only else "inproj",
    )(*args)
    names = ["k", "v", "q", "cb", "cc", "cu", "ab", "gate"]
    return dict(zip(names, outs))


def _attn_kernel(*refs, n_seg):
    q_ref = refs[0]
    o_ref = refs[-1]
    nt = (((1,), (1,)), ((), ()))
    outs = []
    for hh in range(2):
        qh = q_ref[:, hh * HEAD_PAD:(hh + 1) * HEAD_PAD]
        s = [lax.dot_general(qh, refs[1 + 2 * i][:, hh * HEAD_PAD:(hh + 1) * HEAD_PAD], nt,
                             preferred_element_type=F32) for i in range(n_seg)]
        m = functools.reduce(jnp.maximum, [jnp.max(si, axis=-1, keepdims=True) for si in s])
        p = [jnp.exp(si - m) for si in s]
        l = functools.reduce(jnp.add, [jnp.sum(pi, axis=-1, keepdims=True) for pi in p])
        acc = functools.reduce(jnp.add, [
            _dot(p[i].astype(BF16), refs[2 + 2 * i][:, hh * V_DIM:(hh + 1) * V_DIM])
            for i in range(n_seg)])
        outs.append(acc / l)
    o_ref[...] = jnp.concatenate(outs, axis=1).astype(o_ref.dtype)


def _attention(q, segs, batch, seq_q, *, tq):
    t = q.shape[0]
    qt = seq_q // tq
    in_specs = [pl.BlockSpec((tq, 2 * HEAD_PAD), lambda b, hp, j: (b * qt + j, hp))]
    args = [q]
    for k, v, m in segs:
        in_specs.append(pl.BlockSpec((m, 2 * HEAD_PAD), lambda b, hp, j: (b, hp)))
        in_specs.append(pl.BlockSpec((m, 2 * V_DIM), lambda b, hp, j: (b, hp)))
        args += [k, v]
    return pl.pallas_call(
        functools.partial(_attn_kernel, n_seg=len(segs)),
        grid=(batch, N_HEADS // 2, qt),
        in_specs=in_specs,
        out_specs=pl.BlockSpec((tq, 2 * V_DIM), lambda b, hp, j: (b * qt + j, hp)),
        out_shape=jax.ShapeDtypeStruct((t, N_HEADS * V_DIM), BF16),
        compiler_params=_params(3),
        name="attention",
    )(*args)


def _four_kernel(cs_ref, ab_ref, o_ref, *, n, fw):
    o = _dot(cs_ref[:, 0:n], ab_ref[:, 0:fw]) + _dot(cs_ref[:, n:2 * n], ab_ref[:, fw:2 * fw])
    o_ref[...] = o.astype(o_ref.dtype)


def _fourier(ab, cs, batch, seq_len, *, tn):
    t, fw2 = ab.shape
    fw = fw2 // 2
    nt = seq_len // tn
    return pl.pallas_call(
        functools.partial(_four_kernel, n=seq_len, fw=fw),
        grid=(batch, nt),
        in_specs=[
            pl.BlockSpec((tn, 2 * seq_len), lambda b, j: (j, 0)),
            pl.BlockSpec((seq_len, fw2), lambda b, j: (b, 0)),
        ],
        out_specs=pl.BlockSpec((tn, fw), lambda b, j: (b * nt + j, 0)),
        out_shape=jax.ShapeDtypeStruct((t, fw), BF16),
        compiler_params=_params(2),
        name="fourier",
    )(cs, ab)


def _merge_kernel(x_ref, mod_ref, o_ref, cb_ref, cc_ref, cu_ref, ccp_ref, cup_ref, ccn_ref,
                  cun_ref, f_ref, gate_ref, convw_ref, wmo_ref, wco_ref, wfo_ref, wout_ref,
                  gpost_ref, out_ref, pad_ref, *, tiles_per_seq, tm, d_model):
    i = pl.program_id(0)
    pos = i % tiles_per_seq
    has_prev = (pos > 0).astype(F32)
    has_next = (pos < tiles_per_seq - 1).astype(F32)
    pad_ref[0:8, :] = ccp_ref[...].astype(F32) * cup_ref[...].astype(F32) * has_prev
    pad_ref[8:8 + tm, :] = cc_ref[...].astype(F32) * cu_ref[...].astype(F32)
    pad_ref[8 + tm:16 + tm, :] = ccn_ref[...].astype(F32) * cun_ref[...].astype(F32) * has_next
    conv = (pad_ref[7:7 + tm, :] * convw_ref[0:1, :] + pad_ref[8:8 + tm, :] * convw_ref[1:2, :]
            + pad_ref[9:9 + tm, :] * convw_ref[2:3, :])
    y_conv = _dot((cb_ref[...].astype(F32) * conv).astype(BF16), wco_ref[...])
    y_attn = _dot(o_ref[...], wmo_ref[...])
    y_four = _dot(f_ref[...], wfo_ref[...])
    d = d_model
    merged = (gate_ref[:, 0:d].astype(F32) * y_attn + gate_ref[:, d:2 * d].astype(F32) * y_conv
              + gate_ref[:, 2 * d:3 * d].astype(F32) * y_four)
    y = _dot(merged.astype(BF16), wout_ref[...])
    g1 = mod_ref[0, 2:3, :]
    out_ref[...] = x_ref[...] + g1 * _rms(y, gpost_ref[...])


def _merge(xs, mod, seq_len, pr, o, four, lw, *, tm):
    t, d = xs.shape
    nb = mod.shape[0]
    tiles_per_seq = seq_len // tm
    cw = lw["conv_w"].shape[1]
    fw = four.shape[1]
    hb = tm // 8
    last_hb = t // 8 - 1

    def row(i):
        return (i, 0)

    def prev_map(i):
        return (jnp.maximum(i * hb - 1, 0), 0)

    def next_map(i):
        return (jnp.minimum((i + 1) * hb, last_hb), 0)

    in_specs = [
        pl.BlockSpec((tm, d), row),
        pl.BlockSpec((1,) + mod.shape[1:], lambda i: ((i // tiles_per_seq) % nb, 0, 0)),
        pl.BlockSpec((tm, o.shape[1]), row),
        pl.BlockSpec((tm, cw), row),
        pl.BlockSpec((tm, cw), row),
        pl.BlockSpec((tm, cw), row),
        pl.BlockSpec((8, cw), prev_map),
        pl.BlockSpec((8, cw), prev_map),
        pl.BlockSpec((8, cw), next_map),
        pl.BlockSpec((8, cw), next_map),
        pl.BlockSpec((tm, fw), row),
        pl.BlockSpec((tm, N_BRANCHES * d), row),
        _resident(lw["conv_w"].shape),
        _resident(lw["w_mla_out"].shape),
        _resident(lw["w_conv_out"].shape),
        _resident(lw["w_four_out"].shape),
        _resident(lw["w_out"].shape),
        _resident((1, d)),
    ]
    return pl.pallas_call(
        functools.partial(_merge_kernel, tiles_per_seq=tiles_per_seq, tm=tm, d_model=d),
        grid=(t // tm,),
        in_specs=in_specs,
        out_specs=pl.BlockSpec((tm, d), row),
        out_shape=jax.ShapeDtypeStruct((t, d), F32),
        scratch_shapes=[pltpu.VMEM((tm + 16, cw), F32)],
        compiler_params=_params(1),
        name="merge",
    )(xs, mod, o, pr["cb"], pr["cc"], pr["cu"], pr["cc"], pr["cu"], pr["cc"], pr["cu"], four,
      pr["gate"], lw["conv_w"], lw["w_mla_out"], lw["w_conv_out"], lw["w_four_out"], lw["w_out"],
      lw["g_post_mix"])


def _moe_kernel(x_ref, mod_ref, gpre_ref, gpost_ref, wr_ref, br_ref, wgs_ref, wus_ref, wds_ref,
                wge_ref, wue_ref, wde_ref, out_ref, t_ref, comb_ref, acc_ref, *, n_experts):
    e = pl.program_id(1)

    @pl.when(e == 0)
    def _():
        shift = mod_ref[0, 3:4, :]
        scale = mod_ref[0, 4:5, :]
        t = _rms(x_ref[...], gpre_ref[...]) * (1.0 + scale) + shift
        t_hi = t.astype(BF16)
        t_lo = (t - t_hi.astype(F32)).astype(BF16)
        t_ref[...] = t_hi
        hh = _dot(t_hi, wr_ref[...])
        logits = hh[:, 0:n_experts] + hh[:, n_experts:2 * n_experts] + _dot(t_lo, wr_ref[:, 0:n_experts])
        scores = _sigmoid(logits)
        work = scores + br_ref[...]
        lane = lax.broadcasted_iota(jnp.int32, scores.shape, 1)
        comb = jnp.zeros_like(scores)
        for _ in range(TOP_K):
            best = jnp.max(work, axis=-1, keepdims=True)
            first = jnp.min(jnp.where(work == best, lane, n_experts), axis=-1, keepdims=True)
            hit = lane == first
            comb = jnp.where(hit, scores, comb)
            work = jnp.where(hit, -jnp.inf, work)
        comb_ref[...] = comb / jnp.sum(comb, axis=-1, keepdims=True) * ROUTED_SCALE
        gate = _dot(t_hi, wgs_ref[...])
        act = (gate * _sigmoid(gate) * _dot(t_hi, wus_ref[...])).astype(BF16)
        acc_ref[...] = _dot(act, wds_ref[...])

    t_hi = t_ref[...]
    gate = _dot(t_hi, wge_ref[0, 0].astype(BF16))
    up = _dot(t_hi, wue_ref[0, 0].astype(BF16))
    lane = lax.broadcasted_iota(jnp.int32, comb_ref.shape, 1)
    w_e = jnp.sum(jnp.where(lane == e, comb_ref[...], 0.0), axis=-1, keepdims=True)
    act = (gate * _sigmoid(gate) * up * w_e).astype(BF16)
    acc_ref[...] += _dot(act, wde_ref[0, 0].astype(BF16))

    @pl.when(e == n_experts - 1)
    def _():
        g2 = mod_ref[0, 5:6, :]
        out_ref[...] = x_ref[...] + g2 * _rms(acc_ref[...], gpost_ref[...])


def _moe(xs, mod, seq_len, lw, w_gate_e, w_up_e, w_down_e, layer, *, tm):
    t, d = xs.shape
    nb = mod.shape[0]
    tiles_per_seq = max(seq_len // tm, 1)
    n_experts, _, f = w_gate_e.shape[1:]
    in_specs = [
        pl.BlockSpec((tm, d), lambda i, e: (i, 0)),
        pl.BlockSpec((1,) + mod.shape[1:], lambda i, e: ((i // tiles_per_seq) % nb, 0, 0)),
        _resident((1, d)),
        _resident((1, d)),
        _resident(lw["w_router"].shape),
        _resident((1, n_experts)),
        _resident(lw["w_gate_s"].shape),
        _resident(lw["w_up_s"].shape),
        _resident(lw["w_down_s"].shape),
        pl.BlockSpec((1, 1, d, f), lambda i, e: (layer, e, 0, 0)),
        pl.BlockSpec((1, 1, d, f), lambda i, e: (layer, e, 0, 0)),
        pl.BlockSpec((1, 1, f, d), lambda i, e: (layer, e, 0, 0)),
    ]
    return pl.pallas_call(
        functools.partial(_moe_kernel, n_experts=n_experts),
        grid=(t // tm, n_experts),
        in_specs=in_specs,
        out_specs=pl.BlockSpec((tm, d), lambda i, e: (i, 0)),
        out_shape=jax.ShapeDtypeStruct((t, d), F32),
        scratch_shapes=[pltpu.VMEM((tm, d), BF16), pltpu.VMEM((tm, n_experts), F32),
                        pltpu.VMEM((tm, d), F32)],
        compiler_params=_params(2),
        name="moe",
    )(xs, mod, lw["g_pre_ffn"], lw["g_post_ffn"], lw["w_router"], lw["b_router"], lw["w_gate_s"],
      lw["w_up_s"], lw["w_down_s"], w_gate_e, w_up_e, w_down_e)


def _pack_rows(v):
    bits = lax.bitcast_convert_type(v.astype(BF16).astype(F32), jnp.uint32)
    rows = []
    for j in range(v.shape[1] // (2 * PACK_W)):
        lo = bits[:, (2 * j) * PACK_W:(2 * j + 1) * PACK_W]
        hi = bits[:, (2 * j + 1) * PACK_W:(2 * j + 2) * PACK_W]
        rows.append(lax.bitcast_convert_type((hi & jnp.uint32(0xFFFF0000)) | (lo >> 16), jnp.int32))
    return rows


def _unpack_rows(rows):
    parts = []
    for r in rows:
        u = lax.bitcast_convert_type(r, jnp.uint32)
        parts.append(lax.bitcast_convert_type(u << 16, F32))
        parts.append(lax.bitcast_convert_type(u & jnp.uint32(0xFFFF0000), F32))
    return jnp.concatenate(parts, axis=1)


def _route_kernel(x_ref, mod_ref, gpre_ref, wr_ref, br_ref, tri_ref,
                  tp_ref, sel_ref, wgt_ref, rank_ref, cnt_ref, *, n_experts):
    shift = mod_ref[0, 3:4, :]
    scale = mod_ref[0, 4:5, :]
    t = _rms(x_ref[...], gpre_ref[...]) * (1.0 + scale) + shift
    t_hi = t.astype(BF16)
    t_lo = (t - t_hi.astype(F32)).astype(BF16)
    for j, r in enumerate(_pack_rows(t)):
        tp_ref[j] = r
    hh = _dot(t_hi, wr_ref[...])
    logits = hh[:, 0:n_experts] + hh[:, n_experts:2 * n_experts] + _dot(t_lo, wr_ref[:, 0:n_experts])
    scores = _sigmoid(logits)
    work = scores + br_ref[...]
    lane = lax.broadcasted_iota(jnp.int32, scores.shape, 1)
    wide = lax.broadcasted_iota(jnp.int32, (scores.shape[0], LANE), 1)
    firsts, picked, hits = [], [], []
    for k in range(TOP_K):
        best = jnp.max(work, axis=-1, keepdims=True)
        first = jnp.min(jnp.where(work == best, lane, n_experts), axis=-1, keepdims=True)
        hit = lane == first
        firsts.append(first)
        picked.append(jnp.sum(jnp.where(hit, scores, 0.0), axis=-1, keepdims=True))
        hits.append(wide == first + k * n_experts)
        work = jnp.where(hit, -jnp.inf, work)
    total = functools.reduce(jnp.add, picked)
    onehot = functools.reduce(jnp.add, [jnp.where(h, 1.0, 0.0) for h in hits])
    earlier = _dot(tri_ref[...], onehot.astype(BF16))
    col = jnp.sum(onehot, axis=0, keepdims=True)
    col8 = jnp.broadcast_to(col, (8, LANE))
    lane8 = lax.broadcasted_iota(jnp.int32, (8, LANE), 1)
    before = jnp.zeros((8, LANE), F32)
    for s in range(1, TOP_K):
        before = before + jnp.where(lane8 >= s * n_experts, pltpu.roll(col8, s * n_experts, 1), 0.0)
    ahead = earlier + before[0:1, :]
    sel = jnp.zeros(wide.shape, jnp.int32)
    wgt = jnp.zeros(wide.shape, F32)
    rank = jnp.zeros(wide.shape, F32)
    for k in range(TOP_K):
        sel = jnp.where(wide == k, firsts[k], sel)
        wgt = jnp.where(wide == k, picked[k] / total * ROUTED_SCALE, wgt)
        rank = jnp.where(wide == k, jnp.sum(jnp.where(hits[k], ahead, 0.0), axis=-1, keepdims=True), rank)
    sel_ref[...] = sel
    wgt_ref[...] = wgt
    rank_ref[...] = rank.astype(jnp.int32)
    cnt_ref[0] = col8.astype(jnp.int32)


def _route(x1, mod, seq_len, lw, *, tm):
    t, d = x1.shape
    nb = mod.shape[0]
    tiles_per_seq = max(seq_len // tm, 1)
    n_experts = lw["b_router"].shape[1]
    rows = d // (2 * PACK_W)
    tri = jnp.tri(tm, tm, -1, dtype=BF16)
    row = lambda i: (i, 0)
    return pl.pallas_call(
        functools.partial(_route_kernel, n_experts=n_experts),
        grid=(t // tm,),
        in_specs=[
            pl.BlockSpec((tm, d), row),
            pl.BlockSpec((1,) + mod.shape[1:], lambda i: ((i // tiles_per_seq) % nb, 0, 0)),
            _resident((1, d)),
            _resident(lw["w_router"].shape),
            _resident((1, n_experts)),
            _resident((tm, tm)),
        ],
        out_specs=[
            pl.BlockSpec((rows, tm, PACK_W), lambda i: (0, i, 0)),
            pl.BlockSpec((tm, LANE), row),
            pl.BlockSpec((tm, LANE), row),
            pl.BlockSpec((tm, LANE), row),
            pl.BlockSpec((1, 8, LANE), lambda i: (i, 0, 0)),
        ],
        out_shape=[
            jax.ShapeDtypeStruct((rows, t, PACK_W), jnp.int32),
            jax.ShapeDtypeStruct((t, LANE), jnp.int32),
            jax.ShapeDtypeStruct((t, LANE), F32),
            jax.ShapeDtypeStruct((t, LANE), jnp.int32),
            jax.ShapeDtypeStruct((t // tm, 8, LANE), jnp.int32),
        ],
        compiler_params=_params(1),
        name="moe_route",
    )(x1, mod, lw["g_pre_ffn"], lw["w_router"], lw["b_router"], tri)


def _expert_kernel(te_ref, meta_ref, xs_ref, wg_ref, wu_ref, wd_ref, ys_ref, wg_sc, wu_sc, wd_sc):
    i = pl.program_id(0)
    live = i < meta_ref[0]
    new_expert = (i == 0) | (te_ref[i] != te_ref[jnp.maximum(i - 1, 0)])

    @pl.when(live & new_expert)
    def _():
        wg_sc[...] = wg_ref[0, 0].astype(BF16)
        wu_sc[...] = wu_ref[0, 0].astype(BF16)
        wd_sc[...] = wd_ref[0, 0].astype(BF16)

    @pl.when(live)
    def _():
        x = _unpack_rows([xs_ref[j] for j in range(xs_ref.shape[0])]).astype(BF16)
        gate = _dot(x, wg_sc[...])
        act = (gate * _sigmoid(gate) * _dot(x, wu_sc[...])).astype(BF16)
        for j, r in enumerate(_pack_rows(_dot(act, wd_sc[...]))):
            ys_ref[j] = r


def _experts(xs, tile_expert, meta, w_gate_e, w_up_e, w_down_e, layer, *, tm):
    rows, p, _ = xs.shape
    n_experts, d, f = w_gate_e.shape[1:]

    def slot(i, te, meta):
        return (0, jnp.minimum(i, meta[0] - 1), 0)

    grid_spec = pltpu.PrefetchScalarGridSpec(
        num_scalar_prefetch=2,
        grid=(p // tm,),
        in_specs=[
            pl.BlockSpec((rows, tm, PACK_W), slot),
            pl.BlockSpec((1, 1, d, f), lambda i, te, meta: (layer, te[i], 0, 0)),
            pl.BlockSpec((1, 1, d, f), lambda i, te, meta: (layer, te[i], 0, 0)),
            pl.BlockSpec((1, 1, f, d), lambda i, te, meta: (layer, te[i], 0, 0)),
        ],
        out_specs=pl.BlockSpec((rows, tm, PACK_W), slot),
        scratch_shapes=[pltpu.VMEM((d, f), BF16), pltpu.VMEM((d, f), BF16), pltpu.VMEM((f, d), BF16)],
    )
    return pl.pallas_call(
        _expert_kernel,
        grid_spec=grid_spec,
        out_shape=jax.ShapeDtypeStruct(xs.shape, jnp.int32),
        compiler_params=_params(1),
        name="moe_experts",
    )(tile_expert, meta, xs, w_gate_e, w_up_e, w_down_e)


def _combine_kernel(x_ref, mod_ref, gpre_ref, gpost_ref, wgs_ref, wus_ref, wds_ref, yk_ref, wgt_ref,
                    out_ref):
    shift = mod_ref[0, 3:4, :]
    scale = mod_ref[0, 4:5, :]
    x = x_ref[...]
    t_hi = (_rms(x, gpre_ref[...]) * (1.0 + scale) + shift).astype(BF16)
    gate = _dot(t_hi, wgs_ref[...])
    act = (gate * _sigmoid(gate) * _dot(t_hi, wus_ref[...])).astype(BF16)
    acc = _dot(act, wds_ref[...])
    for k in range(TOP_K):
        y = _unpack_rows([yk_ref[k, j] for j in range(yk_ref.shape[1])])
        acc = acc + wgt_ref[:, k:k + 1] * y
    g2 = mod_ref[0, 5:6, :]
    out_ref[...] = x + g2 * _rms(acc, gpost_ref[...])


def _combine(x1, mod, seq_len, lw, yk, wgt, *, tm):
    t, d = x1.shape
    nb = mod.shape[0]
    tiles_per_seq = max(seq_len // tm, 1)
    rows = yk.shape[1]
    row = lambda i: (i, 0)
    return pl.pallas_call(
        _combine_kernel,
        grid=(t // tm,),
        in_specs=[
            pl.BlockSpec((tm, d), row),
            pl.BlockSpec((1,) + mod.shape[1:], lambda i: ((i // tiles_per_seq) % nb, 0, 0)),
            _resident((1, d)),
            _resident((1, d)),
            _resident(lw["w_gate_s"].shape),
            _resident(lw["w_up_s"].shape),
            _resident(lw["w_down_s"].shape),
            pl.BlockSpec((TOP_K, rows, tm, PACK_W), lambda i: (0, 0, i, 0)),
            pl.BlockSpec((tm, LANE), row),
        ],
        out_specs=pl.BlockSpec((tm, d), row),
        out_shape=jax.ShapeDtypeStruct((t, d), F32),
        compiler_params=_params(1),
        name="moe_combine",
    )(x1, mod, lw["g_pre_ffn"], lw["g_post_ffn"], lw["w_gate_s"], lw["w_up_s"], lw["w_down_s"], yk, wgt)


def _sc_mesh():
    return plsc.VectorSubcoreMesh(core_axis_name="core", subcore_axis_name="subcore")


def _sc_scatter_rows(src, idx, n_out):
    n_lists, n = idx.shape
    width = src.shape[1]
    windows = n // SC_WINDOW

    @pl.kernel(out_type=jax.ShapeDtypeStruct((n_out, width), src.dtype), mesh=_sc_mesh(), scratch_types=[])
    def scatter(x_hbm, i_hbm, o_hbm):
        def body(x_vmem, i_vmem):
            pltpu.sync_copy(x_vmem, o_hbm.at[i_vmem.at[0]])

        pltpu.emit_pipeline(
            body,
            grid=(n_lists * windows,),
            in_specs=[pl.BlockSpec((SC_WINDOW, width), lambda i: (i % windows, 0)),
                      pl.BlockSpec((1, SC_WINDOW), lambda i: (0, i))],
            out_specs=[],
            core_axis_name=("core", "subcore"),
            dimension_semantics=(pltpu.PARALLEL,),
        )(x_hbm, i_hbm)

    return scatter(src, idx.reshape(1, n_lists * n))


def _sc_gather_rows(table, idx):
    n = idx.shape[0]
    width = table.shape[1]

    @pl.kernel(out_type=jax.ShapeDtypeStruct((n, width), table.dtype), mesh=_sc_mesh(), scratch_types=[])
    def gather(x_hbm, i_hbm, o_hbm):
        def body(i_vmem, o_vmem):
            pltpu.sync_copy(x_hbm.at[i_vmem.at[0]], o_vmem)

        pltpu.emit_pipeline(
            body,
            grid=(n // SC_WINDOW,),
            in_specs=[pl.BlockSpec((1, SC_WINDOW), lambda i: (0, i))],
            out_specs=[pl.BlockSpec((SC_WINDOW, width), lambda i: (i, 0))],
            core_axis_name=("core", "subcore"),
            dimension_semantics=(pltpu.PARALLEL,),
        )(i_hbm, o_hbm)

    return gather(table, idx.reshape(1, n))


def _moe_sparse(x1, mod, seq_len, lw, w_gate_e, w_up_e, w_down_e, layer, *, tm, tm_e):
    t, d = x1.shape
    n_experts = lw["b_router"].shape[1]
    assert n_experts * TOP_K == LANE
    n_tok_tiles = t // tm
    tp, sel, wgt, rank, cnt = _route(x1, mod, seq_len, lw, tm=tm)
    rows = tp.shape[0]
    sel = sel[:, :TOP_K]
    rank = rank[:, :TOP_K]
    cnt = cnt[:, 0, :].reshape(n_tok_tiles, TOP_K, n_experts).sum(axis=1)
    total = cnt.sum(axis=0)
    padded = (total + tm_e - 1) // tm_e * tm_e
    group_end = jnp.cumsum(padded)
    base = (group_end - padded)[None, :] + jnp.cumsum(cnt, axis=0) - cnt
    base_t = jnp.broadcast_to(base[:, None, :], (n_tok_tiles, tm, n_experts)).reshape(t, n_experts)
    pos = jnp.take_along_axis(base_t, sel, axis=1) + rank

    n_row_tiles = (t * TOP_K) // tm_e + n_experts
    p = n_row_tiles * tm_e
    n_used = group_end[-1] // tm_e
    tile_start = jnp.arange(n_row_tiles, dtype=jnp.int32) * tm_e
    tile_expert = jnp.sum(tile_start[:, None] >= group_end[None, :], axis=1).astype(jnp.int32)
    tile_expert = jnp.minimum(tile_expert, n_experts - 1)
    tile_expert = jnp.where(tile_start < group_end[-1], tile_expert, tile_expert[n_used - 1])
    meta = jnp.stack([n_used, n_used]).astype(jnp.int32)

    idx = pos.T[:, None, :] + (jnp.arange(rows, dtype=jnp.int32) * p)[None, :, None]
    xs = _sc_scatter_rows(tp.reshape(rows * t, PACK_W), idx.reshape(TOP_K, rows * t), rows * p)
    ys = _experts(xs.reshape(rows, p, PACK_W), tile_expert, meta, w_gate_e, w_up_e, w_down_e, layer, tm=tm_e)
    yk = _sc_gather_rows(ys.reshape(rows * p, PACK_W), idx.reshape(-1))
    return _combine(x1, mod, seq_len, lw, yk.reshape(TOP_K, rows, t, PACK_W), wgt, tm=tm)


def _rope_tables(n):
    rows = n // GRID_W
    r, col = jnp.meshgrid(jnp.arange(rows), jnp.arange(GRID_W), indexing="ij")
    r = r.reshape(-1).astype(F32)
    col = col.reshape(-1).astype(F32)
    pairs = QK_ROPE // 4
    inv = ROPE_BASE ** (-jnp.arange(pairs, dtype=F32) / pairs)
    ang = jnp.concatenate([r[:, None] * inv, col[:, None] * inv], axis=-1)
    cos, sin = jnp.cos(ang), jnp.sin(ang)
    pad = HEAD_PAD - QK_NOPE - QK_ROPE
    cos_t = jnp.concatenate([jnp.ones((n, QK_NOPE), F32), cos, cos, jnp.zeros((n, pad), F32)], axis=1)
    sin_t = jnp.concatenate([jnp.zeros((n, QK_NOPE), F32), sin, sin, jnp.zeros((n, pad), F32)], axis=1)
    return cos_t, sin_t


def _identity_tables(n):
    pad = HEAD_PAD - QK_NOPE - QK_ROPE
    cos_t = jnp.concatenate([jnp.ones((n, QK_NOPE + QK_ROPE), F32), jnp.zeros((n, pad), F32)], axis=1)
    return cos_t, jnp.zeros((n, HEAD_PAD), F32)


def _position_dft(n):
    idx = (jnp.arange(n, dtype=jnp.int32)[:, None] * jnp.arange(n, dtype=jnp.int32)[None, :]) % n
    ang = idx.astype(F32) * (2.0 * np.pi / n)
    norm = 1.0 / np.sqrt(n)
    return jnp.concatenate([jnp.cos(ang) * norm, jnp.sin(ang) * (-norm)], axis=1).astype(BF16)


def _channel_dft(width):
    gc = width // FOURIER_GROUPS
    idx = (jnp.arange(gc, dtype=jnp.int32)[:, None] * jnp.arange(gc, dtype=jnp.int32)[None, :]) % gc
    ang = idx.astype(F32) * (2.0 * np.pi / gc)
    eye = jnp.eye(FOURIER_GROUPS, dtype=F32)
    norm = 1.0 / np.sqrt(gc)
    return jnp.concatenate([jnp.kron(eye, jnp.cos(ang) * norm), jnp.kron(eye, jnp.sin(ang) * norm)],
                           axis=1).astype(BF16)


def _pad_cols(w, width):
    return jnp.pad(w, ((0, 0), (0, width - w.shape[1])))


def _rot_cols(w):
    half = w.shape[-1] // 2
    return jnp.concatenate([-w[..., half:], w[..., :half]], axis=-1)


def _layer_weights(l, g_pre_mix, g_post_mix, g_pre_ffn, g_post_ffn, w_in, b_gate, g_q, w_uq, g_kv,
                   w_ukv, w_mla_out, conv_w, w_conv_out, w_four_out, w_out, w_router, b_router,
                   w_gate_s, w_up_s, w_down_s, dc):
    d = w_in.shape[1]
    kv_rank = g_kv.shape[1]
    q_rank = g_q.shape[1]
    w = w_in[l]
    kv_end = kv_rank + QK_ROPE
    zeros_nope = jnp.zeros((d, QK_NOPE), F32)
    kpe = w[:, kv_rank:kv_end]
    w1 = jnp.concatenate([
        w[:, :kv_rank],
        _pad_cols(jnp.concatenate([zeros_nope, kpe], axis=1), HEAD_PAD),
        _pad_cols(jnp.concatenate([zeros_nope, _rot_cols(kpe)], axis=1), HEAD_PAD),
        w[:, kv_end:],
    ], axis=1).astype(BF16)

    qk_dim = QK_NOPE + QK_ROPE
    pad = HEAD_PAD - qk_dim
    uq = w_uq[l].reshape(q_rank, N_HEADS, qk_dim) * (qk_dim ** -0.5)
    zq = jnp.zeros((q_rank, N_HEADS, pad), F32)
    wuq = jnp.concatenate([uq, zq], axis=-1).reshape(q_rank, N_HEADS * HEAD_PAD).astype(BF16)
    wuq_rot = jnp.concatenate([jnp.zeros((q_rank, N_HEADS, QK_NOPE), F32), _rot_cols(uq[..., QK_NOPE:]), zq],
                              axis=-1).reshape(q_rank, N_HEADS * HEAD_PAD).astype(BF16)
    ukv = w_ukv[l].reshape(kv_rank, N_HEADS, QK_NOPE + V_DIM)
    wuk = jnp.concatenate([ukv[..., :QK_NOPE], jnp.zeros((kv_rank, N_HEADS, HEAD_PAD - QK_NOPE), F32)],
                          axis=-1).reshape(kv_rank, N_HEADS * HEAD_PAD).astype(BF16)
    wuv = ukv[..., QK_NOPE:].reshape(kv_rank, N_HEADS * V_DIM).astype(BF16)

    wr = w_router[l]
    wr_hi = wr.astype(BF16)
    wr_lo = (wr - wr_hi.astype(F32)).astype(BF16)
    return {
        "g_pre_mix": g_pre_mix[l][None], "g_post_mix": g_post_mix[l][None],
        "g_pre_ffn": g_pre_ffn[l][None], "g_post_ffn": g_post_ffn[l][None],
        "w1": w1, "b_gate": b_gate[l][None], "g_q": g_q[l][None], "g_kv": g_kv[l][None],
        "wuq": wuq, "wuq_rot": wuq_rot, "wuk": wuk, "wuv": wuv, "dc": dc,
        "conv_w": conv_w[l],
        "w_mla_out": w_mla_out[l].astype(BF16), "w_conv_out": w_conv_out[l].astype(BF16),
        "w_four_out": w_four_out[l].astype(BF16), "w_out": w_out[l].astype(BF16),
        "w_router": jnp.concatenate([wr_hi, wr_lo], axis=1), "b_router": b_router[l][None],
        "w_gate_s": w_gate_s[l].astype(BF16), "w_up_s": w_up_s[l].astype(BF16),
        "w_down_s": w_down_s[l].astype(BF16),
    }


def _tile(n, pref):
    return pref if n % pref == 0 else n


def kernel(x, c, ctx, c_ctx, w_ada, b_ada, g_pre_mix, g_post_mix, g_pre_ffn, g_post_ffn, w_in, b_gate,
           g_q, w_uq, g_kv, w_ukv, w_mla_out, conv_w, w_conv_out, w_four_out, w_out, w_router, b_router,
           w_gate_e, w_up_e, w_down_e, w_gate_s, w_up_s, w_down_s):
    batch, seq, d = x.shape
    n_ctx = ctx.shape[1]
    n_layers = w_in.shape[0]
    xs = x.reshape(batch * seq, d)
    cs = ctx.reshape(batch * n_ctx, d)

    mod_rows = 16
    c_all = jnp.concatenate([c, c_ctx[None], jnp.zeros((mod_rows - batch - 1, d), F32)], axis=0)
    ada = _ada(c_all, w_ada, b_ada)

    tab_x = _rope_tables(seq)
    tm_c = _tile(n_ctx, 256)
    tab_c = _identity_tables(tm_c)
    cs_x = _position_dft(seq)
    cs_c = _position_dft(n_ctx)
    dc = _channel_dft(w_four_out.shape[1])

    tm_x = _tile(seq, 512)
    tm_moe_x = _tile(seq, 1024)
    tm_moe_c = _tile(batch * n_ctx, 1024)

    for l in range(n_layers):
        last = l == n_layers - 1
        lw = _layer_weights(l, g_pre_mix, g_post_mix, g_pre_ffn, g_post_ffn, w_in, b_gate, g_q, w_uq,
                            g_kv, w_ukv, w_mla_out, conv_w, w_conv_out, w_four_out, w_out, w_router,
                            b_router, w_gate_s, w_up_s, w_down_s, dc)
        mods = ada[l].reshape(mod_rows, 6, d)
        mod_x = mods[:batch]
        mod_c = mods[batch:batch + 1]

        pc = _inproj(cs, mod_c, n_ctx, lw, tab_c, kv_only=last, tm=tm_c)
        px = _inproj(xs, mod_x, seq, lw, tab_x, kv_only=False, tm=tm_x)
        o_x = _attention(px["q"], [(pc["k"], pc["v"], n_ctx), (px["k"], px["v"], seq)], batch, seq, tq=tm_x)
        f_x = _fourier(px["ab"], cs_x, batch, seq, tn=tm_x)
        x1 = _merge(xs, mod_x, seq, px, o_x, f_x, lw, tm=tm_x)
        xs = _moe_sparse(x1, mod_x, seq, lw, w_gate_e, w_up_e, w_down_e, l, tm=tm_moe_x, tm_e=512)
        if not last:
            o_c = _attention(pc["q"], [(pc["k"], pc["v"], n_ctx)], batch, n_ctx, tq=tm_c)
            f_c = _fourier(pc["ab"], cs_c, batch, n_ctx, tn=tm_c)
            c1 = _merge(cs, mod_c, n_ctx, pc, o_c, f_c, lw, tm=tm_c)
            cs = _moe(c1, mod_c, batch * n_ctx, lw, w_gate_e, w_up_e, w_down_e, l, tm=tm_moe_c)
    return xs.reshape(batch, seq, d)
```

```python
import functools

import numpy as np
import jax
import jax.numpy as jnp
from jax import lax
from jax.experimental import pallas as pl
from jax.experimental.pallas import tpu as pltpu
from jax.experimental.pallas import tpu_sc as plsc

N_HEADS = 8
QK_NOPE = 64
QK_ROPE = 32
V_DIM = 64
GRID_W = 64
ROPE_BASE = 10000.0
FOURIER_GROUPS = 4
TOP_K = 4
ROUTED_SCALE = 2.5
N_BRANCHES = 3
EPS = 1e-6

LANE = 128
HEAD_PAD = LANE
VMEM_LIMIT = 56 * 1024 * 1024
PACK_W = 256
SC_WINDOW = 128

F32 = jnp.float32
BF16 = jnp.bfloat16


def _rms(x, g):
    return x * lax.rsqrt(jnp.mean(x * x, axis=-1, keepdims=True) + EPS) * g


def _sigmoid(x):
    return 1.0 / (1.0 + jnp.exp(-x))


def _dot(a, b):
    return jnp.dot(a, b, preferred_element_type=F32)


def _resident(shape):
    nd = len(shape)
    return pl.BlockSpec(shape, lambda *_: (0,) * nd, pipeline_mode=pl.Buffered(1))


def _params(n_grid):
    return pltpu.CompilerParams(dimension_semantics=("arbitrary",) * n_grid,
                                vmem_limit_bytes=VMEM_LIMIT)


def _ada_kernel(c_ref, w_ref, b_ref, o_ref):
    c = c_ref[...]
    a = (c * _sigmoid(c)).astype(BF16)
    o_ref[0] = _dot(a, w_ref[0].astype(BF16)) + b_ref[0]


def _ada(c_all, w_ada, b_ada):
    n_layers, d, n_out = w_ada.shape
    rows = c_all.shape[0]
    tn = 1536
    return pl.pallas_call(
        _ada_kernel,
        grid=(n_layers, n_out // tn),
        in_specs=[
            pl.BlockSpec((rows, d), lambda l, j: (0, 0)),
            pl.BlockSpec((1, d, tn), lambda l, j: (l, 0, j)),
            pl.BlockSpec((1, 1, tn), lambda l, j: (l, 0, j)),
        ],
        out_specs=pl.BlockSpec((1, rows, tn), lambda l, j: (l, 0, j)),
        out_shape=jax.ShapeDtypeStruct((n_layers, rows, n_out), F32),
        compiler_params=_params(2),
        name="ada",
    )(c_all, w_ada, b_ada.reshape(n_layers, 1, n_out))


def _inproj_kernel(*refs, kv_only, kv_rank, q_rank, conv_w, four_w, d_model):
    if kv_only:
        (x_ref, mod_ref, gpre_ref, w1_ref, gkv_ref, wuk_ref, wuv_ref, vone_ref, cos_ref, sin_ref,
         k_ref, v_ref) = refs
    else:
        (x_ref, mod_ref, gpre_ref, w1_ref, gkv_ref, wuk_ref, wuv_ref, vone_ref, cos_ref, sin_ref,
         bg_ref, gq_ref, wuq_ref, wuqr_ref, dc_ref,
         k_ref, v_ref, q_ref, cb_ref, cc_ref, cu_ref, ab_ref, gate_ref) = refs

    x = x_ref[...]
    shift = mod_ref[0, 0:1, :]
    scale = mod_ref[0, 1:2, :]
    h = (_rms(x, gpre_ref[...]) * (1.0 + scale) + shift).astype(BF16)
    cos = cos_ref[...]
    sin = sin_ref[...]

    o_kpe = kv_rank
    o_rot = o_kpe + HEAD_PAD
    o_q = o_rot + HEAD_PAD
    p = _dot(h, w1_ref[:, 0:o_q])
    ckv = _rms(p[:, 0:kv_rank], gkv_ref[...]).astype(BF16)
    kpe = p[:, o_kpe:o_rot] * cos + p[:, o_rot:o_q] * sin
    k = _dot(ckv, wuk_ref[...]) + jnp.concatenate([kpe] * N_HEADS, axis=1)
    k_ref[...] = k.astype(k_ref.dtype)
    v_ref[...] = (_dot(ckv, wuv_ref[...]) + vone_ref[...]).astype(v_ref.dtype)
    if kv_only:
        return

    o_cb = o_q + q_rank
    cq = _rms(_dot(h, w1_ref[:, o_q:o_cb]), gq_ref[...]).astype(BF16)
    cos_h = jnp.concatenate([cos] * N_HEADS, axis=1)
    sin_h = jnp.concatenate([sin] * N_HEADS, axis=1)
    q = _dot(cq, wuq_ref[...]) * cos_h + _dot(cq, wuqr_ref[...]) * sin_h
    q_ref[...] = q.astype(q_ref.dtype)

    o_cc = o_cb + conv_w
    o_cu = o_cc + conv_w
    o_four = o_cu + conv_w
    cb_ref[...] = _dot(h, w1_ref[:, o_cb:o_cc]).astype(cb_ref.dtype)
    cc_ref[...] = _dot(h, w1_ref[:, o_cc:o_cu]).astype(cc_ref.dtype)
    cu_ref[...] = _dot(h, w1_ref[:, o_cu:o_four]).astype(cu_ref.dtype)

    o_gate = o_four + four_w
    uf = _dot(h, w1_ref[:, o_four:o_gate]).astype(BF16)
    ab_ref[...] = _dot(uf, dc_ref[...]).astype(ab_ref.dtype)

    for j in range(N_BRANCHES):
        lo = o_gate + j * d_model
        z = _dot(h, w1_ref[:, lo:lo + d_model]) + bg_ref[:, j * d_model:(j + 1) * d_model]
        gate_ref[:, j * d_model:(j + 1) * d_model] = _sigmoid(z).astype(gate_ref.dtype)


def _inproj(xs, mod, seq_len, lw, tables, *, kv_only, tm):
    t, d = xs.shape
    nb = mod.shape[0]
    tiles_per_seq = seq_len // tm
    cos_t, sin_t = tables
    table_tiles = cos_t.shape[0] // tm
    kv_rank = lw["g_kv"].shape[1]
    q_rank = lw["g_q"].shape[1]
    conv_w = lw["conv_w"].shape[1]
    four_w = lw["dc"].shape[0]
    n_k = N_HEADS * HEAD_PAD

    def row(i):
        return (i, 0)

    def mod_map(i):
        return ((i // tiles_per_seq) % nb, 0, 0)

    def tab_map(i):
        return (i % table_tiles, 0)

    w1 = lw["w1"]
    if kv_only:
        w1 = w1[:, :kv_rank + 2 * HEAD_PAD]
    in_specs = [
        pl.BlockSpec((tm, d), row),
        pl.BlockSpec((1,) + mod.shape[1:], mod_map),
        _resident((1, d)),
        _resident(w1.shape),
        _resident((1, kv_rank)),
        _resident(lw["wuk"].shape),
        _resident(lw["wuv"].shape),
        _resident(lw["v_one"].shape),
        pl.BlockSpec((tm, HEAD_PAD), tab_map),
        pl.BlockSpec((tm, HEAD_PAD), tab_map),
    ]
    args = [xs, mod, lw["g_pre_mix"], w1, lw["g_kv"], lw["wuk"], lw["wuv"], lw["v_one"], cos_t, sin_t]
    out_shape = [jax.ShapeDtypeStruct((t, n_k), BF16), jax.ShapeDtypeStruct((t, n_k), BF16)]
    out_specs = [pl.BlockSpec((tm, n_k), row), pl.BlockSpec((tm, n_k), row)]
    if not kv_only:
        in_specs += [
            _resident(lw["b_gate"].shape),
            _resident((1, q_rank)),
            _resident(lw["wuq"].shape),
            _resident(lw["wuq_rot"].shape),
            _resident(lw["dc"].shape),
        ]
        args += [lw["b_gate"], lw["g_q"], lw["wuq"], lw["wuq_rot"], lw["dc"]]
        widths = [n_k, conv_w, conv_w, conv_w, 2 * four_w, N_BRANCHES * d]
        out_shape += [jax.ShapeDtypeStruct((t, w), BF16) for w in widths]
        out_specs += [pl.BlockSpec((tm, w), row) for w in widths]
    outs = pl.pallas_call(
        functools.partial(_inproj_kernel, kv_only=kv_only, kv_rank=kv_rank, q_rank=q_rank,
                          conv_w=conv_w, four_w=four_w, d_model=d),
        grid=(t // tm,),
        in_specs=in_specs,
        out_specs=out_specs,
        out_shape=out_shape,
        compiler_params=_params(1),
        name="inproj_kv" if kv_only else "inproj",
    )(*args)
    names = ["k", "v", "q", "cb", "cc", "cu", "ab", "gate"]
    return dict(zip(names, outs))


def _attn_kernel(*refs, n_seg):
    q_ref = refs[0]
    o_ref = refs[-1]
    nt = (((1,), (1,)), ((), ()))
    outs = []
    for hh in range(N_HEADS):
        head = slice(hh * HEAD_PAD, (hh + 1) * HEAD_PAD)
        qh = q_ref[:, head]
        s = [lax.dot_general(qh, refs[1 + 2 * i][:, head], nt, preferred_element_type=F32)
             for i in range(n_seg)]
        m = functools.reduce(jnp.maximum, [jnp.max(si, axis=-1, keepdims=True) for si in s])
        acc = functools.reduce(jnp.add, [
            _dot(jnp.exp2((s[i] - m).astype(BF16)), refs[2 + 2 * i][:, head]) for i in range(n_seg)])
        outs.append(acc[:, 0:V_DIM] / acc[:, V_DIM:V_DIM + 1])
    o_ref[...] = jnp.concatenate(outs, axis=1).astype(o_ref.dtype)


def _attention(q, segs, batch, seq_q, *, tq):
    t = q.shape[0]
    qt = seq_q // tq
    n_k = N_HEADS * HEAD_PAD
    in_specs = [pl.BlockSpec((tq, n_k), lambda b, j: (b * qt + j, 0))]
    args = [q]
    for k, v, m in segs:
        in_specs.append(pl.BlockSpec((m, n_k), lambda b, j: (b, 0)))
        in_specs.append(pl.BlockSpec((m, n_k), lambda b, j: (b, 0)))
        args += [k, v]
    return pl.pallas_call(
        functools.partial(_attn_kernel, n_seg=len(segs)),
        grid=(batch, qt),
        in_specs=in_specs,
        out_specs=pl.BlockSpec((tq, N_HEADS * V_DIM), lambda b, j: (b * qt + j, 0)),
        out_shape=jax.ShapeDtypeStruct((t, N_HEADS * V_DIM), BF16),
        compiler_params=_params(2),
        name="attention",
    )(*args)


def _four_kernel(cs_ref, ab_ref, o_ref, *, n, fw):
    o = _dot(cs_ref[:, 0:n], ab_ref[:, 0:fw]) + _dot(cs_ref[:, n:2 * n], ab_ref[:, fw:2 * fw])
    o_ref[...] = o.astype(o_ref.dtype)


def _fourier(ab, cs, batch, seq_len, *, tn):
    t, fw2 = ab.shape
    fw = fw2 // 2
    nt = seq_len // tn
    return pl.pallas_call(
        functools.partial(_four_kernel, n=seq_len, fw=fw),
        grid=(batch, nt),
        in_specs=[
            pl.BlockSpec((tn, 2 * seq_len), lambda b, j: (j, 0)),
            pl.BlockSpec((seq_len, fw2), lambda b, j: (b, 0)),
        ],
        out_specs=pl.BlockSpec((tn, fw), lambda b, j: (b * nt + j, 0)),
        out_shape=jax.ShapeDtypeStruct((t, fw), BF16),
        compiler_params=_params(2),
        name="fourier",
    )(cs, ab)


def _merge_kernel(x_ref, mod_ref, o_ref, cb_ref, cc_ref, cu_ref, ccp_ref, cup_ref, ccn_ref,
                  cun_ref, f_ref, gate_ref, convw_ref, wmo_ref, wco_ref, wfo_ref, wout_ref,
                  gpost_ref, out_ref, pad_ref, *, tiles_per_seq, tm, d_model):
    i = pl.program_id(0)
    pos = i % tiles_per_seq
    has_prev = (pos > 0).astype(F32)
    has_next = (pos < tiles_per_seq - 1).astype(F32)
    pad_ref[0:8, :] = ccp_ref[...].astype(F32) * cup_ref[...].astype(F32) * has_prev
    pad_ref[8:8 + tm, :] = cc_ref[...].astype(F32) * cu_ref[...].astype(F32)
    pad_ref[8 + tm:16 + tm, :] = ccn_ref[...].astype(F32) * cun_ref[...].astype(F32) * has_next
    conv = (pad_ref[7:7 + tm, :] * convw_ref[0:1, :] + pad_ref[8:8 + tm, :] * convw_ref[1:2, :]
            + pad_ref[9:9 + tm, :] * convw_ref[2:3, :])
    y_conv = _dot((cb_ref[...].astype(F32) * conv).astype(BF16), wco_ref[...])
    y_attn = _dot(o_ref[...], wmo_ref[...])
    y_four = _dot(f_ref[...], wfo_ref[...])
    d = d_model
    merged = (gate_ref[:, 0:d].astype(F32) * y_attn + gate_ref[:, d:2 * d].astype(F32) * y_conv
              + gate_ref[:, 2 * d:3 * d].astype(F32) * y_four)
    y = _dot(merged.astype(BF16), wout_ref[...])
    g1 = mod_ref[0, 2:3, :]
    out_ref[...] = x_ref[...] + g1 * _rms(y, gpost_ref[...])


def _merge(xs, mod, seq_len, pr, o, four, lw, *, tm):
    t, d = xs.shape
    nb = mod.shape[0]
    tiles_per_seq = seq_len // tm
    cw = lw["conv_w"].shape[1]
    fw = four.shape[1]
    hb = tm // 8
    last_hb = t // 8 - 1

    def row(i):
        return (i, 0)

    def prev_map(i):
        return (jnp.maximum(i * hb - 1, 0), 0)

    def next_map(i):
        return (jnp.minimum((i + 1) * hb, last_hb), 0)

    in_specs = [
        pl.BlockSpec((tm, d), row),
        pl.BlockSpec((1,) + mod.shape[1:], lambda i: ((i // tiles_per_seq) % nb, 0, 0)),
        pl.BlockSpec((tm, o.shape[1]), row),
        pl.BlockSpec((tm, cw), row),
        pl.BlockSpec((tm, cw), row),
        pl.BlockSpec((tm, cw), row),
        pl.BlockSpec((8, cw), prev_map),
        pl.BlockSpec((8, cw), prev_map),
        pl.BlockSpec((8, cw), next_map),
        pl.BlockSpec((8, cw), next_map),
        pl.BlockSpec((tm, fw), row),
        pl.BlockSpec((tm, N_BRANCHES * d), row),
        _resident(lw["conv_w"].shape),
        _resident(lw["w_mla_out"].shape),
        _resident(lw["w_conv_out"].shape),
        _resident(lw["w_four_out"].shape),
        _resident(lw["w_out"].shape),
        _resident((1, d)),
    ]
    return pl.pallas_call(
        functools.partial(_merge_kernel, tiles_per_seq=tiles_per_seq, tm=tm, d_model=d),
        grid=(t // tm,),
        in_specs=in_specs,
        out_specs=pl.BlockSpec((tm, d), row),
        out_shape=jax.ShapeDtypeStruct((t, d), F32),
        scratch_shapes=[pltpu.VMEM((tm + 16, cw), F32)],
        compiler_params=_params(1),
        name="merge",
    )(xs, mod, o, pr["cb"], pr["cc"], pr["cu"], pr["cc"], pr["cu"], pr["cc"], pr["cu"], four,
      pr["gate"], lw["conv_w"], lw["w_mla_out"], lw["w_conv_out"], lw["w_four_out"], lw["w_out"],
      lw["g_post_mix"])


def _moe_kernel(x_ref, mod_ref, gpre_ref, gpost_ref, wr_ref, br_ref, wgs_ref, wus_ref, wds_ref,
                wge_ref, wue_ref, wde_ref, out_ref, t_ref, comb_ref, acc_ref, *, n_experts):
    e = pl.program_id(1)

    @pl.when(e == 0)
    def _():
        shift = mod_ref[0, 3:4, :]
        scale = mod_ref[0, 4:5, :]
        t = _rms(x_ref[...], gpre_ref[...]) * (1.0 + scale) + shift
        t_hi = t.astype(BF16)
        t_lo = (t - t_hi.astype(F32)).astype(BF16)
        t_ref[...] = t_hi
        hh = _dot(t_hi, wr_ref[...])
        logits = hh[:, 0:n_experts] + hh[:, n_experts:2 * n_experts] + _dot(t_lo, wr_ref[:, 0:n_experts])
        scores = _sigmoid(logits)
        work = scores + br_ref[...]
        lane = lax.broadcasted_iota(jnp.int32, scores.shape, 1)
        comb = jnp.zeros_like(scores)
        for _ in range(TOP_K):
            best = jnp.max(work, axis=-1, keepdims=True)
            first = jnp.min(jnp.where(work == best, lane, n_experts), axis=-1, keepdims=True)
            hit = lane == first
            comb = jnp.where(hit, scores, comb)
            work = jnp.where(hit, -jnp.inf, work)
        comb_ref[...] = comb / jnp.sum(comb, axis=-1, keepdims=True) * ROUTED_SCALE
        gate = _dot(t_hi, wgs_ref[...])
        act = (gate * _sigmoid(gate) * _dot(t_hi, wus_ref[...])).astype(BF16)
        acc_ref[...] = _dot(act, wds_ref[...])

    t_hi = t_ref[...]
    gate = _dot(t_hi, wge_ref[0, 0].astype(BF16))
    up = _dot(t_hi, wue_ref[0, 0].astype(BF16))
    lane = lax.broadcasted_iota(jnp.int32, comb_ref.shape, 1)
    w_e = jnp.sum(jnp.where(lane == e, comb_ref[...], 0.0), axis=-1, keepdims=True)
    act = (gate * _sigmoid(gate) * up * w_e).astype(BF16)
    acc_ref[...] += _dot(act, wde_ref[0, 0].astype(BF16))

    @pl.when(e == n_experts - 1)
    def _():
        g2 = mod_ref[0, 5:6, :]
        out_ref[...] = x_ref[...] + g2 * _rms(acc_ref[...], gpost_ref[...])


def _moe(xs, mod, seq_len, lw, w_gate_e, w_up_e, w_down_e, layer, *, tm):
    t, d = xs.shape
    nb = mod.shape[0]
    tiles_per_seq = max(seq_len // tm, 1)
    n_experts, _, f = w_gate_e.shape[1:]
    in_specs = [
        pl.BlockSpec((tm, d), lambda i, e: (i, 0)),
        pl.BlockSpec((1,) + mod.shape[1:], lambda i, e: ((i // tiles_per_seq) % nb, 0, 0)),
        _resident((1, d)),
        _resident((1, d)),
        _resident(lw["w_router"].shape),
        _resident((1, n_experts)),
        _resident(lw["w_gate_s"].shape),
        _resident(lw["w_up_s"].shape),
        _resident(lw["w_down_s"].shape),
        pl.BlockSpec((1, 1, d, f), lambda i, e: (layer, e, 0, 0)),
        pl.BlockSpec((1, 1, d, f), lambda i, e: (layer, e, 0, 0)),
        pl.BlockSpec((1, 1, f, d), lambda i, e: (layer, e, 0, 0)),
    ]
    return pl.pallas_call(
        functools.partial(_moe_kernel, n_experts=n_experts),
        grid=(t // tm, n_experts),
        in_specs=in_specs,
        out_specs=pl.BlockSpec((tm, d), lambda i, e: (i, 0)),
        out_shape=jax.ShapeDtypeStruct((t, d), F32),
        scratch_shapes=[pltpu.VMEM((tm, d), BF16), pltpu.VMEM((tm, n_experts), F32),
                        pltpu.VMEM((tm, d), F32)],
        compiler_params=_params(2),
        name="moe",
    )(xs, mod, lw["g_pre_ffn"], lw["g_post_ffn"], lw["w_router"], lw["b_router"], lw["w_gate_s"],
      lw["w_up_s"], lw["w_down_s"], w_gate_e, w_up_e, w_down_e)


def _pack_rows(v):
    bits = lax.bitcast_convert_type(v.astype(BF16).astype(F32), jnp.uint32)
    rows = []
    for j in range(v.shape[1] // (2 * PACK_W)):
        lo = bits[:, (2 * j) * PACK_W:(2 * j + 1) * PACK_W]
        hi = bits[:, (2 * j + 1) * PACK_W:(2 * j + 2) * PACK_W]
        rows.append(lax.bitcast_convert_type((hi & jnp.uint32(0xFFFF0000)) | (lo >> 16), jnp.int32))
    return rows


def _unpack_rows(rows):
    parts = []
    for r in rows:
        u = lax.bitcast_convert_type(r, jnp.uint32)
        parts.append(lax.bitcast_convert_type(u << 16, F32))
        parts.append(lax.bitcast_convert_type(u & jnp.uint32(0xFFFF0000), F32))
    return jnp.concatenate(parts, axis=1)


def _route_kernel(x_ref, mod_ref, gpre_ref, wr_ref, br_ref, tri_ref,
                  tp_ref, sel_ref, wgt_ref, rank_ref, cnt_ref, *, n_experts):
    shift = mod_ref[0, 3:4, :]
    scale = mod_ref[0, 4:5, :]
    t = _rms(x_ref[...], gpre_ref[...]) * (1.0 + scale) + shift
    t_hi = t.astype(BF16)
    t_lo = (t - t_hi.astype(F32)).astype(BF16)
    for j, r in enumerate(_pack_rows(t)):
        tp_ref[j] = r
    hh = _dot(t_hi, wr_ref[...])
    logits = hh[:, 0:n_experts] + hh[:, n_experts:2 * n_experts] + _dot(t_lo, wr_ref[:, 0:n_experts])
    scores = _sigmoid(logits)
    work = scores + br_ref[...]
    lane = lax.broadcasted_iota(jnp.int32, scores.shape, 1)
    wide = lax.broadcasted_iota(jnp.int32, (scores.shape[0], LANE), 1)
    firsts, picked, hits = [], [], []
    for k in range(TOP_K):
        best = jnp.max(work, axis=-1, keepdims=True)
        first = jnp.min(jnp.where(work == best, lane, n_experts), axis=-1, keepdims=True)
        hit = lane == first
        firsts.append(first)
        picked.append(jnp.sum(jnp.where(hit, scores, 0.0), axis=-1, keepdims=True))
        hits.append(wide == first + k * n_experts)
        work = jnp.where(hit, -jnp.inf, work)
    total = functools.reduce(jnp.add, picked)
    onehot = functools.reduce(jnp.add, [jnp.where(h, 1.0, 0.0) for h in hits])
    earlier = _dot(tri_ref[...], onehot.astype(BF16))
    col = jnp.sum(onehot, axis=0, keepdims=True)
    col8 = jnp.broadcast_to(col, (8, LANE))
    lane8 = lax.broadcasted_iota(jnp.int32, (8, LANE), 1)
    before = jnp.zeros((8, LANE), F32)
    for s in range(1, TOP_K):
        before = before + jnp.where(lane8 >= s * n_experts, pltpu.roll(col8, s * n_experts, 1), 0.0)
    ahead = earlier + before[0:1, :]
    sel = jnp.zeros(wide.shape, jnp.int32)
    wgt = jnp.zeros(wide.shape, F32)
    rank = jnp.zeros(wide.shape, F32)
    for k in range(TOP_K):
        sel = jnp.where(wide == k, firsts[k], sel)
        wgt = jnp.where(wide == k, picked[k] / total * ROUTED_SCALE, wgt)
        rank = jnp.where(wide == k, jnp.sum(jnp.where(hits[k], ahead, 0.0), axis=-1, keepdims=True), rank)
    sel_ref[...] = sel
    wgt_ref[...] = wgt
    rank_ref[...] = rank.astype(jnp.int32)
    cnt_ref[0] = col8.astype(jnp.int32)


def _route(x1, mod, seq_len, lw, *, tm):
    t, d = x1.shape
    nb = mod.shape[0]
    tiles_per_seq = max(seq_len // tm, 1)
    n_experts = lw["b_router"].shape[1]
    rows = d // (2 * PACK_W)
    tri = jnp.tri(tm, tm, -1, dtype=BF16)
    row = lambda i: (i, 0)
    return pl.pallas_call(
        functools.partial(_route_kernel, n_experts=n_experts),
        grid=(t // tm,),
        in_specs=[
            pl.BlockSpec((tm, d), row),
            pl.BlockSpec((1,) + mod.shape[1:], lambda i: ((i // tiles_per_seq) % nb, 0, 0)),
            _resident((1, d)),
            _resident(lw["w_router"].shape),
            _resident((1, n_experts)),
            _resident((tm, tm)),
        ],
        out_specs=[
            pl.BlockSpec((rows, tm, PACK_W), lambda i: (0, i, 0)),
            pl.BlockSpec((tm, LANE), row),
            pl.BlockSpec((tm, LANE), row),
            pl.BlockSpec((tm, LANE), row),
            pl.BlockSpec((1, 8, LANE), lambda i: (i, 0, 0)),
        ],
        out_shape=[
            jax.ShapeDtypeStruct((rows, t, PACK_W), jnp.int32),
            jax.ShapeDtypeStruct((t, LANE), jnp.int32),
            jax.ShapeDtypeStruct((t, LANE), F32),
            jax.ShapeDtypeStruct((t, LANE), jnp.int32),
            jax.ShapeDtypeStruct((t // tm, 8, LANE), jnp.int32),
        ],
        compiler_params=_params(1),
        name="moe_route",
    )(x1, mod, lw["g_pre_ffn"], lw["w_router"], lw["b_router"], tri)


def _expert_kernel(te_ref, meta_ref, xs_ref, wg_ref, wu_ref, wd_ref, ys_ref, wg_sc, wu_sc, wd_sc):
    i = pl.program_id(0)
    live = i < meta_ref[0]
    new_expert = (i == 0) | (te_ref[i] != te_ref[jnp.maximum(i - 1, 0)])

    @pl.when(live & new_expert)
    def _():
        wg_sc[...] = wg_ref[0, 0].astype(BF16)
        wu_sc[...] = wu_ref[0, 0].astype(BF16)
        wd_sc[...] = wd_ref[0, 0].astype(BF16)

    @pl.when(live)
    def _():
        x = _unpack_rows([xs_ref[j] for j in range(xs_ref.shape[0])]).astype(BF16)
        gate = _dot(x, wg_sc[...])
        act = (gate * _sigmoid(gate) * _dot(x, wu_sc[...])).astype(BF16)
        for j, r in enumerate(_pack_rows(_dot(act, wd_sc[...]))):
            ys_ref[j] = r


def _experts(xs, tile_expert, meta, w_gate_e, w_up_e, w_down_e, layer, *, tm):
    rows, p, _ = xs.shape
    n_experts, d, f = w_gate_e.shape[1:]

    def slot(i, te, meta):
        return (0, jnp.minimum(i, meta[0] - 1), 0)

    grid_spec = pltpu.PrefetchScalarGridSpec(
        num_scalar_prefetch=2,
        grid=(p // tm,),
        in_specs=[
            pl.BlockSpec((rows, tm, PACK_W), slot),
            pl.BlockSpec((1, 1, d, f), lambda i, te, meta: (layer, te[i], 0, 0)),
            pl.BlockSpec((1, 1, d, f), lambda i, te, meta: (layer, te[i], 0, 0)),
            pl.BlockSpec((1, 1, f, d), lambda i, te, meta: (layer, te[i], 0, 0)),
        ],
        out_specs=pl.BlockSpec((rows, tm, PACK_W), slot),
        scratch_shapes=[pltpu.VMEM((d, f), BF16), pltpu.VMEM((d, f), BF16), pltpu.VMEM((f, d), BF16)],
    )
    return pl.pallas_call(
        _expert_kernel,
        grid_spec=grid_spec,
        out_shape=jax.ShapeDtypeStruct(xs.shape, jnp.int32),
        compiler_params=_params(1),
        name="moe_experts",
    )(tile_expert, meta, xs, w_gate_e, w_up_e, w_down_e)


def _combine_kernel(x_ref, mod_ref, gpre_ref, gpost_ref, wgs_ref, wus_ref, wds_ref, yk_ref, wgt_ref,
                    out_ref):
    shift = mod_ref[0, 3:4, :]
    scale = mod_ref[0, 4:5, :]
    x = x_ref[...]
    t_hi = (_rms(x, gpre_ref[...]) * (1.0 + scale) + shift).astype(BF16)
    gate = _dot(t_hi, wgs_ref[...])
    act = (gate * _sigmoid(gate) * _dot(t_hi, wus_ref[...])).astype(BF16)
    acc = _dot(act, wds_ref[...])
    for k in range(TOP_K):
        y = _unpack_rows([yk_ref[k, j] for j in range(yk_ref.shape[1])])
        acc = acc + wgt_ref[:, k:k + 1] * y
    g2 = mod_ref[0, 5:6, :]
    out_ref[...] = x + g2 * _rms(acc, gpost_ref[...])


def _combine(x1, mod, seq_len, lw, yk, wgt, *, tm):
    t, d = x1.shape
    nb = mod.shape[0]
    tiles_per_seq = max(seq_len // tm, 1)
    rows = yk.shape[1]
    row = lambda i: (i, 0)
    return pl.pallas_call(
        _combine_kernel,
        grid=(t // tm,),
        in_specs=[
            pl.BlockSpec((tm, d), row),
            pl.BlockSpec((1,) + mod.shape[1:], lambda i: ((i // tiles_per_seq) % nb, 0, 0)),
            _resident((1, d)),
            _resident((1, d)),
            _resident(lw["w_gate_s"].shape),
            _resident(lw["w_up_s"].shape),
            _resident(lw["w_down_s"].shape),
            pl.BlockSpec((TOP_K, rows, tm, PACK_W), lambda i: (0, 0, i, 0)),
            pl.BlockSpec((tm, LANE), row),
        ],
        out_specs=pl.BlockSpec((tm, d), row),
        out_shape=jax.ShapeDtypeStruct((t, d), F32),
        compiler_params=_params(1),
        name="moe_combine",
    )(x1, mod, lw["g_pre_ffn"], lw["g_post_ffn"], lw["w_gate_s"], lw["w_up_s"], lw["w_down_s"], yk, wgt)


def _sc_mesh():
    return plsc.VectorSubcoreMesh(core_axis_name="core", subcore_axis_name="subcore")


def _sc_scatter_rows(src, idx, n_out):
    n_lists, n = idx.shape
    width = src.shape[1]
    windows = n // SC_WINDOW

    @pl.kernel(out_type=jax.ShapeDtypeStruct((n_out, width), src.dtype), mesh=_sc_mesh(), scratch_types=[])
    def scatter(x_hbm, i_hbm, o_hbm):
        def body(x_vmem, i_vmem):
            pltpu.sync_copy(x_vmem, o_hbm.at[i_vmem.at[0]])

        pltpu.emit_pipeline(
            body,
            grid=(n_lists * windows,),
            in_specs=[pl.BlockSpec((SC_WINDOW, width), lambda i: (i % windows, 0)),
                      pl.BlockSpec((1, SC_WINDOW), lambda i: (0, i))],
            out_specs=[],
            core_axis_name=("core", "subcore"),
            dimension_semantics=(pltpu.PARALLEL,),
        )(x_hbm, i_hbm)

    return scatter(src, idx.reshape(1, n_lists * n))


def _sc_gather_rows(table, idx):
    n = idx.shape[0]
    width = table.shape[1]

    @pl.kernel(out_type=jax.ShapeDtypeStruct((n, width), table.dtype), mesh=_sc_mesh(), scratch_types=[])
    def gather(x_hbm, i_hbm, o_hbm):
        def body(i_vmem, o_vmem):
            pltpu.sync_copy(x_hbm.at[i_vmem.at[0]], o_vmem)

        pltpu.emit_pipeline(
            body,
            grid=(n // SC_WINDOW,),
            in_specs=[pl.BlockSpec((1, SC_WINDOW), lambda i: (0, i))],
            out_specs=[pl.BlockSpec((SC_WINDOW, width), lambda i: (i, 0))],
            core_axis_name=("core", "subcore"),
            dimension_semantics=(pltpu.PARALLEL,),
        )(i_hbm, o_hbm)

    return gather(table, idx.reshape(1, n))


def _moe_sparse(x1, mod, seq_len, lw, w_gate_e, w_up_e, w_down_e, layer, *, tm, tm_e):
    t, d = x1.shape
    n_experts = lw["b_router"].shape[1]
    assert n_experts * TOP_K == LANE
    n_tok_tiles = t // tm
    tp, sel, wgt, rank, cnt = _route(x1, mod, seq_len, lw, tm=tm)
    rows = tp.shape[0]
    sel = sel[:, :TOP_K]
    rank = rank[:, :TOP_K]
    cnt = cnt[:, 0, :].reshape(n_tok_tiles, TOP_K, n_experts).sum(axis=1)
    total = cnt.sum(axis=0)
    padded = (total + tm_e - 1) // tm_e * tm_e
    group_end = jnp.cumsum(padded)
    base = (group_end - padded)[None, :] + jnp.cumsum(cnt, axis=0) - cnt
    base_t = jnp.broadcast_to(base[:, None, :], (n_tok_tiles, tm, n_experts)).reshape(t, n_experts)
    pos = jnp.take_along_axis(base_t, sel, axis=1) + rank

    n_row_tiles = (t * TOP_K) // tm_e + n_experts
    p = n_row_tiles * tm_e
    n_used = group_end[-1] // tm_e
    tile_start = jnp.arange(n_row_tiles, dtype=jnp.int32) * tm_e
    tile_expert = jnp.sum(tile_start[:, None] >= group_end[None, :], axis=1).astype(jnp.int32)
    tile_expert = jnp.minimum(tile_expert, n_experts - 1)
    tile_expert = jnp.where(tile_start < group_end[-1], tile_expert, tile_expert[n_used - 1])
    meta = jnp.stack([n_used, n_used]).astype(jnp.int32)

    idx = pos.T[:, None, :] + (jnp.arange(rows, dtype=jnp.int32) * p)[None, :, None]
    xs = _sc_scatter_rows(tp.reshape(rows * t, PACK_W), idx.reshape(TOP_K, rows * t), rows * p)
    ys = _experts(xs.reshape(rows, p, PACK_W), tile_expert, meta, w_gate_e, w_up_e, w_down_e, layer, tm=tm_e)
    yk = _sc_gather_rows(ys.reshape(rows * p, PACK_W), idx.reshape(-1))
    return _combine(x1, mod, seq_len, lw, yk.reshape(TOP_K, rows, t, PACK_W), wgt, tm=tm)


def _rope_tables(n):
    rows = n // GRID_W
    r, col = jnp.meshgrid(jnp.arange(rows), jnp.arange(GRID_W), indexing="ij")
    r = r.reshape(-1).astype(F32)
    col = col.reshape(-1).astype(F32)
    pairs = QK_ROPE // 4
    inv = ROPE_BASE ** (-jnp.arange(pairs, dtype=F32) / pairs)
    ang = jnp.concatenate([r[:, None] * inv, col[:, None] * inv], axis=-1)
    cos, sin = jnp.cos(ang), jnp.sin(ang)
    pad = HEAD_PAD - QK_NOPE - QK_ROPE
    cos_t = jnp.concatenate([jnp.ones((n, QK_NOPE), F32), cos, cos, jnp.zeros((n, pad), F32)], axis=1)
    sin_t = jnp.concatenate([jnp.zeros((n, QK_NOPE), F32), sin, sin, jnp.zeros((n, pad), F32)], axis=1)
    return cos_t, sin_t


def _identity_tables(n):
    pad = HEAD_PAD - QK_NOPE - QK_ROPE
    cos_t = jnp.concatenate([jnp.ones((n, QK_NOPE + QK_ROPE), F32), jnp.zeros((n, pad), F32)], axis=1)
    return cos_t, jnp.zeros((n, HEAD_PAD), F32)


def _position_dft(n):
    idx = (jnp.arange(n, dtype=jnp.int32)[:, None] * jnp.arange(n, dtype=jnp.int32)[None, :]) % n
    ang = idx.astype(F32) * (2.0 * np.pi / n)
    norm = 1.0 / np.sqrt(n)
    return jnp.concatenate([jnp.cos(ang) * norm, jnp.sin(ang) * (-norm)], axis=1).astype(BF16)


def _channel_dft(width):
    gc = width // FOURIER_GROUPS
    idx = (jnp.arange(gc, dtype=jnp.int32)[:, None] * jnp.arange(gc, dtype=jnp.int32)[None, :]) % gc
    ang = idx.astype(F32) * (2.0 * np.pi / gc)
    eye = jnp.eye(FOURIER_GROUPS, dtype=F32)
    norm = 1.0 / np.sqrt(gc)
    return jnp.concatenate([jnp.kron(eye, jnp.cos(ang) * norm), jnp.kron(eye, jnp.sin(ang) * norm)],
                           axis=1).astype(BF16)


def _pad_cols(w, width):
    return jnp.pad(w, ((0, 0), (0, width - w.shape[1])))


def _rot_cols(w):
    half = w.shape[-1] // 2
    return jnp.concatenate([-w[..., half:], w[..., :half]], axis=-1)


def _layer_weights(l, g_pre_mix, g_post_mix, g_pre_ffn, g_post_ffn, w_in, b_gate, g_q, w_uq, g_kv,
                   w_ukv, w_mla_out, conv_w, w_conv_out, w_four_out, w_out, w_router, b_router,
                   w_gate_s, w_up_s, w_down_s, dc):
    d = w_in.shape[1]
    kv_rank = g_kv.shape[1]
    q_rank = g_q.shape[1]
    w = w_in[l]
    kv_end = kv_rank + QK_ROPE
    zeros_nope = jnp.zeros((d, QK_NOPE), F32)
    kpe = w[:, kv_rank:kv_end]
    w1 = jnp.concatenate([
        w[:, :kv_rank],
        _pad_cols(jnp.concatenate([zeros_nope, kpe], axis=1), HEAD_PAD),
        _pad_cols(jnp.concatenate([zeros_nope, _rot_cols(kpe)], axis=1), HEAD_PAD),
        w[:, kv_end:],
    ], axis=1).astype(BF16)

    qk_dim = QK_NOPE + QK_ROPE
    pad = HEAD_PAD - qk_dim
    uq = w_uq[l].reshape(q_rank, N_HEADS, qk_dim) * (qk_dim ** -0.5 * np.log2(np.e))
    zq = jnp.zeros((q_rank, N_HEADS, pad), F32)
    wuq = jnp.concatenate([uq, zq], axis=-1).reshape(q_rank, N_HEADS * HEAD_PAD).astype(BF16)
    wuq_rot = jnp.concatenate([jnp.zeros((q_rank, N_HEADS, QK_NOPE), F32), _rot_cols(uq[..., QK_NOPE:]), zq],
                              axis=-1).reshape(q_rank, N_HEADS * HEAD_PAD).astype(BF16)
    ukv = w_ukv[l].reshape(kv_rank, N_HEADS, QK_NOPE + V_DIM)
    wuk = jnp.concatenate([ukv[..., :QK_NOPE], jnp.zeros((kv_rank, N_HEADS, HEAD_PAD - QK_NOPE), F32)],
                          axis=-1).reshape(kv_rank, N_HEADS * HEAD_PAD).astype(BF16)
    wuv = jnp.concatenate([ukv[..., QK_NOPE:], jnp.zeros((kv_rank, N_HEADS, HEAD_PAD - V_DIM), F32)],
                          axis=-1).reshape(kv_rank, N_HEADS * HEAD_PAD).astype(BF16)
    v_one = jnp.tile((jnp.arange(HEAD_PAD) == V_DIM).astype(F32), N_HEADS)[None]

    wr = w_router[l]
    wr_hi = wr.astype(BF16)
    wr_lo = (wr - wr_hi.astype(F32)).astype(BF16)
    return {
        "g_pre_mix": g_pre_mix[l][None], "g_post_mix": g_post_mix[l][None],
        "g_pre_ffn": g_pre_ffn[l][None], "g_post_ffn": g_post_ffn[l][None],
        "w1": w1, "b_gate": b_gate[l][None], "g_q": g_q[l][None], "g_kv": g_kv[l][None],
        "wuq": wuq, "wuq_rot": wuq_rot, "wuk": wuk, "wuv": wuv, "v_one": v_one, "dc": dc,
        "conv_w": conv_w[l],
        "w_mla_out": w_mla_out[l].astype(BF16), "w_conv_out": w_conv_out[l].astype(BF16),
        "w_four_out": w_four_out[l].astype(BF16), "w_out": w_out[l].astype(BF16),
        "w_router": jnp.concatenate([wr_hi, wr_lo], axis=1), "b_router": b_router[l][None],
        "w_gate_s": w_gate_s[l].astype(BF16), "w_up_s": w_up_s[l].astype(BF16),
        "w_down_s": w_down_s[l].astype(BF16),
    }


def _tile(n, pref):
    return pref if n % pref == 0 else n


def kernel(x, c, ctx, c_ctx, w_ada, b_ada, g_pre_mix, g_post_mix, g_pre_ffn, g_post_ffn, w_in, b_gate,
           g_q, w_uq, g_kv, w_ukv, w_mla_out, conv_w, w_conv_out, w_four_out, w_out, w_router, b_router,
           w_gate_e, w_up_e, w_down_e, w_gate_s, w_up_s, w_down_s):
    batch, seq, d = x.shape
    n_ctx = ctx.shape[1]
    n_layers = w_in.shape[0]
    xs = x.reshape(batch * seq, d)
    cs = ctx.reshape(batch * n_ctx, d)

    mod_rows = 16
    c_all = jnp.concatenate([c, c_ctx[None], jnp.zeros((mod_rows - batch - 1, d), F32)], axis=0)
    ada = _ada(c_all, w_ada, b_ada)

    tab_x = _rope_tables(seq)
    tm_c = _tile(n_ctx, 256)
    tab_c = _identity_tables(tm_c)
    cs_x = _position_dft(seq)
    cs_c = _position_dft(n_ctx)
    dc = _channel_dft(w_four_out.shape[1])

    tm_x = _tile(seq, 512)
    tm_moe_x = _tile(seq, 1024)
    tm_moe_c = _tile(batch * n_ctx, 1024)

    for l in range(n_layers):
        last = l == n_layers - 1
        lw = _layer_weights(l, g_pre_mix, g_post_mix, g_pre_ffn, g_post_ffn, w_in, b_gate, g_q, w_uq,
                            g_kv, w_ukv, w_mla_out, conv_w, w_conv_out, w_four_out, w_out, w_router,
                            b_router, w_gate_s, w_up_s, w_down_s, dc)
        mods = ada[l].reshape(mod_rows, 6, d)
        mod_x = mods[:batch]
        mod_c = mods[batch:batch + 1]

        pc = _inproj(cs, mod_c, n_ctx, lw, tab_c, kv_only=last, tm=tm_c)
        px = _inproj(xs, mod_x, seq, lw, tab_x, kv_only=False, tm=tm_x)
        o_x = _attention(px["q"], [(pc["k"], pc["v"], n_ctx), (px["k"], px["v"], seq)], batch, seq, tq=tm_x)
        f_x = _fourier(px["ab"], cs_x, batch, seq, tn=tm_x)
        x1 = _merge(xs, mod_x, seq, px, o_x, f_x, lw, tm=tm_x)
        xs = _moe_sparse(x1, mod_x, seq, lw, w_gate_e, w_up_e, w_down_e, l, tm=tm_moe_x, tm_e=512)
        if not last:
            o_c = _attention(pc["q"], [(pc["k"], pc["v"], n_ctx)], batch, n_ctx, tq=tm_c)
            f_c = _fourier(pc["ab"], cs_c, batch, n_ctx, tn=tm_c)
            c1 = _merge(cs, mod_c, n_ctx, pc, o_c, f_c, lw, tm=tm_c)
            cs = _moe(c1, mod_c, batch * n_ctx, lw, w_gate_e, w_up_e, w_down_e, l, tm=tm_moe_c)
    return xs.reshape(batch, seq, d)
```

```python
import functools

import numpy as np
import jax
import jax.numpy as jnp
from jax import lax
from jax.experimental import pallas as pl
from jax.experimental.pallas import tpu as pltpu
from jax.experimental.pallas import tpu_sc as plsc

N_HEADS = 8
QK_NOPE = 64
QK_ROPE = 32
V_DIM = 64
GRID_W = 64
ROPE_BASE = 10000.0
FOURIER_GROUPS = 4
TOP_K = 4
ROUTED_SCALE = 2.5
N_BRANCHES = 3
EPS = 1e-6

LANE = 128
HEAD_PAD = LANE
VMEM_LIMIT = 56 * 1024 * 1024
PACK_W = 256
SC_WINDOW = 128

F32 = jnp.float32
BF16 = jnp.bfloat16


def _rms(x, g):
    return x * lax.rsqrt(jnp.mean(x * x, axis=-1, keepdims=True) + EPS) * g


def _sigmoid(x):
    return 1.0 / (1.0 + jnp.exp(-x))


def _dot(a, b):
    return jnp.dot(a, b, preferred_element_type=F32)


def _resident(shape):
    nd = len(shape)
    return pl.BlockSpec(shape, lambda *_: (0,) * nd, pipeline_mode=pl.Buffered(1))


def _params(n_grid):
    return pltpu.CompilerParams(dimension_semantics=("arbitrary",) * n_grid,
                                vmem_limit_bytes=VMEM_LIMIT)


def _ada_kernel(c_ref, w_ref, b_ref, o_ref):
    c = c_ref[...]
    a = (c * _sigmoid(c)).astype(BF16)
    o_ref[0] = _dot(a, w_ref[0].astype(BF16)) + b_ref[0]


def _ada(c_all, w_ada, b_ada):
    n_layers, d, n_out = w_ada.shape
    rows = c_all.shape[0]
    tn = 1536
    return pl.pallas_call(
        _ada_kernel,
        grid=(n_layers, n_out // tn),
        in_specs=[
            pl.BlockSpec((rows, d), lambda l, j: (0, 0)),
            pl.BlockSpec((1, d, tn), lambda l, j: (l, 0, j)),
            pl.BlockSpec((1, 1, tn), lambda l, j: (l, 0, j)),
        ],
        out_specs=pl.BlockSpec((1, rows, tn), lambda l, j: (l, 0, j)),
        out_shape=jax.ShapeDtypeStruct((n_layers, rows, n_out), F32),
        compiler_params=_params(2),
        name="ada",
    )(c_all, w_ada, b_ada.reshape(n_layers, 1, n_out))


def _inproj_kernel(*refs, kv_only, kv_rank, q_rank, conv_w, four_w, d_model):
    if kv_only:
        (x_ref, mod_ref, gpre_ref, w1_ref, gkv_ref, wuk_ref, wuv_ref, vone_ref, cos_ref, sin_ref,
         k_ref, v_ref) = refs
    else:
        (x_ref, mod_ref, gpre_ref, w1_ref, gkv_ref, wuk_ref, wuv_ref, vone_ref, cos_ref, sin_ref,
         bg_ref, gq_ref, wuq_ref, wuqr_ref, dc_ref,
         k_ref, v_ref, q_ref, cb_ref, cc_ref, cu_ref, ab_ref, gate_ref) = refs

    x = x_ref[...]
    shift = mod_ref[0, 0:1, :]
    scale = mod_ref[0, 1:2, :]
    h = (_rms(x, gpre_ref[...]) * (1.0 + scale) + shift).astype(BF16)
    cos = cos_ref[...]
    sin = sin_ref[...]

    o_kpe = kv_rank
    o_rot = o_kpe + HEAD_PAD
    o_q = o_rot + HEAD_PAD
    p = _dot(h, w1_ref[:, 0:o_q])
    ckv = _rms(p[:, 0:kv_rank], gkv_ref[...]).astype(BF16)
    kpe = p[:, o_kpe:o_rot] * cos + p[:, o_rot:o_q] * sin
    k = _dot(ckv, wuk_ref[...]) + jnp.concatenate([kpe] * N_HEADS, axis=1)
    k_ref[...] = k.astype(k_ref.dtype)
    v_ref[...] = (_dot(ckv, wuv_ref[...]) + vone_ref[...]).astype(v_ref.dtype)
    if kv_only:
        return

    o_cb = o_q + q_rank
    cq = _rms(_dot(h, w1_ref[:, o_q:o_cb]), gq_ref[...]).astype(BF16)
    cos_h = jnp.concatenate([cos] * N_HEADS, axis=1)
    sin_h = jnp.concatenate([sin] * N_HEADS, axis=1)
    q = _dot(cq, wuq_ref[...]) * cos_h + _dot(cq, wuqr_ref[...]) * sin_h
    q_ref[...] = q.astype(q_ref.dtype)

    o_cc = o_cb + conv_w
    o_cu = o_cc + conv_w
    o_four = o_cu + conv_w
    cb_ref[...] = _dot(h, w1_ref[:, o_cb:o_cc]).astype(cb_ref.dtype)
    cc_ref[...] = _dot(h, w1_ref[:, o_cc:o_cu]).astype(cc_ref.dtype)
    cu_ref[...] = _dot(h, w1_ref[:, o_cu:o_four]).astype(cu_ref.dtype)

    o_gate = o_four + four_w
    uf = _dot(h, w1_ref[:, o_four:o_gate]).astype(BF16)
    ab_ref[...] = _dot(uf, dc_ref[...]).astype(ab_ref.dtype)

    for j in range(N_BRANCHES):
        lo = o_gate + j * d_model
        z = _dot(h, w1_ref[:, lo:lo + d_model]) + bg_ref[:, j * d_model:(j + 1) * d_model]
        gate_ref[:, j * d_model:(j + 1) * d_model] = _sigmoid(z).astype(gate_ref.dtype)


def _inproj(xs, mod, seq_len, lw, tables, *, kv_only, tm):
    t, d = xs.shape
    nb = mod.shape[0]
    tiles_per_seq = seq_len // tm
    cos_t, sin_t = tables
    table_tiles = cos_t.shape[0] // tm
    kv_rank = lw["g_kv"].shape[1]
    q_rank = lw["g_q"].shape[1]
    conv_w = lw["conv_w"].shape[1]
    four_w = lw["dc"].shape[0]
    n_k = N_HEADS * HEAD_PAD

    def row(i):
        return (i, 0)

    def mod_map(i):
        return ((i // tiles_per_seq) % nb, 0, 0)

    def tab_map(i):
        return (i % table_tiles, 0)

    w1 = lw["w1"]
    if kv_only:
        w1 = w1[:, :kv_rank + 2 * HEAD_PAD]
    in_specs = [
        pl.BlockSpec((tm, d), row),
        pl.BlockSpec((1,) + mod.shape[1:], mod_map),
        _resident((1, d)),
        _resident(w1.shape),
        _resident((1, kv_rank)),
        _resident(lw["wuk"].shape),
        _resident(lw["wuv"].shape),
        _resident(lw["v_one"].shape),
        pl.BlockSpec((tm, HEAD_PAD), tab_map),
        pl.BlockSpec((tm, HEAD_PAD), tab_map),
    ]
    args = [xs, mod, lw["g_pre_mix"], w1, lw["g_kv"], lw["wuk"], lw["wuv"], lw["v_one"], cos_t, sin_t]
    out_shape = [jax.ShapeDtypeStruct((t, n_k), BF16), jax.ShapeDtypeStruct((t, n_k), BF16)]
    out_specs = [pl.BlockSpec((tm, n_k), row), pl.BlockSpec((tm, n_k), row)]
    if not kv_only:
        in_specs += [
            _resident(lw["b_gate"].shape),
            _resident((1, q_rank)),
            _resident(lw["wuq"].shape),
            _resident(lw["wuq_rot"].shape),
            _resident(lw["dc"].shape),
        ]
        args += [lw["b_gate"], lw["g_q"], lw["wuq"], lw["wuq_rot"], lw["dc"]]
        widths = [n_k, conv_w, conv_w, conv_w, 2 * four_w, N_BRANCHES * d]
        out_shape += [jax.ShapeDtypeStruct((t, w), BF16) for w in widths]
        out_specs += [pl.BlockSpec((tm, w), row) for w in widths]
    outs = pl.pallas_call(
        functools.partial(_inproj_kernel, kv_only=kv_only, kv_rank=kv_rank, q_rank=q_rank,
                          conv_w=conv_w, four_w=four_w, d_model=d),
        grid=(t // tm,),
        in_specs=in_specs,
        out_specs=out_specs,
        out_shape=out_shape,
        compiler_params=_params(1),
        name="inproj_kv" if kv_only else "inproj",
    )(*args)
    names = ["k", "v", "q", "cb", "cc", "cu", "ab", "gate"]
    return dict(zip(names, outs))


def _attn_kernel(*refs, n_seg):
    q_ref = refs[0]
    o_ref = refs[-1]
    nt = (((1,), (1,)), ((), ()))
    outs = []
    for hh in range(N_HEADS):
        head = slice(hh * HEAD_PAD, (hh + 1) * HEAD_PAD)
        qh = q_ref[:, head]
        s = [lax.dot_general(qh, refs[1 + 2 * i][:, head], nt, preferred_element_type=F32)
             for i in range(n_seg)]
        m = functools.reduce(jnp.maximum, [jnp.max(si, axis=-1, keepdims=True) for si in s])
        acc = functools.reduce(jnp.add, [
            _dot(jnp.exp2((s[i] - m).astype(BF16)), refs[2 + 2 * i][:, head]) for i in range(n_seg)])
        outs.append(acc[:, 0:V_DIM] / acc[:, V_DIM:V_DIM + 1])
    o_ref[...] = jnp.concatenate(outs, axis=1).astype(o_ref.dtype)


def _attention(q, segs, batch, seq_q, *, tq):
    t = q.shape[0]
    qt = seq_q // tq
    n_k = N_HEADS * HEAD_PAD
    in_specs = [pl.BlockSpec((tq, n_k), lambda b, j: (b * qt + j, 0))]
    args = [q]
    for k, v, m in segs:
        in_specs.append(pl.BlockSpec((m, n_k), lambda b, j: (b, 0)))
        in_specs.append(pl.BlockSpec((m, n_k), lambda b, j: (b, 0)))
        args += [k, v]
    return pl.pallas_call(
        functools.partial(_attn_kernel, n_seg=len(segs)),
        grid=(batch, qt),
        in_specs=in_specs,
        out_specs=pl.BlockSpec((tq, N_HEADS * V_DIM), lambda b, j: (b * qt + j, 0)),
        out_shape=jax.ShapeDtypeStruct((t, N_HEADS * V_DIM), BF16),
        compiler_params=_params(2),
        name="attention",
    )(*args)


def _four_kernel(cs_ref, ab_ref, o_ref, *, n, fw):
    o = _dot(cs_ref[:, 0:n], ab_ref[:, 0:fw]) + _dot(cs_ref[:, n:2 * n], ab_ref[:, fw:2 * fw])
    o_ref[...] = o.astype(o_ref.dtype)


def _fourier(ab, cs, batch, seq_len, *, tn):
    t, fw2 = ab.shape
    fw = fw2 // 2
    nt = seq_len // tn
    return pl.pallas_call(
        functools.partial(_four_kernel, n=seq_len, fw=fw),
        grid=(batch, nt),
        in_specs=[
            pl.BlockSpec((tn, 2 * seq_len), lambda b, j: (j, 0)),
            pl.BlockSpec((seq_len, fw2), lambda b, j: (b, 0)),
        ],
        out_specs=pl.BlockSpec((tn, fw), lambda b, j: (b * nt + j, 0)),
        out_shape=jax.ShapeDtypeStruct((t, fw), BF16),
        compiler_params=_params(2),
        name="fourier",
    )(cs, ab)


def _merge_kernel(x_ref, mod_ref, o_ref, cb_ref, cc_ref, cu_ref, ccp_ref, cup_ref, ccn_ref,
                  cun_ref, f_ref, gate_ref, convw_ref, wmo_ref, wco_ref, wfo_ref, wout_ref,
                  gpost_ref, out_ref, pad_ref, *, tiles_per_seq, tm, d_model):
    i = pl.program_id(0)
    pos = i % tiles_per_seq
    has_prev = (pos > 0).astype(F32)
    has_next = (pos < tiles_per_seq - 1).astype(F32)
    pad_ref[0:8, :] = ccp_ref[...].astype(F32) * cup_ref[...].astype(F32) * has_prev
    pad_ref[8:8 + tm, :] = cc_ref[...].astype(F32) * cu_ref[...].astype(F32)
    pad_ref[8 + tm:16 + tm, :] = ccn_ref[...].astype(F32) * cun_ref[...].astype(F32) * has_next
    conv = (pad_ref[7:7 + tm, :] * convw_ref[0:1, :] + pad_ref[8:8 + tm, :] * convw_ref[1:2, :]
            + pad_ref[9:9 + tm, :] * convw_ref[2:3, :])
    y_conv = _dot((cb_ref[...].astype(F32) * conv).astype(BF16), wco_ref[...])
    y_attn = _dot(o_ref[...], wmo_ref[...])
    y_four = _dot(f_ref[...], wfo_ref[...])
    d = d_model
    merged = (gate_ref[:, 0:d].astype(F32) * y_attn + gate_ref[:, d:2 * d].astype(F32) * y_conv
              + gate_ref[:, 2 * d:3 * d].astype(F32) * y_four)
    y = _dot(merged.astype(BF16), wout_ref[...])
    g1 = mod_ref[0, 2:3, :]
    out_ref[...] = x_ref[...] + g1 * _rms(y, gpost_ref[...])


def _merge(xs, mod, seq_len, pr, o, four, lw, *, tm):
    t, d = xs.shape
    nb = mod.shape[0]
    tiles_per_seq = seq_len // tm
    cw = lw["conv_w"].shape[1]
    fw = four.shape[1]
    hb = tm // 8
    last_hb = t // 8 - 1

    def row(i):
        return (i, 0)

    def prev_map(i):
        return (jnp.maximum(i * hb - 1, 0), 0)

    def next_map(i):
        return (jnp.minimum((i + 1) * hb, last_hb), 0)

    in_specs = [
        pl.BlockSpec((tm, d), row),
        pl.BlockSpec((1,) + mod.shape[1:], lambda i: ((i // tiles_per_seq) % nb, 0, 0)),
        pl.BlockSpec((tm, o.shape[1]), row),
        pl.BlockSpec((tm, cw), row),
        pl.BlockSpec((tm, cw), row),
        pl.BlockSpec((tm, cw), row),
        pl.BlockSpec((8, cw), prev_map),
        pl.BlockSpec((8, cw), prev_map),
        pl.BlockSpec((8, cw), next_map),
        pl.BlockSpec((8, cw), next_map),
        pl.BlockSpec((tm, fw), row),
        pl.BlockSpec((tm, N_BRANCHES * d), row),
        _resident(lw["conv_w"].shape),
        _resident(lw["w_mla_out"].shape),
        _resident(lw["w_conv_out"].shape),
        _resident(lw["w_four_out"].shape),
        _resident(lw["w_out"].shape),
        _resident((1, d)),
    ]
    return pl.pallas_call(
        functools.partial(_merge_kernel, tiles_per_seq=tiles_per_seq, tm=tm, d_model=d),
        grid=(t // tm,),
        in_specs=in_specs,
        out_specs=pl.BlockSpec((tm, d), row),
        out_shape=jax.ShapeDtypeStruct((t, d), F32),
        scratch_shapes=[pltpu.VMEM((tm + 16, cw), F32)],
        compiler_params=_params(1),
        name="merge",
    )(xs, mod, o, pr["cb"], pr["cc"], pr["cu"], pr["cc"], pr["cu"], pr["cc"], pr["cu"], four,
      pr["gate"], lw["conv_w"], lw["w_mla_out"], lw["w_conv_out"], lw["w_four_out"], lw["w_out"],
      lw["g_post_mix"])


def _moe_kernel(x_ref, mod_ref, gpre_ref, gpost_ref, wr_ref, br_ref, wgs_ref, wus_ref, wds_ref,
                wge_ref, wue_ref, wde_ref, out_ref, t_ref, comb_ref, acc_ref, *, n_experts):
    e = pl.program_id(1)

    @pl.when(e == 0)
    def _():
        shift = mod_ref[0, 3:4, :]
        scale = mod_ref[0, 4:5, :]
        t = _rms(x_ref[...], gpre_ref[...]) * (1.0 + scale) + shift
        t_hi = t.astype(BF16)
        t_lo = (t - t_hi.astype(F32)).astype(BF16)
        t_ref[...] = t_hi
        hh = _dot(t_hi, wr_ref[...])
        logits = hh[:, 0:n_experts] + hh[:, n_experts:2 * n_experts] + _dot(t_lo, wr_ref[:, 0:n_experts])
        scores = _sigmoid(logits)
        work = scores + br_ref[...]
        lane = lax.broadcasted_iota(jnp.int32, scores.shape, 1)
        comb = jnp.zeros_like(scores)
        for _ in range(TOP_K):
            best = jnp.max(work, axis=-1, keepdims=True)
            first = jnp.min(jnp.where(work == best, lane, n_experts), axis=-1, keepdims=True)
            hit = lane == first
            comb = jnp.where(hit, scores, comb)
            work = jnp.where(hit, -jnp.inf, work)
        comb_ref[...] = comb / jnp.sum(comb, axis=-1, keepdims=True) * ROUTED_SCALE
        gate = _dot(t_hi, wgs_ref[...])
        act = (gate * _sigmoid(gate) * _dot(t_hi, wus_ref[...])).astype(BF16)
        acc_ref[...] = _dot(act, wds_ref[...])

    t_hi = t_ref[...]
    gate = _dot(t_hi, wge_ref[0, 0].astype(BF16))
    up = _dot(t_hi, wue_ref[0, 0].astype(BF16))
    lane = lax.broadcasted_iota(jnp.int32, comb_ref.shape, 1)
    w_e = jnp.sum(jnp.where(lane == e, comb_ref[...], 0.0), axis=-1, keepdims=True)
    act = (gate * _sigmoid(gate) * up * w_e).astype(BF16)
    acc_ref[...] += _dot(act, wde_ref[0, 0].astype(BF16))

    @pl.when(e == n_experts - 1)
    def _():
        g2 = mod_ref[0, 5:6, :]
        out_ref[...] = x_ref[...] + g2 * _rms(acc_ref[...], gpost_ref[...])


def _moe(xs, mod, seq_len, lw, w_gate_e, w_up_e, w_down_e, layer, *, tm):
    t, d = xs.shape
    nb = mod.shape[0]
    tiles_per_seq = max(seq_len // tm, 1)
    n_experts, _, f = w_gate_e.shape[1:]
    in_specs = [
        pl.BlockSpec((tm, d), lambda i, e: (i, 0)),
        pl.BlockSpec((1,) + mod.shape[1:], lambda i, e: ((i // tiles_per_seq) % nb, 0, 0)),
        _resident((1, d)),
        _resident((1, d)),
        _resident(lw["w_router"].shape),
        _resident((1, n_experts)),
        _resident(lw["w_gate_s"].shape),
        _resident(lw["w_up_s"].shape),
        _resident(lw["w_down_s"].shape),
        pl.BlockSpec((1, 1, d, f), lambda i, e: (layer, e, 0, 0)),
        pl.BlockSpec((1, 1, d, f), lambda i, e: (layer, e, 0, 0)),
        pl.BlockSpec((1, 1, f, d), lambda i, e: (layer, e, 0, 0)),
    ]
    return pl.pallas_call(
        functools.partial(_moe_kernel, n_experts=n_experts),
        grid=(t // tm, n_experts),
        in_specs=in_specs,
        out_specs=pl.BlockSpec((tm, d), lambda i, e: (i, 0)),
        out_shape=jax.ShapeDtypeStruct((t, d), F32),
        scratch_shapes=[pltpu.VMEM((tm, d), BF16), pltpu.VMEM((tm, n_experts), F32),
                        pltpu.VMEM((tm, d), F32)],
        compiler_params=_params(2),
        name="moe",
    )(xs, mod, lw["g_pre_ffn"], lw["g_post_ffn"], lw["w_router"], lw["b_router"], lw["w_gate_s"],
      lw["w_up_s"], lw["w_down_s"], w_gate_e, w_up_e, w_down_e)


def _pack_rows(v):
    bits = lax.bitcast_convert_type(v.astype(BF16).astype(F32), jnp.uint32)
    rows = []
    for j in range(v.shape[1] // (2 * PACK_W)):
        lo = bits[:, (2 * j) * PACK_W:(2 * j + 1) * PACK_W]
        hi = bits[:, (2 * j + 1) * PACK_W:(2 * j + 2) * PACK_W]
        rows.append(lax.bitcast_convert_type((hi & jnp.uint32(0xFFFF0000)) | (lo >> 16), jnp.int32))
    return rows


def _unpack_rows(rows):
    parts = []
    for r in rows:
        u = lax.bitcast_convert_type(r, jnp.uint32)
        parts.append(lax.bitcast_convert_type(u << 16, F32))
        parts.append(lax.bitcast_convert_type(u & jnp.uint32(0xFFFF0000), F32))
    return jnp.concatenate(parts, axis=1)


def _route_kernel(x_ref, mod_ref, gpre_ref, wr_ref, br_ref, tri_ref,
                  tp_ref, sel_ref, wgt_ref, rank_ref, cnt_ref, *, n_experts):
    shift = mod_ref[0, 3:4, :]
    scale = mod_ref[0, 4:5, :]
    t = _rms(x_ref[...], gpre_ref[...]) * (1.0 + scale) + shift
    t_hi = t.astype(BF16)
    t_lo = (t - t_hi.astype(F32)).astype(BF16)
    for j, r in enumerate(_pack_rows(t)):
        tp_ref[j] = r
    hh = _dot(t_hi, wr_ref[...])
    logits = hh[:, 0:n_experts] + hh[:, n_experts:2 * n_experts] + _dot(t_lo, wr_ref[:, 0:n_experts])
    scores = _sigmoid(logits)
    work = scores + br_ref[...]
    lane = lax.broadcasted_iota(jnp.int32, scores.shape, 1)
    wide = lax.broadcasted_iota(jnp.int32, (scores.shape[0], LANE), 1)
    firsts, picked, hits = [], [], []
    for k in range(TOP_K):
        best = jnp.max(work, axis=-1, keepdims=True)
        first = jnp.min(jnp.where(work == best, lane, n_experts), axis=-1, keepdims=True)
        hit = lane == first
        firsts.append(first)
        picked.append(jnp.sum(jnp.where(hit, scores, 0.0), axis=-1, keepdims=True))
        hits.append(wide == first + k * n_experts)
        work = jnp.where(hit, -jnp.inf, work)
    total = functools.reduce(jnp.add, picked)
    onehot = functools.reduce(jnp.add, [jnp.where(h, 1.0, 0.0) for h in hits])
    earlier = _dot(tri_ref[...], onehot.astype(BF16))
    col = jnp.sum(onehot, axis=0, keepdims=True)
    col8 = jnp.broadcast_to(col, (8, LANE))
    lane8 = lax.broadcasted_iota(jnp.int32, (8, LANE), 1)
    before = jnp.zeros((8, LANE), F32)
    for s in range(1, TOP_K):
        before = before + jnp.where(lane8 >= s * n_experts, pltpu.roll(col8, s * n_experts, 1), 0.0)
    ahead = earlier + before[0:1, :]
    sel = jnp.zeros(wide.shape, jnp.int32)
    wgt = jnp.zeros(wide.shape, F32)
    rank = jnp.zeros(wide.shape, F32)
    for k in range(TOP_K):
        sel = jnp.where(wide == k, firsts[k], sel)
        wgt = jnp.where(wide == k, picked[k] / total * ROUTED_SCALE, wgt)
        rank = jnp.where(wide == k, jnp.sum(jnp.where(hits[k], ahead, 0.0), axis=-1, keepdims=True), rank)
    sel_ref[...] = sel
    wgt_ref[...] = wgt
    rank_ref[...] = rank.astype(jnp.int32)
    cnt_ref[0] = col8.astype(jnp.int32)


def _route(x1, mod, seq_len, lw, *, tm, tile_off, n_tiles):
    d = x1.shape[1]
    t = n_tiles * tm
    nb = mod.shape[0]
    tiles_per_seq = max(seq_len // tm, 1)
    n_experts = lw["b_router"].shape[1]
    rows = d // (2 * PACK_W)
    tri = jnp.tri(tm, tm, -1, dtype=BF16)
    row = lambda i: (i, 0)
    return pl.pallas_call(
        functools.partial(_route_kernel, n_experts=n_experts),
        grid=(n_tiles,),
        in_specs=[
            pl.BlockSpec((tm, d), lambda i: (i + tile_off, 0)),
            pl.BlockSpec((1,) + mod.shape[1:], lambda i: (((i + tile_off) // tiles_per_seq) % nb, 0, 0)),
            _resident((1, d)),
            _resident(lw["w_router"].shape),
            _resident((1, n_experts)),
            _resident((tm, tm)),
        ],
        out_specs=[
            pl.BlockSpec((rows, tm, PACK_W), lambda i: (0, i, 0)),
            pl.BlockSpec((tm, LANE), row),
            pl.BlockSpec((tm, LANE), row),
            pl.BlockSpec((tm, LANE), row),
            pl.BlockSpec((1, 8, LANE), lambda i: (i, 0, 0)),
        ],
        out_shape=[
            jax.ShapeDtypeStruct((rows, t, PACK_W), jnp.int32),
            jax.ShapeDtypeStruct((t, LANE), jnp.int32),
            jax.ShapeDtypeStruct((t, LANE), F32),
            jax.ShapeDtypeStruct((t, LANE), jnp.int32),
            jax.ShapeDtypeStruct((n_tiles, 8, LANE), jnp.int32),
        ],
        compiler_params=_params(1),
        name="moe_route",
    )(x1, mod, lw["g_pre_ffn"], lw["w_router"], lw["b_router"], tri)


def _expert_kernel(te_ref, meta_ref, xs_ref, wg_ref, wu_ref, wd_ref, ys_ref, wg_sc, wu_sc, wd_sc):
    i = pl.program_id(0)
    live = i < meta_ref[0]
    new_expert = (i == 0) | (te_ref[i] != te_ref[jnp.maximum(i - 1, 0)])

    @pl.when(live & new_expert)
    def _():
        wg_sc[...] = wg_ref[0, 0].astype(BF16)
        wu_sc[...] = wu_ref[0, 0].astype(BF16)
        wd_sc[...] = wd_ref[0, 0].astype(BF16)

    @pl.when(live)
    def _():
        x = _unpack_rows([xs_ref[j] for j in range(xs_ref.shape[0])]).astype(BF16)
        gate = _dot(x, wg_sc[...])
        act = (gate * _sigmoid(gate) * _dot(x, wu_sc[...])).astype(BF16)
        for j, r in enumerate(_pack_rows(_dot(act, wd_sc[...]))):
            ys_ref[j] = r


def _experts(xs, tile_expert, meta, w_gate_e, w_up_e, w_down_e, layer, *, tm):
    rows, p, _ = xs.shape
    n_experts, d, f = w_gate_e.shape[1:]

    def slot(i, te, meta):
        return (0, jnp.minimum(i, meta[0] - 1), 0)

    grid_spec = pltpu.PrefetchScalarGridSpec(
        num_scalar_prefetch=2,
        grid=(p // tm,),
        in_specs=[
            pl.BlockSpec((rows, tm, PACK_W), slot),
            pl.BlockSpec((1, 1, d, f), lambda i, te, meta: (layer, te[i], 0, 0)),
            pl.BlockSpec((1, 1, d, f), lambda i, te, meta: (layer, te[i], 0, 0)),
            pl.BlockSpec((1, 1, f, d), lambda i, te, meta: (layer, te[i], 0, 0)),
        ],
        out_specs=pl.BlockSpec((rows, tm, PACK_W), slot),
        scratch_shapes=[pltpu.VMEM((d, f), BF16), pltpu.VMEM((d, f), BF16), pltpu.VMEM((f, d), BF16)],
    )
    return pl.pallas_call(
        _expert_kernel,
        grid_spec=grid_spec,
        out_shape=jax.ShapeDtypeStruct(xs.shape, jnp.int32),
        compiler_params=_params(1),
        name="moe_experts",
    )(tile_expert, meta, xs, w_gate_e, w_up_e, w_down_e)


def _combine_kernel(x_ref, mod_ref, gpre_ref, gpost_ref, wgs_ref, wus_ref, wds_ref, yk_ref, wgt_ref,
                    *rest):
    out_ref = rest[-1]
    shift = mod_ref[0, 3:4, :]
    scale = mod_ref[0, 4:5, :]
    x = x_ref[...]
    t_hi = (_rms(x, gpre_ref[...]) * (1.0 + scale) + shift).astype(BF16)
    gate = _dot(t_hi, wgs_ref[...])
    act = (gate * _sigmoid(gate) * _dot(t_hi, wus_ref[...])).astype(BF16)
    acc = _dot(act, wds_ref[...])
    for k in range(TOP_K):
        y = _unpack_rows([yk_ref[k, j] for j in range(yk_ref.shape[1])])
        acc = acc + wgt_ref[:, k:k + 1] * y
    g2 = mod_ref[0, 5:6, :]
    out_ref[...] = x + g2 * _rms(acc, gpost_ref[...])


def _combine(x1, mod, seq_len, lw, yk, wgt, *, tm, tile_off, prev):
    t, d = x1.shape
    nb = mod.shape[0]
    tiles_per_seq = max(seq_len // tm, 1)
    rows = yk.shape[1]
    n_tiles = yk.shape[2] // tm
    glob = lambda i: (i + tile_off, 0)
    in_specs = [
        pl.BlockSpec((tm, d), glob),
        pl.BlockSpec((1,) + mod.shape[1:], lambda i: (((i + tile_off) // tiles_per_seq) % nb, 0, 0)),
        _resident((1, d)),
        _resident((1, d)),
        _resident(lw["w_gate_s"].shape),
        _resident(lw["w_up_s"].shape),
        _resident(lw["w_down_s"].shape),
        pl.BlockSpec((TOP_K, rows, tm, PACK_W), lambda i: (0, 0, i, 0)),
        pl.BlockSpec((tm, LANE), lambda i: (i, 0)),
    ]
    args = [x1, mod, lw["g_pre_ffn"], lw["g_post_ffn"], lw["w_gate_s"], lw["w_up_s"], lw["w_down_s"], yk, wgt]
    aliases = {}
    if prev is not None:
        in_specs.append(pl.BlockSpec(memory_space=pl.ANY))
        aliases = {len(args): 0}
        args.append(prev)
    return pl.pallas_call(
        _combine_kernel,
        grid=(n_tiles,),
        in_specs=in_specs,
        out_specs=pl.BlockSpec((tm, d), glob),
        out_shape=jax.ShapeDtypeStruct((t, d), F32),
        input_output_aliases=aliases,
        compiler_params=_params(1),
        name="moe_combine",
    )(*args)


def _sc_mesh():
    return plsc.VectorSubcoreMesh(core_axis_name="core", subcore_axis_name="subcore")


def _sc_scatter_rows(src, idx, n_out):
    n_lists, n = idx.shape
    width = src.shape[1]
    windows = n // SC_WINDOW

    @pl.kernel(out_type=jax.ShapeDtypeStruct((n_out, width), src.dtype), mesh=_sc_mesh(), scratch_types=[])
    def scatter(x_hbm, i_hbm, o_hbm):
        def body(x_vmem, i_vmem):
            pltpu.sync_copy(x_vmem, o_hbm.at[i_vmem.at[0]])

        pltpu.emit_pipeline(
            body,
            grid=(n_lists * windows,),
            in_specs=[pl.BlockSpec((SC_WINDOW, width), lambda i: (i % windows, 0)),
                      pl.BlockSpec((1, SC_WINDOW), lambda i: (0, i))],
            out_specs=[],
            core_axis_name=("core", "subcore"),
            dimension_semantics=(pltpu.PARALLEL,),
        )(x_hbm, i_hbm)

    return scatter(src, idx.reshape(1, n_lists * n))


def _sc_gather_rows(table, idx):
    n = idx.shape[0]
    width = table.shape[1]

    @pl.kernel(out_type=jax.ShapeDtypeStruct((n, width), table.dtype), mesh=_sc_mesh(), scratch_types=[])
    def gather(x_hbm, i_hbm, o_hbm):
        def body(i_vmem, o_vmem):
            pltpu.sync_copy(x_hbm.at[i_vmem.at[0]], o_vmem)

        pltpu.emit_pipeline(
            body,
            grid=(n // SC_WINDOW,),
            in_specs=[pl.BlockSpec((1, SC_WINDOW), lambda i: (0, i))],
            out_specs=[pl.BlockSpec((SC_WINDOW, width), lambda i: (i, 0))],
            core_axis_name=("core", "subcore"),
            dimension_semantics=(pltpu.PARALLEL,),
        )(i_hbm, o_hbm)

    return gather(table, idx.reshape(1, n))


def _moe_sparse(x1, mod, seq_len, lw, w_gate_e, w_up_e, w_down_e, layer, *, tm, tm_e, n_parts):
    n_experts = lw["b_router"].shape[1]
    assert n_experts * TOP_K == LANE
    n_tok_tiles = x1.shape[0] // tm // n_parts
    t = n_tok_tiles * tm
    n_row_tiles = (t * TOP_K) // tm_e + n_experts
    p = n_row_tiles * tm_e
    experts = jnp.arange(n_experts, dtype=jnp.int32)

    staged = []
    for part in range(n_parts):
        tp, sel, wgt, rank, cnt = _route(x1, mod, seq_len, lw, tm=tm, tile_off=part * n_tok_tiles,
                                         n_tiles=n_tok_tiles)
        rows = tp.shape[0]
        cnt = cnt[:, 0, :].reshape(n_tok_tiles, TOP_K, n_experts).sum(axis=1)
        padded = (cnt.sum(axis=0) + tm_e - 1) // tm_e * tm_e
        group_end = jnp.cumsum(padded)
        base = (group_end - padded)[None, :] + jnp.cumsum(cnt, axis=0) - cnt
        base_t = jnp.broadcast_to(base[:, None, None, :], (n_tok_tiles, tm, 1, n_experts))
        chosen = sel[:, :TOP_K].reshape(n_tok_tiles, tm, TOP_K, 1) == experts
        pos = jnp.sum(jnp.where(chosen, base_t, 0), axis=-1).reshape(t, TOP_K) + rank[:, :TOP_K]

        n_used = group_end[-1] // tm_e
        tile_start = jnp.arange(n_row_tiles, dtype=jnp.int32) * tm_e
        tile_expert = jnp.sum(tile_start[:, None] >= group_end[None, :], axis=1).astype(jnp.int32)
        tile_expert = jnp.minimum(tile_expert, n_experts - 1)
        tile_expert = jnp.where(tile_start < group_end[-1], tile_expert, tile_expert[n_used - 1])
        meta = jnp.stack([n_used, n_used]).astype(jnp.int32)

        idx = pos.T[:, None, :] + (jnp.arange(rows, dtype=jnp.int32) * p)[None, :, None]
        xs = _sc_scatter_rows(tp.reshape(rows * t, PACK_W), idx.reshape(TOP_K, rows * t), rows * p)
        staged.append((xs.reshape(rows, p, PACK_W), tile_expert, meta, idx.reshape(-1), wgt))

    gathered = []
    for xs, tile_expert, meta, idx, wgt in staged:
        ys = _experts(xs, tile_expert, meta, w_gate_e, w_up_e, w_down_e, layer, tm=tm_e)
        yk = _sc_gather_rows(ys.reshape(-1, PACK_W), idx)
        gathered.append((yk.reshape(TOP_K, xs.shape[0], t, PACK_W), wgt))

    out = None
    for part, (yk, wgt) in enumerate(gathered):
        out = _combine(x1, mod, seq_len, lw, yk, wgt, tm=tm, tile_off=part * n_tok_tiles, prev=out)
    return out


def _rope_tables(n):
    rows = n // GRID_W
    r, col = jnp.meshgrid(jnp.arange(rows), jnp.arange(GRID_W), indexing="ij")
    r = r.reshape(-1).astype(F32)
    col = col.reshape(-1).astype(F32)
    pairs = QK_ROPE // 4
    inv = ROPE_BASE ** (-jnp.arange(pairs, dtype=F32) / pairs)
    ang = jnp.concatenate([r[:, None] * inv, col[:, None] * inv], axis=-1)
    cos, sin = jnp.cos(ang), jnp.sin(ang)
    pad = HEAD_PAD - QK_NOPE - QK_ROPE
    cos_t = jnp.concatenate([jnp.ones((n, QK_NOPE), F32), cos, cos, jnp.zeros((n, pad), F32)], axis=1)
    sin_t = jnp.concatenate([jnp.zeros((n, QK_NOPE), F32), sin, sin, jnp.zeros((n, pad), F32)], axis=1)
    return cos_t, sin_t


def _identity_tables(n):
    pad = HEAD_PAD - QK_NOPE - QK_ROPE
    cos_t = jnp.concatenate([jnp.ones((n, QK_NOPE + QK_ROPE), F32), jnp.zeros((n, pad), F32)], axis=1)
    return cos_t, jnp.zeros((n, HEAD_PAD), F32)


def _position_dft(n):
    idx = (jnp.arange(n, dtype=jnp.int32)[:, None] * jnp.arange(n, dtype=jnp.int32)[None, :]) % n
    ang = idx.astype(F32) * (2.0 * np.pi / n)
    norm = 1.0 / np.sqrt(n)
    return jnp.concatenate([jnp.cos(ang) * norm, jnp.sin(ang) * (-norm)], axis=1).astype(BF16)


def _channel_dft(width):
    gc = width // FOURIER_GROUPS
    idx = (jnp.arange(gc, dtype=jnp.int32)[:, None] * jnp.arange(gc, dtype=jnp.int32)[None, :]) % gc
    ang = idx.astype(F32) * (2.0 * np.pi / gc)
    eye = jnp.eye(FOURIER_GROUPS, dtype=F32)
    norm = 1.0 / np.sqrt(gc)
    return jnp.concatenate([jnp.kron(eye, jnp.cos(ang) * norm), jnp.kron(eye, jnp.sin(ang) * norm)],
                           axis=1).astype(BF16)


def _pad_cols(w, width):
    return jnp.pad(w, ((0, 0), (0, width - w.shape[1])))


def _rot_cols(w):
    half = w.shape[-1] // 2
    return jnp.concatenate([-w[..., half:], w[..., :half]], axis=-1)


def _layer_weights(l, g_pre_mix, g_post_mix, g_pre_ffn, g_post_ffn, w_in, b_gate, g_q, w_uq, g_kv,
                   w_ukv, w_mla_out, conv_w, w_conv_out, w_four_out, w_out, w_router, b_router,
                   w_gate_s, w_up_s, w_down_s, dc):
    d = w_in.shape[1]
    kv_rank = g_kv.shape[1]
    q_rank = g_q.shape[1]
    w = w_in[l]
    kv_end = kv_rank + QK_ROPE
    zeros_nope = jnp.zeros((d, QK_NOPE), F32)
    kpe = w[:, kv_rank:kv_end]
    w1 = jnp.concatenate([
        w[:, :kv_rank],
        _pad_cols(jnp.concatenate([zeros_nope, kpe], axis=1), HEAD_PAD),
        _pad_cols(jnp.concatenate([zeros_nope, _rot_cols(kpe)], axis=1), HEAD_PAD),
        w[:, kv_end:],
    ], axis=1).astype(BF16)

    qk_dim = QK_NOPE + QK_ROPE
    pad = HEAD_PAD - qk_dim
    uq = w_uq[l].reshape(q_rank, N_HEADS, qk_dim) * (qk_dim ** -0.5 * np.log2(np.e))
    zq = jnp.zeros((q_rank, N_HEADS, pad), F32)
    wuq = jnp.concatenate([uq, zq], axis=-1).reshape(q_rank, N_HEADS * HEAD_PAD).astype(BF16)
    wuq_rot = jnp.concatenate([jnp.zeros((q_rank, N_HEADS, QK_NOPE), F32), _rot_cols(uq[..., QK_NOPE:]), zq],
                              axis=-1).reshape(q_rank, N_HEADS * HEAD_PAD).astype(BF16)
    ukv = w_ukv[l].reshape(kv_rank, N_HEADS, QK_NOPE + V_DIM)
    wuk = jnp.concatenate([ukv[..., :QK_NOPE], jnp.zeros((kv_rank, N_HEADS, HEAD_PAD - QK_NOPE), F32)],
                          axis=-1).reshape(kv_rank, N_HEADS * HEAD_PAD).astype(BF16)
    wuv = jnp.concatenate([ukv[..., QK_NOPE:], jnp.zeros((kv_rank, N_HEADS, HEAD_PAD - V_DIM), F32)],
                          axis=-1).reshape(kv_rank, N_HEADS * HEAD_PAD).astype(BF16)
    v_one = jnp.tile((jnp.arange(HEAD_PAD) == V_DIM).astype(F32), N_HEADS)[None]

    wr = w_router[l]
    wr_hi = wr.astype(BF16)
    wr_lo = (wr - wr_hi.astype(F32)).astype(BF16)
    return {
        "g_pre_mix": g_pre_mix[l][None], "g_post_mix": g_post_mix[l][None],
        "g_pre_ffn": g_pre_ffn[l][None], "g_post_ffn": g_post_ffn[l][None],
        "w1": w1, "b_gate": b_gate[l][None], "g_q": g_q[l][None], "g_kv": g_kv[l][None],
        "wuq": wuq, "wuq_rot": wuq_rot, "wuk": wuk, "wuv": wuv, "v_one": v_one, "dc": dc,
        "conv_w": conv_w[l],
        "w_mla_out": w_mla_out[l].astype(BF16), "w_conv_out": w_conv_out[l].astype(BF16),
        "w_four_out": w_four_out[l].astype(BF16), "w_out": w_out[l].astype(BF16),
        "w_router": jnp.concatenate([wr_hi, wr_lo], axis=1), "b_router": b_router[l][None],
        "w_gate_s": w_gate_s[l].astype(BF16), "w_up_s": w_up_s[l].astype(BF16),
        "w_down_s": w_down_s[l].astype(BF16),
    }


def _tile(n, pref):
    return pref if n % pref == 0 else n


def kernel(x, c, ctx, c_ctx, w_ada, b_ada, g_pre_mix, g_post_mix, g_pre_ffn, g_post_ffn, w_in, b_gate,
           g_q, w_uq, g_kv, w_ukv, w_mla_out, conv_w, w_conv_out, w_four_out, w_out, w_router, b_router,
           w_gate_e, w_up_e, w_down_e, w_gate_s, w_up_s, w_down_s):
    batch, seq, d = x.shape
    n_ctx = ctx.shape[1]
    n_layers = w_in.shape[0]
    xs = x.reshape(batch * seq, d)
    cs = ctx.reshape(batch * n_ctx, d)

    mod_rows = 16
    c_all = jnp.concatenate([c, c_ctx[None], jnp.zeros((mod_rows - batch - 1, d), F32)], axis=0)
    ada = _ada(c_all, w_ada, b_ada)

    tab_x = _rope_tables(seq)
    tm_c = _tile(n_ctx, 256)
    tab_c = _identity_tables(tm_c)
    cs_x = _position_dft(seq)
    cs_c = _position_dft(n_ctx)
    dc = _channel_dft(w_four_out.shape[1])

    tm_x = _tile(seq, 512)
    tm_moe_x = _tile(seq, 1024)
    tm_moe_c = _tile(batch * n_ctx, 1024)

    for l in range(n_layers):
        last = l == n_layers - 1
        lw = _layer_weights(l, g_pre_mix, g_post_mix, g_pre_ffn, g_post_ffn, w_in, b_gate, g_q, w_uq,
                            g_kv, w_ukv, w_mla_out, conv_w, w_conv_out, w_four_out, w_out, w_router,
                            b_router, w_gate_s, w_up_s, w_down_s, dc)
        mods = ada[l].reshape(mod_rows, 6, d)
        mod_x = mods[:batch]
        mod_c = mods[batch:batch + 1]

        pc = _inproj(cs, mod_c, n_ctx, lw, tab_c, kv_only=last, tm=tm_c)
        px = _inproj(xs, mod_x, seq, lw, tab_x, kv_only=False, tm=tm_x)
        o_x = _attention(px["q"], [(pc["k"], pc["v"], n_ctx), (px["k"], px["v"], seq)], batch, seq, tq=tm_x)
        f_x = _fourier(px["ab"], cs_x, batch, seq, tn=tm_x)
        x1 = _merge(xs, mod_x, seq, px, o_x, f_x, lw, tm=tm_x)
        xs = _moe_sparse(x1, mod_x, seq, lw, w_gate_e, w_up_e, w_down_e, l, tm=tm_moe_x, tm_e=512,
                         n_parts=2 if batch % 2 == 0 else 1)
        if not last:
            o_c = _attention(pc["q"], [(pc["k"], pc["v"], n_ctx)], batch, n_ctx, tq=tm_c)
            f_c = _fourier(pc["ab"], cs_c, batch, n_ctx, tn=tm_c)
            c1 = _merge(cs, mod_c, n_ctx, pc, o_c, f_c, lw, tm=tm_c)
            cs = _moe(c1, mod_c, batch * n_ctx, lw, w_gate_e, w_up_e, w_down_e, l, tm=tm_moe_c)
    return xs.reshape(batch, seq, d)
```

```python
import functools

import numpy as np
import jax
import jax.numpy as jnp
from jax import lax
from jax.experimental import pallas as pl
from jax.experimental.pallas import tpu as pltpu
from jax.experimental.pallas import tpu_sc as plsc

N_HEADS = 8
QK_NOPE = 64
QK_ROPE = 32
V_DIM = 64
GRID_W = 64
ROPE_BASE = 10000.0
FOURIER_GROUPS = 4
TOP_K = 4
ROUTED_SCALE = 2.5
N_BRANCHES = 3
EPS = 1e-6

LANE = 128
HEAD_PAD = LANE
VMEM_LIMIT = 56 * 1024 * 1024
PACK_W = 256
SC_WINDOW = 128

F32 = jnp.float32
BF16 = jnp.bfloat16


def _rms(x, g):
    return x * lax.rsqrt(jnp.mean(x * x, axis=-1, keepdims=True) + EPS) * g


def _sigmoid(x):
    return 1.0 / (1.0 + jnp.exp(-x))


def _dot(a, b):
    return jnp.dot(a, b, preferred_element_type=F32)


def _resident(shape):
    nd = len(shape)
    return pl.BlockSpec(shape, lambda *_: (0,) * nd, pipeline_mode=pl.Buffered(1))


def _params(n_grid):
    return pltpu.CompilerParams(dimension_semantics=("arbitrary",) * n_grid,
                                vmem_limit_bytes=VMEM_LIMIT)


def _ada_kernel(c_ref, w_ref, b_ref, o_ref):
    c = c_ref[...]
    a = (c * _sigmoid(c)).astype(BF16)
    o_ref[0] = _dot(a, w_ref[0].astype(BF16)) + b_ref[0]


def _ada(c_all, w_ada, b_ada):
    n_layers, d, n_out = w_ada.shape
    rows = c_all.shape[0]
    tn = 1536
    return pl.pallas_call(
        _ada_kernel,
        grid=(n_layers, n_out // tn),
        in_specs=[
            pl.BlockSpec((rows, d), lambda l, j: (0, 0)),
            pl.BlockSpec((1, d, tn), lambda l, j: (l, 0, j)),
            pl.BlockSpec((1, 1, tn), lambda l, j: (l, 0, j)),
        ],
        out_specs=pl.BlockSpec((1, rows, tn), lambda l, j: (l, 0, j)),
        out_shape=jax.ShapeDtypeStruct((n_layers, rows, n_out), F32),
        compiler_params=_params(2),
        name="ada",
    )(c_all, w_ada, b_ada.reshape(n_layers, 1, n_out))


def _inproj_kernel(*refs, kv_only, kv_rank, q_rank, conv_w, four_w, d_model):
    if kv_only:
        (x_ref, mod_ref, gpre_ref, w1_ref, gkv_ref, wuk_ref, wuv_ref, vone_ref, cos_ref, sin_ref,
         k_ref, v_ref) = refs
    else:
        (x_ref, mod_ref, gpre_ref, w1_ref, gkv_ref, wuk_ref, wuv_ref, vone_ref, cos_ref, sin_ref,
         bg_ref, gq_ref, wuq_ref, wuqr_ref, dc_ref,
         k_ref, v_ref, q_ref, cb_ref, cc_ref, cu_ref, ab_ref, gate_ref) = refs

    x = x_ref[...]
    shift = mod_ref[0, 0:1, :]
    scale = mod_ref[0, 1:2, :]
    h = (_rms(x, gpre_ref[...]) * (1.0 + scale) + shift).astype(BF16)
    cos = cos_ref[...]
    sin = sin_ref[...]

    o_kpe = kv_rank
    o_rot = o_kpe + HEAD_PAD
    o_q = o_rot + HEAD_PAD
    p = _dot(h, w1_ref[0, :,0:o_q])
    ckv = _rms(p[:, 0:kv_rank], gkv_ref[...]).astype(BF16)
    kpe = p[:, o_kpe:o_rot] * cos + p[:, o_rot:o_q] * sin
    k = _dot(ckv, wuk_ref[...]) + jnp.concatenate([kpe] * N_HEADS, axis=1)
    k_ref[...] = k.astype(k_ref.dtype)
    v_ref[...] = (_dot(ckv, wuv_ref[...]) + vone_ref[...]).astype(v_ref.dtype)
    if kv_only:
        return

    o_cb = o_q + q_rank
    cq = _rms(_dot(h, w1_ref[0, :,o_q:o_cb]), gq_ref[...]).astype(BF16)
    cos_h = jnp.concatenate([cos] * N_HEADS, axis=1)
    sin_h = jnp.concatenate([sin] * N_HEADS, axis=1)
    q = _dot(cq, wuq_ref[...]) * cos_h + _dot(cq, wuqr_ref[...]) * sin_h
    q_ref[...] = q.astype(q_ref.dtype)

    o_cc = o_cb + conv_w
    o_cu = o_cc + conv_w
    o_four = o_cu + conv_w
    cb_ref[...] = _dot(h, w1_ref[0, :,o_cb:o_cc]).astype(cb_ref.dtype)
    cc_ref[...] = _dot(h, w1_ref[0, :,o_cc:o_cu]).astype(cc_ref.dtype)
    cu_ref[...] = _dot(h, w1_ref[0, :,o_cu:o_four]).astype(cu_ref.dtype)

    o_gate = o_four + four_w
    uf = _dot(h, w1_ref[0, :,o_four:o_gate]).astype(BF16)
    ab_ref[...] = _dot(uf, dc_ref[...]).astype(ab_ref.dtype)

    for j in range(N_BRANCHES):
        lo = o_gate + j * d_model
        z = _dot(h, w1_ref[0, :,lo:lo + d_model]) + bg_ref[:, j * d_model:(j + 1) * d_model]
        gate_ref[:, j * d_model:(j + 1) * d_model] = _sigmoid(z).astype(gate_ref.dtype)


def _inproj(xs, mod, seq_len, lw, tables, *, kv_only, tm):
    t, d = xs.shape
    nb = mod.shape[0]
    tiles_per_seq = seq_len // tm
    cos_t, sin_t = tables
    table_tiles = cos_t.shape[0] // tm
    kv_rank = lw["g_kv"].shape[1]
    q_rank = lw["g_q"].shape[1]
    conv_w = lw["conv_w"].shape[1]
    four_w = lw["dc"].shape[0]
    n_k = N_HEADS * HEAD_PAD

    def row(i):
        return (i, 0)

    def mod_map(i):
        return ((i // tiles_per_seq) % nb, 0, 0)

    def tab_map(i):
        return (i % table_tiles, 0)

    w1, layer = lw["w1"], lw["layer"]
    w1_cols = kv_rank + 2 * HEAD_PAD if kv_only else w1.shape[2]
    in_specs = [
        pl.BlockSpec((tm, d), row),
        pl.BlockSpec((1,) + mod.shape[1:], mod_map),
        _resident((1, d)),
        pl.BlockSpec((1, d, w1_cols), lambda i: (layer, 0, 0), pipeline_mode=pl.Buffered(1)),
        _resident((1, kv_rank)),
        _resident(lw["wuk"].shape),
        _resident(lw["wuv"].shape),
        _resident(lw["v_one"].shape),
        pl.BlockSpec((tm, HEAD_PAD), tab_map),
        pl.BlockSpec((tm, HEAD_PAD), tab_map),
    ]
    args = [xs, mod, lw["g_pre_mix"], w1, lw["g_kv"], lw["wuk"], lw["wuv"], lw["v_one"], cos_t, sin_t]
    out_shape = [jax.ShapeDtypeStruct((t, n_k), BF16), jax.ShapeDtypeStruct((t, n_k), BF16)]
    out_specs = [pl.BlockSpec((tm, n_k), row), pl.BlockSpec((tm, n_k), row)]
    if not kv_only:
        in_specs += [
            _resident(lw["b_gate"].shape),
            _resident((1, q_rank)),
            _resident(lw["wuq"].shape),
            _resident(lw["wuq_rot"].shape),
            _resident(lw["dc"].shape),
        ]
        args += [lw["b_gate"], lw["g_q"], lw["wuq"], lw["wuq_rot"], lw["dc"]]
        widths = [n_k, conv_w, conv_w, conv_w, 2 * four_w, N_BRANCHES * d]
        out_shape += [jax.ShapeDtypeStruct((t, w), BF16) for w in widths]
        out_specs += [pl.BlockSpec((tm, w), row) for w in widths]
    outs = pl.pallas_call(
        functools.partial(_inproj_kernel, kv_only=kv_only, kv_rank=kv_rank, q_rank=q_rank,
                          conv_w=conv_w, four_w=four_w, d_model=d),
        grid=(t // tm,),
        in_specs=in_specs,
        out_specs=out_specs,
        out_shape=out_shape,
        compiler_params=_params(1),
        name="inproj_kv" if kv_only else "inproj",
    )(*args)
    names = ["k", "v", "q", "cb", "cc", "cu", "ab", "gate"]
    return dict(zip(names, outs))


def _attn_kernel(*refs, n_seg):
    q_ref = refs[0]
    o_ref = refs[-1]
    nt = (((1,), (1,)), ((), ()))
    outs = []
    for hh in range(N_HEADS):
        head = slice(hh * HEAD_PAD, (hh + 1) * HEAD_PAD)
        qh = q_ref[:, head]
        s = [lax.dot_general(qh, refs[1 + 2 * i][:, head], nt, preferred_element_type=F32)
             for i in range(n_seg)]
        m = functools.reduce(jnp.maximum, [jnp.max(si, axis=-1, keepdims=True) for si in s])
        acc = functools.reduce(jnp.add, [
            _dot(jnp.exp2((s[i] - m).astype(BF16)), refs[2 + 2 * i][:, head]) for i in range(n_seg)])
        outs.append(acc[:, 0:V_DIM] / acc[:, V_DIM:V_DIM + 1])
    o_ref[...] = jnp.concatenate(outs, axis=1).astype(o_ref.dtype)


def _attention(q, segs, batch, seq_q, *, tq):
    t = q.shape[0]
    qt = seq_q // tq
    n_k = N_HEADS * HEAD_PAD
    in_specs = [pl.BlockSpec((tq, n_k), lambda b, j: (b * qt + j, 0))]
    args = [q]
    for k, v, m in segs:
        in_specs.append(pl.BlockSpec((m, n_k), lambda b, j: (b, 0)))
        in_specs.append(pl.BlockSpec((m, n_k), lambda b, j: (b, 0)))
        args += [k, v]
    return pl.pallas_call(
        functools.partial(_attn_kernel, n_seg=len(segs)),
        grid=(batch, qt),
        in_specs=in_specs,
        out_specs=pl.BlockSpec((tq, N_HEADS * V_DIM), lambda b, j: (b * qt + j, 0)),
        out_shape=jax.ShapeDtypeStruct((t, N_HEADS * V_DIM), BF16),
        compiler_params=_params(2),
        name="attention",
    )(*args)


def _four_kernel(cs_ref, ab_ref, o_ref, *, n, fw):
    o = _dot(cs_ref[:, 0:n], ab_ref[:, 0:fw]) + _dot(cs_ref[:, n:2 * n], ab_ref[:, fw:2 * fw])
    o_ref[...] = o.astype(o_ref.dtype)


def _fourier(ab, cs, batch, seq_len, *, tn):
    t, fw2 = ab.shape
    fw = fw2 // 2
    nt = seq_len // tn
    return pl.pallas_call(
        functools.partial(_four_kernel, n=seq_len, fw=fw),
        grid=(batch, nt),
        in_specs=[
            pl.BlockSpec((tn, 2 * seq_len), lambda b, j: (j, 0)),
            pl.BlockSpec((seq_len, fw2), lambda b, j: (b, 0)),
        ],
        out_specs=pl.BlockSpec((tn, fw), lambda b, j: (b * nt + j, 0)),
        out_shape=jax.ShapeDtypeStruct((t, fw), BF16),
        compiler_params=_params(2),
        name="fourier",
    )(cs, ab)


def _merge_kernel(x_ref, mod_ref, o_ref, cb_ref, cc_ref, cu_ref, ccp_ref, cup_ref, ccn_ref,
                  cun_ref, f_ref, gate_ref, convw_ref, wmo_ref, wco_ref, wfo_ref, wout_ref,
                  gpost_ref, out_ref, pad_ref, *, tiles_per_seq, tm, d_model):
    i = pl.program_id(0)
    pos = i % tiles_per_seq
    has_prev = (pos > 0).astype(F32)
    has_next = (pos < tiles_per_seq - 1).astype(F32)
    pad_ref[0:8, :] = ccp_ref[...].astype(F32) * cup_ref[...].astype(F32) * has_prev
    pad_ref[8:8 + tm, :] = cc_ref[...].astype(F32) * cu_ref[...].astype(F32)
    pad_ref[8 + tm:16 + tm, :] = ccn_ref[...].astype(F32) * cun_ref[...].astype(F32) * has_next
    conv = (pad_ref[7:7 + tm, :] * convw_ref[0:1, :] + pad_ref[8:8 + tm, :] * convw_ref[1:2, :]
            + pad_ref[9:9 + tm, :] * convw_ref[2:3, :])
    y_conv = _dot((cb_ref[...].astype(F32) * conv).astype(BF16), wco_ref[...])
    y_attn = _dot(o_ref[...], wmo_ref[...])
    y_four = _dot(f_ref[...], wfo_ref[...])
    d = d_model
    merged = (gate_ref[:, 0:d].astype(F32) * y_attn + gate_ref[:, d:2 * d].astype(F32) * y_conv
              + gate_ref[:, 2 * d:3 * d].astype(F32) * y_four)
    y = _dot(merged.astype(BF16), wout_ref[...])
    g1 = mod_ref[0, 2:3, :]
    out_ref[...] = x_ref[...] + g1 * _rms(y, gpost_ref[...])


def _merge(xs, mod, seq_len, pr, o, four, lw, *, tm):
    t, d = xs.shape
    nb = mod.shape[0]
    tiles_per_seq = seq_len // tm
    cw = lw["conv_w"].shape[1]
    fw = four.shape[1]
    hb = tm // 8
    last_hb = t // 8 - 1

    def row(i):
        return (i, 0)

    def prev_map(i):
        return (jnp.maximum(i * hb - 1, 0), 0)

    def next_map(i):
        return (jnp.minimum((i + 1) * hb, last_hb), 0)

    in_specs = [
        pl.BlockSpec((tm, d), row),
        pl.BlockSpec((1,) + mod.shape[1:], lambda i: ((i // tiles_per_seq) % nb, 0, 0)),
        pl.BlockSpec((tm, o.shape[1]), row),
        pl.BlockSpec((tm, cw), row),
        pl.BlockSpec((tm, cw), row),
        pl.BlockSpec((tm, cw), row),
        pl.BlockSpec((8, cw), prev_map),
        pl.BlockSpec((8, cw), prev_map),
        pl.BlockSpec((8, cw), next_map),
        pl.BlockSpec((8, cw), next_map),
        pl.BlockSpec((tm, fw), row),
        pl.BlockSpec((tm, N_BRANCHES * d), row),
        _resident(lw["conv_w"].shape),
        _resident(lw["w_mla_out"].shape),
        _resident(lw["w_conv_out"].shape),
        _resident(lw["w_four_out"].shape),
        _resident(lw["w_out"].shape),
        _resident((1, d)),
    ]
    return pl.pallas_call(
        functools.partial(_merge_kernel, tiles_per_seq=tiles_per_seq, tm=tm, d_model=d),
        grid=(t // tm,),
        in_specs=in_specs,
        out_specs=pl.BlockSpec((tm, d), row),
        out_shape=jax.ShapeDtypeStruct((t, d), F32),
        scratch_shapes=[pltpu.VMEM((tm + 16, cw), F32)],
        compiler_params=_params(1),
        name="merge",
    )(xs, mod, o, pr["cb"], pr["cc"], pr["cu"], pr["cc"], pr["cu"], pr["cc"], pr["cu"], four,
      pr["gate"], lw["conv_w"], lw["w_mla_out"], lw["w_conv_out"], lw["w_four_out"], lw["w_out"],
      lw["g_post_mix"])


def _moe_kernel(x_ref, mod_ref, gpre_ref, gpost_ref, wr_ref, br_ref, wgs_ref, wus_ref, wds_ref,
                wge_ref, wue_ref, wde_ref, out_ref, t_ref, comb_ref, acc_ref, *, n_experts):
    e = pl.program_id(1)

    @pl.when(e == 0)
    def _():
        shift = mod_ref[0, 3:4, :]
        scale = mod_ref[0, 4:5, :]
        t = _rms(x_ref[...], gpre_ref[...]) * (1.0 + scale) + shift
        t_hi = t.astype(BF16)
        t_lo = (t - t_hi.astype(F32)).astype(BF16)
        t_ref[...] = t_hi
        hh = _dot(t_hi, wr_ref[...])
        logits = hh[:, 0:n_experts] + hh[:, n_experts:2 * n_experts] + _dot(t_lo, wr_ref[:, 0:n_experts])
        scores = _sigmoid(logits)
        work = scores + br_ref[...]
        lane = lax.broadcasted_iota(jnp.int32, scores.shape, 1)
        comb = jnp.zeros_like(scores)
        for _ in range(TOP_K):
            best = jnp.max(work, axis=-1, keepdims=True)
            first = jnp.min(jnp.where(work == best, lane, n_experts), axis=-1, keepdims=True)
            hit = lane == first
            comb = jnp.where(hit, scores, comb)
            work = jnp.where(hit, -jnp.inf, work)
        comb_ref[...] = comb / jnp.sum(comb, axis=-1, keepdims=True) * ROUTED_SCALE
        gate = _dot(t_hi, wgs_ref[...])
        act = (gate * _sigmoid(gate) * _dot(t_hi, wus_ref[...])).astype(BF16)
        acc_ref[...] = _dot(act, wds_ref[...])

    t_hi = t_ref[...]
    gate = _dot(t_hi, wge_ref[0, 0].astype(BF16))
    up = _dot(t_hi, wue_ref[0, 0].astype(BF16))
    lane = lax.broadcasted_iota(jnp.int32, comb_ref.shape, 1)
    w_e = jnp.sum(jnp.where(lane == e, comb_ref[...], 0.0), axis=-1, keepdims=True)
    act = (gate * _sigmoid(gate) * up * w_e).astype(BF16)
    acc_ref[...] += _dot(act, wde_ref[0, 0].astype(BF16))

    @pl.when(e == n_experts - 1)
    def _():
        g2 = mod_ref[0, 5:6, :]
        out_ref[...] = x_ref[...] + g2 * _rms(acc_ref[...], gpost_ref[...])


def _moe(xs, mod, seq_len, lw, w_gate_e, w_up_e, w_down_e, layer, *, tm):
    t, d = xs.shape
    nb = mod.shape[0]
    tiles_per_seq = max(seq_len // tm, 1)
    n_experts, _, f = w_gate_e.shape[1:]
    in_specs = [
        pl.BlockSpec((tm, d), lambda i, e: (i, 0)),
        pl.BlockSpec((1,) + mod.shape[1:], lambda i, e: ((i // tiles_per_seq) % nb, 0, 0)),
        _resident((1, d)),
        _resident((1, d)),
        _resident(lw["w_router"].shape),
        _resident((1, n_experts)),
        _resident(lw["w_gate_s"].shape),
        _resident(lw["w_up_s"].shape),
        _resident(lw["w_down_s"].shape),
        pl.BlockSpec((1, 1, d, f), lambda i, e: (layer, e, 0, 0)),
        pl.BlockSpec((1, 1, d, f), lambda i, e: (layer, e, 0, 0)),
        pl.BlockSpec((1, 1, f, d), lambda i, e: (layer, e, 0, 0)),
    ]
    return pl.pallas_call(
        functools.partial(_moe_kernel, n_experts=n_experts),
        grid=(t // tm, n_experts),
        in_specs=in_specs,
        out_specs=pl.BlockSpec((tm, d), lambda i, e: (i, 0)),
        out_shape=jax.ShapeDtypeStruct((t, d), F32),
        scratch_shapes=[pltpu.VMEM((tm, d), BF16), pltpu.VMEM((tm, n_experts), F32),
                        pltpu.VMEM((tm, d), F32)],
        compiler_params=_params(2),
        name="moe",
    )(xs, mod, lw["g_pre_ffn"], lw["g_post_ffn"], lw["w_router"], lw["b_router"], lw["w_gate_s"],
      lw["w_up_s"], lw["w_down_s"], w_gate_e, w_up_e, w_down_e)


def _pack_rows(v):
    bits = lax.bitcast_convert_type(v.astype(BF16).astype(F32), jnp.uint32)
    rows = []
    for j in range(v.shape[1] // (2 * PACK_W)):
        lo = bits[:, (2 * j) * PACK_W:(2 * j + 1) * PACK_W]
        hi = bits[:, (2 * j + 1) * PACK_W:(2 * j + 2) * PACK_W]
        rows.append(lax.bitcast_convert_type((hi & jnp.uint32(0xFFFF0000)) | (lo >> 16), jnp.int32))
    return rows


def _unpack_rows(rows):
    parts = []
    for r in rows:
        u = lax.bitcast_convert_type(r, jnp.uint32)
        parts.append(lax.bitcast_convert_type(u << 16, F32))
        parts.append(lax.bitcast_convert_type(u & jnp.uint32(0xFFFF0000), F32))
    return jnp.concatenate(parts, axis=1)


def _route_kernel(x_ref, mod_ref, gpre_ref, wr_ref, br_ref, tri_ref,
                  tp_ref, sel_ref, wgt_ref, rank_ref, cnt_ref, *, n_experts):
    shift = mod_ref[0, 3:4, :]
    scale = mod_ref[0, 4:5, :]
    t = _rms(x_ref[...], gpre_ref[...]) * (1.0 + scale) + shift
    t_hi = t.astype(BF16)
    t_lo = (t - t_hi.astype(F32)).astype(BF16)
    for j, r in enumerate(_pack_rows(t)):
        tp_ref[j] = r
    hh = _dot(t_hi, wr_ref[...])
    logits = hh[:, 0:n_experts] + hh[:, n_experts:2 * n_experts] + _dot(t_lo, wr_ref[:, 0:n_experts])
    scores = _sigmoid(logits)
    work = scores + br_ref[...]
    lane = lax.broadcasted_iota(jnp.int32, scores.shape, 1)
    wide = lax.broadcasted_iota(jnp.int32, (scores.shape[0], LANE), 1)
    firsts, picked, hits = [], [], []
    for k in range(TOP_K):
        best = jnp.max(work, axis=-1, keepdims=True)
        first = jnp.min(jnp.where(work == best, lane, n_experts), axis=-1, keepdims=True)
        hit = lane == first
        firsts.append(first)
        picked.append(jnp.sum(jnp.where(hit, scores, 0.0), axis=-1, keepdims=True))
        hits.append(wide == first + k * n_experts)
        work = jnp.where(hit, -jnp.inf, work)
    total = functools.reduce(jnp.add, picked)
    onehot = functools.reduce(jnp.add, [jnp.where(h, 1.0, 0.0) for h in hits])
    earlier = _dot(tri_ref[...], onehot.astype(BF16))
    col = jnp.sum(onehot, axis=0, keepdims=True)
    col8 = jnp.broadcast_to(col, (8, LANE))
    lane8 = lax.broadcasted_iota(jnp.int32, (8, LANE), 1)
    before = jnp.zeros((8, LANE), F32)
    for s in range(1, TOP_K):
        before = before + jnp.where(lane8 >= s * n_experts, pltpu.roll(col8, s * n_experts, 1), 0.0)
    ahead = earlier + before[0:1, :]
    sel = jnp.zeros(wide.shape, jnp.int32)
    wgt = jnp.zeros(wide.shape, F32)
    rank = jnp.zeros(wide.shape, F32)
    for k in range(TOP_K):
        sel = jnp.where(wide == k, firsts[k], sel)
        wgt = jnp.where(wide == k, picked[k] / total * ROUTED_SCALE, wgt)
        rank = jnp.where(wide == k, jnp.sum(jnp.where(hits[k], ahead, 0.0), axis=-1, keepdims=True), rank)
    sel_ref[...] = sel
    wgt_ref[...] = wgt
    rank_ref[...] = rank.astype(jnp.int32)
    cnt_ref[0] = col8.astype(jnp.int32)


def _route(x1, mod, seq_len, lw, *, tm, tile_off, n_tiles):
    d = x1.shape[1]
    t = n_tiles * tm
    nb = mod.shape[0]
    tiles_per_seq = max(seq_len // tm, 1)
    n_experts = lw["b_router"].shape[1]
    rows = d // (2 * PACK_W)
    tri = jnp.tri(tm, tm, -1, dtype=BF16)
    row = lambda i: (i, 0)
    return pl.pallas_call(
        functools.partial(_route_kernel, n_experts=n_experts),
        grid=(n_tiles,),
        in_specs=[
            pl.BlockSpec((tm, d), lambda i: (i + tile_off, 0)),
            pl.BlockSpec((1,) + mod.shape[1:], lambda i: (((i + tile_off) // tiles_per_seq) % nb, 0, 0)),
            _resident((1, d)),
            _resident(lw["w_router"].shape),
            _resident((1, n_experts)),
            _resident((tm, tm)),
        ],
        out_specs=[
            pl.BlockSpec((rows, tm, PACK_W), lambda i: (0, i, 0)),
            pl.BlockSpec((tm, LANE), row),
            pl.BlockSpec((tm, LANE), row),
            pl.BlockSpec((tm, LANE), row),
            pl.BlockSpec((1, 8, LANE), lambda i: (i, 0, 0)),
        ],
        out_shape=[
            jax.ShapeDtypeStruct((rows, t, PACK_W), jnp.int32),
            jax.ShapeDtypeStruct((t, LANE), jnp.int32),
            jax.ShapeDtypeStruct((t, LANE), F32),
            jax.ShapeDtypeStruct((t, LANE), jnp.int32),
            jax.ShapeDtypeStruct((n_tiles, 8, LANE), jnp.int32),
        ],
        compiler_params=_params(1),
        name="moe_route",
    )(x1, mod, lw["g_pre_ffn"], lw["w_router"], lw["b_router"], tri)


def _expert_kernel(te_ref, meta_ref, xs_ref, wg_ref, wu_ref, wd_ref, ys_ref, wg_sc, wu_sc, wd_sc):
    i = pl.program_id(0)
    live = i < meta_ref[0]
    new_expert = (i == 0) | (te_ref[i] != te_ref[jnp.maximum(i - 1, 0)])

    @pl.when(live & new_expert)
    def _():
        wg_sc[...] = wg_ref[0, 0].astype(BF16)
        wu_sc[...] = wu_ref[0, 0].astype(BF16)
        wd_sc[...] = wd_ref[0, 0].astype(BF16)

    @pl.when(live)
    def _():
        x = _unpack_rows([xs_ref[j] for j in range(xs_ref.shape[0])]).astype(BF16)
        gate = _dot(x, wg_sc[...])
        act = (gate * _sigmoid(gate) * _dot(x, wu_sc[...])).astype(BF16)
        for j, r in enumerate(_pack_rows(_dot(act, wd_sc[...]))):
            ys_ref[j] = r


def _experts(xs, tile_expert, meta, w_gate_e, w_up_e, w_down_e, layer, *, tm):
    rows, p, _ = xs.shape
    n_experts, d, f = w_gate_e.shape[1:]

    def slot(i, te, meta):
        return (0, jnp.minimum(i, meta[0] - 1), 0)

    grid_spec = pltpu.PrefetchScalarGridSpec(
        num_scalar_prefetch=2,
        grid=(p // tm,),
        in_specs=[
            pl.BlockSpec((rows, tm, PACK_W), slot),
            pl.BlockSpec((1, 1, d, f), lambda i, te, meta: (layer, te[i], 0, 0)),
            pl.BlockSpec((1, 1, d, f), lambda i, te, meta: (layer, te[i], 0, 0)),
            pl.BlockSpec((1, 1, f, d), lambda i, te, meta: (layer, te[i], 0, 0)),
        ],
        out_specs=pl.BlockSpec((rows, tm, PACK_W), slot),
        scratch_shapes=[pltpu.VMEM((d, f), BF16), pltpu.VMEM((d, f), BF16), pltpu.VMEM((f, d), BF16)],
    )
    return pl.pallas_call(
        _expert_kernel,
        grid_spec=grid_spec,
        out_shape=jax.ShapeDtypeStruct(xs.shape, jnp.int32),
        compiler_params=_params(1),
        name="moe_experts",
    )(tile_expert, meta, xs, w_gate_e, w_up_e, w_down_e)


def _combine_kernel(x_ref, mod_ref, gpre_ref, gpost_ref, wgs_ref, wus_ref, wds_ref, yk_ref, wgt_ref,
                    *rest):
    out_ref = rest[-1]
    shift = mod_ref[0, 3:4, :]
    scale = mod_ref[0, 4:5, :]
    x = x_ref[...]
    t_hi = (_rms(x, gpre_ref[...]) * (1.0 + scale) + shift).astype(BF16)
    gate = _dot(t_hi, wgs_ref[...])
    act = (gate * _sigmoid(gate) * _dot(t_hi, wus_ref[...])).astype(BF16)
    acc = _dot(act, wds_ref[...])
    for k in range(TOP_K):
        y = _unpack_rows([yk_ref[k, j] for j in range(yk_ref.shape[1])])
        acc = acc + wgt_ref[:, k:k + 1] * y
    g2 = mod_ref[0, 5:6, :]
    out_ref[...] = x + g2 * _rms(acc, gpost_ref[...])


def _combine(x1, mod, seq_len, lw, yk, wgt, *, tm, tile_off, prev):
    t, d = x1.shape
    nb = mod.shape[0]
    tiles_per_seq = max(seq_len // tm, 1)
    rows = yk.shape[1]
    n_tiles = yk.shape[2] // tm
    glob = lambda i: (i + tile_off, 0)
    in_specs = [
        pl.BlockSpec((tm, d), glob),
        pl.BlockSpec((1,) + mod.shape[1:], lambda i: (((i + tile_off) // tiles_per_seq) % nb, 0, 0)),
        _resident((1, d)),
        _resident((1, d)),
        _resident(lw["w_gate_s"].shape),
        _resident(lw["w_up_s"].shape),
        _resident(lw["w_down_s"].shape),
        pl.BlockSpec((TOP_K, rows, tm, PACK_W), lambda i: (0, 0, i, 0)),
        pl.BlockSpec((tm, LANE), lambda i: (i, 0)),
    ]
    args = [x1, mod, lw["g_pre_ffn"], lw["g_post_ffn"], lw["w_gate_s"], lw["w_up_s"], lw["w_down_s"], yk, wgt]
    aliases = {}
    if prev is not None:
        in_specs.append(pl.BlockSpec(memory_space=pl.ANY))
        aliases = {len(args): 0}
        args.append(prev)
    return pl.pallas_call(
        _combine_kernel,
        grid=(n_tiles,),
        in_specs=in_specs,
        out_specs=pl.BlockSpec((tm, d), glob),
        out_shape=jax.ShapeDtypeStruct((t, d), F32),
        input_output_aliases=aliases,
        compiler_params=_params(1),
        name="moe_combine",
    )(*args)


def _sc_mesh():
    return plsc.VectorSubcoreMesh(core_axis_name="core", subcore_axis_name="subcore")


def _sc_scatter_rows(src, idx, n_out):
    n_lists, n = idx.shape
    width = src.shape[1]

    @pl.kernel(out_type=jax.ShapeDtypeStruct((n_out, width), src.dtype), mesh=_sc_mesh(), scratch_types=[])
    def scatter(x_hbm, *refs):
        i_hbms, o_hbm = refs[:n_lists], refs[n_lists]

        def body(x_vmem, *i_vmems):
            for i_vmem in i_vmems:
                pltpu.sync_copy(x_vmem, o_hbm.at[i_vmem.at[0]])

        pltpu.emit_pipeline(
            body,
            grid=(n // SC_WINDOW,),
            in_specs=[pl.BlockSpec((SC_WINDOW, width), lambda i: (i, 0))]
            + [pl.BlockSpec((1, SC_WINDOW), lambda i: (0, i))] * n_lists,
            out_specs=[],
            core_axis_name=("core", "subcore"),
            dimension_semantics=(pltpu.PARALLEL,),
        )(x_hbm, *i_hbms)

    return scatter(src, *[idx[r].reshape(1, n) for r in range(n_lists)])


def _sc_gather_rows(table, idx):
    n = idx.shape[0]
    width = table.shape[1]

    @pl.kernel(out_type=jax.ShapeDtypeStruct((n, width), table.dtype), mesh=_sc_mesh(), scratch_types=[])
    def gather(x_hbm, i_hbm, o_hbm):
        def body(i_vmem, o_vmem):
            pltpu.sync_copy(x_hbm.at[i_vmem.at[0]], o_vmem)

        pltpu.emit_pipeline(
            body,
            grid=(n // SC_WINDOW,),
            in_specs=[pl.BlockSpec((1, SC_WINDOW), lambda i: (0, i))],
            out_specs=[pl.BlockSpec((SC_WINDOW, width), lambda i: (i, 0))],
            core_axis_name=("core", "subcore"),
            dimension_semantics=(pltpu.PARALLEL,),
        )(i_hbm, o_hbm)

    return gather(table, idx.reshape(1, n))


def _moe_sparse(x1, mod, seq_len, lw, w_gate_e, w_up_e, w_down_e, layer, *, tm, tm_e, n_parts):
    n_experts = lw["b_router"].shape[1]
    assert n_experts * TOP_K == LANE
    n_tok_tiles = x1.shape[0] // tm // n_parts
    t = n_tok_tiles * tm
    n_row_tiles = (t * TOP_K) // tm_e + n_experts
    p = n_row_tiles * tm_e
    experts = jnp.arange(n_experts, dtype=jnp.int32)

    staged = []
    for part in range(n_parts):
        tp, sel, wgt, rank, cnt = _route(x1, mod, seq_len, lw, tm=tm, tile_off=part * n_tok_tiles,
                                         n_tiles=n_tok_tiles)
        rows = tp.shape[0]
        cnt = cnt[:, 0, :].reshape(n_tok_tiles, TOP_K, n_experts).sum(axis=1)
        padded = (cnt.sum(axis=0) + tm_e - 1) // tm_e * tm_e
        group_end = jnp.cumsum(padded)
        base = (group_end - padded)[None, :] + jnp.cumsum(cnt, axis=0) - cnt
        base_t = jnp.broadcast_to(base[:, None, None, :], (n_tok_tiles, tm, 1, n_experts))
        chosen = sel[:, :TOP_K].reshape(n_tok_tiles, tm, TOP_K, 1) == experts
        pos = jnp.sum(jnp.where(chosen, base_t, 0), axis=-1).reshape(t, TOP_K) + rank[:, :TOP_K]

        n_used = group_end[-1] // tm_e
        tile_start = jnp.arange(n_row_tiles, dtype=jnp.int32) * tm_e
        tile_expert = jnp.sum(tile_start[:, None] >= group_end[None, :], axis=1).astype(jnp.int32)
        tile_expert = jnp.minimum(tile_expert, n_experts - 1)
        tile_expert = jnp.where(tile_start < group_end[-1], tile_expert, tile_expert[n_used - 1])
        meta = jnp.stack([n_used, n_used]).astype(jnp.int32)

        idx = pos.T[:, None, :] + (jnp.arange(rows, dtype=jnp.int32) * p)[None, :, None]
        xs = _sc_scatter_rows(tp.reshape(rows * t, PACK_W), idx.reshape(TOP_K, rows * t), rows * p)
        staged.append((xs.reshape(rows, p, PACK_W), tile_expert, meta, idx.reshape(-1), wgt))

    gathered = []
    for xs, tile_expert, meta, idx, wgt in staged:
        ys = _experts(xs, tile_expert, meta, w_gate_e, w_up_e, w_down_e, layer, tm=tm_e)
        yk = _sc_gather_rows(ys.reshape(-1, PACK_W), idx)
        gathered.append((yk.reshape(TOP_K, xs.shape[0], t, PACK_W), wgt))

    out = None
    for part, (yk, wgt) in enumerate(gathered):
        out = _combine(x1, mod, seq_len, lw, yk, wgt, tm=tm, tile_off=part * n_tok_tiles, prev=out)
    return out


def _rope_tables(n):
    rows = n // GRID_W
    r, col = jnp.meshgrid(jnp.arange(rows), jnp.arange(GRID_W), indexing="ij")
    r = r.reshape(-1).astype(F32)
    col = col.reshape(-1).astype(F32)
    pairs = QK_ROPE // 4
    inv = ROPE_BASE ** (-jnp.arange(pairs, dtype=F32) / pairs)
    ang = jnp.concatenate([r[:, None] * inv, col[:, None] * inv], axis=-1)
    cos, sin = jnp.cos(ang), jnp.sin(ang)
    pad = HEAD_PAD - QK_NOPE - QK_ROPE
    cos_t = jnp.concatenate([jnp.ones((n, QK_NOPE), F32), cos, cos, jnp.zeros((n, pad), F32)], axis=1)
    sin_t = jnp.concatenate([jnp.zeros((n, QK_NOPE), F32), sin, sin, jnp.zeros((n, pad), F32)], axis=1)
    return cos_t, sin_t


def _identity_tables(n):
    pad = HEAD_PAD - QK_NOPE - QK_ROPE
    cos_t = jnp.concatenate([jnp.ones((n, QK_NOPE + QK_ROPE), F32), jnp.zeros((n, pad), F32)], axis=1)
    return cos_t, jnp.zeros((n, HEAD_PAD), F32)


def _position_dft(n):
    nb = 64 if n % 64 == 0 else 1
    na = n // nb
    m = jnp.arange(n, dtype=jnp.int32)[None, :]
    ang_a = ((jnp.arange(na, dtype=jnp.int32)[:, None] * m) % na).astype(F32) * (2.0 * np.pi / na)
    ang_b = ((jnp.arange(nb, dtype=jnp.int32)[:, None] * m) % n).astype(F32) * (2.0 * np.pi / n)
    ca, sa = jnp.cos(ang_a)[:, None, :], jnp.sin(ang_a)[:, None, :]
    cb, sb = jnp.cos(ang_b)[None, :, :], jnp.sin(ang_b)[None, :, :]
    norm = 1.0 / np.sqrt(n)
    cos = ((ca * cb - sa * sb) * norm).reshape(n, n)
    sin = ((sa * cb + ca * sb) * (-norm)).reshape(n, n)
    return jnp.concatenate([cos, sin], axis=1).astype(BF16)


def _channel_dft(width):
    gc = width // FOURIER_GROUPS
    idx = (jnp.arange(gc, dtype=jnp.int32)[:, None] * jnp.arange(gc, dtype=jnp.int32)[None, :]) % gc
    ang = idx.astype(F32) * (2.0 * np.pi / gc)
    eye = jnp.eye(FOURIER_GROUPS, dtype=F32)
    norm = 1.0 / np.sqrt(gc)
    return jnp.concatenate([jnp.kron(eye, jnp.cos(ang) * norm), jnp.kron(eye, jnp.sin(ang) * norm)],
                           axis=1).astype(BF16)


def _kpe_placement():
    place = np.zeros((QK_ROPE, 2 * HEAD_PAD), np.float32)
    half = QK_ROPE // 2
    for j in range(QK_ROPE):
        place[j, QK_NOPE + j] = 1.0
    for i in range(half):
        place[half + i, HEAD_PAD + QK_NOPE + i] = -1.0
        place[i, HEAD_PAD + QK_NOPE + half + i] = 1.0
    return jnp.asarray(place, dtype=BF16)


def _w1_kernel(w_ref, place_ref, o_ref, *, kv_rank):
    w = w_ref[0]
    kv_end = kv_rank + QK_ROPE
    kpe = _dot(w[:, kv_rank:kv_end].astype(BF16), place_ref[...])
    o_ref[0] = jnp.concatenate([w[:, :kv_rank], kpe, w[:, kv_end:]], axis=1).astype(o_ref.dtype)


def _prep_w1(w_in, kv_rank):
    n_layers, d, width = w_in.shape
    out_w = width - QK_ROPE + 2 * HEAD_PAD
    tr = _tile(d, 256)
    return pl.pallas_call(
        functools.partial(_w1_kernel, kv_rank=kv_rank),
        grid=(n_layers, d // tr),
        in_specs=[pl.BlockSpec((1, tr, width), lambda l, i: (l, i, 0)),
                  pl.BlockSpec((QK_ROPE, 2 * HEAD_PAD), lambda l, i: (0, 0))],
        out_specs=pl.BlockSpec((1, tr, out_w), lambda l, i: (l, i, 0)),
        out_shape=jax.ShapeDtypeStruct((n_layers, d, out_w), BF16),
        compiler_params=_params(2),
        name="prep_w1",
    )(w_in, _kpe_placement())


def _rot_cols(w):
    half = w.shape[-1] // 2
    return jnp.concatenate([-w[..., half:], w[..., :half]], axis=-1)


def _layer_weights(l, g_pre_mix, g_post_mix, g_pre_ffn, g_post_ffn, w1, b_gate, g_q, w_uq, g_kv,
                   w_ukv, w_mla_out, conv_w, w_conv_out, w_four_out, w_out, w_router, b_router,
                   w_gate_s, w_up_s, w_down_s, dc):
    kv_rank = g_kv.shape[1]
    q_rank = g_q.shape[1]
    qk_dim = QK_NOPE + QK_ROPE
    pad = HEAD_PAD - qk_dim
    uq = w_uq[l].reshape(q_rank, N_HEADS, qk_dim) * (qk_dim ** -0.5 * np.log2(np.e))
    zq = jnp.zeros((q_rank, N_HEADS, pad), F32)
    wuq = jnp.concatenate([uq, zq], axis=-1).reshape(q_rank, N_HEADS * HEAD_PAD).astype(BF16)
    wuq_rot = jnp.concatenate([jnp.zeros((q_rank, N_HEADS, QK_NOPE), F32), _rot_cols(uq[..., QK_NOPE:]), zq],
                              axis=-1).reshape(q_rank, N_HEADS * HEAD_PAD).astype(BF16)
    ukv = w_ukv[l].reshape(kv_rank, N_HEADS, QK_NOPE + V_DIM)
    wuk = jnp.concatenate([ukv[..., :QK_NOPE], jnp.zeros((kv_rank, N_HEADS, HEAD_PAD - QK_NOPE), F32)],
                          axis=-1).reshape(kv_rank, N_HEADS * HEAD_PAD).astype(BF16)
    wuv = jnp.concatenate([ukv[..., QK_NOPE:], jnp.zeros((kv_rank, N_HEADS, HEAD_PAD - V_DIM), F32)],
                          axis=-1).reshape(kv_rank, N_HEADS * HEAD_PAD).astype(BF16)
    v_one = jnp.tile((jnp.arange(HEAD_PAD) == V_DIM).astype(F32), N_HEADS)[None]

    wr = w_router[l]
    wr_hi = wr.astype(BF16)
    wr_lo = (wr - wr_hi.astype(F32)).astype(BF16)
    return {
        "g_pre_mix": g_pre_mix[l][None], "g_post_mix": g_post_mix[l][None],
        "g_pre_ffn": g_pre_ffn[l][None], "g_post_ffn": g_post_ffn[l][None],
        "w1": w1, "layer": l, "b_gate": b_gate[l][None], "g_q": g_q[l][None], "g_kv": g_kv[l][None],
        "wuq": wuq, "wuq_rot": wuq_rot, "wuk": wuk, "wuv": wuv, "v_one": v_one, "dc": dc,
        "conv_w": conv_w[l],
        "w_mla_out": w_mla_out[l].astype(BF16), "w_conv_out": w_conv_out[l].astype(BF16),
        "w_four_out": w_four_out[l].astype(BF16), "w_out": w_out[l].astype(BF16),
        "w_router": jnp.concatenate([wr_hi, wr_lo], axis=1), "b_router": b_router[l][None],
        "w_gate_s": w_gate_s[l].astype(BF16), "w_up_s": w_up_s[l].astype(BF16),
        "w_down_s": w_down_s[l].astype(BF16),
    }


def _tile(n, pref):
    return pref if n % pref == 0 else n


def kernel(x, c, ctx, c_ctx, w_ada, b_ada, g_pre_mix, g_post_mix, g_pre_ffn, g_post_ffn, w_in, b_gate,
           g_q, w_uq, g_kv, w_ukv, w_mla_out, conv_w, w_conv_out, w_four_out, w_out, w_router, b_router,
           w_gate_e, w_up_e, w_down_e, w_gate_s, w_up_s, w_down_s):
    batch, seq, d = x.shape
    n_ctx = ctx.shape[1]
    n_layers = w_in.shape[0]
    xs = x.reshape(batch * seq, d)
    cs = ctx.reshape(batch * n_ctx, d)

    mod_rows = 16
    c_all = jnp.concatenate([c, c_ctx[None], jnp.zeros((mod_rows - batch - 1, d), F32)], axis=0)
    ada = _ada(c_all, w_ada, b_ada)

    tab_x = _rope_tables(seq)
    tm_c = _tile(n_ctx, 256)
    tab_c = _identity_tables(tm_c)
    cs_x = _position_dft(seq)
    cs_c = _position_dft(n_ctx)
    dc = _channel_dft(w_four_out.shape[1])
    w1 = _prep_w1(w_in, g_kv.shape[1])

    tm_x = _tile(seq, 512)
    tm_moe_x = _tile(seq, 1024)
    tm_moe_c = _tile(batch * n_ctx, 1024)

    for l in range(n_layers):
        last = l == n_layers - 1
        lw = _layer_weights(l, g_pre_mix, g_post_mix, g_pre_ffn, g_post_ffn, w1, b_gate, g_q, w_uq,
                            g_kv, w_ukv, w_mla_out, conv_w, w_conv_out, w_four_out, w_out, w_router,
                            b_router, w_gate_s, w_up_s, w_down_s, dc)
        mods = ada[l].reshape(mod_rows, 6, d)
        mod_x = mods[:batch]
        mod_c = mods[batch:batch + 1]

        pc = _inproj(cs, mod_c, n_ctx, lw, tab_c, kv_only=last, tm=tm_c)
        px = _inproj(xs, mod_x, seq, lw, tab_x, kv_only=False, tm=tm_x)
        o_x = _attention(px["q"], [(pc["k"], pc["v"], n_ctx), (px["k"], px["v"], seq)], batch, seq, tq=tm_x)
        f_x = _fourier(px["ab"], cs_x, batch, seq, tn=tm_x)
        x1 = _merge(xs, mod_x, seq, px, o_x, f_x, lw, tm=tm_x)
        xs = _moe_sparse(x1, mod_x, seq, lw, w_gate_e, w_up_e, w_down_e, l, tm=tm_moe_x, tm_e=512,
                         n_parts=1)
        if not last:
            o_c = _attention(pc["q"], [(pc["k"], pc["v"], n_ctx)], batch, n_ctx, tq=tm_c)
            f_c = _fourier(pc["ab"], cs_c, batch, n_ctx, tn=tm_c)
            c1 = _merge(cs, mod_c, n_ctx, pc, o_c, f_c, lw, tm=tm_c)
            cs = _moe(c1, mod_c, batch * n_ctx, lw, w_gate_e, w_up_e, w_down_e, l, tm=tm_moe_c)
    return xs.reshape(batch, seq, d)
```

```python
import functools

import numpy as np
import jax
import jax.numpy as jnp
from jax import lax
from jax.experimental import pallas as pl
from jax.experimental.pallas import tpu as pltpu
from jax.experimental.pallas import tpu_sc as plsc

N_HEADS = 8
QK_NOPE = 64
QK_ROPE = 32
V_DIM = 64
GRID_W = 64
ROPE_BASE = 10000.0
FOURIER_GROUPS = 4
TOP_K = 4
ROUTED_SCALE = 2.5
N_BRANCHES = 3
EPS = 1e-6

LANE = 128
HEAD_PAD = LANE
VMEM_LIMIT = 56 * 1024 * 1024
PACK_W = 256
SC_WINDOW = 128

F32 = jnp.float32
BF16 = jnp.bfloat16


def _rms(x, g):
    return x * lax.rsqrt(jnp.mean(x * x, axis=-1, keepdims=True) + EPS) * g


def _sigmoid(x):
    return 1.0 / (1.0 + jnp.exp(-x))


def _dot(a, b):
    return jnp.dot(a, b, preferred_element_type=F32)


def _dot_t(a, b_t):
    return lax.dot_general(a, b_t, (((1,), (1,)), ((), ())), preferred_element_type=F32)


def _resident(shape):
    nd = len(shape)
    return pl.BlockSpec(shape, lambda *_: (0,) * nd, pipeline_mode=pl.Buffered(1))


def _params(n_grid):
    return pltpu.CompilerParams(dimension_semantics=("arbitrary",) * n_grid,
                                vmem_limit_bytes=VMEM_LIMIT)


def _ada_kernel(c_ref, w_ref, b_ref, o_ref):
    c = c_ref[...]
    a = (c * _sigmoid(c)).astype(BF16)
    o_ref[0] = _dot(a, w_ref[0].astype(BF16)) + b_ref[0]


def _ada(c_all, w_ada, b_ada):
    n_layers, d, n_out = w_ada.shape
    rows = c_all.shape[0]
    tn = 1536
    return pl.pallas_call(
        _ada_kernel,
        grid=(n_layers, n_out // tn),
        in_specs=[
            pl.BlockSpec((rows, d), lambda l, j: (0, 0)),
            pl.BlockSpec((1, d, tn), lambda l, j: (l, 0, j)),
            pl.BlockSpec((1, 1, tn), lambda l, j: (l, 0, j)),
        ],
        out_specs=pl.BlockSpec((1, rows, tn), lambda l, j: (l, 0, j)),
        out_shape=jax.ShapeDtypeStruct((n_layers, rows, n_out), F32),
        compiler_params=_params(2),
        name="ada",
    )(c_all, w_ada, b_ada.reshape(n_layers, 1, n_out))


def _inproj_kernel(*refs, kv_only, kv_rank, q_rank, conv_w, four_w, d_model):
    if kv_only:
        (x_ref, mod_ref, gpre_ref, w1_ref, gkv_ref, wuk_ref, wuv_ref, vone_ref, cos_ref, sin_ref,
         k_ref, v_ref) = refs
    else:
        (x_ref, mod_ref, gpre_ref, w1_ref, gkv_ref, wuk_ref, wuv_ref, vone_ref, cos_ref, sin_ref,
         bg_ref, gq_ref, wuq_ref, wuqr_ref, dc_ref,
         k_ref, v_ref, q_ref, cb_ref, cc_ref, cu_ref, ab_ref, gate_ref) = refs

    x = x_ref[...]
    shift = mod_ref[0, 0:1, :]
    scale = mod_ref[0, 1:2, :]
    h = (_rms(x, gpre_ref[...]) * (1.0 + scale) + shift).astype(BF16)
    cos = cos_ref[...]
    sin = sin_ref[...]

    o_kpe = kv_rank
    o_rot = o_kpe + HEAD_PAD
    o_q = o_rot + HEAD_PAD
    p = _dot_t(h, w1_ref[0, 0:o_q, :])
    ckv = _rms(p[:, 0:kv_rank], gkv_ref[...]).astype(BF16)
    kpe = p[:, o_kpe:o_rot] * cos + p[:, o_rot:o_q] * sin
    k = _dot(ckv, wuk_ref[...]) + jnp.concatenate([kpe] * N_HEADS, axis=1)
    k_ref[...] = k.astype(k_ref.dtype)
    v_ref[...] = (_dot(ckv, wuv_ref[...]) + vone_ref[...]).astype(v_ref.dtype)
    if kv_only:
        return

    o_cb = o_q + q_rank
    cq = _rms(_dot_t(h, w1_ref[0, o_q:o_cb, :]), gq_ref[...]).astype(BF16)
    cos_h = jnp.concatenate([cos] * N_HEADS, axis=1)
    sin_h = jnp.concatenate([sin] * N_HEADS, axis=1)
    q = _dot(cq, wuq_ref[...]) * cos_h + _dot(cq, wuqr_ref[...]) * sin_h
    q_ref[...] = q.astype(q_ref.dtype)

    o_cc = o_cb + conv_w
    o_cu = o_cc + conv_w
    o_four = o_cu + conv_w
    cb_ref[...] = _dot_t(h, w1_ref[0, o_cb:o_cc, :]).astype(cb_ref.dtype)
    cc_ref[...] = _dot_t(h, w1_ref[0, o_cc:o_cu, :]).astype(cc_ref.dtype)
    cu_ref[...] = _dot_t(h, w1_ref[0, o_cu:o_four, :]).astype(cu_ref.dtype)

    o_gate = o_four + four_w
    uf = _dot_t(h, w1_ref[0, o_four:o_gate, :]).astype(BF16)
    ab_ref[...] = _dot(uf, dc_ref[...]).astype(ab_ref.dtype)

    for j in range(N_BRANCHES):
        lo = o_gate + j * d_model
        z = _dot_t(h, w1_ref[0, lo:lo + d_model, :]) + bg_ref[:, j * d_model:(j + 1) * d_model]
        gate_ref[:, j * d_model:(j + 1) * d_model] = _sigmoid(z).astype(gate_ref.dtype)


def _inproj(xs, mod, seq_len, lw, tables, *, kv_only, tm):
    t, d = xs.shape
    nb = mod.shape[0]
    tiles_per_seq = seq_len // tm
    cos_t, sin_t = tables
    table_tiles = cos_t.shape[0] // tm
    kv_rank = lw["g_kv"].shape[1]
    q_rank = lw["g_q"].shape[1]
    conv_w = lw["conv_w"].shape[1]
    four_w = lw["dc"].shape[0]
    n_k = N_HEADS * HEAD_PAD

    def row(i):
        return (i, 0)

    def mod_map(i):
        return ((i // tiles_per_seq) % nb, 0, 0)

    def tab_map(i):
        return (i % table_tiles, 0)

    w1, layer = lw["w1"], lw["layer"]
    w1_rows = kv_rank + 2 * HEAD_PAD if kv_only else w1.shape[1]
    in_specs = [
        pl.BlockSpec((tm, d), row),
        pl.BlockSpec((1,) + mod.shape[1:], mod_map),
        _resident((1, d)),
        pl.BlockSpec((1, w1_rows, d), lambda i: (layer, 0, 0), pipeline_mode=pl.Buffered(1)),
        _resident((1, kv_rank)),
        _resident(lw["wuk"].shape),
        _resident(lw["wuv"].shape),
        _resident(lw["v_one"].shape),
        pl.BlockSpec((tm, HEAD_PAD), tab_map),
        pl.BlockSpec((tm, HEAD_PAD), tab_map),
    ]
    args = [xs, mod, lw["g_pre_mix"], w1, lw["g_kv"], lw["wuk"], lw["wuv"], lw["v_one"], cos_t, sin_t]
    out_shape = [jax.ShapeDtypeStruct((t, n_k), BF16), jax.ShapeDtypeStruct((t, n_k), BF16)]
    out_specs = [pl.BlockSpec((tm, n_k), row), pl.BlockSpec((tm, n_k), row)]
    if not kv_only:
        in_specs += [
            _resident(lw["b_gate"].shape),
            _resident((1, q_rank)),
            _resident(lw["wuq"].shape),
            _resident(lw["wuq_rot"].shape),
            _resident(lw["dc"].shape),
        ]
        args += [lw["b_gate"], lw["g_q"], lw["wuq"], lw["wuq_rot"], lw["dc"]]
        widths = [n_k, conv_w, conv_w, conv_w, 2 * four_w, N_BRANCHES * d]
        out_shape += [jax.ShapeDtypeStruct((t, w), BF16) for w in widths]
        out_specs += [pl.BlockSpec((tm, w), row) for w in widths]
    outs = pl.pallas_call(
        functools.partial(_inproj_kernel, kv_only=kv_only, kv_rank=kv_rank, q_rank=q_rank,
                          conv_w=conv_w, four_w=four_w, d_model=d),
        grid=(t // tm,),
        in_specs=in_specs,
        out_specs=out_specs,
        out_shape=out_shape,
        compiler_params=_params(1),
        name="inproj_kv" if kv_only else "inproj",
    )(*args)
    names = ["k", "v", "q", "cb", "cc", "cu", "ab", "gate"]
    return dict(zip(names, outs))


def _attn_kernel(*refs, n_seg):
    q_ref = refs[0]
    o_ref = refs[-1]
    nt = (((1,), (1,)), ((), ()))
    outs = []
    for hh in range(N_HEADS):
        head = slice(hh * HEAD_PAD, (hh + 1) * HEAD_PAD)
        qh = q_ref[:, head]
        s = [lax.dot_general(qh, refs[1 + 2 * i][:, head], nt, preferred_element_type=F32)
             for i in range(n_seg)]
        m = functools.reduce(jnp.maximum, [jnp.max(si, axis=-1, keepdims=True) for si in s])
        acc = functools.reduce(jnp.add, [
            _dot(jnp.exp2((s[i] - m).astype(BF16)), refs[2 + 2 * i][:, head]) for i in range(n_seg)])
        outs.append(acc[:, 0:V_DIM] / acc[:, V_DIM:V_DIM + 1])
    o_ref[...] = jnp.concatenate(outs, axis=1).astype(o_ref.dtype)


def _attention(q, segs, batch, seq_q, *, tq):
    t = q.shape[0]
    qt = seq_q // tq
    n_k = N_HEADS * HEAD_PAD
    in_specs = [pl.BlockSpec((tq, n_k), lambda b, j: (b * qt + j, 0))]
    args = [q]
    for k, v, m in segs:
        in_specs.append(pl.BlockSpec((m, n_k), lambda b, j: (b, 0)))
        in_specs.append(pl.BlockSpec((m, n_k), lambda b, j: (b, 0)))
        args += [k, v]
    return pl.pallas_call(
        functools.partial(_attn_kernel, n_seg=len(segs)),
        grid=(batch, qt),
        in_specs=in_specs,
        out_specs=pl.BlockSpec((tq, N_HEADS * V_DIM), lambda b, j: (b * qt + j, 0)),
        out_shape=jax.ShapeDtypeStruct((t, N_HEADS * V_DIM), BF16),
        compiler_params=_params(2),
        name="attention",
    )(*args)


def _four_kernel(cs_ref, ab_ref, o_ref, *, n, fw):
    o = _dot(cs_ref[:, 0:n], ab_ref[:, 0:fw]) + _dot(cs_ref[:, n:2 * n], ab_ref[:, fw:2 * fw])
    o_ref[...] = o.astype(o_ref.dtype)


def _fourier(ab, cs, batch, seq_len, *, tn):
    t, fw2 = ab.shape
    fw = fw2 // 2
    nt = seq_len // tn
    return pl.pallas_call(
        functools.partial(_four_kernel, n=seq_len, fw=fw),
        grid=(batch, nt),
        in_specs=[
            pl.BlockSpec((tn, 2 * seq_len), lambda b, j: (j, 0)),
            pl.BlockSpec((seq_len, fw2), lambda b, j: (b, 0)),
        ],
        out_specs=pl.BlockSpec((tn, fw), lambda b, j: (b * nt + j, 0)),
        out_shape=jax.ShapeDtypeStruct((t, fw), BF16),
        compiler_params=_params(2),
        name="fourier",
    )(cs, ab)


def _merge_kernel(x_ref, mod_ref, o_ref, cb_ref, cc_ref, cu_ref, ccp_ref, cup_ref, ccn_ref,
                  cun_ref, f_ref, gate_ref, convw_ref, wmo_ref, wco_ref, wfo_ref, wout_ref,
                  gpost_ref, out_ref, pad_ref, *, tiles_per_seq, tm, d_model):
    i = pl.program_id(0)
    pos = i % tiles_per_seq
    has_prev = (pos > 0).astype(F32)
    has_next = (pos < tiles_per_seq - 1).astype(F32)
    pad_ref[0:8, :] = ccp_ref[...].astype(F32) * cup_ref[...].astype(F32) * has_prev
    pad_ref[8:8 + tm, :] = cc_ref[...].astype(F32) * cu_ref[...].astype(F32)
    pad_ref[8 + tm:16 + tm, :] = ccn_ref[...].astype(F32) * cun_ref[...].astype(F32) * has_next
    conv = (pad_ref[7:7 + tm, :] * convw_ref[0:1, :] + pad_ref[8:8 + tm, :] * convw_ref[1:2, :]
            + pad_ref[9:9 + tm, :] * convw_ref[2:3, :])
    y_conv = _dot((cb_ref[...].astype(F32) * conv).astype(BF16), wco_ref[...])
    y_attn = _dot(o_ref[...], wmo_ref[...])
    y_four = _dot(f_ref[...], wfo_ref[...])
    d = d_model
    merged = (gate_ref[:, 0:d].astype(F32) * y_attn + gate_ref[:, d:2 * d].astype(F32) * y_conv
              + gate_ref[:, 2 * d:3 * d].astype(F32) * y_four)
    y = _dot(merged.astype(BF16), wout_ref[...])
    g1 = mod_ref[0, 2:3, :]
    out_ref[...] = x_ref[...] + g1 * _rms(y, gpost_ref[...])


def _merge(xs, mod, seq_len, pr, o, four, lw, *, tm):
    t, d = xs.shape
    nb = mod.shape[0]
    tiles_per_seq = seq_len // tm
    cw = lw["conv_w"].shape[1]
    fw = four.shape[1]
    hb = tm // 8
    last_hb = t // 8 - 1

    def row(i):
        return (i, 0)

    def prev_map(i):
        return (jnp.maximum(i * hb - 1, 0), 0)

    def next_map(i):
        return (jnp.minimum((i + 1) * hb, last_hb), 0)

    in_specs = [
        pl.BlockSpec((tm, d), row),
        pl.BlockSpec((1,) + mod.shape[1:], lambda i: ((i // tiles_per_seq) % nb, 0, 0)),
        pl.BlockSpec((tm, o.shape[1]), row),
        pl.BlockSpec((tm, cw), row),
        pl.BlockSpec((tm, cw), row),
        pl.BlockSpec((tm, cw), row),
        pl.BlockSpec((8, cw), prev_map),
        pl.BlockSpec((8, cw), prev_map),
        pl.BlockSpec((8, cw), next_map),
        pl.BlockSpec((8, cw), next_map),
        pl.BlockSpec((tm, fw), row),
        pl.BlockSpec((tm, N_BRANCHES * d), row),
        _resident(lw["conv_w"].shape),
        _resident(lw["w_mla_out"].shape),
        _resident(lw["w_conv_out"].shape),
        _resident(lw["w_four_out"].shape),
        _resident(lw["w_out"].shape),
        _resident((1, d)),
    ]
    return pl.pallas_call(
        functools.partial(_merge_kernel, tiles_per_seq=tiles_per_seq, tm=tm, d_model=d),
        grid=(t // tm,),
        in_specs=in_specs,
        out_specs=pl.BlockSpec((tm, d), row),
        out_shape=jax.ShapeDtypeStruct((t, d), F32),
        scratch_shapes=[pltpu.VMEM((tm + 16, cw), F32)],
        compiler_params=_params(1),
        name="merge",
    )(xs, mod, o, pr["cb"], pr["cc"], pr["cu"], pr["cc"], pr["cu"], pr["cc"], pr["cu"], four,
      pr["gate"], lw["conv_w"], lw["w_mla_out"], lw["w_conv_out"], lw["w_four_out"], lw["w_out"],
      lw["g_post_mix"])


def _moe_kernel(x_ref, mod_ref, gpre_ref, gpost_ref, wr_ref, br_ref, wgs_ref, wus_ref, wds_ref,
                wge_ref, wue_ref, wde_ref, out_ref, t_ref, comb_ref, acc_ref, *, n_experts):
    e = pl.program_id(1)

    @pl.when(e == 0)
    def _():
        shift = mod_ref[0, 3:4, :]
        scale = mod_ref[0, 4:5, :]
        t = _rms(x_ref[...], gpre_ref[...]) * (1.0 + scale) + shift
        t_hi = t.astype(BF16)
        t_lo = (t - t_hi.astype(F32)).astype(BF16)
        t_ref[...] = t_hi
        hh = _dot(t_hi, wr_ref[...])
        logits = hh[:, 0:n_experts] + hh[:, n_experts:2 * n_experts] + _dot(t_lo, wr_ref[:, 0:n_experts])
        scores = _sigmoid(logits)
        work = scores + br_ref[...]
        lane = lax.broadcasted_iota(jnp.int32, scores.shape, 1)
        comb = jnp.zeros_like(scores)
        for _ in range(TOP_K):
            best = jnp.max(work, axis=-1, keepdims=True)
            first = jnp.min(jnp.where(work == best, lane, n_experts), axis=-1, keepdims=True)
            hit = lane == first
            comb = jnp.where(hit, scores, comb)
            work = jnp.where(hit, -jnp.inf, work)
        comb_ref[...] = comb / jnp.sum(comb, axis=-1, keepdims=True) * ROUTED_SCALE
        gate = _dot(t_hi, wgs_ref[...])
        act = (gate * _sigmoid(gate) * _dot(t_hi, wus_ref[...])).astype(BF16)
        acc_ref[...] = _dot(act, wds_ref[...])

    t_hi = t_ref[...]
    gate = _dot(t_hi, wge_ref[0, 0].astype(BF16))
    up = _dot(t_hi, wue_ref[0, 0].astype(BF16))
    lane = lax.broadcasted_iota(jnp.int32, comb_ref.shape, 1)
    w_e = jnp.sum(jnp.where(lane == e, comb_ref[...], 0.0), axis=-1, keepdims=True)
    act = (gate * _sigmoid(gate) * up * w_e).astype(BF16)
    acc_ref[...] += _dot(act, wde_ref[0, 0].astype(BF16))

    @pl.when(e == n_experts - 1)
    def _():
        g2 = mod_ref[0, 5:6, :]
        out_ref[...] = x_ref[...] + g2 * _rms(acc_ref[...], gpost_ref[...])


def _moe(xs, mod, seq_len, lw, w_gate_e, w_up_e, w_down_e, layer, *, tm):
    t, d = xs.shape
    nb = mod.shape[0]
    tiles_per_seq = max(seq_len // tm, 1)
    n_experts, _, f = w_gate_e.shape[1:]
    in_specs = [
        pl.BlockSpec((tm, d), lambda i, e: (i, 0)),
        pl.BlockSpec((1,) + mod.shape[1:], lambda i, e: ((i // tiles_per_seq) % nb, 0, 0)),
        _resident((1, d)),
        _resident((1, d)),
        _resident(lw["w_router"].shape),
        _resident((1, n_experts)),
        _resident(lw["w_gate_s"].shape),
        _resident(lw["w_up_s"].shape),
        _resident(lw["w_down_s"].shape),
        pl.BlockSpec((1, 1, d, f), lambda i, e: (layer, e, 0, 0)),
        pl.BlockSpec((1, 1, d, f), lambda i, e: (layer, e, 0, 0)),
        pl.BlockSpec((1, 1, f, d), lambda i, e: (layer, e, 0, 0)),
    ]
    return pl.pallas_call(
        functools.partial(_moe_kernel, n_experts=n_experts),
        grid=(t // tm, n_experts),
        in_specs=in_specs,
        out_specs=pl.BlockSpec((tm, d), lambda i, e: (i, 0)),
        out_shape=jax.ShapeDtypeStruct((t, d), F32),
        scratch_shapes=[pltpu.VMEM((tm, d), BF16), pltpu.VMEM((tm, n_experts), F32),
                        pltpu.VMEM((tm, d), F32)],
        compiler_params=_params(2),
        name="moe",
    )(xs, mod, lw["g_pre_ffn"], lw["g_post_ffn"], lw["w_router"], lw["b_router"], lw["w_gate_s"],
      lw["w_up_s"], lw["w_down_s"], w_gate_e, w_up_e, w_down_e)


def _pack_rows(v):
    bits = lax.bitcast_convert_type(v.astype(BF16).astype(F32), jnp.uint32)
    rows = []
    for j in range(v.shape[1] // (2 * PACK_W)):
        lo = bits[:, (2 * j) * PACK_W:(2 * j + 1) * PACK_W]
        hi = bits[:, (2 * j + 1) * PACK_W:(2 * j + 2) * PACK_W]
        rows.append(lax.bitcast_convert_type((hi & jnp.uint32(0xFFFF0000)) | (lo >> 16), jnp.int32))
    return rows


def _unpack_rows(rows):
    parts = []
    for r in rows:
        u = lax.bitcast_convert_type(r, jnp.uint32)
        parts.append(lax.bitcast_convert_type(u << 16, F32))
        parts.append(lax.bitcast_convert_type(u & jnp.uint32(0xFFFF0000), F32))
    return jnp.concatenate(parts, axis=1)


def _route_kernel(x_ref, mod_ref, gpre_ref, wr_ref, br_ref, tri_ref,
                  tp_ref, sel_ref, wgt_ref, rank_ref, cnt_ref, *, n_experts):
    shift = mod_ref[0, 3:4, :]
    scale = mod_ref[0, 4:5, :]
    t = _rms(x_ref[...], gpre_ref[...]) * (1.0 + scale) + shift
    t_hi = t.astype(BF16)
    t_lo = (t - t_hi.astype(F32)).astype(BF16)
    for j, r in enumerate(_pack_rows(t)):
        tp_ref[j] = r
    hh = _dot(t_hi, wr_ref[...])
    logits = hh[:, 0:n_experts] + hh[:, n_experts:2 * n_experts] + _dot(t_lo, wr_ref[:, 0:n_experts])
    scores = _sigmoid(logits)
    work = scores + br_ref[...]
    lane = lax.broadcasted_iota(jnp.int32, scores.shape, 1)
    wide = lax.broadcasted_iota(jnp.int32, (scores.shape[0], LANE), 1)
    firsts, picked, hits = [], [], []
    for k in range(TOP_K):
        best = jnp.max(work, axis=-1, keepdims=True)
        first = jnp.min(jnp.where(work == best, lane, n_experts), axis=-1, keepdims=True)
        hit = lane == first
        firsts.append(first)
        picked.append(jnp.sum(jnp.where(hit, scores, 0.0), axis=-1, keepdims=True))
        hits.append(wide == first + k * n_experts)
        work = jnp.where(hit, -jnp.inf, work)
    total = functools.reduce(jnp.add, picked)
    onehot = functools.reduce(jnp.add, [jnp.where(h, 1.0, 0.0) for h in hits])
    earlier = _dot(tri_ref[...], onehot.astype(BF16))
    col = jnp.sum(onehot, axis=0, keepdims=True)
    col8 = jnp.broadcast_to(col, (8, LANE))
    lane8 = lax.broadcasted_iota(jnp.int32, (8, LANE), 1)
    before = jnp.zeros((8, LANE), F32)
    for s in range(1, TOP_K):
        before = before + jnp.where(lane8 >= s * n_experts, pltpu.roll(col8, s * n_experts, 1), 0.0)
    ahead = earlier + before[0:1, :]
    sel = jnp.zeros(wide.shape, jnp.int32)
    wgt = jnp.zeros(wide.shape, F32)
    rank = jnp.zeros(wide.shape, F32)
    for k in range(TOP_K):
        sel = jnp.where(wide == k, firsts[k], sel)
        wgt = jnp.where(wide == k, picked[k] / total * ROUTED_SCALE, wgt)
        rank = jnp.where(wide == k, jnp.sum(jnp.where(hits[k], ahead, 0.0), axis=-1, keepdims=True), rank)
    sel_ref[...] = sel
    wgt_ref[...] = wgt
    rank_ref[...] = rank.astype(jnp.int32)
    cnt_ref[0] = col8.astype(jnp.int32)


def _route(x1, mod, seq_len, lw, *, tm, tile_off, n_tiles):
    d = x1.shape[1]
    t = n_tiles * tm
    nb = mod.shape[0]
    tiles_per_seq = max(seq_len // tm, 1)
    n_experts = lw["b_router"].shape[1]
    rows = d // (2 * PACK_W)
    tri = jnp.tri(tm, tm, -1, dtype=BF16)
    row = lambda i: (i, 0)
    return pl.pallas_call(
        functools.partial(_route_kernel, n_experts=n_experts),
        grid=(n_tiles,),
        in_specs=[
            pl.BlockSpec((tm, d), lambda i: (i + tile_off, 0)),
            pl.BlockSpec((1,) + mod.shape[1:], lambda i: (((i + tile_off) // tiles_per_seq) % nb, 0, 0)),
            _resident((1, d)),
            _resident(lw["w_router"].shape),
            _resident((1, n_experts)),
            _resident((tm, tm)),
        ],
        out_specs=[
            pl.BlockSpec((rows, tm, PACK_W), lambda i: (0, i, 0)),
            pl.BlockSpec((tm, LANE), row),
            pl.BlockSpec((tm, LANE), row),
            pl.BlockSpec((tm, LANE), row),
            pl.BlockSpec((1, 8, LANE), lambda i: (i, 0, 0)),
        ],
        out_shape=[
            jax.ShapeDtypeStruct((rows, t, PACK_W), jnp.int32),
            jax.ShapeDtypeStruct((t, LANE), jnp.int32),
            jax.ShapeDtypeStruct((t, LANE), F32),
            jax.ShapeDtypeStruct((t, LANE), jnp.int32),
            jax.ShapeDtypeStruct((n_tiles, 8, LANE), jnp.int32),
        ],
        compiler_params=_params(1),
        name="moe_route",
    )(x1, mod, lw["g_pre_ffn"], lw["w_router"], lw["b_router"], tri)


def _expert_kernel(te_ref, meta_ref, xs_ref, wg_ref, wu_ref, wd_ref, ys_ref, wg_sc, wu_sc, wd_sc):
    i = pl.program_id(0)
    live = i < meta_ref[0]
    new_expert = (i == 0) | (te_ref[i] != te_ref[jnp.maximum(i - 1, 0)])

    @pl.when(live & new_expert)
    def _():
        wg_sc[...] = wg_ref[0, 0].astype(BF16)
        wu_sc[...] = wu_ref[0, 0].astype(BF16)
        wd_sc[...] = wd_ref[0, 0].astype(BF16)

    @pl.when(live)
    def _():
        x = _unpack_rows([xs_ref[j] for j in range(xs_ref.shape[0])]).astype(BF16)
        gate = _dot(x, wg_sc[...])
        act = (gate * _sigmoid(gate) * _dot(x, wu_sc[...])).astype(BF16)
        for j, r in enumerate(_pack_rows(_dot(act, wd_sc[...]))):
            ys_ref[j] = r


def _experts(xs, tile_expert, meta, w_gate_e, w_up_e, w_down_e, layer, *, tm):
    rows, p, _ = xs.shape
    n_experts, d, f = w_gate_e.shape[1:]

    def slot(i, te, meta):
        return (0, jnp.minimum(i, meta[0] - 1), 0)

    grid_spec = pltpu.PrefetchScalarGridSpec(
        num_scalar_prefetch=2,
        grid=(p // tm,),
        in_specs=[
            pl.BlockSpec((rows, tm, PACK_W), slot),
            pl.BlockSpec((1, 1, d, f), lambda i, te, meta: (layer, te[i], 0, 0)),
            pl.BlockSpec((1, 1, d, f), lambda i, te, meta: (layer, te[i], 0, 0)),
            pl.BlockSpec((1, 1, f, d), lambda i, te, meta: (layer, te[i], 0, 0)),
        ],
        out_specs=pl.BlockSpec((rows, tm, PACK_W), slot),
        scratch_shapes=[pltpu.VMEM((d, f), BF16), pltpu.VMEM((d, f), BF16), pltpu.VMEM((f, d), BF16)],
    )
    return pl.pallas_call(
        _expert_kernel,
        grid_spec=grid_spec,
        out_shape=jax.ShapeDtypeStruct(xs.shape, jnp.int32),
        compiler_params=_params(1),
        name="moe_experts",
    )(tile_expert, meta, xs, w_gate_e, w_up_e, w_down_e)


def _combine_kernel(x_ref, mod_ref, gpre_ref, gpost_ref, wgs_ref, wus_ref, wds_ref, yk_ref, wgt_ref,
                    *rest):
    out_ref = rest[-1]
    shift = mod_ref[0, 3:4, :]
    scale = mod_ref[0, 4:5, :]
    x = x_ref[...]
    t_hi = (_rms(x, gpre_ref[...]) * (1.0 + scale) + shift).astype(BF16)
    gate = _dot(t_hi, wgs_ref[...])
    act = (gate * _sigmoid(gate) * _dot(t_hi, wus_ref[...])).astype(BF16)
    acc = _dot(act, wds_ref[...])
    for k in range(TOP_K):
        y = _unpack_rows([yk_ref[k, j] for j in range(yk_ref.shape[1])])
        acc = acc + wgt_ref[:, k:k + 1] * y
    g2 = mod_ref[0, 5:6, :]
    out_ref[...] = x + g2 * _rms(acc, gpost_ref[...])


def _combine(x1, mod, seq_len, lw, yk, wgt, *, tm, tile_off, prev):
    t, d = x1.shape
    nb = mod.shape[0]
    tiles_per_seq = max(seq_len // tm, 1)
    rows = yk.shape[1]
    n_tiles = yk.shape[2] // tm
    glob = lambda i: (i + tile_off, 0)
    in_specs = [
        pl.BlockSpec((tm, d), glob),
        pl.BlockSpec((1,) + mod.shape[1:], lambda i: (((i + tile_off) // tiles_per_seq) % nb, 0, 0)),
        _resident((1, d)),
        _resident((1, d)),
        _resident(lw["w_gate_s"].shape),
        _resident(lw["w_up_s"].shape),
        _resident(lw["w_down_s"].shape),
        pl.BlockSpec((TOP_K, rows, tm, PACK_W), lambda i: (0, 0, i, 0)),
        pl.BlockSpec((tm, LANE), lambda i: (i, 0)),
    ]
    args = [x1, mod, lw["g_pre_ffn"], lw["g_post_ffn"], lw["w_gate_s"], lw["w_up_s"], lw["w_down_s"], yk, wgt]
    aliases = {}
    if prev is not None:
        in_specs.append(pl.BlockSpec(memory_space=pl.ANY))
        aliases = {len(args): 0}
        args.append(prev)
    return pl.pallas_call(
        _combine_kernel,
        grid=(n_tiles,),
        in_specs=in_specs,
        out_specs=pl.BlockSpec((tm, d), glob),
        out_shape=jax.ShapeDtypeStruct((t, d), F32),
        input_output_aliases=aliases,
        compiler_params=_params(1),
        name="moe_combine",
    )(*args)


def _sc_mesh():
    return plsc.VectorSubcoreMesh(core_axis_name="core", subcore_axis_name="subcore")


def _sc_scatter_rows(src, idx, n_out):
    n_lists, n = idx.shape
    width = src.shape[1]

    @pl.kernel(out_type=jax.ShapeDtypeStruct((n_out, width), src.dtype), mesh=_sc_mesh(), scratch_types=[])
    def scatter(x_hbm, *refs):
        i_hbms, o_hbm = refs[:n_lists], refs[n_lists]

        def body(x_vmem, *i_vmems):
            for i_vmem in i_vmems:
                pltpu.sync_copy(x_vmem, o_hbm.at[i_vmem.at[0]])

        pltpu.emit_pipeline(
            body,
            grid=(n // SC_WINDOW,),
            in_specs=[pl.BlockSpec((SC_WINDOW, width), lambda i: (i, 0))]
            + [pl.BlockSpec((1, SC_WINDOW), lambda i: (0, i))] * n_lists,
            out_specs=[],
            core_axis_name=("core", "subcore"),
            dimension_semantics=(pltpu.PARALLEL,),
        )(x_hbm, *i_hbms)

    return scatter(src, *[idx[r].reshape(1, n) for r in range(n_lists)])


def _sc_gather_rows(table, idx):
    n = idx.shape[0]
    width = table.shape[1]

    @pl.kernel(out_type=jax.ShapeDtypeStruct((n, width), table.dtype), mesh=_sc_mesh(), scratch_types=[])
    def gather(x_hbm, i_hbm, o_hbm):
        def body(i_vmem, o_vmem):
            pltpu.sync_copy(x_hbm.at[i_vmem.at[0]], o_vmem)

        pltpu.emit_pipeline(
            body,
            grid=(n // SC_WINDOW,),
            in_specs=[pl.BlockSpec((1, SC_WINDOW), lambda i: (0, i))],
            out_specs=[pl.BlockSpec((SC_WINDOW, width), lambda i: (i, 0))],
            core_axis_name=("core", "subcore"),
            dimension_semantics=(pltpu.PARALLEL,),
        )(i_hbm, o_hbm)

    return gather(table, idx.reshape(1, n))


def _moe_sparse(x1, mod, seq_len, lw, w_gate_e, w_up_e, w_down_e, layer, *, tm, tm_e, n_parts):
    n_experts = lw["b_router"].shape[1]
    assert n_experts * TOP_K == LANE
    n_tok_tiles = x1.shape[0] // tm // n_parts
    t = n_tok_tiles * tm
    n_row_tiles = (t * TOP_K) // tm_e + n_experts
    p = n_row_tiles * tm_e
    experts = jnp.arange(n_experts, dtype=jnp.int32)

    staged = []
    for part in range(n_parts):
        tp, sel, wgt, rank, cnt = _route(x1, mod, seq_len, lw, tm=tm, tile_off=part * n_tok_tiles,
                                         n_tiles=n_tok_tiles)
        rows = tp.shape[0]
        cnt = cnt[:, 0, :].reshape(n_tok_tiles, TOP_K, n_experts).sum(axis=1)
        padded = (cnt.sum(axis=0) + tm_e - 1) // tm_e * tm_e
        group_end = jnp.cumsum(padded)
        base = (group_end - padded)[None, :] + jnp.cumsum(cnt, axis=0) - cnt
        base_t = jnp.broadcast_to(base[:, None, None, :], (n_tok_tiles, tm, 1, n_experts))
        chosen = sel[:, :TOP_K].reshape(n_tok_tiles, tm, TOP_K, 1) == experts
        pos = jnp.sum(jnp.where(chosen, base_t, 0), axis=-1).reshape(t, TOP_K) + rank[:, :TOP_K]

        n_used = group_end[-1] // tm_e
        tile_start = jnp.arange(n_row_tiles, dtype=jnp.int32) * tm_e
        tile_expert = jnp.sum(tile_start[:, None] >= group_end[None, :], axis=1).astype(jnp.int32)
        tile_expert = jnp.minimum(tile_expert, n_experts - 1)
        tile_expert = jnp.where(tile_start < group_end[-1], tile_expert, tile_expert[n_used - 1])
        meta = jnp.stack([n_used, n_used]).astype(jnp.int32)

        idx = pos.T[:, None, :] + (jnp.arange(rows, dtype=jnp.int32) * p)[None, :, None]
        xs = _sc_scatter_rows(tp.reshape(rows * t, PACK_W), idx.reshape(TOP_K, rows * t), rows * p)
        staged.append((xs.reshape(rows, p, PACK_W), tile_expert, meta, idx.reshape(-1), wgt))

    gathered = []
    for xs, tile_expert, meta, idx, wgt in staged:
        ys = _experts(xs, tile_expert, meta, w_gate_e, w_up_e, w_down_e, layer, tm=tm_e)
        yk = _sc_gather_rows(ys.reshape(-1, PACK_W), idx)
        gathered.append((yk.reshape(TOP_K, xs.shape[0], t, PACK_W), wgt))

    out = None
    for part, (yk, wgt) in enumerate(gathered):
        out = _combine(x1, mod, seq_len, lw, yk, wgt, tm=tm, tile_off=part * n_tok_tiles, prev=out)
    return out


def _rope_tables(n):
    rows = n // GRID_W
    r, col = jnp.meshgrid(jnp.arange(rows), jnp.arange(GRID_W), indexing="ij")
    r = r.reshape(-1).astype(F32)
    col = col.reshape(-1).astype(F32)
    pairs = QK_ROPE // 4
    inv = ROPE_BASE ** (-jnp.arange(pairs, dtype=F32) / pairs)
    ang = jnp.concatenate([r[:, None] * inv, col[:, None] * inv], axis=-1)
    cos, sin = jnp.cos(ang), jnp.sin(ang)
    pad = HEAD_PAD - QK_NOPE - QK_ROPE
    cos_t = jnp.concatenate([jnp.ones((n, QK_NOPE), F32), cos, cos, jnp.zeros((n, pad), F32)], axis=1)
    sin_t = jnp.concatenate([jnp.zeros((n, QK_NOPE), F32), sin, sin, jnp.zeros((n, pad), F32)], axis=1)
    return cos_t, sin_t


def _identity_tables(n):
    pad = HEAD_PAD - QK_NOPE - QK_ROPE
    cos_t = jnp.concatenate([jnp.ones((n, QK_NOPE + QK_ROPE), F32), jnp.zeros((n, pad), F32)], axis=1)
    return cos_t, jnp.zeros((n, HEAD_PAD), F32)


def _position_dft(n):
    nb = 64 if n % 64 == 0 else 1
    na = n // nb
    m = jnp.arange(n, dtype=jnp.int32)[None, :]
    ang_a = ((jnp.arange(na, dtype=jnp.int32)[:, None] * m) % na).astype(F32) * (2.0 * np.pi / na)
    ang_b = ((jnp.arange(nb, dtype=jnp.int32)[:, None] * m) % n).astype(F32) * (2.0 * np.pi / n)
    ca, sa = jnp.cos(ang_a)[:, None, :], jnp.sin(ang_a)[:, None, :]
    cb, sb = jnp.cos(ang_b)[None, :, :], jnp.sin(ang_b)[None, :, :]
    norm = 1.0 / np.sqrt(n)
    cos = ((ca * cb - sa * sb) * norm).reshape(n, n)
    sin = ((sa * cb + ca * sb) * (-norm)).reshape(n, n)
    return jnp.concatenate([cos, sin], axis=1).astype(BF16)


def _channel_dft(width):
    gc = width // FOURIER_GROUPS
    idx = (jnp.arange(gc, dtype=jnp.int32)[:, None] * jnp.arange(gc, dtype=jnp.int32)[None, :]) % gc
    ang = idx.astype(F32) * (2.0 * np.pi / gc)
    eye = jnp.eye(FOURIER_GROUPS, dtype=F32)
    norm = 1.0 / np.sqrt(gc)
    return jnp.concatenate([jnp.kron(eye, jnp.cos(ang) * norm), jnp.kron(eye, jnp.sin(ang) * norm)],
                           axis=1).astype(BF16)


def _w1_kernel(w_ref, o_ref, *, kv_rank):
    cols = w_ref.shape[2]
    kv_end = kv_rank + QK_ROPE
    half = QK_ROPE // 2
    tail = HEAD_PAD - QK_NOPE - QK_ROPE
    o_kpe = kv_rank
    o_rot = o_kpe + HEAD_PAD
    o_rest = o_rot + HEAD_PAD
    dt = o_ref.dtype
    o_ref[0, 0:kv_rank, :] = w_ref[0, 0:kv_rank, :].astype(dt)
    for base in (o_kpe, o_rot):
        o_ref[0, base:base + QK_NOPE, :] = jnp.zeros((QK_NOPE, cols), dt)
        o_ref[0, base + QK_NOPE + QK_ROPE:base + HEAD_PAD, :] = jnp.zeros((tail, cols), dt)
    o_ref[0, o_kpe + QK_NOPE:o_kpe + QK_NOPE + QK_ROPE, :] = w_ref[0, kv_rank:kv_end, :].astype(dt)
    o_ref[0, o_rot + QK_NOPE:o_rot + QK_NOPE + half, :] = (-w_ref[0, kv_rank + half:kv_end, :]).astype(dt)
    o_ref[0, o_rot + QK_NOPE + half:o_rot + QK_NOPE + QK_ROPE, :] = w_ref[0, kv_rank:kv_rank + half, :].astype(dt)
    o_ref[0, o_rest:, :] = w_ref[0, kv_end:, :].astype(dt)


def _prep_w1(w_in_t, kv_rank):
    n_layers, width, d = w_in_t.shape
    out_w = width - QK_ROPE + 2 * HEAD_PAD
    tc = _tile(d, 256)
    return pl.pallas_call(
        functools.partial(_w1_kernel, kv_rank=kv_rank),
        grid=(n_layers, d // tc),
        in_specs=[pl.BlockSpec((1, width, tc), lambda l, i: (l, 0, i))],
        out_specs=pl.BlockSpec((1, out_w, tc), lambda l, i: (l, 0, i)),
        out_shape=jax.ShapeDtypeStruct((n_layers, out_w, d), BF16),
        compiler_params=_params(2),
        name="prep_w1",
    )(w_in_t)


def _rot_cols(w):
    half = w.shape[-1] // 2
    return jnp.concatenate([-w[..., half:], w[..., :half]], axis=-1)


def _layer_weights(l, g_pre_mix, g_post_mix, g_pre_ffn, g_post_ffn, w1, b_gate, g_q, w_uq, g_kv,
                   w_ukv, w_mla_out, conv_w, w_conv_out, w_four_out, w_out, w_router, b_router,
                   w_gate_s, w_up_s, w_down_s, dc):
    kv_rank = g_kv.shape[1]
    q_rank = g_q.shape[1]
    qk_dim = QK_NOPE + QK_ROPE
    pad = HEAD_PAD - qk_dim
    uq = w_uq[l].reshape(q_rank, N_HEADS, qk_dim) * (qk_dim ** -0.5 * np.log2(np.e))
    zq = jnp.zeros((q_rank, N_HEADS, pad), F32)
    wuq = jnp.concatenate([uq, zq], axis=-1).reshape(q_rank, N_HEADS * HEAD_PAD).astype(BF16)
    wuq_rot = jnp.concatenate([jnp.zeros((q_rank, N_HEADS, QK_NOPE), F32), _rot_cols(uq[..., QK_NOPE:]), zq],
                              axis=-1).reshape(q_rank, N_HEADS * HEAD_PAD).astype(BF16)
    ukv = w_ukv[l].reshape(kv_rank, N_HEADS, QK_NOPE + V_DIM)
    wuk = jnp.concatenate([ukv[..., :QK_NOPE], jnp.zeros((kv_rank, N_HEADS, HEAD_PAD - QK_NOPE), F32)],
                          axis=-1).reshape(kv_rank, N_HEADS * HEAD_PAD).astype(BF16)
    wuv = jnp.concatenate([ukv[..., QK_NOPE:], jnp.zeros((kv_rank, N_HEADS, HEAD_PAD - V_DIM), F32)],
                          axis=-1).reshape(kv_rank, N_HEADS * HEAD_PAD).astype(BF16)
    v_one = jnp.tile((jnp.arange(HEAD_PAD) == V_DIM).astype(F32), N_HEADS)[None]

    wr = w_router[l]
    wr_hi = wr.astype(BF16)
    wr_lo = (wr - wr_hi.astype(F32)).astype(BF16)
    return {
        "g_pre_mix": g_pre_mix[l][None], "g_post_mix": g_post_mix[l][None],
        "g_pre_ffn": g_pre_ffn[l][None], "g_post_ffn": g_post_ffn[l][None],
        "w1": w1, "layer": l, "b_gate": b_gate[l][None], "g_q": g_q[l][None], "g_kv": g_kv[l][None],
        "wuq": wuq, "wuq_rot": wuq_rot, "wuk": wuk, "wuv": wuv, "v_one": v_one, "dc": dc,
        "conv_w": conv_w[l],
        "w_mla_out": w_mla_out[l].astype(BF16), "w_conv_out": w_conv_out[l].astype(BF16),
        "w_four_out": w_four_out[l].astype(BF16), "w_out": w_out[l].astype(BF16),
        "w_router": jnp.concatenate([wr_hi, wr_lo], axis=1), "b_router": b_router[l][None],
        "w_gate_s": w_gate_s[l].astype(BF16), "w_up_s": w_up_s[l].astype(BF16),
        "w_down_s": w_down_s[l].astype(BF16),
    }


def _tile(n, pref):
    return pref if n % pref == 0 else n


def kernel(x, c, ctx, c_ctx, w_ada, b_ada, g_pre_mix, g_post_mix, g_pre_ffn, g_post_ffn, w_in, b_gate,
           g_q, w_uq, g_kv, w_ukv, w_mla_out, conv_w, w_conv_out, w_four_out, w_out, w_router, b_router,
           w_gate_e, w_up_e, w_down_e, w_gate_s, w_up_s, w_down_s):
    batch, seq, d = x.shape
    n_ctx = ctx.shape[1]
    n_layers = w_in.shape[0]
    xs = x.reshape(batch * seq, d)
    cs = ctx.reshape(batch * n_ctx, d)

    mod_rows = 16
    c_all = jnp.concatenate([c, c_ctx[None], jnp.zeros((mod_rows - batch - 1, d), F32)], axis=0)
    ada = _ada(c_all, w_ada, b_ada)

    tab_x = _rope_tables(seq)
    tm_c = _tile(n_ctx, 256)
    tab_c = _identity_tables(tm_c)
    cs_x = _position_dft(seq)
    cs_c = _position_dft(n_ctx)
    dc = _channel_dft(w_four_out.shape[1])
    w1 = _prep_w1(jnp.swapaxes(w_in, 1, 2), g_kv.shape[1])

    tm_x = _tile(seq, 512)
    tm_moe_x = _tile(seq, 1024)
    tm_moe_c = _tile(batch * n_ctx, 1024)

    for l in range(n_layers):
        last = l == n_layers - 1
        lw = _layer_weights(l, g_pre_mix, g_post_mix, g_pre_ffn, g_post_ffn, w1, b_gate, g_q, w_uq,
                            g_kv, w_ukv, w_mla_out, conv_w, w_conv_out, w_four_out, w_out, w_router,
                            b_router, w_gate_s, w_up_s, w_down_s, dc)
        mods = ada[l].reshape(mod_rows, 6, d)
        mod_x = mods[:batch]
        mod_c = mods[batch:batch + 1]

        pc = _inproj(cs, mod_c, n_ctx, lw, tab_c, kv_only=last, tm=tm_c)
        px = _inproj(xs, mod_x, seq, lw, tab_x, kv_only=False, tm=tm_x)
        o_x = _attention(px["q"], [(pc["k"], pc["v"], n_ctx), (px["k"], px["v"], seq)], batch, seq, tq=tm_x)
        f_x = _fourier(px["ab"], cs_x, batch, seq, tn=tm_x)
        x1 = _merge(xs, mod_x, seq, px, o_x, f_x, lw, tm=tm_x)
        xs = _moe_sparse(x1, mod_x, seq, lw, w_gate_e, w_up_e, w_down_e, l, tm=tm_moe_x, tm_e=512,
                         n_parts=1)
        if not last:
            o_c = _attention(pc["q"], [(pc["k"], pc["v"], n_ctx)], batch, n_ctx, tq=tm_c)
            f_c = _fourier(pc["ab"], cs_c, batch, n_ctx, tn=tm_c)
            c1 = _merge(cs, mod_c, n_ctx, pc, o_c, f_c, lw, tm=tm_c)
            cs = _moe(c1, mod_c, batch * n_ctx, lw, w_gate_e, w_up_e, w_down_e, l, tm=tm_moe_c)
    return xs.reshape(batch, seq, d)
```

```python
import functools

import numpy as np
import jax
import jax.numpy as jnp
from jax import lax
from jax.experimental import pallas as pl
from jax.experimental.pallas import tpu as pltpu
from jax.experimental.pallas import tpu_sc as plsc

N_HEADS = 8
QK_NOPE = 64
QK_ROPE = 32
V_DIM = 64
GRID_W = 64
ROPE_BASE = 10000.0
FOURIER_GROUPS = 4
TOP_K = 4
ROUTED_SCALE = 2.5
N_BRANCHES = 3
EPS = 1e-6

LANE = 128
HEAD_PAD = LANE
VMEM_LIMIT = 56 * 1024 * 1024
PACK_W = 256
SC_WINDOW = 128

F32 = jnp.float32
BF16 = jnp.bfloat16


def _rms(x, g):
    return x * lax.rsqrt(jnp.mean(x * x, axis=-1, keepdims=True) + EPS) * g


def _sigmoid(x):
    return 1.0 / (1.0 + jnp.exp(-x))


def _dot(a, b):
    return jnp.dot(a, b, preferred_element_type=F32)


def _dot_t(a, b_t):
    return lax.dot_general(a, b_t, (((1,), (1,)), ((), ())), preferred_element_type=F32)


def _resident(shape):
    nd = len(shape)
    return pl.BlockSpec(shape, lambda *_: (0,) * nd, pipeline_mode=pl.Buffered(1))


def _params(n_grid):
    return pltpu.CompilerParams(dimension_semantics=("arbitrary",) * n_grid,
                                vmem_limit_bytes=VMEM_LIMIT)


def _ada_kernel(c_ref, w_ref, b_ref, o_ref):
    c = c_ref[...]
    a = (c * _sigmoid(c)).astype(BF16)
    o_ref[0] = _dot(a, w_ref[0].astype(BF16)) + b_ref[0]


def _ada(c_all, w_ada, b_ada):
    n_layers, d, n_out = w_ada.shape
    rows = c_all.shape[0]
    tn = 1536
    return pl.pallas_call(
        _ada_kernel,
        grid=(n_layers, n_out // tn),
        in_specs=[
            pl.BlockSpec((rows, d), lambda l, j: (0, 0)),
            pl.BlockSpec((1, d, tn), lambda l, j: (l, 0, j)),
            pl.BlockSpec((1, 1, tn), lambda l, j: (l, 0, j)),
        ],
        out_specs=pl.BlockSpec((1, rows, tn), lambda l, j: (l, 0, j)),
        out_shape=jax.ShapeDtypeStruct((n_layers, rows, n_out), F32),
        compiler_params=_params(2),
        name="ada",
    )(c_all, w_ada, b_ada.reshape(n_layers, 1, n_out))


def _inproj_kernel(*refs, kv_only, kv_rank, q_rank, conv_w, four_w, d_model):
    if kv_only:
        (x_ref, mod_ref, gpre_ref, w1_ref, gkv_ref, wuk_ref, wuv_ref, vone_ref, cos_ref, sin_ref,
         k_ref, v_ref) = refs
    else:
        (x_ref, mod_ref, gpre_ref, w1_ref, gkv_ref, wuk_ref, wuv_ref, vone_ref, cos_ref, sin_ref,
         bg_ref, gq_ref, wuq_ref, wuqr_ref, dc_ref,
         k_ref, v_ref, q_ref, cb_ref, cc_ref, cu_ref, ab_ref, gate_ref) = refs

    x = x_ref[...]
    shift = mod_ref[0, 0:1, :]
    scale = mod_ref[0, 1:2, :]
    h = (_rms(x, gpre_ref[...]) * (1.0 + scale) + shift).astype(BF16)
    cos = cos_ref[...]
    sin = sin_ref[...]

    o_kpe = kv_rank
    o_rot = o_kpe + HEAD_PAD
    o_q = o_rot + HEAD_PAD
    p = _dot_t(h, w1_ref[0, 0:o_q, :])
    ckv = _rms(p[:, 0:kv_rank], gkv_ref[...]).astype(BF16)
    kpe = p[:, o_kpe:o_rot] * cos + p[:, o_rot:o_q] * sin
    k = _dot(ckv, wuk_ref[...]) + jnp.concatenate([kpe] * N_HEADS, axis=1)
    k_ref[...] = k.astype(k_ref.dtype)
    v_ref[...] = (_dot(ckv, wuv_ref[...]) + vone_ref[...]).astype(v_ref.dtype)
    if kv_only:
        return

    o_cb = o_q + q_rank
    cq = _rms(_dot_t(h, w1_ref[0, o_q:o_cb, :]), gq_ref[...]).astype(BF16)
    cos_h = jnp.concatenate([cos] * N_HEADS, axis=1)
    sin_h = jnp.concatenate([sin] * N_HEADS, axis=1)
    q = _dot(cq, wuq_ref[...]) * cos_h + _dot(cq, wuqr_ref[...]) * sin_h
    q_ref[...] = q.astype(q_ref.dtype)

    o_cc = o_cb + conv_w
    o_cu = o_cc + conv_w
    o_four = o_cu + conv_w
    cb_ref[...] = _dot_t(h, w1_ref[0, o_cb:o_cc, :]).astype(cb_ref.dtype)
    cc_ref[...] = _dot_t(h, w1_ref[0, o_cc:o_cu, :]).astype(cc_ref.dtype)
    cu_ref[...] = _dot_t(h, w1_ref[0, o_cu:o_four, :]).astype(cu_ref.dtype)

    o_gate = o_four + four_w
    uf = _dot_t(h, w1_ref[0, o_four:o_gate, :]).astype(BF16)
    ab_ref[...] = _dot(uf, dc_ref[...]).astype(ab_ref.dtype)

    for j in range(N_BRANCHES):
        lo = o_gate + j * d_model
        z = _dot_t(h, w1_ref[0, lo:lo + d_model, :]) + bg_ref[:, j * d_model:(j + 1) * d_model]
        gate_ref[:, j * d_model:(j + 1) * d_model] = _sigmoid(z).astype(gate_ref.dtype)


def _inproj(xs, mod, seq_len, lw, tables, *, kv_only, tm):
    t, d = xs.shape
    nb = mod.shape[0]
    tiles_per_seq = seq_len // tm
    cos_t, sin_t = tables
    table_tiles = cos_t.shape[0] // tm
    kv_rank = lw["g_kv"].shape[1]
    q_rank = lw["g_q"].shape[1]
    conv_w = lw["conv_w"].shape[1]
    four_w = lw["dc"].shape[0]
    n_k = N_HEADS * HEAD_PAD

    def row(i):
        return (i, 0)

    def mod_map(i):
        return ((i // tiles_per_seq) % nb, 0, 0)

    def tab_map(i):
        return (i % table_tiles, 0)

    w1, layer = lw["w1"], lw["layer"]
    w1_rows = kv_rank + 2 * HEAD_PAD if kv_only else w1.shape[1]
    in_specs = [
        pl.BlockSpec((tm, d), row),
        pl.BlockSpec((1,) + mod.shape[1:], mod_map),
        _resident((1, d)),
        pl.BlockSpec((1, w1_rows, d), lambda i: (layer, 0, 0), pipeline_mode=pl.Buffered(1)),
        _resident((1, kv_rank)),
        _resident(lw["wuk"].shape),
        _resident(lw["wuv"].shape),
        _resident(lw["v_one"].shape),
        pl.BlockSpec((tm, HEAD_PAD), tab_map),
        pl.BlockSpec((tm, HEAD_PAD), tab_map),
    ]
    args = [xs, mod, lw["g_pre_mix"], w1, lw["g_kv"], lw["wuk"], lw["wuv"], lw["v_one"], cos_t, sin_t]
    out_shape = [jax.ShapeDtypeStruct((t, n_k), BF16), jax.ShapeDtypeStruct((t, n_k), BF16)]
    out_specs = [pl.BlockSpec((tm, n_k), row), pl.BlockSpec((tm, n_k), row)]
    if not kv_only:
        in_specs += [
            _resident(lw["b_gate"].shape),
            _resident((1, q_rank)),
            _resident(lw["wuq"].shape),
            _resident(lw["wuq_rot"].shape),
            _resident(lw["dc"].shape),
        ]
        args += [lw["b_gate"], lw["g_q"], lw["wuq"], lw["wuq_rot"], lw["dc"]]
        widths = [n_k, conv_w, conv_w, conv_w, 2 * four_w, N_BRANCHES * d]
        out_shape += [jax.ShapeDtypeStruct((t, w), BF16) for w in widths]
        out_specs += [pl.BlockSpec((tm, w), row) for w in widths]
    outs = pl.pallas_call(
        functools.partial(_inproj_kernel, kv_only=kv_only, kv_rank=kv_rank, q_rank=q_rank,
                          conv_w=conv_w, four_w=four_w, d_model=d),
        grid=(t // tm,),
        in_specs=in_specs,
        out_specs=out_specs,
        out_shape=out_shape,
        compiler_params=_params(1),
        name="inproj_kv" if kv_only else "inproj",
    )(*args)
    names = ["k", "v", "q", "cb", "cc", "cu", "ab", "gate"]
    return dict(zip(names, outs))


def _attn_kernel(*refs, n_seg):
    q_ref = refs[0]
    o_ref = refs[-1]
    nt = (((1,), (1,)), ((), ()))
    outs = []
    for hh in range(N_HEADS):
        head = slice(hh * HEAD_PAD, (hh + 1) * HEAD_PAD)
        qh = q_ref[:, head]
        s = [lax.dot_general(qh, refs[1 + 2 * i][:, head], nt, preferred_element_type=F32)
             for i in range(n_seg)]
        m = functools.reduce(jnp.maximum, [jnp.max(si, axis=-1, keepdims=True) for si in s])
        acc = functools.reduce(jnp.add, [
            _dot(jnp.exp2((s[i] - m).astype(BF16)), refs[2 + 2 * i][:, head]) for i in range(n_seg)])
        outs.append(acc[:, 0:V_DIM] / acc[:, V_DIM:V_DIM + 1])
    o_ref[...] = jnp.concatenate(outs, axis=1).astype(o_ref.dtype)


def _attention(q, segs, batch, seq_q, *, tq):
    t = q.shape[0]
    qt = seq_q // tq
    n_k = N_HEADS * HEAD_PAD
    in_specs = [pl.BlockSpec((tq, n_k), lambda b, j: (b * qt + j, 0))]
    args = [q]
    for k, v, m in segs:
        in_specs.append(pl.BlockSpec((m, n_k), lambda b, j: (b, 0)))
        in_specs.append(pl.BlockSpec((m, n_k), lambda b, j: (b, 0)))
        args += [k, v]
    return pl.pallas_call(
        functools.partial(_attn_kernel, n_seg=len(segs)),
        grid=(batch, qt),
        in_specs=in_specs,
        out_specs=pl.BlockSpec((tq, N_HEADS * V_DIM), lambda b, j: (b * qt + j, 0)),
        out_shape=jax.ShapeDtypeStruct((t, N_HEADS * V_DIM), BF16),
        compiler_params=_params(2),
        name="attention",
    )(*args)


def _four_kernel(cs_ref, ab_ref, o_ref, *, n, fw):
    o = _dot(cs_ref[:, 0:n], ab_ref[:, 0:fw]) + _dot(cs_ref[:, n:2 * n], ab_ref[:, fw:2 * fw])
    o_ref[...] = o.astype(o_ref.dtype)


def _fourier(ab, cs, batch, seq_len, *, tn):
    t, fw2 = ab.shape
    fw = fw2 // 2
    nt = seq_len // tn
    return pl.pallas_call(
        functools.partial(_four_kernel, n=seq_len, fw=fw),
        grid=(batch, nt),
        in_specs=[
            pl.BlockSpec((tn, 2 * seq_len), lambda b, j: (j, 0)),
            pl.BlockSpec((seq_len, fw2), lambda b, j: (b, 0)),
        ],
        out_specs=pl.BlockSpec((tn, fw), lambda b, j: (b * nt + j, 0)),
        out_shape=jax.ShapeDtypeStruct((t, fw), BF16),
        compiler_params=_params(2),
        name="fourier",
    )(cs, ab)


def _merge_kernel(x_ref, mod_ref, o_ref, cb_ref, cc_ref, cu_ref, ccp_ref, cup_ref, ccn_ref,
                  cun_ref, f_ref, gate_ref, convw_ref, wmo_ref, wco_ref, wfo_ref, wout_ref,
                  gpost_ref, out_ref, pad_ref, *, tiles_per_seq, tm, d_model):
    i = pl.program_id(0)
    pos = i % tiles_per_seq
    has_prev = (pos > 0).astype(F32)
    has_next = (pos < tiles_per_seq - 1).astype(F32)
    pad_ref[0:8, :] = ccp_ref[...].astype(F32) * cup_ref[...].astype(F32) * has_prev
    pad_ref[8:8 + tm, :] = cc_ref[...].astype(F32) * cu_ref[...].astype(F32)
    pad_ref[8 + tm:16 + tm, :] = ccn_ref[...].astype(F32) * cun_ref[...].astype(F32) * has_next
    conv = (pad_ref[7:7 + tm, :] * convw_ref[0:1, :] + pad_ref[8:8 + tm, :] * convw_ref[1:2, :]
            + pad_ref[9:9 + tm, :] * convw_ref[2:3, :])
    y_conv = _dot((cb_ref[...].astype(F32) * conv).astype(BF16), wco_ref[...])
    y_attn = _dot(o_ref[...], wmo_ref[...])
    y_four = _dot(f_ref[...], wfo_ref[...])
    d = d_model
    merged = (gate_ref[:, 0:d].astype(F32) * y_attn + gate_ref[:, d:2 * d].astype(F32) * y_conv
              + gate_ref[:, 2 * d:3 * d].astype(F32) * y_four)
    y = _dot(merged.astype(BF16), wout_ref[...])
    g1 = mod_ref[0, 2:3, :]
    out_ref[...] = x_ref[...] + g1 * _rms(y, gpost_ref[...])


def _merge(xs, mod, seq_len, pr, o, four, lw, *, tm):
    t, d = xs.shape
    nb = mod.shape[0]
    tiles_per_seq = seq_len // tm
    cw = lw["conv_w"].shape[1]
    fw = four.shape[1]
    hb = tm // 8
    last_hb = t // 8 - 1

    def row(i):
        return (i, 0)

    def prev_map(i):
        return (jnp.maximum(i * hb - 1, 0), 0)

    def next_map(i):
        return (jnp.minimum((i + 1) * hb, last_hb), 0)

    in_specs = [
        pl.BlockSpec((tm, d), row),
        pl.BlockSpec((1,) + mod.shape[1:], lambda i: ((i // tiles_per_seq) % nb, 0, 0)),
        pl.BlockSpec((tm, o.shape[1]), row),
        pl.BlockSpec((tm, cw), row),
        pl.BlockSpec((tm, cw), row),
        pl.BlockSpec((tm, cw), row),
        pl.BlockSpec((8, cw), prev_map),
        pl.BlockSpec((8, cw), prev_map),
        pl.BlockSpec((8, cw), next_map),
        pl.BlockSpec((8, cw), next_map),
        pl.BlockSpec((tm, fw), row),
        pl.BlockSpec((tm, N_BRANCHES * d), row),
        _resident(lw["conv_w"].shape),
        _resident(lw["w_mla_out"].shape),
        _resident(lw["w_conv_out"].shape),
        _resident(lw["w_four_out"].shape),
        _resident(lw["w_out"].shape),
        _resident((1, d)),
    ]
    return pl.pallas_call(
        functools.partial(_merge_kernel, tiles_per_seq=tiles_per_seq, tm=tm, d_model=d),
        grid=(t // tm,),
        in_specs=in_specs,
        out_specs=pl.BlockSpec((tm, d), row),
        out_shape=jax.ShapeDtypeStruct((t, d), F32),
        scratch_shapes=[pltpu.VMEM((tm + 16, cw), F32)],
        compiler_params=_params(1),
        name="merge",
    )(xs, mod, o, pr["cb"], pr["cc"], pr["cu"], pr["cc"], pr["cu"], pr["cc"], pr["cu"], four,
      pr["gate"], lw["conv_w"], lw["w_mla_out"], lw["w_conv_out"], lw["w_four_out"], lw["w_out"],
      lw["g_post_mix"])


def _moe_kernel(x_ref, mod_ref, gpre_ref, gpost_ref, wr_ref, br_ref, wgs_ref, wus_ref, wds_ref,
                wge_ref, wue_ref, wde_ref, out_ref, t_ref, comb_ref, acc_ref, *, n_experts):
    e = pl.program_id(1)

    @pl.when(e == 0)
    def _():
        shift = mod_ref[0, 3:4, :]
        scale = mod_ref[0, 4:5, :]
        t = _rms(x_ref[...], gpre_ref[...]) * (1.0 + scale) + shift
        t_hi = t.astype(BF16)
        t_lo = (t - t_hi.astype(F32)).astype(BF16)
        t_ref[...] = t_hi
        hh = _dot(t_hi, wr_ref[...])
        logits = hh[:, 0:n_experts] + hh[:, n_experts:2 * n_experts] + _dot(t_lo, wr_ref[:, 0:n_experts])
        scores = _sigmoid(logits)
        work = scores + br_ref[...]
        lane = lax.broadcasted_iota(jnp.int32, scores.shape, 1)
        comb = jnp.zeros_like(scores)
        for _ in range(TOP_K):
            best = jnp.max(work, axis=-1, keepdims=True)
            first = jnp.min(jnp.where(work == best, lane, n_experts), axis=-1, keepdims=True)
            hit = lane == first
            comb = jnp.where(hit, scores, comb)
            work = jnp.where(hit, -jnp.inf, work)
        comb_ref[...] = comb / jnp.sum(comb, axis=-1, keepdims=True) * ROUTED_SCALE
        gate = _dot(t_hi, wgs_ref[...])
        act = (gate * _sigmoid(gate) * _dot(t_hi, wus_ref[...])).astype(BF16)
        acc_ref[...] = _dot(act, wds_ref[...])

    t_hi = t_ref[...]
    gate = _dot(t_hi, wge_ref[0, 0].astype(BF16))
    up = _dot(t_hi, wue_ref[0, 0].astype(BF16))
    lane = lax.broadcasted_iota(jnp.int32, comb_ref.shape, 1)
    w_e = jnp.sum(jnp.where(lane == e, comb_ref[...], 0.0), axis=-1, keepdims=True)
    act = (gate * _sigmoid(gate) * up * w_e).astype(BF16)
    acc_ref[...] += _dot(act, wde_ref[0, 0].astype(BF16))

    @pl.when(e == n_experts - 1)
    def _():
        g2 = mod_ref[0, 5:6, :]
        out_ref[...] = x_ref[...] + g2 * _rms(acc_ref[...], gpost_ref[...])


def _moe(xs, mod, seq_len, lw, w_gate_e, w_up_e, w_down_e, layer, *, tm):
    t, d = xs.shape
    nb = mod.shape[0]
    tiles_per_seq = max(seq_len // tm, 1)
    n_experts, _, f = w_gate_e.shape[1:]
    in_specs = [
        pl.BlockSpec((tm, d), lambda i, e: (i, 0)),
        pl.BlockSpec((1,) + mod.shape[1:], lambda i, e: ((i // tiles_per_seq) % nb, 0, 0)),
        _resident((1, d)),
        _resident((1, d)),
        _resident(lw["w_router"].shape),
        _resident((1, n_experts)),
        _resident(lw["w_gate_s"].shape),
        _resident(lw["w_up_s"].shape),
        _resident(lw["w_down_s"].shape),
        pl.BlockSpec((1, 1, d, f), lambda i, e: (layer, e, 0, 0)),
        pl.BlockSpec((1, 1, d, f), lambda i, e: (layer, e, 0, 0)),
        pl.BlockSpec((1, 1, f, d), lambda i, e: (layer, e, 0, 0)),
    ]
    return pl.pallas_call(
        functools.partial(_moe_kernel, n_experts=n_experts),
        grid=(t // tm, n_experts),
        in_specs=in_specs,
        out_specs=pl.BlockSpec((tm, d), lambda i, e: (i, 0)),
        out_shape=jax.ShapeDtypeStruct((t, d), F32),
        scratch_shapes=[pltpu.VMEM((tm, d), BF16), pltpu.VMEM((tm, n_experts), F32),
                        pltpu.VMEM((tm, d), F32)],
        compiler_params=_params(2),
        name="moe",
    )(xs, mod, lw["g_pre_ffn"], lw["g_post_ffn"], lw["w_router"], lw["b_router"], lw["w_gate_s"],
      lw["w_up_s"], lw["w_down_s"], w_gate_e, w_up_e, w_down_e)


def _pack_rows(v):
    bits = lax.bitcast_convert_type(v.astype(BF16).astype(F32), jnp.uint32)
    rows = []
    for j in range(v.shape[1] // (2 * PACK_W)):
        lo = bits[:, (2 * j) * PACK_W:(2 * j + 1) * PACK_W]
        hi = bits[:, (2 * j + 1) * PACK_W:(2 * j + 2) * PACK_W]
        rows.append(lax.bitcast_convert_type((hi & jnp.uint32(0xFFFF0000)) | (lo >> 16), jnp.int32))
    return rows


def _unpack_rows(rows):
    parts = []
    for r in rows:
        u = lax.bitcast_convert_type(r, jnp.uint32)
        parts.append(lax.bitcast_convert_type(u << 16, F32))
        parts.append(lax.bitcast_convert_type(u & jnp.uint32(0xFFFF0000), F32))
    return jnp.concatenate(parts, axis=1)


def _route_kernel(x_ref, mod_ref, gpre_ref, wrt_ref, brt_ref, triu_ref,
                  tp_ref, sel_ref, rank_ref, cnt_ref, wgt_ref, *, n_experts):
    shift = mod_ref[0, 3:4, :]
    scale = mod_ref[0, 4:5, :]
    t = _rms(x_ref[...], gpre_ref[...]) * (1.0 + scale) + shift
    t_hi = t.astype(BF16)
    t_lo = (t - t_hi.astype(F32)).astype(BF16)
    for j, r in enumerate(_pack_rows(t)):
        tp_ref[j] = r
    tm = t.shape[0]
    hh = _dot_t(wrt_ref[...], t_hi)
    logits = hh[0:n_experts] + hh[n_experts:2 * n_experts] + _dot_t(wrt_ref[0:n_experts, :], t_lo)
    scores = _sigmoid(logits)
    work = scores + brt_ref[...]
    row = lax.broadcasted_iota(jnp.int32, scores.shape, 0)
    wide = lax.broadcasted_iota(jnp.int32, (LANE, tm), 0)
    firsts, picked, hits = [], [], []
    for k in range(TOP_K):
        best = jnp.max(work, axis=0, keepdims=True)
        first = jnp.min(jnp.where(work == best, row, n_experts), axis=0, keepdims=True)
        hit = row == first
        firsts.append(first)
        picked.append(jnp.sum(jnp.where(hit, scores, 0.0), axis=0, keepdims=True))
        hits.append(wide == first + k * n_experts)
        work = jnp.where(hit, -jnp.inf, work)
    total = functools.reduce(jnp.add, picked)
    onehot = functools.reduce(jnp.add, [jnp.where(h, 1.0, 0.0) for h in hits])
    earlier = _dot(onehot.astype(BF16), triu_ref[...])
    col = jnp.broadcast_to(jnp.sum(onehot, axis=1, keepdims=True), (LANE, LANE))
    row_c = lax.broadcasted_iota(jnp.int32, (LANE, LANE), 0)
    before = jnp.zeros((LANE, LANE), F32)
    for s in range(1, TOP_K):
        before = before + jnp.where(row_c >= s * n_experts, pltpu.roll(col, s * n_experts, 0), 0.0)
    ahead = earlier + before[:, 0:1]
    row8 = lax.broadcasted_iota(jnp.int32, (8, tm), 0)
    sel = jnp.zeros((8, tm), jnp.int32)
    rank = jnp.zeros((8, tm), F32)
    wgt_t = jnp.zeros((LANE, tm), F32)
    for k in range(TOP_K):
        sel = jnp.where(row8 == k, firsts[k], sel)
        rank = jnp.where(row8 == k, jnp.sum(jnp.where(hits[k], ahead, 0.0), axis=0, keepdims=True), rank)
        wgt_t = jnp.where(wide == k, picked[k] / total * ROUTED_SCALE, wgt_t)
    sel_ref[0] = sel
    rank_ref[0] = rank.astype(jnp.int32)
    cnt_ref[0] = col.astype(jnp.int32)
    wgt_ref[...] = wgt_t.T


def _route(x1, mod, seq_len, lw, *, tm, tile_off, n_tiles):
    d = x1.shape[1]
    t = n_tiles * tm
    nb = mod.shape[0]
    tiles_per_seq = max(seq_len // tm, 1)
    n_experts = lw["b_router"].shape[1]
    rows = d // (2 * PACK_W)
    triu = jnp.tri(tm, tm, -1, dtype=BF16).T
    tile3 = lambda i: (i, 0, 0)
    return pl.pallas_call(
        functools.partial(_route_kernel, n_experts=n_experts),
        grid=(n_tiles,),
        in_specs=[
            pl.BlockSpec((tm, d), lambda i: (i + tile_off, 0)),
            pl.BlockSpec((1,) + mod.shape[1:], lambda i: (((i + tile_off) // tiles_per_seq) % nb, 0, 0)),
            _resident((1, d)),
            _resident(lw["w_router_t"].shape),
            _resident((n_experts, 1)),
            _resident((tm, tm)),
        ],
        out_specs=[
            pl.BlockSpec((rows, tm, PACK_W), lambda i: (0, i, 0)),
            pl.BlockSpec((1, 8, tm), tile3),
            pl.BlockSpec((1, 8, tm), tile3),
            pl.BlockSpec((1, LANE, LANE), tile3),
            pl.BlockSpec((tm, LANE), lambda i: (i, 0)),
        ],
        out_shape=[
            jax.ShapeDtypeStruct((rows, t, PACK_W), jnp.int32),
            jax.ShapeDtypeStruct((n_tiles, 8, tm), jnp.int32),
            jax.ShapeDtypeStruct((n_tiles, 8, tm), jnp.int32),
            jax.ShapeDtypeStruct((n_tiles, LANE, LANE), jnp.int32),
            jax.ShapeDtypeStruct((t, LANE), F32),
        ],
        compiler_params=_params(1),
        name="moe_route",
    )(x1, mod, lw["g_pre_ffn"], lw["w_router_t"], lw["b_router_t"], triu)


def _expert_kernel(te_ref, meta_ref, xs_ref, wg_ref, wu_ref, wd_ref, ys_ref, wg_sc, wu_sc, wd_sc):
    i = pl.program_id(0)
    live = i < meta_ref[0]
    new_expert = (i == 0) | (te_ref[i] != te_ref[jnp.maximum(i - 1, 0)])

    @pl.when(live & new_expert)
    def _():
        wg_sc[...] = wg_ref[0, 0].astype(BF16)
        wu_sc[...] = wu_ref[0, 0].astype(BF16)
        wd_sc[...] = wd_ref[0, 0].astype(BF16)

    @pl.when(live)
    def _():
        x = _unpack_rows([xs_ref[j] for j in range(xs_ref.shape[0])]).astype(BF16)
        gate = _dot(x, wg_sc[...])
        act = (gate * _sigmoid(gate) * _dot(x, wu_sc[...])).astype(BF16)
        for j, r in enumerate(_pack_rows(_dot(act, wd_sc[...]))):
            ys_ref[j] = r


def _experts(xs, tile_expert, meta, w_gate_e, w_up_e, w_down_e, layer, *, tm):
    rows, p, _ = xs.shape
    n_experts, d, f = w_gate_e.shape[1:]

    def slot(i, te, meta):
        return (0, jnp.minimum(i, meta[0] - 1), 0)

    grid_spec = pltpu.PrefetchScalarGridSpec(
        num_scalar_prefetch=2,
        grid=(p // tm,),
        in_specs=[
            pl.BlockSpec((rows, tm, PACK_W), slot),
            pl.BlockSpec((1, 1, d, f), lambda i, te, meta: (layer, te[i], 0, 0)),
            pl.BlockSpec((1, 1, d, f), lambda i, te, meta: (layer, te[i], 0, 0)),
            pl.BlockSpec((1, 1, f, d), lambda i, te, meta: (layer, te[i], 0, 0)),
        ],
        out_specs=pl.BlockSpec((rows, tm, PACK_W), slot),
        scratch_shapes=[pltpu.VMEM((d, f), BF16), pltpu.VMEM((d, f), BF16), pltpu.VMEM((f, d), BF16)],
    )
    return pl.pallas_call(
        _expert_kernel,
        grid_spec=grid_spec,
        out_shape=jax.ShapeDtypeStruct(xs.shape, jnp.int32),
        compiler_params=_params(1),
        name="moe_experts",
    )(tile_expert, meta, xs, w_gate_e, w_up_e, w_down_e)


def _combine_kernel(x_ref, mod_ref, gpre_ref, gpost_ref, wgs_ref, wus_ref, wds_ref, yk_ref, wgt_ref,
                    *rest):
    out_ref = rest[-1]
    shift = mod_ref[0, 3:4, :]
    scale = mod_ref[0, 4:5, :]
    x = x_ref[...]
    t_hi = (_rms(x, gpre_ref[...]) * (1.0 + scale) + shift).astype(BF16)
    gate = _dot(t_hi, wgs_ref[...])
    act = (gate * _sigmoid(gate) * _dot(t_hi, wus_ref[...])).astype(BF16)
    acc = _dot(act, wds_ref[...])
    for k in range(TOP_K):
        y = _unpack_rows([yk_ref[k, j] for j in range(yk_ref.shape[1])])
        acc = acc + wgt_ref[:, k:k + 1] * y
    g2 = mod_ref[0, 5:6, :]
    out_ref[...] = x + g2 * _rms(acc, gpost_ref[...])


def _combine(x1, mod, seq_len, lw, yk, wgt, *, tm, tile_off, prev):
    t, d = x1.shape
    nb = mod.shape[0]
    tiles_per_seq = max(seq_len // tm, 1)
    rows = yk.shape[1]
    n_tiles = yk.shape[2] // tm
    glob = lambda i: (i + tile_off, 0)
    in_specs = [
        pl.BlockSpec((tm, d), glob),
        pl.BlockSpec((1,) + mod.shape[1:], lambda i: (((i + tile_off) // tiles_per_seq) % nb, 0, 0)),
        _resident((1, d)),
        _resident((1, d)),
        _resident(lw["w_gate_s"].shape),
        _resident(lw["w_up_s"].shape),
        _resident(lw["w_down_s"].shape),
        pl.BlockSpec((TOP_K, rows, tm, PACK_W), lambda i: (0, 0, i, 0)),
        pl.BlockSpec((tm, LANE), lambda i: (i, 0)),
    ]
    args = [x1, mod, lw["g_pre_ffn"], lw["g_post_ffn"], lw["w_gate_s"], lw["w_up_s"], lw["w_down_s"], yk, wgt]
    aliases = {}
    if prev is not None:
        in_specs.append(pl.BlockSpec(memory_space=pl.ANY))
        aliases = {len(args): 0}
        args.append(prev)
    return pl.pallas_call(
        _combine_kernel,
        grid=(n_tiles,),
        in_specs=in_specs,
        out_specs=pl.BlockSpec((tm, d), glob),
        out_shape=jax.ShapeDtypeStruct((t, d), F32),
        input_output_aliases=aliases,
        compiler_params=_params(1),
        name="moe_combine",
    )(*args)


def _sc_mesh():
    return plsc.VectorSubcoreMesh(core_axis_name="core", subcore_axis_name="subcore")


def _sc_scatter_rows(src, idx, n_out):
    n_lists, n = idx.shape
    width = src.shape[1]

    @pl.kernel(out_type=jax.ShapeDtypeStruct((n_out, width), src.dtype), mesh=_sc_mesh(), scratch_types=[])
    def scatter(x_hbm, *refs):
        i_hbms, o_hbm = refs[:n_lists], refs[n_lists]

        def body(x_vmem, *i_vmems):
            for i_vmem in i_vmems:
                pltpu.sync_copy(x_vmem, o_hbm.at[i_vmem.at[0]])

        pltpu.emit_pipeline(
            body,
            grid=(n // SC_WINDOW,),
            in_specs=[pl.BlockSpec((SC_WINDOW, width), lambda i: (i, 0))]
            + [pl.BlockSpec((1, SC_WINDOW), lambda i: (0, i))] * n_lists,
            out_specs=[],
            core_axis_name=("core", "subcore"),
            dimension_semantics=(pltpu.PARALLEL,),
        )(x_hbm, *i_hbms)

    return scatter(src, *[idx[r].reshape(1, n) for r in range(n_lists)])


def _sc_gather_rows(table, idx):
    n = idx.shape[0]
    width = table.shape[1]

    @pl.kernel(out_type=jax.ShapeDtypeStruct((n, width), table.dtype), mesh=_sc_mesh(), scratch_types=[])
    def gather(x_hbm, i_hbm, o_hbm):
        def body(i_vmem, o_vmem):
            pltpu.sync_copy(x_hbm.at[i_vmem.at[0]], o_vmem)

        pltpu.emit_pipeline(
            body,
            grid=(n // SC_WINDOW,),
            in_specs=[pl.BlockSpec((1, SC_WINDOW), lambda i: (0, i))],
            out_specs=[pl.BlockSpec((SC_WINDOW, width), lambda i: (i, 0))],
            core_axis_name=("core", "subcore"),
            dimension_semantics=(pltpu.PARALLEL,),
        )(i_hbm, o_hbm)

    return gather(table, idx.reshape(1, n))


def _moe_sparse(x1, mod, seq_len, lw, w_gate_e, w_up_e, w_down_e, layer, *, tm, tm_e, n_parts):
    n_experts = lw["b_router"].shape[1]
    assert n_experts * TOP_K == LANE
    n_tok_tiles = x1.shape[0] // tm // n_parts
    t = n_tok_tiles * tm
    n_row_tiles = (t * TOP_K) // tm_e + n_experts
    p = n_row_tiles * tm_e
    experts = jnp.arange(n_experts, dtype=jnp.int32)

    staged = []
    for part in range(n_parts):
        tp, sel, rank, cnt, wgt = _route(x1, mod, seq_len, lw, tm=tm, tile_off=part * n_tok_tiles,
                                         n_tiles=n_tok_tiles)
        rows = tp.shape[0]
        cnt = cnt[:, :, 0].reshape(n_tok_tiles, TOP_K, n_experts).sum(axis=1)
        padded = (cnt.sum(axis=0) + tm_e - 1) // tm_e * tm_e
        group_end = jnp.cumsum(padded)
        base = (group_end - padded)[None, :] + jnp.cumsum(cnt, axis=0) - cnt
        chosen = sel[:, :TOP_K, :, None] == experts
        pos = jnp.sum(jnp.where(chosen, base[:, None, None, :], 0), axis=-1) + rank[:, :TOP_K, :]
        pos = pos.transpose(1, 0, 2).reshape(TOP_K, t)

        n_used = group_end[-1] // tm_e
        tile_start = jnp.arange(n_row_tiles, dtype=jnp.int32) * tm_e
        tile_expert = jnp.sum(tile_start[:, None] >= group_end[None, :], axis=1).astype(jnp.int32)
        tile_expert = jnp.minimum(tile_expert, n_experts - 1)
        tile_expert = jnp.where(tile_start < group_end[-1], tile_expert, tile_expert[n_used - 1])
        meta = jnp.stack([n_used, n_used]).astype(jnp.int32)

        idx = pos[:, None, :] + (jnp.arange(rows, dtype=jnp.int32) * p)[None, :, None]
        xs = _sc_scatter_rows(tp.reshape(rows * t, PACK_W), idx.reshape(TOP_K, rows * t), rows * p)
        staged.append((xs.reshape(rows, p, PACK_W), tile_expert, meta, idx.reshape(-1), wgt))

    gathered = []
    for xs, tile_expert, meta, idx, wgt in staged:
        ys = _experts(xs, tile_expert, meta, w_gate_e, w_up_e, w_down_e, layer, tm=tm_e)
        yk = _sc_gather_rows(ys.reshape(-1, PACK_W), idx)
        gathered.append((yk.reshape(TOP_K, xs.shape[0], t, PACK_W), wgt))

    out = None
    for part, (yk, wgt) in enumerate(gathered):
        out = _combine(x1, mod, seq_len, lw, yk, wgt, tm=tm, tile_off=part * n_tok_tiles, prev=out)
    return out


def _rope_tables(n):
    rows = n // GRID_W
    r, col = jnp.meshgrid(jnp.arange(rows), jnp.arange(GRID_W), indexing="ij")
    r = r.reshape(-1).astype(F32)
    col = col.reshape(-1).astype(F32)
    pairs = QK_ROPE // 4
    inv = ROPE_BASE ** (-jnp.arange(pairs, dtype=F32) / pairs)
    ang = jnp.concatenate([r[:, None] * inv, col[:, None] * inv], axis=-1)
    cos, sin = jnp.cos(ang), jnp.sin(ang)
    pad = HEAD_PAD - QK_NOPE - QK_ROPE
    cos_t = jnp.concatenate([jnp.ones((n, QK_NOPE), F32), cos, cos, jnp.zeros((n, pad), F32)], axis=1)
    sin_t = jnp.concatenate([jnp.zeros((n, QK_NOPE), F32), sin, sin, jnp.zeros((n, pad), F32)], axis=1)
    return cos_t, sin_t


def _identity_tables(n):
    pad = HEAD_PAD - QK_NOPE - QK_ROPE
    cos_t = jnp.concatenate([jnp.ones((n, QK_NOPE + QK_ROPE), F32), jnp.zeros((n, pad), F32)], axis=1)
    return cos_t, jnp.zeros((n, HEAD_PAD), F32)


def _position_dft(n):
    nb = 64 if n % 64 == 0 else 1
    na = n // nb
    m = jnp.arange(n, dtype=jnp.int32)[None, :]
    ang_a = ((jnp.arange(na, dtype=jnp.int32)[:, None] * m) % na).astype(F32) * (2.0 * np.pi / na)
    ang_b = ((jnp.arange(nb, dtype=jnp.int32)[:, None] * m) % n).astype(F32) * (2.0 * np.pi / n)
    ca, sa = jnp.cos(ang_a)[:, None, :], jnp.sin(ang_a)[:, None, :]
    cb, sb = jnp.cos(ang_b)[None, :, :], jnp.sin(ang_b)[None, :, :]
    norm = 1.0 / np.sqrt(n)
    cos = ((ca * cb - sa * sb) * norm).reshape(n, n)
    sin = ((sa * cb + ca * sb) * (-norm)).reshape(n, n)
    return jnp.concatenate([cos, sin], axis=1).astype(BF16)


def _channel_dft(width):
    gc = width // FOURIER_GROUPS
    idx = (jnp.arange(gc, dtype=jnp.int32)[:, None] * jnp.arange(gc, dtype=jnp.int32)[None, :]) % gc
    ang = idx.astype(F32) * (2.0 * np.pi / gc)
    eye = jnp.eye(FOURIER_GROUPS, dtype=F32)
    norm = 1.0 / np.sqrt(gc)
    return jnp.concatenate([jnp.kron(eye, jnp.cos(ang) * norm), jnp.kron(eye, jnp.sin(ang) * norm)],
                           axis=1).astype(BF16)


def _w1_kernel(w_ref, o_ref, *, kv_rank):
    cols = w_ref.shape[2]
    kv_end = kv_rank + QK_ROPE
    half = QK_ROPE // 2
    tail = HEAD_PAD - QK_NOPE - QK_ROPE
    o_kpe = kv_rank
    o_rot = o_kpe + HEAD_PAD
    o_rest = o_rot + HEAD_PAD
    dt = o_ref.dtype
    o_ref[0, 0:kv_rank, :] = w_ref[0, 0:kv_rank, :].astype(dt)
    for base in (o_kpe, o_rot):
        o_ref[0, base:base + QK_NOPE, :] = jnp.zeros((QK_NOPE, cols), dt)
        o_ref[0, base + QK_NOPE + QK_ROPE:base + HEAD_PAD, :] = jnp.zeros((tail, cols), dt)
    o_ref[0, o_kpe + QK_NOPE:o_kpe + QK_NOPE + QK_ROPE, :] = w_ref[0, kv_rank:kv_end, :].astype(dt)
    o_ref[0, o_rot + QK_NOPE:o_rot + QK_NOPE + half, :] = (-w_ref[0, kv_rank + half:kv_end, :]).astype(dt)
    o_ref[0, o_rot + QK_NOPE + half:o_rot + QK_NOPE + QK_ROPE, :] = w_ref[0, kv_rank:kv_rank + half, :].astype(dt)
    o_ref[0, o_rest:, :] = w_ref[0, kv_end:, :].astype(dt)


def _prep_w1(w_in_t, kv_rank):
    n_layers, width, d = w_in_t.shape
    out_w = width - QK_ROPE + 2 * HEAD_PAD
    tc = _tile(d, 256)
    return pl.pallas_call(
        functools.partial(_w1_kernel, kv_rank=kv_rank),
        grid=(n_layers, d // tc),
        in_specs=[pl.BlockSpec((1, width, tc), lambda l, i: (l, 0, i))],
        out_specs=pl.BlockSpec((1, out_w, tc), lambda l, i: (l, 0, i)),
        out_shape=jax.ShapeDtypeStruct((n_layers, out_w, d), BF16),
        compiler_params=_params(2),
        name="prep_w1",
    )(w_in_t)


def _rot_cols(w):
    half = w.shape[-1] // 2
    return jnp.concatenate([-w[..., half:], w[..., :half]], axis=-1)


def _layer_weights(l, g_pre_mix, g_post_mix, g_pre_ffn, g_post_ffn, w1, b_gate, g_q, w_uq, g_kv,
                   w_ukv, w_mla_out, conv_w, w_conv_out, w_four_out, w_out, w_router, b_router,
                   w_gate_s, w_up_s, w_down_s, dc):
    kv_rank = g_kv.shape[1]
    q_rank = g_q.shape[1]
    qk_dim = QK_NOPE + QK_ROPE
    pad = HEAD_PAD - qk_dim
    uq = w_uq[l].reshape(q_rank, N_HEADS, qk_dim) * (qk_dim ** -0.5 * np.log2(np.e))
    zq = jnp.zeros((q_rank, N_HEADS, pad), F32)
    wuq = jnp.concatenate([uq, zq], axis=-1).reshape(q_rank, N_HEADS * HEAD_PAD).astype(BF16)
    wuq_rot = jnp.concatenate([jnp.zeros((q_rank, N_HEADS, QK_NOPE), F32), _rot_cols(uq[..., QK_NOPE:]), zq],
                              axis=-1).reshape(q_rank, N_HEADS * HEAD_PAD).astype(BF16)
    ukv = w_ukv[l].reshape(kv_rank, N_HEADS, QK_NOPE + V_DIM)
    wuk = jnp.concatenate([ukv[..., :QK_NOPE], jnp.zeros((kv_rank, N_HEADS, HEAD_PAD - QK_NOPE), F32)],
                          axis=-1).reshape(kv_rank, N_HEADS * HEAD_PAD).astype(BF16)
    wuv = jnp.concatenate([ukv[..., QK_NOPE:], jnp.zeros((kv_rank, N_HEADS, HEAD_PAD - V_DIM), F32)],
                          axis=-1).reshape(kv_rank, N_HEADS * HEAD_PAD).astype(BF16)
    v_one = jnp.tile((jnp.arange(HEAD_PAD) == V_DIM).astype(F32), N_HEADS)[None]

    wr = w_router[l]
    wr_hi = wr.astype(BF16)
    wr_lo = (wr - wr_hi.astype(F32)).astype(BF16)
    return {
        "g_pre_mix": g_pre_mix[l][None], "g_post_mix": g_post_mix[l][None],
        "g_pre_ffn": g_pre_ffn[l][None], "g_post_ffn": g_post_ffn[l][None],
        "w1": w1, "layer": l, "b_gate": b_gate[l][None], "g_q": g_q[l][None], "g_kv": g_kv[l][None],
        "wuq": wuq, "wuq_rot": wuq_rot, "wuk": wuk, "wuv": wuv, "v_one": v_one, "dc": dc,
        "conv_w": conv_w[l],
        "w_mla_out": w_mla_out[l].astype(BF16), "w_conv_out": w_conv_out[l].astype(BF16),
        "w_four_out": w_four_out[l].astype(BF16), "w_out": w_out[l].astype(BF16),
        "w_router": jnp.concatenate([wr_hi, wr_lo], axis=1), "b_router": b_router[l][None],
        "w_router_t": jnp.concatenate([wr_hi.T, wr_lo.T], axis=0), "b_router_t": b_router[l][:, None],
        "w_gate_s": w_gate_s[l].astype(BF16), "w_up_s": w_up_s[l].astype(BF16),
        "w_down_s": w_down_s[l].astype(BF16),
    }


def _tile(n, pref):
    return pref if n % pref == 0 else n


def kernel(x, c, ctx, c_ctx, w_ada, b_ada, g_pre_mix, g_post_mix, g_pre_ffn, g_post_ffn, w_in, b_gate,
           g_q, w_uq, g_kv, w_ukv, w_mla_out, conv_w, w_conv_out, w_four_out, w_out, w_router, b_router,
           w_gate_e, w_up_e, w_down_e, w_gate_s, w_up_s, w_down_s):
    batch, seq, d = x.shape
    n_ctx = ctx.shape[1]
    n_layers = w_in.shape[0]
    xs = x.reshape(batch * seq, d)
    cs = ctx.reshape(batch * n_ctx, d)

    mod_rows = 16
    c_all = jnp.concatenate([c, c_ctx[None], jnp.zeros((mod_rows - batch - 1, d), F32)], axis=0)
    ada = _ada(c_all, w_ada, b_ada)

    tab_x = _rope_tables(seq)
    tm_c = _tile(n_ctx, 256)
    tab_c = _identity_tables(tm_c)
    cs_x = _position_dft(seq)
    cs_c = _position_dft(n_ctx)
    dc = _channel_dft(w_four_out.shape[1])
    w1 = _prep_w1(jnp.swapaxes(w_in, 1, 2), g_kv.shape[1])

    tm_x = _tile(seq, 512)
    tm_moe_x = _tile(seq, 1024)
    tm_moe_c = _tile(batch * n_ctx, 1024)

    for l in range(n_layers):
        last = l == n_layers - 1
        lw = _layer_weights(l, g_pre_mix, g_post_mix, g_pre_ffn, g_post_ffn, w1, b_gate, g_q, w_uq,
                            g_kv, w_ukv, w_mla_out, conv_w, w_conv_out, w_four_out, w_out, w_router,
                            b_router, w_gate_s, w_up_s, w_down_s, dc)
        mods = ada[l].reshape(mod_rows, 6, d)
        mod_x = mods[:batch]
        mod_c = mods[batch:batch + 1]

        pc = _inproj(cs, mod_c, n_ctx, lw, tab_c, kv_only=last, tm=tm_c)
        px = _inproj(xs, mod_x, seq, lw, tab_x, kv_only=False, tm=tm_x)
        o_x = _attention(px["q"], [(pc["k"], pc["v"], n_ctx), (px["k"], px["v"], seq)], batch, seq, tq=tm_x)
        f_x = _fourier(px["ab"], cs_x, batch, seq, tn=tm_x)
        x1 = _merge(xs, mod_x, seq, px, o_x, f_x, lw, tm=tm_x)
        xs = _moe_sparse(x1, mod_x, seq, lw, w_gate_e, w_up_e, w_down_e, l, tm=tm_moe_x, tm_e=512,
                         n_parts=1)
        if not last:
            o_c = _attention(pc["q"], [(pc["k"], pc["v"], n_ctx)], batch, n_ctx, tq=tm_c)
            f_c = _fourier(pc["ab"], cs_c, batch, n_ctx, tn=tm_c)
            c1 = _merge(cs, mod_c, n_ctx, pc, o_c, f_c, lw, tm=tm_c)
            cs = _moe(c1, mod_c, batch * n_ctx, lw, w_gate_e, w_up_e, w_down_e, l, tm=tm_moe_c)
    return xs.reshape(batch, seq, d)
```

```python
import functools

import numpy as np
import jax
import jax.numpy as jnp
from jax import lax
from jax.experimental import pallas as pl
from jax.experimental.pallas import tpu as pltpu
from jax.experimental.pallas import tpu_sc as plsc

N_HEADS = 8
QK_NOPE = 64
QK_ROPE = 32
V_DIM = 64
GRID_W = 64
ROPE_BASE = 10000.0
FOURIER_GROUPS = 4
TOP_K = 4
ROUTED_SCALE = 2.5
N_BRANCHES = 3
EPS = 1e-6

LANE = 128
HEAD_PAD = LANE
VMEM_LIMIT = 56 * 1024 * 1024
PACK_W = 256
SC_WINDOW = 128

F32 = jnp.float32
BF16 = jnp.bfloat16


def _rms(x, g):
    return x * lax.rsqrt(jnp.mean(x * x, axis=-1, keepdims=True) + EPS) * g


def _sigmoid(x):
    return 1.0 / (1.0 + jnp.exp(-x))


def _dot(a, b):
    return jnp.dot(a, b, preferred_element_type=F32)


def _dot_t(a, b_t):
    return lax.dot_general(a, b_t, (((1,), (1,)), ((), ())), preferred_element_type=F32)


def _resident(shape):
    nd = len(shape)
    return pl.BlockSpec(shape, lambda *_: (0,) * nd, pipeline_mode=pl.Buffered(1))


def _params(n_grid):
    return pltpu.CompilerParams(dimension_semantics=("arbitrary",) * n_grid,
                                vmem_limit_bytes=VMEM_LIMIT)


def _ada_kernel(c_ref, w_ref, b_ref, o_ref):
    c = c_ref[...]
    a = (c * _sigmoid(c)).astype(BF16)
    o_ref[0] = _dot(a, w_ref[0].astype(BF16)) + b_ref[0]


def _ada(c_all, w_ada, b_ada):
    n_layers, d, n_out = w_ada.shape
    rows = c_all.shape[0]
    tn = 1536
    return pl.pallas_call(
        _ada_kernel,
        grid=(n_layers, n_out // tn),
        in_specs=[
            pl.BlockSpec((rows, d), lambda l, j: (0, 0)),
            pl.BlockSpec((1, d, tn), lambda l, j: (l, 0, j)),
            pl.BlockSpec((1, 1, tn), lambda l, j: (l, 0, j)),
        ],
        out_specs=pl.BlockSpec((1, rows, tn), lambda l, j: (l, 0, j)),
        out_shape=jax.ShapeDtypeStruct((n_layers, rows, n_out), F32),
        compiler_params=_params(2),
        name="ada",
    )(c_all, w_ada, b_ada.reshape(n_layers, 1, n_out))


def _inproj_kernel(*refs, kv_only, kv_rank, q_rank, conv_w, four_w, d_model):
    if kv_only:
        (x_ref, mod_ref, gpre_ref, w1_ref, gkv_ref, wuk_ref, wuv_ref, vone_ref, cos_ref, sin_ref,
         k_ref, v_ref) = refs
    else:
        (x_ref, mod_ref, gpre_ref, w1_ref, gkv_ref, wuk_ref, wuv_ref, vone_ref, cos_ref, sin_ref,
         bg_ref, gq_ref, wuq_ref, wuqr_ref, dc_ref,
         k_ref, v_ref, q_ref, cb_ref, cc_ref, cu_ref, ab_ref, gate_ref) = refs

    x = x_ref[...]
    shift = mod_ref[0, 0:1, :]
    scale = mod_ref[0, 1:2, :]
    h = (_rms(x, gpre_ref[...]) * (1.0 + scale) + shift).astype(BF16)
    cos = cos_ref[...]
    sin = sin_ref[...]

    o_kpe = kv_rank
    o_rot = o_kpe + HEAD_PAD
    o_q = o_rot + HEAD_PAD
    p = _dot_t(h, w1_ref[0, 0:o_q, :])
    ckv = _rms(p[:, 0:kv_rank], gkv_ref[...]).astype(BF16)
    kpe = p[:, o_kpe:o_rot] * cos + p[:, o_rot:o_q] * sin
    k = _dot(ckv, wuk_ref[...]) + jnp.concatenate([kpe] * N_HEADS, axis=1)
    k_ref[...] = k.astype(k_ref.dtype)
    v_ref[...] = (_dot(ckv, wuv_ref[...]) + vone_ref[...]).astype(v_ref.dtype)
    if kv_only:
        return

    o_cb = o_q + q_rank
    cq = _rms(_dot_t(h, w1_ref[0, o_q:o_cb, :]), gq_ref[...]).astype(BF16)
    cos_h = jnp.concatenate([cos] * N_HEADS, axis=1)
    sin_h = jnp.concatenate([sin] * N_HEADS, axis=1)
    q = _dot(cq, wuq_ref[...]) * cos_h + _dot(cq, wuqr_ref[...]) * sin_h
    q_ref[...] = q.astype(q_ref.dtype)

    o_cc = o_cb + conv_w
    o_cu = o_cc + conv_w
    o_four = o_cu + conv_w
    cb_ref[...] = _dot_t(h, w1_ref[0, o_cb:o_cc, :]).astype(cb_ref.dtype)
    cc_ref[...] = _dot_t(h, w1_ref[0, o_cc:o_cu, :]).astype(cc_ref.dtype)
    cu_ref[...] = _dot_t(h, w1_ref[0, o_cu:o_four, :]).astype(cu_ref.dtype)

    o_gate = o_four + four_w
    uf = _dot_t(h, w1_ref[0, o_four:o_gate, :]).astype(BF16)
    ab_ref[...] = _dot(uf, dc_ref[...]).astype(ab_ref.dtype)

    for j in range(N_BRANCHES):
        lo = o_gate + j * d_model
        z = _dot_t(h, w1_ref[0, lo:lo + d_model, :]) + bg_ref[:, j * d_model:(j + 1) * d_model]
        gate_ref[:, j * d_model:(j + 1) * d_model] = _sigmoid(z).astype(gate_ref.dtype)


def _inproj(xs, mod, seq_len, lw, tables, *, kv_only, tm):
    t, d = xs.shape
    nb = mod.shape[0]
    tiles_per_seq = seq_len // tm
    cos_t, sin_t = tables
    table_tiles = cos_t.shape[0] // tm
    kv_rank = lw["g_kv"].shape[1]
    q_rank = lw["g_q"].shape[1]
    conv_w = lw["conv_w"].shape[1]
    four_w = lw["dc"].shape[0]
    n_k = N_HEADS * HEAD_PAD

    def row(i):
        return (i, 0)

    def mod_map(i):
        return ((i // tiles_per_seq) % nb, 0, 0)

    def tab_map(i):
        return (i % table_tiles, 0)

    w1, layer = lw["w1"], lw["layer"]
    w1_rows = kv_rank + 2 * HEAD_PAD if kv_only else w1.shape[1]
    in_specs = [
        pl.BlockSpec((tm, d), row),
        pl.BlockSpec((1,) + mod.shape[1:], mod_map),
        _resident((1, d)),
        pl.BlockSpec((1, w1_rows, d), lambda i: (layer, 0, 0), pipeline_mode=pl.Buffered(1)),
        _resident((1, kv_rank)),
        _resident(lw["wuk"].shape),
        _resident(lw["wuv"].shape),
        _resident(lw["v_one"].shape),
        pl.BlockSpec((tm, HEAD_PAD), tab_map),
        pl.BlockSpec((tm, HEAD_PAD), tab_map),
    ]
    args = [xs, mod, lw["g_pre_mix"], w1, lw["g_kv"], lw["wuk"], lw["wuv"], lw["v_one"], cos_t, sin_t]
    out_shape = [jax.ShapeDtypeStruct((t, n_k), BF16), jax.ShapeDtypeStruct((t, n_k), BF16)]
    out_specs = [pl.BlockSpec((tm, n_k), row), pl.BlockSpec((tm, n_k), row)]
    if not kv_only:
        in_specs += [
            _resident(lw["b_gate"].shape),
            _resident((1, q_rank)),
            _resident(lw["wuq"].shape),
            _resident(lw["wuq_rot"].shape),
            _resident(lw["dc"].shape),
        ]
        args += [lw["b_gate"], lw["g_q"], lw["wuq"], lw["wuq_rot"], lw["dc"]]
        widths = [n_k, conv_w, conv_w, conv_w, 2 * four_w, N_BRANCHES * d]
        out_shape += [jax.ShapeDtypeStruct((t, w), BF16) for w in widths]
        out_specs += [pl.BlockSpec((tm, w), row) for w in widths]
    outs = pl.pallas_call(
        functools.partial(_inproj_kernel, kv_only=kv_only, kv_rank=kv_rank, q_rank=q_rank,
                          conv_w=conv_w, four_w=four_w, d_model=d),
        grid=(t // tm,),
        in_specs=in_specs,
        out_specs=out_specs,
        out_shape=out_shape,
        compiler_params=_params(1),
        name="inproj_kv" if kv_only else "inproj",
    )(*args)
    names = ["k", "v", "q", "cb", "cc", "cu", "ab", "gate"]
    return dict(zip(names, outs))


def _attn_kernel(*refs, n_seg):
    q_ref = refs[0]
    o_ref = refs[-1]
    nt = (((1,), (1,)), ((), ()))
    outs = []
    for hh in range(N_HEADS):
        head = slice(hh * HEAD_PAD, (hh + 1) * HEAD_PAD)
        qh = q_ref[:, head]
        s = [lax.dot_general(qh, refs[1 + 2 * i][:, head], nt, preferred_element_type=F32)
             for i in range(n_seg)]
        m = functools.reduce(jnp.maximum, [jnp.max(si, axis=-1, keepdims=True) for si in s])
        acc = functools.reduce(jnp.add, [
            _dot(jnp.exp2((s[i] - m).astype(BF16)), refs[2 + 2 * i][:, head]) for i in range(n_seg)])
        outs.append(acc[:, 0:V_DIM] / acc[:, V_DIM:V_DIM + 1])
    o_ref[...] = jnp.concatenate(outs, axis=1).astype(o_ref.dtype)


def _attention(q, segs, batch, seq_q, *, tq):
    t = q.shape[0]
    qt = seq_q // tq
    n_k = N_HEADS * HEAD_PAD
    in_specs = [pl.BlockSpec((tq, n_k), lambda b, j: (b * qt + j, 0))]
    args = [q]
    for k, v, m in segs:
        in_specs.append(pl.BlockSpec((m, n_k), lambda b, j: (b, 0)))
        in_specs.append(pl.BlockSpec((m, n_k), lambda b, j: (b, 0)))
        args += [k, v]
    return pl.pallas_call(
        functools.partial(_attn_kernel, n_seg=len(segs)),
        grid=(batch, qt),
        in_specs=in_specs,
        out_specs=pl.BlockSpec((tq, N_HEADS * V_DIM), lambda b, j: (b * qt + j, 0)),
        out_shape=jax.ShapeDtypeStruct((t, N_HEADS * V_DIM), BF16),
        compiler_params=_params(2),
        name="attention",
    )(*args)


def _four_kernel(cs_ref, ab_ref, o_ref, *, n, fw):
    o = _dot(cs_ref[:, 0:n], ab_ref[:, 0:fw]) + _dot(cs_ref[:, n:2 * n], ab_ref[:, fw:2 * fw])
    o_ref[...] = o.astype(o_ref.dtype)


def _fourier(ab, cs, batch, seq_len, *, tn):
    t, fw2 = ab.shape
    fw = fw2 // 2
    nt = seq_len // tn
    return pl.pallas_call(
        functools.partial(_four_kernel, n=seq_len, fw=fw),
        grid=(batch, nt),
        in_specs=[
            pl.BlockSpec((tn, 2 * seq_len), lambda b, j: (j, 0)),
            pl.BlockSpec((seq_len, fw2), lambda b, j: (b, 0)),
        ],
        out_specs=pl.BlockSpec((tn, fw), lambda b, j: (b * nt + j, 0)),
        out_shape=jax.ShapeDtypeStruct((t, fw), BF16),
        compiler_params=_params(2),
        name="fourier",
    )(cs, ab)


def _merge_kernel(x_ref, mod_ref, o_ref, cb_ref, cc_ref, cu_ref, ccp_ref, cup_ref, ccn_ref,
                  cun_ref, f_ref, gate_ref, convw_ref, wmo_ref, wco_ref, wfo_ref, wout_ref,
                  gpost_ref, out_ref, pad_ref, *, tiles_per_seq, tm, d_model):
    i = pl.program_id(0)
    pos = i % tiles_per_seq
    has_prev = (pos > 0).astype(F32)
    has_next = (pos < tiles_per_seq - 1).astype(F32)
    pad_ref[0:8, :] = ccp_ref[...].astype(F32) * cup_ref[...].astype(F32) * has_prev
    pad_ref[8:8 + tm, :] = cc_ref[...].astype(F32) * cu_ref[...].astype(F32)
    pad_ref[8 + tm:16 + tm, :] = ccn_ref[...].astype(F32) * cun_ref[...].astype(F32) * has_next
    conv = (pad_ref[7:7 + tm, :] * convw_ref[0:1, :] + pad_ref[8:8 + tm, :] * convw_ref[1:2, :]
            + pad_ref[9:9 + tm, :] * convw_ref[2:3, :])
    y_conv = _dot((cb_ref[...].astype(F32) * conv).astype(BF16), wco_ref[...])
    y_attn = _dot(o_ref[...], wmo_ref[...])
    y_four = _dot(f_ref[...], wfo_ref[...])
    d = d_model
    merged = (gate_ref[:, 0:d].astype(F32) * y_attn + gate_ref[:, d:2 * d].astype(F32) * y_conv
              + gate_ref[:, 2 * d:3 * d].astype(F32) * y_four)
    y = _dot(merged.astype(BF16), wout_ref[...])
    g1 = mod_ref[0, 2:3, :]
    out_ref[...] = x_ref[...] + g1 * _rms(y, gpost_ref[...])


def _merge(xs, mod, seq_len, pr, o, four, lw, *, tm):
    t, d = xs.shape
    nb = mod.shape[0]
    tiles_per_seq = seq_len // tm
    cw = lw["conv_w"].shape[1]
    fw = four.shape[1]
    hb = tm // 8
    last_hb = t // 8 - 1

    def row(i):
        return (i, 0)

    def prev_map(i):
        return (jnp.maximum(i * hb - 1, 0), 0)

    def next_map(i):
        return (jnp.minimum((i + 1) * hb, last_hb), 0)

    in_specs = [
        pl.BlockSpec((tm, d), row),
        pl.BlockSpec((1,) + mod.shape[1:], lambda i: ((i // tiles_per_seq) % nb, 0, 0)),
        pl.BlockSpec((tm, o.shape[1]), row),
        pl.BlockSpec((tm, cw), row),
        pl.BlockSpec((tm, cw), row),
        pl.BlockSpec((tm, cw), row),
        pl.BlockSpec((8, cw), prev_map),
        pl.BlockSpec((8, cw), prev_map),
        pl.BlockSpec((8, cw), next_map),
        pl.BlockSpec((8, cw), next_map),
        pl.BlockSpec((tm, fw), row),
        pl.BlockSpec((tm, N_BRANCHES * d), row),
        _resident(lw["conv_w"].shape),
        _resident(lw["w_mla_out"].shape),
        _resident(lw["w_conv_out"].shape),
        _resident(lw["w_four_out"].shape),
        _resident(lw["w_out"].shape),
        _resident((1, d)),
    ]
    return pl.pallas_call(
        functools.partial(_merge_kernel, tiles_per_seq=tiles_per_seq, tm=tm, d_model=d),
        grid=(t // tm,),
        in_specs=in_specs,
        out_specs=pl.BlockSpec((tm, d), row),
        out_shape=jax.ShapeDtypeStruct((t, d), F32),
        scratch_shapes=[pltpu.VMEM((tm + 16, cw), F32)],
        compiler_params=_params(1),
        name="merge",
    )(xs, mod, o, pr["cb"], pr["cc"], pr["cu"], pr["cc"], pr["cu"], pr["cc"], pr["cu"], four,
      pr["gate"], lw["conv_w"], lw["w_mla_out"], lw["w_conv_out"], lw["w_four_out"], lw["w_out"],
      lw["g_post_mix"])


def _moe_kernel(x_ref, mod_ref, gpre_ref, gpost_ref, wr_ref, br_ref, wgs_ref, wus_ref, wds_ref,
                wge_ref, wue_ref, wde_ref, out_ref, t_ref, comb_ref, acc_ref, *, n_experts):
    e = pl.program_id(1)

    @pl.when(e == 0)
    def _():
        shift = mod_ref[0, 3:4, :]
        scale = mod_ref[0, 4:5, :]
        t = _rms(x_ref[...], gpre_ref[...]) * (1.0 + scale) + shift
        t_hi = t.astype(BF16)
        t_lo = (t - t_hi.astype(F32)).astype(BF16)
        t_ref[...] = t_hi
        hh = _dot(t_hi, wr_ref[...])
        logits = hh[:, 0:n_experts] + hh[:, n_experts:2 * n_experts] + _dot(t_lo, wr_ref[:, 0:n_experts])
        scores = _sigmoid(logits)
        work = scores + br_ref[...]
        lane = lax.broadcasted_iota(jnp.int32, scores.shape, 1)
        comb = jnp.zeros_like(scores)
        for _ in range(TOP_K):
            best = jnp.max(work, axis=-1, keepdims=True)
            first = jnp.min(jnp.where(work == best, lane, n_experts), axis=-1, keepdims=True)
            hit = lane == first
            comb = jnp.where(hit, scores, comb)
            work = jnp.where(hit, -jnp.inf, work)
        comb_ref[...] = comb / jnp.sum(comb, axis=-1, keepdims=True) * ROUTED_SCALE
        gate = _dot(t_hi, wgs_ref[...])
        act = (gate * _sigmoid(gate) * _dot(t_hi, wus_ref[...])).astype(BF16)
        acc_ref[...] = _dot(act, wds_ref[...])

    t_hi = t_ref[...]
    gate = _dot(t_hi, wge_ref[0, 0].astype(BF16))
    up = _dot(t_hi, wue_ref[0, 0].astype(BF16))
    lane = lax.broadcasted_iota(jnp.int32, comb_ref.shape, 1)
    w_e = jnp.sum(jnp.where(lane == e, comb_ref[...], 0.0), axis=-1, keepdims=True)
    act = (gate * _sigmoid(gate) * up * w_e).astype(BF16)
    acc_ref[...] += _dot(act, wde_ref[0, 0].astype(BF16))

    @pl.when(e == n_experts - 1)
    def _():
        g2 = mod_ref[0, 5:6, :]
        out_ref[...] = x_ref[...] + g2 * _rms(acc_ref[...], gpost_ref[...])


def _moe(xs, mod, seq_len, lw, w_gate_e, w_up_e, w_down_e, layer, *, tm):
    t, d = xs.shape
    nb = mod.shape[0]
    tiles_per_seq = max(seq_len // tm, 1)
    n_experts, _, f = w_gate_e.shape[1:]
    in_specs = [
        pl.BlockSpec((tm, d), lambda i, e: (i, 0)),
        pl.BlockSpec((1,) + mod.shape[1:], lambda i, e: ((i // tiles_per_seq) % nb, 0, 0)),
        _resident((1, d)),
        _resident((1, d)),
        _resident(lw["w_router"].shape),
        _resident((1, n_experts)),
        _resident(lw["w_gate_s"].shape),
        _resident(lw["w_up_s"].shape),
        _resident(lw["w_down_s"].shape),
        pl.BlockSpec((1, 1, d, f), lambda i, e: (layer, e, 0, 0)),
        pl.BlockSpec((1, 1, d, f), lambda i, e: (layer, e, 0, 0)),
        pl.BlockSpec((1, 1, f, d), lambda i, e: (layer, e, 0, 0)),
    ]
    return pl.pallas_call(
        functools.partial(_moe_kernel, n_experts=n_experts),
        grid=(t // tm, n_experts),
        in_specs=in_specs,
        out_specs=pl.BlockSpec((tm, d), lambda i, e: (i, 0)),
        out_shape=jax.ShapeDtypeStruct((t, d), F32),
        scratch_shapes=[pltpu.VMEM((tm, d), BF16), pltpu.VMEM((tm, n_experts), F32),
                        pltpu.VMEM((tm, d), F32)],
        compiler_params=_params(2),
        name="moe",
    )(xs, mod, lw["g_pre_ffn"], lw["g_post_ffn"], lw["w_router"], lw["b_router"], lw["w_gate_s"],
      lw["w_up_s"], lw["w_down_s"], w_gate_e, w_up_e, w_down_e)


def _pack_rows(v):
    bits = lax.bitcast_convert_type(v.astype(BF16).astype(F32), jnp.uint32)
    rows = []
    for j in range(v.shape[1] // (2 * PACK_W)):
        lo = bits[:, (2 * j) * PACK_W:(2 * j + 1) * PACK_W]
        hi = bits[:, (2 * j + 1) * PACK_W:(2 * j + 2) * PACK_W]
        rows.append(lax.bitcast_convert_type((hi & jnp.uint32(0xFFFF0000)) | (lo >> 16), jnp.int32))
    return rows


def _unpack_rows(rows):
    parts = []
    for r in rows:
        u = lax.bitcast_convert_type(r, jnp.uint32)
        parts.append(lax.bitcast_convert_type(u << 16, F32))
        parts.append(lax.bitcast_convert_type(u & jnp.uint32(0xFFFF0000), F32))
    return jnp.concatenate(parts, axis=1)


def _route_kernel(x_ref, mod_ref, gpre_ref, wrt_ref, brt_ref, triu_ref,
                  tp_ref, sel_ref, rank_ref, cnt_ref, wgt_ref, *, n_experts):
    shift = mod_ref[0, 3:4, :]
    scale = mod_ref[0, 4:5, :]
    t = _rms(x_ref[...], gpre_ref[...]) * (1.0 + scale) + shift
    t_hi = t.astype(BF16)
    t_lo = (t - t_hi.astype(F32)).astype(BF16)
    for j, r in enumerate(_pack_rows(t)):
        tp_ref[j] = r
    tm = t.shape[0]
    hh = _dot_t(wrt_ref[...], t_hi)
    logits = hh[0:n_experts] + hh[n_experts:2 * n_experts] + _dot_t(wrt_ref[0:n_experts, :], t_lo)
    scores = _sigmoid(logits)
    work = scores + brt_ref[...]
    row = lax.broadcasted_iota(jnp.int32, scores.shape, 0)
    wide = lax.broadcasted_iota(jnp.int32, (LANE, tm), 0)
    firsts, picked, hits = [], [], []
    for k in range(TOP_K):
        best = jnp.max(work, axis=0, keepdims=True)
        first = jnp.min(jnp.where(work == best, row, n_experts), axis=0, keepdims=True)
        hit = row == first
        firsts.append(first)
        picked.append(jnp.sum(jnp.where(hit, scores, 0.0), axis=0, keepdims=True))
        hits.append(wide == first + k * n_experts)
        work = jnp.where(hit, -jnp.inf, work)
    total = functools.reduce(jnp.add, picked)
    onehot = functools.reduce(jnp.add, [jnp.where(h, 1.0, 0.0) for h in hits])
    earlier = _dot(onehot.astype(BF16), triu_ref[...])
    col = jnp.broadcast_to(jnp.sum(onehot, axis=1, keepdims=True), (LANE, LANE))
    row_c = lax.broadcasted_iota(jnp.int32, (LANE, LANE), 0)
    before = jnp.zeros((LANE, LANE), F32)
    for s in range(1, TOP_K):
        before = before + jnp.where(row_c >= s * n_experts, pltpu.roll(col, s * n_experts, 0), 0.0)
    ahead = earlier + before[:, 0:1]
    row8 = lax.broadcasted_iota(jnp.int32, (8, tm), 0)
    sel = jnp.zeros((8, tm), jnp.int32)
    rank = jnp.zeros((8, tm), F32)
    wgt_t = jnp.zeros((LANE, tm), F32)
    for k in range(TOP_K):
        sel = jnp.where(row8 == k, firsts[k], sel)
        rank = jnp.where(row8 == k, jnp.sum(jnp.where(hits[k], ahead, 0.0), axis=0, keepdims=True), rank)
        wgt_t = jnp.where(wide == k, picked[k] / total * ROUTED_SCALE, wgt_t)
    sel_ref[0] = sel
    rank_ref[0] = rank.astype(jnp.int32)
    cnt_ref[0] = col.astype(jnp.int32)
    wgt_ref[...] = wgt_t.T


def _route(x1, mod, seq_len, lw, *, tm, tile_off, n_tiles):
    d = x1.shape[1]
    t = n_tiles * tm
    nb = mod.shape[0]
    tiles_per_seq = max(seq_len // tm, 1)
    n_experts = lw["b_router"].shape[1]
    rows = d // (2 * PACK_W)
    triu = jnp.tri(tm, tm, -1, dtype=BF16).T
    tile3 = lambda i: (i, 0, 0)
    return pl.pallas_call(
        functools.partial(_route_kernel, n_experts=n_experts),
        grid=(n_tiles,),
        in_specs=[
            pl.BlockSpec((tm, d), lambda i: (i + tile_off, 0)),
            pl.BlockSpec((1,) + mod.shape[1:], lambda i: (((i + tile_off) // tiles_per_seq) % nb, 0, 0)),
            _resident((1, d)),
            _resident(lw["w_router_t"].shape),
            _resident((n_experts, 1)),
            _resident((tm, tm)),
        ],
        out_specs=[
            pl.BlockSpec((rows, tm, PACK_W), lambda i: (0, i, 0)),
            pl.BlockSpec((1, 8, tm), tile3),
            pl.BlockSpec((1, 8, tm), tile3),
            pl.BlockSpec((1, LANE, LANE), tile3),
            pl.BlockSpec((tm, LANE), lambda i: (i, 0)),
        ],
        out_shape=[
            jax.ShapeDtypeStruct((rows, t, PACK_W), jnp.int32),
            jax.ShapeDtypeStruct((n_tiles, 8, tm), jnp.int32),
            jax.ShapeDtypeStruct((n_tiles, 8, tm), jnp.int32),
            jax.ShapeDtypeStruct((n_tiles, LANE, LANE), jnp.int32),
            jax.ShapeDtypeStruct((t, LANE), F32),
        ],
        compiler_params=_params(1),
        name="moe_route",
    )(x1, mod, lw["g_pre_ffn"], lw["w_router_t"], lw["b_router_t"], triu)


def _expert_kernel(te_ref, meta_ref, xs_ref, wg_ref, wu_ref, wd_ref, ys_ref, wg_sc, wu_sc, wd_sc):
    i = pl.program_id(0)
    live = i < meta_ref[0]
    new_expert = (i == 0) | (te_ref[i] != te_ref[jnp.maximum(i - 1, 0)])

    @pl.when(live & new_expert)
    def _():
        wg_sc[...] = wg_ref[0, 0].astype(BF16)
        wu_sc[...] = wu_ref[0, 0].astype(BF16)
        wd_sc[...] = wd_ref[0, 0].astype(BF16)

    @pl.when(live)
    def _():
        x = _unpack_rows([xs_ref[j] for j in range(xs_ref.shape[0])]).astype(BF16)
        gate = _dot(x, wg_sc[...])
        act = (gate * _sigmoid(gate) * _dot(x, wu_sc[...])).astype(BF16)
        for j, r in enumerate(_pack_rows(_dot(act, wd_sc[...]))):
            ys_ref[j] = r


def _experts(xs, tile_expert, meta, w_gate_e, w_up_e, w_down_e, layer, *, tm):
    rows, p, _ = xs.shape
    n_experts, d, f = w_gate_e.shape[1:]

    def slot(i, te, meta):
        return (0, jnp.minimum(i, meta[0] - 1), 0)

    grid_spec = pltpu.PrefetchScalarGridSpec(
        num_scalar_prefetch=2,
        grid=(p // tm,),
        in_specs=[
            pl.BlockSpec((rows, tm, PACK_W), slot),
            pl.BlockSpec((1, 1, d, f), lambda i, te, meta: (layer, te[i], 0, 0)),
            pl.BlockSpec((1, 1, d, f), lambda i, te, meta: (layer, te[i], 0, 0)),
            pl.BlockSpec((1, 1, f, d), lambda i, te, meta: (layer, te[i], 0, 0)),
        ],
        out_specs=pl.BlockSpec((rows, tm, PACK_W), slot),
        scratch_shapes=[pltpu.VMEM((d, f), BF16), pltpu.VMEM((d, f), BF16), pltpu.VMEM((f, d), BF16)],
    )
    return pl.pallas_call(
        _expert_kernel,
        grid_spec=grid_spec,
        out_shape=jax.ShapeDtypeStruct(xs.shape, jnp.int32),
        compiler_params=_params(1),
        name="moe_experts",
    )(tile_expert, meta, xs, w_gate_e, w_up_e, w_down_e)


def _combine_kernel(x_ref, mod_ref, gpre_ref, gpost_ref, wgs_ref, wus_ref, wds_ref, yk_ref, wgt_ref,
                    *rest):
    out_ref = rest[-1]
    shift = mod_ref[0, 3:4, :]
    scale = mod_ref[0, 4:5, :]
    x = x_ref[...]
    t_hi = (_rms(x, gpre_ref[...]) * (1.0 + scale) + shift).astype(BF16)
    gate = _dot(t_hi, wgs_ref[...])
    act = (gate * _sigmoid(gate) * _dot(t_hi, wus_ref[...])).astype(BF16)
    acc = _dot(act, wds_ref[...])
    for k in range(TOP_K):
        y = _unpack_rows([yk_ref[k, j] for j in range(yk_ref.shape[1])])
        acc = acc + wgt_ref[:, k:k + 1] * y
    g2 = mod_ref[0, 5:6, :]
    out_ref[...] = x + g2 * _rms(acc, gpost_ref[...])


def _combine(x1, mod, seq_len, lw, yk, wgt, *, tm, tile_off, prev):
    t, d = x1.shape
    nb = mod.shape[0]
    tiles_per_seq = max(seq_len // tm, 1)
    rows = yk.shape[1]
    n_tiles = yk.shape[2] // tm
    glob = lambda i: (i + tile_off, 0)
    in_specs = [
        pl.BlockSpec((tm, d), glob),
        pl.BlockSpec((1,) + mod.shape[1:], lambda i: (((i + tile_off) // tiles_per_seq) % nb, 0, 0)),
        _resident((1, d)),
        _resident((1, d)),
        _resident(lw["w_gate_s"].shape),
        _resident(lw["w_up_s"].shape),
        _resident(lw["w_down_s"].shape),
        pl.BlockSpec((TOP_K, rows, tm, PACK_W), lambda i: (0, 0, i, 0)),
        pl.BlockSpec((tm, LANE), lambda i: (i, 0)),
    ]
    args = [x1, mod, lw["g_pre_ffn"], lw["g_post_ffn"], lw["w_gate_s"], lw["w_up_s"], lw["w_down_s"], yk, wgt]
    aliases = {}
    if prev is not None:
        in_specs.append(pl.BlockSpec(memory_space=pl.ANY))
        aliases = {len(args): 0}
        args.append(prev)
    return pl.pallas_call(
        _combine_kernel,
        grid=(n_tiles,),
        in_specs=in_specs,
        out_specs=pl.BlockSpec((tm, d), glob),
        out_shape=jax.ShapeDtypeStruct((t, d), F32),
        input_output_aliases=aliases,
        compiler_params=_params(1),
        name="moe_combine",
    )(*args)


def _sc_mesh():
    return plsc.VectorSubcoreMesh(core_axis_name="core", subcore_axis_name="subcore")


def _sc_scatter_rows(src, idx, n_out):
    n_lists, n = idx.shape
    width = src.shape[1]

    @pl.kernel(out_type=jax.ShapeDtypeStruct((n_out, width), src.dtype), mesh=_sc_mesh(), scratch_types=[])
    def scatter(x_hbm, *refs):
        i_hbms, o_hbm = refs[:n_lists], refs[n_lists]

        def body(x_vmem, *i_vmems):
            for i_vmem in i_vmems:
                pltpu.sync_copy(x_vmem, o_hbm.at[i_vmem.at[0]])

        pltpu.emit_pipeline(
            body,
            grid=(n // SC_WINDOW,),
            in_specs=[pl.BlockSpec((SC_WINDOW, width), lambda i: (i, 0))]
            + [pl.BlockSpec((1, SC_WINDOW), lambda i: (0, i))] * n_lists,
            out_specs=[],
            core_axis_name=("core", "subcore"),
            dimension_semantics=(pltpu.PARALLEL,),
        )(x_hbm, *i_hbms)

    return scatter(src, *[idx[r].reshape(1, n) for r in range(n_lists)])


def _sc_gather_rows(table, idx):
    n = idx.shape[0]
    width = table.shape[1]

    @pl.kernel(out_type=jax.ShapeDtypeStruct((n, width), table.dtype), mesh=_sc_mesh(), scratch_types=[])
    def gather(x_hbm, i_hbm, o_hbm):
        def body(i_vmem, o_vmem):
            pltpu.sync_copy(x_hbm.at[i_vmem.at[0]], o_vmem)

        pltpu.emit_pipeline(
            body,
            grid=(n // SC_WINDOW,),
            in_specs=[pl.BlockSpec((1, SC_WINDOW), lambda i: (0, i))],
            out_specs=[pl.BlockSpec((SC_WINDOW, width), lambda i: (i, 0))],
            core_axis_name=("core", "subcore"),
            dimension_semantics=(pltpu.PARALLEL,),
        )(i_hbm, o_hbm)

    return gather(table, idx.reshape(1, n))


def _moe_sparse(x1, mod, seq_len, lw, w_gate_e, w_up_e, w_down_e, layer, *, tm, tm_e, n_parts):
    n_experts = lw["b_router"].shape[1]
    assert n_experts * TOP_K == LANE
    n_tok_tiles = x1.shape[0] // tm // n_parts
    t = n_tok_tiles * tm
    n_row_tiles = (t * TOP_K) // tm_e + n_experts
    p = n_row_tiles * tm_e
    experts = jnp.arange(n_experts, dtype=jnp.int32)

    staged = []
    for part in range(n_parts):
        tp, sel, rank, cnt, wgt = _route(x1, mod, seq_len, lw, tm=tm, tile_off=part * n_tok_tiles,
                                         n_tiles=n_tok_tiles)
        rows = tp.shape[0]
        cnt = cnt[:, :, 0].reshape(n_tok_tiles, TOP_K, n_experts).sum(axis=1)
        padded = (cnt.sum(axis=0) + tm_e - 1) // tm_e * tm_e
        group_end = jnp.cumsum(padded)
        base = (group_end - padded)[None, :] + jnp.cumsum(cnt, axis=0) - cnt
        chosen = sel[:, :TOP_K, :, None] == experts
        pos = jnp.sum(jnp.where(chosen, base[:, None, None, :], 0), axis=-1) + rank[:, :TOP_K, :]
        pos = pos.transpose(1, 0, 2).reshape(TOP_K, t)

        n_used = group_end[-1] // tm_e
        tile_start = jnp.arange(n_row_tiles, dtype=jnp.int32) * tm_e
        tile_expert = jnp.sum(tile_start[:, None] >= group_end[None, :], axis=1).astype(jnp.int32)
        tile_expert = jnp.minimum(tile_expert, n_experts - 1)
        tile_expert = jnp.where(tile_start < group_end[-1], tile_expert, tile_expert[n_used - 1])
        meta = jnp.stack([n_used, n_used]).astype(jnp.int32)

        idx = pos[:, None, :] + (jnp.arange(rows, dtype=jnp.int32) * p)[None, :, None]
        xs = _sc_scatter_rows(tp.reshape(rows * t, PACK_W), idx.reshape(TOP_K, rows * t), rows * p)
        staged.append((xs.reshape(rows, p, PACK_W), tile_expert, meta, idx.reshape(-1), wgt))

    gathered = []
    for xs, tile_expert, meta, idx, wgt in staged:
        ys = _experts(xs, tile_expert, meta, w_gate_e, w_up_e, w_down_e, layer, tm=tm_e)
        yk = _sc_gather_rows(ys.reshape(-1, PACK_W), idx)
        gathered.append((yk.reshape(TOP_K, xs.shape[0], t, PACK_W), wgt))

    out = None
    for part, (yk, wgt) in enumerate(gathered):
        out = _combine(x1, mod, seq_len, lw, yk, wgt, tm=tm, tile_off=part * n_tok_tiles, prev=out)
    return out


def _rope_tables(n):
    rows = n // GRID_W
    r, col = jnp.meshgrid(jnp.arange(rows), jnp.arange(GRID_W), indexing="ij")
    r = r.reshape(-1).astype(F32)
    col = col.reshape(-1).astype(F32)
    pairs = QK_ROPE // 4
    inv = ROPE_BASE ** (-jnp.arange(pairs, dtype=F32) / pairs)
    ang = jnp.concatenate([r[:, None] * inv, col[:, None] * inv], axis=-1)
    cos, sin = jnp.cos(ang), jnp.sin(ang)
    pad = HEAD_PAD - QK_NOPE - QK_ROPE
    cos_t = jnp.concatenate([jnp.ones((n, QK_NOPE), F32), cos, cos, jnp.zeros((n, pad), F32)], axis=1)
    sin_t = jnp.concatenate([jnp.zeros((n, QK_NOPE), F32), sin, sin, jnp.zeros((n, pad), F32)], axis=1)
    return cos_t, sin_t


def _identity_tables(n):
    pad = HEAD_PAD - QK_NOPE - QK_ROPE
    cos_t = jnp.concatenate([jnp.ones((n, QK_NOPE + QK_ROPE), F32), jnp.zeros((n, pad), F32)], axis=1)
    return cos_t, jnp.zeros((n, HEAD_PAD), F32)


def _position_dft(n):
    nb = 64 if n % 64 == 0 else 1
    na = n // nb
    m = jnp.arange(n, dtype=jnp.int32)[None, :]
    ang_a = ((jnp.arange(na, dtype=jnp.int32)[:, None] * m) % na).astype(F32) * (2.0 * np.pi / na)
    ang_b = ((jnp.arange(nb, dtype=jnp.int32)[:, None] * m) % n).astype(F32) * (2.0 * np.pi / n)
    ca, sa = jnp.cos(ang_a)[:, None, :], jnp.sin(ang_a)[:, None, :]
    cb, sb = jnp.cos(ang_b)[None, :, :], jnp.sin(ang_b)[None, :, :]
    norm = 1.0 / np.sqrt(n)
    cos = ((ca * cb - sa * sb) * norm).reshape(n, n)
    sin = ((sa * cb + ca * sb) * (-norm)).reshape(n, n)
    return jnp.concatenate([cos, sin], axis=1).astype(BF16)


def _channel_dft(width):
    gc = width // FOURIER_GROUPS
    idx = (jnp.arange(gc, dtype=jnp.int32)[:, None] * jnp.arange(gc, dtype=jnp.int32)[None, :]) % gc
    ang = idx.astype(F32) * (2.0 * np.pi / gc)
    eye = jnp.eye(FOURIER_GROUPS, dtype=F32)
    norm = 1.0 / np.sqrt(gc)
    return jnp.concatenate([jnp.kron(eye, jnp.cos(ang) * norm), jnp.kron(eye, jnp.sin(ang) * norm)],
                           axis=1).astype(BF16)


def _w1_kernel(w_ref, o_ref, *, kv_rank):
    cols = w_ref.shape[2]
    kv_end = kv_rank + QK_ROPE
    half = QK_ROPE // 2
    tail = HEAD_PAD - QK_NOPE - QK_ROPE
    o_kpe = kv_rank
    o_rot = o_kpe + HEAD_PAD
    o_rest = o_rot + HEAD_PAD
    dt = o_ref.dtype
    o_ref[0, 0:kv_rank, :] = w_ref[0, 0:kv_rank, :].astype(dt)
    for base in (o_kpe, o_rot):
        o_ref[0, base:base + QK_NOPE, :] = jnp.zeros((QK_NOPE, cols), dt)
        o_ref[0, base + QK_NOPE + QK_ROPE:base + HEAD_PAD, :] = jnp.zeros((tail, cols), dt)
    o_ref[0, o_kpe + QK_NOPE:o_kpe + QK_NOPE + QK_ROPE, :] = w_ref[0, kv_rank:kv_end, :].astype(dt)
    o_ref[0, o_rot + QK_NOPE:o_rot + QK_NOPE + half, :] = (-w_ref[0, kv_rank + half:kv_end, :]).astype(dt)
    o_ref[0, o_rot + QK_NOPE + half:o_rot + QK_NOPE + QK_ROPE, :] = w_ref[0, kv_rank:kv_rank + half, :].astype(dt)
    o_ref[0, o_rest:, :] = w_ref[0, kv_end:, :].astype(dt)


def _prep_w1(w_in_t, kv_rank):
    n_layers, width, d = w_in_t.shape
    out_w = width - QK_ROPE + 2 * HEAD_PAD
    tc = _tile(d, 256)
    return pl.pallas_call(
        functools.partial(_w1_kernel, kv_rank=kv_rank),
        grid=(n_layers, d // tc),
        in_specs=[pl.BlockSpec((1, width, tc), lambda l, i: (l, 0, i))],
        out_specs=pl.BlockSpec((1, out_w, tc), lambda l, i: (l, 0, i)),
        out_shape=jax.ShapeDtypeStruct((n_layers, out_w, d), BF16),
        compiler_params=_params(2),
        name="prep_w1",
    )(w_in_t)


def _rot_cols(w):
    half = w.shape[-1] // 2
    return jnp.concatenate([-w[..., half:], w[..., :half]], axis=-1)


def _layer_weights(l, g_pre_mix, g_post_mix, g_pre_ffn, g_post_ffn, w1, b_gate, g_q, w_uq, g_kv,
                   w_ukv, w_mla_out, conv_w, w_conv_out, w_four_out, w_out, w_router, b_router,
                   w_gate_s, w_up_s, w_down_s, dc):
    kv_rank = g_kv.shape[1]
    q_rank = g_q.shape[1]
    qk_dim = QK_NOPE + QK_ROPE
    pad = HEAD_PAD - qk_dim
    uq = w_uq[l].reshape(q_rank, N_HEADS, qk_dim) * (qk_dim ** -0.5 * np.log2(np.e))
    zq = jnp.zeros((q_rank, N_HEADS, pad), F32)
    wuq = jnp.concatenate([uq, zq], axis=-1).reshape(q_rank, N_HEADS * HEAD_PAD).astype(BF16)
    wuq_rot = jnp.concatenate([jnp.zeros((q_rank, N_HEADS, QK_NOPE), F32), _rot_cols(uq[..., QK_NOPE:]), zq],
                              axis=-1).reshape(q_rank, N_HEADS * HEAD_PAD).astype(BF16)
    ukv = w_ukv[l].reshape(kv_rank, N_HEADS, QK_NOPE + V_DIM)
    wuk = jnp.concatenate([ukv[..., :QK_NOPE], jnp.zeros((kv_rank, N_HEADS, HEAD_PAD - QK_NOPE), F32)],
                          axis=-1).reshape(kv_rank, N_HEADS * HEAD_PAD).astype(BF16)
    wuv = jnp.concatenate([ukv[..., QK_NOPE:], jnp.zeros((kv_rank, N_HEADS, HEAD_PAD - V_DIM), F32)],
                          axis=-1).reshape(kv_rank, N_HEADS * HEAD_PAD).astype(BF16)
    v_one = jnp.tile((jnp.arange(HEAD_PAD) == V_DIM).astype(F32), N_HEADS)[None]

    wr = w_router[l]
    wr_hi = wr.astype(BF16)
    wr_lo = (wr - wr_hi.astype(F32)).astype(BF16)
    return {
        "g_pre_mix": g_pre_mix[l][None], "g_post_mix": g_post_mix[l][None],
        "g_pre_ffn": g_pre_ffn[l][None], "g_post_ffn": g_post_ffn[l][None],
        "w1": w1, "layer": l, "b_gate": b_gate[l][None], "g_q": g_q[l][None], "g_kv": g_kv[l][None],
        "wuq": wuq, "wuq_rot": wuq_rot, "wuk": wuk, "wuv": wuv, "v_one": v_one, "dc": dc,
        "conv_w": conv_w[l],
        "w_mla_out": w_mla_out[l].astype(BF16), "w_conv_out": w_conv_out[l].astype(BF16),
        "w_four_out": w_four_out[l].astype(BF16), "w_out": w_out[l].astype(BF16),
        "w_router": jnp.concatenate([wr_hi, wr_lo], axis=1), "b_router": b_router[l][None],
        "w_router_t": jnp.concatenate([wr_hi.T, wr_lo.T], axis=0), "b_router_t": b_router[l][:, None],
        "w_gate_s": w_gate_s[l].astype(BF16), "w_up_s": w_up_s[l].astype(BF16),
        "w_down_s": w_down_s[l].astype(BF16),
    }


def _tile(n, pref):
    return pref if n % pref == 0 else n


def kernel(x, c, ctx, c_ctx, w_ada, b_ada, g_pre_mix, g_post_mix, g_pre_ffn, g_post_ffn, w_in, b_gate,
           g_q, w_uq, g_kv, w_ukv, w_mla_out, conv_w, w_conv_out, w_four_out, w_out, w_router, b_router,
           w_gate_e, w_up_e, w_down_e, w_gate_s, w_up_s, w_down_s):
    batch, seq, d = x.shape
    n_ctx = ctx.shape[1]
    n_layers = w_in.shape[0]
    xs = x.reshape(batch * seq, d)
    cs = ctx.reshape(batch * n_ctx, d)

    mod_rows = 16
    c_all = jnp.concatenate([c, c_ctx[None], jnp.zeros((mod_rows - batch - 1, d), F32)], axis=0)
    ada = _ada(c_all, w_ada, b_ada)

    tab_x = _rope_tables(seq)
    tm_c = _tile(n_ctx, 256)
    tab_c = _identity_tables(tm_c)
    cs_x = _position_dft(seq)
    cs_c = _position_dft(n_ctx)
    dc = _channel_dft(w_four_out.shape[1])
    w1 = _prep_w1(jnp.swapaxes(w_in, 1, 2), g_kv.shape[1])

    tm_x = _tile(seq, 512)
    tm_moe_x = _tile(seq, 1024)
    tm_moe_c = _tile(batch * n_ctx, 1024)

    for l in range(n_layers):
        last = l == n_layers - 1
        lw = _layer_weights(l, g_pre_mix, g_post_mix, g_pre_ffn, g_post_ffn, w1, b_gate, g_q, w_uq,
                            g_kv, w_ukv, w_mla_out, conv_w, w_conv_out, w_four_out, w_out, w_router,
                            b_router, w_gate_s, w_up_s, w_down_s, dc)
        mods = ada[l].reshape(mod_rows, 6, d)
        mod_x = mods[:batch]
        mod_c = mods[batch:batch + 1]

        pc = _inproj(cs, mod_c, n_ctx, lw, tab_c, kv_only=last, tm=tm_c)
        px = _inproj(xs, mod_x, seq, lw, tab_x, kv_only=False, tm=tm_x)
        o_x = _attention(px["q"], [(pc["k"], pc["v"], n_ctx), (px["k"], px["v"], seq)], batch, seq, tq=tm_x)
        f_x = _fourier(px["ab"], cs_x, batch, seq, tn=tm_x)
        x1 = _merge(xs, mod_x, seq, px, o_x, f_x, lw, tm=tm_x)
        xs = _moe_sparse(x1, mod_x, seq, lw, w_gate_e, w_up_e, w_down_e, l, tm=tm_moe_x, tm_e=1024,
                         n_parts=1)
        if not last:
            o_c = _attention(pc["q"], [(pc["k"], pc["v"], n_ctx)], batch, n_ctx, tq=tm_c)
            f_c = _fourier(pc["ab"], cs_c, batch, n_ctx, tn=tm_c)
            c1 = _merge(cs, mod_c, n_ctx, pc, o_c, f_c, lw, tm=tm_c)
            cs = _moe(c1, mod_c, batch * n_ctx, lw, w_gate_e, w_up_e, w_down_e, l, tm=tm_moe_c)
    return xs.reshape(batch, seq, d)
```

```python
import functools

import numpy as np
import jax
import jax.numpy as jnp
from jax import lax
from jax.experimental import pallas as pl
from jax.experimental.pallas import tpu as pltpu
from jax.experimental.pallas import tpu_sc as plsc

N_HEADS = 8
QK_NOPE = 64
QK_ROPE = 32
V_DIM = 64
GRID_W = 64
ROPE_BASE = 10000.0
FOURIER_GROUPS = 4
TOP_K = 4
ROUTED_SCALE = 2.5
N_BRANCHES = 3
EPS = 1e-6

LANE = 128
HEAD_PAD = LANE
VMEM_LIMIT = 56 * 1024 * 1024
PACK_W = 256
SC_WINDOW = 128

F32 = jnp.float32
BF16 = jnp.bfloat16


def _rms(x, g):
    return x * lax.rsqrt(jnp.mean(x * x, axis=-1, keepdims=True) + EPS) * g


def _sigmoid(x):
    return 1.0 / (1.0 + jnp.exp(-x))


def _dot(a, b):
    return jnp.dot(a, b, preferred_element_type=F32)


def _dot_t(a, b_t):
    return lax.dot_general(a, b_t, (((1,), (1,)), ((), ())), preferred_element_type=F32)


def _resident(shape):
    nd = len(shape)
    return pl.BlockSpec(shape, lambda *_: (0,) * nd, pipeline_mode=pl.Buffered(1))


def _params(n_grid):
    return pltpu.CompilerParams(dimension_semantics=("arbitrary",) * n_grid,
                                vmem_limit_bytes=VMEM_LIMIT)


def _ada_kernel(c_ref, w_ref, b_ref, o_ref):
    c = c_ref[...]
    a = (c * _sigmoid(c)).astype(BF16)
    o_ref[0] = _dot(a, w_ref[0].astype(BF16)) + b_ref[0]


def _ada(c_all, w_ada, b_ada):
    n_layers, d, n_out = w_ada.shape
    rows = c_all.shape[0]
    tn = 1536
    return pl.pallas_call(
        _ada_kernel,
        grid=(n_layers, n_out // tn),
        in_specs=[
            pl.BlockSpec((rows, d), lambda l, j: (0, 0)),
            pl.BlockSpec((1, d, tn), lambda l, j: (l, 0, j)),
            pl.BlockSpec((1, 1, tn), lambda l, j: (l, 0, j)),
        ],
        out_specs=pl.BlockSpec((1, rows, tn), lambda l, j: (l, 0, j)),
        out_shape=jax.ShapeDtypeStruct((n_layers, rows, n_out), F32),
        compiler_params=_params(2),
        name="ada",
    )(c_all, w_ada, b_ada.reshape(n_layers, 1, n_out))


def _inproj_kernel(*refs, kv_only, kv_rank, q_rank, conv_w, four_w, d_model):
    if kv_only:
        (x_ref, mod_ref, gpre_ref, w1_ref, gkv_ref, wuk_ref, wuv_ref, vone_ref, cos_ref, sin_ref,
         k_ref, v_ref) = refs
    else:
        (x_ref, mod_ref, gpre_ref, w1_ref, gkv_ref, wuk_ref, wuv_ref, vone_ref, cos_ref, sin_ref,
         bg_ref, gq_ref, wuq_ref, wuqr_ref, dc_ref,
         k_ref, v_ref, q_ref, cb_ref, cc_ref, cu_ref, ab_ref, gate_ref) = refs

    x = x_ref[...]
    shift = mod_ref[0, 0:1, :]
    scale = mod_ref[0, 1:2, :]
    h = (_rms(x, gpre_ref[...]) * (1.0 + scale) + shift).astype(BF16)
    cos = cos_ref[...]
    sin = sin_ref[...]

    o_kpe = kv_rank
    o_rot = o_kpe + HEAD_PAD
    o_q = o_rot + HEAD_PAD
    p = _dot_t(h, w1_ref[0, 0:o_q, :])
    ckv = _rms(p[:, 0:kv_rank], gkv_ref[...]).astype(BF16)
    kpe = p[:, o_kpe:o_rot] * cos + p[:, o_rot:o_q] * sin
    k = _dot(ckv, wuk_ref[...]) + jnp.concatenate([kpe] * N_HEADS, axis=1)
    k_ref[...] = k.astype(k_ref.dtype)
    v_ref[...] = (_dot(ckv, wuv_ref[...]) + vone_ref[...]).astype(v_ref.dtype)
    if kv_only:
        return

    o_cb = o_q + q_rank
    cq = _rms(_dot_t(h, w1_ref[0, o_q:o_cb, :]), gq_ref[...]).astype(BF16)
    cos_h = jnp.concatenate([cos] * N_HEADS, axis=1)
    sin_h = jnp.concatenate([sin] * N_HEADS, axis=1)
    q = _dot(cq, wuq_ref[...]) * cos_h + _dot(cq, wuqr_ref[...]) * sin_h
    q_ref[...] = q.astype(q_ref.dtype)

    o_cc = o_cb + conv_w
    o_cu = o_cc + conv_w
    o_four = o_cu + conv_w
    cb_ref[...] = _dot_t(h, w1_ref[0, o_cb:o_cc, :]).astype(cb_ref.dtype)
    cc_ref[...] = _dot_t(h, w1_ref[0, o_cc:o_cu, :]).astype(cc_ref.dtype)
    cu_ref[...] = _dot_t(h, w1_ref[0, o_cu:o_four, :]).astype(cu_ref.dtype)

    o_gate = o_four + four_w
    uf = _dot_t(h, w1_ref[0, o_four:o_gate, :]).astype(BF16)
    ab_ref[...] = _dot(uf, dc_ref[...]).astype(ab_ref.dtype)

    for j in range(N_BRANCHES):
        lo = o_gate + j * d_model
        z = _dot_t(h, w1_ref[0, lo:lo + d_model, :]) + bg_ref[:, j * d_model:(j + 1) * d_model]
        gate_ref[:, j * d_model:(j + 1) * d_model] = _sigmoid(z).astype(gate_ref.dtype)


def _inproj(xs, mod, seq_len, lw, tables, *, kv_only, tm):
    t, d = xs.shape
    nb = mod.shape[0]
    tiles_per_seq = seq_len // tm
    cos_t, sin_t = tables
    table_tiles = cos_t.shape[0] // tm
    kv_rank = lw["g_kv"].shape[1]
    q_rank = lw["g_q"].shape[1]
    conv_w = lw["conv_w"].shape[1]
    four_w = lw["dc"].shape[0]
    n_k = N_HEADS * HEAD_PAD

    def row(i):
        return (i, 0)

    def mod_map(i):
        return ((i // tiles_per_seq) % nb, 0, 0)

    def tab_map(i):
        return (i % table_tiles, 0)

    w1, layer = lw["w1"], lw["layer"]
    w1_rows = kv_rank + 2 * HEAD_PAD if kv_only else w1.shape[1]
    in_specs = [
        pl.BlockSpec((tm, d), row),
        pl.BlockSpec((1,) + mod.shape[1:], mod_map),
        _resident((1, d)),
        pl.BlockSpec((1, w1_rows, d), lambda i: (layer, 0, 0), pipeline_mode=pl.Buffered(1)),
        _resident((1, kv_rank)),
        _resident(lw["wuk"].shape),
        _resident(lw["wuv"].shape),
        _resident(lw["v_one"].shape),
        pl.BlockSpec((tm, HEAD_PAD), tab_map),
        pl.BlockSpec((tm, HEAD_PAD), tab_map),
    ]
    args = [xs, mod, lw["g_pre_mix"], w1, lw["g_kv"], lw["wuk"], lw["wuv"], lw["v_one"], cos_t, sin_t]
    out_shape = [jax.ShapeDtypeStruct((t, n_k), BF16), jax.ShapeDtypeStruct((t, n_k), BF16)]
    out_specs = [pl.BlockSpec((tm, n_k), row), pl.BlockSpec((tm, n_k), row)]
    if not kv_only:
        in_specs += [
            _resident(lw["b_gate"].shape),
            _resident((1, q_rank)),
            _resident(lw["wuq"].shape),
            _resident(lw["wuq_rot"].shape),
            _resident(lw["dc"].shape),
        ]
        args += [lw["b_gate"], lw["g_q"], lw["wuq"], lw["wuq_rot"], lw["dc"]]
        widths = [n_k, conv_w, conv_w, conv_w, 2 * four_w, N_BRANCHES * d]
        out_shape += [jax.ShapeDtypeStruct((t, w), BF16) for w in widths]
        out_specs += [pl.BlockSpec((tm, w), row) for w in widths]
    outs = pl.pallas_call(
        functools.partial(_inproj_kernel, kv_only=kv_only, kv_rank=kv_rank, q_rank=q_rank,
                          conv_w=conv_w, four_w=four_w, d_model=d),
        grid=(t // tm,),
        in_specs=in_specs,
        out_specs=out_specs,
        out_shape=out_shape,
        compiler_params=_params(1),
        name="inproj_kv" if kv_only else "inproj",
    )(*args)
    names = ["k", "v", "q", "cb", "cc", "cu", "ab", "gate"]
    return dict(zip(names, outs))


def _attn_kernel(*refs, n_seg):
    q_ref = refs[0]
    o_ref = refs[-1]
    nt = (((1,), (1,)), ((), ()))
    outs = []
    for hh in range(N_HEADS):
        head = slice(hh * HEAD_PAD, (hh + 1) * HEAD_PAD)
        qh = q_ref[:, head]
        s = [lax.dot_general(qh, refs[1 + 2 * i][:, head], nt, preferred_element_type=F32)
             for i in range(n_seg)]
        m = functools.reduce(jnp.maximum, [jnp.max(si, axis=-1, keepdims=True) for si in s])
        acc = functools.reduce(jnp.add, [
            _dot(jnp.exp2((s[i] - m).astype(BF16)), refs[2 + 2 * i][:, head]) for i in range(n_seg)])
        outs.append(acc[:, 0:V_DIM] / acc[:, V_DIM:V_DIM + 1])
    o_ref[...] = jnp.concatenate(outs, axis=1).astype(o_ref.dtype)


def _attention(q, segs, batch, seq_q, *, tq):
    t = q.shape[0]
    qt = seq_q // tq
    n_k = N_HEADS * HEAD_PAD
    in_specs = [pl.BlockSpec((tq, n_k), lambda b, j: (b * qt + j, 0))]
    args = [q]
    for k, v, m in segs:
        in_specs.append(pl.BlockSpec((m, n_k), lambda b, j: (b, 0)))
        in_specs.append(pl.BlockSpec((m, n_k), lambda b, j: (b, 0)))
        args += [k, v]
    return pl.pallas_call(
        functools.partial(_attn_kernel, n_seg=len(segs)),
        grid=(batch, qt),
        in_specs=in_specs,
        out_specs=pl.BlockSpec((tq, N_HEADS * V_DIM), lambda b, j: (b * qt + j, 0)),
        out_shape=jax.ShapeDtypeStruct((t, N_HEADS * V_DIM), BF16),
        compiler_params=_params(2),
        name="attention",
    )(*args)


def _four_kernel(cs_ref, ab_ref, o_ref, *, n, fw):
    o = _dot(cs_ref[:, 0:n], ab_ref[:, 0:fw]) + _dot(cs_ref[:, n:2 * n], ab_ref[:, fw:2 * fw])
    o_ref[...] = o.astype(o_ref.dtype)


def _fourier(ab, cs, batch, seq_len, *, tn):
    t, fw2 = ab.shape
    fw = fw2 // 2
    nt = seq_len // tn
    return pl.pallas_call(
        functools.partial(_four_kernel, n=seq_len, fw=fw),
        grid=(batch, nt),
        in_specs=[
            pl.BlockSpec((tn, 2 * seq_len), lambda b, j: (j, 0)),
            pl.BlockSpec((seq_len, fw2), lambda b, j: (b, 0)),
        ],
        out_specs=pl.BlockSpec((tn, fw), lambda b, j: (b * nt + j, 0)),
        out_shape=jax.ShapeDtypeStruct((t, fw), BF16),
        compiler_params=_params(2),
        name="fourier",
    )(cs, ab)


def _merge_kernel(x_ref, mod_ref, o_ref, cb_ref, cc_ref, cu_ref, ccp_ref, cup_ref, ccn_ref,
                  cun_ref, f_ref, gate_ref, convw_ref, wmo_ref, wco_ref, wfo_ref, wout_ref,
                  gpost_ref, out_ref, pad_ref, *, tiles_per_seq, tm, d_model):
    i = pl.program_id(0)
    pos = i % tiles_per_seq
    has_prev = (pos > 0).astype(F32)
    has_next = (pos < tiles_per_seq - 1).astype(F32)
    pad_ref[0:8, :] = ccp_ref[...].astype(F32) * cup_ref[...].astype(F32) * has_prev
    pad_ref[8:8 + tm, :] = cc_ref[...].astype(F32) * cu_ref[...].astype(F32)
    pad_ref[8 + tm:16 + tm, :] = ccn_ref[...].astype(F32) * cun_ref[...].astype(F32) * has_next
    conv = (pad_ref[7:7 + tm, :] * convw_ref[0:1, :] + pad_ref[8:8 + tm, :] * convw_ref[1:2, :]
            + pad_ref[9:9 + tm, :] * convw_ref[2:3, :])
    y_conv = _dot((cb_ref[...].astype(F32) * conv).astype(BF16), wco_ref[...])
    y_attn = _dot(o_ref[...], wmo_ref[...])
    y_four = _dot(f_ref[...], wfo_ref[...])
    d = d_model
    merged = (gate_ref[:, 0:d].astype(F32) * y_attn + gate_ref[:, d:2 * d].astype(F32) * y_conv
              + gate_ref[:, 2 * d:3 * d].astype(F32) * y_four)
    y = _dot(merged.astype(BF16), wout_ref[...])
    g1 = mod_ref[0, 2:3, :]
    out_ref[...] = x_ref[...] + g1 * _rms(y, gpost_ref[...])


def _merge(xs, mod, seq_len, pr, o, four, lw, *, tm):
    t, d = xs.shape
    nb = mod.shape[0]
    tiles_per_seq = seq_len // tm
    cw = lw["conv_w"].shape[1]
    fw = four.shape[1]
    hb = tm // 8
    last_hb = t // 8 - 1

    def row(i):
        return (i, 0)

    def prev_map(i):
        return (jnp.maximum(i * hb - 1, 0), 0)

    def next_map(i):
        return (jnp.minimum((i + 1) * hb, last_hb), 0)

    in_specs = [
        pl.BlockSpec((tm, d), row),
        pl.BlockSpec((1,) + mod.shape[1:], lambda i: ((i // tiles_per_seq) % nb, 0, 0)),
        pl.BlockSpec((tm, o.shape[1]), row),
        pl.BlockSpec((tm, cw), row),
        pl.BlockSpec((tm, cw), row),
        pl.BlockSpec((tm, cw), row),
        pl.BlockSpec((8, cw), prev_map),
        pl.BlockSpec((8, cw), prev_map),
        pl.BlockSpec((8, cw), next_map),
        pl.BlockSpec((8, cw), next_map),
        pl.BlockSpec((tm, fw), row),
        pl.BlockSpec((tm, N_BRANCHES * d), row),
        _resident(lw["conv_w"].shape),
        _resident(lw["w_mla_out"].shape),
        _resident(lw["w_conv_out"].shape),
        _resident(lw["w_four_out"].shape),
        _resident(lw["w_out"].shape),
        _resident((1, d)),
    ]
    return pl.pallas_call(
        functools.partial(_merge_kernel, tiles_per_seq=tiles_per_seq, tm=tm, d_model=d),
        grid=(t // tm,),
        in_specs=in_specs,
        out_specs=pl.BlockSpec((tm, d), row),
        out_shape=jax.ShapeDtypeStruct((t, d), F32),
        scratch_shapes=[pltpu.VMEM((tm + 16, cw), F32)],
        compiler_params=_params(1),
        name="merge",
    )(xs, mod, o, pr["cb"], pr["cc"], pr["cu"], pr["cc"], pr["cu"], pr["cc"], pr["cu"], four,
      pr["gate"], lw["conv_w"], lw["w_mla_out"], lw["w_conv_out"], lw["w_four_out"], lw["w_out"],
      lw["g_post_mix"])


def _moe_kernel(x_ref, mod_ref, gpre_ref, gpost_ref, wr_ref, br_ref, wgs_ref, wus_ref, wds_ref,
                wge_ref, wue_ref, wde_ref, out_ref, t_ref, comb_ref, acc_ref, *, n_experts):
    e = pl.program_id(1)

    @pl.when(e == 0)
    def _():
        shift = mod_ref[0, 3:4, :]
        scale = mod_ref[0, 4:5, :]
        t = _rms(x_ref[...], gpre_ref[...]) * (1.0 + scale) + shift
        t_hi = t.astype(BF16)
        t_lo = (t - t_hi.astype(F32)).astype(BF16)
        t_ref[...] = t_hi
        hh = _dot(t_hi, wr_ref[...])
        logits = hh[:, 0:n_experts] + hh[:, n_experts:2 * n_experts] + _dot(t_lo, wr_ref[:, 0:n_experts])
        scores = _sigmoid(logits)
        work = scores + br_ref[...]
        lane = lax.broadcasted_iota(jnp.int32, scores.shape, 1)
        comb = jnp.zeros_like(scores)
        for _ in range(TOP_K):
            best = jnp.max(work, axis=-1, keepdims=True)
            first = jnp.min(jnp.where(work == best, lane, n_experts), axis=-1, keepdims=True)
            hit = lane == first
            comb = jnp.where(hit, scores, comb)
            work = jnp.where(hit, -jnp.inf, work)
        comb_ref[...] = comb / jnp.sum(comb, axis=-1, keepdims=True) * ROUTED_SCALE
        gate = _dot(t_hi, wgs_ref[...])
        act = (gate * _sigmoid(gate) * _dot(t_hi, wus_ref[...])).astype(BF16)
        acc_ref[...] = _dot(act, wds_ref[...])

    t_hi = t_ref[...]
    gate = _dot(t_hi, wge_ref[0, 0].astype(BF16))
    up = _dot(t_hi, wue_ref[0, 0].astype(BF16))
    lane = lax.broadcasted_iota(jnp.int32, comb_ref.shape, 1)
    w_e = jnp.sum(jnp.where(lane == e, comb_ref[...], 0.0), axis=-1, keepdims=True)
    act = (gate * _sigmoid(gate) * up * w_e).astype(BF16)
    acc_ref[...] += _dot(act, wde_ref[0, 0].astype(BF16))

    @pl.when(e == n_experts - 1)
    def _():
        g2 = mod_ref[0, 5:6, :]
        out_ref[...] = x_ref[...] + g2 * _rms(acc_ref[...], gpost_ref[...])


def _moe(xs, mod, seq_len, lw, w_gate_e, w_up_e, w_down_e, layer, *, tm):
    t, d = xs.shape
    nb = mod.shape[0]
    tiles_per_seq = max(seq_len // tm, 1)
    n_experts, _, f = w_gate_e.shape[1:]
    in_specs = [
        pl.BlockSpec((tm, d), lambda i, e: (i, 0)),
        pl.BlockSpec((1,) + mod.shape[1:], lambda i, e: ((i // tiles_per_seq) % nb, 0, 0)),
        _resident((1, d)),
        _resident((1, d)),
        _resident(lw["w_router"].shape),
        _resident((1, n_experts)),
        _resident(lw["w_gate_s"].shape),
        _resident(lw["w_up_s"].shape),
        _resident(lw["w_down_s"].shape),
        pl.BlockSpec((1, 1, d, f), lambda i, e: (layer, e, 0, 0)),
        pl.BlockSpec((1, 1, d, f), lambda i, e: (layer, e, 0, 0)),
        pl.BlockSpec((1, 1, f, d), lambda i, e: (layer, e, 0, 0)),
    ]
    return pl.pallas_call(
        functools.partial(_moe_kernel, n_experts=n_experts),
        grid=(t // tm, n_experts),
        in_specs=in_specs,
        out_specs=pl.BlockSpec((tm, d), lambda i, e: (i, 0)),
        out_shape=jax.ShapeDtypeStruct((t, d), F32),
        scratch_shapes=[pltpu.VMEM((tm, d), BF16), pltpu.VMEM((tm, n_experts), F32),
                        pltpu.VMEM((tm, d), F32)],
        compiler_params=_params(2),
        name="moe",
    )(xs, mod, lw["g_pre_ffn"], lw["g_post_ffn"], lw["w_router"], lw["b_router"], lw["w_gate_s"],
      lw["w_up_s"], lw["w_down_s"], w_gate_e, w_up_e, w_down_e)


def _pack_rows(v):
    bits = lax.bitcast_convert_type(v.astype(BF16).astype(F32), jnp.uint32)
    rows = []
    for j in range(v.shape[1] // (2 * PACK_W)):
        lo = bits[:, (2 * j) * PACK_W:(2 * j + 1) * PACK_W]
        hi = bits[:, (2 * j + 1) * PACK_W:(2 * j + 2) * PACK_W]
        rows.append(lax.bitcast_convert_type((hi & jnp.uint32(0xFFFF0000)) | (lo >> 16), jnp.int32))
    return rows


def _unpack_rows(rows):
    parts = []
    for r in rows:
        u = lax.bitcast_convert_type(r, jnp.uint32)
        parts.append(lax.bitcast_convert_type(u << 16, F32))
        parts.append(lax.bitcast_convert_type(u & jnp.uint32(0xFFFF0000), F32))
    return jnp.concatenate(parts, axis=1)


def _route_kernel(x_ref, mod_ref, gpre_ref, wrt_ref, brt_ref, triu_ref,
                  tp_ref, sel_ref, rank_ref, cnt_ref, wgt_ref, *, n_experts):
    shift = mod_ref[0, 3:4, :]
    scale = mod_ref[0, 4:5, :]
    t = _rms(x_ref[...], gpre_ref[...]) * (1.0 + scale) + shift
    t_hi = t.astype(BF16)
    t_lo = (t - t_hi.astype(F32)).astype(BF16)
    for j, r in enumerate(_pack_rows(t)):
        tp_ref[j] = r
    tm = t.shape[0]
    hh = _dot_t(wrt_ref[...], t_hi)
    logits = hh[0:n_experts] + hh[n_experts:2 * n_experts] + _dot_t(wrt_ref[0:n_experts, :], t_lo)
    scores = _sigmoid(logits)
    work = scores + brt_ref[...]
    row = lax.broadcasted_iota(jnp.int32, scores.shape, 0)
    wide = lax.broadcasted_iota(jnp.int32, (LANE, tm), 0)
    firsts, picked, hits = [], [], []
    for k in range(TOP_K):
        best = jnp.max(work, axis=0, keepdims=True)
        first = jnp.min(jnp.where(work == best, row, n_experts), axis=0, keepdims=True)
        hit = row == first
        firsts.append(first)
        picked.append(jnp.sum(jnp.where(hit, scores, 0.0), axis=0, keepdims=True))
        hits.append(wide == first + k * n_experts)
        work = jnp.where(hit, -jnp.inf, work)
    total = functools.reduce(jnp.add, picked)
    onehot = functools.reduce(jnp.add, [jnp.where(h, 1.0, 0.0) for h in hits])
    earlier = _dot(onehot.astype(BF16), triu_ref[...])
    col = jnp.broadcast_to(jnp.sum(onehot, axis=1, keepdims=True), (LANE, LANE))
    row_c = lax.broadcasted_iota(jnp.int32, (LANE, LANE), 0)
    before = jnp.zeros((LANE, LANE), F32)
    for s in range(1, TOP_K):
        before = before + jnp.where(row_c >= s * n_experts, pltpu.roll(col, s * n_experts, 0), 0.0)
    ahead = earlier + before[:, 0:1]
    row8 = lax.broadcasted_iota(jnp.int32, (8, tm), 0)
    sel = jnp.zeros((8, tm), jnp.int32)
    rank = jnp.zeros((8, tm), F32)
    wgt_t = jnp.zeros((LANE, tm), F32)
    for k in range(TOP_K):
        sel = jnp.where(row8 == k, firsts[k], sel)
        rank = jnp.where(row8 == k, jnp.sum(jnp.where(hits[k], ahead, 0.0), axis=0, keepdims=True), rank)
        wgt_t = jnp.where(wide == k, picked[k] / total * ROUTED_SCALE, wgt_t)
    sel_ref[0] = sel
    rank_ref[0] = rank.astype(jnp.int32)
    cnt_ref[0] = col.astype(jnp.int32)
    wgt_ref[...] = wgt_t.T


def _route(x1, mod, seq_len, lw, *, tm, tile_off, n_tiles):
    d = x1.shape[1]
    t = n_tiles * tm
    nb = mod.shape[0]
    tiles_per_seq = max(seq_len // tm, 1)
    n_experts = lw["b_router"].shape[1]
    rows = d // (2 * PACK_W)
    triu = jnp.tri(tm, tm, -1, dtype=BF16).T
    tile3 = lambda i: (i, 0, 0)
    return pl.pallas_call(
        functools.partial(_route_kernel, n_experts=n_experts),
        grid=(n_tiles,),
        in_specs=[
            pl.BlockSpec((tm, d), lambda i: (i + tile_off, 0)),
            pl.BlockSpec((1,) + mod.shape[1:], lambda i: (((i + tile_off) // tiles_per_seq) % nb, 0, 0)),
            _resident((1, d)),
            _resident(lw["w_router_t"].shape),
            _resident((n_experts, 1)),
            _resident((tm, tm)),
        ],
        out_specs=[
            pl.BlockSpec((rows, tm, PACK_W), lambda i: (0, i, 0)),
            pl.BlockSpec((1, 8, tm), tile3),
            pl.BlockSpec((1, 8, tm), tile3),
            pl.BlockSpec((1, LANE, LANE), tile3),
            pl.BlockSpec((tm, LANE), lambda i: (i, 0)),
        ],
        out_shape=[
            jax.ShapeDtypeStruct((rows, t, PACK_W), jnp.int32),
            jax.ShapeDtypeStruct((n_tiles, 8, tm), jnp.int32),
            jax.ShapeDtypeStruct((n_tiles, 8, tm), jnp.int32),
            jax.ShapeDtypeStruct((n_tiles, LANE, LANE), jnp.int32),
            jax.ShapeDtypeStruct((t, LANE), F32),
        ],
        compiler_params=_params(1),
        name="moe_route",
    )(x1, mod, lw["g_pre_ffn"], lw["w_router_t"], lw["b_router_t"], triu)


def _expert_kernel(te_ref, meta_ref, xs_ref, wg_ref, wu_ref, wd_ref, ys_ref, wg_sc, wu_sc, wd_sc):
    i = pl.program_id(0)
    live = i < meta_ref[0]
    new_expert = (i == 0) | (te_ref[i] != te_ref[jnp.maximum(i - 1, 0)])

    @pl.when(live & new_expert)
    def _():
        wg_sc[...] = wg_ref[0, 0].astype(BF16)
        wu_sc[...] = wu_ref[0, 0].astype(BF16)
        wd_sc[...] = wd_ref[0, 0].astype(BF16)

    @pl.when(live)
    def _():
        x = _unpack_rows([xs_ref[j] for j in range(xs_ref.shape[0])]).astype(BF16)
        gate = _dot(x, wg_sc[...])
        act = (gate * _sigmoid(gate) * _dot(x, wu_sc[...])).astype(BF16)
        for j, r in enumerate(_pack_rows(_dot(act, wd_sc[...]))):
            ys_ref[j] = r


def _experts(xs, tile_expert, meta, w_gate_e, w_up_e, w_down_e, layer, *, tm):
    rows, p, _ = xs.shape
    n_experts, d, f = w_gate_e.shape[1:]

    def slot(i, te, meta):
        return (0, jnp.minimum(i, meta[0] - 1), 0)

    grid_spec = pltpu.PrefetchScalarGridSpec(
        num_scalar_prefetch=2,
        grid=(p // tm,),
        in_specs=[
            pl.BlockSpec((rows, tm, PACK_W), slot),
            pl.BlockSpec((1, 1, d, f), lambda i, te, meta: (layer, te[i], 0, 0)),
            pl.BlockSpec((1, 1, d, f), lambda i, te, meta: (layer, te[i], 0, 0)),
            pl.BlockSpec((1, 1, f, d), lambda i, te, meta: (layer, te[i], 0, 0)),
        ],
        out_specs=pl.BlockSpec((rows, tm, PACK_W), slot),
        scratch_shapes=[pltpu.VMEM((d, f), BF16), pltpu.VMEM((d, f), BF16), pltpu.VMEM((f, d), BF16)],
    )
    return pl.pallas_call(
        _expert_kernel,
        grid_spec=grid_spec,
        out_shape=jax.ShapeDtypeStruct(xs.shape, jnp.int32),
        compiler_params=_params(1),
        name="moe_experts",
    )(tile_expert, meta, xs, w_gate_e, w_up_e, w_down_e)


def _combine_kernel(x_ref, mod_ref, gpre_ref, gpost_ref, wgs_ref, wus_ref, wds_ref, yk_ref, wgt_ref,
                    *rest):
    out_ref = rest[-1]
    shift = mod_ref[0, 3:4, :]
    scale = mod_ref[0, 4:5, :]
    x = x_ref[...]
    t_hi = (_rms(x, gpre_ref[...]) * (1.0 + scale) + shift).astype(BF16)
    gate = _dot(t_hi, wgs_ref[...])
    act = (gate * _sigmoid(gate) * _dot(t_hi, wus_ref[...])).astype(BF16)
    acc = _dot(act, wds_ref[...])
    for k in range(TOP_K):
        y = _unpack_rows([yk_ref[k, j] for j in range(yk_ref.shape[1])])
        acc = acc + wgt_ref[:, k:k + 1] * y
    g2 = mod_ref[0, 5:6, :]
    out_ref[...] = x + g2 * _rms(acc, gpost_ref[...])


def _combine(x1, mod, seq_len, lw, yk, wgt, *, tm, tile_off, prev):
    t, d = x1.shape
    nb = mod.shape[0]
    tiles_per_seq = max(seq_len // tm, 1)
    rows = yk.shape[1]
    n_tiles = yk.shape[2] // tm
    glob = lambda i: (i + tile_off, 0)
    in_specs = [
        pl.BlockSpec((tm, d), glob),
        pl.BlockSpec((1,) + mod.shape[1:], lambda i: (((i + tile_off) // tiles_per_seq) % nb, 0, 0)),
        _resident((1, d)),
        _resident((1, d)),
        _resident(lw["w_gate_s"].shape),
        _resident(lw["w_up_s"].shape),
        _resident(lw["w_down_s"].shape),
        pl.BlockSpec((TOP_K, rows, tm, PACK_W), lambda i: (0, 0, i, 0)),
        pl.BlockSpec((tm, LANE), lambda i: (i, 0)),
    ]
    args = [x1, mod, lw["g_pre_ffn"], lw["g_post_ffn"], lw["w_gate_s"], lw["w_up_s"], lw["w_down_s"], yk, wgt]
    aliases = {}
    if prev is not None:
        in_specs.append(pl.BlockSpec(memory_space=pl.ANY))
        aliases = {len(args): 0}
        args.append(prev)
    return pl.pallas_call(
        _combine_kernel,
        grid=(n_tiles,),
        in_specs=in_specs,
        out_specs=pl.BlockSpec((tm, d), glob),
        out_shape=jax.ShapeDtypeStruct((t, d), F32),
        input_output_aliases=aliases,
        compiler_params=_params(1),
        name="moe_combine",
    )(*args)


def _sc_mesh():
    return plsc.VectorSubcoreMesh(core_axis_name="core", subcore_axis_name="subcore")


def _sc_scatter_rows(src, idx, n_out):
    n_lists, n = idx.shape
    width = src.shape[1]

    @pl.kernel(out_type=jax.ShapeDtypeStruct((n_out, width), src.dtype), mesh=_sc_mesh(), scratch_types=[])
    def scatter(x_hbm, *refs):
        i_hbms, o_hbm = refs[:n_lists], refs[n_lists]

        def body(x_vmem, *i_vmems):
            for i_vmem in i_vmems:
                pltpu.sync_copy(x_vmem, o_hbm.at[i_vmem.at[0]])

        pltpu.emit_pipeline(
            body,
            grid=(n // SC_WINDOW,),
            in_specs=[pl.BlockSpec((SC_WINDOW, width), lambda i: (i, 0))]
            + [pl.BlockSpec((1, SC_WINDOW), lambda i: (0, i))] * n_lists,
            out_specs=[],
            core_axis_name=("core", "subcore"),
            dimension_semantics=(pltpu.PARALLEL,),
        )(x_hbm, *i_hbms)

    return scatter(src, *[idx[r].reshape(1, n) for r in range(n_lists)])


def _sc_gather_rows(table, idx):
    n = idx.shape[0]
    width = table.shape[1]

    @pl.kernel(out_type=jax.ShapeDtypeStruct((n, width), table.dtype), mesh=_sc_mesh(), scratch_types=[])
    def gather(x_hbm, i_hbm, o_hbm):
        def body(i_vmem, o_vmem):
            pltpu.sync_copy(x_hbm.at[i_vmem.at[0]], o_vmem)

        pltpu.emit_pipeline(
            body,
            grid=(n // SC_WINDOW,),
            in_specs=[pl.BlockSpec((1, SC_WINDOW), lambda i: (0, i))],
            out_specs=[pl.BlockSpec((SC_WINDOW, width), lambda i: (i, 0))],
            core_axis_name=("core", "subcore"),
            dimension_semantics=(pltpu.PARALLEL,),
        )(i_hbm, o_hbm)

    return gather(table, idx.reshape(1, n))


def _moe_sparse(x1, mod, seq_len, lw, w_gate_e, w_up_e, w_down_e, layer, *, tm, tm_e, n_parts):
    n_experts = lw["b_router"].shape[1]
    assert n_experts * TOP_K == LANE
    n_tok_tiles = x1.shape[0] // tm // n_parts
    t = n_tok_tiles * tm
    n_row_tiles = (t * TOP_K) // tm_e + n_experts
    p = n_row_tiles * tm_e
    experts = jnp.arange(n_experts, dtype=jnp.int32)

    staged = []
    for part in range(n_parts):
        tp, sel, rank, cnt, wgt = _route(x1, mod, seq_len, lw, tm=tm, tile_off=part * n_tok_tiles,
                                         n_tiles=n_tok_tiles)
        rows = tp.shape[0]
        cnt = cnt[:, :, 0].reshape(n_tok_tiles, TOP_K, n_experts).sum(axis=1)
        padded = (cnt.sum(axis=0) + tm_e - 1) // tm_e * tm_e
        group_end = jnp.cumsum(padded)
        base = (group_end - padded)[None, :] + jnp.cumsum(cnt, axis=0) - cnt
        chosen = sel[:, :TOP_K, :, None] == experts
        pos = jnp.sum(jnp.where(chosen, base[:, None, None, :], 0), axis=-1) + rank[:, :TOP_K, :]
        pos = pos.transpose(1, 0, 2).reshape(TOP_K, t)

        n_used = group_end[-1] // tm_e
        tile_start = jnp.arange(n_row_tiles, dtype=jnp.int32) * tm_e
        tile_expert = jnp.sum(tile_start[:, None] >= group_end[None, :], axis=1).astype(jnp.int32)
        tile_expert = jnp.minimum(tile_expert, n_experts - 1)
        tile_expert = jnp.where(tile_start < group_end[-1], tile_expert, tile_expert[n_used - 1])
        meta = jnp.stack([n_used, n_used]).astype(jnp.int32)

        idx = pos[:, None, :] + (jnp.arange(rows, dtype=jnp.int32) * p)[None, :, None]
        xs = _sc_scatter_rows(tp.reshape(rows * t, PACK_W), idx.reshape(TOP_K, rows * t), rows * p)
        staged.append((xs.reshape(rows, p, PACK_W), tile_expert, meta, idx.reshape(-1), wgt))

    gathered = []
    for xs, tile_expert, meta, idx, wgt in staged:
        ys = _experts(xs, tile_expert, meta, w_gate_e, w_up_e, w_down_e, layer, tm=tm_e)
        yk = _sc_gather_rows(ys.reshape(-1, PACK_W), idx)
        gathered.append((yk.reshape(TOP_K, xs.shape[0], t, PACK_W), wgt))

    out = None
    for part, (yk, wgt) in enumerate(gathered):
        out = _combine(x1, mod, seq_len, lw, yk, wgt, tm=tm, tile_off=part * n_tok_tiles, prev=out)
    return out


def _rope_tables(n):
    rows = n // GRID_W
    r, col = jnp.meshgrid(jnp.arange(rows), jnp.arange(GRID_W), indexing="ij")
    r = r.reshape(-1).astype(F32)
    col = col.reshape(-1).astype(F32)
    pairs = QK_ROPE // 4
    inv = ROPE_BASE ** (-jnp.arange(pairs, dtype=F32) / pairs)
    ang = jnp.concatenate([r[:, None] * inv, col[:, None] * inv], axis=-1)
    cos, sin = jnp.cos(ang), jnp.sin(ang)
    pad = HEAD_PAD - QK_NOPE - QK_ROPE
    cos_t = jnp.concatenate([jnp.ones((n, QK_NOPE), F32), cos, cos, jnp.zeros((n, pad), F32)], axis=1)
    sin_t = jnp.concatenate([jnp.zeros((n, QK_NOPE), F32), sin, sin, jnp.zeros((n, pad), F32)], axis=1)
    return cos_t, sin_t


def _identity_tables(n):
    pad = HEAD_PAD - QK_NOPE - QK_ROPE
    cos_t = jnp.concatenate([jnp.ones((n, QK_NOPE + QK_ROPE), F32), jnp.zeros((n, pad), F32)], axis=1)
    return cos_t, jnp.zeros((n, HEAD_PAD), F32)


def _position_dft(n):
    nb = 64 if n % 64 == 0 else 1
    na = n // nb
    m = jnp.arange(n, dtype=jnp.int32)[None, :]
    ang_a = ((jnp.arange(na, dtype=jnp.int32)[:, None] * m) % na).astype(F32) * (2.0 * np.pi / na)
    ang_b = ((jnp.arange(nb, dtype=jnp.int32)[:, None] * m) % n).astype(F32) * (2.0 * np.pi / n)
    ca, sa = jnp.cos(ang_a)[:, None, :], jnp.sin(ang_a)[:, None, :]
    cb, sb = jnp.cos(ang_b)[None, :, :], jnp.sin(ang_b)[None, :, :]
    norm = 1.0 / np.sqrt(n)
    cos = ((ca * cb - sa * sb) * norm).reshape(n, n)
    sin = ((sa * cb + ca * sb) * (-norm)).reshape(n, n)
    return jnp.concatenate([cos, sin], axis=1).astype(BF16)


def _channel_dft(width):
    gc = width // FOURIER_GROUPS
    idx = (jnp.arange(gc, dtype=jnp.int32)[:, None] * jnp.arange(gc, dtype=jnp.int32)[None, :]) % gc
    ang = idx.astype(F32) * (2.0 * np.pi / gc)
    eye = jnp.eye(FOURIER_GROUPS, dtype=F32)
    norm = 1.0 / np.sqrt(gc)
    return jnp.concatenate([jnp.kron(eye, jnp.cos(ang) * norm), jnp.kron(eye, jnp.sin(ang) * norm)],
                           axis=1).astype(BF16)


def _w1_kernel(w_ref, o_ref, *, kv_rank):
    cols = w_ref.shape[2]
    kv_end = kv_rank + QK_ROPE
    half = QK_ROPE // 2
    tail = HEAD_PAD - QK_NOPE - QK_ROPE
    o_kpe = kv_rank
    o_rot = o_kpe + HEAD_PAD
    o_rest = o_rot + HEAD_PAD
    dt = o_ref.dtype
    o_ref[0, 0:kv_rank, :] = w_ref[0, 0:kv_rank, :].astype(dt)
    for base in (o_kpe, o_rot):
        o_ref[0, base:base + QK_NOPE, :] = jnp.zeros((QK_NOPE, cols), dt)
        o_ref[0, base + QK_NOPE + QK_ROPE:base + HEAD_PAD, :] = jnp.zeros((tail, cols), dt)
    o_ref[0, o_kpe + QK_NOPE:o_kpe + QK_NOPE + QK_ROPE, :] = w_ref[0, kv_rank:kv_end, :].astype(dt)
    o_ref[0, o_rot + QK_NOPE:o_rot + QK_NOPE + half, :] = (-w_ref[0, kv_rank + half:kv_end, :]).astype(dt)
    o_ref[0, o_rot + QK_NOPE + half:o_rot + QK_NOPE + QK_ROPE, :] = w_ref[0, kv_rank:kv_rank + half, :].astype(dt)
    o_ref[0, o_rest:, :] = w_ref[0, kv_end:, :].astype(dt)


def _prep_w1(w_in_t, kv_rank):
    n_layers, width, d = w_in_t.shape
    out_w = width - QK_ROPE + 2 * HEAD_PAD
    tc = _tile(d, 256)
    return pl.pallas_call(
        functools.partial(_w1_kernel, kv_rank=kv_rank),
        grid=(n_layers, d // tc),
        in_specs=[pl.BlockSpec((1, width, tc), lambda l, i: (l, 0, i))],
        out_specs=pl.BlockSpec((1, out_w, tc), lambda l, i: (l, 0, i)),
        out_shape=jax.ShapeDtypeStruct((n_layers, out_w, d), BF16),
        compiler_params=_params(2),
        name="prep_w1",
    )(w_in_t)


def _rot_cols(w):
    half = w.shape[-1] // 2
    return jnp.concatenate([-w[..., half:], w[..., :half]], axis=-1)


def _layer_weights(l, g_pre_mix, g_post_mix, g_pre_ffn, g_post_ffn, w1, b_gate, g_q, w_uq, g_kv,
                   w_ukv, w_mla_out, conv_w, w_conv_out, w_four_out, w_out, w_router, b_router,
                   w_gate_s, w_up_s, w_down_s, dc):
    kv_rank = g_kv.shape[1]
    q_rank = g_q.shape[1]
    qk_dim = QK_NOPE + QK_ROPE
    pad = HEAD_PAD - qk_dim
    uq = w_uq[l].reshape(q_rank, N_HEADS, qk_dim) * (qk_dim ** -0.5 * np.log2(np.e))
    zq = jnp.zeros((q_rank, N_HEADS, pad), F32)
    wuq = jnp.concatenate([uq, zq], axis=-1).reshape(q_rank, N_HEADS * HEAD_PAD).astype(BF16)
    wuq_rot = jnp.concatenate([jnp.zeros((q_rank, N_HEADS, QK_NOPE), F32), _rot_cols(uq[..., QK_NOPE:]), zq],
                              axis=-1).reshape(q_rank, N_HEADS * HEAD_PAD).astype(BF16)
    ukv = w_ukv[l].reshape(kv_rank, N_HEADS, QK_NOPE + V_DIM)
    wuk = jnp.concatenate([ukv[..., :QK_NOPE], jnp.zeros((kv_rank, N_HEADS, HEAD_PAD - QK_NOPE), F32)],
                          axis=-1).reshape(kv_rank, N_HEADS * HEAD_PAD).astype(BF16)
    wuv = jnp.concatenate([ukv[..., QK_NOPE:], jnp.zeros((kv_rank, N_HEADS, HEAD_PAD - V_DIM), F32)],
                          axis=-1).reshape(kv_rank, N_HEADS * HEAD_PAD).astype(BF16)
    v_one = jnp.tile((jnp.arange(HEAD_PAD) == V_DIM).astype(F32), N_HEADS)[None]

    wr = w_router[l]
    wr_hi = wr.astype(BF16)
    wr_lo = (wr - wr_hi.astype(F32)).astype(BF16)
    return {
        "g_pre_mix": g_pre_mix[l][None], "g_post_mix": g_post_mix[l][None],
        "g_pre_ffn": g_pre_ffn[l][None], "g_post_ffn": g_post_ffn[l][None],
        "w1": w1, "layer": l, "b_gate": b_gate[l][None], "g_q": g_q[l][None], "g_kv": g_kv[l][None],
        "wuq": wuq, "wuq_rot": wuq_rot, "wuk": wuk, "wuv": wuv, "v_one": v_one, "dc": dc,
        "conv_w": conv_w[l],
        "w_mla_out": w_mla_out[l].astype(BF16), "w_conv_out": w_conv_out[l].astype(BF16),
        "w_four_out": w_four_out[l].astype(BF16), "w_out": w_out[l].astype(BF16),
        "w_router": jnp.concatenate([wr_hi, wr_lo], axis=1), "b_router": b_router[l][None],
        "w_router_t": jnp.concatenate([wr_hi.T, wr_lo.T], axis=0), "b_router_t": b_router[l][:, None],
        "w_gate_s": w_gate_s[l].astype(BF16), "w_up_s": w_up_s[l].astype(BF16),
        "w_down_s": w_down_s[l].astype(BF16),
    }


def _tile(n, pref):
    return pref if n % pref == 0 else n


def kernel(x, c, ctx, c_ctx, w_ada, b_ada, g_pre_mix, g_post_mix, g_pre_ffn, g_post_ffn, w_in, b_gate,
           g_q, w_uq, g_kv, w_ukv, w_mla_out, conv_w, w_conv_out, w_four_out, w_out, w_router, b_router,
           w_gate_e, w_up_e, w_down_e, w_gate_s, w_up_s, w_down_s):
    batch, seq, d = x.shape
    n_ctx = ctx.shape[1]
    n_layers = w_in.shape[0]
    xs = x.reshape(batch * seq, d)
    cs = ctx.reshape(batch * n_ctx, d)

    mod_rows = 16
    c_all = jnp.concatenate([c, c_ctx[None], jnp.zeros((mod_rows - batch - 1, d), F32)], axis=0)
    ada = _ada(c_all, w_ada, b_ada)

    tab_x = _rope_tables(seq)
    tm_c = _tile(n_ctx, 256)
    tab_c = _identity_tables(tm_c)
    cs_x = _position_dft(seq)
    cs_c = _position_dft(n_ctx)
    dc = _channel_dft(w_four_out.shape[1])
    w1 = _prep_w1(jnp.swapaxes(w_in, 1, 2), g_kv.shape[1])

    tm_x = _tile(seq, 512)
    tm_moe_x = _tile(seq, 1024)
    tm_moe_c = _tile(batch * n_ctx, 1024)

    for l in range(n_layers):
        last = l == n_layers - 1
        lw = _layer_weights(l, g_pre_mix, g_post_mix, g_pre_ffn, g_post_ffn, w1, b_gate, g_q, w_uq,
                            g_kv, w_ukv, w_mla_out, conv_w, w_conv_out, w_four_out, w_out, w_router,
                            b_router, w_gate_s, w_up_s, w_down_s, dc)
        mods = ada[l].reshape(mod_rows, 6, d)
        mod_x = mods[:batch]
        mod_c = mods[batch:batch + 1]

        pc = _inproj(cs, mod_c, n_ctx, lw, tab_c, kv_only=last, tm=tm_c)
        px = _inproj(xs, mod_x, seq, lw, tab_x, kv_only=False, tm=tm_x)
        o_x = _attention(px["q"], [(pc["k"], pc["v"], n_ctx), (px["k"], px["v"], seq)], batch, seq, tq=tm_x)
        f_x = _fourier(px["ab"], cs_x, batch, seq, tn=tm_x)
        x1 = _merge(xs, mod_x, seq, px, o_x, f_x, lw, tm=tm_x)
        xs = _moe_sparse(x1, mod_x, seq, lw, w_gate_e, w_up_e, w_down_e, l, tm=tm_moe_x, tm_e=1024,
                         n_parts=1)
        if not last:
            o_c = _attention(pc["q"], [(pc["k"], pc["v"], n_ctx)], batch, n_ctx, tq=tm_c)
            f_c = _fourier(pc["ab"], cs_c, batch, n_ctx, tn=tm_c)
            c1 = _merge(cs, mod_c, n_ctx, pc, o_c, f_c, lw, tm=tm_c)
            cs = _moe_sparse(c1, mod_c, batch * n_ctx, lw, w_gate_e, w_up_e, w_down_e, l, tm=tm_moe_c,
                             tm_e=256, n_parts=1)
    return xs.reshape(batch, seq, d)
```

```python
import functools

import numpy as np
import jax
import jax.numpy as jnp
from jax import lax
from jax.experimental import pallas as pl
from jax.experimental.pallas import tpu as pltpu
from jax.experimental.pallas import tpu_sc as plsc

N_HEADS = 8
QK_NOPE = 64
QK_ROPE = 32
V_DIM = 64
GRID_W = 64
ROPE_BASE = 10000.0
FOURIER_GROUPS = 4
TOP_K = 4
ROUTED_SCALE = 2.5
N_BRANCHES = 3
EPS = 1e-6

LANE = 128
HEAD_PAD = LANE
VMEM_LIMIT = 56 * 1024 * 1024
PACK_W = 256
SC_WINDOW = 128

F32 = jnp.float32
BF16 = jnp.bfloat16


def _rms(x, g):
    return x * lax.rsqrt(jnp.mean(x * x, axis=-1, keepdims=True) + EPS) * g


def _sigmoid(x):
    return 1.0 / (1.0 + jnp.exp(-x))


def _dot(a, b):
    return jnp.dot(a, b, preferred_element_type=F32)


def _dot_t(a, b_t):
    return lax.dot_general(a, b_t, (((1,), (1,)), ((), ())), preferred_element_type=F32)


def _resident(shape):
    nd = len(shape)
    return pl.BlockSpec(shape, lambda *_: (0,) * nd, pipeline_mode=pl.Buffered(1))


def _params(n_grid):
    return pltpu.CompilerParams(dimension_semantics=("arbitrary",) * n_grid,
                                vmem_limit_bytes=VMEM_LIMIT)


def _ada_kernel(c_ref, w_ref, b_ref, o_ref):
    c = c_ref[...]
    a = (c * _sigmoid(c)).astype(BF16)
    o_ref[0] = _dot(a, w_ref[0].astype(BF16)) + b_ref[0]


def _ada(c_all, w_ada, b_ada):
    n_layers, d, n_out = w_ada.shape
    rows = c_all.shape[0]
    tn = 1536
    return pl.pallas_call(
        _ada_kernel,
        grid=(n_layers, n_out // tn),
        in_specs=[
            pl.BlockSpec((rows, d), lambda l, j: (0, 0)),
            pl.BlockSpec((1, d, tn), lambda l, j: (l, 0, j)),
            pl.BlockSpec((1, 1, tn), lambda l, j: (l, 0, j)),
        ],
        out_specs=pl.BlockSpec((1, rows, tn), lambda l, j: (l, 0, j)),
        out_shape=jax.ShapeDtypeStruct((n_layers, rows, n_out), F32),
        compiler_params=_params(2),
        name="ada",
    )(c_all, w_ada, b_ada.reshape(n_layers, 1, n_out))


def _inproj_kernel(*refs, kv_only, kv_rank, q_rank, conv_w, four_w, d_model):
    if kv_only:
        (x_ref, mod_ref, gpre_ref, w1_ref, gkv_ref, wuk_ref, wuv_ref, vone_ref, cos_ref, sin_ref,
         k_ref, v_ref) = refs
    else:
        (x_ref, mod_ref, gpre_ref, w1_ref, gkv_ref, wuk_ref, wuv_ref, vone_ref, cos_ref, sin_ref,
         bg_ref, gq_ref, wuq_ref, wuqr_ref, dc_ref,
         k_ref, v_ref, q_ref, cb_ref, cc_ref, cu_ref, ab_ref, gate_ref) = refs

    x = x_ref[...]
    shift = mod_ref[0, 0:1, :]
    scale = mod_ref[0, 1:2, :]
    h = (_rms(x, gpre_ref[...]) * (1.0 + scale) + shift).astype(BF16)
    cos = cos_ref[...]
    sin = sin_ref[...]

    o_kpe = kv_rank
    o_rot = o_kpe + HEAD_PAD
    o_q = o_rot + HEAD_PAD
    p = _dot_t(h, w1_ref[0, 0:o_q, :])
    ckv = _rms(p[:, 0:kv_rank], gkv_ref[...]).astype(BF16)
    kpe = p[:, o_kpe:o_rot] * cos + p[:, o_rot:o_q] * sin
    k = _dot(ckv, wuk_ref[...]) + jnp.concatenate([kpe] * N_HEADS, axis=1)
    k_ref[...] = k.astype(k_ref.dtype)
    v_ref[...] = (_dot(ckv, wuv_ref[...]) + vone_ref[...]).astype(v_ref.dtype)
    if kv_only:
        return

    o_cb = o_q + q_rank
    cq = _rms(_dot_t(h, w1_ref[0, o_q:o_cb, :]), gq_ref[...]).astype(BF16)
    cos_h = jnp.concatenate([cos] * N_HEADS, axis=1)
    sin_h = jnp.concatenate([sin] * N_HEADS, axis=1)
    q = _dot(cq, wuq_ref[...]) * cos_h + _dot(cq, wuqr_ref[...]) * sin_h
    q_ref[...] = q.astype(q_ref.dtype)

    o_cc = o_cb + conv_w
    o_cu = o_cc + conv_w
    o_four = o_cu + conv_w
    cb_ref[...] = _dot_t(h, w1_ref[0, o_cb:o_cc, :]).astype(cb_ref.dtype)
    cc_ref[...] = _dot_t(h, w1_ref[0, o_cc:o_cu, :]).astype(cc_ref.dtype)
    cu_ref[...] = _dot_t(h, w1_ref[0, o_cu:o_four, :]).astype(cu_ref.dtype)

    o_gate = o_four + four_w
    uf = _dot_t(h, w1_ref[0, o_four:o_gate, :]).astype(BF16)
    ab_ref[...] = _dot(uf, dc_ref[...]).astype(ab_ref.dtype)

    for j in range(N_BRANCHES):
        lo = o_gate + j * d_model
        z = _dot_t(h, w1_ref[0, lo:lo + d_model, :]) + bg_ref[:, j * d_model:(j + 1) * d_model]
        gate_ref[:, j * d_model:(j + 1) * d_model] = _sigmoid(z).astype(gate_ref.dtype)


def _inproj(xs, mod, seq_len, lw, tables, *, kv_only, tm):
    t, d = xs.shape
    nb = mod.shape[0]
    tiles_per_seq = seq_len // tm
    cos_t, sin_t = tables
    table_tiles = cos_t.shape[0] // tm
    kv_rank = lw["g_kv"].shape[1]
    q_rank = lw["g_q"].shape[1]
    conv_w = lw["conv_w"].shape[1]
    four_w = lw["dc"].shape[0]
    n_k = N_HEADS * HEAD_PAD

    def row(i):
        return (i, 0)

    def mod_map(i):
        return ((i // tiles_per_seq) % nb, 0, 0)

    def tab_map(i):
        return (i % table_tiles, 0)

    w1, layer = lw["w1"], lw["layer"]
    w1_rows = kv_rank + 2 * HEAD_PAD if kv_only else w1.shape[1]
    in_specs = [
        pl.BlockSpec((tm, d), row),
        pl.BlockSpec((1,) + mod.shape[1:], mod_map),
        _resident((1, d)),
        pl.BlockSpec((1, w1_rows, d), lambda i: (layer, 0, 0), pipeline_mode=pl.Buffered(1)),
        _resident((1, kv_rank)),
        _resident(lw["wuk"].shape),
        _resident(lw["wuv"].shape),
        _resident(lw["v_one"].shape),
        pl.BlockSpec((tm, HEAD_PAD), tab_map),
        pl.BlockSpec((tm, HEAD_PAD), tab_map),
    ]
    args = [xs, mod, lw["g_pre_mix"], w1, lw["g_kv"], lw["wuk"], lw["wuv"], lw["v_one"], cos_t, sin_t]
    out_shape = [jax.ShapeDtypeStruct((t, n_k), BF16), jax.ShapeDtypeStruct((t, n_k), BF16)]
    out_specs = [pl.BlockSpec((tm, n_k), row), pl.BlockSpec((tm, n_k), row)]
    if not kv_only:
        in_specs += [
            _resident(lw["b_gate"].shape),
            _resident((1, q_rank)),
            _resident(lw["wuq"].shape),
            _resident(lw["wuq_rot"].shape),
            _resident(lw["dc"].shape),
        ]
        args += [lw["b_gate"], lw["g_q"], lw["wuq"], lw["wuq_rot"], lw["dc"]]
        widths = [n_k, conv_w, conv_w, conv_w, 2 * four_w, N_BRANCHES * d]
        out_shape += [jax.ShapeDtypeStruct((t, w), BF16) for w in widths]
        out_specs += [pl.BlockSpec((tm, w), row) for w in widths]
    outs = pl.pallas_call(
        functools.partial(_inproj_kernel, kv_only=kv_only, kv_rank=kv_rank, q_rank=q_rank,
                          conv_w=conv_w, four_w=four_w, d_model=d),
        grid=(t // tm,),
        in_specs=in_specs,
        out_specs=out_specs,
        out_shape=out_shape,
        compiler_params=_params(1),
        name="inproj_kv" if kv_only else "inproj",
    )(*args)
    names = ["k", "v", "q", "cb", "cc", "cu", "ab", "gate"]
    return dict(zip(names, outs))


def _attn_kernel(*refs, n_seg):
    q_ref = refs[0]
    o_ref = refs[-1]
    nt = (((1,), (1,)), ((), ()))
    outs = []
    for hh in range(N_HEADS):
        head = slice(hh * HEAD_PAD, (hh + 1) * HEAD_PAD)
        qh = q_ref[:, head]
        s = [lax.dot_general(qh, refs[1 + 2 * i][:, head], nt, preferred_element_type=F32)
             for i in range(n_seg)]
        m = functools.reduce(jnp.maximum, [jnp.max(si, axis=-1, keepdims=True) for si in s])
        acc = functools.reduce(jnp.add, [
            _dot(jnp.exp2((s[i] - m).astype(BF16)), refs[2 + 2 * i][:, head]) for i in range(n_seg)])
        outs.append(acc[:, 0:V_DIM] / acc[:, V_DIM:V_DIM + 1])
    o_ref[...] = jnp.concatenate(outs, axis=1).astype(o_ref.dtype)


def _attention(q, segs, batch, seq_q, *, tq):
    t = q.shape[0]
    qt = seq_q // tq
    n_k = N_HEADS * HEAD_PAD
    in_specs = [pl.BlockSpec((tq, n_k), lambda b, j: (b * qt + j, 0))]
    args = [q]
    for k, v, m in segs:
        in_specs.append(pl.BlockSpec((m, n_k), lambda b, j: (b, 0)))
        in_specs.append(pl.BlockSpec((m, n_k), lambda b, j: (b, 0)))
        args += [k, v]
    return pl.pallas_call(
        functools.partial(_attn_kernel, n_seg=len(segs)),
        grid=(batch, qt),
        in_specs=in_specs,
        out_specs=pl.BlockSpec((tq, N_HEADS * V_DIM), lambda b, j: (b * qt + j, 0)),
        out_shape=jax.ShapeDtypeStruct((t, N_HEADS * V_DIM), BF16),
        compiler_params=_params(2),
        name="attention",
    )(*args)


def _four_kernel(cs_ref, ab_ref, o_ref, *, n, fw):
    o = _dot(cs_ref[:, 0:n], ab_ref[:, 0:fw]) + _dot(cs_ref[:, n:2 * n], ab_ref[:, fw:2 * fw])
    o_ref[...] = o.astype(o_ref.dtype)


def _fourier(ab, cs, batch, seq_len, *, tn):
    t, fw2 = ab.shape
    fw = fw2 // 2
    nt = seq_len // tn
    return pl.pallas_call(
        functools.partial(_four_kernel, n=seq_len, fw=fw),
        grid=(batch, nt),
        in_specs=[
            pl.BlockSpec((tn, 2 * seq_len), lambda b, j: (j, 0)),
            pl.BlockSpec((seq_len, fw2), lambda b, j: (b, 0)),
        ],
        out_specs=pl.BlockSpec((tn, fw), lambda b, j: (b * nt + j, 0)),
        out_shape=jax.ShapeDtypeStruct((t, fw), BF16),
        compiler_params=_params(2),
        name="fourier",
    )(cs, ab)


def _merge_kernel(x_ref, mod_ref, o_ref, cb_ref, cc_ref, cu_ref, ccp_ref, cup_ref, ccn_ref,
                  cun_ref, f_ref, gate_ref, convw_ref, wmo_ref, wco_ref, wfo_ref, wout_ref,
                  gpost_ref, out_ref, pad_ref, *, tiles_per_seq, tm, d_model):
    i = pl.program_id(0)
    pos = i % tiles_per_seq
    has_prev = (pos > 0).astype(F32)
    has_next = (pos < tiles_per_seq - 1).astype(F32)
    pad_ref[0:8, :] = ccp_ref[...].astype(F32) * cup_ref[...].astype(F32) * has_prev
    pad_ref[8:8 + tm, :] = cc_ref[...].astype(F32) * cu_ref[...].astype(F32)
    pad_ref[8 + tm:16 + tm, :] = ccn_ref[...].astype(F32) * cun_ref[...].astype(F32) * has_next
    conv = (pad_ref[7:7 + tm, :] * convw_ref[0:1, :] + pad_ref[8:8 + tm, :] * convw_ref[1:2, :]
            + pad_ref[9:9 + tm, :] * convw_ref[2:3, :])
    y_conv = _dot((cb_ref[...].astype(F32) * conv).astype(BF16), wco_ref[...])
    y_attn = _dot(o_ref[...], wmo_ref[...])
    y_four = _dot(f_ref[...], wfo_ref[...])
    d = d_model
    merged = (gate_ref[:, 0:d].astype(F32) * y_attn + gate_ref[:, d:2 * d].astype(F32) * y_conv
              + gate_ref[:, 2 * d:3 * d].astype(F32) * y_four)
    y = _dot(merged.astype(BF16), wout_ref[...])
    g1 = mod_ref[0, 2:3, :]
    out_ref[...] = x_ref[...] + g1 * _rms(y, gpost_ref[...])


def _merge(xs, mod, seq_len, pr, o, four, lw, *, tm):
    t, d = xs.shape
    nb = mod.shape[0]
    tiles_per_seq = seq_len // tm
    cw = lw["conv_w"].shape[1]
    fw = four.shape[1]
    hb = tm // 8
    last_hb = t // 8 - 1

    def row(i):
        return (i, 0)

    def prev_map(i):
        return (jnp.maximum(i * hb - 1, 0), 0)

    def next_map(i):
        return (jnp.minimum((i + 1) * hb, last_hb), 0)

    in_specs = [
        pl.BlockSpec((tm, d), row),
        pl.BlockSpec((1,) + mod.shape[1:], lambda i: ((i // tiles_per_seq) % nb, 0, 0)),
        pl.BlockSpec((tm, o.shape[1]), row),
        pl.BlockSpec((tm, cw), row),
        pl.BlockSpec((tm, cw), row),
        pl.BlockSpec((tm, cw), row),
        pl.BlockSpec((8, cw), prev_map),
        pl.BlockSpec((8, cw), prev_map),
        pl.BlockSpec((8, cw), next_map),
        pl.BlockSpec((8, cw), next_map),
        pl.BlockSpec((tm, fw), row),
        pl.BlockSpec((tm, N_BRANCHES * d), row),
        _resident(lw["conv_w"].shape),
        _resident(lw["w_mla_out"].shape),
        _resident(lw["w_conv_out"].shape),
        _resident(lw["w_four_out"].shape),
        _resident(lw["w_out"].shape),
        _resident((1, d)),
    ]
    return pl.pallas_call(
        functools.partial(_merge_kernel, tiles_per_seq=tiles_per_seq, tm=tm, d_model=d),
        grid=(t // tm,),
        in_specs=in_specs,
        out_specs=pl.BlockSpec((tm, d), row),
        out_shape=jax.ShapeDtypeStruct((t, d), F32),
        scratch_shapes=[pltpu.VMEM((tm + 16, cw), F32)],
        compiler_params=_params(1),
        name="merge",
    )(xs, mod, o, pr["cb"], pr["cc"], pr["cu"], pr["cc"], pr["cu"], pr["cc"], pr["cu"], four,
      pr["gate"], lw["conv_w"], lw["w_mla_out"], lw["w_conv_out"], lw["w_four_out"], lw["w_out"],
      lw["g_post_mix"])


def _moe_kernel(x_ref, mod_ref, gpre_ref, gpost_ref, wr_ref, br_ref, wgs_ref, wus_ref, wds_ref,
                wge_ref, wue_ref, wde_ref, out_ref, t_ref, comb_ref, acc_ref, *, n_experts):
    e = pl.program_id(1)

    @pl.when(e == 0)
    def _():
        shift = mod_ref[0, 3:4, :]
        scale = mod_ref[0, 4:5, :]
        t = _rms(x_ref[...], gpre_ref[...]) * (1.0 + scale) + shift
        t_hi = t.astype(BF16)
        t_lo = (t - t_hi.astype(F32)).astype(BF16)
        t_ref[...] = t_hi
        hh = _dot(t_hi, wr_ref[...])
        logits = hh[:, 0:n_experts] + hh[:, n_experts:2 * n_experts] + _dot(t_lo, wr_ref[:, 0:n_experts])
        scores = _sigmoid(logits)
        work = scores + br_ref[...]
        lane = lax.broadcasted_iota(jnp.int32, scores.shape, 1)
        comb = jnp.zeros_like(scores)
        for _ in range(TOP_K):
            best = jnp.max(work, axis=-1, keepdims=True)
            first = jnp.min(jnp.where(work == best, lane, n_experts), axis=-1, keepdims=True)
            hit = lane == first
            comb = jnp.where(hit, scores, comb)
            work = jnp.where(hit, -jnp.inf, work)
        comb_ref[...] = comb / jnp.sum(comb, axis=-1, keepdims=True) * ROUTED_SCALE
        gate = _dot(t_hi, wgs_ref[...])
        act = (gate * _sigmoid(gate) * _dot(t_hi, wus_ref[...])).astype(BF16)
        acc_ref[...] = _dot(act, wds_ref[...])

    t_hi = t_ref[...]
    gate = _dot(t_hi, wge_ref[0, 0].astype(BF16))
    up = _dot(t_hi, wue_ref[0, 0].astype(BF16))
    lane = lax.broadcasted_iota(jnp.int32, comb_ref.shape, 1)
    w_e = jnp.sum(jnp.where(lane == e, comb_ref[...], 0.0), axis=-1, keepdims=True)
    act = (gate * _sigmoid(gate) * up * w_e).astype(BF16)
    acc_ref[...] += _dot(act, wde_ref[0, 0].astype(BF16))

    @pl.when(e == n_experts - 1)
    def _():
        g2 = mod_ref[0, 5:6, :]
        out_ref[...] = x_ref[...] + g2 * _rms(acc_ref[...], gpost_ref[...])


def _moe(xs, mod, seq_len, lw, w_gate_e, w_up_e, w_down_e, layer, *, tm):
    t, d = xs.shape
    nb = mod.shape[0]
    tiles_per_seq = max(seq_len // tm, 1)
    n_experts, _, f = w_gate_e.shape[1:]
    in_specs = [
        pl.BlockSpec((tm, d), lambda i, e: (i, 0)),
        pl.BlockSpec((1,) + mod.shape[1:], lambda i, e: ((i // tiles_per_seq) % nb, 0, 0)),
        _resident((1, d)),
        _resident((1, d)),
        _resident(lw["w_router"].shape),
        _resident((1, n_experts)),
        _resident(lw["w_gate_s"].shape),
        _resident(lw["w_up_s"].shape),
        _resident(lw["w_down_s"].shape),
        pl.BlockSpec((1, 1, d, f), lambda i, e: (layer, e, 0, 0)),
        pl.BlockSpec((1, 1, d, f), lambda i, e: (layer, e, 0, 0)),
        pl.BlockSpec((1, 1, f, d), lambda i, e: (layer, e, 0, 0)),
    ]
    return pl.pallas_call(
        functools.partial(_moe_kernel, n_experts=n_experts),
        grid=(t // tm, n_experts),
        in_specs=in_specs,
        out_specs=pl.BlockSpec((tm, d), lambda i, e: (i, 0)),
        out_shape=jax.ShapeDtypeStruct((t, d), F32),
        scratch_shapes=[pltpu.VMEM((tm, d), BF16), pltpu.VMEM((tm, n_experts), F32),
                        pltpu.VMEM((tm, d), F32)],
        compiler_params=_params(2),
        name="moe",
    )(xs, mod, lw["g_pre_ffn"], lw["g_post_ffn"], lw["w_router"], lw["b_router"], lw["w_gate_s"],
      lw["w_up_s"], lw["w_down_s"], w_gate_e, w_up_e, w_down_e)


def _pack_rows(v):
    bits = lax.bitcast_convert_type(v.astype(BF16).astype(F32), jnp.uint32)
    rows = []
    for j in range(v.shape[1] // (2 * PACK_W)):
        lo = bits[:, (2 * j) * PACK_W:(2 * j + 1) * PACK_W]
        hi = bits[:, (2 * j + 1) * PACK_W:(2 * j + 2) * PACK_W]
        rows.append(lax.bitcast_convert_type((hi & jnp.uint32(0xFFFF0000)) | (lo >> 16), jnp.int32))
    return rows


def _unpack_rows(rows):
    parts = []
    for r in rows:
        u = lax.bitcast_convert_type(r, jnp.uint32)
        parts.append(lax.bitcast_convert_type(u << 16, F32))
        parts.append(lax.bitcast_convert_type(u & jnp.uint32(0xFFFF0000), F32))
    return jnp.concatenate(parts, axis=1)


def _route_kernel(x_ref, mod_ref, gpre_ref, wrt_ref, brt_ref, triu_ref,
                  tp_ref, sel_ref, rank_ref, cnt_ref, wgt_ref, *, n_experts):
    shift = mod_ref[0, 3:4, :]
    scale = mod_ref[0, 4:5, :]
    t = _rms(x_ref[...], gpre_ref[...]) * (1.0 + scale) + shift
    t_hi = t.astype(BF16)
    t_lo = (t - t_hi.astype(F32)).astype(BF16)
    for j, r in enumerate(_pack_rows(t)):
        tp_ref[j] = r
    tm = t.shape[0]
    hh = _dot_t(wrt_ref[...], t_hi)
    logits = hh[0:n_experts] + hh[n_experts:2 * n_experts] + _dot_t(wrt_ref[0:n_experts, :], t_lo)
    scores = _sigmoid(logits)
    work = scores + brt_ref[...]
    row = lax.broadcasted_iota(jnp.int32, scores.shape, 0)
    wide = lax.broadcasted_iota(jnp.int32, (LANE, tm), 0)
    firsts, picked, hits = [], [], []
    for k in range(TOP_K):
        best = jnp.max(work, axis=0, keepdims=True)
        first = jnp.min(jnp.where(work == best, row, n_experts), axis=0, keepdims=True)
        hit = row == first
        firsts.append(first)
        picked.append(jnp.sum(jnp.where(hit, scores, 0.0), axis=0, keepdims=True))
        hits.append(wide == first + k * n_experts)
        work = jnp.where(hit, -jnp.inf, work)
    total = functools.reduce(jnp.add, picked)
    onehot = functools.reduce(jnp.add, [jnp.where(h, 1.0, 0.0) for h in hits])
    earlier = _dot(onehot.astype(BF16), triu_ref[...])
    col = jnp.broadcast_to(jnp.sum(onehot, axis=1, keepdims=True), (LANE, LANE))
    row_c = lax.broadcasted_iota(jnp.int32, (LANE, LANE), 0)
    before = jnp.zeros((LANE, LANE), F32)
    for s in range(1, TOP_K):
        before = before + jnp.where(row_c >= s * n_experts, pltpu.roll(col, s * n_experts, 0), 0.0)
    ahead = earlier + before[:, 0:1]
    row8 = lax.broadcasted_iota(jnp.int32, (8, tm), 0)
    sel = jnp.zeros((8, tm), jnp.int32)
    rank = jnp.zeros((8, tm), F32)
    wgt_t = jnp.zeros((LANE, tm), F32)
    for k in range(TOP_K):
        sel = jnp.where(row8 == k, firsts[k], sel)
        rank = jnp.where(row8 == k, jnp.sum(jnp.where(hits[k], ahead, 0.0), axis=0, keepdims=True), rank)
        wgt_t = jnp.where(wide == k, picked[k] / total * ROUTED_SCALE, wgt_t)
    sel_ref[0] = sel
    rank_ref[0] = rank.astype(jnp.int32)
    cnt_ref[0] = col.astype(jnp.int32)
    wgt_ref[...] = wgt_t.T


def _route(x1, mod, seq_len, lw, *, tm, tile_off, n_tiles):
    d = x1.shape[1]
    t = n_tiles * tm
    nb = mod.shape[0]
    tiles_per_seq = max(seq_len // tm, 1)
    n_experts = lw["b_router"].shape[1]
    rows = d // (2 * PACK_W)
    triu = jnp.tri(tm, tm, -1, dtype=BF16).T
    tile3 = lambda i: (i, 0, 0)
    return pl.pallas_call(
        functools.partial(_route_kernel, n_experts=n_experts),
        grid=(n_tiles,),
        in_specs=[
            pl.BlockSpec((tm, d), lambda i: (i + tile_off, 0)),
            pl.BlockSpec((1,) + mod.shape[1:], lambda i: (((i + tile_off) // tiles_per_seq) % nb, 0, 0)),
            _resident((1, d)),
            _resident(lw["w_router_t"].shape),
            _resident((n_experts, 1)),
            _resident((tm, tm)),
        ],
        out_specs=[
            pl.BlockSpec((rows, tm, PACK_W), lambda i: (0, i, 0)),
            pl.BlockSpec((1, 8, tm), tile3),
            pl.BlockSpec((1, 8, tm), tile3),
            pl.BlockSpec((1, LANE, LANE), tile3),
            pl.BlockSpec((tm, LANE), lambda i: (i, 0)),
        ],
        out_shape=[
            jax.ShapeDtypeStruct((rows, t, PACK_W), jnp.int32),
            jax.ShapeDtypeStruct((n_tiles, 8, tm), jnp.int32),
            jax.ShapeDtypeStruct((n_tiles, 8, tm), jnp.int32),
            jax.ShapeDtypeStruct((n_tiles, LANE, LANE), jnp.int32),
            jax.ShapeDtypeStruct((t, LANE), F32),
        ],
        compiler_params=_params(1),
        name="moe_route",
    )(x1, mod, lw["g_pre_ffn"], lw["w_router_t"], lw["b_router_t"], triu)


def _expert_kernel(te_ref, meta_ref, xs_ref, wg_ref, wu_ref, wd_ref, ys_ref, wg_sc, wu_sc, wd_sc):
    i = pl.program_id(0)
    live = i < meta_ref[0]
    new_expert = (i == 0) | (te_ref[i] != te_ref[jnp.maximum(i - 1, 0)])

    @pl.when(live & new_expert)
    def _():
        wg_sc[...] = wg_ref[0, 0].astype(BF16)
        wu_sc[...] = wu_ref[0, 0].astype(BF16)
        wd_sc[...] = wd_ref[0, 0].astype(BF16)

    @pl.when(live)
    def _():
        x = _unpack_rows([xs_ref[j] for j in range(xs_ref.shape[0])]).astype(BF16)
        gate = _dot(x, wg_sc[...])
        act = (gate * _sigmoid(gate) * _dot(x, wu_sc[...])).astype(BF16)
        for j, r in enumerate(_pack_rows(_dot(act, wd_sc[...]))):
            ys_ref[j] = r


def _experts(xs, tile_expert, meta, w_gate_e, w_up_e, w_down_e, layer, *, tm):
    rows, p, _ = xs.shape
    n_experts, d, f = w_gate_e.shape[1:]

    def slot(i, te, meta):
        return (0, jnp.minimum(i, meta[0] - 1), 0)

    grid_spec = pltpu.PrefetchScalarGridSpec(
        num_scalar_prefetch=2,
        grid=(p // tm,),
        in_specs=[
            pl.BlockSpec((rows, tm, PACK_W), slot),
            pl.BlockSpec((1, 1, d, f), lambda i, te, meta: (layer, te[i], 0, 0)),
            pl.BlockSpec((1, 1, d, f), lambda i, te, meta: (layer, te[i], 0, 0)),
            pl.BlockSpec((1, 1, f, d), lambda i, te, meta: (layer, te[i], 0, 0)),
        ],
        out_specs=pl.BlockSpec((rows, tm, PACK_W), slot),
        scratch_shapes=[pltpu.VMEM((d, f), BF16), pltpu.VMEM((d, f), BF16), pltpu.VMEM((f, d), BF16)],
    )
    return pl.pallas_call(
        _expert_kernel,
        grid_spec=grid_spec,
        out_shape=jax.ShapeDtypeStruct(xs.shape, jnp.int32),
        compiler_params=_params(1),
        name="moe_experts",
    )(tile_expert, meta, xs, w_gate_e, w_up_e, w_down_e)


def _combine_kernel(x_ref, mod_ref, gpre_ref, gpost_ref, wgs_ref, wus_ref, wds_ref, yk_ref, wgt_ref,
                    *rest):
    out_ref = rest[-1]
    shift = mod_ref[0, 3:4, :]
    scale = mod_ref[0, 4:5, :]
    x = x_ref[...]
    t_hi = (_rms(x, gpre_ref[...]) * (1.0 + scale) + shift).astype(BF16)
    gate = _dot(t_hi, wgs_ref[...])
    act = (gate * _sigmoid(gate) * _dot(t_hi, wus_ref[...])).astype(BF16)
    acc = _dot(act, wds_ref[...])
    for k in range(TOP_K):
        y = _unpack_rows([yk_ref[k, j] for j in range(yk_ref.shape[1])])
        acc = acc + wgt_ref[:, k:k + 1] * y
    g2 = mod_ref[0, 5:6, :]
    out_ref[...] = x + g2 * _rms(acc, gpost_ref[...])


def _combine(x1, mod, seq_len, lw, yk, wgt, *, tm, tile_off, prev):
    t, d = x1.shape
    nb = mod.shape[0]
    tiles_per_seq = max(seq_len // tm, 1)
    rows = yk.shape[1]
    n_tiles = yk.shape[2] // tm
    glob = lambda i: (i + tile_off, 0)
    in_specs = [
        pl.BlockSpec((tm, d), glob),
        pl.BlockSpec((1,) + mod.shape[1:], lambda i: (((i + tile_off) // tiles_per_seq) % nb, 0, 0)),
        _resident((1, d)),
        _resident((1, d)),
        _resident(lw["w_gate_s"].shape),
        _resident(lw["w_up_s"].shape),
        _resident(lw["w_down_s"].shape),
        pl.BlockSpec((TOP_K, rows, tm, PACK_W), lambda i: (0, 0, i, 0)),
        pl.BlockSpec((tm, LANE), lambda i: (i, 0)),
    ]
    args = [x1, mod, lw["g_pre_ffn"], lw["g_post_ffn"], lw["w_gate_s"], lw["w_up_s"], lw["w_down_s"], yk, wgt]
    aliases = {}
    if prev is not None:
        in_specs.append(pl.BlockSpec(memory_space=pl.ANY))
        aliases = {len(args): 0}
        args.append(prev)
    return pl.pallas_call(
        _combine_kernel,
        grid=(n_tiles,),
        in_specs=in_specs,
        out_specs=pl.BlockSpec((tm, d), glob),
        out_shape=jax.ShapeDtypeStruct((t, d), F32),
        input_output_aliases=aliases,
        compiler_params=_params(1),
        name="moe_combine",
    )(*args)


def _sc_mesh():
    return plsc.VectorSubcoreMesh(core_axis_name="core", subcore_axis_name="subcore")


def _sc_scatter_rows(src, idx, n_out):
    n_lists, n = idx.shape
    width = src.shape[1]

    @pl.kernel(out_type=jax.ShapeDtypeStruct((n_out, width), src.dtype), mesh=_sc_mesh(), scratch_types=[])
    def scatter(x_hbm, *refs):
        i_hbms, o_hbm = refs[:n_lists], refs[n_lists]

        def body(x_vmem, *i_vmems):
            for i_vmem in i_vmems:
                pltpu.sync_copy(x_vmem, o_hbm.at[i_vmem.at[0]])

        pltpu.emit_pipeline(
            body,
            grid=(n // SC_WINDOW,),
            in_specs=[pl.BlockSpec((SC_WINDOW, width), lambda i: (i, 0))]
            + [pl.BlockSpec((1, SC_WINDOW), lambda i: (0, i))] * n_lists,
            out_specs=[],
            core_axis_name=("core", "subcore"),
            dimension_semantics=(pltpu.PARALLEL,),
        )(x_hbm, *i_hbms)

    return scatter(src, *[idx[r].reshape(1, n) for r in range(n_lists)])


def _sc_gather_rows(table, idx):
    n = idx.shape[0]
    width = table.shape[1]

    @pl.kernel(out_type=jax.ShapeDtypeStruct((n, width), table.dtype), mesh=_sc_mesh(), scratch_types=[])
    def gather(x_hbm, i_hbm, o_hbm):
        def body(i_vmem, o_vmem):
            pltpu.sync_copy(x_hbm.at[i_vmem.at[0]], o_vmem)

        pltpu.emit_pipeline(
            body,
            grid=(n // SC_WINDOW,),
            in_specs=[pl.BlockSpec((1, SC_WINDOW), lambda i: (0, i))],
            out_specs=[pl.BlockSpec((SC_WINDOW, width), lambda i: (i, 0))],
            core_axis_name=("core", "subcore"),
            dimension_semantics=(pltpu.PARALLEL,),
        )(i_hbm, o_hbm)

    return gather(table, idx.reshape(1, n))


def _moe_sparse(x1, mod, seq_len, lw, w_gate_e, w_up_e, w_down_e, layer, *, tm, tm_e, n_parts):
    n_experts = lw["b_router"].shape[1]
    assert n_experts * TOP_K == LANE
    n_tok_tiles = x1.shape[0] // tm // n_parts
    t = n_tok_tiles * tm
    n_row_tiles = (t * TOP_K) // tm_e + n_experts
    p = n_row_tiles * tm_e
    experts = jnp.arange(n_experts, dtype=jnp.int32)

    staged = []
    for part in range(n_parts):
        tp, sel, rank, cnt, wgt = _route(x1, mod, seq_len, lw, tm=tm, tile_off=part * n_tok_tiles,
                                         n_tiles=n_tok_tiles)
        rows = tp.shape[0]
        cnt = cnt[:, :, 0].reshape(n_tok_tiles, TOP_K, n_experts).sum(axis=1)
        padded = (cnt.sum(axis=0) + tm_e - 1) // tm_e * tm_e
        group_end = jnp.cumsum(padded)
        base = (group_end - padded)[None, :] + jnp.cumsum(cnt, axis=0) - cnt
        chosen = sel[:, :TOP_K, :, None] == experts
        pos = jnp.sum(jnp.where(chosen, base[:, None, None, :], 0), axis=-1) + rank[:, :TOP_K, :]
        pos = pos.transpose(1, 0, 2).reshape(TOP_K, t)

        n_used = group_end[-1] // tm_e
        tile_start = jnp.arange(n_row_tiles, dtype=jnp.int32) * tm_e
        tile_expert = jnp.sum(tile_start[:, None] >= group_end[None, :], axis=1).astype(jnp.int32)
        tile_expert = jnp.minimum(tile_expert, n_experts - 1)
        tile_expert = jnp.where(tile_start < group_end[-1], tile_expert, tile_expert[n_used - 1])
        meta = jnp.stack([n_used, n_used]).astype(jnp.int32)

        idx = pos[:, None, :] + (jnp.arange(rows, dtype=jnp.int32) * p)[None, :, None]
        xs = _sc_scatter_rows(tp.reshape(rows * t, PACK_W), idx.reshape(TOP_K, rows * t), rows * p)
        staged.append((xs.reshape(rows, p, PACK_W), tile_expert, meta, idx.reshape(-1), wgt))

    gathered = []
    for xs, tile_expert, meta, idx, wgt in staged:
        ys = _experts(xs, tile_expert, meta, w_gate_e, w_up_e, w_down_e, layer, tm=tm_e)
        yk = _sc_gather_rows(ys.reshape(-1, PACK_W), idx)
        gathered.append((yk.reshape(TOP_K, xs.shape[0], t, PACK_W), wgt))

    out = None
    for part, (yk, wgt) in enumerate(gathered):
        out = _combine(x1, mod, seq_len, lw, yk, wgt, tm=tm, tile_off=part * n_tok_tiles, prev=out)
    return out


def _rope_tables(n):
    rows = n // GRID_W
    r, col = jnp.meshgrid(jnp.arange(rows), jnp.arange(GRID_W), indexing="ij")
    r = r.reshape(-1).astype(F32)
    col = col.reshape(-1).astype(F32)
    pairs = QK_ROPE // 4
    inv = ROPE_BASE ** (-jnp.arange(pairs, dtype=F32) / pairs)
    ang = jnp.concatenate([r[:, None] * inv, col[:, None] * inv], axis=-1)
    cos, sin = jnp.cos(ang), jnp.sin(ang)
    pad = HEAD_PAD - QK_NOPE - QK_ROPE
    cos_t = jnp.concatenate([jnp.ones((n, QK_NOPE), F32), cos, cos, jnp.zeros((n, pad), F32)], axis=1)
    sin_t = jnp.concatenate([jnp.zeros((n, QK_NOPE), F32), sin, sin, jnp.zeros((n, pad), F32)], axis=1)
    return cos_t, sin_t


def _identity_tables(n):
    pad = HEAD_PAD - QK_NOPE - QK_ROPE
    cos_t = jnp.concatenate([jnp.ones((n, QK_NOPE + QK_ROPE), F32), jnp.zeros((n, pad), F32)], axis=1)
    return cos_t, jnp.zeros((n, HEAD_PAD), F32)


def _position_dft(n):
    nb = 64 if n % 64 == 0 else 1
    na = n // nb
    m = jnp.arange(n, dtype=jnp.int32)[None, :]
    ang_a = ((jnp.arange(na, dtype=jnp.int32)[:, None] * m) % na).astype(F32) * (2.0 * np.pi / na)
    ang_b = ((jnp.arange(nb, dtype=jnp.int32)[:, None] * m) % n).astype(F32) * (2.0 * np.pi / n)
    ca, sa = jnp.cos(ang_a)[:, None, :], jnp.sin(ang_a)[:, None, :]
    cb, sb = jnp.cos(ang_b)[None, :, :], jnp.sin(ang_b)[None, :, :]
    norm = 1.0 / np.sqrt(n)
    cos = ((ca * cb - sa * sb) * norm).reshape(n, n)
    sin = ((sa * cb + ca * sb) * (-norm)).reshape(n, n)
    return jnp.concatenate([cos, sin], axis=1).astype(BF16)


def _channel_dft(width):
    gc = width // FOURIER_GROUPS
    idx = (jnp.arange(gc, dtype=jnp.int32)[:, None] * jnp.arange(gc, dtype=jnp.int32)[None, :]) % gc
    ang = idx.astype(F32) * (2.0 * np.pi / gc)
    eye = jnp.eye(FOURIER_GROUPS, dtype=F32)
    norm = 1.0 / np.sqrt(gc)
    return jnp.concatenate([jnp.kron(eye, jnp.cos(ang) * norm), jnp.kron(eye, jnp.sin(ang) * norm)],
                           axis=1).astype(BF16)


def _w1_kernel(w_ref, o_ref, *, kv_rank):
    cols = w_ref.shape[2]
    kv_end = kv_rank + QK_ROPE
    half = QK_ROPE // 2
    tail = HEAD_PAD - QK_NOPE - QK_ROPE
    o_kpe = kv_rank
    o_rot = o_kpe + HEAD_PAD
    o_rest = o_rot + HEAD_PAD
    dt = o_ref.dtype
    o_ref[0, 0:kv_rank, :] = w_ref[0, 0:kv_rank, :].astype(dt)
    for base in (o_kpe, o_rot):
        o_ref[0, base:base + QK_NOPE, :] = jnp.zeros((QK_NOPE, cols), dt)
        o_ref[0, base + QK_NOPE + QK_ROPE:base + HEAD_PAD, :] = jnp.zeros((tail, cols), dt)
    o_ref[0, o_kpe + QK_NOPE:o_kpe + QK_NOPE + QK_ROPE, :] = w_ref[0, kv_rank:kv_end, :].astype(dt)
    o_ref[0, o_rot + QK_NOPE:o_rot + QK_NOPE + half, :] = (-w_ref[0, kv_rank + half:kv_end, :]).astype(dt)
    o_ref[0, o_rot + QK_NOPE + half:o_rot + QK_NOPE + QK_ROPE, :] = w_ref[0, kv_rank:kv_rank + half, :].astype(dt)
    o_ref[0, o_rest:, :] = w_ref[0, kv_end:, :].astype(dt)


def _prep_w1(w_in_t, kv_rank):
    n_layers, width, d = w_in_t.shape
    out_w = width - QK_ROPE + 2 * HEAD_PAD
    tc = _tile(d, 256)
    return pl.pallas_call(
        functools.partial(_w1_kernel, kv_rank=kv_rank),
        grid=(n_layers, d // tc),
        in_specs=[pl.BlockSpec((1, width, tc), lambda l, i: (l, 0, i))],
        out_specs=pl.BlockSpec((1, out_w, tc), lambda l, i: (l, 0, i)),
        out_shape=jax.ShapeDtypeStruct((n_layers, out_w, d), BF16),
        compiler_params=_params(2),
        name="prep_w1",
    )(w_in_t)


def _rot_cols(w):
    half = w.shape[-1] // 2
    return jnp.concatenate([-w[..., half:], w[..., :half]], axis=-1)


def _layer_weights(l, g_pre_mix, g_post_mix, g_pre_ffn, g_post_ffn, w1, b_gate, g_q, w_uq, g_kv,
                   w_ukv, w_mla_out, conv_w, w_conv_out, w_four_out, w_out, w_router, b_router,
                   w_gate_s, w_up_s, w_down_s, dc):
    kv_rank = g_kv.shape[1]
    q_rank = g_q.shape[1]
    qk_dim = QK_NOPE + QK_ROPE
    pad = HEAD_PAD - qk_dim
    uq = w_uq[l].reshape(q_rank, N_HEADS, qk_dim) * (qk_dim ** -0.5 * np.log2(np.e))
    zq = jnp.zeros((q_rank, N_HEADS, pad), F32)
    wuq = jnp.concatenate([uq, zq], axis=-1).reshape(q_rank, N_HEADS * HEAD_PAD).astype(BF16)
    wuq_rot = jnp.concatenate([jnp.zeros((q_rank, N_HEADS, QK_NOPE), F32), _rot_cols(uq[..., QK_NOPE:]), zq],
                              axis=-1).reshape(q_rank, N_HEADS * HEAD_PAD).astype(BF16)
    ukv = w_ukv[l].reshape(kv_rank, N_HEADS, QK_NOPE + V_DIM)
    wuk = jnp.concatenate([ukv[..., :QK_NOPE], jnp.zeros((kv_rank, N_HEADS, HEAD_PAD - QK_NOPE), F32)],
                          axis=-1).reshape(kv_rank, N_HEADS * HEAD_PAD).astype(BF16)
    wuv = jnp.concatenate([ukv[..., QK_NOPE:], jnp.zeros((kv_rank, N_HEADS, HEAD_PAD - V_DIM), F32)],
                          axis=-1).reshape(kv_rank, N_HEADS * HEAD_PAD).astype(BF16)
    v_one = jnp.tile((jnp.arange(HEAD_PAD) == V_DIM).astype(F32), N_HEADS)[None]

    wr = w_router[l]
    wr_hi = wr.astype(BF16)
    wr_lo = (wr - wr_hi.astype(F32)).astype(BF16)
    return {
        "g_pre_mix": g_pre_mix[l][None], "g_post_mix": g_post_mix[l][None],
        "g_pre_ffn": g_pre_ffn[l][None], "g_post_ffn": g_post_ffn[l][None],
        "w1": w1, "layer": l, "b_gate": b_gate[l][None], "g_q": g_q[l][None], "g_kv": g_kv[l][None],
        "wuq": wuq, "wuq_rot": wuq_rot, "wuk": wuk, "wuv": wuv, "v_one": v_one, "dc": dc,
        "conv_w": conv_w[l],
        "w_mla_out": w_mla_out[l].astype(BF16), "w_conv_out": w_conv_out[l].astype(BF16),
        "w_four_out": w_four_out[l].astype(BF16), "w_out": w_out[l].astype(BF16),
        "w_router": jnp.concatenate([wr_hi, wr_lo], axis=1), "b_router": b_router[l][None],
        "w_router_t": jnp.concatenate([wr_hi.T, wr_lo.T], axis=0), "b_router_t": b_router[l][:, None],
        "w_gate_s": w_gate_s[l].astype(BF16), "w_up_s": w_up_s[l].astype(BF16),
        "w_down_s": w_down_s[l].astype(BF16),
    }


def _tile(n, pref):
    return pref if n % pref == 0 else n


def kernel(x, c, ctx, c_ctx, w_ada, b_ada, g_pre_mix, g_post_mix, g_pre_ffn, g_post_ffn, w_in, b_gate,
           g_q, w_uq, g_kv, w_ukv, w_mla_out, conv_w, w_conv_out, w_four_out, w_out, w_router, b_router,
           w_gate_e, w_up_e, w_down_e, w_gate_s, w_up_s, w_down_s):
    batch, seq, d = x.shape
    n_ctx = ctx.shape[1]
    n_layers = w_in.shape[0]
    xs = x.reshape(batch * seq, d)
    cs = ctx.reshape(batch * n_ctx, d)

    mod_rows = 16
    c_all = jnp.concatenate([c, c_ctx[None], jnp.zeros((mod_rows - batch - 1, d), F32)], axis=0)
    ada = _ada(c_all, w_ada, b_ada)

    tab_x = _rope_tables(seq)
    tm_c = _tile(n_ctx, 256)
    tab_c = _identity_tables(tm_c)
    cs_x = _position_dft(seq)
    cs_c = _position_dft(n_ctx)
    dc = _channel_dft(w_four_out.shape[1])
    w1 = _prep_w1(jnp.swapaxes(w_in, 1, 2), g_kv.shape[1])

    tm_x = _tile(seq, 512)
    tm_moe_x = _tile(seq, 1024)
    tm_moe_c = _tile(batch * n_ctx, 1024)

    for l in range(n_layers):
        last = l == n_layers - 1
        lw = _layer_weights(l, g_pre_mix, g_post_mix, g_pre_ffn, g_post_ffn, w1, b_gate, g_q, w_uq,
                            g_kv, w_ukv, w_mla_out, conv_w, w_conv_out, w_four_out, w_out, w_router,
                            b_router, w_gate_s, w_up_s, w_down_s, dc)
        mods = ada[l].reshape(mod_rows, 6, d)
        mod_x = mods[:batch]
        mod_c = mods[batch:batch + 1]

        pc = _inproj(cs, mod_c, n_ctx, lw, tab_c, kv_only=last, tm=tm_c)
        px = _inproj(xs, mod_x, seq, lw, tab_x, kv_only=False, tm=tm_x)
        o_x = _attention(px["q"], [(pc["k"], pc["v"], n_ctx), (px["k"], px["v"], seq)], batch, seq, tq=tm_x)
        f_x = _fourier(px["ab"], cs_x, batch, seq, tn=tm_x)
        x1 = _merge(xs, mod_x, seq, px, o_x, f_x, lw, tm=tm_x)
        xs = _moe_sparse(x1, mod_x, seq, lw, w_gate_e, w_up_e, w_down_e, l, tm=tm_moe_x, tm_e=1024,
                         n_parts=1)
        if not last:
            o_c = _attention(pc["q"], [(pc["k"], pc["v"], n_ctx)], batch, n_ctx, tq=tm_c)
            f_c = _fourier(pc["ab"], cs_c, batch, n_ctx, tn=tm_c)
            c1 = _merge(cs, mod_c, n_ctx, pc, o_c, f_c, lw, tm=tm_c)
            cs = _moe_sparse(c1, mod_c, batch * n_ctx, lw, w_gate_e, w_up_e, w_down_e, l, tm=tm_moe_c,
                             tm_e=512, n_parts=1)
    return xs.reshape(batch, seq, d)
```

```python
import functools

import numpy as np
import jax
import jax.numpy as jnp
from jax import lax
from jax.experimental import pallas as pl
from jax.experimental.pallas import tpu as pltpu
from jax.experimental.pallas import tpu_sc as plsc

N_HEADS = 8
QK_NOPE = 64
QK_ROPE = 32
V_DIM = 64
GRID_W = 64
ROPE_BASE = 10000.0
FOURIER_GROUPS = 4
TOP_K = 4
ROUTED_SCALE = 2.5
N_BRANCHES = 3
EPS = 1e-6

LANE = 128
HEAD_PAD = LANE
VMEM_LIMIT = 56 * 1024 * 1024
PACK_W = 256
SC_WINDOW = 128
ATTN_ROWS = 512

F32 = jnp.float32
BF16 = jnp.bfloat16


def _rms(x, g):
    return x * lax.rsqrt(jnp.mean(x * x, axis=-1, keepdims=True) + EPS) * g


def _sigmoid(x):
    return 1.0 / (1.0 + jnp.exp(-x))


def _dot(a, b):
    return jnp.dot(a, b, preferred_element_type=F32)


def _dot_t(a, b_t):
    return lax.dot_general(a, b_t, (((1,), (1,)), ((), ())), preferred_element_type=F32)


def _resident(shape):
    nd = len(shape)
    return pl.BlockSpec(shape, lambda *_: (0,) * nd, pipeline_mode=pl.Buffered(1))


def _params(n_grid):
    return pltpu.CompilerParams(dimension_semantics=("arbitrary",) * n_grid,
                                vmem_limit_bytes=VMEM_LIMIT)


def _ada_kernel(c_ref, w_ref, b_ref, o_ref):
    c = c_ref[...]
    a = (c * _sigmoid(c)).astype(BF16)
    o_ref[0] = _dot(a, w_ref[0].astype(BF16)) + b_ref[0]


def _ada(c_all, w_ada, b_ada):
    n_layers, d, n_out = w_ada.shape
    rows = c_all.shape[0]
    tn = 1536
    return pl.pallas_call(
        _ada_kernel,
        grid=(n_layers, n_out // tn),
        in_specs=[
            pl.BlockSpec((rows, d), lambda l, j: (0, 0)),
            pl.BlockSpec((1, d, tn), lambda l, j: (l, 0, j)),
            pl.BlockSpec((1, 1, tn), lambda l, j: (l, 0, j)),
        ],
        out_specs=pl.BlockSpec((1, rows, tn), lambda l, j: (l, 0, j)),
        out_shape=jax.ShapeDtypeStruct((n_layers, rows, n_out), F32),
        compiler_params=_params(2),
        name="ada",
    )(c_all, w_ada, b_ada.reshape(n_layers, 1, n_out))


def _inproj_kernel(*refs, kv_only, kv_rank, q_rank, conv_w, four_w, d_model):
    if kv_only:
        (x_ref, mod_ref, gpre_ref, w1_ref, gkv_ref, wuk_ref, wuv_ref, vone_ref, cos_ref, sin_ref,
         k_ref, v_ref) = refs
    else:
        (x_ref, mod_ref, gpre_ref, w1_ref, gkv_ref, wuk_ref, wuv_ref, vone_ref, cos_ref, sin_ref,
         bg_ref, gq_ref, wuq_ref, wuqr_ref, dc_ref,
         k_ref, v_ref, q_ref, cb_ref, cc_ref, cu_ref, ab_ref, gate_ref) = refs

    x = x_ref[...]
    shift = mod_ref[0, 0:1, :]
    scale = mod_ref[0, 1:2, :]
    h = (_rms(x, gpre_ref[...]) * (1.0 + scale) + shift).astype(BF16)
    cos = cos_ref[...]
    sin = sin_ref[...]

    o_kpe = kv_rank
    o_rot = o_kpe + HEAD_PAD
    o_q = o_rot + HEAD_PAD
    p = _dot_t(h, w1_ref[0, 0:o_q, :])
    ckv = _rms(p[:, 0:kv_rank], gkv_ref[...]).astype(BF16)
    kpe = p[:, o_kpe:o_rot] * cos + p[:, o_rot:o_q] * sin
    k = _dot(ckv, wuk_ref[...]) + jnp.concatenate([kpe] * N_HEADS, axis=1)
    k_ref[...] = k.astype(k_ref.dtype)
    v_ref[...] = (_dot(ckv, wuv_ref[...]) + vone_ref[...]).astype(v_ref.dtype)
    if kv_only:
        return

    o_cb = o_q + q_rank
    cq = _rms(_dot_t(h, w1_ref[0, o_q:o_cb, :]), gq_ref[...]).astype(BF16)
    cos_h = jnp.concatenate([cos] * N_HEADS, axis=1)
    sin_h = jnp.concatenate([sin] * N_HEADS, axis=1)
    q = _dot(cq, wuq_ref[...]) * cos_h + _dot(cq, wuqr_ref[...]) * sin_h
    q_ref[...] = q.astype(q_ref.dtype)

    o_cc = o_cb + conv_w
    o_cu = o_cc + conv_w
    o_four = o_cu + conv_w
    cb_ref[...] = _dot_t(h, w1_ref[0, o_cb:o_cc, :]).astype(cb_ref.dtype)
    cc_ref[...] = _dot_t(h, w1_ref[0, o_cc:o_cu, :]).astype(cc_ref.dtype)
    cu_ref[...] = _dot_t(h, w1_ref[0, o_cu:o_four, :]).astype(cu_ref.dtype)

    o_gate = o_four + four_w
    uf = _dot_t(h, w1_ref[0, o_four:o_gate, :]).astype(BF16)
    ab_ref[...] = _dot(uf, dc_ref[...]).astype(ab_ref.dtype)

    for j in range(N_BRANCHES):
        lo = o_gate + j * d_model
        z = _dot_t(h, w1_ref[0, lo:lo + d_model, :]) + bg_ref[:, j * d_model:(j + 1) * d_model]
        gate_ref[:, j * d_model:(j + 1) * d_model] = _sigmoid(z).astype(gate_ref.dtype)


def _inproj(xs, mod, seq_len, lw, tables, *, kv_only, tm):
    t, d = xs.shape
    nb = mod.shape[0]
    tiles_per_seq = seq_len // tm
    cos_t, sin_t = tables
    table_tiles = cos_t.shape[0] // tm
    kv_rank = lw["g_kv"].shape[1]
    q_rank = lw["g_q"].shape[1]
    conv_w = lw["conv_w"].shape[1]
    four_w = lw["dc"].shape[0]
    n_k = N_HEADS * HEAD_PAD

    def row(i):
        return (i, 0)

    def mod_map(i):
        return ((i // tiles_per_seq) % nb, 0, 0)

    def tab_map(i):
        return (i % table_tiles, 0)

    w1, layer = lw["w1"], lw["layer"]
    w1_rows = kv_rank + 2 * HEAD_PAD if kv_only else w1.shape[1]
    in_specs = [
        pl.BlockSpec((tm, d), row),
        pl.BlockSpec((1,) + mod.shape[1:], mod_map),
        _resident((1, d)),
        pl.BlockSpec((1, w1_rows, d), lambda i: (layer, 0, 0), pipeline_mode=pl.Buffered(1)),
        _resident((1, kv_rank)),
        _resident(lw["wuk"].shape),
        _resident(lw["wuv"].shape),
        _resident(lw["v_one"].shape),
        pl.BlockSpec((tm, HEAD_PAD), tab_map),
        pl.BlockSpec((tm, HEAD_PAD), tab_map),
    ]
    args = [xs, mod, lw["g_pre_mix"], w1, lw["g_kv"], lw["wuk"], lw["wuv"], lw["v_one"], cos_t, sin_t]
    out_shape = [jax.ShapeDtypeStruct((t, n_k), BF16), jax.ShapeDtypeStruct((t, n_k), BF16)]
    out_specs = [pl.BlockSpec((tm, n_k), row), pl.BlockSpec((tm, n_k), row)]
    if not kv_only:
        in_specs += [
            _resident(lw["b_gate"].shape),
            _resident((1, q_rank)),
            _resident(lw["wuq"].shape),
            _resident(lw["wuq_rot"].shape),
            _resident(lw["dc"].shape),
        ]
        args += [lw["b_gate"], lw["g_q"], lw["wuq"], lw["wuq_rot"], lw["dc"]]
        widths = [n_k, conv_w, conv_w, conv_w, 2 * four_w, N_BRANCHES * d]
        out_shape += [jax.ShapeDtypeStruct((t, w), BF16) for w in widths]
        out_specs += [pl.BlockSpec((tm, w), row) for w in widths]
    outs = pl.pallas_call(
        functools.partial(_inproj_kernel, kv_only=kv_only, kv_rank=kv_rank, q_rank=q_rank,
                          conv_w=conv_w, four_w=four_w, d_model=d),
        grid=(t // tm,),
        in_specs=in_specs,
        out_specs=out_specs,
        out_shape=out_shape,
        compiler_params=_params(1),
        name="inproj_kv" if kv_only else "inproj",
    )(*args)
    names = ["k", "v", "q", "cb", "cc", "cu", "ab", "gate"]
    return dict(zip(names, outs))


def _attn_kernel(*refs, n_seg):
    q_ref = refs[0]
    o_ref = refs[-1]
    chunk = min(q_ref.shape[0], ATTN_ROWS)
    for c in range(q_ref.shape[0] // chunk):
        rows = slice(c * chunk, (c + 1) * chunk)
        outs = []
        for hh in range(N_HEADS):
            head = slice(hh * HEAD_PAD, (hh + 1) * HEAD_PAD)
            qh = q_ref[rows, head]
            s = [_dot_t(qh, refs[1 + 2 * i][:, head]) for i in range(n_seg)]
            m = functools.reduce(jnp.maximum, [jnp.max(si, axis=-1, keepdims=True) for si in s])
            acc = functools.reduce(jnp.add, [
                _dot(jnp.exp2((s[i] - m).astype(BF16)), refs[2 + 2 * i][:, head]) for i in range(n_seg)])
            outs.append(acc[:, 0:V_DIM] / acc[:, V_DIM:V_DIM + 1])
        o_ref[rows, :] = jnp.concatenate(outs, axis=1).astype(o_ref.dtype)


def _attention(q, segs, batch, seq_q, *, tq):
    t = q.shape[0]
    qt = seq_q // tq
    n_k = N_HEADS * HEAD_PAD
    in_specs = [pl.BlockSpec((tq, n_k), lambda b, j: (b * qt + j, 0))]
    args = [q]
    for k, v, m in segs:
        in_specs.append(pl.BlockSpec((m, n_k), lambda b, j: (b, 0)))
        in_specs.append(pl.BlockSpec((m, n_k), lambda b, j: (b, 0)))
        args += [k, v]
    return pl.pallas_call(
        functools.partial(_attn_kernel, n_seg=len(segs)),
        grid=(batch, qt),
        in_specs=in_specs,
        out_specs=pl.BlockSpec((tq, N_HEADS * V_DIM), lambda b, j: (b * qt + j, 0)),
        out_shape=jax.ShapeDtypeStruct((t, N_HEADS * V_DIM), BF16),
        compiler_params=_params(2),
        name="attention",
    )(*args)


def _four_kernel(cs_ref, ab_ref, o_ref, *, n, fw):
    o = _dot(cs_ref[:, 0:n], ab_ref[:, 0:fw]) + _dot(cs_ref[:, n:2 * n], ab_ref[:, fw:2 * fw])
    o_ref[...] = o.astype(o_ref.dtype)


def _fourier(ab, cs, batch, seq_len, *, tn):
    t, fw2 = ab.shape
    fw = fw2 // 2
    nt = seq_len // tn
    return pl.pallas_call(
        functools.partial(_four_kernel, n=seq_len, fw=fw),
        grid=(batch, nt),
        in_specs=[
            pl.BlockSpec((tn, 2 * seq_len), lambda b, j: (j, 0)),
            pl.BlockSpec((seq_len, fw2), lambda b, j: (b, 0)),
        ],
        out_specs=pl.BlockSpec((tn, fw), lambda b, j: (b * nt + j, 0)),
        out_shape=jax.ShapeDtypeStruct((t, fw), BF16),
        compiler_params=_params(2),
        name="fourier",
    )(cs, ab)


def _merge_kernel(x_ref, mod_ref, o_ref, cb_ref, cc_ref, cu_ref, ccp_ref, cup_ref, ccn_ref,
                  cun_ref, f_ref, gate_ref, convw_ref, wmo_ref, wco_ref, wfo_ref, wout_ref,
                  gpost_ref, out_ref, pad_ref, *, tiles_per_seq, tm, d_model):
    i = pl.program_id(0)
    pos = i % tiles_per_seq
    has_prev = (pos > 0).astype(F32)
    has_next = (pos < tiles_per_seq - 1).astype(F32)
    pad_ref[0:8, :] = ccp_ref[...].astype(F32) * cup_ref[...].astype(F32) * has_prev
    pad_ref[8:8 + tm, :] = cc_ref[...].astype(F32) * cu_ref[...].astype(F32)
    pad_ref[8 + tm:16 + tm, :] = ccn_ref[...].astype(F32) * cun_ref[...].astype(F32) * has_next
    conv = (pad_ref[7:7 + tm, :] * convw_ref[0:1, :] + pad_ref[8:8 + tm, :] * convw_ref[1:2, :]
            + pad_ref[9:9 + tm, :] * convw_ref[2:3, :])
    y_conv = _dot((cb_ref[...].astype(F32) * conv).astype(BF16), wco_ref[...])
    y_attn = _dot(o_ref[...], wmo_ref[...])
    y_four = _dot(f_ref[...], wfo_ref[...])
    d = d_model
    merged = (gate_ref[:, 0:d].astype(F32) * y_attn + gate_ref[:, d:2 * d].astype(F32) * y_conv
              + gate_ref[:, 2 * d:3 * d].astype(F32) * y_four)
    y = _dot(merged.astype(BF16), wout_ref[...])
    g1 = mod_ref[0, 2:3, :]
    out_ref[...] = x_ref[...] + g1 * _rms(y, gpost_ref[...])


def _merge(xs, mod, seq_len, pr, o, four, lw, *, tm):
    t, d = xs.shape
    nb = mod.shape[0]
    tiles_per_seq = seq_len // tm
    cw = lw["conv_w"].shape[1]
    fw = four.shape[1]
    hb = tm // 8
    last_hb = t // 8 - 1

    def row(i):
        return (i, 0)

    def prev_map(i):
        return (jnp.maximum(i * hb - 1, 0), 0)

    def next_map(i):
        return (jnp.minimum((i + 1) * hb, last_hb), 0)

    in_specs = [
        pl.BlockSpec((tm, d), row),
        pl.BlockSpec((1,) + mod.shape[1:], lambda i: ((i // tiles_per_seq) % nb, 0, 0)),
        pl.BlockSpec((tm, o.shape[1]), row),
        pl.BlockSpec((tm, cw), row),
        pl.BlockSpec((tm, cw), row),
        pl.BlockSpec((tm, cw), row),
        pl.BlockSpec((8, cw), prev_map),
        pl.BlockSpec((8, cw), prev_map),
        pl.BlockSpec((8, cw), next_map),
        pl.BlockSpec((8, cw), next_map),
        pl.BlockSpec((tm, fw), row),
        pl.BlockSpec((tm, N_BRANCHES * d), row),
        _resident(lw["conv_w"].shape),
        _resident(lw["w_mla_out"].shape),
        _resident(lw["w_conv_out"].shape),
        _resident(lw["w_four_out"].shape),
        _resident(lw["w_out"].shape),
        _resident((1, d)),
    ]
    return pl.pallas_call(
        functools.partial(_merge_kernel, tiles_per_seq=tiles_per_seq, tm=tm, d_model=d),
        grid=(t // tm,),
        in_specs=in_specs,
        out_specs=pl.BlockSpec((tm, d), row),
        out_shape=jax.ShapeDtypeStruct((t, d), F32),
        scratch_shapes=[pltpu.VMEM((tm + 16, cw), F32)],
        compiler_params=_params(1),
        name="merge",
    )(xs, mod, o, pr["cb"], pr["cc"], pr["cu"], pr["cc"], pr["cu"], pr["cc"], pr["cu"], four,
      pr["gate"], lw["conv_w"], lw["w_mla_out"], lw["w_conv_out"], lw["w_four_out"], lw["w_out"],
      lw["g_post_mix"])


def _moe_kernel(x_ref, mod_ref, gpre_ref, gpost_ref, wr_ref, br_ref, wgs_ref, wus_ref, wds_ref,
                wge_ref, wue_ref, wde_ref, out_ref, t_ref, comb_ref, acc_ref, *, n_experts):
    e = pl.program_id(1)

    @pl.when(e == 0)
    def _():
        shift = mod_ref[0, 3:4, :]
        scale = mod_ref[0, 4:5, :]
        t = _rms(x_ref[...], gpre_ref[...]) * (1.0 + scale) + shift
        t_hi = t.astype(BF16)
        t_lo = (t - t_hi.astype(F32)).astype(BF16)
        t_ref[...] = t_hi
        hh = _dot(t_hi, wr_ref[...])
        logits = hh[:, 0:n_experts] + hh[:, n_experts:2 * n_experts] + _dot(t_lo, wr_ref[:, 0:n_experts])
        scores = _sigmoid(logits)
        work = scores + br_ref[...]
        lane = lax.broadcasted_iota(jnp.int32, scores.shape, 1)
        comb = jnp.zeros_like(scores)
        for _ in range(TOP_K):
            best = jnp.max(work, axis=-1, keepdims=True)
            first = jnp.min(jnp.where(work == best, lane, n_experts), axis=-1, keepdims=True)
            hit = lane == first
            comb = jnp.where(hit, scores, comb)
            work = jnp.where(hit, -jnp.inf, work)
        comb_ref[...] = comb / jnp.sum(comb, axis=-1, keepdims=True) * ROUTED_SCALE
        gate = _dot(t_hi, wgs_ref[...])
        act = (gate * _sigmoid(gate) * _dot(t_hi, wus_ref[...])).astype(BF16)
        acc_ref[...] = _dot(act, wds_ref[...])

    t_hi = t_ref[...]
    gate = _dot(t_hi, wge_ref[0, 0].astype(BF16))
    up = _dot(t_hi, wue_ref[0, 0].astype(BF16))
    lane = lax.broadcasted_iota(jnp.int32, comb_ref.shape, 1)
    w_e = jnp.sum(jnp.where(lane == e, comb_ref[...], 0.0), axis=-1, keepdims=True)
    act = (gate * _sigmoid(gate) * up * w_e).astype(BF16)
    acc_ref[...] += _dot(act, wde_ref[0, 0].astype(BF16))

    @pl.when(e == n_experts - 1)
    def _():
        g2 = mod_ref[0, 5:6, :]
        out_ref[...] = x_ref[...] + g2 * _rms(acc_ref[...], gpost_ref[...])


def _moe(xs, mod, seq_len, lw, w_gate_e, w_up_e, w_down_e, layer, *, tm):
    t, d = xs.shape
    nb = mod.shape[0]
    tiles_per_seq = max(seq_len // tm, 1)
    n_experts, _, f = w_gate_e.shape[1:]
    in_specs = [
        pl.BlockSpec((tm, d), lambda i, e: (i, 0)),
        pl.BlockSpec((1,) + mod.shape[1:], lambda i, e: ((i // tiles_per_seq) % nb, 0, 0)),
        _resident((1, d)),
        _resident((1, d)),
        _resident(lw["w_router"].shape),
        _resident((1, n_experts)),
        _resident(lw["w_gate_s"].shape),
        _resident(lw["w_up_s"].shape),
        _resident(lw["w_down_s"].shape),
        pl.BlockSpec((1, 1, d, f), lambda i, e: (layer, e, 0, 0)),
        pl.BlockSpec((1, 1, d, f), lambda i, e: (layer, e, 0, 0)),
        pl.BlockSpec((1, 1, f, d), lambda i, e: (layer, e, 0, 0)),
    ]
    return pl.pallas_call(
        functools.partial(_moe_kernel, n_experts=n_experts),
        grid=(t // tm, n_experts),
        in_specs=in_specs,
        out_specs=pl.BlockSpec((tm, d), lambda i, e: (i, 0)),
        out_shape=jax.ShapeDtypeStruct((t, d), F32),
        scratch_shapes=[pltpu.VMEM((tm, d), BF16), pltpu.VMEM((tm, n_experts), F32),
                        pltpu.VMEM((tm, d), F32)],
        compiler_params=_params(2),
        name="moe",
    )(xs, mod, lw["g_pre_ffn"], lw["g_post_ffn"], lw["w_router"], lw["b_router"], lw["w_gate_s"],
      lw["w_up_s"], lw["w_down_s"], w_gate_e, w_up_e, w_down_e)


def _pack_rows(v):
    bits = lax.bitcast_convert_type(v.astype(BF16).astype(F32), jnp.uint32)
    rows = []
    for j in range(v.shape[1] // (2 * PACK_W)):
        lo = bits[:, (2 * j) * PACK_W:(2 * j + 1) * PACK_W]
        hi = bits[:, (2 * j + 1) * PACK_W:(2 * j + 2) * PACK_W]
        rows.append(lax.bitcast_convert_type((hi & jnp.uint32(0xFFFF0000)) | (lo >> 16), jnp.int32))
    return rows


def _unpack_rows(rows):
    parts = []
    for r in rows:
        u = lax.bitcast_convert_type(r, jnp.uint32)
        parts.append(lax.bitcast_convert_type(u << 16, F32))
        parts.append(lax.bitcast_convert_type(u & jnp.uint32(0xFFFF0000), F32))
    return jnp.concatenate(parts, axis=1)


def _route_kernel(x_ref, mod_ref, gpre_ref, wrt_ref, brt_ref, triu_ref,
                  tp_ref, sel_ref, rank_ref, cnt_ref, wgt_ref, *, n_experts):
    shift = mod_ref[0, 3:4, :]
    scale = mod_ref[0, 4:5, :]
    t = _rms(x_ref[...], gpre_ref[...]) * (1.0 + scale) + shift
    t_hi = t.astype(BF16)
    t_lo = (t - t_hi.astype(F32)).astype(BF16)
    for j, r in enumerate(_pack_rows(t)):
        tp_ref[j] = r
    tm = t.shape[0]
    hh = _dot_t(wrt_ref[...], t_hi)
    logits = hh[0:n_experts] + hh[n_experts:2 * n_experts] + _dot_t(wrt_ref[0:n_experts, :], t_lo)
    scores = _sigmoid(logits)
    work = scores + brt_ref[...]
    row = lax.broadcasted_iota(jnp.int32, scores.shape, 0)
    wide = lax.broadcasted_iota(jnp.int32, (LANE, tm), 0)
    firsts, picked, hits = [], [], []
    for k in range(TOP_K):
        best = jnp.max(work, axis=0, keepdims=True)
        first = jnp.min(jnp.where(work == best, row, n_experts), axis=0, keepdims=True)
        hit = row == first
        firsts.append(first)
        picked.append(jnp.sum(jnp.where(hit, scores, 0.0), axis=0, keepdims=True))
        hits.append(wide == first + k * n_experts)
        work = jnp.where(hit, -jnp.inf, work)
    total = functools.reduce(jnp.add, picked)
    onehot = functools.reduce(jnp.add, [jnp.where(h, 1.0, 0.0) for h in hits])
    earlier = _dot(onehot.astype(BF16), triu_ref[...])
    col = jnp.broadcast_to(jnp.sum(onehot, axis=1, keepdims=True), (LANE, LANE))
    row_c = lax.broadcasted_iota(jnp.int32, (LANE, LANE), 0)
    before = jnp.zeros((LANE, LANE), F32)
    for s in range(1, TOP_K):
        before = before + jnp.where(row_c >= s * n_experts, pltpu.roll(col, s * n_experts, 0), 0.0)
    ahead = earlier + before[:, 0:1]
    row8 = lax.broadcasted_iota(jnp.int32, (8, tm), 0)
    sel = jnp.zeros((8, tm), jnp.int32)
    rank = jnp.zeros((8, tm), F32)
    wgt_t = jnp.zeros((LANE, tm), F32)
    for k in range(TOP_K):
        sel = jnp.where(row8 == k, firsts[k], sel)
        rank = jnp.where(row8 == k, jnp.sum(jnp.where(hits[k], ahead, 0.0), axis=0, keepdims=True), rank)
        wgt_t = jnp.where(wide == k, picked[k] / total * ROUTED_SCALE, wgt_t)
    sel_ref[0] = sel
    rank_ref[0] = rank.astype(jnp.int32)
    cnt_ref[0] = col.astype(jnp.int32)
    wgt_ref[...] = wgt_t.T


def _route(x1, mod, seq_len, lw, *, tm, tile_off, n_tiles):
    d = x1.shape[1]
    t = n_tiles * tm
    nb = mod.shape[0]
    tiles_per_seq = max(seq_len // tm, 1)
    n_experts = lw["b_router"].shape[1]
    rows = d // (2 * PACK_W)
    triu = jnp.tri(tm, tm, -1, dtype=BF16).T
    tile3 = lambda i: (i, 0, 0)
    return pl.pallas_call(
        functools.partial(_route_kernel, n_experts=n_experts),
        grid=(n_tiles,),
        in_specs=[
            pl.BlockSpec((tm, d), lambda i: (i + tile_off, 0)),
            pl.BlockSpec((1,) + mod.shape[1:], lambda i: (((i + tile_off) // tiles_per_seq) % nb, 0, 0)),
            _resident((1, d)),
            _resident(lw["w_router_t"].shape),
            _resident((n_experts, 1)),
            _resident((tm, tm)),
        ],
        out_specs=[
            pl.BlockSpec((rows, tm, PACK_W), lambda i: (0, i, 0)),
            pl.BlockSpec((1, 8, tm), tile3),
            pl.BlockSpec((1, 8, tm), tile3),
            pl.BlockSpec((1, LANE, LANE), tile3),
            pl.BlockSpec((tm, LANE), lambda i: (i, 0)),
        ],
        out_shape=[
            jax.ShapeDtypeStruct((rows, t, PACK_W), jnp.int32),
            jax.ShapeDtypeStruct((n_tiles, 8, tm), jnp.int32),
            jax.ShapeDtypeStruct((n_tiles, 8, tm), jnp.int32),
            jax.ShapeDtypeStruct((n_tiles, LANE, LANE), jnp.int32),
            jax.ShapeDtypeStruct((t, LANE), F32),
        ],
        compiler_params=_params(1),
        name="moe_route",
    )(x1, mod, lw["g_pre_ffn"], lw["w_router_t"], lw["b_router_t"], triu)


def _expert_kernel(te_ref, meta_ref, xs_ref, wg_ref, wu_ref, wd_ref, ys_ref, wg_sc, wu_sc, wd_sc):
    i = pl.program_id(0)
    live = i < meta_ref[0]
    new_expert = (i == 0) | (te_ref[i] != te_ref[jnp.maximum(i - 1, 0)])

    @pl.when(live & new_expert)
    def _():
        wg_sc[...] = wg_ref[0, 0].astype(BF16)
        wu_sc[...] = wu_ref[0, 0].astype(BF16)
        wd_sc[...] = wd_ref[0, 0].astype(BF16)

    @pl.when(live)
    def _():
        x = _unpack_rows([xs_ref[j] for j in range(xs_ref.shape[0])]).astype(BF16)
        gate = _dot(x, wg_sc[...])
        act = (gate * _sigmoid(gate) * _dot(x, wu_sc[...])).astype(BF16)
        for j, r in enumerate(_pack_rows(_dot(act, wd_sc[...]))):
            ys_ref[j] = r


def _experts(xs, tile_expert, meta, w_gate_e, w_up_e, w_down_e, layer, *, tm):
    rows, p, _ = xs.shape
    n_experts, d, f = w_gate_e.shape[1:]

    def slot(i, te, meta):
        return (0, jnp.minimum(i, meta[0] - 1), 0)

    grid_spec = pltpu.PrefetchScalarGridSpec(
        num_scalar_prefetch=2,
        grid=(p // tm,),
        in_specs=[
            pl.BlockSpec((rows, tm, PACK_W), slot),
            pl.BlockSpec((1, 1, d, f), lambda i, te, meta: (layer, te[i], 0, 0)),
            pl.BlockSpec((1, 1, d, f), lambda i, te, meta: (layer, te[i], 0, 0)),
            pl.BlockSpec((1, 1, f, d), lambda i, te, meta: (layer, te[i], 0, 0)),
        ],
        out_specs=pl.BlockSpec((rows, tm, PACK_W), slot),
        scratch_shapes=[pltpu.VMEM((d, f), BF16), pltpu.VMEM((d, f), BF16), pltpu.VMEM((f, d), BF16)],
    )
    return pl.pallas_call(
        _expert_kernel,
        grid_spec=grid_spec,
        out_shape=jax.ShapeDtypeStruct(xs.shape, jnp.int32),
        compiler_params=_params(1),
        name="moe_experts",
    )(tile_expert, meta, xs, w_gate_e, w_up_e, w_down_e)


def _combine_kernel(x_ref, mod_ref, gpre_ref, gpost_ref, wgs_ref, wus_ref, wds_ref, yk_ref, wgt_ref,
                    *rest):
    out_ref = rest[-1]
    shift = mod_ref[0, 3:4, :]
    scale = mod_ref[0, 4:5, :]
    x = x_ref[...]
    t_hi = (_rms(x, gpre_ref[...]) * (1.0 + scale) + shift).astype(BF16)
    gate = _dot(t_hi, wgs_ref[...])
    act = (gate * _sigmoid(gate) * _dot(t_hi, wus_ref[...])).astype(BF16)
    acc = _dot(act, wds_ref[...])
    for k in range(TOP_K):
        y = _unpack_rows([yk_ref[k, j] for j in range(yk_ref.shape[1])])
        acc = acc + wgt_ref[:, k:k + 1] * y
    g2 = mod_ref[0, 5:6, :]
    out_ref[...] = x + g2 * _rms(acc, gpost_ref[...])


def _combine(x1, mod, seq_len, lw, yk, wgt, *, tm, tile_off, prev):
    t, d = x1.shape
    nb = mod.shape[0]
    tiles_per_seq = max(seq_len // tm, 1)
    rows = yk.shape[1]
    n_tiles = yk.shape[2] // tm
    glob = lambda i: (i + tile_off, 0)
    in_specs = [
        pl.BlockSpec((tm, d), glob),
        pl.BlockSpec((1,) + mod.shape[1:], lambda i: (((i + tile_off) // tiles_per_seq) % nb, 0, 0)),
        _resident((1, d)),
        _resident((1, d)),
        _resident(lw["w_gate_s"].shape),
        _resident(lw["w_up_s"].shape),
        _resident(lw["w_down_s"].shape),
        pl.BlockSpec((TOP_K, rows, tm, PACK_W), lambda i: (0, 0, i, 0)),
        pl.BlockSpec((tm, LANE), lambda i: (i, 0)),
    ]
    args = [x1, mod, lw["g_pre_ffn"], lw["g_post_ffn"], lw["w_gate_s"], lw["w_up_s"], lw["w_down_s"], yk, wgt]
    aliases = {}
    if prev is not None:
        in_specs.append(pl.BlockSpec(memory_space=pl.ANY))
        aliases = {len(args): 0}
        args.append(prev)
    return pl.pallas_call(
        _combine_kernel,
        grid=(n_tiles,),
        in_specs=in_specs,
        out_specs=pl.BlockSpec((tm, d), glob),
        out_shape=jax.ShapeDtypeStruct((t, d), F32),
        input_output_aliases=aliases,
        compiler_params=_params(1),
        name="moe_combine",
    )(*args)


def _sc_mesh():
    return plsc.VectorSubcoreMesh(core_axis_name="core", subcore_axis_name="subcore")


def _sc_scatter_rows(src, idx, n_out):
    n_lists, n = idx.shape
    width = src.shape[1]

    @pl.kernel(out_type=jax.ShapeDtypeStruct((n_out, width), src.dtype), mesh=_sc_mesh(), scratch_types=[])
    def scatter(x_hbm, *refs):
        i_hbms, o_hbm = refs[:n_lists], refs[n_lists]

        def body(x_vmem, *i_vmems):
            for i_vmem in i_vmems:
                pltpu.sync_copy(x_vmem, o_hbm.at[i_vmem.at[0]])

        pltpu.emit_pipeline(
            body,
            grid=(n // SC_WINDOW,),
            in_specs=[pl.BlockSpec((SC_WINDOW, width), lambda i: (i, 0))]
            + [pl.BlockSpec((1, SC_WINDOW), lambda i: (0, i))] * n_lists,
            out_specs=[],
            core_axis_name=("core", "subcore"),
            dimension_semantics=(pltpu.PARALLEL,),
        )(x_hbm, *i_hbms)

    return scatter(src, *[idx[r].reshape(1, n) for r in range(n_lists)])


def _sc_gather_rows(table, idx):
    n = idx.shape[0]
    width = table.shape[1]

    @pl.kernel(out_type=jax.ShapeDtypeStruct((n, width), table.dtype), mesh=_sc_mesh(), scratch_types=[])
    def gather(x_hbm, i_hbm, o_hbm):
        def body(i_vmem, o_vmem):
            pltpu.sync_copy(x_hbm.at[i_vmem.at[0]], o_vmem)

        pltpu.emit_pipeline(
            body,
            grid=(n // SC_WINDOW,),
            in_specs=[pl.BlockSpec((1, SC_WINDOW), lambda i: (0, i))],
            out_specs=[pl.BlockSpec((SC_WINDOW, width), lambda i: (i, 0))],
            core_axis_name=("core", "subcore"),
            dimension_semantics=(pltpu.PARALLEL,),
        )(i_hbm, o_hbm)

    return gather(table, idx.reshape(1, n))


def _moe_sparse(x1, mod, seq_len, lw, w_gate_e, w_up_e, w_down_e, layer, *, tm, tm_e, n_parts):
    n_experts = lw["b_router"].shape[1]
    assert n_experts * TOP_K == LANE
    n_tok_tiles = x1.shape[0] // tm // n_parts
    t = n_tok_tiles * tm
    n_row_tiles = (t * TOP_K) // tm_e + n_experts
    p = n_row_tiles * tm_e
    experts = jnp.arange(n_experts, dtype=jnp.int32)

    staged = []
    for part in range(n_parts):
        tp, sel, rank, cnt, wgt = _route(x1, mod, seq_len, lw, tm=tm, tile_off=part * n_tok_tiles,
                                         n_tiles=n_tok_tiles)
        rows = tp.shape[0]
        cnt = cnt[:, :, 0].reshape(n_tok_tiles, TOP_K, n_experts).sum(axis=1)
        padded = (cnt.sum(axis=0) + tm_e - 1) // tm_e * tm_e
        group_end = jnp.cumsum(padded)
        base = (group_end - padded)[None, :] + jnp.cumsum(cnt, axis=0) - cnt
        chosen = sel[:, :TOP_K, :, None] == experts
        pos = jnp.sum(jnp.where(chosen, base[:, None, None, :], 0), axis=-1) + rank[:, :TOP_K, :]
        pos = pos.transpose(1, 0, 2).reshape(TOP_K, t)

        n_used = group_end[-1] // tm_e
        tile_start = jnp.arange(n_row_tiles, dtype=jnp.int32) * tm_e
        tile_expert = jnp.sum(tile_start[:, None] >= group_end[None, :], axis=1).astype(jnp.int32)
        tile_expert = jnp.minimum(tile_expert, n_experts - 1)
        tile_expert = jnp.where(tile_start < group_end[-1], tile_expert, tile_expert[n_used - 1])
        meta = jnp.stack([n_used, n_used]).astype(jnp.int32)

        idx = pos[:, None, :] + (jnp.arange(rows, dtype=jnp.int32) * p)[None, :, None]
        xs = _sc_scatter_rows(tp.reshape(rows * t, PACK_W), idx.reshape(TOP_K, rows * t), rows * p)
        staged.append((xs.reshape(rows, p, PACK_W), tile_expert, meta, idx.reshape(-1), wgt))

    gathered = []
    for xs, tile_expert, meta, idx, wgt in staged:
        ys = _experts(xs, tile_expert, meta, w_gate_e, w_up_e, w_down_e, layer, tm=tm_e)
        yk = _sc_gather_rows(ys.reshape(-1, PACK_W), idx)
        gathered.append((yk.reshape(TOP_K, xs.shape[0], t, PACK_W), wgt))

    out = None
    for part, (yk, wgt) in enumerate(gathered):
        out = _combine(x1, mod, seq_len, lw, yk, wgt, tm=tm, tile_off=part * n_tok_tiles, prev=out)
    return out


def _rope_tables(n):
    rows = n // GRID_W
    r, col = jnp.meshgrid(jnp.arange(rows), jnp.arange(GRID_W), indexing="ij")
    r = r.reshape(-1).astype(F32)
    col = col.reshape(-1).astype(F32)
    pairs = QK_ROPE // 4
    inv = ROPE_BASE ** (-jnp.arange(pairs, dtype=F32) / pairs)
    ang = jnp.concatenate([r[:, None] * inv, col[:, None] * inv], axis=-1)
    cos, sin = jnp.cos(ang), jnp.sin(ang)
    pad = HEAD_PAD - QK_NOPE - QK_ROPE
    cos_t = jnp.concatenate([jnp.ones((n, QK_NOPE), F32), cos, cos, jnp.zeros((n, pad), F32)], axis=1)
    sin_t = jnp.concatenate([jnp.zeros((n, QK_NOPE), F32), sin, sin, jnp.zeros((n, pad), F32)], axis=1)
    return cos_t, sin_t


def _identity_tables(n):
    pad = HEAD_PAD - QK_NOPE - QK_ROPE
    cos_t = jnp.concatenate([jnp.ones((n, QK_NOPE + QK_ROPE), F32), jnp.zeros((n, pad), F32)], axis=1)
    return cos_t, jnp.zeros((n, HEAD_PAD), F32)


def _position_dft(n):
    nb = 64 if n % 64 == 0 else 1
    na = n // nb
    m = jnp.arange(n, dtype=jnp.int32)[None, :]
    ang_a = ((jnp.arange(na, dtype=jnp.int32)[:, None] * m) % na).astype(F32) * (2.0 * np.pi / na)
    ang_b = ((jnp.arange(nb, dtype=jnp.int32)[:, None] * m) % n).astype(F32) * (2.0 * np.pi / n)
    ca, sa = jnp.cos(ang_a)[:, None, :], jnp.sin(ang_a)[:, None, :]
    cb, sb = jnp.cos(ang_b)[None, :, :], jnp.sin(ang_b)[None, :, :]
    norm = 1.0 / np.sqrt(n)
    cos = ((ca * cb - sa * sb) * norm).reshape(n, n)
    sin = ((sa * cb + ca * sb) * (-norm)).reshape(n, n)
    return jnp.concatenate([cos, sin], axis=1).astype(BF16)


def _channel_dft(width):
    gc = width // FOURIER_GROUPS
    idx = (jnp.arange(gc, dtype=jnp.int32)[:, None] * jnp.arange(gc, dtype=jnp.int32)[None, :]) % gc
    ang = idx.astype(F32) * (2.0 * np.pi / gc)
    eye = jnp.eye(FOURIER_GROUPS, dtype=F32)
    norm = 1.0 / np.sqrt(gc)
    return jnp.concatenate([jnp.kron(eye, jnp.cos(ang) * norm), jnp.kron(eye, jnp.sin(ang) * norm)],
                           axis=1).astype(BF16)


def _w1_kernel(w_ref, o_ref, *, kv_rank):
    cols = w_ref.shape[2]
    kv_end = kv_rank + QK_ROPE
    half = QK_ROPE // 2
    tail = HEAD_PAD - QK_NOPE - QK_ROPE
    o_kpe = kv_rank
    o_rot = o_kpe + HEAD_PAD
    o_rest = o_rot + HEAD_PAD
    dt = o_ref.dtype
    o_ref[0, 0:kv_rank, :] = w_ref[0, 0:kv_rank, :].astype(dt)
    for base in (o_kpe, o_rot):
        o_ref[0, base:base + QK_NOPE, :] = jnp.zeros((QK_NOPE, cols), dt)
        o_ref[0, base + QK_NOPE + QK_ROPE:base + HEAD_PAD, :] = jnp.zeros((tail, cols), dt)
    o_ref[0, o_kpe + QK_NOPE:o_kpe + QK_NOPE + QK_ROPE, :] = w_ref[0, kv_rank:kv_end, :].astype(dt)
    o_ref[0, o_rot + QK_NOPE:o_rot + QK_NOPE + half, :] = (-w_ref[0, kv_rank + half:kv_end, :]).astype(dt)
    o_ref[0, o_rot + QK_NOPE + half:o_rot + QK_NOPE + QK_ROPE, :] = w_ref[0, kv_rank:kv_rank + half, :].astype(dt)
    o_ref[0, o_rest:, :] = w_ref[0, kv_end:, :].astype(dt)


def _prep_w1(w_in_t, kv_rank):
    n_layers, width, d = w_in_t.shape
    out_w = width - QK_ROPE + 2 * HEAD_PAD
    tc = _tile(d, 256)
    return pl.pallas_call(
        functools.partial(_w1_kernel, kv_rank=kv_rank),
        grid=(n_layers, d // tc),
        in_specs=[pl.BlockSpec((1, width, tc), lambda l, i: (l, 0, i))],
        out_specs=pl.BlockSpec((1, out_w, tc), lambda l, i: (l, 0, i)),
        out_shape=jax.ShapeDtypeStruct((n_layers, out_w, d), BF16),
        compiler_params=_params(2),
        name="prep_w1",
    )(w_in_t)


def _rot_cols(w):
    half = w.shape[-1] // 2
    return jnp.concatenate([-w[..., half:], w[..., :half]], axis=-1)


def _layer_weights(l, g_pre_mix, g_post_mix, g_pre_ffn, g_post_ffn, w1, b_gate, g_q, w_uq, g_kv,
                   w_ukv, w_mla_out, conv_w, w_conv_out, w_four_out, w_out, w_router, b_router,
                   w_gate_s, w_up_s, w_down_s, dc):
    kv_rank = g_kv.shape[1]
    q_rank = g_q.shape[1]
    qk_dim = QK_NOPE + QK_ROPE
    pad = HEAD_PAD - qk_dim
    uq = w_uq[l].reshape(q_rank, N_HEADS, qk_dim) * (qk_dim ** -0.5 * np.log2(np.e))
    zq = jnp.zeros((q_rank, N_HEADS, pad), F32)
    wuq = jnp.concatenate([uq, zq], axis=-1).reshape(q_rank, N_HEADS * HEAD_PAD).astype(BF16)
    wuq_rot = jnp.concatenate([jnp.zeros((q_rank, N_HEADS, QK_NOPE), F32), _rot_cols(uq[..., QK_NOPE:]), zq],
                              axis=-1).reshape(q_rank, N_HEADS * HEAD_PAD).astype(BF16)
    ukv = w_ukv[l].reshape(kv_rank, N_HEADS, QK_NOPE + V_DIM)
    wuk = jnp.concatenate([ukv[..., :QK_NOPE], jnp.zeros((kv_rank, N_HEADS, HEAD_PAD - QK_NOPE), F32)],
                          axis=-1).reshape(kv_rank, N_HEADS * HEAD_PAD).astype(BF16)
    wuv = jnp.concatenate([ukv[..., QK_NOPE:], jnp.zeros((kv_rank, N_HEADS, HEAD_PAD - V_DIM), F32)],
                          axis=-1).reshape(kv_rank, N_HEADS * HEAD_PAD).astype(BF16)
    v_one = jnp.tile((jnp.arange(HEAD_PAD) == V_DIM).astype(F32), N_HEADS)[None]

    wr = w_router[l]
    wr_hi = wr.astype(BF16)
    wr_lo = (wr - wr_hi.astype(F32)).astype(BF16)
    return {
        "g_pre_mix": g_pre_mix[l][None], "g_post_mix": g_post_mix[l][None],
        "g_pre_ffn": g_pre_ffn[l][None], "g_post_ffn": g_post_ffn[l][None],
        "w1": w1, "layer": l, "b_gate": b_gate[l][None], "g_q": g_q[l][None], "g_kv": g_kv[l][None],
        "wuq": wuq, "wuq_rot": wuq_rot, "wuk": wuk, "wuv": wuv, "v_one": v_one, "dc": dc,
        "conv_w": conv_w[l],
        "w_mla_out": w_mla_out[l].astype(BF16), "w_conv_out": w_conv_out[l].astype(BF16),
        "w_four_out": w_four_out[l].astype(BF16), "w_out": w_out[l].astype(BF16),
        "w_router": jnp.concatenate([wr_hi, wr_lo], axis=1), "b_router": b_router[l][None],
        "w_router_t": jnp.concatenate([wr_hi.T, wr_lo.T], axis=0), "b_router_t": b_router[l][:, None],
        "w_gate_s": w_gate_s[l].astype(BF16), "w_up_s": w_up_s[l].astype(BF16),
        "w_down_s": w_down_s[l].astype(BF16),
    }


def _tile(n, pref):
    return pref if n % pref == 0 else n


def kernel(x, c, ctx, c_ctx, w_ada, b_ada, g_pre_mix, g_post_mix, g_pre_ffn, g_post_ffn, w_in, b_gate,
           g_q, w_uq, g_kv, w_ukv, w_mla_out, conv_w, w_conv_out, w_four_out, w_out, w_router, b_router,
           w_gate_e, w_up_e, w_down_e, w_gate_s, w_up_s, w_down_s):
    batch, seq, d = x.shape
    n_ctx = ctx.shape[1]
    n_layers = w_in.shape[0]
    xs = x.reshape(batch * seq, d)
    cs = ctx.reshape(batch * n_ctx, d)

    mod_rows = 16
    c_all = jnp.concatenate([c, c_ctx[None], jnp.zeros((mod_rows - batch - 1, d), F32)], axis=0)
    ada = _ada(c_all, w_ada, b_ada)

    tab_x = _rope_tables(seq)
    tm_c = _tile(n_ctx, 256)
    tab_c = _identity_tables(tm_c)
    cs_x = _position_dft(seq)
    cs_c = _position_dft(n_ctx)
    dc = _channel_dft(w_four_out.shape[1])
    w1 = _prep_w1(jnp.swapaxes(w_in, 1, 2), g_kv.shape[1])

    tm_x = _tile(seq, 512)
    tm_moe_x = _tile(seq, 1024)
    tm_moe_c = _tile(batch * n_ctx, 1024)

    for l in range(n_layers):
        last = l == n_layers - 1
        lw = _layer_weights(l, g_pre_mix, g_post_mix, g_pre_ffn, g_post_ffn, w1, b_gate, g_q, w_uq,
                            g_kv, w_ukv, w_mla_out, conv_w, w_conv_out, w_four_out, w_out, w_router,
                            b_router, w_gate_s, w_up_s, w_down_s, dc)
        mods = ada[l].reshape(mod_rows, 6, d)
        mod_x = mods[:batch]
        mod_c = mods[batch:batch + 1]

        pc = _inproj(cs, mod_c, n_ctx, lw, tab_c, kv_only=last, tm=tm_c)
        px = _inproj(xs, mod_x, seq, lw, tab_x, kv_only=False, tm=tm_x)
        o_x = _attention(px["q"], [(pc["k"], pc["v"], n_ctx), (px["k"], px["v"], seq)], batch, seq,
                         tq=_tile(seq, 2 * ATTN_ROWS))
        f_x = _fourier(px["ab"], cs_x, batch, seq, tn=tm_x)
        x1 = _merge(xs, mod_x, seq, px, o_x, f_x, lw, tm=tm_x)
        xs = _moe_sparse(x1, mod_x, seq, lw, w_gate_e, w_up_e, w_down_e, l, tm=tm_moe_x, tm_e=1024,
                         n_parts=1)
        if not last:
            o_c = _attention(pc["q"], [(pc["k"], pc["v"], n_ctx)], batch, n_ctx, tq=tm_c)
            f_c = _fourier(pc["ab"], cs_c, batch, n_ctx, tn=tm_c)
            c1 = _merge(cs, mod_c, n_ctx, pc, o_c, f_c, lw, tm=tm_c)
            cs = _moe_sparse(c1, mod_c, batch * n_ctx, lw, w_gate_e, w_up_e, w_down_e, l, tm=tm_moe_c,
                             tm_e=512, n_parts=1)
    return xs.reshape(batch, seq, d)
```

```python
import functools

import numpy as np
import jax
import jax.numpy as jnp
from jax import lax
from jax.experimental import pallas as pl
from jax.experimental.pallas import tpu as pltpu
from jax.experimental.pallas import tpu_sc as plsc

N_HEADS = 8
QK_NOPE = 64
QK_ROPE = 32
V_DIM = 64
GRID_W = 64
ROPE_BASE = 10000.0
FOURIER_GROUPS = 4
TOP_K = 4
ROUTED_SCALE = 2.5
N_BRANCHES = 3
EPS = 1e-6

LANE = 128
HEAD_PAD = LANE
VMEM_LIMIT = 56 * 1024 * 1024
PACK_W = 256
SC_WINDOW = 128
ATTN_ROWS = 512
MERGE_ROWS = 512

F32 = jnp.float32
BF16 = jnp.bfloat16


def _rms(x, g):
    return x * lax.rsqrt(jnp.mean(x * x, axis=-1, keepdims=True) + EPS) * g


def _sigmoid(x):
    return 1.0 / (1.0 + jnp.exp(-x))


def _dot(a, b):
    return jnp.dot(a, b, preferred_element_type=F32)


def _dot_t(a, b_t):
    return lax.dot_general(a, b_t, (((1,), (1,)), ((), ())), preferred_element_type=F32)


def _resident(shape):
    nd = len(shape)
    return pl.BlockSpec(shape, lambda *_: (0,) * nd, pipeline_mode=pl.Buffered(1))


def _params(n_grid):
    return pltpu.CompilerParams(dimension_semantics=("arbitrary",) * n_grid,
                                vmem_limit_bytes=VMEM_LIMIT)


def _ada_kernel(c_ref, w_ref, b_ref, o_ref):
    c = c_ref[...]
    a = (c * _sigmoid(c)).astype(BF16)
    o_ref[0] = _dot(a, w_ref[0].astype(BF16)) + b_ref[0]


def _ada(c_all, w_ada, b_ada):
    n_layers, d, n_out = w_ada.shape
    rows = c_all.shape[0]
    tn = 1536
    return pl.pallas_call(
        _ada_kernel,
        grid=(n_layers, n_out // tn),
        in_specs=[
            pl.BlockSpec((rows, d), lambda l, j: (0, 0)),
            pl.BlockSpec((1, d, tn), lambda l, j: (l, 0, j)),
            pl.BlockSpec((1, 1, tn), lambda l, j: (l, 0, j)),
        ],
        out_specs=pl.BlockSpec((1, rows, tn), lambda l, j: (l, 0, j)),
        out_shape=jax.ShapeDtypeStruct((n_layers, rows, n_out), F32),
        compiler_params=_params(2),
        name="ada",
    )(c_all, w_ada, b_ada.reshape(n_layers, 1, n_out))


def _inproj_kernel(*refs, kv_only, kv_rank, q_rank, conv_w, four_w, d_model):
    if kv_only:
        (x_ref, mod_ref, gpre_ref, w1_ref, gkv_ref, wuk_ref, wuv_ref, vone_ref, cos_ref, sin_ref,
         k_ref, v_ref) = refs
    else:
        (x_ref, mod_ref, gpre_ref, w1_ref, gkv_ref, wuk_ref, wuv_ref, vone_ref, cos_ref, sin_ref,
         bg_ref, gq_ref, wuq_ref, dc_ref,
         k_ref, v_ref, q_ref, cb_ref, cc_ref, cu_ref, ab_ref, gate_ref) = refs

    x = x_ref[...]
    shift = mod_ref[0, 0:1, :]
    scale = mod_ref[0, 1:2, :]
    h = (_rms(x, gpre_ref[...]) * (1.0 + scale) + shift).astype(BF16)
    cos = cos_ref[...]
    sin = sin_ref[...]

    o_kpe = kv_rank
    o_rot = o_kpe + HEAD_PAD
    o_q = o_rot + HEAD_PAD
    p = _dot_t(h, w1_ref[0, 0:o_q, :])
    ckv = _rms(p[:, 0:kv_rank], gkv_ref[...]).astype(BF16)
    kpe = p[:, o_kpe:o_rot] * cos + p[:, o_rot:o_q] * sin
    k = _dot(ckv, wuk_ref[...]) + jnp.concatenate([kpe] * N_HEADS, axis=1)
    k_ref[...] = k.astype(k_ref.dtype)
    v_ref[...] = (_dot(ckv, wuv_ref[...]) + vone_ref[...]).astype(v_ref.dtype)
    if kv_only:
        return

    o_cb = o_q + q_rank
    cq = _rms(_dot_t(h, w1_ref[0, o_q:o_cb, :]), gq_ref[...]).astype(BF16)
    half = QK_ROPE // 2
    lane = lax.broadcasted_iota(jnp.int32, (1, HEAD_PAD), 1)
    first = (lane >= QK_NOPE) & (lane < QK_NOPE + half)
    cos_h = jnp.concatenate([cos] * N_HEADS, axis=1)
    sin_h = jnp.concatenate([jnp.where(first, -sin, sin)] * N_HEADS, axis=1)
    first_h = jnp.concatenate([first] * N_HEADS, axis=1)
    lin = _dot(cq, wuq_ref[...])
    width = lin.shape[1]
    partner = jnp.where(first_h, pltpu.roll(lin, width - half, 1), pltpu.roll(lin, half, 1))
    q_ref[...] = (lin * cos_h + partner * sin_h).astype(q_ref.dtype)

    o_cc = o_cb + conv_w
    o_cu = o_cc + conv_w
    o_four = o_cu + conv_w
    cb_ref[...] = _dot_t(h, w1_ref[0, o_cb:o_cc, :]).astype(cb_ref.dtype)
    cc_ref[...] = _dot_t(h, w1_ref[0, o_cc:o_cu, :]).astype(cc_ref.dtype)
    cu_ref[...] = _dot_t(h, w1_ref[0, o_cu:o_four, :]).astype(cu_ref.dtype)

    o_gate = o_four + four_w
    uf = _dot_t(h, w1_ref[0, o_four:o_gate, :]).astype(BF16)
    ab_ref[...] = _dot(uf, dc_ref[...]).astype(ab_ref.dtype)

    for j in range(N_BRANCHES):
        lo = o_gate + j * d_model
        z = _dot_t(h, w1_ref[0, lo:lo + d_model, :]) + bg_ref[:, j * d_model:(j + 1) * d_model]
        gate_ref[:, j * d_model:(j + 1) * d_model] = _sigmoid(z).astype(gate_ref.dtype)


def _inproj(xs, mod, seq_len, lw, tables, *, kv_only, tm):
    t, d = xs.shape
    nb = mod.shape[0]
    tiles_per_seq = seq_len // tm
    cos_t, sin_t = tables
    table_tiles = cos_t.shape[0] // tm
    kv_rank = lw["g_kv"].shape[1]
    q_rank = lw["g_q"].shape[1]
    conv_w = lw["conv_w"].shape[1]
    four_w = lw["dc"].shape[0]
    n_k = N_HEADS * HEAD_PAD

    def row(i):
        return (i, 0)

    def mod_map(i):
        return ((i // tiles_per_seq) % nb, 0, 0)

    def tab_map(i):
        return (i % table_tiles, 0)

    w1, layer = lw["w1"], lw["layer"]
    w1_rows = kv_rank + 2 * HEAD_PAD if kv_only else w1.shape[1]
    in_specs = [
        pl.BlockSpec((tm, d), row),
        pl.BlockSpec((1,) + mod.shape[1:], mod_map),
        _resident((1, d)),
        pl.BlockSpec((1, w1_rows, d), lambda i: (layer, 0, 0), pipeline_mode=pl.Buffered(1)),
        _resident((1, kv_rank)),
        _resident(lw["wuk"].shape),
        _resident(lw["wuv"].shape),
        _resident(lw["v_one"].shape),
        pl.BlockSpec((tm, HEAD_PAD), tab_map),
        pl.BlockSpec((tm, HEAD_PAD), tab_map),
    ]
    args = [xs, mod, lw["g_pre_mix"], w1, lw["g_kv"], lw["wuk"], lw["wuv"], lw["v_one"], cos_t, sin_t]
    out_shape = [jax.ShapeDtypeStruct((t, n_k), BF16), jax.ShapeDtypeStruct((t, n_k), BF16)]
    out_specs = [pl.BlockSpec((tm, n_k), row), pl.BlockSpec((tm, n_k), row)]
    if not kv_only:
        in_specs += [
            _resident(lw["b_gate"].shape),
            _resident((1, q_rank)),
            _resident(lw["wuq"].shape),
            _resident(lw["dc"].shape),
        ]
        args += [lw["b_gate"], lw["g_q"], lw["wuq"], lw["dc"]]
        widths = [n_k, conv_w, conv_w, conv_w, 2 * four_w, N_BRANCHES * d]
        out_shape += [jax.ShapeDtypeStruct((t, w), BF16) for w in widths]
        out_specs += [pl.BlockSpec((tm, w), row) for w in widths]
    outs = pl.pallas_call(
        functools.partial(_inproj_kernel, kv_only=kv_only, kv_rank=kv_rank, q_rank=q_rank,
                          conv_w=conv_w, four_w=four_w, d_model=d),
        grid=(t // tm,),
        in_specs=in_specs,
        out_specs=out_specs,
        out_shape=out_shape,
        compiler_params=_params(1),
        name="inproj_kv" if kv_only else "inproj",
    )(*args)
    names = ["k", "v", "q", "cb", "cc", "cu", "ab", "gate"]
    return dict(zip(names, outs))


def _attn_kernel(*refs, n_seg):
    q_ref = refs[0]
    o_ref = refs[-1]
    chunk = min(q_ref.shape[0], ATTN_ROWS)
    for c in range(q_ref.shape[0] // chunk):
        rows = slice(c * chunk, (c + 1) * chunk)
        outs = []
        for hh in range(N_HEADS):
            head = slice(hh * HEAD_PAD, (hh + 1) * HEAD_PAD)
            qh = q_ref[rows, head]
            s = [_dot_t(qh, refs[1 + 2 * i][:, head]) for i in range(n_seg)]
            m = functools.reduce(jnp.maximum, [jnp.max(si, axis=-1, keepdims=True) for si in s])
            acc = functools.reduce(jnp.add, [
                _dot(jnp.exp2((s[i] - m).astype(BF16)), refs[2 + 2 * i][:, head]) for i in range(n_seg)])
            outs.append(acc[:, 0:V_DIM] / acc[:, V_DIM:V_DIM + 1])
        o_ref[rows, :] = jnp.concatenate(outs, axis=1).astype(o_ref.dtype)


def _attention(q, segs, batch, seq_q, *, tq):
    t = q.shape[0]
    qt = seq_q // tq
    n_k = N_HEADS * HEAD_PAD
    in_specs = [pl.BlockSpec((tq, n_k), lambda b, j: (b * qt + j, 0))]
    args = [q]
    for k, v, m in segs:
        in_specs.append(pl.BlockSpec((m, n_k), lambda b, j: (b, 0)))
        in_specs.append(pl.BlockSpec((m, n_k), lambda b, j: (b, 0)))
        args += [k, v]
    return pl.pallas_call(
        functools.partial(_attn_kernel, n_seg=len(segs)),
        grid=(batch, qt),
        in_specs=in_specs,
        out_specs=pl.BlockSpec((tq, N_HEADS * V_DIM), lambda b, j: (b * qt + j, 0)),
        out_shape=jax.ShapeDtypeStruct((t, N_HEADS * V_DIM), BF16),
        compiler_params=_params(2),
        name="attention",
    )(*args)


def _four_kernel(cs_ref, ab_ref, o_ref, *, n, fw):
    o = _dot(cs_ref[:, 0:n], ab_ref[:, 0:fw]) + _dot(cs_ref[:, n:2 * n], ab_ref[:, fw:2 * fw])
    o_ref[...] = o.astype(o_ref.dtype)


def _fourier(ab, cs, batch, seq_len, *, tn):
    t, fw2 = ab.shape
    fw = fw2 // 2
    nt = seq_len // tn
    return pl.pallas_call(
        functools.partial(_four_kernel, n=seq_len, fw=fw),
        grid=(batch, nt),
        in_specs=[
            pl.BlockSpec((tn, 2 * seq_len), lambda b, j: (j, 0)),
            pl.BlockSpec((seq_len, fw2), lambda b, j: (b, 0)),
        ],
        out_specs=pl.BlockSpec((tn, fw), lambda b, j: (b * nt + j, 0)),
        out_shape=jax.ShapeDtypeStruct((t, fw), BF16),
        compiler_params=_params(2),
        name="fourier",
    )(cs, ab)


def _merge_kernel(x_ref, mod_ref, o_ref, cb_ref, cc_ref, cu_ref, ccp_ref, cup_ref, ccn_ref,
                  cun_ref, f_ref, gate_ref, convw_ref, wmo_ref, wco_ref, wfo_ref, wout_ref,
                  gpost_ref, out_ref, pad_ref, *, tiles_per_seq, tm, d_model):
    i = pl.program_id(0)
    pos = i % tiles_per_seq
    has_prev = (pos > 0).astype(F32)
    has_next = (pos < tiles_per_seq - 1).astype(F32)
    pad_ref[0:8, :] = ccp_ref[...].astype(F32) * cup_ref[...].astype(F32) * has_prev
    pad_ref[8:8 + tm, :] = cc_ref[...].astype(F32) * cu_ref[...].astype(F32)
    pad_ref[8 + tm:16 + tm, :] = ccn_ref[...].astype(F32) * cun_ref[...].astype(F32) * has_next
    d = d_model
    g1 = mod_ref[0, 2:3, :]
    chunk = min(tm, MERGE_ROWS)

    def conv_stage(r0):
        conv = (pad_ref[7 + r0:7 + r0 + chunk, :] * convw_ref[0:1, :]
                + pad_ref[8 + r0:8 + r0 + chunk, :] * convw_ref[1:2, :]
                + pad_ref[9 + r0:9 + r0 + chunk, :] * convw_ref[2:3, :])
        return (cb_ref[r0:r0 + chunk, :].astype(F32) * conv).astype(BF16)

    def branch_stage(r0, conv_in):
        return (_dot(o_ref[r0:r0 + chunk, :], wmo_ref[...]), _dot(conv_in, wco_ref[...]),
                _dot(f_ref[r0:r0 + chunk, :], wfo_ref[...]))

    def gate_stage(r0, ys):
        rows = slice(r0, r0 + chunk)
        return (gate_ref[rows, 0:d].astype(F32) * ys[0] + gate_ref[rows, d:2 * d].astype(F32) * ys[1]
                + gate_ref[rows, 2 * d:3 * d].astype(F32) * ys[2]).astype(BF16)

    def out_stage(r0, merged):
        return _dot(merged, wout_ref[...])

    def tail_stage(r0, y):
        rows = slice(r0, r0 + chunk)
        out_ref[rows, :] = x_ref[rows, :] + g1 * _rms(y, gpost_ref[...])

    stages = (conv_stage, branch_stage, gate_stage, out_stage, tail_stage)
    n_chunks = tm // chunk
    live = [None] * n_chunks
    for step in range(n_chunks + len(stages) - 1):
        for c in range(n_chunks):
            s = step - c
            if 0 <= s < len(stages):
                live[c] = stages[s](c * chunk) if s == 0 else stages[s](c * chunk, live[c])


def _merge(xs, mod, seq_len, pr, o, four, lw, *, tm):
    t, d = xs.shape
    nb = mod.shape[0]
    tiles_per_seq = seq_len // tm
    cw = lw["conv_w"].shape[1]
    fw = four.shape[1]
    hb = tm // 8
    last_hb = t // 8 - 1

    def row(i):
        return (i, 0)

    def prev_map(i):
        return (jnp.maximum(i * hb - 1, 0), 0)

    def next_map(i):
        return (jnp.minimum((i + 1) * hb, last_hb), 0)

    in_specs = [
        pl.BlockSpec((tm, d), row),
        pl.BlockSpec((1,) + mod.shape[1:], lambda i: ((i // tiles_per_seq) % nb, 0, 0)),
        pl.BlockSpec((tm, o.shape[1]), row),
        pl.BlockSpec((tm, cw), row),
        pl.BlockSpec((tm, cw), row),
        pl.BlockSpec((tm, cw), row),
        pl.BlockSpec((8, cw), prev_map),
        pl.BlockSpec((8, cw), prev_map),
        pl.BlockSpec((8, cw), next_map),
        pl.BlockSpec((8, cw), next_map),
        pl.BlockSpec((tm, fw), row),
        pl.BlockSpec((tm, N_BRANCHES * d), row),
        _resident(lw["conv_w"].shape),
        _resident(lw["w_mla_out"].shape),
        _resident(lw["w_conv_out"].shape),
        _resident(lw["w_four_out"].shape),
        _resident(lw["w_out"].shape),
        _resident((1, d)),
    ]
    return pl.pallas_call(
        functools.partial(_merge_kernel, tiles_per_seq=tiles_per_seq, tm=tm, d_model=d),
        grid=(t // tm,),
        in_specs=in_specs,
        out_specs=pl.BlockSpec((tm, d), row),
        out_shape=jax.ShapeDtypeStruct((t, d), F32),
        scratch_shapes=[pltpu.VMEM((tm + 16, cw), F32)],
        compiler_params=_params(1),
        name="merge",
    )(xs, mod, o, pr["cb"], pr["cc"], pr["cu"], pr["cc"], pr["cu"], pr["cc"], pr["cu"], four,
      pr["gate"], lw["conv_w"], lw["w_mla_out"], lw["w_conv_out"], lw["w_four_out"], lw["w_out"],
      lw["g_post_mix"])


def _moe_kernel(x_ref, mod_ref, gpre_ref, gpost_ref, wr_ref, br_ref, wgs_ref, wus_ref, wds_ref,
                wge_ref, wue_ref, wde_ref, out_ref, t_ref, comb_ref, acc_ref, *, n_experts):
    e = pl.program_id(1)

    @pl.when(e == 0)
    def _():
        shift = mod_ref[0, 3:4, :]
        scale = mod_ref[0, 4:5, :]
        t = _rms(x_ref[...], gpre_ref[...]) * (1.0 + scale) + shift
        t_hi = t.astype(BF16)
        t_lo = (t - t_hi.astype(F32)).astype(BF16)
        t_ref[...] = t_hi
        hh = _dot(t_hi, wr_ref[...])
        logits = hh[:, 0:n_experts] + hh[:, n_experts:2 * n_experts] + _dot(t_lo, wr_ref[:, 0:n_experts])
        scores = _sigmoid(logits)
        work = scores + br_ref[...]
        lane = lax.broadcasted_iota(jnp.int32, scores.shape, 1)
        comb = jnp.zeros_like(scores)
        for _ in range(TOP_K):
            best = jnp.max(work, axis=-1, keepdims=True)
            first = jnp.min(jnp.where(work == best, lane, n_experts), axis=-1, keepdims=True)
            hit = lane == first
            comb = jnp.where(hit, scores, comb)
            work = jnp.where(hit, -jnp.inf, work)
        comb_ref[...] = comb / jnp.sum(comb, axis=-1, keepdims=True) * ROUTED_SCALE
        gate = _dot(t_hi, wgs_ref[...])
        act = (gate * _sigmoid(gate) * _dot(t_hi, wus_ref[...])).astype(BF16)
        acc_ref[...] = _dot(act, wds_ref[...])

    t_hi = t_ref[...]
    gate = _dot(t_hi, wge_ref[0, 0].astype(BF16))
    up = _dot(t_hi, wue_ref[0, 0].astype(BF16))
    lane = lax.broadcasted_iota(jnp.int32, comb_ref.shape, 1)
    w_e = jnp.sum(jnp.where(lane == e, comb_ref[...], 0.0), axis=-1, keepdims=True)
    act = (gate * _sigmoid(gate) * up * w_e).astype(BF16)
    acc_ref[...] += _dot(act, wde_ref[0, 0].astype(BF16))

    @pl.when(e == n_experts - 1)
    def _():
        g2 = mod_ref[0, 5:6, :]
        out_ref[...] = x_ref[...] + g2 * _rms(acc_ref[...], gpost_ref[...])


def _moe(xs, mod, seq_len, lw, w_gate_e, w_up_e, w_down_e, layer, *, tm):
    t, d = xs.shape
    nb = mod.shape[0]
    tiles_per_seq = max(seq_len // tm, 1)
    n_experts, _, f = w_gate_e.shape[1:]
    in_specs = [
        pl.BlockSpec((tm, d), lambda i, e: (i, 0)),
        pl.BlockSpec((1,) + mod.shape[1:], lambda i, e: ((i // tiles_per_seq) % nb, 0, 0)),
        _resident((1, d)),
        _resident((1, d)),
        _resident(lw["w_router"].shape),
        _resident((1, n_experts)),
        _resident(lw["w_gate_s"].shape),
        _resident(lw["w_up_s"].shape),
        _resident(lw["w_down_s"].shape),
        pl.BlockSpec((1, 1, d, f), lambda i, e: (layer, e, 0, 0)),
        pl.BlockSpec((1, 1, d, f), lambda i, e: (layer, e, 0, 0)),
        pl.BlockSpec((1, 1, f, d), lambda i, e: (layer, e, 0, 0)),
    ]
    return pl.pallas_call(
        functools.partial(_moe_kernel, n_experts=n_experts),
        grid=(t // tm, n_experts),
        in_specs=in_specs,
        out_specs=pl.BlockSpec((tm, d), lambda i, e: (i, 0)),
        out_shape=jax.ShapeDtypeStruct((t, d), F32),
        scratch_shapes=[pltpu.VMEM((tm, d), BF16), pltpu.VMEM((tm, n_experts), F32),
                        pltpu.VMEM((tm, d), F32)],
        compiler_params=_params(2),
        name="moe",
    )(xs, mod, lw["g_pre_ffn"], lw["g_post_ffn"], lw["w_router"], lw["b_router"], lw["w_gate_s"],
      lw["w_up_s"], lw["w_down_s"], w_gate_e, w_up_e, w_down_e)


def _pack_rows(v):
    bits = lax.bitcast_convert_type(v.astype(BF16).astype(F32), jnp.uint32)
    rows = []
    for j in range(v.shape[1] // (2 * PACK_W)):
        lo = bits[:, (2 * j) * PACK_W:(2 * j + 1) * PACK_W]
        hi = bits[:, (2 * j + 1) * PACK_W:(2 * j + 2) * PACK_W]
        rows.append(lax.bitcast_convert_type((hi & jnp.uint32(0xFFFF0000)) | (lo >> 16), jnp.int32))
    return rows


def _unpack_rows(rows):
    parts = []
    for r in rows:
        u = lax.bitcast_convert_type(r, jnp.uint32)
        parts.append(lax.bitcast_convert_type(u << 16, F32))
        parts.append(lax.bitcast_convert_type(u & jnp.uint32(0xFFFF0000), F32))
    return jnp.concatenate(parts, axis=1)


def _route_kernel(x_ref, mod_ref, gpre_ref, wrt_ref, brt_ref, triu_ref,
                  tp_ref, sel_ref, rank_ref, cnt_ref, wgt_ref, *, n_experts):
    shift = mod_ref[0, 3:4, :]
    scale = mod_ref[0, 4:5, :]
    t = _rms(x_ref[...], gpre_ref[...]) * (1.0 + scale) + shift
    t_hi = t.astype(BF16)
    t_lo = (t - t_hi.astype(F32)).astype(BF16)
    for j, r in enumerate(_pack_rows(t)):
        tp_ref[j] = r
    tm = t.shape[0]
    hh = _dot_t(wrt_ref[...], t_hi)
    logits = hh[0:n_experts] + hh[n_experts:2 * n_experts] + _dot_t(wrt_ref[0:n_experts, :], t_lo)
    scores = _sigmoid(logits)
    work = scores + brt_ref[...]
    row = lax.broadcasted_iota(jnp.int32, scores.shape, 0)
    wide = lax.broadcasted_iota(jnp.int32, (LANE, tm), 0)
    firsts, picked, hits = [], [], []
    for k in range(TOP_K):
        best = jnp.max(work, axis=0, keepdims=True)
        first = jnp.min(jnp.where(work == best, row, n_experts), axis=0, keepdims=True)
        hit = row == first
        firsts.append(first)
        picked.append(jnp.sum(jnp.where(hit, scores, 0.0), axis=0, keepdims=True))
        hits.append(wide == first + k * n_experts)
        work = jnp.where(hit, -jnp.inf, work)
    total = functools.reduce(jnp.add, picked)
    onehot = functools.reduce(jnp.add, [jnp.where(h, 1.0, 0.0) for h in hits])
    earlier = _dot(onehot.astype(BF16), triu_ref[...])
    col = jnp.broadcast_to(jnp.sum(onehot, axis=1, keepdims=True), (LANE, LANE))
    row_c = lax.broadcasted_iota(jnp.int32, (LANE, LANE), 0)
    before = jnp.zeros((LANE, LANE), F32)
    for s in range(1, TOP_K):
        before = before + jnp.where(row_c >= s * n_experts, pltpu.roll(col, s * n_experts, 0), 0.0)
    ahead = earlier + before[:, 0:1]
    row8 = lax.broadcasted_iota(jnp.int32, (8, tm), 0)
    sel = jnp.zeros((8, tm), jnp.int32)
    rank = jnp.zeros((8, tm), F32)
    wgt_t = jnp.zeros((LANE, tm), F32)
    for k in range(TOP_K):
        sel = jnp.where(row8 == k, firsts[k], sel)
        rank = jnp.where(row8 == k, jnp.sum(jnp.where(hits[k], ahead, 0.0), axis=0, keepdims=True), rank)
        wgt_t = jnp.where(wide == k, picked[k] / total * ROUTED_SCALE, wgt_t)
    sel_ref[0] = sel
    rank_ref[0] = rank.astype(jnp.int32)
    cnt_ref[0] = col.astype(jnp.int32)
    wgt_ref[...] = wgt_t.T


def _route(x1, mod, seq_len, lw, *, tm, tile_off, n_tiles):
    d = x1.shape[1]
    t = n_tiles * tm
    nb = mod.shape[0]
    tiles_per_seq = max(seq_len // tm, 1)
    n_experts = lw["b_router"].shape[1]
    rows = d // (2 * PACK_W)
    triu = jnp.tri(tm, tm, -1, dtype=BF16).T
    tile3 = lambda i: (i, 0, 0)
    return pl.pallas_call(
        functools.partial(_route_kernel, n_experts=n_experts),
        grid=(n_tiles,),
        in_specs=[
            pl.BlockSpec((tm, d), lambda i: (i + tile_off, 0)),
            pl.BlockSpec((1,) + mod.shape[1:], lambda i: (((i + tile_off) // tiles_per_seq) % nb, 0, 0)),
            _resident((1, d)),
            _resident(lw["w_router_t"].shape),
            _resident((n_experts, 1)),
            _resident((tm, tm)),
        ],
        out_specs=[
            pl.BlockSpec((rows, tm, PACK_W), lambda i: (0, i, 0)),
            pl.BlockSpec((1, 8, tm), tile3),
            pl.BlockSpec((1, 8, tm), tile3),
            pl.BlockSpec((1, LANE, LANE), tile3),
            pl.BlockSpec((tm, LANE), lambda i: (i, 0)),
        ],
        out_shape=[
            jax.ShapeDtypeStruct((rows, t, PACK_W), jnp.int32),
            jax.ShapeDtypeStruct((n_tiles, 8, tm), jnp.int32),
            jax.ShapeDtypeStruct((n_tiles, 8, tm), jnp.int32),
            jax.ShapeDtypeStruct((n_tiles, LANE, LANE), jnp.int32),
            jax.ShapeDtypeStruct((t, LANE), F32),
        ],
        compiler_params=_params(1),
        name="moe_route",
    )(x1, mod, lw["g_pre_ffn"], lw["w_router_t"], lw["b_router_t"], triu)


def _expert_kernel(te_ref, meta_ref, xs_ref, wg_ref, wu_ref, wd_ref, ys_ref, wg_sc, wu_sc, wd_sc):
    i = pl.program_id(0)
    live = i < meta_ref[0]
    new_expert = (i == 0) | (te_ref[i] != te_ref[jnp.maximum(i - 1, 0)])

    @pl.when(live & new_expert)
    def _():
        wg_sc[...] = wg_ref[0, 0].astype(BF16)
        wu_sc[...] = wu_ref[0, 0].astype(BF16)
        wd_sc[...] = wd_ref[0, 0].astype(BF16)

    @pl.when(live)
    def _():
        x = _unpack_rows([xs_ref[j] for j in range(xs_ref.shape[0])]).astype(BF16)
        gate = _dot(x, wg_sc[...])
        act = (gate * _sigmoid(gate) * _dot(x, wu_sc[...])).astype(BF16)
        for j, r in enumerate(_pack_rows(_dot(act, wd_sc[...]))):
            ys_ref[j] = r


def _experts(xs, tile_expert, meta, w_gate_e, w_up_e, w_down_e, layer, *, tm):
    rows, p, _ = xs.shape
    n_experts, d, f = w_gate_e.shape[1:]

    def slot(i, te, meta):
        return (0, jnp.minimum(i, meta[0] - 1), 0)

    grid_spec = pltpu.PrefetchScalarGridSpec(
        num_scalar_prefetch=2,
        grid=(p // tm,),
        in_specs=[
            pl.BlockSpec((rows, tm, PACK_W), slot),
            pl.BlockSpec((1, 1, d, f), lambda i, te, meta: (layer, te[i], 0, 0)),
            pl.BlockSpec((1, 1, d, f), lambda i, te, meta: (layer, te[i], 0, 0)),
            pl.BlockSpec((1, 1, f, d), lambda i, te, meta: (layer, te[i], 0, 0)),
        ],
        out_specs=pl.BlockSpec((rows, tm, PACK_W), slot),
        scratch_shapes=[pltpu.VMEM((d, f), BF16), pltpu.VMEM((d, f), BF16), pltpu.VMEM((f, d), BF16)],
    )
    return pl.pallas_call(
        _expert_kernel,
        grid_spec=grid_spec,
        out_shape=jax.ShapeDtypeStruct(xs.shape, jnp.int32),
        compiler_params=_params(1),
        name="moe_experts",
    )(tile_expert, meta, xs, w_gate_e, w_up_e, w_down_e)


def _combine_kernel(x_ref, mod_ref, gpre_ref, gpost_ref, wgs_ref, wus_ref, wds_ref, yk_ref, wgt_ref,
                    *rest):
    out_ref = rest[-1]
    shift = mod_ref[0, 3:4, :]
    scale = mod_ref[0, 4:5, :]
    x = x_ref[...]
    t_hi = (_rms(x, gpre_ref[...]) * (1.0 + scale) + shift).astype(BF16)
    gate = _dot(t_hi, wgs_ref[...])
    act = (gate * _sigmoid(gate) * _dot(t_hi, wus_ref[...])).astype(BF16)
    acc = _dot(act, wds_ref[...])
    for k in range(TOP_K):
        y = _unpack_rows([yk_ref[k, j] for j in range(yk_ref.shape[1])])
        acc = acc + wgt_ref[:, k:k + 1] * y
    g2 = mod_ref[0, 5:6, :]
    out_ref[...] = x + g2 * _rms(acc, gpost_ref[...])


def _combine(x1, mod, seq_len, lw, yk, wgt, *, tm, tile_off, prev):
    t, d = x1.shape
    nb = mod.shape[0]
    tiles_per_seq = max(seq_len // tm, 1)
    rows = yk.shape[1]
    n_tiles = yk.shape[2] // tm
    glob = lambda i: (i + tile_off, 0)
    in_specs = [
        pl.BlockSpec((tm, d), glob),
        pl.BlockSpec((1,) + mod.shape[1:], lambda i: (((i + tile_off) // tiles_per_seq) % nb, 0, 0)),
        _resident((1, d)),
        _resident((1, d)),
        _resident(lw["w_gate_s"].shape),
        _resident(lw["w_up_s"].shape),
        _resident(lw["w_down_s"].shape),
        pl.BlockSpec((TOP_K, rows, tm, PACK_W), lambda i: (0, 0, i, 0)),
        pl.BlockSpec((tm, LANE), lambda i: (i, 0)),
    ]
    args = [x1, mod, lw["g_pre_ffn"], lw["g_post_ffn"], lw["w_gate_s"], lw["w_up_s"], lw["w_down_s"], yk, wgt]
    aliases = {}
    if prev is not None:
        in_specs.append(pl.BlockSpec(memory_space=pl.ANY))
        aliases = {len(args): 0}
        args.append(prev)
    return pl.pallas_call(
        _combine_kernel,
        grid=(n_tiles,),
        in_specs=in_specs,
        out_specs=pl.BlockSpec((tm, d), glob),
        out_shape=jax.ShapeDtypeStruct((t, d), F32),
        input_output_aliases=aliases,
        compiler_params=_params(1),
        name="moe_combine",
    )(*args)


def _sc_mesh():
    return plsc.VectorSubcoreMesh(core_axis_name="core", subcore_axis_name="subcore")


def _sc_scatter_rows(src, idx, n_out):
    n_lists, n = idx.shape
    width = src.shape[1]

    @pl.kernel(out_type=jax.ShapeDtypeStruct((n_out, width), src.dtype), mesh=_sc_mesh(), scratch_types=[])
    def scatter(x_hbm, *refs):
        i_hbms, o_hbm = refs[:n_lists], refs[n_lists]

        def body(x_vmem, *i_vmems):
            for i_vmem in i_vmems:
                pltpu.sync_copy(x_vmem, o_hbm.at[i_vmem.at[0]])

        pltpu.emit_pipeline(
            body,
            grid=(n // SC_WINDOW,),
            in_specs=[pl.BlockSpec((SC_WINDOW, width), lambda i: (i, 0))]
            + [pl.BlockSpec((1, SC_WINDOW), lambda i: (0, i))] * n_lists,
            out_specs=[],
            core_axis_name=("core", "subcore"),
            dimension_semantics=(pltpu.PARALLEL,),
        )(x_hbm, *i_hbms)

    return scatter(src, *[idx[r].reshape(1, n) for r in range(n_lists)])


def _sc_gather_rows(table, idx):
    n = idx.shape[0]
    width = table.shape[1]

    @pl.kernel(out_type=jax.ShapeDtypeStruct((n, width), table.dtype), mesh=_sc_mesh(), scratch_types=[])
    def gather(x_hbm, i_hbm, o_hbm):
        def body(i_vmem, o_vmem):
            pltpu.sync_copy(x_hbm.at[i_vmem.at[0]], o_vmem)

        pltpu.emit_pipeline(
            body,
            grid=(n // SC_WINDOW,),
            in_specs=[pl.BlockSpec((1, SC_WINDOW), lambda i: (0, i))],
            out_specs=[pl.BlockSpec((SC_WINDOW, width), lambda i: (i, 0))],
            core_axis_name=("core", "subcore"),
            dimension_semantics=(pltpu.PARALLEL,),
        )(i_hbm, o_hbm)

    return gather(table, idx.reshape(1, n))


def _moe_sparse(x1, mod, seq_len, lw, w_gate_e, w_up_e, w_down_e, layer, *, tm, tm_e, n_parts):
    n_experts = lw["b_router"].shape[1]
    assert n_experts * TOP_K == LANE
    n_tok_tiles = x1.shape[0] // tm // n_parts
    t = n_tok_tiles * tm
    n_row_tiles = (t * TOP_K) // tm_e + n_experts
    p = n_row_tiles * tm_e
    experts = jnp.arange(n_experts, dtype=jnp.int32)

    staged = []
    for part in range(n_parts):
        tp, sel, rank, cnt, wgt = _route(x1, mod, seq_len, lw, tm=tm, tile_off=part * n_tok_tiles,
                                         n_tiles=n_tok_tiles)
        rows = tp.shape[0]
        cnt = cnt[:, :, 0].reshape(n_tok_tiles, TOP_K, n_experts).sum(axis=1)
        padded = (cnt.sum(axis=0) + tm_e - 1) // tm_e * tm_e
        group_end = jnp.cumsum(padded)
        base = (group_end - padded)[None, :] + jnp.cumsum(cnt, axis=0) - cnt
        chosen = sel[:, :TOP_K, :, None] == experts
        pos = jnp.sum(jnp.where(chosen, base[:, None, None, :], 0), axis=-1) + rank[:, :TOP_K, :]
        pos = pos.transpose(1, 0, 2).reshape(TOP_K, t)

        n_used = group_end[-1] // tm_e
        tile_start = jnp.arange(n_row_tiles, dtype=jnp.int32) * tm_e
        tile_expert = jnp.sum(tile_start[:, None] >= group_end[None, :], axis=1).astype(jnp.int32)
        tile_expert = jnp.minimum(tile_expert, n_experts - 1)
        tile_expert = jnp.where(tile_start < group_end[-1], tile_expert, tile_expert[n_used - 1])
        meta = jnp.stack([n_used, n_used]).astype(jnp.int32)

        idx = pos[:, None, :] + (jnp.arange(rows, dtype=jnp.int32) * p)[None, :, None]
        xs = _sc_scatter_rows(tp.reshape(rows * t, PACK_W), idx.reshape(TOP_K, rows * t), rows * p)
        staged.append((xs.reshape(rows, p, PACK_W), tile_expert, meta, idx.reshape(-1), wgt))

    gathered = []
    for xs, tile_expert, meta, idx, wgt in staged:
        ys = _experts(xs, tile_expert, meta, w_gate_e, w_up_e, w_down_e, layer, tm=tm_e)
        yk = _sc_gather_rows(ys.reshape(-1, PACK_W), idx)
        gathered.append((yk.reshape(TOP_K, xs.shape[0], t, PACK_W), wgt))

    out = None
    for part, (yk, wgt) in enumerate(gathered):
        out = _combine(x1, mod, seq_len, lw, yk, wgt, tm=tm, tile_off=part * n_tok_tiles, prev=out)
    return out


def _rope_tables(n):
    rows = n // GRID_W
    r, col = jnp.meshgrid(jnp.arange(rows), jnp.arange(GRID_W), indexing="ij")
    r = r.reshape(-1).astype(F32)
    col = col.reshape(-1).astype(F32)
    pairs = QK_ROPE // 4
    inv = ROPE_BASE ** (-jnp.arange(pairs, dtype=F32) / pairs)
    ang = jnp.concatenate([r[:, None] * inv, col[:, None] * inv], axis=-1)
    cos, sin = jnp.cos(ang), jnp.sin(ang)
    pad = HEAD_PAD - QK_NOPE - QK_ROPE
    cos_t = jnp.concatenate([jnp.ones((n, QK_NOPE), F32), cos, cos, jnp.zeros((n, pad), F32)], axis=1)
    sin_t = jnp.concatenate([jnp.zeros((n, QK_NOPE), F32), sin, sin, jnp.zeros((n, pad), F32)], axis=1)
    return cos_t, sin_t


def _identity_tables(n):
    pad = HEAD_PAD - QK_NOPE - QK_ROPE
    cos_t = jnp.concatenate([jnp.ones((n, QK_NOPE + QK_ROPE), F32), jnp.zeros((n, pad), F32)], axis=1)
    return cos_t, jnp.zeros((n, HEAD_PAD), F32)


def _position_dft(n):
    nb = 64 if n % 64 == 0 else 1
    na = n // nb
    m = jnp.arange(n, dtype=jnp.int32)[None, :]
    ang_a = ((jnp.arange(na, dtype=jnp.int32)[:, None] * m) % na).astype(F32) * (2.0 * np.pi / na)
    ang_b = ((jnp.arange(nb, dtype=jnp.int32)[:, None] * m) % n).astype(F32) * (2.0 * np.pi / n)
    ca, sa = jnp.cos(ang_a)[:, None, :], jnp.sin(ang_a)[:, None, :]
    cb, sb = jnp.cos(ang_b)[None, :, :], jnp.sin(ang_b)[None, :, :]
    norm = 1.0 / np.sqrt(n)
    cos = ((ca * cb - sa * sb) * norm).reshape(n, n)
    sin = ((sa * cb + ca * sb) * (-norm)).reshape(n, n)
    return jnp.concatenate([cos, sin], axis=1).astype(BF16)


def _channel_dft(width):
    gc = width // FOURIER_GROUPS
    idx = (jnp.arange(gc, dtype=jnp.int32)[:, None] * jnp.arange(gc, dtype=jnp.int32)[None, :]) % gc
    ang = idx.astype(F32) * (2.0 * np.pi / gc)
    eye = jnp.eye(FOURIER_GROUPS, dtype=F32)
    norm = 1.0 / np.sqrt(gc)
    return jnp.concatenate([jnp.kron(eye, jnp.cos(ang) * norm), jnp.kron(eye, jnp.sin(ang) * norm)],
                           axis=1).astype(BF16)


def _w1_kernel(w_ref, o_ref, *, kv_rank):
    cols = w_ref.shape[2]
    kv_end = kv_rank + QK_ROPE
    half = QK_ROPE // 2
    tail = HEAD_PAD - QK_NOPE - QK_ROPE
    o_kpe = kv_rank
    o_rot = o_kpe + HEAD_PAD
    o_rest = o_rot + HEAD_PAD
    dt = o_ref.dtype
    o_ref[0, 0:kv_rank, :] = w_ref[0, 0:kv_rank, :].astype(dt)
    for base in (o_kpe, o_rot):
        o_ref[0, base:base + QK_NOPE, :] = jnp.zeros((QK_NOPE, cols), dt)
        o_ref[0, base + QK_NOPE + QK_ROPE:base + HEAD_PAD, :] = jnp.zeros((tail, cols), dt)
    o_ref[0, o_kpe + QK_NOPE:o_kpe + QK_NOPE + QK_ROPE, :] = w_ref[0, kv_rank:kv_end, :].astype(dt)
    o_ref[0, o_rot + QK_NOPE:o_rot + QK_NOPE + half, :] = (-w_ref[0, kv_rank + half:kv_end, :]).astype(dt)
    o_ref[0, o_rot + QK_NOPE + half:o_rot + QK_NOPE + QK_ROPE, :] = w_ref[0, kv_rank:kv_rank + half, :].astype(dt)
    o_ref[0, o_rest:, :] = w_ref[0, kv_end:, :].astype(dt)


def _prep_w1(w_in_t, kv_rank):
    n_layers, width, d = w_in_t.shape
    out_w = width - QK_ROPE + 2 * HEAD_PAD
    tc = _tile(d, 256)
    return pl.pallas_call(
        functools.partial(_w1_kernel, kv_rank=kv_rank),
        grid=(n_layers, d // tc),
        in_specs=[pl.BlockSpec((1, width, tc), lambda l, i: (l, 0, i))],
        out_specs=pl.BlockSpec((1, out_w, tc), lambda l, i: (l, 0, i)),
        out_shape=jax.ShapeDtypeStruct((n_layers, out_w, d), BF16),
        compiler_params=_params(2),
        name="prep_w1",
    )(w_in_t)


def _layer_weights(l, g_pre_mix, g_post_mix, g_pre_ffn, g_post_ffn, w1, b_gate, g_q, w_uq, g_kv,
                   w_ukv, w_mla_out, conv_w, w_conv_out, w_four_out, w_out, w_router, b_router,
                   w_gate_s, w_up_s, w_down_s, dc):
    kv_rank = g_kv.shape[1]
    q_rank = g_q.shape[1]
    qk_dim = QK_NOPE + QK_ROPE
    pad = HEAD_PAD - qk_dim
    uq = w_uq[l].reshape(q_rank, N_HEADS, qk_dim) * (qk_dim ** -0.5 * np.log2(np.e))
    zq = jnp.zeros((q_rank, N_HEADS, pad), F32)
    wuq = jnp.concatenate([uq, zq], axis=-1).reshape(q_rank, N_HEADS * HEAD_PAD).astype(BF16)
    ukv = w_ukv[l].reshape(kv_rank, N_HEADS, QK_NOPE + V_DIM)
    wuk = jnp.concatenate([ukv[..., :QK_NOPE], jnp.zeros((kv_rank, N_HEADS, HEAD_PAD - QK_NOPE), F32)],
                          axis=-1).reshape(kv_rank, N_HEADS * HEAD_PAD).astype(BF16)
    wuv = jnp.concatenate([ukv[..., QK_NOPE:], jnp.zeros((kv_rank, N_HEADS, HEAD_PAD - V_DIM), F32)],
                          axis=-1).reshape(kv_rank, N_HEADS * HEAD_PAD).astype(BF16)
    v_one = jnp.tile((jnp.arange(HEAD_PAD) == V_DIM).astype(F32), N_HEADS)[None]

    wr = w_router[l]
    wr_hi = wr.astype(BF16)
    wr_lo = (wr - wr_hi.astype(F32)).astype(BF16)
    return {
        "g_pre_mix": g_pre_mix[l][None], "g_post_mix": g_post_mix[l][None],
        "g_pre_ffn": g_pre_ffn[l][None], "g_post_ffn": g_post_ffn[l][None],
        "w1": w1, "layer": l, "b_gate": b_gate[l][None], "g_q": g_q[l][None], "g_kv": g_kv[l][None],
        "wuq": wuq, "wuk": wuk, "wuv": wuv, "v_one": v_one, "dc": dc,
        "conv_w": conv_w[l],
        "w_mla_out": w_mla_out[l].astype(BF16), "w_conv_out": w_conv_out[l].astype(BF16),
        "w_four_out": w_four_out[l].astype(BF16), "w_out": w_out[l].astype(BF16),
        "w_router": jnp.concatenate([wr_hi, wr_lo], axis=1), "b_router": b_router[l][None],
        "w_router_t": jnp.concatenate([wr_hi.T, wr_lo.T], axis=0), "b_router_t": b_router[l][:, None],
        "w_gate_s": w_gate_s[l].astype(BF16), "w_up_s": w_up_s[l].astype(BF16),
        "w_down_s": w_down_s[l].astype(BF16),
    }


def _tile(n, pref):
    return pref if n % pref == 0 else n


def kernel(x, c, ctx, c_ctx, w_ada, b_ada, g_pre_mix, g_post_mix, g_pre_ffn, g_post_ffn, w_in, b_gate,
           g_q, w_uq, g_kv, w_ukv, w_mla_out, conv_w, w_conv_out, w_four_out, w_out, w_router, b_router,
           w_gate_e, w_up_e, w_down_e, w_gate_s, w_up_s, w_down_s):
    batch, seq, d = x.shape
    n_ctx = ctx.shape[1]
    n_layers = w_in.shape[0]
    xs = x.reshape(batch * seq, d)
    cs = ctx.reshape(batch * n_ctx, d)

    mod_rows = 16
    c_all = jnp.concatenate([c, c_ctx[None], jnp.zeros((mod_rows - batch - 1, d), F32)], axis=0)
    ada = _ada(c_all, w_ada, b_ada)

    tab_x = _rope_tables(seq)
    tm_c = _tile(n_ctx, 256)
    tab_c = _identity_tables(tm_c)
    cs_x = _position_dft(seq)
    cs_c = _position_dft(n_ctx)
    dc = _channel_dft(w_four_out.shape[1])
    w1 = _prep_w1(jnp.swapaxes(w_in, 1, 2), g_kv.shape[1])

    tm_x = _tile(seq, 512)
    tm_moe_x = _tile(seq, 1024)
    tm_moe_c = _tile(batch * n_ctx, 1024)

    for l in range(n_layers):
        last = l == n_layers - 1
        lw = _layer_weights(l, g_pre_mix, g_post_mix, g_pre_ffn, g_post_ffn, w1, b_gate, g_q, w_uq,
                            g_kv, w_ukv, w_mla_out, conv_w, w_conv_out, w_four_out, w_out, w_router,
                            b_router, w_gate_s, w_up_s, w_down_s, dc)
        mods = ada[l].reshape(mod_rows, 6, d)
        mod_x = mods[:batch]
        mod_c = mods[batch:batch + 1]

        pc = _inproj(cs, mod_c, n_ctx, lw, tab_c, kv_only=last, tm=tm_c)
        px = _inproj(xs, mod_x, seq, lw, tab_x, kv_only=False, tm=tm_x)
        o_x = _attention(px["q"], [(pc["k"], pc["v"], n_ctx), (px["k"], px["v"], seq)], batch, seq,
                         tq=_tile(seq, 2 * ATTN_ROWS))
        f_x = _fourier(px["ab"], cs_x, batch, seq, tn=tm_x)
        x1 = _merge(xs, mod_x, seq, px, o_x, f_x, lw, tm=_tile(seq, 2 * MERGE_ROWS))
        xs = _moe_sparse(x1, mod_x, seq, lw, w_gate_e, w_up_e, w_down_e, l, tm=tm_moe_x, tm_e=1024,
                         n_parts=1)
        if not last:
            o_c = _attention(pc["q"], [(pc["k"], pc["v"], n_ctx)], batch, n_ctx, tq=tm_c)
            f_c = _fourier(pc["ab"], cs_c, batch, n_ctx, tn=tm_c)
            c1 = _merge(cs, mod_c, n_ctx, pc, o_c, f_c, lw, tm=tm_c)
            cs = _moe_sparse(c1, mod_c, batch * n_ctx, lw, w_gate_e, w_up_e, w_down_e, l, tm=tm_moe_c,
                             tm_e=512, n_parts=1)
    return xs.reshape(batch, seq, d)
```

```python
import functools

import numpy as np
import jax
import jax.numpy as jnp
from jax import lax
from jax.experimental import pallas as pl
from jax.experimental.pallas import tpu as pltpu
from jax.experimental.pallas import tpu_sc as plsc

N_HEADS = 8
QK_NOPE = 64
QK_ROPE = 32
V_DIM = 64
GRID_W = 64
ROPE_BASE = 10000.0
FOURIER_GROUPS = 4
TOP_K = 4
ROUTED_SCALE = 2.5
N_BRANCHES = 3
EPS = 1e-6

LANE = 128
HEAD_PAD = LANE
VMEM_LIMIT = 56 * 1024 * 1024
PACK_W = 256
SC_WINDOW = 128
ATTN_ROWS = 512
MERGE_ROWS = 512

F32 = jnp.float32
BF16 = jnp.bfloat16


def _rms(x, g):
    return x * lax.rsqrt(jnp.mean(x * x, axis=-1, keepdims=True) + EPS) * g


def _sigmoid(x):
    return 1.0 / (1.0 + jnp.exp(-x))


def _dot(a, b):
    return jnp.dot(a, b, preferred_element_type=F32)


def _dot_t(a, b_t):
    return lax.dot_general(a, b_t, (((1,), (1,)), ((), ())), preferred_element_type=F32)


def _resident(shape):
    nd = len(shape)
    return pl.BlockSpec(shape, lambda *_: (0,) * nd, pipeline_mode=pl.Buffered(1))


def _params(n_grid):
    return pltpu.CompilerParams(dimension_semantics=("arbitrary",) * n_grid,
                                vmem_limit_bytes=VMEM_LIMIT)


def _ada_kernel(c_ref, w_ref, b_ref, o_ref):
    c = c_ref[...]
    a = (c * _sigmoid(c)).astype(BF16)
    o_ref[0] = _dot(a, w_ref[0].astype(BF16)) + b_ref[0]


def _ada(c_all, w_ada, b_ada):
    n_layers, d, n_out = w_ada.shape
    rows = c_all.shape[0]
    tn = 1536
    return pl.pallas_call(
        _ada_kernel,
        grid=(n_layers, n_out // tn),
        in_specs=[
            pl.BlockSpec((rows, d), lambda l, j: (0, 0)),
            pl.BlockSpec((1, d, tn), lambda l, j: (l, 0, j)),
            pl.BlockSpec((1, 1, tn), lambda l, j: (l, 0, j)),
        ],
        out_specs=pl.BlockSpec((1, rows, tn), lambda l, j: (l, 0, j)),
        out_shape=jax.ShapeDtypeStruct((n_layers, rows, n_out), F32),
        compiler_params=_params(2),
        name="ada",
    )(c_all, w_ada, b_ada.reshape(n_layers, 1, n_out))


def _inproj_kernel(*refs, kv_only, kv_rank, q_rank, conv_w, four_w, d_model):
    if kv_only:
        (x_ref, mod_ref, gpre_ref, w1_ref, gkv_ref, wuk_ref, wuv_ref, vone_ref, cos_ref, sin_ref,
         k_ref, v_ref) = refs
    else:
        (x_ref, mod_ref, gpre_ref, w1_ref, gkv_ref, wuk_ref, wuv_ref, vone_ref, cos_ref, sin_ref,
         bg_ref, gq_ref, wuq_ref, dc_ref,
         k_ref, v_ref, q_ref, cb_ref, cc_ref, cu_ref, ab_ref, gate_ref) = refs

    x = x_ref[...]
    shift = mod_ref[0, 0:1, :]
    scale = mod_ref[0, 1:2, :]
    h = (_rms(x, gpre_ref[...]) * (1.0 + scale) + shift).astype(BF16)
    cos = cos_ref[...]
    sin = sin_ref[...]

    o_kpe = kv_rank
    o_rot = o_kpe + HEAD_PAD
    o_q = o_rot + HEAD_PAD
    p = _dot_t(h, w1_ref[0, 0:o_q, :])
    ckv = _rms(p[:, 0:kv_rank], gkv_ref[...]).astype(BF16)
    kpe = p[:, o_kpe:o_rot] * cos + p[:, o_rot:o_q] * sin
    k = _dot(ckv, wuk_ref[...]) + jnp.concatenate([kpe] * N_HEADS, axis=1)
    k_ref[...] = k.astype(k_ref.dtype)
    v_ref[...] = (_dot(ckv, wuv_ref[...]) + vone_ref[...]).astype(v_ref.dtype)
    if kv_only:
        return

    o_cb = o_q + q_rank
    cq = _rms(_dot_t(h, w1_ref[0, o_q:o_cb, :]), gq_ref[...]).astype(BF16)
    half = QK_ROPE // 2
    lane = lax.broadcasted_iota(jnp.int32, (1, HEAD_PAD), 1)
    first = (lane >= QK_NOPE) & (lane < QK_NOPE + half)
    cos_h = jnp.concatenate([cos] * N_HEADS, axis=1)
    sin_h = jnp.concatenate([jnp.where(first, -sin, sin)] * N_HEADS, axis=1)
    first_h = jnp.concatenate([first] * N_HEADS, axis=1)
    lin = _dot(cq, wuq_ref[...])
    width = lin.shape[1]
    partner = jnp.where(first_h, pltpu.roll(lin, width - half, 1), pltpu.roll(lin, half, 1))
    q_ref[...] = (lin * cos_h + partner * sin_h).astype(q_ref.dtype)

    o_cc = o_cb + conv_w
    o_cu = o_cc + conv_w
    o_four = o_cu + conv_w
    cb_ref[...] = _dot_t(h, w1_ref[0, o_cb:o_cc, :]).astype(cb_ref.dtype)
    cc_ref[...] = _dot_t(h, w1_ref[0, o_cc:o_cu, :]).astype(cc_ref.dtype)
    cu_ref[...] = _dot_t(h, w1_ref[0, o_cu:o_four, :]).astype(cu_ref.dtype)

    o_gate = o_four + four_w
    uf = _dot_t(h, w1_ref[0, o_four:o_gate, :]).astype(BF16)
    ab_ref[...] = _dot(uf, dc_ref[...]).astype(ab_ref.dtype)

    for j in range(N_BRANCHES):
        lo = o_gate + j * d_model
        z = _dot_t(h, w1_ref[0, lo:lo + d_model, :]) + bg_ref[:, j * d_model:(j + 1) * d_model]
        gate_ref[:, j * d_model:(j + 1) * d_model] = _sigmoid(z).astype(gate_ref.dtype)


def _inproj(xs, mod, seq_len, lw, tables, *, kv_only, tm):
    t, d = xs.shape
    nb = mod.shape[0]
    cos_t, sin_t = tables
    table_tiles = cos_t.shape[0] // tm
    kv_rank = lw["g_kv"].shape[1]
    q_rank = lw["g_q"].shape[1]
    conv_w = lw["conv_w"].shape[1]
    four_w = lw["dc"].shape[0]
    n_k = N_HEADS * HEAD_PAD

    def row(i):
        return (i, 0)

    def mod_map(i):
        return ((i * tm // seq_len) % nb, 0, 0)

    def tab_map(i):
        return (i % table_tiles, 0)

    w1, layer = lw["w1"], lw["layer"]
    w1_rows = kv_rank + 2 * HEAD_PAD if kv_only else w1.shape[1]
    in_specs = [
        pl.BlockSpec((tm, d), row),
        pl.BlockSpec((1,) + mod.shape[1:], mod_map),
        _resident((1, d)),
        pl.BlockSpec((1, w1_rows, d), lambda i: (layer, 0, 0), pipeline_mode=pl.Buffered(1)),
        _resident((1, kv_rank)),
        _resident(lw["wuk"].shape),
        _resident(lw["wuv"].shape),
        _resident(lw["v_one"].shape),
        pl.BlockSpec((tm, HEAD_PAD), tab_map),
        pl.BlockSpec((tm, HEAD_PAD), tab_map),
    ]
    args = [xs, mod, lw["g_pre_mix"], w1, lw["g_kv"], lw["wuk"], lw["wuv"], lw["v_one"], cos_t, sin_t]
    out_shape = [jax.ShapeDtypeStruct((t, n_k), BF16), jax.ShapeDtypeStruct((t, n_k), BF16)]
    out_specs = [pl.BlockSpec((tm, n_k), row), pl.BlockSpec((tm, n_k), row)]
    if not kv_only:
        in_specs += [
            _resident(lw["b_gate"].shape),
            _resident((1, q_rank)),
            _resident(lw["wuq"].shape),
            _resident(lw["dc"].shape),
        ]
        args += [lw["b_gate"], lw["g_q"], lw["wuq"], lw["dc"]]
        widths = [n_k, conv_w, conv_w, conv_w, 2 * four_w, N_BRANCHES * d]
        out_shape += [jax.ShapeDtypeStruct((t, w), BF16) for w in widths]
        out_specs += [pl.BlockSpec((tm, w), row) for w in widths]
    outs = pl.pallas_call(
        functools.partial(_inproj_kernel, kv_only=kv_only, kv_rank=kv_rank, q_rank=q_rank,
                          conv_w=conv_w, four_w=four_w, d_model=d),
        grid=(t // tm,),
        in_specs=in_specs,
        out_specs=out_specs,
        out_shape=out_shape,
        compiler_params=_params(1),
        name="inproj_kv" if kv_only else "inproj",
    )(*args)
    names = ["k", "v", "q", "cb", "cc", "cu", "ab", "gate"]
    return dict(zip(names, outs))


def _attn_kernel(*refs, n_seg):
    q_ref = refs[0]
    o_ref = refs[-1]
    chunk = min(q_ref.shape[0], ATTN_ROWS)
    for c in range(q_ref.shape[0] // chunk):
        rows = slice(c * chunk, (c + 1) * chunk)
        outs = []
        for hh in range(N_HEADS):
            head = slice(hh * HEAD_PAD, (hh + 1) * HEAD_PAD)
            qh = q_ref[rows, head]
            s = [_dot_t(qh, refs[1 + 2 * i][:, head]) for i in range(n_seg)]
            m = functools.reduce(jnp.maximum, [jnp.max(si, axis=-1, keepdims=True) for si in s])
            acc = functools.reduce(jnp.add, [
                _dot(jnp.exp2((s[i] - m).astype(BF16)), refs[2 + 2 * i][:, head]) for i in range(n_seg)])
            outs.append(acc[:, 0:V_DIM] / acc[:, V_DIM:V_DIM + 1])
        o_ref[rows, :] = jnp.concatenate(outs, axis=1).astype(o_ref.dtype)


def _attention(q, segs, batch, seq_q, *, tq):
    t = q.shape[0]
    qt = seq_q // tq
    n_k = N_HEADS * HEAD_PAD
    in_specs = [pl.BlockSpec((tq, n_k), lambda b, j: (b * qt + j, 0))]
    args = [q]
    for k, v, m in segs:
        in_specs.append(pl.BlockSpec((m, n_k), lambda b, j: (b, 0)))
        in_specs.append(pl.BlockSpec((m, n_k), lambda b, j: (b, 0)))
        args += [k, v]
    return pl.pallas_call(
        functools.partial(_attn_kernel, n_seg=len(segs)),
        grid=(batch, qt),
        in_specs=in_specs,
        out_specs=pl.BlockSpec((tq, N_HEADS * V_DIM), lambda b, j: (b * qt + j, 0)),
        out_shape=jax.ShapeDtypeStruct((t, N_HEADS * V_DIM), BF16),
        compiler_params=_params(2),
        name="attention",
    )(*args)


def _four_kernel(cs_ref, ab_ref, o_ref, *, n, fw):
    o = _dot(cs_ref[:, 0:n], ab_ref[:, 0:fw]) + _dot(cs_ref[:, n:2 * n], ab_ref[:, fw:2 * fw])
    o_ref[...] = o.astype(o_ref.dtype)


def _fourier(ab, cs, batch, seq_len, *, tn):
    t, fw2 = ab.shape
    fw = fw2 // 2
    nt = seq_len // tn
    cs_spec = (_resident(cs.shape) if nt == 1
               else pl.BlockSpec((tn, 2 * seq_len), lambda b, j: (j, 0)))
    return pl.pallas_call(
        functools.partial(_four_kernel, n=seq_len, fw=fw),
        grid=(batch, nt),
        in_specs=[
            cs_spec,
            pl.BlockSpec((seq_len, fw2), lambda b, j: (b, 0)),
        ],
        out_specs=pl.BlockSpec((tn, fw), lambda b, j: (b * nt + j, 0)),
        out_shape=jax.ShapeDtypeStruct((t, fw), BF16),
        compiler_params=_params(2),
        name="fourier",
    )(cs, ab)


def _merge_kernel(x_ref, mod_ref, o_ref, cb_ref, cc_ref, cu_ref, ccp_ref, cup_ref, ccn_ref,
                  cun_ref, f_ref, gate_ref, convw_ref, wmo_ref, wco_ref, wfo_ref, wout_ref,
                  gpost_ref, out_ref, pad_ref, *, seq_len, tm, d_model):
    i = pl.program_id(0)
    has_prev = ((i * tm) % seq_len != 0).astype(F32)
    has_next = (((i + 1) * tm) % seq_len != 0).astype(F32)
    pad_ref[0:8, :] = ccp_ref[...].astype(F32) * cup_ref[...].astype(F32) * has_prev
    pad_ref[8:8 + tm, :] = cc_ref[...].astype(F32) * cu_ref[...].astype(F32)
    pad_ref[8 + tm:16 + tm, :] = ccn_ref[...].astype(F32) * cun_ref[...].astype(F32) * has_next
    d = d_model
    g1 = mod_ref[0, 2:3, :]
    chunk = min(tm, MERGE_ROWS)

    def conv_stage(r0):
        before = pad_ref[7 + r0:7 + r0 + chunk, :]
        after = pad_ref[9 + r0:9 + r0 + chunk, :]
        if tm > seq_len:
            at = (lax.broadcasted_iota(jnp.int32, (chunk, 1), 0) + r0) % seq_len
            before = jnp.where(at == 0, 0.0, before)
            after = jnp.where(at == seq_len - 1, 0.0, after)
        conv = (before * convw_ref[0:1, :] + pad_ref[8 + r0:8 + r0 + chunk, :] * convw_ref[1:2, :]
                + after * convw_ref[2:3, :])
        return (cb_ref[r0:r0 + chunk, :].astype(F32) * conv).astype(BF16)

    def branch_stage(r0, conv_in):
        return (_dot(o_ref[r0:r0 + chunk, :], wmo_ref[...]), _dot(conv_in, wco_ref[...]),
                _dot(f_ref[r0:r0 + chunk, :], wfo_ref[...]))

    def gate_stage(r0, ys):
        rows = slice(r0, r0 + chunk)
        return (gate_ref[rows, 0:d].astype(F32) * ys[0] + gate_ref[rows, d:2 * d].astype(F32) * ys[1]
                + gate_ref[rows, 2 * d:3 * d].astype(F32) * ys[2]).astype(BF16)

    def out_stage(r0, merged):
        return _dot(merged, wout_ref[...])

    def tail_stage(r0, y):
        rows = slice(r0, r0 + chunk)
        out_ref[rows, :] = x_ref[rows, :] + g1 * _rms(y, gpost_ref[...])

    stages = (conv_stage, branch_stage, gate_stage, out_stage, tail_stage)
    n_chunks = tm // chunk
    live = [None] * n_chunks
    for step in range(n_chunks + len(stages) - 1):
        for c in range(n_chunks):
            s = step - c
            if 0 <= s < len(stages):
                live[c] = stages[s](c * chunk) if s == 0 else stages[s](c * chunk, live[c])


def _merge(xs, mod, seq_len, pr, o, four, lw, *, tm):
    t, d = xs.shape
    nb = mod.shape[0]
    assert tm % seq_len == 0 or seq_len % tm == 0
    cw = lw["conv_w"].shape[1]
    fw = four.shape[1]
    hb = tm // 8
    last_hb = t // 8 - 1

    def row(i):
        return (i, 0)

    def prev_map(i):
        return (jnp.maximum(i * hb - 1, 0), 0)

    def next_map(i):
        return (jnp.minimum((i + 1) * hb, last_hb), 0)

    in_specs = [
        pl.BlockSpec((tm, d), row),
        pl.BlockSpec((1,) + mod.shape[1:], lambda i: ((i * tm // seq_len) % nb, 0, 0)),
        pl.BlockSpec((tm, o.shape[1]), row),
        pl.BlockSpec((tm, cw), row),
        pl.BlockSpec((tm, cw), row),
        pl.BlockSpec((tm, cw), row),
        pl.BlockSpec((8, cw), prev_map),
        pl.BlockSpec((8, cw), prev_map),
        pl.BlockSpec((8, cw), next_map),
        pl.BlockSpec((8, cw), next_map),
        pl.BlockSpec((tm, fw), row),
        pl.BlockSpec((tm, N_BRANCHES * d), row),
        _resident(lw["conv_w"].shape),
        _resident(lw["w_mla_out"].shape),
        _resident(lw["w_conv_out"].shape),
        _resident(lw["w_four_out"].shape),
        _resident(lw["w_out"].shape),
        _resident((1, d)),
    ]
    return pl.pallas_call(
        functools.partial(_merge_kernel, seq_len=seq_len, tm=tm, d_model=d),
        grid=(t // tm,),
        in_specs=in_specs,
        out_specs=pl.BlockSpec((tm, d), row),
        out_shape=jax.ShapeDtypeStruct((t, d), F32),
        scratch_shapes=[pltpu.VMEM((tm + 16, cw), F32)],
        compiler_params=_params(1),
        name="merge",
    )(xs, mod, o, pr["cb"], pr["cc"], pr["cu"], pr["cc"], pr["cu"], pr["cc"], pr["cu"], four,
      pr["gate"], lw["conv_w"], lw["w_mla_out"], lw["w_conv_out"], lw["w_four_out"], lw["w_out"],
      lw["g_post_mix"])


def _moe_kernel(x_ref, mod_ref, gpre_ref, gpost_ref, wr_ref, br_ref, wgs_ref, wus_ref, wds_ref,
                wge_ref, wue_ref, wde_ref, out_ref, t_ref, comb_ref, acc_ref, *, n_experts):
    e = pl.program_id(1)

    @pl.when(e == 0)
    def _():
        shift = mod_ref[0, 3:4, :]
        scale = mod_ref[0, 4:5, :]
        t = _rms(x_ref[...], gpre_ref[...]) * (1.0 + scale) + shift
        t_hi = t.astype(BF16)
        t_lo = (t - t_hi.astype(F32)).astype(BF16)
        t_ref[...] = t_hi
        hh = _dot(t_hi, wr_ref[...])
        logits = hh[:, 0:n_experts] + hh[:, n_experts:2 * n_experts] + _dot(t_lo, wr_ref[:, 0:n_experts])
        scores = _sigmoid(logits)
        work = scores + br_ref[...]
        lane = lax.broadcasted_iota(jnp.int32, scores.shape, 1)
        comb = jnp.zeros_like(scores)
        for _ in range(TOP_K):
            best = jnp.max(work, axis=-1, keepdims=True)
            first = jnp.min(jnp.where(work == best, lane, n_experts), axis=-1, keepdims=True)
            hit = lane == first
            comb = jnp.where(hit, scores, comb)
            work = jnp.where(hit, -jnp.inf, work)
        comb_ref[...] = comb / jnp.sum(comb, axis=-1, keepdims=True) * ROUTED_SCALE
        gate = _dot(t_hi, wgs_ref[...])
        act = (gate * _sigmoid(gate) * _dot(t_hi, wus_ref[...])).astype(BF16)
        acc_ref[...] = _dot(act, wds_ref[...])

    t_hi = t_ref[...]
    gate = _dot(t_hi, wge_ref[0, 0].astype(BF16))
    up = _dot(t_hi, wue_ref[0, 0].astype(BF16))
    lane = lax.broadcasted_iota(jnp.int32, comb_ref.shape, 1)
    w_e = jnp.sum(jnp.where(lane == e, comb_ref[...], 0.0), axis=-1, keepdims=True)
    act = (gate * _sigmoid(gate) * up * w_e).astype(BF16)
    acc_ref[...] += _dot(act, wde_ref[0, 0].astype(BF16))

    @pl.when(e == n_experts - 1)
    def _():
        g2 = mod_ref[0, 5:6, :]
        out_ref[...] = x_ref[...] + g2 * _rms(acc_ref[...], gpost_ref[...])


def _moe(xs, mod, seq_len, lw, w_gate_e, w_up_e, w_down_e, layer, *, tm):
    t, d = xs.shape
    nb = mod.shape[0]
    tiles_per_seq = max(seq_len // tm, 1)
    n_experts, _, f = w_gate_e.shape[1:]
    in_specs = [
        pl.BlockSpec((tm, d), lambda i, e: (i, 0)),
        pl.BlockSpec((1,) + mod.shape[1:], lambda i, e: ((i // tiles_per_seq) % nb, 0, 0)),
        _resident((1, d)),
        _resident((1, d)),
        _resident(lw["w_router"].shape),
        _resident((1, n_experts)),
        _resident(lw["w_gate_s"].shape),
        _resident(lw["w_up_s"].shape),
        _resident(lw["w_down_s"].shape),
        pl.BlockSpec((1, 1, d, f), lambda i, e: (layer, e, 0, 0)),
        pl.BlockSpec((1, 1, d, f), lambda i, e: (layer, e, 0, 0)),
        pl.BlockSpec((1, 1, f, d), lambda i, e: (layer, e, 0, 0)),
    ]
    return pl.pallas_call(
        functools.partial(_moe_kernel, n_experts=n_experts),
        grid=(t // tm, n_experts),
        in_specs=in_specs,
        out_specs=pl.BlockSpec((tm, d), lambda i, e: (i, 0)),
        out_shape=jax.ShapeDtypeStruct((t, d), F32),
        scratch_shapes=[pltpu.VMEM((tm, d), BF16), pltpu.VMEM((tm, n_experts), F32),
                        pltpu.VMEM((tm, d), F32)],
        compiler_params=_params(2),
        name="moe",
    )(xs, mod, lw["g_pre_ffn"], lw["g_post_ffn"], lw["w_router"], lw["b_router"], lw["w_gate_s"],
      lw["w_up_s"], lw["w_down_s"], w_gate_e, w_up_e, w_down_e)


def _pack_rows(v):
    bits = lax.bitcast_convert_type(v.astype(BF16).astype(F32), jnp.uint32)
    rows = []
    for j in range(v.shape[1] // (2 * PACK_W)):
        lo = bits[:, (2 * j) * PACK_W:(2 * j + 1) * PACK_W]
        hi = bits[:, (2 * j + 1) * PACK_W:(2 * j + 2) * PACK_W]
        rows.append(lax.bitcast_convert_type((hi & jnp.uint32(0xFFFF0000)) | (lo >> 16), jnp.int32))
    return rows


def _unpack_rows(rows):
    parts = []
    for r in rows:
        u = lax.bitcast_convert_type(r, jnp.uint32)
        parts.append(lax.bitcast_convert_type(u << 16, F32))
        parts.append(lax.bitcast_convert_type(u & jnp.uint32(0xFFFF0000), F32))
    return jnp.concatenate(parts, axis=1)


def _route_kernel(x_ref, mod_ref, gpre_ref, wrt_ref, brt_ref, triu_ref,
                  tp_ref, sel_ref, rank_ref, cnt_ref, wgt_ref, *, n_experts):
    shift = mod_ref[0, 3:4, :]
    scale = mod_ref[0, 4:5, :]
    t = _rms(x_ref[...], gpre_ref[...]) * (1.0 + scale) + shift
    t_hi = t.astype(BF16)
    t_lo = (t - t_hi.astype(F32)).astype(BF16)
    for j, r in enumerate(_pack_rows(t)):
        tp_ref[j] = r
    tm = t.shape[0]
    hh = _dot_t(wrt_ref[...], t_hi)
    logits = hh[0:n_experts] + hh[n_experts:2 * n_experts] + _dot_t(wrt_ref[0:n_experts, :], t_lo)
    scores = _sigmoid(logits)
    work = scores + brt_ref[...]
    row = lax.broadcasted_iota(jnp.int32, scores.shape, 0)
    wide = lax.broadcasted_iota(jnp.int32, (LANE, tm), 0)
    firsts, picked, hits = [], [], []
    for k in range(TOP_K):
        best = jnp.max(work, axis=0, keepdims=True)
        first = jnp.min(jnp.where(work == best, row, n_experts), axis=0, keepdims=True)
        hit = row == first
        firsts.append(first)
        picked.append(jnp.sum(jnp.where(hit, scores, 0.0), axis=0, keepdims=True))
        hits.append(wide == first + k * n_experts)
        work = jnp.where(hit, -jnp.inf, work)
    total = functools.reduce(jnp.add, picked)
    onehot = functools.reduce(jnp.add, [jnp.where(h, 1.0, 0.0) for h in hits])
    earlier = _dot(onehot.astype(BF16), triu_ref[...])
    col = jnp.broadcast_to(jnp.sum(onehot, axis=1, keepdims=True), (LANE, LANE))
    row_c = lax.broadcasted_iota(jnp.int32, (LANE, LANE), 0)
    before = jnp.zeros((LANE, LANE), F32)
    for s in range(1, TOP_K):
        before = before + jnp.where(row_c >= s * n_experts, pltpu.roll(col, s * n_experts, 0), 0.0)
    ahead = earlier + before[:, 0:1]
    row8 = lax.broadcasted_iota(jnp.int32, (8, tm), 0)
    sel = jnp.zeros((8, tm), jnp.int32)
    rank = jnp.zeros((8, tm), F32)
    wgt_t = jnp.zeros((LANE, tm), F32)
    for k in range(TOP_K):
        sel = jnp.where(row8 == k, firsts[k], sel)
        rank = jnp.where(row8 == k, jnp.sum(jnp.where(hits[k], ahead, 0.0), axis=0, keepdims=True), rank)
        wgt_t = jnp.where(wide == k, picked[k] / total * ROUTED_SCALE, wgt_t)
    sel_ref[0] = sel
    rank_ref[0] = rank.astype(jnp.int32)
    cnt_ref[0] = col.astype(jnp.int32)
    wgt_ref[...] = wgt_t.T


def _route(x1, mod, seq_len, lw, *, tm, tile_off, n_tiles):
    d = x1.shape[1]
    t = n_tiles * tm
    nb = mod.shape[0]
    tiles_per_seq = max(seq_len // tm, 1)
    n_experts = lw["b_router"].shape[1]
    rows = d // (2 * PACK_W)
    triu = jnp.tri(tm, tm, -1, dtype=BF16).T
    tile3 = lambda i: (i, 0, 0)
    return pl.pallas_call(
        functools.partial(_route_kernel, n_experts=n_experts),
        grid=(n_tiles,),
        in_specs=[
            pl.BlockSpec((tm, d), lambda i: (i + tile_off, 0)),
            pl.BlockSpec((1,) + mod.shape[1:], lambda i: (((i + tile_off) // tiles_per_seq) % nb, 0, 0)),
            _resident((1, d)),
            _resident(lw["w_router_t"].shape),
            _resident((n_experts, 1)),
            _resident((tm, tm)),
        ],
        out_specs=[
            pl.BlockSpec((rows, tm, PACK_W), lambda i: (0, i, 0)),
            pl.BlockSpec((1, 8, tm), tile3),
            pl.BlockSpec((1, 8, tm), tile3),
            pl.BlockSpec((1, LANE, LANE), tile3),
            pl.BlockSpec((tm, LANE), lambda i: (i, 0)),
        ],
        out_shape=[
            jax.ShapeDtypeStruct((rows, t, PACK_W), jnp.int32),
            jax.ShapeDtypeStruct((n_tiles, 8, tm), jnp.int32),
            jax.ShapeDtypeStruct((n_tiles, 8, tm), jnp.int32),
            jax.ShapeDtypeStruct((n_tiles, LANE, LANE), jnp.int32),
            jax.ShapeDtypeStruct((t, LANE), F32),
        ],
        compiler_params=_params(1),
        name="moe_route",
    )(x1, mod, lw["g_pre_ffn"], lw["w_router_t"], lw["b_router_t"], triu)


def _expert_kernel(te_ref, meta_ref, xs_ref, wg_ref, wu_ref, wd_ref, ys_ref, wg_sc, wu_sc, wd_sc):
    i = pl.program_id(0)
    live = i < meta_ref[0]
    new_expert = (i == 0) | (te_ref[i] != te_ref[jnp.maximum(i - 1, 0)])

    @pl.when(live & new_expert)
    def _():
        wg_sc[...] = wg_ref[0, 0].astype(BF16)
        wu_sc[...] = wu_ref[0, 0].astype(BF16)
        wd_sc[...] = wd_ref[0, 0].astype(BF16)

    @pl.when(live)
    def _():
        x = _unpack_rows([xs_ref[j] for j in range(xs_ref.shape[0])]).astype(BF16)
        gate = _dot(x, wg_sc[...])
        act = (gate * _sigmoid(gate) * _dot(x, wu_sc[...])).astype(BF16)
        for j, r in enumerate(_pack_rows(_dot(act, wd_sc[...]))):
            ys_ref[j] = r


def _experts(xs, tile_expert, meta, w_gate_e, w_up_e, w_down_e, layer, *, tm):
    rows, p, _ = xs.shape
    n_experts, d, f = w_gate_e.shape[1:]

    def slot(i, te, meta):
        return (0, jnp.minimum(i, meta[0] - 1), 0)

    grid_spec = pltpu.PrefetchScalarGridSpec(
        num_scalar_prefetch=2,
        grid=(p // tm,),
        in_specs=[
            pl.BlockSpec((rows, tm, PACK_W), slot),
            pl.BlockSpec((1, 1, d, f), lambda i, te, meta: (layer, te[i], 0, 0)),
            pl.BlockSpec((1, 1, d, f), lambda i, te, meta: (layer, te[i], 0, 0)),
            pl.BlockSpec((1, 1, f, d), lambda i, te, meta: (layer, te[i], 0, 0)),
        ],
        out_specs=pl.BlockSpec((rows, tm, PACK_W), slot),
        scratch_shapes=[pltpu.VMEM((d, f), BF16), pltpu.VMEM((d, f), BF16), pltpu.VMEM((f, d), BF16)],
    )
    return pl.pallas_call(
        _expert_kernel,
        grid_spec=grid_spec,
        out_shape=jax.ShapeDtypeStruct(xs.shape, jnp.int32),
        compiler_params=_params(1),
        name="moe_experts",
    )(tile_expert, meta, xs, w_gate_e, w_up_e, w_down_e)


def _combine_kernel(x_ref, mod_ref, gpre_ref, gpost_ref, wgs_ref, wus_ref, wds_ref, yk_ref, wgt_ref,
                    *rest):
    out_ref = rest[-1]
    shift = mod_ref[0, 3:4, :]
    scale = mod_ref[0, 4:5, :]
    x = x_ref[...]
    t_hi = (_rms(x, gpre_ref[...]) * (1.0 + scale) + shift).astype(BF16)
    gate = _dot(t_hi, wgs_ref[...])
    act = (gate * _sigmoid(gate) * _dot(t_hi, wus_ref[...])).astype(BF16)
    acc = _dot(act, wds_ref[...])
    for k in range(TOP_K):
        y = _unpack_rows([yk_ref[k, j] for j in range(yk_ref.shape[1])])
        acc = acc + wgt_ref[:, k:k + 1] * y
    g2 = mod_ref[0, 5:6, :]
    out_ref[...] = x + g2 * _rms(acc, gpost_ref[...])


def _combine(x1, mod, seq_len, lw, yk, wgt, *, tm, tile_off, prev):
    t, d = x1.shape
    nb = mod.shape[0]
    tiles_per_seq = max(seq_len // tm, 1)
    rows = yk.shape[1]
    n_tiles = yk.shape[2] // tm
    glob = lambda i: (i + tile_off, 0)
    in_specs = [
        pl.BlockSpec((tm, d), glob),
        pl.BlockSpec((1,) + mod.shape[1:], lambda i: (((i + tile_off) // tiles_per_seq) % nb, 0, 0)),
        _resident((1, d)),
        _resident((1, d)),
        _resident(lw["w_gate_s"].shape),
        _resident(lw["w_up_s"].shape),
        _resident(lw["w_down_s"].shape),
        pl.BlockSpec((TOP_K, rows, tm, PACK_W), lambda i: (0, 0, i, 0)),
        pl.BlockSpec((tm, LANE), lambda i: (i, 0)),
    ]
    args = [x1, mod, lw["g_pre_ffn"], lw["g_post_ffn"], lw["w_gate_s"], lw["w_up_s"], lw["w_down_s"], yk, wgt]
    aliases = {}
    if prev is not None:
        in_specs.append(pl.BlockSpec(memory_space=pl.ANY))
        aliases = {len(args): 0}
        args.append(prev)
    return pl.pallas_call(
        _combine_kernel,
        grid=(n_tiles,),
        in_specs=in_specs,
        out_specs=pl.BlockSpec((tm, d), glob),
        out_shape=jax.ShapeDtypeStruct((t, d), F32),
        input_output_aliases=aliases,
        compiler_params=_params(1),
        name="moe_combine",
    )(*args)


def _sc_mesh():
    return plsc.VectorSubcoreMesh(core_axis_name="core", subcore_axis_name="subcore")


def _sc_scatter_rows(src, idx, n_out):
    n_lists, n = idx.shape
    width = src.shape[1]

    @pl.kernel(out_type=jax.ShapeDtypeStruct((n_out, width), src.dtype), mesh=_sc_mesh(), scratch_types=[])
    def scatter(x_hbm, *refs):
        i_hbms, o_hbm = refs[:n_lists], refs[n_lists]

        def body(x_vmem, *i_vmems):
            for i_vmem in i_vmems:
                pltpu.sync_copy(x_vmem, o_hbm.at[i_vmem.at[0]])

        pltpu.emit_pipeline(
            body,
            grid=(n // SC_WINDOW,),
            in_specs=[pl.BlockSpec((SC_WINDOW, width), lambda i: (i, 0))]
            + [pl.BlockSpec((1, SC_WINDOW), lambda i: (0, i))] * n_lists,
            out_specs=[],
            core_axis_name=("core", "subcore"),
            dimension_semantics=(pltpu.PARALLEL,),
        )(x_hbm, *i_hbms)

    return scatter(src, *[idx[r].reshape(1, n) for r in range(n_lists)])


def _sc_gather_rows(table, idx):
    n = idx.shape[0]
    width = table.shape[1]

    @pl.kernel(out_type=jax.ShapeDtypeStruct((n, width), table.dtype), mesh=_sc_mesh(), scratch_types=[])
    def gather(x_hbm, i_hbm, o_hbm):
        def body(i_vmem, o_vmem):
            pltpu.sync_copy(x_hbm.at[i_vmem.at[0]], o_vmem)

        pltpu.emit_pipeline(
            body,
            grid=(n // SC_WINDOW,),
            in_specs=[pl.BlockSpec((1, SC_WINDOW), lambda i: (0, i))],
            out_specs=[pl.BlockSpec((SC_WINDOW, width), lambda i: (i, 0))],
            core_axis_name=("core", "subcore"),
            dimension_semantics=(pltpu.PARALLEL,),
        )(i_hbm, o_hbm)

    return gather(table, idx.reshape(1, n))


def _moe_sparse(x1, mod, seq_len, lw, w_gate_e, w_up_e, w_down_e, layer, *, tm, tm_e, n_parts):
    n_experts = lw["b_router"].shape[1]
    assert n_experts * TOP_K == LANE
    n_tok_tiles = x1.shape[0] // tm // n_parts
    t = n_tok_tiles * tm
    n_row_tiles = (t * TOP_K) // tm_e + n_experts
    p = n_row_tiles * tm_e
    experts = jnp.arange(n_experts, dtype=jnp.int32)

    staged = []
    for part in range(n_parts):
        tp, sel, rank, cnt, wgt = _route(x1, mod, seq_len, lw, tm=tm, tile_off=part * n_tok_tiles,
                                         n_tiles=n_tok_tiles)
        rows = tp.shape[0]
        cnt = cnt[:, :, 0].reshape(n_tok_tiles, TOP_K, n_experts).sum(axis=1)
        padded = (cnt.sum(axis=0) + tm_e - 1) // tm_e * tm_e
        group_end = jnp.cumsum(padded)
        base = (group_end - padded)[None, :] + jnp.cumsum(cnt, axis=0) - cnt
        chosen = sel[:, :TOP_K, :, None] == experts
        pos = jnp.sum(jnp.where(chosen, base[:, None, None, :], 0), axis=-1) + rank[:, :TOP_K, :]
        pos = pos.transpose(1, 0, 2).reshape(TOP_K, t)

        n_used = group_end[-1] // tm_e
        tile_start = jnp.arange(n_row_tiles, dtype=jnp.int32) * tm_e
        tile_expert = jnp.sum(tile_start[:, None] >= group_end[None, :], axis=1).astype(jnp.int32)
        tile_expert = jnp.minimum(tile_expert, n_experts - 1)
        tile_expert = jnp.where(tile_start < group_end[-1], tile_expert, tile_expert[n_used - 1])
        meta = jnp.stack([n_used, n_used]).astype(jnp.int32)

        idx = pos[:, None, :] + (jnp.arange(rows, dtype=jnp.int32) * p)[None, :, None]
        xs = _sc_scatter_rows(tp.reshape(rows * t, PACK_W), idx.reshape(TOP_K, rows * t), rows * p)
        staged.append((xs.reshape(rows, p, PACK_W), tile_expert, meta, idx.reshape(-1), wgt))

    gathered = []
    for xs, tile_expert, meta, idx, wgt in staged:
        ys = _experts(xs, tile_expert, meta, w_gate_e, w_up_e, w_down_e, layer, tm=tm_e)
        yk = _sc_gather_rows(ys.reshape(-1, PACK_W), idx)
        gathered.append((yk.reshape(TOP_K, xs.shape[0], t, PACK_W), wgt))

    out = None
    for part, (yk, wgt) in enumerate(gathered):
        out = _combine(x1, mod, seq_len, lw, yk, wgt, tm=tm, tile_off=part * n_tok_tiles, prev=out)
    return out


def _rope_tables(n):
    rows = n // GRID_W
    r, col = jnp.meshgrid(jnp.arange(rows), jnp.arange(GRID_W), indexing="ij")
    r = r.reshape(-1).astype(F32)
    col = col.reshape(-1).astype(F32)
    pairs = QK_ROPE // 4
    inv = ROPE_BASE ** (-jnp.arange(pairs, dtype=F32) / pairs)
    ang = jnp.concatenate([r[:, None] * inv, col[:, None] * inv], axis=-1)
    cos, sin = jnp.cos(ang), jnp.sin(ang)
    pad = HEAD_PAD - QK_NOPE - QK_ROPE
    cos_t = jnp.concatenate([jnp.ones((n, QK_NOPE), F32), cos, cos, jnp.zeros((n, pad), F32)], axis=1)
    sin_t = jnp.concatenate([jnp.zeros((n, QK_NOPE), F32), sin, sin, jnp.zeros((n, pad), F32)], axis=1)
    return cos_t, sin_t


def _identity_tables(n):
    pad = HEAD_PAD - QK_NOPE - QK_ROPE
    cos_t = jnp.concatenate([jnp.ones((n, QK_NOPE + QK_ROPE), F32), jnp.zeros((n, pad), F32)], axis=1)
    return cos_t, jnp.zeros((n, HEAD_PAD), F32)


def _position_dft(n):
    nb = 64 if n % 64 == 0 else 1
    na = n // nb
    m = jnp.arange(n, dtype=jnp.int32)[None, :]
    ang_a = ((jnp.arange(na, dtype=jnp.int32)[:, None] * m) % na).astype(F32) * (2.0 * np.pi / na)
    ang_b = ((jnp.arange(nb, dtype=jnp.int32)[:, None] * m) % n).astype(F32) * (2.0 * np.pi / n)
    ca, sa = jnp.cos(ang_a)[:, None, :], jnp.sin(ang_a)[:, None, :]
    cb, sb = jnp.cos(ang_b)[None, :, :], jnp.sin(ang_b)[None, :, :]
    norm = 1.0 / np.sqrt(n)
    cos = ((ca * cb - sa * sb) * norm).reshape(n, n)
    sin = ((sa * cb + ca * sb) * (-norm)).reshape(n, n)
    return jnp.concatenate([cos, sin], axis=1).astype(BF16)


def _channel_dft(width):
    gc = width // FOURIER_GROUPS
    idx = (jnp.arange(gc, dtype=jnp.int32)[:, None] * jnp.arange(gc, dtype=jnp.int32)[None, :]) % gc
    ang = idx.astype(F32) * (2.0 * np.pi / gc)
    eye = jnp.eye(FOURIER_GROUPS, dtype=F32)
    norm = 1.0 / np.sqrt(gc)
    return jnp.concatenate([jnp.kron(eye, jnp.cos(ang) * norm), jnp.kron(eye, jnp.sin(ang) * norm)],
                           axis=1).astype(BF16)


def _w1_kernel(w_ref, o_ref, *, kv_rank):
    cols = w_ref.shape[2]
    kv_end = kv_rank + QK_ROPE
    half = QK_ROPE // 2
    tail = HEAD_PAD - QK_NOPE - QK_ROPE
    o_kpe = kv_rank
    o_rot = o_kpe + HEAD_PAD
    o_rest = o_rot + HEAD_PAD
    dt = o_ref.dtype
    o_ref[0, 0:kv_rank, :] = w_ref[0, 0:kv_rank, :].astype(dt)
    for base in (o_kpe, o_rot):
        o_ref[0, base:base + QK_NOPE, :] = jnp.zeros((QK_NOPE, cols), dt)
        o_ref[0, base + QK_NOPE + QK_ROPE:base + HEAD_PAD, :] = jnp.zeros((tail, cols), dt)
    o_ref[0, o_kpe + QK_NOPE:o_kpe + QK_NOPE + QK_ROPE, :] = w_ref[0, kv_rank:kv_end, :].astype(dt)
    o_ref[0, o_rot + QK_NOPE:o_rot + QK_NOPE + half, :] = (-w_ref[0, kv_rank + half:kv_end, :]).astype(dt)
    o_ref[0, o_rot + QK_NOPE + half:o_rot + QK_NOPE + QK_ROPE, :] = w_ref[0, kv_rank:kv_rank + half, :].astype(dt)
    o_ref[0, o_rest:, :] = w_ref[0, kv_end:, :].astype(dt)


def _prep_w1(w_in_t, kv_rank):
    n_layers, width, d = w_in_t.shape
    out_w = width - QK_ROPE + 2 * HEAD_PAD
    tc = _tile(d, 256)
    return pl.pallas_call(
        functools.partial(_w1_kernel, kv_rank=kv_rank),
        grid=(n_layers, d // tc),
        in_specs=[pl.BlockSpec((1, width, tc), lambda l, i: (l, 0, i))],
        out_specs=pl.BlockSpec((1, out_w, tc), lambda l, i: (l, 0, i)),
        out_shape=jax.ShapeDtypeStruct((n_layers, out_w, d), BF16),
        compiler_params=_params(2),
        name="prep_w1",
    )(w_in_t)


def _layer_weights(l, g_pre_mix, g_post_mix, g_pre_ffn, g_post_ffn, w1, b_gate, g_q, w_uq, g_kv,
                   w_ukv, w_mla_out, conv_w, w_conv_out, w_four_out, w_out, w_router, b_router,
                   w_gate_s, w_up_s, w_down_s, dc):
    kv_rank = g_kv.shape[1]
    q_rank = g_q.shape[1]
    qk_dim = QK_NOPE + QK_ROPE
    pad = HEAD_PAD - qk_dim
    uq = w_uq[l].reshape(q_rank, N_HEADS, qk_dim) * (qk_dim ** -0.5 * np.log2(np.e))
    zq = jnp.zeros((q_rank, N_HEADS, pad), F32)
    wuq = jnp.concatenate([uq, zq], axis=-1).reshape(q_rank, N_HEADS * HEAD_PAD).astype(BF16)
    ukv = w_ukv[l].reshape(kv_rank, N_HEADS, QK_NOPE + V_DIM)
    wuk = jnp.concatenate([ukv[..., :QK_NOPE], jnp.zeros((kv_rank, N_HEADS, HEAD_PAD - QK_NOPE), F32)],
                          axis=-1).reshape(kv_rank, N_HEADS * HEAD_PAD).astype(BF16)
    wuv = jnp.concatenate([ukv[..., QK_NOPE:], jnp.zeros((kv_rank, N_HEADS, HEAD_PAD - V_DIM), F32)],
                          axis=-1).reshape(kv_rank, N_HEADS * HEAD_PAD).astype(BF16)
    v_one = jnp.tile((jnp.arange(HEAD_PAD) == V_DIM).astype(F32), N_HEADS)[None]

    wr = w_router[l]
    wr_hi = wr.astype(BF16)
    wr_lo = (wr - wr_hi.astype(F32)).astype(BF16)
    return {
        "g_pre_mix": g_pre_mix[l][None], "g_post_mix": g_post_mix[l][None],
        "g_pre_ffn": g_pre_ffn[l][None], "g_post_ffn": g_post_ffn[l][None],
        "w1": w1, "layer": l, "b_gate": b_gate[l][None], "g_q": g_q[l][None], "g_kv": g_kv[l][None],
        "wuq": wuq, "wuk": wuk, "wuv": wuv, "v_one": v_one, "dc": dc,
        "conv_w": conv_w[l],
        "w_mla_out": w_mla_out[l].astype(BF16), "w_conv_out": w_conv_out[l].astype(BF16),
        "w_four_out": w_four_out[l].astype(BF16), "w_out": w_out[l].astype(BF16),
        "w_router": jnp.concatenate([wr_hi, wr_lo], axis=1), "b_router": b_router[l][None],
        "w_router_t": jnp.concatenate([wr_hi.T, wr_lo.T], axis=0), "b_router_t": b_router[l][:, None],
        "w_gate_s": w_gate_s[l].astype(BF16), "w_up_s": w_up_s[l].astype(BF16),
        "w_down_s": w_down_s[l].astype(BF16),
    }


def _tile(n, pref):
    return pref if n % pref == 0 else n


def kernel(x, c, ctx, c_ctx, w_ada, b_ada, g_pre_mix, g_post_mix, g_pre_ffn, g_post_ffn, w_in, b_gate,
           g_q, w_uq, g_kv, w_ukv, w_mla_out, conv_w, w_conv_out, w_four_out, w_out, w_router, b_router,
           w_gate_e, w_up_e, w_down_e, w_gate_s, w_up_s, w_down_s):
    batch, seq, d = x.shape
    n_ctx = ctx.shape[1]
    n_layers = w_in.shape[0]
    xs = x.reshape(batch * seq, d)
    cs = ctx.reshape(batch * n_ctx, d)

    mod_rows = 16
    c_all = jnp.concatenate([c, c_ctx[None], jnp.zeros((mod_rows - batch - 1, d), F32)], axis=0)
    ada = _ada(c_all, w_ada, b_ada)

    tab_x = _rope_tables(seq)
    tm_c = _tile(n_ctx, 256)
    tm_c_wide = _tile(batch * n_ctx, 512)
    tab_c = _identity_tables(tm_c_wide)
    cs_x = _position_dft(seq)
    cs_c = _position_dft(n_ctx)
    dc = _channel_dft(w_four_out.shape[1])
    w1 = _prep_w1(jnp.swapaxes(w_in, 1, 2), g_kv.shape[1])

    tm_x = _tile(seq, 512)
    tm_moe_x = _tile(seq, 1024)
    tm_moe_c = _tile(batch * n_ctx, 1024)

    for l in range(n_layers):
        last = l == n_layers - 1
        lw = _layer_weights(l, g_pre_mix, g_post_mix, g_pre_ffn, g_post_ffn, w1, b_gate, g_q, w_uq,
                            g_kv, w_ukv, w_mla_out, conv_w, w_conv_out, w_four_out, w_out, w_router,
                            b_router, w_gate_s, w_up_s, w_down_s, dc)
        mods = ada[l].reshape(mod_rows, 6, d)
        mod_x = mods[:batch]
        mod_c = mods[batch:batch + 1]

        pc = _inproj(cs, mod_c, n_ctx, lw, tab_c, kv_only=last, tm=tm_c_wide)
        px = _inproj(xs, mod_x, seq, lw, tab_x, kv_only=False, tm=tm_x)
        o_x = _attention(px["q"], [(pc["k"], pc["v"], n_ctx), (px["k"], px["v"], seq)], batch, seq,
                         tq=_tile(seq, 2 * ATTN_ROWS))
        f_x = _fourier(px["ab"], cs_x, batch, seq, tn=seq)
        x1 = _merge(xs, mod_x, seq, px, o_x, f_x, lw, tm=_tile(seq, 2 * MERGE_ROWS))
        xs = _moe_sparse(x1, mod_x, seq, lw, w_gate_e, w_up_e, w_down_e, l, tm=tm_moe_x, tm_e=1024,
                         n_parts=1)
        if not last:
            o_c = _attention(pc["q"], [(pc["k"], pc["v"], n_ctx)], batch, n_ctx, tq=tm_c)
            f_c = _fourier(pc["ab"], cs_c, batch, n_ctx, tn=tm_c)
            c1 = _merge(cs, mod_c, n_ctx, pc, o_c, f_c, lw, tm=tm_moe_c)
            cs = _moe_sparse(c1, mod_c, batch * n_ctx, lw, w_gate_e, w_up_e, w_down_e, l, tm=tm_moe_c,
                             tm_e=512, n_parts=1)
    return xs.reshape(batch, seq, d)
```

```python
import functools

import numpy as np
import jax
import jax.numpy as jnp
from jax import lax
from jax.experimental import pallas as pl
from jax.experimental.pallas import tpu as pltpu
from jax.experimental.pallas import tpu_sc as plsc

N_HEADS = 8
QK_NOPE = 64
QK_ROPE = 32
V_DIM = 64
GRID_W = 64
ROPE_BASE = 10000.0
FOURIER_GROUPS = 4
TOP_K = 4
ROUTED_SCALE = 2.5
N_BRANCHES = 3
EPS = 1e-6

LANE = 128
HEAD_PAD = LANE
VMEM_LIMIT = 56 * 1024 * 1024
PACK_W = 256
SC_WINDOW = 128
ATTN_ROWS = 512
MERGE_ROWS = 512

F32 = jnp.float32
BF16 = jnp.bfloat16


def _rms(x, g):
    return x * lax.rsqrt(jnp.mean(x * x, axis=-1, keepdims=True) + EPS) * g


def _sigmoid(x):
    return 1.0 / (1.0 + jnp.exp(-x))


def _dot(a, b):
    return jnp.dot(a, b, preferred_element_type=F32)


def _dot_t(a, b_t):
    return lax.dot_general(a, b_t, (((1,), (1,)), ((), ())), preferred_element_type=F32)


def _resident(shape):
    nd = len(shape)
    return pl.BlockSpec(shape, lambda *_: (0,) * nd, pipeline_mode=pl.Buffered(1))


def _params(n_grid):
    return pltpu.CompilerParams(dimension_semantics=("arbitrary",) * n_grid,
                                vmem_limit_bytes=VMEM_LIMIT)


def _ada_kernel(c_ref, w_ref, b_ref, o_ref):
    c = c_ref[...]
    a = (c * _sigmoid(c)).astype(BF16)
    o_ref[0] = _dot(a, w_ref[0].astype(BF16)) + b_ref[0]


def _ada(c_all, w_ada, b_ada):
    n_layers, d, n_out = w_ada.shape
    rows = c_all.shape[0]
    tn = 1536
    return pl.pallas_call(
        _ada_kernel,
        grid=(n_layers, n_out // tn),
        in_specs=[
            pl.BlockSpec((rows, d), lambda l, j: (0, 0)),
            pl.BlockSpec((1, d, tn), lambda l, j: (l, 0, j)),
            pl.BlockSpec((1, 1, tn), lambda l, j: (l, 0, j)),
        ],
        out_specs=pl.BlockSpec((1, rows, tn), lambda l, j: (l, 0, j)),
        out_shape=jax.ShapeDtypeStruct((n_layers, rows, n_out), F32),
        compiler_params=_params(2),
        name="ada",
    )(c_all, w_ada, b_ada.reshape(n_layers, 1, n_out))


def _inproj_kernel(*refs, kv_only, kv_rank, q_rank, conv_w, four_w, d_model):
    if kv_only:
        (x_ref, mod_ref, gpre_ref, w1_ref, gkv_ref, wuk_ref, wuv_ref, vone_ref, cos_ref, sin_ref,
         k_ref, v_ref) = refs
    else:
        (x_ref, mod_ref, gpre_ref, w1_ref, gkv_ref, wuk_ref, wuv_ref, vone_ref, cos_ref, sin_ref,
         bg_ref, gq_ref, wuq_ref, dc_ref,
         k_ref, v_ref, q_ref, cb_ref, cc_ref, cu_ref, ab_ref, gate_ref) = refs

    x = x_ref[...]
    shift = mod_ref[0, 0:1, :]
    scale = mod_ref[0, 1:2, :]
    h = (_rms(x, gpre_ref[...]) * (1.0 + scale) + shift).astype(BF16)
    cos = cos_ref[...]
    sin = sin_ref[...]

    o_kpe = kv_rank
    o_rot = o_kpe + HEAD_PAD
    o_q = o_rot + HEAD_PAD
    p = _dot_t(h, w1_ref[0, 0:o_q, :])
    ckv = _rms(p[:, 0:kv_rank], gkv_ref[...]).astype(BF16)
    kpe = p[:, o_kpe:o_rot] * cos + p[:, o_rot:o_q] * sin
    k = _dot(ckv, wuk_ref[...]) + jnp.concatenate([kpe] * N_HEADS, axis=1)
    k_ref[...] = k.astype(k_ref.dtype)
    v_ref[...] = (_dot(ckv, wuv_ref[...]) + vone_ref[...]).astype(v_ref.dtype)
    if kv_only:
        return

    o_cb = o_q + q_rank
    cq = _rms(_dot_t(h, w1_ref[0, o_q:o_cb, :]), gq_ref[...]).astype(BF16)
    half = QK_ROPE // 2
    lane = lax.broadcasted_iota(jnp.int32, (1, HEAD_PAD), 1)
    first = (lane >= QK_NOPE) & (lane < QK_NOPE + half)
    cos_h = jnp.concatenate([cos] * N_HEADS, axis=1)
    sin_h = jnp.concatenate([jnp.where(first, -sin, sin)] * N_HEADS, axis=1)
    first_h = jnp.concatenate([first] * N_HEADS, axis=1)
    lin = _dot(cq, wuq_ref[...])
    width = lin.shape[1]
    partner = jnp.where(first_h, pltpu.roll(lin, width - half, 1), pltpu.roll(lin, half, 1))
    q_ref[...] = (lin * cos_h + partner * sin_h).astype(q_ref.dtype)

    o_cc = o_cb + conv_w
    o_cu = o_cc + conv_w
    o_four = o_cu + conv_w
    cb_ref[...] = _dot_t(h, w1_ref[0, o_cb:o_cc, :]).astype(cb_ref.dtype)
    cc_ref[...] = _dot_t(h, w1_ref[0, o_cc:o_cu, :]).astype(cc_ref.dtype)
    cu_ref[...] = _dot_t(h, w1_ref[0, o_cu:o_four, :]).astype(cu_ref.dtype)

    o_gate = o_four + four_w
    uf = _dot_t(h, w1_ref[0, o_four:o_gate, :]).astype(BF16)
    ab_ref[...] = _dot(uf, dc_ref[...]).astype(ab_ref.dtype)

    for j in range(N_BRANCHES):
        lo = o_gate + j * d_model
        z = _dot_t(h, w1_ref[0, lo:lo + d_model, :]) + bg_ref[:, j * d_model:(j + 1) * d_model]
        gate_ref[:, j * d_model:(j + 1) * d_model] = _sigmoid(z).astype(gate_ref.dtype)


def _inproj(xs, mod, seq_len, lw, tables, *, kv_only, tm):
    t, d = xs.shape
    nb = mod.shape[0]
    cos_t, sin_t = tables
    table_tiles = cos_t.shape[0] // tm
    kv_rank = lw["g_kv"].shape[1]
    q_rank = lw["g_q"].shape[1]
    conv_w = lw["conv_w"].shape[1]
    four_w = lw["dc"].shape[0]
    n_k = N_HEADS * HEAD_PAD

    def row(i):
        return (i, 0)

    def mod_map(i):
        return ((i * tm // seq_len) % nb, 0, 0)

    def tab_map(i):
        return (i % table_tiles, 0)

    w1, layer = lw["w1"], lw["layer"]
    w1_rows = kv_rank + 2 * HEAD_PAD if kv_only else w1.shape[1]
    in_specs = [
        pl.BlockSpec((tm, d), row),
        pl.BlockSpec((1,) + mod.shape[1:], mod_map),
        _resident((1, d)),
        pl.BlockSpec((1, w1_rows, d), lambda i: (layer, 0, 0), pipeline_mode=pl.Buffered(1)),
        _resident((1, kv_rank)),
        _resident(lw["wuk"].shape),
        _resident(lw["wuv"].shape),
        _resident(lw["v_one"].shape),
        pl.BlockSpec((tm, HEAD_PAD), tab_map),
        pl.BlockSpec((tm, HEAD_PAD), tab_map),
    ]
    args = [xs, mod, lw["g_pre_mix"], w1, lw["g_kv"], lw["wuk"], lw["wuv"], lw["v_one"], cos_t, sin_t]
    out_shape = [jax.ShapeDtypeStruct((t, n_k), BF16), jax.ShapeDtypeStruct((t, n_k), BF16)]
    out_specs = [pl.BlockSpec((tm, n_k), row), pl.BlockSpec((tm, n_k), row)]
    if not kv_only:
        in_specs += [
            _resident(lw["b_gate"].shape),
            _resident((1, q_rank)),
            _resident(lw["wuq"].shape),
            _resident(lw["dc"].shape),
        ]
        args += [lw["b_gate"], lw["g_q"], lw["wuq"], lw["dc"]]
        widths = [n_k, conv_w, conv_w, conv_w, 2 * four_w, N_BRANCHES * d]
        out_shape += [jax.ShapeDtypeStruct((t, w), BF16) for w in widths]
        out_specs += [pl.BlockSpec((tm, w), row) for w in widths]
    outs = pl.pallas_call(
        functools.partial(_inproj_kernel, kv_only=kv_only, kv_rank=kv_rank, q_rank=q_rank,
                          conv_w=conv_w, four_w=four_w, d_model=d),
        grid=(t // tm,),
        in_specs=in_specs,
        out_specs=out_specs,
        out_shape=out_shape,
        compiler_params=_params(1),
        name="inproj_kv" if kv_only else "inproj",
    )(*args)
    names = ["k", "v", "q", "cb", "cc", "cu", "ab", "gate"]
    return dict(zip(names, outs))


def _attn_kernel(*refs, n_seg):
    q_ref = refs[0]
    o_ref = refs[-1]
    chunk = min(q_ref.shape[0], ATTN_ROWS)
    for c in range(q_ref.shape[0] // chunk):
        rows = slice(c * chunk, (c + 1) * chunk)
        outs = []
        for hh in range(N_HEADS):
            head = slice(hh * HEAD_PAD, (hh + 1) * HEAD_PAD)
            qh = q_ref[rows, head]
            s = [_dot_t(qh, refs[1 + 2 * i][:, head]) for i in range(n_seg)]
            m = functools.reduce(jnp.maximum, [jnp.max(si, axis=-1, keepdims=True) for si in s])
            acc = functools.reduce(jnp.add, [
                _dot(jnp.exp2((s[i] - m).astype(BF16)), refs[2 + 2 * i][:, head]) for i in range(n_seg)])
            outs.append(acc[:, 0:V_DIM] / acc[:, V_DIM:V_DIM + 1])
        o_ref[rows, :] = jnp.concatenate(outs, axis=1).astype(o_ref.dtype)


def _attention(q, segs, batch, seq_q, *, tq):
    t = q.shape[0]
    qt = seq_q // tq
    n_k = N_HEADS * HEAD_PAD
    in_specs = [pl.BlockSpec((tq, n_k), lambda b, j: (b * qt + j, 0))]
    args = [q]
    for k, v, m in segs:
        in_specs.append(pl.BlockSpec((m, n_k), lambda b, j: (b, 0)))
        in_specs.append(pl.BlockSpec((m, n_k), lambda b, j: (b, 0)))
        args += [k, v]
    return pl.pallas_call(
        functools.partial(_attn_kernel, n_seg=len(segs)),
        grid=(batch, qt),
        in_specs=in_specs,
        out_specs=pl.BlockSpec((tq, N_HEADS * V_DIM), lambda b, j: (b * qt + j, 0)),
        out_shape=jax.ShapeDtypeStruct((t, N_HEADS * V_DIM), BF16),
        compiler_params=_params(2),
        name="attention",
    )(*args)


def _four_kernel(cs_ref, ab_ref, o_ref, *, n, fw):
    o = _dot(cs_ref[:, 0:n], ab_ref[:, 0:fw]) + _dot(cs_ref[:, n:2 * n], ab_ref[:, fw:2 * fw])
    o_ref[...] = o.astype(o_ref.dtype)


def _fourier(ab, cs, batch, seq_len, *, tn):
    t, fw2 = ab.shape
    fw = fw2 // 2
    nt = seq_len // tn
    cs_spec = (_resident(cs.shape) if nt == 1
               else pl.BlockSpec((tn, 2 * seq_len), lambda b, j: (j, 0)))
    return pl.pallas_call(
        functools.partial(_four_kernel, n=seq_len, fw=fw),
        grid=(batch, nt),
        in_specs=[
            cs_spec,
            pl.BlockSpec((seq_len, fw2), lambda b, j: (b, 0)),
        ],
        out_specs=pl.BlockSpec((tn, fw), lambda b, j: (b * nt + j, 0)),
        out_shape=jax.ShapeDtypeStruct((t, fw), BF16),
        compiler_params=_params(2),
        name="fourier",
    )(cs, ab)


def _merge_kernel(x_ref, mod_ref, o_ref, cb_ref, cc_ref, cu_ref, ccp_ref, cup_ref, ccn_ref,
                  cun_ref, f_ref, gate_ref, convw_ref, wmo_ref, wco_ref, wfo_ref, wout_ref,
                  gpost_ref, out_ref, pad_ref, *, seq_len, tm, d_model):
    i = pl.program_id(0)
    has_prev = ((i * tm) % seq_len != 0).astype(F32)
    has_next = (((i + 1) * tm) % seq_len != 0).astype(F32)
    pad_ref[0:8, :] = ccp_ref[...].astype(F32) * cup_ref[...].astype(F32) * has_prev
    pad_ref[8:8 + tm, :] = cc_ref[...].astype(F32) * cu_ref[...].astype(F32)
    pad_ref[8 + tm:16 + tm, :] = ccn_ref[...].astype(F32) * cun_ref[...].astype(F32) * has_next
    d = d_model
    g1 = mod_ref[0, 2:3, :]
    chunk = min(tm, MERGE_ROWS)

    def conv_stage(r0):
        before = pad_ref[7 + r0:7 + r0 + chunk, :]
        after = pad_ref[9 + r0:9 + r0 + chunk, :]
        if tm > seq_len:
            at = (lax.broadcasted_iota(jnp.int32, (chunk, 1), 0) + r0) % seq_len
            before = jnp.where(at == 0, 0.0, before)
            after = jnp.where(at == seq_len - 1, 0.0, after)
        conv = (before * convw_ref[0:1, :] + pad_ref[8 + r0:8 + r0 + chunk, :] * convw_ref[1:2, :]
                + after * convw_ref[2:3, :])
        return (cb_ref[r0:r0 + chunk, :].astype(F32) * conv).astype(BF16)

    def branch_stage(r0, conv_in):
        return (_dot(o_ref[r0:r0 + chunk, :], wmo_ref[...]), _dot(conv_in, wco_ref[...]),
                _dot(f_ref[r0:r0 + chunk, :], wfo_ref[...]))

    def gate_stage(r0, ys):
        rows = slice(r0, r0 + chunk)
        return (gate_ref[rows, 0:d].astype(F32) * ys[0] + gate_ref[rows, d:2 * d].astype(F32) * ys[1]
                + gate_ref[rows, 2 * d:3 * d].astype(F32) * ys[2]).astype(BF16)

    def out_stage(r0, merged):
        return _dot(merged, wout_ref[...])

    def tail_stage(r0, y):
        rows = slice(r0, r0 + chunk)
        out_ref[rows, :] = x_ref[rows, :] + g1 * _rms(y, gpost_ref[...])

    stages = (conv_stage, branch_stage, gate_stage, out_stage, tail_stage)
    n_chunks = tm // chunk
    live = [None] * n_chunks
    for step in range(n_chunks + len(stages) - 1):
        for c in range(n_chunks):
            s = step - c
            if 0 <= s < len(stages):
                live[c] = stages[s](c * chunk) if s == 0 else stages[s](c * chunk, live[c])


def _merge(xs, mod, seq_len, pr, o, four, lw, *, tm):
    t, d = xs.shape
    nb = mod.shape[0]
    assert tm % seq_len == 0 or seq_len % tm == 0
    cw = lw["conv_w"].shape[1]
    fw = four.shape[1]
    hb = tm // 8
    last_hb = t // 8 - 1

    def row(i):
        return (i, 0)

    def prev_map(i):
        return (jnp.maximum(i * hb - 1, 0), 0)

    def next_map(i):
        return (jnp.minimum((i + 1) * hb, last_hb), 0)

    in_specs = [
        pl.BlockSpec((tm, d), row),
        pl.BlockSpec((1,) + mod.shape[1:], lambda i: ((i * tm // seq_len) % nb, 0, 0)),
        pl.BlockSpec((tm, o.shape[1]), row),
        pl.BlockSpec((tm, cw), row),
        pl.BlockSpec((tm, cw), row),
        pl.BlockSpec((tm, cw), row),
        pl.BlockSpec((8, cw), prev_map),
        pl.BlockSpec((8, cw), prev_map),
        pl.BlockSpec((8, cw), next_map),
        pl.BlockSpec((8, cw), next_map),
        pl.BlockSpec((tm, fw), row),
        pl.BlockSpec((tm, N_BRANCHES * d), row),
        _resident(lw["conv_w"].shape),
        _resident(lw["w_mla_out"].shape),
        _resident(lw["w_conv_out"].shape),
        _resident(lw["w_four_out"].shape),
        _resident(lw["w_out"].shape),
        _resident((1, d)),
    ]
    return pl.pallas_call(
        functools.partial(_merge_kernel, seq_len=seq_len, tm=tm, d_model=d),
        grid=(t // tm,),
        in_specs=in_specs,
        out_specs=pl.BlockSpec((tm, d), row),
        out_shape=jax.ShapeDtypeStruct((t, d), F32),
        scratch_shapes=[pltpu.VMEM((tm + 16, cw), F32)],
        compiler_params=_params(1),
        name="merge",
    )(xs, mod, o, pr["cb"], pr["cc"], pr["cu"], pr["cc"], pr["cu"], pr["cc"], pr["cu"], four,
      pr["gate"], lw["conv_w"], lw["w_mla_out"], lw["w_conv_out"], lw["w_four_out"], lw["w_out"],
      lw["g_post_mix"])


def _pack_rows(v):
    bits = lax.bitcast_convert_type(v.astype(BF16).astype(F32), jnp.uint32)
    rows = []
    for j in range(v.shape[1] // (2 * PACK_W)):
        lo = bits[:, (2 * j) * PACK_W:(2 * j + 1) * PACK_W]
        hi = bits[:, (2 * j + 1) * PACK_W:(2 * j + 2) * PACK_W]
        rows.append(lax.bitcast_convert_type((hi & jnp.uint32(0xFFFF0000)) | (lo >> 16), jnp.int32))
    return rows


def _unpack_rows(rows):
    parts = []
    for r in rows:
        u = lax.bitcast_convert_type(r, jnp.uint32)
        parts.append(lax.bitcast_convert_type(u << 16, F32))
        parts.append(lax.bitcast_convert_type(u & jnp.uint32(0xFFFF0000), F32))
    return jnp.concatenate(parts, axis=1)


def _route_kernel(x_ref, mod_ref, gpre_ref, wrt_ref, brt_ref, triu_ref,
                  tp_ref, sel_ref, rank_ref, cnt_ref, wgt_ref, *, n_experts):
    shift = mod_ref[0, 3:4, :]
    scale = mod_ref[0, 4:5, :]
    t = _rms(x_ref[...], gpre_ref[...]) * (1.0 + scale) + shift
    t_hi = t.astype(BF16)
    t_lo = (t - t_hi.astype(F32)).astype(BF16)
    for j, r in enumerate(_pack_rows(t)):
        tp_ref[j] = r
    tm = t.shape[0]
    hh = _dot_t(wrt_ref[...], t_hi)
    logits = hh[0:n_experts] + hh[n_experts:2 * n_experts] + _dot_t(wrt_ref[0:n_experts, :], t_lo)
    scores = _sigmoid(logits)
    work = scores + brt_ref[...]
    row = lax.broadcasted_iota(jnp.int32, scores.shape, 0)
    wide = lax.broadcasted_iota(jnp.int32, (LANE, tm), 0)
    firsts, picked, hits = [], [], []
    for k in range(TOP_K):
        best = jnp.max(work, axis=0, keepdims=True)
        first = jnp.min(jnp.where(work == best, row, n_experts), axis=0, keepdims=True)
        hit = row == first
        firsts.append(first)
        picked.append(jnp.sum(jnp.where(hit, scores, 0.0), axis=0, keepdims=True))
        hits.append(wide == first + k * n_experts)
        work = jnp.where(hit, -jnp.inf, work)
    total = functools.reduce(jnp.add, picked)
    onehot = functools.reduce(jnp.add, [jnp.where(h, 1.0, 0.0) for h in hits])
    earlier = _dot(onehot.astype(BF16), triu_ref[...])
    col = jnp.broadcast_to(jnp.sum(onehot, axis=1, keepdims=True), (LANE, LANE))
    row_c = lax.broadcasted_iota(jnp.int32, (LANE, LANE), 0)
    before = jnp.zeros((LANE, LANE), F32)
    for s in range(1, TOP_K):
        before = before + jnp.where(row_c >= s * n_experts, pltpu.roll(col, s * n_experts, 0), 0.0)
    ahead = earlier + before[:, 0:1]
    row8 = lax.broadcasted_iota(jnp.int32, (8, tm), 0)
    sel = jnp.zeros((8, tm), jnp.int32)
    rank = jnp.zeros((8, tm), F32)
    wgt_t = jnp.zeros((LANE, tm), F32)
    for k in range(TOP_K):
        sel = jnp.where(row8 == k, firsts[k], sel)
        rank = jnp.where(row8 == k, jnp.sum(jnp.where(hits[k], ahead, 0.0), axis=0, keepdims=True), rank)
        wgt_t = jnp.where(wide == k, picked[k] / total * ROUTED_SCALE, wgt_t)
    sel_ref[0] = sel
    rank_ref[0] = rank.astype(jnp.int32)
    cnt_ref[0] = col.astype(jnp.int32)
    wgt_ref[...] = wgt_t.T


def _route(x1, mod, seq_len, lw, *, tm):
    t, d = x1.shape
    n_tiles = t // tm
    nb = mod.shape[0]
    tiles_per_seq = max(seq_len // tm, 1)
    n_experts = lw["b_router_t"].shape[0]
    rows = d // (2 * PACK_W)
    triu = jnp.tri(tm, tm, -1, dtype=BF16).T
    tile3 = lambda i: (i, 0, 0)
    return pl.pallas_call(
        functools.partial(_route_kernel, n_experts=n_experts),
        grid=(n_tiles,),
        in_specs=[
            pl.BlockSpec((tm, d), lambda i: (i, 0)),
            pl.BlockSpec((1,) + mod.shape[1:], lambda i: ((i // tiles_per_seq) % nb, 0, 0)),
            _resident((1, d)),
            _resident(lw["w_router_t"].shape),
            _resident((n_experts, 1)),
            _resident((tm, tm)),
        ],
        out_specs=[
            pl.BlockSpec((rows, tm, PACK_W), lambda i: (0, i, 0)),
            pl.BlockSpec((1, 8, tm), tile3),
            pl.BlockSpec((1, 8, tm), tile3),
            pl.BlockSpec((1, LANE, LANE), tile3),
            pl.BlockSpec((tm, LANE), lambda i: (i, 0)),
        ],
        out_shape=[
            jax.ShapeDtypeStruct((rows, t, PACK_W), jnp.int32),
            jax.ShapeDtypeStruct((n_tiles, 8, tm), jnp.int32),
            jax.ShapeDtypeStruct((n_tiles, 8, tm), jnp.int32),
            jax.ShapeDtypeStruct((n_tiles, LANE, LANE), jnp.int32),
            jax.ShapeDtypeStruct((t, LANE), F32),
        ],
        compiler_params=_params(1),
        name="moe_route",
    )(x1, mod, lw["g_pre_ffn"], lw["w_router_t"], lw["b_router_t"], triu)


def _expert_kernel(te_ref, meta_ref, xs_ref, wg_ref, wu_ref, wd_ref, ys_ref, wg_sc, wu_sc, wd_sc):
    i = pl.program_id(0)
    live = i < meta_ref[0]
    new_expert = (i == 0) | (te_ref[i] != te_ref[jnp.maximum(i - 1, 0)])

    @pl.when(live & new_expert)
    def _():
        wg_sc[...] = wg_ref[0, 0].astype(BF16)
        wu_sc[...] = wu_ref[0, 0].astype(BF16)
        wd_sc[...] = wd_ref[0, 0].astype(BF16)

    @pl.when(live)
    def _():
        x = _unpack_rows([xs_ref[j] for j in range(xs_ref.shape[0])]).astype(BF16)
        gate = _dot(x, wg_sc[...])
        act = (gate * _sigmoid(gate) * _dot(x, wu_sc[...])).astype(BF16)
        for j, r in enumerate(_pack_rows(_dot(act, wd_sc[...]))):
            ys_ref[j] = r


def _experts(xs, tile_expert, meta, w_gate_e, w_up_e, w_down_e, layer, *, tm):
    rows, p, _ = xs.shape
    n_experts, d, f = w_gate_e.shape[1:]

    def slot(i, te, meta):
        return (0, jnp.minimum(i, meta[0] - 1), 0)

    grid_spec = pltpu.PrefetchScalarGridSpec(
        num_scalar_prefetch=2,
        grid=(p // tm,),
        in_specs=[
            pl.BlockSpec((rows, tm, PACK_W), slot),
            pl.BlockSpec((1, 1, d, f), lambda i, te, meta: (layer, te[i], 0, 0)),
            pl.BlockSpec((1, 1, d, f), lambda i, te, meta: (layer, te[i], 0, 0)),
            pl.BlockSpec((1, 1, f, d), lambda i, te, meta: (layer, te[i], 0, 0)),
        ],
        out_specs=pl.BlockSpec((rows, tm, PACK_W), slot),
        scratch_shapes=[pltpu.VMEM((d, f), BF16), pltpu.VMEM((d, f), BF16), pltpu.VMEM((f, d), BF16)],
    )
    return pl.pallas_call(
        _expert_kernel,
        grid_spec=grid_spec,
        out_shape=jax.ShapeDtypeStruct(xs.shape, jnp.int32),
        compiler_params=_params(1),
        name="moe_experts",
    )(tile_expert, meta, xs, w_gate_e, w_up_e, w_down_e)


def _combine_kernel(x_ref, mod_ref, gpre_ref, gpost_ref, wgs_ref, wus_ref, wds_ref, yk_ref, wgt_ref,
                    out_ref):
    shift = mod_ref[0, 3:4, :]
    scale = mod_ref[0, 4:5, :]
    x = x_ref[...]
    t_hi = (_rms(x, gpre_ref[...]) * (1.0 + scale) + shift).astype(BF16)
    gate = _dot(t_hi, wgs_ref[...])
    act = (gate * _sigmoid(gate) * _dot(t_hi, wus_ref[...])).astype(BF16)
    acc = _dot(act, wds_ref[...])
    for k in range(TOP_K):
        y = _unpack_rows([yk_ref[k, j] for j in range(yk_ref.shape[1])])
        acc = acc + wgt_ref[:, k:k + 1] * y
    g2 = mod_ref[0, 5:6, :]
    out_ref[...] = x + g2 * _rms(acc, gpost_ref[...])


def _combine(x1, mod, seq_len, lw, yk, wgt, *, tm):
    t, d = x1.shape
    nb = mod.shape[0]
    tiles_per_seq = max(seq_len // tm, 1)
    rows = yk.shape[1]
    row = lambda i: (i, 0)
    return pl.pallas_call(
        _combine_kernel,
        grid=(t // tm,),
        in_specs=[
            pl.BlockSpec((tm, d), row),
            pl.BlockSpec((1,) + mod.shape[1:], lambda i: ((i // tiles_per_seq) % nb, 0, 0)),
            _resident((1, d)),
            _resident((1, d)),
            _resident(lw["w_gate_s"].shape),
            _resident(lw["w_up_s"].shape),
            _resident(lw["w_down_s"].shape),
            pl.BlockSpec((TOP_K, rows, tm, PACK_W), lambda i: (0, 0, i, 0)),
            pl.BlockSpec((tm, LANE), row),
        ],
        out_specs=pl.BlockSpec((tm, d), row),
        out_shape=jax.ShapeDtypeStruct((t, d), F32),
        compiler_params=_params(1),
        name="moe_combine",
    )(x1, mod, lw["g_pre_ffn"], lw["g_post_ffn"], lw["w_gate_s"], lw["w_up_s"], lw["w_down_s"], yk, wgt)


def _sc_mesh():
    return plsc.VectorSubcoreMesh(core_axis_name="core", subcore_axis_name="subcore")


def _sc_scatter_rows(src, idx, n_out):
    n_lists, n = idx.shape
    width = src.shape[1]

    @pl.kernel(out_type=jax.ShapeDtypeStruct((n_out, width), src.dtype), mesh=_sc_mesh(), scratch_types=[])
    def scatter(x_hbm, *refs):
        i_hbms, o_hbm = refs[:n_lists], refs[n_lists]

        def body(x_vmem, *i_vmems):
            for i_vmem in i_vmems:
                pltpu.sync_copy(x_vmem, o_hbm.at[i_vmem.at[0]])

        pltpu.emit_pipeline(
            body,
            grid=(n // SC_WINDOW,),
            in_specs=[pl.BlockSpec((SC_WINDOW, width), lambda i: (i, 0))]
            + [pl.BlockSpec((1, SC_WINDOW), lambda i: (0, i))] * n_lists,
            out_specs=[],
            core_axis_name=("core", "subcore"),
            dimension_semantics=(pltpu.PARALLEL,),
        )(x_hbm, *i_hbms)

    return scatter(src, *[idx[r].reshape(1, n) for r in range(n_lists)])


def _sc_gather_rows(table, idx):
    n = idx.shape[0]
    width = table.shape[1]

    @pl.kernel(out_type=jax.ShapeDtypeStruct((n, width), table.dtype), mesh=_sc_mesh(), scratch_types=[])
    def gather(x_hbm, i_hbm, o_hbm):
        def body(i_vmem, o_vmem):
            pltpu.sync_copy(x_hbm.at[i_vmem.at[0]], o_vmem)

        pltpu.emit_pipeline(
            body,
            grid=(n // SC_WINDOW,),
            in_specs=[pl.BlockSpec((1, SC_WINDOW), lambda i: (0, i))],
            out_specs=[pl.BlockSpec((SC_WINDOW, width), lambda i: (i, 0))],
            core_axis_name=("core", "subcore"),
            dimension_semantics=(pltpu.PARALLEL,),
        )(i_hbm, o_hbm)

    return gather(table, idx.reshape(1, n))


def _moe_sparse(x1, mod, seq_len, lw, w_gate_e, w_up_e, w_down_e, layer, *, tm, tm_e):
    t = x1.shape[0]
    n_experts = lw["b_router_t"].shape[0]
    assert n_experts * TOP_K == LANE
    n_tok_tiles = t // tm
    n_row_tiles = (t * TOP_K) // tm_e + n_experts
    p = n_row_tiles * tm_e
    experts = jnp.arange(n_experts, dtype=jnp.int32)

    tp, sel, rank, cnt, wgt = _route(x1, mod, seq_len, lw, tm=tm)
    rows = tp.shape[0]
    cnt = cnt[:, :, 0].reshape(n_tok_tiles, TOP_K, n_experts).sum(axis=1)
    padded = (cnt.sum(axis=0) + tm_e - 1) // tm_e * tm_e
    group_end = jnp.cumsum(padded)
    base = (group_end - padded)[None, :] + jnp.cumsum(cnt, axis=0) - cnt
    chosen = sel[:, :TOP_K, :, None] == experts
    pos = jnp.sum(jnp.where(chosen, base[:, None, None, :], 0), axis=-1) + rank[:, :TOP_K, :]
    pos = pos.transpose(1, 0, 2).reshape(TOP_K, t)

    n_used = group_end[-1] // tm_e
    tile_start = jnp.arange(n_row_tiles, dtype=jnp.int32) * tm_e
    tile_expert = jnp.sum(tile_start[:, None] >= group_end[None, :], axis=1).astype(jnp.int32)
    tile_expert = jnp.minimum(tile_expert, n_experts - 1)
    tile_expert = jnp.where(tile_start < group_end[-1], tile_expert, tile_expert[n_used - 1])
    meta = jnp.stack([n_used, n_used]).astype(jnp.int32)

    idx = pos[:, None, :] + (jnp.arange(rows, dtype=jnp.int32) * p)[None, :, None]
    xs = _sc_scatter_rows(tp.reshape(rows * t, PACK_W), idx.reshape(TOP_K, rows * t), rows * p)
    ys = _experts(xs.reshape(rows, p, PACK_W), tile_expert, meta, w_gate_e, w_up_e, w_down_e, layer, tm=tm_e)
    yk = _sc_gather_rows(ys.reshape(rows * p, PACK_W), idx.reshape(-1))
    return _combine(x1, mod, seq_len, lw, yk.reshape(TOP_K, rows, t, PACK_W), wgt, tm=tm)


def _rope_tables(n):
    rows = n // GRID_W
    r, col = jnp.meshgrid(jnp.arange(rows), jnp.arange(GRID_W), indexing="ij")
    r = r.reshape(-1).astype(F32)
    col = col.reshape(-1).astype(F32)
    pairs = QK_ROPE // 4
    inv = ROPE_BASE ** (-jnp.arange(pairs, dtype=F32) / pairs)
    ang = jnp.concatenate([r[:, None] * inv, col[:, None] * inv], axis=-1)
    cos, sin = jnp.cos(ang), jnp.sin(ang)
    pad = HEAD_PAD - QK_NOPE - QK_ROPE
    cos_t = jnp.concatenate([jnp.ones((n, QK_NOPE), F32), cos, cos, jnp.zeros((n, pad), F32)], axis=1)
    sin_t = jnp.concatenate([jnp.zeros((n, QK_NOPE), F32), sin, sin, jnp.zeros((n, pad), F32)], axis=1)
    return cos_t, sin_t


def _identity_tables(n):
    pad = HEAD_PAD - QK_NOPE - QK_ROPE
    cos_t = jnp.concatenate([jnp.ones((n, QK_NOPE + QK_ROPE), F32), jnp.zeros((n, pad), F32)], axis=1)
    return cos_t, jnp.zeros((n, HEAD_PAD), F32)


def _position_dft(n):
    nb = 64 if n % 64 == 0 else 1
    na = n // nb
    m = jnp.arange(n, dtype=jnp.int32)[None, :]
    ang_a = ((jnp.arange(na, dtype=jnp.int32)[:, None] * m) % na).astype(F32) * (2.0 * np.pi / na)
    ang_b = ((jnp.arange(nb, dtype=jnp.int32)[:, None] * m) % n).astype(F32) * (2.0 * np.pi / n)
    ca, sa = jnp.cos(ang_a)[:, None, :], jnp.sin(ang_a)[:, None, :]
    cb, sb = jnp.cos(ang_b)[None, :, :], jnp.sin(ang_b)[None, :, :]
    norm = 1.0 / np.sqrt(n)
    cos = ((ca * cb - sa * sb) * norm).reshape(n, n)
    sin = ((sa * cb + ca * sb) * (-norm)).reshape(n, n)
    return jnp.concatenate([cos, sin], axis=1).astype(BF16)


def _channel_dft(width):
    gc = width // FOURIER_GROUPS
    idx = (jnp.arange(gc, dtype=jnp.int32)[:, None] * jnp.arange(gc, dtype=jnp.int32)[None, :]) % gc
    ang = idx.astype(F32) * (2.0 * np.pi / gc)
    eye = jnp.eye(FOURIER_GROUPS, dtype=F32)
    norm = 1.0 / np.sqrt(gc)
    return jnp.concatenate([jnp.kron(eye, jnp.cos(ang) * norm), jnp.kron(eye, jnp.sin(ang) * norm)],
                           axis=1).astype(BF16)


def _w1_kernel(w_ref, o_ref, *, kv_rank):
    cols = w_ref.shape[2]
    kv_end = kv_rank + QK_ROPE
    half = QK_ROPE // 2
    tail = HEAD_PAD - QK_NOPE - QK_ROPE
    o_kpe = kv_rank
    o_rot = o_kpe + HEAD_PAD
    o_rest = o_rot + HEAD_PAD
    dt = o_ref.dtype
    o_ref[0, 0:kv_rank, :] = w_ref[0, 0:kv_rank, :].astype(dt)
    for base in (o_kpe, o_rot):
        o_ref[0, base:base + QK_NOPE, :] = jnp.zeros((QK_NOPE, cols), dt)
        o_ref[0, base + QK_NOPE + QK_ROPE:base + HEAD_PAD, :] = jnp.zeros((tail, cols), dt)
    o_ref[0, o_kpe + QK_NOPE:o_kpe + QK_NOPE + QK_ROPE, :] = w_ref[0, kv_rank:kv_end, :].astype(dt)
    o_ref[0, o_rot + QK_NOPE:o_rot + QK_NOPE + half, :] = (-w_ref[0, kv_rank + half:kv_end, :]).astype(dt)
    o_ref[0, o_rot + QK_NOPE + half:o_rot + QK_NOPE + QK_ROPE, :] = w_ref[0, kv_rank:kv_rank + half, :].astype(dt)
    o_ref[0, o_rest:, :] = w_ref[0, kv_end:, :].astype(dt)


def _prep_w1(w_in_t, kv_rank):
    n_layers, width, d = w_in_t.shape
    out_w = width - QK_ROPE + 2 * HEAD_PAD
    tc = _tile(d, 256)
    return pl.pallas_call(
        functools.partial(_w1_kernel, kv_rank=kv_rank),
        grid=(n_layers, d // tc),
        in_specs=[pl.BlockSpec((1, width, tc), lambda l, i: (l, 0, i))],
        out_specs=pl.BlockSpec((1, out_w, tc), lambda l, i: (l, 0, i)),
        out_shape=jax.ShapeDtypeStruct((n_layers, out_w, d), BF16),
        compiler_params=_params(2),
        name="prep_w1",
    )(w_in_t)


def _layer_weights(l, g_pre_mix, g_post_mix, g_pre_ffn, g_post_ffn, w1, b_gate, g_q, w_uq, g_kv,
                   w_ukv, w_mla_out, conv_w, w_conv_out, w_four_out, w_out, w_router, b_router,
                   w_gate_s, w_up_s, w_down_s, dc):
    kv_rank = g_kv.shape[1]
    q_rank = g_q.shape[1]
    qk_dim = QK_NOPE + QK_ROPE
    pad = HEAD_PAD - qk_dim
    uq = w_uq[l].reshape(q_rank, N_HEADS, qk_dim) * (qk_dim ** -0.5 * np.log2(np.e))
    zq = jnp.zeros((q_rank, N_HEADS, pad), F32)
    wuq = jnp.concatenate([uq, zq], axis=-1).reshape(q_rank, N_HEADS * HEAD_PAD).astype(BF16)
    ukv = w_ukv[l].reshape(kv_rank, N_HEADS, QK_NOPE + V_DIM)
    wuk = jnp.concatenate([ukv[..., :QK_NOPE], jnp.zeros((kv_rank, N_HEADS, HEAD_PAD - QK_NOPE), F32)],
                          axis=-1).reshape(kv_rank, N_HEADS * HEAD_PAD).astype(BF16)
    wuv = jnp.concatenate([ukv[..., QK_NOPE:], jnp.zeros((kv_rank, N_HEADS, HEAD_PAD - V_DIM), F32)],
                          axis=-1).reshape(kv_rank, N_HEADS * HEAD_PAD).astype(BF16)
    v_one = jnp.tile((jnp.arange(HEAD_PAD) == V_DIM).astype(F32), N_HEADS)[None]

    wr = w_router[l]
    wr_hi = wr.astype(BF16)
    wr_lo = (wr - wr_hi.astype(F32)).astype(BF16)
    return {
        "g_pre_mix": g_pre_mix[l][None], "g_post_mix": g_post_mix[l][None],
        "g_pre_ffn": g_pre_ffn[l][None], "g_post_ffn": g_post_ffn[l][None],
        "w1": w1, "layer": l, "b_gate": b_gate[l][None], "g_q": g_q[l][None], "g_kv": g_kv[l][None],
        "wuq": wuq, "wuk": wuk, "wuv": wuv, "v_one": v_one, "dc": dc,
        "conv_w": conv_w[l],
        "w_mla_out": w_mla_out[l].astype(BF16), "w_conv_out": w_conv_out[l].astype(BF16),
        "w_four_out": w_four_out[l].astype(BF16), "w_out": w_out[l].astype(BF16),
        "w_router_t": jnp.concatenate([wr_hi.T, wr_lo.T], axis=0), "b_router_t": b_router[l][:, None],
        "w_gate_s": w_gate_s[l].astype(BF16), "w_up_s": w_up_s[l].astype(BF16),
        "w_down_s": w_down_s[l].astype(BF16),
    }


def _tile(n, pref):
    return pref if n % pref == 0 else n


def kernel(x, c, ctx, c_ctx, w_ada, b_ada, g_pre_mix, g_post_mix, g_pre_ffn, g_post_ffn, w_in, b_gate,
           g_q, w_uq, g_kv, w_ukv, w_mla_out, conv_w, w_conv_out, w_four_out, w_out, w_router, b_router,
           w_gate_e, w_up_e, w_down_e, w_gate_s, w_up_s, w_down_s):
    batch, seq, d = x.shape
    n_ctx = ctx.shape[1]
    n_layers = w_in.shape[0]
    xs = x.reshape(batch * seq, d)
    cs = ctx.reshape(batch * n_ctx, d)

    mod_rows = 16
    c_all = jnp.concatenate([c, c_ctx[None], jnp.zeros((mod_rows - batch - 1, d), F32)], axis=0)
    ada = _ada(c_all, w_ada, b_ada)

    tab_x = _rope_tables(seq)
    tm_c = _tile(n_ctx, 256)
    tm_c_wide = _tile(batch * n_ctx, 512)
    tab_c = _identity_tables(tm_c_wide)
    cs_x = _position_dft(seq)
    cs_c = _position_dft(n_ctx)
    dc = _channel_dft(w_four_out.shape[1])
    w1 = _prep_w1(jnp.swapaxes(w_in, 1, 2), g_kv.shape[1])

    tm_x = _tile(seq, 512)
    tm_moe_x = _tile(seq, 1024)
    tm_moe_c = _tile(batch * n_ctx, 1024)

    for l in range(n_layers):
        last = l == n_layers - 1
        lw = _layer_weights(l, g_pre_mix, g_post_mix, g_pre_ffn, g_post_ffn, w1, b_gate, g_q, w_uq,
                            g_kv, w_ukv, w_mla_out, conv_w, w_conv_out, w_four_out, w_out, w_router,
                            b_router, w_gate_s, w_up_s, w_down_s, dc)
        mods = ada[l].reshape(mod_rows, 6, d)
        mod_x = mods[:batch]
        mod_c = mods[batch:batch + 1]

        pc = _inproj(cs, mod_c, n_ctx, lw, tab_c, kv_only=last, tm=tm_c_wide)
        px = _inproj(xs, mod_x, seq, lw, tab_x, kv_only=False, tm=tm_x)
        o_x = _attention(px["q"], [(pc["k"], pc["v"], n_ctx), (px["k"], px["v"], seq)], batch, seq,
                         tq=_tile(seq, 2 * ATTN_ROWS))
        f_x = _fourier(px["ab"], cs_x, batch, seq, tn=seq)
        x1 = _merge(xs, mod_x, seq, px, o_x, f_x, lw, tm=_tile(seq, 2 * MERGE_ROWS))
        xs = _moe_sparse(x1, mod_x, seq, lw, w_gate_e, w_up_e, w_down_e, l, tm=tm_moe_x, tm_e=1024)
        if not last:
            o_c = _attention(pc["q"], [(pc["k"], pc["v"], n_ctx)], batch, n_ctx, tq=tm_c)
            f_c = _fourier(pc["ab"], cs_c, batch, n_ctx, tn=tm_c)
            c1 = _merge(cs, mod_c, n_ctx, pc, o_c, f_c, lw, tm=tm_moe_c)
            cs = _moe_sparse(c1, mod_c, batch * n_ctx, lw, w_gate_e, w_up_e, w_down_e, l, tm=tm_moe_c,
                             tm_e=512)
    return xs.reshape(batch, seq, d)
```

```python
import functools

import numpy as np
import jax
import jax.numpy as jnp
from jax import lax
from jax.experimental import pallas as pl
from jax.experimental.pallas import tpu as pltpu
from jax.experimental.pallas import tpu_sc as plsc

N_HEADS = 8
QK_NOPE = 64
QK_ROPE = 32
V_DIM = 64
GRID_W = 64
ROPE_BASE = 10000.0
FOURIER_GROUPS = 4
TOP_K = 4
ROUTED_SCALE = 2.5
N_BRANCHES = 3
EPS = 1e-6

LANE = 128
HEAD_PAD = LANE
VMEM_LIMIT = 56 * 1024 * 1024
PACK_W = 256
SC_WINDOW = 128
ATTN_ROWS = 512
MERGE_ROWS = 512

F32 = jnp.float32
BF16 = jnp.bfloat16


def _rms(x, g):
    return x * lax.rsqrt(jnp.mean(x * x, axis=-1, keepdims=True) + EPS) * g


def _sigmoid(x):
    return 1.0 / (1.0 + jnp.exp(-x))


def _dot(a, b):
    return jnp.dot(a, b, preferred_element_type=F32)


def _dot_t(a, b_t):
    return lax.dot_general(a, b_t, (((1,), (1,)), ((), ())), preferred_element_type=F32)


def _skewed(stages, n_chunks, chunk):
    live = [None] * n_chunks
    for step in range(n_chunks + len(stages) - 1):
        for c in range(n_chunks):
            s = step - c
            if 0 <= s < len(stages):
                live[c] = stages[s](c * chunk) if s == 0 else stages[s](c * chunk, live[c])


def _resident(shape):
    nd = len(shape)
    return pl.BlockSpec(shape, lambda *_: (0,) * nd, pipeline_mode=pl.Buffered(1))


class _Layer:
    def __init__(self, stacked, index):
        self.stacked, self.index = stacked, index
        self.shape = stacked.shape[1:]

    def spec(self):
        index = (self.index,) + (0,) * len(self.shape)
        return pl.BlockSpec((None,) + self.shape, lambda *_: index, pipeline_mode=pl.Buffered(1))


def _params(n_grid):
    return pltpu.CompilerParams(dimension_semantics=("arbitrary",) * n_grid,
                                vmem_limit_bytes=VMEM_LIMIT)


def _ada_kernel(c_ref, w_ref, b_ref, o_ref):
    c = c_ref[...]
    a = (c * _sigmoid(c)).astype(BF16)
    o_ref[0] = _dot(a, w_ref[0].astype(BF16)) + b_ref[0]


def _ada(c_all, w_ada, b_ada):
    n_layers, d, n_out = w_ada.shape
    rows = c_all.shape[0]
    tn = 1536
    return pl.pallas_call(
        _ada_kernel,
        grid=(n_layers, n_out // tn),
        in_specs=[
            pl.BlockSpec((rows, d), lambda l, j: (0, 0)),
            pl.BlockSpec((1, d, tn), lambda l, j: (l, 0, j)),
            pl.BlockSpec((1, 1, tn), lambda l, j: (l, 0, j)),
        ],
        out_specs=pl.BlockSpec((1, rows, tn), lambda l, j: (l, 0, j)),
        out_shape=jax.ShapeDtypeStruct((n_layers, rows, n_out), F32),
        compiler_params=_params(2),
        name="ada",
    )(c_all, w_ada, b_ada.reshape(n_layers, 1, n_out))


def _inproj_kernel(*refs, kv_only, kv_rank, q_rank, conv_w, four_w, d_model):
    if kv_only:
        (x_ref, mod_ref, gpre_ref, w1_ref, gkv_ref, wuk_ref, wuv_ref, vone_ref, cos_ref, sin_ref,
         k_ref, v_ref) = refs
    else:
        (x_ref, mod_ref, gpre_ref, w1_ref, gkv_ref, wuk_ref, wuv_ref, vone_ref, cos_ref, sin_ref,
         bg_ref, gq_ref, wuq_ref, dc_ref,
         k_ref, v_ref, q_ref, cb_ref, cc_ref, cu_ref, ab_ref, gate_ref) = refs

    x = x_ref[...]
    shift = mod_ref[0, 0:1, :]
    scale = mod_ref[0, 1:2, :]
    h = (_rms(x, gpre_ref[...]) * (1.0 + scale) + shift).astype(BF16)
    cos = cos_ref[...]
    sin = sin_ref[...]

    o_kpe = kv_rank
    o_rot = o_kpe + HEAD_PAD
    o_q = o_rot + HEAD_PAD
    p = _dot_t(h, w1_ref[0, 0:o_q, :])
    ckv = _rms(p[:, 0:kv_rank], gkv_ref[...]).astype(BF16)
    kpe = p[:, o_kpe:o_rot] * cos + p[:, o_rot:o_q] * sin
    k = _dot(ckv, wuk_ref[...]) + jnp.concatenate([kpe] * N_HEADS, axis=1)
    k_ref[...] = k.astype(k_ref.dtype)
    v_ref[...] = (_dot(ckv, wuv_ref[...]) + vone_ref[...]).astype(v_ref.dtype)
    if kv_only:
        return

    o_cb = o_q + q_rank
    cq = _rms(_dot_t(h, w1_ref[0, o_q:o_cb, :]), gq_ref[...]).astype(BF16)
    half = QK_ROPE // 2
    lane = lax.broadcasted_iota(jnp.int32, (1, HEAD_PAD), 1)
    first = (lane >= QK_NOPE) & (lane < QK_NOPE + half)
    cos_h = jnp.concatenate([cos] * N_HEADS, axis=1)
    sin_h = jnp.concatenate([jnp.where(first, -sin, sin)] * N_HEADS, axis=1)
    first_h = jnp.concatenate([first] * N_HEADS, axis=1)
    lin = _dot(cq, wuq_ref[...])
    width = lin.shape[1]
    partner = jnp.where(first_h, pltpu.roll(lin, width - half, 1), pltpu.roll(lin, half, 1))
    q_ref[...] = (lin * cos_h + partner * sin_h).astype(q_ref.dtype)

    o_cc = o_cb + conv_w
    o_cu = o_cc + conv_w
    o_four = o_cu + conv_w
    cb_ref[...] = _dot_t(h, w1_ref[0, o_cb:o_cc, :]).astype(cb_ref.dtype)
    cc_ref[...] = _dot_t(h, w1_ref[0, o_cc:o_cu, :]).astype(cc_ref.dtype)
    cu_ref[...] = _dot_t(h, w1_ref[0, o_cu:o_four, :]).astype(cu_ref.dtype)

    o_gate = o_four + four_w
    uf = _dot_t(h, w1_ref[0, o_four:o_gate, :]).astype(BF16)
    ab_ref[...] = _dot(uf, dc_ref[...]).astype(ab_ref.dtype)

    for j in range(N_BRANCHES):
        lo = o_gate + j * d_model
        z = _dot_t(h, w1_ref[0, lo:lo + d_model, :]) + bg_ref[:, j * d_model:(j + 1) * d_model]
        gate_ref[:, j * d_model:(j + 1) * d_model] = _sigmoid(z).astype(gate_ref.dtype)


def _inproj(xs, mod, seq_len, lw, tables, *, kv_only, tm):
    t, d = xs.shape
    nb = mod.shape[0]
    cos_t, sin_t = tables
    table_tiles = cos_t.shape[0] // tm
    kv_rank = lw["g_kv"].shape[1]
    q_rank = lw["g_q"].shape[1]
    conv_w = lw["conv_w"].shape[1]
    four_w = lw["dc"].shape[0]
    n_k = N_HEADS * HEAD_PAD

    def row(i):
        return (i, 0)

    def mod_map(i):
        return ((i * tm // seq_len) % nb, 0, 0)

    def tab_map(i):
        return (i % table_tiles, 0)

    w1, layer = lw["w1"], lw["layer"]
    w1_rows = kv_rank + 2 * HEAD_PAD if kv_only else w1.shape[1]
    in_specs = [
        pl.BlockSpec((tm, d), row),
        pl.BlockSpec((1,) + mod.shape[1:], mod_map),
        lw["g_pre_mix"].spec(),
        pl.BlockSpec((1, w1_rows, d), lambda i: (layer, 0, 0), pipeline_mode=pl.Buffered(1)),
        lw["g_kv"].spec(),
        lw["wuk"].spec(),
        lw["wuv"].spec(),
        _resident(lw["v_one"].shape),
        pl.BlockSpec((tm, HEAD_PAD), tab_map),
        pl.BlockSpec((tm, HEAD_PAD), tab_map),
    ]
    args = [xs, mod, lw["g_pre_mix"].stacked, w1, lw["g_kv"].stacked, lw["wuk"].stacked, lw["wuv"].stacked,
            lw["v_one"], cos_t, sin_t]
    out_shape = [jax.ShapeDtypeStruct((t, n_k), BF16), jax.ShapeDtypeStruct((t, n_k), BF16)]
    out_specs = [pl.BlockSpec((tm, n_k), row), pl.BlockSpec((tm, n_k), row)]
    if not kv_only:
        in_specs += [
            lw["b_gate"].spec(),
            lw["g_q"].spec(),
            lw["wuq"].spec(),
            _resident(lw["dc"].shape),
        ]
        args += [lw["b_gate"].stacked, lw["g_q"].stacked, lw["wuq"].stacked, lw["dc"]]
        widths = [n_k, conv_w, conv_w, conv_w, 2 * four_w, N_BRANCHES * d]
        out_shape += [jax.ShapeDtypeStruct((t, w), BF16) for w in widths]
        out_specs += [pl.BlockSpec((tm, w), row) for w in widths]
    outs = pl.pallas_call(
        functools.partial(_inproj_kernel, kv_only=kv_only, kv_rank=kv_rank, q_rank=q_rank,
                          conv_w=conv_w, four_w=four_w, d_model=d),
        grid=(t // tm,),
        in_specs=in_specs,
        out_specs=out_specs,
        out_shape=out_shape,
        compiler_params=_params(1),
        name="inproj_kv" if kv_only else "inproj",
    )(*args)
    names = ["k", "v", "q", "cb", "cc", "cu", "ab", "gate"]
    return dict(zip(names, outs))


def _attn_kernel(*refs, n_seg):
    q_ref = refs[0]
    o_ref = refs[-1]
    chunk = min(q_ref.shape[0], ATTN_ROWS)
    for c in range(q_ref.shape[0] // chunk):
        rows = slice(c * chunk, (c + 1) * chunk)
        outs = []
        for hh in range(N_HEADS):
            head = slice(hh * HEAD_PAD, (hh + 1) * HEAD_PAD)
            qh = q_ref[rows, head]
            s = [_dot_t(qh, refs[1 + 2 * i][:, head]) for i in range(n_seg)]
            m = functools.reduce(jnp.maximum, [jnp.max(si, axis=-1, keepdims=True) for si in s])
            acc = functools.reduce(jnp.add, [
                _dot(jnp.exp2((s[i] - m).astype(BF16)), refs[2 + 2 * i][:, head]) for i in range(n_seg)])
            outs.append(acc[:, 0:V_DIM] / acc[:, V_DIM:V_DIM + 1])
        o_ref[rows, :] = jnp.concatenate(outs, axis=1).astype(o_ref.dtype)


def _attention(q, segs, batch, seq_q, *, tq):
    t = q.shape[0]
    qt = seq_q // tq
    n_k = N_HEADS * HEAD_PAD
    in_specs = [pl.BlockSpec((tq, n_k), lambda b, j: (b * qt + j, 0))]
    args = [q]
    for k, v, m in segs:
        in_specs.append(pl.BlockSpec((m, n_k), lambda b, j: (b, 0)))
        in_specs.append(pl.BlockSpec((m, n_k), lambda b, j: (b, 0)))
        args += [k, v]
    return pl.pallas_call(
        functools.partial(_attn_kernel, n_seg=len(segs)),
        grid=(batch, qt),
        in_specs=in_specs,
        out_specs=pl.BlockSpec((tq, N_HEADS * V_DIM), lambda b, j: (b * qt + j, 0)),
        out_shape=jax.ShapeDtypeStruct((t, N_HEADS * V_DIM), BF16),
        compiler_params=_params(2),
        name="attention",
    )(*args)


def _four_kernel(cs_ref, ab_ref, o_ref, *, n, fw):
    o = _dot(cs_ref[:, 0:n], ab_ref[:, 0:fw]) + _dot(cs_ref[:, n:2 * n], ab_ref[:, fw:2 * fw])
    o_ref[...] = o.astype(o_ref.dtype)


def _fourier(ab, cs, batch, seq_len, *, tn):
    t, fw2 = ab.shape
    fw = fw2 // 2
    nt = seq_len // tn
    cs_spec = (_resident(cs.shape) if nt == 1
               else pl.BlockSpec((tn, 2 * seq_len), lambda b, j: (j, 0)))
    return pl.pallas_call(
        functools.partial(_four_kernel, n=seq_len, fw=fw),
        grid=(batch, nt),
        in_specs=[
            cs_spec,
            pl.BlockSpec((seq_len, fw2), lambda b, j: (b, 0)),
        ],
        out_specs=pl.BlockSpec((tn, fw), lambda b, j: (b * nt + j, 0)),
        out_shape=jax.ShapeDtypeStruct((t, fw), BF16),
        compiler_params=_params(2),
        name="fourier",
    )(cs, ab)


def _merge_kernel(x_ref, mod_ref, o_ref, cb_ref, cc_ref, cu_ref, ccp_ref, cup_ref, ccn_ref,
                  cun_ref, f_ref, gate_ref, convw_ref, wmo_ref, wco_ref, wfo_ref, wout_ref,
                  gpost_ref, out_ref, pad_ref, *, seq_len, tm, d_model):
    i = pl.program_id(0)
    has_prev = ((i * tm) % seq_len != 0).astype(F32)
    has_next = (((i + 1) * tm) % seq_len != 0).astype(F32)
    pad_ref[0:8, :] = ccp_ref[...].astype(F32) * cup_ref[...].astype(F32) * has_prev
    pad_ref[8:8 + tm, :] = cc_ref[...].astype(F32) * cu_ref[...].astype(F32)
    pad_ref[8 + tm:16 + tm, :] = ccn_ref[...].astype(F32) * cun_ref[...].astype(F32) * has_next
    d = d_model
    g1 = mod_ref[0, 2:3, :]
    chunk = min(tm, MERGE_ROWS)

    def conv_stage(r0):
        before = pad_ref[7 + r0:7 + r0 + chunk, :]
        after = pad_ref[9 + r0:9 + r0 + chunk, :]
        if tm > seq_len:
            at = (lax.broadcasted_iota(jnp.int32, (chunk, 1), 0) + r0) % seq_len
            before = jnp.where(at == 0, 0.0, before)
            after = jnp.where(at == seq_len - 1, 0.0, after)
        conv = (before * convw_ref[0:1, :] + pad_ref[8 + r0:8 + r0 + chunk, :] * convw_ref[1:2, :]
                + after * convw_ref[2:3, :])
        return (cb_ref[r0:r0 + chunk, :].astype(F32) * conv).astype(BF16)

    def branch_stage(r0, conv_in):
        return (_dot(o_ref[r0:r0 + chunk, :], wmo_ref[...]), _dot(conv_in, wco_ref[...]),
                _dot(f_ref[r0:r0 + chunk, :], wfo_ref[...]))

    def gate_stage(r0, ys):
        rows = slice(r0, r0 + chunk)
        return (gate_ref[rows, 0:d].astype(F32) * ys[0] + gate_ref[rows, d:2 * d].astype(F32) * ys[1]
                + gate_ref[rows, 2 * d:3 * d].astype(F32) * ys[2]).astype(BF16)

    def out_stage(r0, merged):
        return _dot(merged, wout_ref[...])

    def tail_stage(r0, y):
        rows = slice(r0, r0 + chunk)
        out_ref[rows, :] = x_ref[rows, :] + g1 * _rms(y, gpost_ref[...])

    _skewed((conv_stage, branch_stage, gate_stage, out_stage, tail_stage), tm // chunk, chunk)


def _merge(xs, mod, seq_len, pr, o, four, lw, *, tm):
    t, d = xs.shape
    nb = mod.shape[0]
    assert tm % seq_len == 0 or seq_len % tm == 0
    cw = lw["conv_w"].shape[1]
    fw = four.shape[1]
    hb = tm // 8
    last_hb = t // 8 - 1

    def row(i):
        return (i, 0)

    def prev_map(i):
        return (jnp.maximum(i * hb - 1, 0), 0)

    def next_map(i):
        return (jnp.minimum((i + 1) * hb, last_hb), 0)

    in_specs = [
        pl.BlockSpec((tm, d), row),
        pl.BlockSpec((1,) + mod.shape[1:], lambda i: ((i * tm // seq_len) % nb, 0, 0)),
        pl.BlockSpec((tm, o.shape[1]), row),
        pl.BlockSpec((tm, cw), row),
        pl.BlockSpec((tm, cw), row),
        pl.BlockSpec((tm, cw), row),
        pl.BlockSpec((8, cw), prev_map),
        pl.BlockSpec((8, cw), prev_map),
        pl.BlockSpec((8, cw), next_map),
        pl.BlockSpec((8, cw), next_map),
        pl.BlockSpec((tm, fw), row),
        pl.BlockSpec((tm, N_BRANCHES * d), row),
        lw["conv_w"].spec(),
        lw["w_mla_out"].spec(),
        lw["w_conv_out"].spec(),
        lw["w_four_out"].spec(),
        lw["w_out"].spec(),
        lw["g_post_mix"].spec(),
    ]
    return pl.pallas_call(
        functools.partial(_merge_kernel, seq_len=seq_len, tm=tm, d_model=d),
        grid=(t // tm,),
        in_specs=in_specs,
        out_specs=pl.BlockSpec((tm, d), row),
        out_shape=jax.ShapeDtypeStruct((t, d), F32),
        scratch_shapes=[pltpu.VMEM((tm + 16, cw), F32)],
        compiler_params=_params(1),
        name="merge",
    )(xs, mod, o, pr["cb"], pr["cc"], pr["cu"], pr["cc"], pr["cu"], pr["cc"], pr["cu"], four,
      pr["gate"], lw["conv_w"].stacked, lw["w_mla_out"].stacked, lw["w_conv_out"].stacked,
      lw["w_four_out"].stacked, lw["w_out"].stacked, lw["g_post_mix"].stacked)


def _pack_rows(v):
    bits = lax.bitcast_convert_type(v.astype(BF16).astype(F32), jnp.uint32)
    rows = []
    for j in range(v.shape[1] // (2 * PACK_W)):
        lo = bits[:, (2 * j) * PACK_W:(2 * j + 1) * PACK_W]
        hi = bits[:, (2 * j + 1) * PACK_W:(2 * j + 2) * PACK_W]
        rows.append(lax.bitcast_convert_type((hi & jnp.uint32(0xFFFF0000)) | (lo >> 16), jnp.int32))
    return rows


def _unpack_rows(rows):
    parts = []
    for r in rows:
        u = lax.bitcast_convert_type(r, jnp.uint32)
        parts.append(lax.bitcast_convert_type(u << 16, F32))
        parts.append(lax.bitcast_convert_type(u & jnp.uint32(0xFFFF0000), F32))
    return jnp.concatenate(parts, axis=1)


def _route_kernel(x_ref, mod_ref, gpre_ref, wrt_ref, brt_ref, triu_ref,
                  tp_ref, sel_ref, rank_ref, cnt_ref, wgt_ref, *, n_experts):
    shift = mod_ref[0, 3:4, :]
    scale = mod_ref[0, 4:5, :]
    t = _rms(x_ref[...], gpre_ref[...]) * (1.0 + scale) + shift
    t_hi = t.astype(BF16)
    t_lo = (t - t_hi.astype(F32)).astype(BF16)
    for j, r in enumerate(_pack_rows(t)):
        tp_ref[j] = r
    tm = t.shape[0]
    hh = _dot_t(wrt_ref[...], t_hi)
    logits = hh[0:n_experts] + hh[n_experts:2 * n_experts] + _dot_t(wrt_ref[0:n_experts, :], t_lo)
    scores = _sigmoid(logits)
    work = scores + brt_ref[...]
    row = lax.broadcasted_iota(jnp.int32, scores.shape, 0)
    wide = lax.broadcasted_iota(jnp.int32, (LANE, tm), 0)
    firsts, picked, hits = [], [], []
    for k in range(TOP_K):
        best = jnp.max(work, axis=0, keepdims=True)
        first = jnp.min(jnp.where(work == best, row, n_experts), axis=0, keepdims=True)
        hit = row == first
        firsts.append(first)
        picked.append(jnp.sum(jnp.where(hit, scores, 0.0), axis=0, keepdims=True))
        hits.append(wide == first + k * n_experts)
        work = jnp.where(hit, -jnp.inf, work)
    total = functools.reduce(jnp.add, picked)
    onehot = functools.reduce(jnp.add, [jnp.where(h, 1.0, 0.0) for h in hits])
    earlier = _dot(onehot.astype(BF16), triu_ref[...])
    col = jnp.broadcast_to(jnp.sum(onehot, axis=1, keepdims=True), (LANE, LANE))
    row_c = lax.broadcasted_iota(jnp.int32, (LANE, LANE), 0)
    before = jnp.zeros((LANE, LANE), F32)
    for s in range(1, TOP_K):
        before = before + jnp.where(row_c >= s * n_experts, pltpu.roll(col, s * n_experts, 0), 0.0)
    ahead = earlier + before[:, 0:1]
    row8 = lax.broadcasted_iota(jnp.int32, (8, tm), 0)
    sel = jnp.zeros((8, tm), jnp.int32)
    rank = jnp.zeros((8, tm), F32)
    wgt_t = jnp.zeros((LANE, tm), F32)
    for k in range(TOP_K):
        sel = jnp.where(row8 == k, firsts[k], sel)
        rank = jnp.where(row8 == k, jnp.sum(jnp.where(hits[k], ahead, 0.0), axis=0, keepdims=True), rank)
        wgt_t = jnp.where(wide == k, picked[k] / total * ROUTED_SCALE, wgt_t)
    sel_ref[0] = sel
    rank_ref[0] = rank.astype(jnp.int32)
    cnt_ref[0] = col.astype(jnp.int32)
    wgt_ref[...] = wgt_t.T


def _route(x1, mod, seq_len, lw, *, tm):
    t, d = x1.shape
    n_tiles = t // tm
    nb = mod.shape[0]
    tiles_per_seq = max(seq_len // tm, 1)
    n_experts = lw["b_router_t"].shape[0]
    rows = d // (2 * PACK_W)
    triu = jnp.tri(tm, tm, -1, dtype=BF16).T
    tile3 = lambda i: (i, 0, 0)
    return pl.pallas_call(
        functools.partial(_route_kernel, n_experts=n_experts),
        grid=(n_tiles,),
        in_specs=[
            pl.BlockSpec((tm, d), lambda i: (i, 0)),
            pl.BlockSpec((1,) + mod.shape[1:], lambda i: ((i // tiles_per_seq) % nb, 0, 0)),
            lw["g_pre_ffn"].spec(),
            lw["w_router_t"].spec(),
            lw["b_router_t"].spec(),
            _resident((tm, tm)),
        ],
        out_specs=[
            pl.BlockSpec((rows, tm, PACK_W), lambda i: (0, i, 0)),
            pl.BlockSpec((1, 8, tm), tile3),
            pl.BlockSpec((1, 8, tm), tile3),
            pl.BlockSpec((1, LANE, LANE), tile3),
            pl.BlockSpec((tm, LANE), lambda i: (i, 0)),
        ],
        out_shape=[
            jax.ShapeDtypeStruct((rows, t, PACK_W), jnp.int32),
            jax.ShapeDtypeStruct((n_tiles, 8, tm), jnp.int32),
            jax.ShapeDtypeStruct((n_tiles, 8, tm), jnp.int32),
            jax.ShapeDtypeStruct((n_tiles, LANE, LANE), jnp.int32),
            jax.ShapeDtypeStruct((t, LANE), F32),
        ],
        compiler_params=_params(1),
        name="moe_route",
    )(x1, mod, lw["g_pre_ffn"].stacked, lw["w_router_t"].stacked, lw["b_router_t"].stacked, triu)


def _expert_kernel(te_ref, meta_ref, xs_ref, wg_ref, wu_ref, wd_ref, ys_ref, wg_sc, wu_sc, wd_sc):
    i = pl.program_id(0)
    live = i < meta_ref[0]
    new_expert = (i == 0) | (te_ref[i] != te_ref[jnp.maximum(i - 1, 0)])

    @pl.when(live & new_expert)
    def _():
        wg_sc[...] = wg_ref[0, 0].astype(BF16)
        wu_sc[...] = wu_ref[0, 0].astype(BF16)
        wd_sc[...] = wd_ref[0, 0].astype(BF16)

    @pl.when(live)
    def _():
        x = _unpack_rows([xs_ref[j] for j in range(xs_ref.shape[0])]).astype(BF16)
        gate = _dot(x, wg_sc[...])
        act = (gate * _sigmoid(gate) * _dot(x, wu_sc[...])).astype(BF16)
        for j, r in enumerate(_pack_rows(_dot(act, wd_sc[...]))):
            ys_ref[j] = r


def _experts(xs, tile_expert, meta, w_gate_e, w_up_e, w_down_e, layer, *, tm):
    rows, p, _ = xs.shape
    n_experts, d, f = w_gate_e.shape[1:]

    def slot(i, te, meta):
        return (0, jnp.minimum(i, meta[0] - 1), 0)

    grid_spec = pltpu.PrefetchScalarGridSpec(
        num_scalar_prefetch=2,
        grid=(p // tm,),
        in_specs=[
            pl.BlockSpec((rows, tm, PACK_W), slot),
            pl.BlockSpec((1, 1, d, f), lambda i, te, meta: (layer, te[i], 0, 0)),
            pl.BlockSpec((1, 1, d, f), lambda i, te, meta: (layer, te[i], 0, 0)),
            pl.BlockSpec((1, 1, f, d), lambda i, te, meta: (layer, te[i], 0, 0)),
        ],
        out_specs=pl.BlockSpec((rows, tm, PACK_W), slot),
        scratch_shapes=[pltpu.VMEM((d, f), BF16), pltpu.VMEM((d, f), BF16), pltpu.VMEM((f, d), BF16)],
    )
    return pl.pallas_call(
        _expert_kernel,
        grid_spec=grid_spec,
        out_shape=jax.ShapeDtypeStruct(xs.shape, jnp.int32),
        compiler_params=_params(1),
        name="moe_experts",
    )(tile_expert, meta, xs, w_gate_e, w_up_e, w_down_e)


def _combine_kernel(x_ref, mod_ref, gpre_ref, gpost_ref, wgs_ref, wus_ref, wds_ref, yk_ref, wgt_ref,
                    out_ref):
    shift = mod_ref[0, 3:4, :]
    scale = mod_ref[0, 4:5, :]
    x = x_ref[...]
    t_hi = (_rms(x, gpre_ref[...]) * (1.0 + scale) + shift).astype(BF16)
    gate = _dot(t_hi, wgs_ref[...])
    act = (gate * _sigmoid(gate) * _dot(t_hi, wus_ref[...])).astype(BF16)
    acc = _dot(act, wds_ref[...])
    for k in range(TOP_K):
        y = _unpack_rows([yk_ref[k, j] for j in range(yk_ref.shape[1])])
        acc = acc + wgt_ref[:, k:k + 1] * y
    g2 = mod_ref[0, 5:6, :]
    out_ref[...] = x + g2 * _rms(acc, gpost_ref[...])


def _combine(x1, mod, seq_len, lw, yk, wgt, *, tm):
    t, d = x1.shape
    nb = mod.shape[0]
    tiles_per_seq = max(seq_len // tm, 1)
    rows = yk.shape[1]
    row = lambda i: (i, 0)
    return pl.pallas_call(
        _combine_kernel,
        grid=(t // tm,),
        in_specs=[
            pl.BlockSpec((tm, d), row),
            pl.BlockSpec((1,) + mod.shape[1:], lambda i: ((i // tiles_per_seq) % nb, 0, 0)),
            lw["g_pre_ffn"].spec(),
            lw["g_post_ffn"].spec(),
            lw["w_gate_s"].spec(),
            lw["w_up_s"].spec(),
            lw["w_down_s"].spec(),
            pl.BlockSpec((TOP_K, rows, tm, PACK_W), lambda i: (0, 0, i, 0)),
            pl.BlockSpec((tm, LANE), row),
        ],
        out_specs=pl.BlockSpec((tm, d), row),
        out_shape=jax.ShapeDtypeStruct((t, d), F32),
        compiler_params=_params(1),
        name="moe_combine",
    )(x1, mod, lw["g_pre_ffn"].stacked, lw["g_post_ffn"].stacked, lw["w_gate_s"].stacked, lw["w_up_s"].stacked,
      lw["w_down_s"].stacked, yk, wgt)


def _sc_mesh():
    return plsc.VectorSubcoreMesh(core_axis_name="core", subcore_axis_name="subcore")


def _sc_scatter_rows(src, idx, n_out):
    n_lists, n = idx.shape
    width = src.shape[1]

    @pl.kernel(out_type=jax.ShapeDtypeStruct((n_out, width), src.dtype), mesh=_sc_mesh(), scratch_types=[])
    def scatter(x_hbm, *refs):
        i_hbms, o_hbm = refs[:n_lists], refs[n_lists]

        def body(x_vmem, *i_vmems):
            for i_vmem in i_vmems:
                pltpu.sync_copy(x_vmem, o_hbm.at[i_vmem.at[0]])

        pltpu.emit_pipeline(
            body,
            grid=(n // SC_WINDOW,),
            in_specs=[pl.BlockSpec((SC_WINDOW, width), lambda i: (i, 0))]
            + [pl.BlockSpec((1, SC_WINDOW), lambda i: (0, i))] * n_lists,
            out_specs=[],
            core_axis_name=("core", "subcore"),
            dimension_semantics=(pltpu.PARALLEL,),
        )(x_hbm, *i_hbms)

    return scatter(src, *[idx[r].reshape(1, n) for r in range(n_lists)])


def _sc_gather_rows(table, idx):
    n = idx.shape[0]
    width = table.shape[1]

    @pl.kernel(out_type=jax.ShapeDtypeStruct((n, width), table.dtype), mesh=_sc_mesh(), scratch_types=[])
    def gather(x_hbm, i_hbm, o_hbm):
        def body(i_vmem, o_vmem):
            pltpu.sync_copy(x_hbm.at[i_vmem.at[0]], o_vmem)

        pltpu.emit_pipeline(
            body,
            grid=(n // SC_WINDOW,),
            in_specs=[pl.BlockSpec((1, SC_WINDOW), lambda i: (0, i))],
            out_specs=[pl.BlockSpec((SC_WINDOW, width), lambda i: (i, 0))],
            core_axis_name=("core", "subcore"),
            dimension_semantics=(pltpu.PARALLEL,),
        )(i_hbm, o_hbm)

    return gather(table, idx.reshape(1, n))


def _moe_sparse(x1, mod, seq_len, lw, w_gate_e, w_up_e, w_down_e, layer, *, tm, tm_e):
    t = x1.shape[0]
    n_experts = lw["b_router_t"].shape[0]
    assert n_experts * TOP_K == LANE
    n_tok_tiles = t // tm
    n_row_tiles = (t * TOP_K) // tm_e + n_experts
    p = n_row_tiles * tm_e
    experts = jnp.arange(n_experts, dtype=jnp.int32)

    tp, sel, rank, cnt, wgt = _route(x1, mod, seq_len, lw, tm=tm)
    rows = tp.shape[0]
    cnt = cnt[:, :, 0].reshape(n_tok_tiles, TOP_K, n_experts).sum(axis=1)
    padded = (cnt.sum(axis=0) + tm_e - 1) // tm_e * tm_e
    group_end = jnp.cumsum(padded)
    base = (group_end - padded)[None, :] + jnp.cumsum(cnt, axis=0) - cnt
    chosen = sel[:, :TOP_K, :, None] == experts
    pos = jnp.sum(jnp.where(chosen, base[:, None, None, :], 0), axis=-1) + rank[:, :TOP_K, :]
    pos = pos.transpose(1, 0, 2).reshape(TOP_K, t)

    n_used = group_end[-1] // tm_e
    tile_start = jnp.arange(n_row_tiles, dtype=jnp.int32) * tm_e
    tile_expert = jnp.sum(tile_start[:, None] >= group_end[None, :], axis=1).astype(jnp.int32)
    tile_expert = jnp.minimum(tile_expert, n_experts - 1)
    tile_expert = jnp.where(tile_start < group_end[-1], tile_expert, tile_expert[n_used - 1])
    meta = jnp.stack([n_used, n_used]).astype(jnp.int32)

    idx = pos[:, None, :] + (jnp.arange(rows, dtype=jnp.int32) * p)[None, :, None]
    xs = _sc_scatter_rows(tp.reshape(rows * t, PACK_W), idx.reshape(TOP_K, rows * t), rows * p)
    ys = _experts(xs.reshape(rows, p, PACK_W), tile_expert, meta, w_gate_e, w_up_e, w_down_e, layer, tm=tm_e)
    yk = _sc_gather_rows(ys.reshape(rows * p, PACK_W), idx.reshape(-1))
    return _combine(x1, mod, seq_len, lw, yk.reshape(TOP_K, rows, t, PACK_W), wgt, tm=tm)


def _rope_tables(n):
    rows = n // GRID_W
    r, col = jnp.meshgrid(jnp.arange(rows), jnp.arange(GRID_W), indexing="ij")
    r = r.reshape(-1).astype(F32)
    col = col.reshape(-1).astype(F32)
    pairs = QK_ROPE // 4
    inv = ROPE_BASE ** (-jnp.arange(pairs, dtype=F32) / pairs)
    ang = jnp.concatenate([r[:, None] * inv, col[:, None] * inv], axis=-1)
    cos, sin = jnp.cos(ang), jnp.sin(ang)
    pad = HEAD_PAD - QK_NOPE - QK_ROPE
    cos_t = jnp.concatenate([jnp.ones((n, QK_NOPE), F32), cos, cos, jnp.zeros((n, pad), F32)], axis=1)
    sin_t = jnp.concatenate([jnp.zeros((n, QK_NOPE), F32), sin, sin, jnp.zeros((n, pad), F32)], axis=1)
    return cos_t, sin_t


def _identity_tables(n):
    pad = HEAD_PAD - QK_NOPE - QK_ROPE
    cos_t = jnp.concatenate([jnp.ones((n, QK_NOPE + QK_ROPE), F32), jnp.zeros((n, pad), F32)], axis=1)
    return cos_t, jnp.zeros((n, HEAD_PAD), F32)


def _position_dft(n):
    nb = 64 if n % 64 == 0 else 1
    na = n // nb
    m = jnp.arange(n, dtype=jnp.int32)[None, :]
    ang_a = ((jnp.arange(na, dtype=jnp.int32)[:, None] * m) % na).astype(F32) * (2.0 * np.pi / na)
    ang_b = ((jnp.arange(nb, dtype=jnp.int32)[:, None] * m) % n).astype(F32) * (2.0 * np.pi / n)
    ca, sa = jnp.cos(ang_a)[:, None, :], jnp.sin(ang_a)[:, None, :]
    cb, sb = jnp.cos(ang_b)[None, :, :], jnp.sin(ang_b)[None, :, :]
    norm = 1.0 / np.sqrt(n)
    cos = ((ca * cb - sa * sb) * norm).reshape(n, n)
    sin = ((sa * cb + ca * sb) * (-norm)).reshape(n, n)
    return jnp.concatenate([cos, sin], axis=1).astype(BF16)


def _channel_dft(width):
    gc = width // FOURIER_GROUPS
    idx = (jnp.arange(gc, dtype=jnp.int32)[:, None] * jnp.arange(gc, dtype=jnp.int32)[None, :]) % gc
    ang = idx.astype(F32) * (2.0 * np.pi / gc)
    eye = jnp.eye(FOURIER_GROUPS, dtype=F32)
    norm = 1.0 / np.sqrt(gc)
    return jnp.concatenate([jnp.kron(eye, jnp.cos(ang) * norm), jnp.kron(eye, jnp.sin(ang) * norm)],
                           axis=1).astype(BF16)


def _w1_kernel(w_ref, o_ref, *, kv_rank):
    cols = w_ref.shape[2]
    kv_end = kv_rank + QK_ROPE
    half = QK_ROPE // 2
    tail = HEAD_PAD - QK_NOPE - QK_ROPE
    o_kpe = kv_rank
    o_rot = o_kpe + HEAD_PAD
    o_rest = o_rot + HEAD_PAD
    dt = o_ref.dtype
    o_ref[0, 0:kv_rank, :] = w_ref[0, 0:kv_rank, :].astype(dt)
    for base in (o_kpe, o_rot):
        o_ref[0, base:base + QK_NOPE, :] = jnp.zeros((QK_NOPE, cols), dt)
        o_ref[0, base + QK_NOPE + QK_ROPE:base + HEAD_PAD, :] = jnp.zeros((tail, cols), dt)
    o_ref[0, o_kpe + QK_NOPE:o_kpe + QK_NOPE + QK_ROPE, :] = w_ref[0, kv_rank:kv_end, :].astype(dt)
    o_ref[0, o_rot + QK_NOPE:o_rot + QK_NOPE + half, :] = (-w_ref[0, kv_rank + half:kv_end, :]).astype(dt)
    o_ref[0, o_rot + QK_NOPE + half:o_rot + QK_NOPE + QK_ROPE, :] = w_ref[0, kv_rank:kv_rank + half, :].astype(dt)
    o_ref[0, o_rest:, :] = w_ref[0, kv_end:, :].astype(dt)


def _prep_w1(w_in_t, kv_rank):
    n_layers, width, d = w_in_t.shape
    out_w = width - QK_ROPE + 2 * HEAD_PAD
    tc = _tile(d, 256)
    return pl.pallas_call(
        functools.partial(_w1_kernel, kv_rank=kv_rank),
        grid=(n_layers, d // tc),
        in_specs=[pl.BlockSpec((1, width, tc), lambda l, i: (l, 0, i))],
        out_specs=pl.BlockSpec((1, out_w, tc), lambda l, i: (l, 0, i)),
        out_shape=jax.ShapeDtypeStruct((n_layers, out_w, d), BF16),
        compiler_params=_params(2),
        name="prep_w1",
    )(w_in_t)


def _stacked_weights(g_pre_mix, g_post_mix, g_pre_ffn, g_post_ffn, b_gate, g_q, w_uq, g_kv, w_ukv,
                     w_mla_out, conv_w, w_conv_out, w_four_out, w_out, w_router, b_router, w_gate_s,
                     w_up_s, w_down_s):
    n_layers, kv_rank = g_kv.shape
    q_rank = g_q.shape[1]
    qk_dim = QK_NOPE + QK_ROPE
    uq = w_uq.reshape(n_layers, q_rank, N_HEADS, qk_dim) * (qk_dim ** -0.5 * np.log2(np.e))
    zq = jnp.zeros((n_layers, q_rank, N_HEADS, HEAD_PAD - qk_dim), F32)
    wuq = jnp.concatenate([uq, zq], axis=-1).reshape(n_layers, q_rank, N_HEADS * HEAD_PAD).astype(BF16)
    ukv = w_ukv.reshape(n_layers, kv_rank, N_HEADS, QK_NOPE + V_DIM)
    zk = jnp.zeros((n_layers, kv_rank, N_HEADS, HEAD_PAD - QK_NOPE), F32)
    zv = jnp.zeros((n_layers, kv_rank, N_HEADS, HEAD_PAD - V_DIM), F32)
    wuk = jnp.concatenate([ukv[..., :QK_NOPE], zk], axis=-1).reshape(n_layers, kv_rank, -1).astype(BF16)
    wuv = jnp.concatenate([ukv[..., QK_NOPE:], zv], axis=-1).reshape(n_layers, kv_rank, -1).astype(BF16)
    wr_hi = w_router.astype(BF16)
    wr_lo = (w_router - wr_hi.astype(F32)).astype(BF16)
    wr_t = jnp.concatenate([jnp.swapaxes(wr_hi, 1, 2), jnp.swapaxes(wr_lo, 1, 2)], axis=1)
    return {
        "g_pre_mix": g_pre_mix[:, None], "g_post_mix": g_post_mix[:, None],
        "g_pre_ffn": g_pre_ffn[:, None], "g_post_ffn": g_post_ffn[:, None],
        "b_gate": b_gate[:, None], "g_q": g_q[:, None], "g_kv": g_kv[:, None],
        "wuq": wuq, "wuk": wuk, "wuv": wuv, "conv_w": conv_w,
        "w_mla_out": w_mla_out.astype(BF16), "w_conv_out": w_conv_out.astype(BF16),
        "w_four_out": w_four_out.astype(BF16), "w_out": w_out.astype(BF16),
        "w_router_t": wr_t, "b_router_t": b_router[:, :, None],
        "w_gate_s": w_gate_s.astype(BF16), "w_up_s": w_up_s.astype(BF16), "w_down_s": w_down_s.astype(BF16),
    }


def _tile(n, pref):
    return pref if n % pref == 0 else n


def kernel(x, c, ctx, c_ctx, w_ada, b_ada, g_pre_mix, g_post_mix, g_pre_ffn, g_post_ffn, w_in, b_gate,
           g_q, w_uq, g_kv, w_ukv, w_mla_out, conv_w, w_conv_out, w_four_out, w_out, w_router, b_router,
           w_gate_e, w_up_e, w_down_e, w_gate_s, w_up_s, w_down_s):
    batch, seq, d = x.shape
    n_ctx = ctx.shape[1]
    n_layers = w_in.shape[0]
    xs = x.reshape(batch * seq, d)
    cs = ctx.reshape(batch * n_ctx, d)

    mod_rows = 16
    c_all = jnp.concatenate([c, c_ctx[None], jnp.zeros((mod_rows - batch - 1, d), F32)], axis=0)
    ada = _ada(c_all, w_ada, b_ada)

    tab_x = _rope_tables(seq)
    tm_c = _tile(n_ctx, 256)
    tm_c_wide = _tile(batch * n_ctx, 512)
    tab_c = _identity_tables(tm_c_wide)
    cs_x = _position_dft(seq)
    cs_c = _position_dft(n_ctx)
    dc = _channel_dft(w_four_out.shape[1])
    w1 = _prep_w1(jnp.swapaxes(w_in, 1, 2), g_kv.shape[1])
    stacked = _stacked_weights(g_pre_mix, g_post_mix, g_pre_ffn, g_post_ffn, b_gate, g_q, w_uq, g_kv, w_ukv,
                               w_mla_out, conv_w, w_conv_out, w_four_out, w_out, w_router, b_router,
                               w_gate_s, w_up_s, w_down_s)
    v_one = jnp.tile((jnp.arange(HEAD_PAD) == V_DIM).astype(F32), N_HEADS)[None]

    tm_x = _tile(seq, 512)
    tm_moe_x = _tile(seq, 1024)
    tm_moe_c = _tile(batch * n_ctx, 1024)

    for l in range(n_layers):
        last = l == n_layers - 1
        lw = {name: _Layer(arr, l) for name, arr in stacked.items()}
        lw.update(w1=w1, layer=l, dc=dc, v_one=v_one)
        mods = ada[l].reshape(mod_rows, 6, d)
        mod_x = mods[:batch]
        mod_c = mods[batch:batch + 1]

        pc = _inproj(cs, mod_c, n_ctx, lw, tab_c, kv_only=last, tm=tm_c_wide)
        px = _inproj(xs, mod_x, seq, lw, tab_x, kv_only=False, tm=tm_x)
        o_x = _attention(px["q"], [(pc["k"], pc["v"], n_ctx), (px["k"], px["v"], seq)], batch, seq,
                         tq=_tile(seq, 2 * ATTN_ROWS))
        f_x = _fourier(px["ab"], cs_x, batch, seq, tn=seq)
        x1 = _merge(xs, mod_x, seq, px, o_x, f_x, lw, tm=_tile(seq, 2 * MERGE_ROWS))
        xs = _moe_sparse(x1, mod_x, seq, lw, w_gate_e, w_up_e, w_down_e, l, tm=tm_moe_x, tm_e=1024)
        if not last:
            o_c = _attention(pc["q"], [(pc["k"], pc["v"], n_ctx)], batch, n_ctx, tq=tm_c)
            f_c = _fourier(pc["ab"], cs_c, batch, n_ctx, tn=tm_c)
            c1 = _merge(cs, mod_c, n_ctx, pc, o_c, f_c, lw, tm=tm_moe_c)
            cs = _moe_sparse(c1, mod_c, batch * n_ctx, lw, w_gate_e, w_up_e, w_down_e, l, tm=tm_moe_c,
                             tm_e=512)
    return xs.reshape(batch, seq, d)
```

```python
import functools

import numpy as np
import jax
import jax.numpy as jnp
from jax import lax
from jax.experimental import pallas as pl
from jax.experimental.pallas import tpu as pltpu
from jax.experimental.pallas import tpu_sc as plsc

N_HEADS = 8
QK_NOPE = 64
QK_ROPE = 32
V_DIM = 64
GRID_W = 64
ROPE_BASE = 10000.0
FOURIER_GROUPS = 4
TOP_K = 4
ROUTED_SCALE = 2.5
N_BRANCHES = 3
EPS = 1e-6

LANE = 128
HEAD_PAD = LANE
VMEM_LIMIT = 56 * 1024 * 1024
PACK_W = 256
SC_WINDOW = 128
ATTN_ROWS = 512
MERGE_ROWS = 512

F32 = jnp.float32
BF16 = jnp.bfloat16


def _rms(x, g):
    return x * lax.rsqrt(jnp.mean(x * x, axis=-1, keepdims=True) + EPS) * g


def _sigmoid(x):
    return 1.0 / (1.0 + jnp.exp(-x))


def _dot(a, b):
    return jnp.dot(a, b, preferred_element_type=F32)


def _dot_t(a, b_t):
    return lax.dot_general(a, b_t, (((1,), (1,)), ((), ())), preferred_element_type=F32)


def _skewed(stages, n_chunks, chunk):
    live = [None] * n_chunks
    for step in range(n_chunks + len(stages) - 1):
        for c in range(n_chunks):
            s = step - c
            if 0 <= s < len(stages):
                live[c] = stages[s](c * chunk) if s == 0 else stages[s](c * chunk, live[c])


def _resident(shape):
    nd = len(shape)
    return pl.BlockSpec(shape, lambda *_: (0,) * nd, pipeline_mode=pl.Buffered(1))


class _Layer:
    def __init__(self, stacked, index):
        self.stacked, self.index = stacked, index
        self.shape = stacked.shape[1:]

    def spec(self):
        index = (self.index,) + (0,) * len(self.shape)
        return pl.BlockSpec((None,) + self.shape, lambda *_: index, pipeline_mode=pl.Buffered(1))


class _Mod:
    def __init__(self, table, layer, row0, n):
        self.table, self.layer, self.row0, self.n = table, layer, row0, n

    def spec(self, seq_of_step):
        layer, row0, n = self.layer, self.row0, self.n
        return pl.BlockSpec((None, 1) + self.table.shape[2:],
                            lambda i: (layer, row0 + seq_of_step(i) % n, 0, 0))


def _params(n_grid):
    return pltpu.CompilerParams(dimension_semantics=("arbitrary",) * n_grid,
                                vmem_limit_bytes=VMEM_LIMIT)


def _ada_kernel(c_ref, w_ref, b_ref, o_ref):
    c = c_ref[...]
    a = (c * _sigmoid(c)).astype(BF16)
    o_ref[0] = _dot(a, w_ref[0].astype(BF16)) + b_ref[0]


def _ada(c_all, w_ada, b_ada):
    n_layers, d, n_out = w_ada.shape
    rows = c_all.shape[0]
    tn = _tile(n_out, 1536)
    return pl.pallas_call(
        _ada_kernel,
        grid=(n_layers, n_out // tn),
        in_specs=[
            pl.BlockSpec((rows, d), lambda l, j: (0, 0)),
            pl.BlockSpec((1, d, tn), lambda l, j: (l, 0, j)),
            pl.BlockSpec((1, 1, tn), lambda l, j: (l, 0, j)),
        ],
        out_specs=pl.BlockSpec((1, rows, tn), lambda l, j: (l, 0, j)),
        out_shape=jax.ShapeDtypeStruct((n_layers, rows, n_out), F32),
        compiler_params=_params(2),
        name="ada",
    )(c_all, w_ada, b_ada.reshape(n_layers, 1, n_out))


def _inproj_kernel(*refs, kv_only, kv_rank, q_rank, conv_w, four_w, d_model):
    if kv_only:
        (x_ref, mod_ref, gpre_ref, w1_ref, gkv_ref, wuk_ref, wuv_ref, vone_ref, cos_ref, sin_ref,
         k_ref, v_ref) = refs
    else:
        (x_ref, mod_ref, gpre_ref, w1_ref, gkv_ref, wuk_ref, wuv_ref, vone_ref, cos_ref, sin_ref,
         bg_ref, gq_ref, wuq_ref, dc_ref,
         k_ref, v_ref, q_ref, cb_ref, cc_ref, cu_ref, ab_ref, gate_ref) = refs

    x = x_ref[...]
    shift = mod_ref[0, 0:1, :]
    scale = mod_ref[0, 1:2, :]
    h = (_rms(x, gpre_ref[...]) * (1.0 + scale) + shift).astype(BF16)
    cos = cos_ref[...]
    sin = sin_ref[...]

    o_kpe = kv_rank
    o_rot = o_kpe + HEAD_PAD
    o_q = o_rot + HEAD_PAD
    p = _dot_t(h, w1_ref[0, 0:o_q, :])
    ckv = _rms(p[:, 0:kv_rank], gkv_ref[...]).astype(BF16)
    kpe = p[:, o_kpe:o_rot] * cos + p[:, o_rot:o_q] * sin
    k = _dot(ckv, wuk_ref[...]) + jnp.concatenate([kpe] * N_HEADS, axis=1)
    k_ref[...] = k.astype(k_ref.dtype)
    v_ref[...] = (_dot(ckv, wuv_ref[...]) + vone_ref[...]).astype(v_ref.dtype)
    if kv_only:
        return

    o_cb = o_q + q_rank
    cq = _rms(_dot_t(h, w1_ref[0, o_q:o_cb, :]), gq_ref[...]).astype(BF16)
    half = QK_ROPE // 2
    lane = lax.broadcasted_iota(jnp.int32, (1, HEAD_PAD), 1)
    first = (lane >= QK_NOPE) & (lane < QK_NOPE + half)
    cos_h = jnp.concatenate([cos] * N_HEADS, axis=1)
    sin_h = jnp.concatenate([jnp.where(first, -sin, sin)] * N_HEADS, axis=1)
    first_h = jnp.concatenate([first] * N_HEADS, axis=1)
    lin = _dot(cq, wuq_ref[...])
    width = lin.shape[1]
    partner = jnp.where(first_h, pltpu.roll(lin, width - half, 1), pltpu.roll(lin, half, 1))
    q_ref[...] = (lin * cos_h + partner * sin_h).astype(q_ref.dtype)

    o_cc = o_cb + conv_w
    o_cu = o_cc + conv_w
    o_four = o_cu + conv_w
    cb_ref[...] = _dot_t(h, w1_ref[0, o_cb:o_cc, :]).astype(cb_ref.dtype)
    cc_ref[...] = _dot_t(h, w1_ref[0, o_cc:o_cu, :]).astype(cc_ref.dtype)
    cu_ref[...] = _dot_t(h, w1_ref[0, o_cu:o_four, :]).astype(cu_ref.dtype)

    o_gate = o_four + four_w
    uf = _dot_t(h, w1_ref[0, o_four:o_gate, :]).astype(BF16)
    ab_ref[...] = _dot(uf, dc_ref[...]).astype(ab_ref.dtype)

    for j in range(N_BRANCHES):
        lo = o_gate + j * d_model
        z = _dot_t(h, w1_ref[0, lo:lo + d_model, :]) + bg_ref[:, j * d_model:(j + 1) * d_model]
        gate_ref[:, j * d_model:(j + 1) * d_model] = _sigmoid(z).astype(gate_ref.dtype)


def _inproj(xs, mod, seq_len, lw, tables, *, kv_only, tm):
    t, d = xs.shape
    cos_t, sin_t = tables
    table_tiles = cos_t.shape[0] // tm
    kv_rank = lw["g_kv"].shape[1]
    q_rank = lw["g_q"].shape[1]
    conv_w = lw["conv_w"].shape[1]
    four_w = lw["dc"].shape[0]
    n_k = N_HEADS * HEAD_PAD

    def row(i):
        return (i, 0)

    def tab_map(i):
        return (i % table_tiles, 0)

    w1, layer = lw["w1"], lw["layer"]
    w1_rows = kv_rank + 2 * HEAD_PAD if kv_only else w1.shape[1]
    in_specs = [
        pl.BlockSpec((tm, d), row),
        mod.spec(lambda i: i * tm // seq_len),
        lw["g_pre_mix"].spec(),
        pl.BlockSpec((1, w1_rows, d), lambda i: (layer, 0, 0), pipeline_mode=pl.Buffered(1)),
        lw["g_kv"].spec(),
        lw["wuk"].spec(),
        lw["wuv"].spec(),
        _resident(lw["v_one"].shape),
        pl.BlockSpec((tm, HEAD_PAD), tab_map),
        pl.BlockSpec((tm, HEAD_PAD), tab_map),
    ]
    args = [xs, mod.table, lw["g_pre_mix"].stacked, w1, lw["g_kv"].stacked, lw["wuk"].stacked, lw["wuv"].stacked,
            lw["v_one"], cos_t, sin_t]
    out_shape = [jax.ShapeDtypeStruct((t, n_k), BF16), jax.ShapeDtypeStruct((t, n_k), BF16)]
    out_specs = [pl.BlockSpec((tm, n_k), row), pl.BlockSpec((tm, n_k), row)]
    if not kv_only:
        in_specs += [
            lw["b_gate"].spec(),
            lw["g_q"].spec(),
            lw["wuq"].spec(),
            _resident(lw["dc"].shape),
        ]
        args += [lw["b_gate"].stacked, lw["g_q"].stacked, lw["wuq"].stacked, lw["dc"]]
        widths = [n_k, conv_w, conv_w, conv_w, 2 * four_w, N_BRANCHES * d]
        out_shape += [jax.ShapeDtypeStruct((t, w), BF16) for w in widths]
        out_specs += [pl.BlockSpec((tm, w), row) for w in widths]
    outs = pl.pallas_call(
        functools.partial(_inproj_kernel, kv_only=kv_only, kv_rank=kv_rank, q_rank=q_rank,
                          conv_w=conv_w, four_w=four_w, d_model=d),
        grid=(t // tm,),
        in_specs=in_specs,
        out_specs=out_specs,
        out_shape=out_shape,
        compiler_params=_params(1),
        name="inproj_kv" if kv_only else "inproj",
    )(*args)
    names = ["k", "v", "q", "cb", "cc", "cu", "ab", "gate"]
    return dict(zip(names, outs))


def _attn_kernel(*refs, n_seg):
    q_ref = refs[0]
    o_ref = refs[-1]
    chunk = min(q_ref.shape[0], ATTN_ROWS)
    for c in range(q_ref.shape[0] // chunk):
        rows = slice(c * chunk, (c + 1) * chunk)
        outs = []
        for hh in range(N_HEADS):
            head = slice(hh * HEAD_PAD, (hh + 1) * HEAD_PAD)
            qh = q_ref[rows, head]
            s = [_dot_t(qh, refs[1 + 2 * i][:, head]) for i in range(n_seg)]
            m = functools.reduce(jnp.maximum, [jnp.max(si, axis=-1, keepdims=True) for si in s])
            acc = functools.reduce(jnp.add, [
                _dot(jnp.exp2((s[i] - m).astype(BF16)), refs[2 + 2 * i][:, head]) for i in range(n_seg)])
            outs.append(acc[:, 0:V_DIM] / acc[:, V_DIM:V_DIM + 1])
        o_ref[rows, :] = jnp.concatenate(outs, axis=1).astype(o_ref.dtype)


def _attention(q, segs, batch, seq_q, *, tq):
    t = q.shape[0]
    qt = seq_q // tq
    n_k = N_HEADS * HEAD_PAD
    in_specs = [pl.BlockSpec((tq, n_k), lambda b, j: (b * qt + j, 0))]
    args = [q]
    for k, v, m in segs:
        in_specs.append(pl.BlockSpec((m, n_k), lambda b, j: (b, 0)))
        in_specs.append(pl.BlockSpec((m, n_k), lambda b, j: (b, 0)))
        args += [k, v]
    return pl.pallas_call(
        functools.partial(_attn_kernel, n_seg=len(segs)),
        grid=(batch, qt),
        in_specs=in_specs,
        out_specs=pl.BlockSpec((tq, N_HEADS * V_DIM), lambda b, j: (b * qt + j, 0)),
        out_shape=jax.ShapeDtypeStruct((t, N_HEADS * V_DIM), BF16),
        compiler_params=_params(2),
        name="attention",
    )(*args)


def _four_kernel(cs_ref, ab_ref, o_ref, *, n, fw):
    o = _dot(cs_ref[:, 0:n], ab_ref[:, 0:fw]) + _dot(cs_ref[:, n:2 * n], ab_ref[:, fw:2 * fw])
    o_ref[...] = o.astype(o_ref.dtype)


def _fourier(ab, cs, batch, seq_len, *, tn):
    t, fw2 = ab.shape
    fw = fw2 // 2
    nt = seq_len // tn
    cs_spec = (_resident(cs.shape) if nt == 1
               else pl.BlockSpec((tn, 2 * seq_len), lambda b, j: (j, 0)))
    return pl.pallas_call(
        functools.partial(_four_kernel, n=seq_len, fw=fw),
        grid=(batch, nt),
        in_specs=[
            cs_spec,
            pl.BlockSpec((seq_len, fw2), lambda b, j: (b, 0)),
        ],
        out_specs=pl.BlockSpec((tn, fw), lambda b, j: (b * nt + j, 0)),
        out_shape=jax.ShapeDtypeStruct((t, fw), BF16),
        compiler_params=_params(2),
        name="fourier",
    )(cs, ab)


def _merge_kernel(x_ref, mod_ref, o_ref, cb_ref, cc_ref, cu_ref, ccp_ref, cup_ref, ccn_ref,
                  cun_ref, f_ref, gate_ref, convw_ref, wmo_ref, wco_ref, wfo_ref, wout_ref,
                  gpost_ref, out_ref, pad_ref, *, seq_len, tm, d_model):
    i = pl.program_id(0)
    has_prev = ((i * tm) % seq_len != 0).astype(F32)
    has_next = (((i + 1) * tm) % seq_len != 0).astype(F32)
    pad_ref[0:8, :] = ccp_ref[...].astype(F32) * cup_ref[...].astype(F32) * has_prev
    pad_ref[8:8 + tm, :] = cc_ref[...].astype(F32) * cu_ref[...].astype(F32)
    pad_ref[8 + tm:16 + tm, :] = ccn_ref[...].astype(F32) * cun_ref[...].astype(F32) * has_next
    d = d_model
    g1 = mod_ref[0, 2:3, :]
    chunk = min(tm, MERGE_ROWS)

    def conv_stage(r0):
        before = pad_ref[7 + r0:7 + r0 + chunk, :]
        after = pad_ref[9 + r0:9 + r0 + chunk, :]
        if tm > seq_len:
            at = (lax.broadcasted_iota(jnp.int32, (chunk, 1), 0) + r0) % seq_len
            before = jnp.where(at == 0, 0.0, before)
            after = jnp.where(at == seq_len - 1, 0.0, after)
        conv = (before * convw_ref[0:1, :] + pad_ref[8 + r0:8 + r0 + chunk, :] * convw_ref[1:2, :]
                + after * convw_ref[2:3, :])
        return (cb_ref[r0:r0 + chunk, :].astype(F32) * conv).astype(BF16)

    def branch_stage(r0, conv_in):
        return (_dot(o_ref[r0:r0 + chunk, :], wmo_ref[...]), _dot(conv_in, wco_ref[...]),
                _dot(f_ref[r0:r0 + chunk, :], wfo_ref[...]))

    def gate_stage(r0, ys):
        rows = slice(r0, r0 + chunk)
        return (gate_ref[rows, 0:d].astype(F32) * ys[0] + gate_ref[rows, d:2 * d].astype(F32) * ys[1]
                + gate_ref[rows, 2 * d:3 * d].astype(F32) * ys[2]).astype(BF16)

    def out_stage(r0, merged):
        return _dot(merged, wout_ref[...])

    def tail_stage(r0, y):
        rows = slice(r0, r0 + chunk)
        out_ref[rows, :] = x_ref[rows, :] + g1 * _rms(y, gpost_ref[...])

    _skewed((conv_stage, branch_stage, gate_stage, out_stage, tail_stage), tm // chunk, chunk)


def _merge(xs, mod, seq_len, pr, o, four, lw, *, tm):
    t, d = xs.shape
    assert tm % seq_len == 0 or seq_len % tm == 0
    cw = lw["conv_w"].shape[1]
    fw = four.shape[1]
    hb = tm // 8
    last_hb = t // 8 - 1

    def row(i):
        return (i, 0)

    def prev_map(i):
        return (jnp.maximum(i * hb - 1, 0), 0)

    def next_map(i):
        return (jnp.minimum((i + 1) * hb, last_hb), 0)

    in_specs = [
        pl.BlockSpec((tm, d), row),
        mod.spec(lambda i: i * tm // seq_len),
        pl.BlockSpec((tm, o.shape[1]), row),
        pl.BlockSpec((tm, cw), row),
        pl.BlockSpec((tm, cw), row),
        pl.BlockSpec((tm, cw), row),
        pl.BlockSpec((8, cw), prev_map),
        pl.BlockSpec((8, cw), prev_map),
        pl.BlockSpec((8, cw), next_map),
        pl.BlockSpec((8, cw), next_map),
        pl.BlockSpec((tm, fw), row),
        pl.BlockSpec((tm, N_BRANCHES * d), row),
        lw["conv_w"].spec(),
        lw["w_mla_out"].spec(),
        lw["w_conv_out"].spec(),
        lw["w_four_out"].spec(),
        lw["w_out"].spec(),
        lw["g_post_mix"].spec(),
    ]
    return pl.pallas_call(
        functools.partial(_merge_kernel, seq_len=seq_len, tm=tm, d_model=d),
        grid=(t // tm,),
        in_specs=in_specs,
        out_specs=pl.BlockSpec((tm, d), row),
        out_shape=jax.ShapeDtypeStruct((t, d), F32),
        scratch_shapes=[pltpu.VMEM((tm + 16, cw), F32)],
        compiler_params=_params(1),
        name="merge",
    )(xs, mod.table, o, pr["cb"], pr["cc"], pr["cu"], pr["cc"], pr["cu"], pr["cc"], pr["cu"], four,
      pr["gate"], lw["conv_w"].stacked, lw["w_mla_out"].stacked, lw["w_conv_out"].stacked,
      lw["w_four_out"].stacked, lw["w_out"].stacked, lw["g_post_mix"].stacked)


def _pack_rows(v):
    bits = lax.bitcast_convert_type(v.astype(BF16).astype(F32), jnp.uint32)
    rows = []
    for j in range(v.shape[1] // (2 * PACK_W)):
        lo = bits[:, (2 * j) * PACK_W:(2 * j + 1) * PACK_W]
        hi = bits[:, (2 * j + 1) * PACK_W:(2 * j + 2) * PACK_W]
        rows.append(lax.bitcast_convert_type((hi & jnp.uint32(0xFFFF0000)) | (lo >> 16), jnp.int32))
    return rows


def _unpack_rows(rows):
    parts = []
    for r in rows:
        u = lax.bitcast_convert_type(r, jnp.uint32)
        parts.append(lax.bitcast_convert_type(u << 16, F32))
        parts.append(lax.bitcast_convert_type(u & jnp.uint32(0xFFFF0000), F32))
    return jnp.concatenate(parts, axis=1)


def _route_kernel(x_ref, mod_ref, gpre_ref, wrt_ref, brt_ref, triu_ref,
                  tp_ref, sel_ref, rank_ref, cnt_ref, wgt_ref, *, n_experts):
    shift = mod_ref[0, 3:4, :]
    scale = mod_ref[0, 4:5, :]
    t = _rms(x_ref[...], gpre_ref[...]) * (1.0 + scale) + shift
    t_hi = t.astype(BF16)
    t_lo = (t - t_hi.astype(F32)).astype(BF16)
    for j, r in enumerate(_pack_rows(t)):
        tp_ref[j] = r
    tm = t.shape[0]
    hh = _dot_t(wrt_ref[...], t_hi)
    logits = hh[0:n_experts] + hh[n_experts:2 * n_experts] + _dot_t(wrt_ref[0:n_experts, :], t_lo)
    scores = _sigmoid(logits)
    work = scores + brt_ref[...]
    row = lax.broadcasted_iota(jnp.int32, scores.shape, 0)
    wide = lax.broadcasted_iota(jnp.int32, (LANE, tm), 0)
    firsts, picked, hits = [], [], []
    for k in range(TOP_K):
        best = jnp.max(work, axis=0, keepdims=True)
        first = jnp.min(jnp.where(work == best, row, n_experts), axis=0, keepdims=True)
        hit = row == first
        firsts.append(first)
        picked.append(jnp.sum(jnp.where(hit, scores, 0.0), axis=0, keepdims=True))
        hits.append(wide == first + k * n_experts)
        work = jnp.where(hit, -jnp.inf, work)
    total = functools.reduce(jnp.add, picked)
    onehot = functools.reduce(jnp.add, [jnp.where(h, 1.0, 0.0) for h in hits])
    earlier = _dot(onehot.astype(BF16), triu_ref[...])
    col = jnp.broadcast_to(jnp.sum(onehot, axis=1, keepdims=True), (LANE, LANE))
    row_c = lax.broadcasted_iota(jnp.int32, (LANE, LANE), 0)
    before = jnp.zeros((LANE, LANE), F32)
    for s in range(1, TOP_K):
        before = before + jnp.where(row_c >= s * n_experts, pltpu.roll(col, s * n_experts, 0), 0.0)
    ahead = earlier + before[:, 0:1]
    row8 = lax.broadcasted_iota(jnp.int32, (8, tm), 0)
    sel = jnp.zeros((8, tm), jnp.int32)
    rank = jnp.zeros((8, tm), F32)
    wgt_t = jnp.zeros((LANE, tm), F32)
    for k in range(TOP_K):
        sel = jnp.where(row8 == k, firsts[k], sel)
        rank = jnp.where(row8 == k, jnp.sum(jnp.where(hits[k], ahead, 0.0), axis=0, keepdims=True), rank)
        wgt_t = jnp.where(wide == k, picked[k] / total * ROUTED_SCALE, wgt_t)
    sel_ref[0] = sel
    rank_ref[0] = rank.astype(jnp.int32)
    cnt_ref[0] = col.astype(jnp.int32)
    wgt_ref[...] = wgt_t.T


def _route(x1, mod, seq_len, lw, *, tm):
    t, d = x1.shape
    n_tiles = t // tm
    tiles_per_seq = max(seq_len // tm, 1)
    n_experts = lw["b_router_t"].shape[0]
    rows = d // (2 * PACK_W)
    triu = jnp.tri(tm, tm, -1, dtype=BF16).T
    tile3 = lambda i: (i, 0, 0)
    return pl.pallas_call(
        functools.partial(_route_kernel, n_experts=n_experts),
        grid=(n_tiles,),
        in_specs=[
            pl.BlockSpec((tm, d), lambda i: (i, 0)),
            mod.spec(lambda i: i // tiles_per_seq),
            lw["g_pre_ffn"].spec(),
            lw["w_router_t"].spec(),
            lw["b_router_t"].spec(),
            _resident((tm, tm)),
        ],
        out_specs=[
            pl.BlockSpec((rows, tm, PACK_W), lambda i: (0, i, 0)),
            pl.BlockSpec((1, 8, tm), tile3),
            pl.BlockSpec((1, 8, tm), tile3),
            pl.BlockSpec((1, LANE, LANE), tile3),
            pl.BlockSpec((tm, LANE), lambda i: (i, 0)),
        ],
        out_shape=[
            jax.ShapeDtypeStruct((rows, t, PACK_W), jnp.int32),
            jax.ShapeDtypeStruct((n_tiles, 8, tm), jnp.int32),
            jax.ShapeDtypeStruct((n_tiles, 8, tm), jnp.int32),
            jax.ShapeDtypeStruct((n_tiles, LANE, LANE), jnp.int32),
            jax.ShapeDtypeStruct((t, LANE), F32),
        ],
        compiler_params=_params(1),
        name="moe_route",
    )(x1, mod.table, lw["g_pre_ffn"].stacked, lw["w_router_t"].stacked, lw["b_router_t"].stacked, triu)


def _expert_kernel(te_ref, meta_ref, xs_ref, wg_ref, wu_ref, wd_ref, ys_ref, wg_sc, wu_sc, wd_sc):
    i = pl.program_id(0)
    live = i < meta_ref[0]
    new_expert = (i == 0) | (te_ref[i] != te_ref[jnp.maximum(i - 1, 0)])

    @pl.when(live & new_expert)
    def _():
        wg_sc[...] = wg_ref[0, 0].astype(BF16)
        wu_sc[...] = wu_ref[0, 0].astype(BF16)
        wd_sc[...] = wd_ref[0, 0].astype(BF16)

    @pl.when(live)
    def _():
        x = _unpack_rows([xs_ref[j] for j in range(xs_ref.shape[0])]).astype(BF16)
        gate = _dot(x, wg_sc[...])
        act = (gate * _sigmoid(gate) * _dot(x, wu_sc[...])).astype(BF16)
        for j, r in enumerate(_pack_rows(_dot(act, wd_sc[...]))):
            ys_ref[j] = r


def _experts(xs, tile_expert, meta, w_gate_e, w_up_e, w_down_e, layer, *, tm):
    rows, p, _ = xs.shape
    n_experts, d, f = w_gate_e.shape[1:]

    def slot(i, te, meta):
        return (0, jnp.minimum(i, meta[0] - 1), 0)

    grid_spec = pltpu.PrefetchScalarGridSpec(
        num_scalar_prefetch=2,
        grid=(p // tm,),
        in_specs=[
            pl.BlockSpec((rows, tm, PACK_W), slot),
            pl.BlockSpec((1, 1, d, f), lambda i, te, meta: (layer, te[i], 0, 0)),
            pl.BlockSpec((1, 1, d, f), lambda i, te, meta: (layer, te[i], 0, 0)),
            pl.BlockSpec((1, 1, f, d), lambda i, te, meta: (layer, te[i], 0, 0)),
        ],
        out_specs=pl.BlockSpec((rows, tm, PACK_W), slot),
        scratch_shapes=[pltpu.VMEM((d, f), BF16), pltpu.VMEM((d, f), BF16), pltpu.VMEM((f, d), BF16)],
    )
    return pl.pallas_call(
        _expert_kernel,
        grid_spec=grid_spec,
        out_shape=jax.ShapeDtypeStruct(xs.shape, jnp.int32),
        compiler_params=_params(1),
        name="moe_experts",
    )(tile_expert, meta, xs, w_gate_e, w_up_e, w_down_e)


def _combine_kernel(x_ref, mod_ref, gpre_ref, gpost_ref, wgs_ref, wus_ref, wds_ref, yk_ref, wgt_ref,
                    out_ref):
    shift = mod_ref[0, 3:4, :]
    scale = mod_ref[0, 4:5, :]
    x = x_ref[...]
    t_hi = (_rms(x, gpre_ref[...]) * (1.0 + scale) + shift).astype(BF16)
    gate = _dot(t_hi, wgs_ref[...])
    act = (gate * _sigmoid(gate) * _dot(t_hi, wus_ref[...])).astype(BF16)
    acc = _dot(act, wds_ref[...])
    for k in range(TOP_K):
        y = _unpack_rows([yk_ref[k, j] for j in range(yk_ref.shape[1])])
        acc = acc + wgt_ref[:, k:k + 1] * y
    g2 = mod_ref[0, 5:6, :]
    out_ref[...] = x + g2 * _rms(acc, gpost_ref[...])


def _combine(x1, mod, seq_len, lw, yk, wgt, *, tm):
    t, d = x1.shape
    tiles_per_seq = max(seq_len // tm, 1)
    rows = yk.shape[1]
    row = lambda i: (i, 0)
    return pl.pallas_call(
        _combine_kernel,
        grid=(t // tm,),
        in_specs=[
            pl.BlockSpec((tm, d), row),
            mod.spec(lambda i: i // tiles_per_seq),
            lw["g_pre_ffn"].spec(),
            lw["g_post_ffn"].spec(),
            lw["w_gate_s"].spec(),
            lw["w_up_s"].spec(),
            lw["w_down_s"].spec(),
            pl.BlockSpec((TOP_K, rows, tm, PACK_W), lambda i: (0, 0, i, 0)),
            pl.BlockSpec((tm, LANE), row),
        ],
        out_specs=pl.BlockSpec((tm, d), row),
        out_shape=jax.ShapeDtypeStruct((t, d), F32),
        compiler_params=_params(1),
        name="moe_combine",
    )(x1, mod.table, lw["g_pre_ffn"].stacked, lw["g_post_ffn"].stacked, lw["w_gate_s"].stacked, lw["w_up_s"].stacked,
      lw["w_down_s"].stacked, yk, wgt)


def _sc_mesh():
    return plsc.VectorSubcoreMesh(core_axis_name="core", subcore_axis_name="subcore")


def _sc_scatter_rows(src, idx, n_out):
    n_lists, n = idx.shape
    width = src.shape[1]

    @pl.kernel(out_type=jax.ShapeDtypeStruct((n_out, width), src.dtype), mesh=_sc_mesh(), scratch_types=[])
    def scatter(x_hbm, *refs):
        i_hbms, o_hbm = refs[:n_lists], refs[n_lists]

        def body(x_vmem, *i_vmems):
            for i_vmem in i_vmems:
                pltpu.sync_copy(x_vmem, o_hbm.at[i_vmem.at[0]])

        pltpu.emit_pipeline(
            body,
            grid=(n // SC_WINDOW,),
            in_specs=[pl.BlockSpec((SC_WINDOW, width), lambda i: (i, 0))]
            + [pl.BlockSpec((1, SC_WINDOW), lambda i, r=r: (r, i)) for r in range(n_lists)],
            out_specs=[],
            core_axis_name=("core", "subcore"),
            dimension_semantics=(pltpu.PARALLEL,),
        )(x_hbm, *i_hbms)

    return scatter(src, *([idx] * n_lists))


def _sc_gather_rows(table, idx):
    n = idx.shape[0]
    width = table.shape[1]

    @pl.kernel(out_type=jax.ShapeDtypeStruct((n, width), table.dtype), mesh=_sc_mesh(), scratch_types=[])
    def gather(x_hbm, i_hbm, o_hbm):
        def body(i_vmem, o_vmem):
            pltpu.sync_copy(x_hbm.at[i_vmem.at[0]], o_vmem)

        pltpu.emit_pipeline(
            body,
            grid=(n // SC_WINDOW,),
            in_specs=[pl.BlockSpec((1, SC_WINDOW), lambda i: (0, i))],
            out_specs=[pl.BlockSpec((SC_WINDOW, width), lambda i: (i, 0))],
            core_axis_name=("core", "subcore"),
            dimension_semantics=(pltpu.PARALLEL,),
        )(i_hbm, o_hbm)

    return gather(table, idx.reshape(1, n))


def _moe_sparse(x1, mod, seq_len, lw, w_gate_e, w_up_e, w_down_e, layer, *, tm, tm_e):
    t = x1.shape[0]
    n_experts = lw["b_router_t"].shape[0]
    assert n_experts * TOP_K == LANE
    n_tok_tiles = t // tm
    n_row_tiles = (t * TOP_K) // tm_e + n_experts
    p = n_row_tiles * tm_e
    experts = jnp.arange(n_experts, dtype=jnp.int32)

    tp, sel, rank, cnt, wgt = _route(x1, mod, seq_len, lw, tm=tm)
    rows = tp.shape[0]
    cnt = cnt[:, :, 0].reshape(n_tok_tiles, TOP_K, n_experts).sum(axis=1)
    padded = (cnt.sum(axis=0) + tm_e - 1) // tm_e * tm_e
    group_end = jnp.cumsum(padded)
    base = (group_end - padded)[None, :] + jnp.cumsum(cnt, axis=0) - cnt
    chosen = sel[:, :TOP_K, :, None] == experts
    pos = jnp.sum(jnp.where(chosen, base[:, None, None, :], 0), axis=-1) + rank[:, :TOP_K, :]
    pos = pos.transpose(1, 0, 2).reshape(TOP_K, t)

    n_used = group_end[-1] // tm_e
    tile_start = jnp.arange(n_row_tiles, dtype=jnp.int32) * tm_e
    tile_expert = jnp.sum(tile_start[:, None] >= group_end[None, :], axis=1).astype(jnp.int32)
    tile_expert = jnp.minimum(tile_expert, n_experts - 1)
    tile_expert = jnp.where(tile_start < group_end[-1], tile_expert, tile_expert[n_used - 1])
    meta = jnp.stack([n_used, n_used]).astype(jnp.int32)

    idx = pos[:, None, :] + (jnp.arange(rows, dtype=jnp.int32) * p)[None, :, None]
    xs = _sc_scatter_rows(tp.reshape(rows * t, PACK_W), idx.reshape(TOP_K, rows * t), rows * p)
    ys = _experts(xs.reshape(rows, p, PACK_W), tile_expert, meta, w_gate_e, w_up_e, w_down_e, layer, tm=tm_e)
    yk = _sc_gather_rows(ys.reshape(rows * p, PACK_W), idx.reshape(-1))
    return _combine(x1, mod, seq_len, lw, yk.reshape(TOP_K, rows, t, PACK_W), wgt, tm=tm)


def _rope_tables(n):
    rows = n // GRID_W
    r, col = jnp.meshgrid(jnp.arange(rows), jnp.arange(GRID_W), indexing="ij")
    r = r.reshape(-1).astype(F32)
    col = col.reshape(-1).astype(F32)
    pairs = QK_ROPE // 4
    inv = ROPE_BASE ** (-jnp.arange(pairs, dtype=F32) / pairs)
    ang = jnp.concatenate([r[:, None] * inv, col[:, None] * inv], axis=-1)
    cos, sin = jnp.cos(ang), jnp.sin(ang)
    pad = HEAD_PAD - QK_NOPE - QK_ROPE
    cos_t = jnp.concatenate([jnp.ones((n, QK_NOPE), F32), cos, cos, jnp.zeros((n, pad), F32)], axis=1)
    sin_t = jnp.concatenate([jnp.zeros((n, QK_NOPE), F32), sin, sin, jnp.zeros((n, pad), F32)], axis=1)
    return cos_t, sin_t


def _identity_tables(n):
    pad = HEAD_PAD - QK_NOPE - QK_ROPE
    cos_t = jnp.concatenate([jnp.ones((n, QK_NOPE + QK_ROPE), F32), jnp.zeros((n, pad), F32)], axis=1)
    return cos_t, jnp.zeros((n, HEAD_PAD), F32)


def _position_dft(n):
    nb = 64 if n % 64 == 0 else 1
    na = n // nb
    m = jnp.arange(n, dtype=jnp.int32)[None, :]
    ang_a = ((jnp.arange(na, dtype=jnp.int32)[:, None] * m) % na).astype(F32) * (2.0 * np.pi / na)
    ang_b = ((jnp.arange(nb, dtype=jnp.int32)[:, None] * m) % n).astype(F32) * (2.0 * np.pi / n)
    ca, sa = jnp.cos(ang_a)[:, None, :], jnp.sin(ang_a)[:, None, :]
    cb, sb = jnp.cos(ang_b)[None, :, :], jnp.sin(ang_b)[None, :, :]
    norm = 1.0 / np.sqrt(n)
    cos = ((ca * cb - sa * sb) * norm).reshape(n, n)
    sin = ((sa * cb + ca * sb) * (-norm)).reshape(n, n)
    return jnp.concatenate([cos, sin], axis=1).astype(BF16)


def _channel_dft(width):
    gc = width // FOURIER_GROUPS
    idx = (jnp.arange(gc, dtype=jnp.int32)[:, None] * jnp.arange(gc, dtype=jnp.int32)[None, :]) % gc
    ang = idx.astype(F32) * (2.0 * np.pi / gc)
    eye = jnp.eye(FOURIER_GROUPS, dtype=F32)
    norm = 1.0 / np.sqrt(gc)
    return jnp.concatenate([jnp.kron(eye, jnp.cos(ang) * norm), jnp.kron(eye, jnp.sin(ang) * norm)],
                           axis=1).astype(BF16)


def _w1_kernel(w_ref, o_ref, *, kv_rank):
    cols = w_ref.shape[2]
    kv_end = kv_rank + QK_ROPE
    half = QK_ROPE // 2
    tail = HEAD_PAD - QK_NOPE - QK_ROPE
    o_kpe = kv_rank
    o_rot = o_kpe + HEAD_PAD
    o_rest = o_rot + HEAD_PAD
    dt = o_ref.dtype
    o_ref[0, 0:kv_rank, :] = w_ref[0, 0:kv_rank, :].astype(dt)
    for base in (o_kpe, o_rot):
        o_ref[0, base:base + QK_NOPE, :] = jnp.zeros((QK_NOPE, cols), dt)
        o_ref[0, base + QK_NOPE + QK_ROPE:base + HEAD_PAD, :] = jnp.zeros((tail, cols), dt)
    o_ref[0, o_kpe + QK_NOPE:o_kpe + QK_NOPE + QK_ROPE, :] = w_ref[0, kv_rank:kv_end, :].astype(dt)
    o_ref[0, o_rot + QK_NOPE:o_rot + QK_NOPE + half, :] = (-w_ref[0, kv_rank + half:kv_end, :]).astype(dt)
    o_ref[0, o_rot + QK_NOPE + half:o_rot + QK_NOPE + QK_ROPE, :] = w_ref[0, kv_rank:kv_rank + half, :].astype(dt)
    o_ref[0, o_rest:, :] = w_ref[0, kv_end:, :].astype(dt)


def _prep_w1(w_in_t, kv_rank):
    n_layers, width, d = w_in_t.shape
    out_w = width - QK_ROPE + 2 * HEAD_PAD
    tc = _tile(d, 256)
    return pl.pallas_call(
        functools.partial(_w1_kernel, kv_rank=kv_rank),
        grid=(n_layers, d // tc),
        in_specs=[pl.BlockSpec((1, width, tc), lambda l, i: (l, 0, i))],
        out_specs=pl.BlockSpec((1, out_w, tc), lambda l, i: (l, 0, i)),
        out_shape=jax.ShapeDtypeStruct((n_layers, out_w, d), BF16),
        compiler_params=_params(2),
        name="prep_w1",
    )(w_in_t)


def _stacked_weights(g_pre_mix, g_post_mix, g_pre_ffn, g_post_ffn, b_gate, g_q, w_uq, g_kv, w_ukv,
                     w_mla_out, conv_w, w_conv_out, w_four_out, w_out, w_router, b_router, w_gate_s,
                     w_up_s, w_down_s):
    n_layers, kv_rank = g_kv.shape
    q_rank = g_q.shape[1]
    qk_dim = QK_NOPE + QK_ROPE
    uq = w_uq.reshape(n_layers, q_rank, N_HEADS, qk_dim) * (qk_dim ** -0.5 * np.log2(np.e))
    zq = jnp.zeros((n_layers, q_rank, N_HEADS, HEAD_PAD - qk_dim), F32)
    wuq = jnp.concatenate([uq, zq], axis=-1).reshape(n_layers, q_rank, N_HEADS * HEAD_PAD).astype(BF16)
    ukv = w_ukv.reshape(n_layers, kv_rank, N_HEADS, QK_NOPE + V_DIM)
    zk = jnp.zeros((n_layers, kv_rank, N_HEADS, HEAD_PAD - QK_NOPE), F32)
    zv = jnp.zeros((n_layers, kv_rank, N_HEADS, HEAD_PAD - V_DIM), F32)
    wuk = jnp.concatenate([ukv[..., :QK_NOPE], zk], axis=-1).reshape(n_layers, kv_rank, -1).astype(BF16)
    wuv = jnp.concatenate([ukv[..., QK_NOPE:], zv], axis=-1).reshape(n_layers, kv_rank, -1).astype(BF16)
    wr_hi = w_router.astype(BF16)
    wr_lo = (w_router - wr_hi.astype(F32)).astype(BF16)
    wr_t = jnp.concatenate([jnp.swapaxes(wr_hi, 1, 2), jnp.swapaxes(wr_lo, 1, 2)], axis=1)
    return {
        "g_pre_mix": g_pre_mix[:, None], "g_post_mix": g_post_mix[:, None],
        "g_pre_ffn": g_pre_ffn[:, None], "g_post_ffn": g_post_ffn[:, None],
        "b_gate": b_gate[:, None], "g_q": g_q[:, None], "g_kv": g_kv[:, None],
        "wuq": wuq, "wuk": wuk, "wuv": wuv, "conv_w": conv_w,
        "w_mla_out": w_mla_out.astype(BF16), "w_conv_out": w_conv_out.astype(BF16),
        "w_four_out": w_four_out.astype(BF16), "w_out": w_out.astype(BF16),
        "w_router_t": wr_t, "b_router_t": b_router[:, :, None],
        "w_gate_s": w_gate_s.astype(BF16), "w_up_s": w_up_s.astype(BF16), "w_down_s": w_down_s.astype(BF16),
    }


def _tile(n, pref):
    return pref if n % pref == 0 else n


def kernel(x, c, ctx, c_ctx, w_ada, b_ada, g_pre_mix, g_post_mix, g_pre_ffn, g_post_ffn, w_in, b_gate,
           g_q, w_uq, g_kv, w_ukv, w_mla_out, conv_w, w_conv_out, w_four_out, w_out, w_router, b_router,
           w_gate_e, w_up_e, w_down_e, w_gate_s, w_up_s, w_down_s):
    batch, seq, d = x.shape
    n_ctx = ctx.shape[1]
    n_layers = w_in.shape[0]
    xs = x.reshape(batch * seq, d)
    cs = ctx.reshape(batch * n_ctx, d)

    mod_rows = -(-(batch + 1) // 16) * 16
    c_all = jnp.concatenate([c, c_ctx[None], jnp.zeros((mod_rows - batch - 1, d), F32)], axis=0)
    ada = _ada(c_all, w_ada, b_ada).reshape(n_layers, mod_rows, 6, d)

    tm_x = _tile(seq, 512)
    tm_x_wide = _tile(seq, 2 * MERGE_ROWS)
    tq_x = _tile(seq, 2 * ATTN_ROWS)
    tm_c = _tile(n_ctx, 256)
    tm_c_mid = _tile(batch * n_ctx, 512)
    tm_c_wide = _tile(batch * n_ctx, 1024)
    te_x, te_c = 1024, 512

    tab_x = _rope_tables(seq)
    tab_c = _identity_tables(tm_c_mid)
    cs_x = _position_dft(seq)
    cs_c = _position_dft(n_ctx)
    dc = _channel_dft(w_four_out.shape[1])
    w1 = _prep_w1(jnp.swapaxes(w_in, 1, 2), g_kv.shape[1])
    stacked = _stacked_weights(g_pre_mix, g_post_mix, g_pre_ffn, g_post_ffn, b_gate, g_q, w_uq, g_kv, w_ukv,
                               w_mla_out, conv_w, w_conv_out, w_four_out, w_out, w_router, b_router,
                               w_gate_s, w_up_s, w_down_s)
    v_one = jnp.tile((jnp.arange(HEAD_PAD) == V_DIM).astype(F32), N_HEADS)[None]

    for l in range(n_layers):
        last = l == n_layers - 1
        lw = {name: _Layer(arr, l) for name, arr in stacked.items()}
        lw.update(w1=w1, layer=l, dc=dc, v_one=v_one)
        mod_x = _Mod(ada, l, 0, batch)
        mod_c = _Mod(ada, l, batch, 1)

        pc = _inproj(cs, mod_c, n_ctx, lw, tab_c, kv_only=last, tm=tm_c_mid)
        px = _inproj(xs, mod_x, seq, lw, tab_x, kv_only=False, tm=tm_x)
        o_x = _attention(px["q"], [(pc["k"], pc["v"], n_ctx), (px["k"], px["v"], seq)], batch, seq, tq=tq_x)
        f_x = _fourier(px["ab"], cs_x, batch, seq, tn=seq)
        x1 = _merge(xs, mod_x, seq, px, o_x, f_x, lw, tm=tm_x_wide)
        xs = _moe_sparse(x1, mod_x, seq, lw, w_gate_e, w_up_e, w_down_e, l, tm=tm_x_wide, tm_e=te_x)
        if not last:
            o_c = _attention(pc["q"], [(pc["k"], pc["v"], n_ctx)], batch, n_ctx, tq=tm_c)
            f_c = _fourier(pc["ab"], cs_c, batch, n_ctx, tn=tm_c)
            c1 = _merge(cs, mod_c, n_ctx, pc, o_c, f_c, lw, tm=tm_c_wide)
            cs = _moe_sparse(c1, mod_c, batch * n_ctx, lw, w_gate_e, w_up_e, w_down_e, l, tm=tm_c_wide,
                             tm_e=te_c)
    return xs.reshape(batch, seq, d)
```

```python
import functools

import numpy as np
import jax
import jax.numpy as jnp
from jax import lax
from jax.experimental import pallas as pl
from jax.experimental.pallas import tpu as pltpu
from jax.experimental.pallas import tpu_sc as plsc

N_HEADS = 8
QK_NOPE = 64
QK_ROPE = 32
V_DIM = 64
GRID_W = 64
ROPE_BASE = 10000.0
FOURIER_GROUPS = 4
TOP_K = 4
ROUTED_SCALE = 2.5
N_BRANCHES = 3
EPS = 1e-6

LANE = 128
HEAD_PAD = LANE
VMEM_LIMIT = 56 * 1024 * 1024
PACK_W = 256
SC_WINDOW = 128
ATTN_ROWS = 512
MERGE_ROWS = 512

F32 = jnp.float32
BF16 = jnp.bfloat16


def _rms(x, g):
    return x * lax.rsqrt(jnp.mean(x * x, axis=-1, keepdims=True) + EPS) * g


def _sigmoid(x):
    return 1.0 / (1.0 + jnp.exp(-x))


def _dot(a, b):
    return jnp.dot(a, b, preferred_element_type=F32)


def _dot_t(a, b_t):
    return lax.dot_general(a, b_t, (((1,), (1,)), ((), ())), preferred_element_type=F32)


def _skewed(stages, n_chunks, chunk):
    live = [None] * n_chunks
    for step in range(n_chunks + len(stages) - 1):
        for c in range(n_chunks):
            s = step - c
            if 0 <= s < len(stages):
                live[c] = stages[s](c * chunk) if s == 0 else stages[s](c * chunk, live[c])


def _resident(shape):
    nd = len(shape)
    return pl.BlockSpec(shape, lambda *_: (0,) * nd, pipeline_mode=pl.Buffered(1))


class _Layer:
    def __init__(self, stacked, index):
        self.stacked, self.index = stacked, index
        self.shape = stacked.shape[1:]

    def spec(self):
        index = (self.index,) + (0,) * len(self.shape)
        return pl.BlockSpec((None,) + self.shape, lambda *_: index, pipeline_mode=pl.Buffered(1))


class _Mod:
    def __init__(self, table, layer, row0, n):
        self.table, self.layer, self.row0, self.n = table, layer, row0, n

    def spec(self, seq_of_step):
        layer, row0, n = self.layer, self.row0, self.n
        return pl.BlockSpec((None, 1) + self.table.shape[2:],
                            lambda i: (layer, row0 + seq_of_step(i) % n, 0, 0))


def _params(n_grid):
    return pltpu.CompilerParams(dimension_semantics=("arbitrary",) * n_grid,
                                vmem_limit_bytes=VMEM_LIMIT)


def _ada_kernel(c_ref, w_ref, b_ref, o_ref):
    c = c_ref[...]
    a = (c * _sigmoid(c)).astype(BF16)
    o_ref[0] = _dot(a, w_ref[0].astype(BF16)) + b_ref[0]


def _ada(c_all, w_ada, b_ada):
    n_layers, d, n_out = w_ada.shape
    rows = c_all.shape[0]
    tn = _tile(n_out, 1536)
    return pl.pallas_call(
        _ada_kernel,
        grid=(n_layers, n_out // tn),
        in_specs=[
            pl.BlockSpec((rows, d), lambda l, j: (0, 0)),
            pl.BlockSpec((1, d, tn), lambda l, j: (l, 0, j)),
            pl.BlockSpec((1, 1, tn), lambda l, j: (l, 0, j)),
        ],
        out_specs=pl.BlockSpec((1, rows, tn), lambda l, j: (l, 0, j)),
        out_shape=jax.ShapeDtypeStruct((n_layers, rows, n_out), F32),
        compiler_params=_params(2),
        name="ada",
    )(c_all, w_ada, b_ada.reshape(n_layers, 1, n_out))


def _inproj_kernel(*refs, kv_only, kv_rank, q_rank, conv_w, four_w, d_model):
    if kv_only:
        (x_ref, mod_ref, gpre_ref, w1_ref, gkv_ref, wuk_ref, wuv_ref, vone_ref, cos_ref, sin_ref,
         k_ref, v_ref) = refs
    else:
        (x_ref, mod_ref, gpre_ref, w1_ref, gkv_ref, wuk_ref, wuv_ref, vone_ref, cos_ref, sin_ref,
         bg_ref, gq_ref, wuq_ref, dc_ref,
         k_ref, v_ref, q_ref, cb_ref, cc_ref, cu_ref, ab_ref, gate_ref) = refs

    x = x_ref[...]
    shift = mod_ref[0, 0:1, :]
    scale = mod_ref[0, 1:2, :]
    h = (_rms(x, gpre_ref[...]) * (1.0 + scale) + shift).astype(BF16)
    cos = cos_ref[...]
    sin = sin_ref[...]

    o_kpe = kv_rank
    o_rot = o_kpe + HEAD_PAD
    o_q = o_rot + HEAD_PAD
    p = _dot_t(h, w1_ref[0, 0:o_q, :])
    ckv = _rms(p[:, 0:kv_rank], gkv_ref[...]).astype(BF16)
    kpe = p[:, o_kpe:o_rot] * cos + p[:, o_rot:o_q] * sin
    k = _dot(ckv, wuk_ref[...]) + jnp.concatenate([kpe] * N_HEADS, axis=1)
    k_ref[...] = k.astype(k_ref.dtype)
    v_ref[...] = (_dot(ckv, wuv_ref[...]) + vone_ref[...]).astype(v_ref.dtype)
    if kv_only:
        return

    o_cb = o_q + q_rank
    cq = _rms(_dot_t(h, w1_ref[0, o_q:o_cb, :]), gq_ref[...]).astype(BF16)
    half = QK_ROPE // 2
    lane = lax.broadcasted_iota(jnp.int32, (1, HEAD_PAD), 1)
    first = (lane >= QK_NOPE) & (lane < QK_NOPE + half)
    cos_h = jnp.concatenate([cos] * N_HEADS, axis=1)
    sin_h = jnp.concatenate([jnp.where(first, -sin, sin)] * N_HEADS, axis=1)
    first_h = jnp.concatenate([first] * N_HEADS, axis=1)
    lin = _dot(cq, wuq_ref[...])
    width = lin.shape[1]
    partner = jnp.where(first_h, pltpu.roll(lin, width - half, 1), pltpu.roll(lin, half, 1))
    q_ref[...] = (lin * cos_h + partner * sin_h).astype(q_ref.dtype)

    o_cc = o_cb + conv_w
    o_cu = o_cc + conv_w
    o_four = o_cu + conv_w
    cb_ref[...] = _dot_t(h, w1_ref[0, o_cb:o_cc, :]).astype(cb_ref.dtype)
    cc_ref[...] = _dot_t(h, w1_ref[0, o_cc:o_cu, :]).astype(cc_ref.dtype)
    cu_ref[...] = _dot_t(h, w1_ref[0, o_cu:o_four, :]).astype(cu_ref.dtype)

    o_gate = o_four + four_w
    uf = _dot_t(h, w1_ref[0, o_four:o_gate, :]).astype(BF16)
    ab_ref[...] = _dot(uf, dc_ref[...]).astype(ab_ref.dtype)

    for j in range(N_BRANCHES):
        lo = o_gate + j * d_model
        z = _dot_t(h, w1_ref[0, lo:lo + d_model, :]) + bg_ref[:, j * d_model:(j + 1) * d_model]
        gate_ref[:, j * d_model:(j + 1) * d_model] = _sigmoid(z).astype(gate_ref.dtype)


def _inproj(xs, mod, seq_len, lw, tables, *, kv_only, tm):
    t, d = xs.shape
    cos_t, sin_t = tables
    table_tiles = cos_t.shape[0] // tm
    kv_rank = lw["g_kv"].shape[1]
    q_rank = lw["g_q"].shape[1]
    conv_w = lw["conv_w"].shape[1]
    four_w = lw["dc"].shape[0]
    n_k = N_HEADS * HEAD_PAD

    def row(i):
        return (i, 0)

    def tab_map(i):
        return (i % table_tiles, 0)

    w1, layer = lw["w1"], lw["layer"]
    w1_rows = kv_rank + 2 * HEAD_PAD if kv_only else w1.shape[1]
    in_specs = [
        pl.BlockSpec((tm, d), row),
        mod.spec(lambda i: i * tm // seq_len),
        lw["g_pre_mix"].spec(),
        pl.BlockSpec((1, w1_rows, d), lambda i: (layer, 0, 0), pipeline_mode=pl.Buffered(1)),
        lw["g_kv"].spec(),
        lw["wuk"].spec(),
        lw["wuv"].spec(),
        _resident(lw["v_one"].shape),
        pl.BlockSpec((tm, HEAD_PAD), tab_map),
        pl.BlockSpec((tm, HEAD_PAD), tab_map),
    ]
    args = [xs, mod.table, lw["g_pre_mix"].stacked, w1, lw["g_kv"].stacked, lw["wuk"].stacked, lw["wuv"].stacked,
            lw["v_one"], cos_t, sin_t]
    out_shape = [jax.ShapeDtypeStruct((t, n_k), BF16), jax.ShapeDtypeStruct((t, n_k), BF16)]
    out_specs = [pl.BlockSpec((tm, n_k), row), pl.BlockSpec((tm, n_k), row)]
    if not kv_only:
        in_specs += [
            lw["b_gate"].spec(),
            lw["g_q"].spec(),
            lw["wuq"].spec(),
            _resident(lw["dc"].shape),
        ]
        args += [lw["b_gate"].stacked, lw["g_q"].stacked, lw["wuq"].stacked, lw["dc"]]
        widths = [n_k, conv_w, conv_w, conv_w, 2 * four_w, N_BRANCHES * d]
        out_shape += [jax.ShapeDtypeStruct((t, w), BF16) for w in widths]
        out_specs += [pl.BlockSpec((tm, w), row) for w in widths]
    outs = pl.pallas_call(
        functools.partial(_inproj_kernel, kv_only=kv_only, kv_rank=kv_rank, q_rank=q_rank,
                          conv_w=conv_w, four_w=four_w, d_model=d),
        grid=(t // tm,),
        in_specs=in_specs,
        out_specs=out_specs,
        out_shape=out_shape,
        compiler_params=_params(1),
        name="inproj_kv" if kv_only else "inproj",
    )(*args)
    names = ["k", "v", "q", "cb", "cc", "cu", "ab", "gate"]
    return dict(zip(names, outs))


def _attn_kernel(*refs, n_seg):
    q_ref = refs[0]
    o_ref = refs[-1]
    chunk = min(q_ref.shape[0], ATTN_ROWS)
    for c in range(q_ref.shape[0] // chunk):
        rows = slice(c * chunk, (c + 1) * chunk)
        outs = []
        for hh in range(N_HEADS):
            head = slice(hh * HEAD_PAD, (hh + 1) * HEAD_PAD)
            qh = q_ref[rows, head]
            s = [_dot_t(qh, refs[1 + 2 * i][:, head]) for i in range(n_seg)]
            m = functools.reduce(jnp.maximum, [jnp.max(si, axis=-1, keepdims=True) for si in s])
            acc = functools.reduce(jnp.add, [
                _dot(jnp.exp2((s[i] - m).astype(BF16)), refs[2 + 2 * i][:, head]) for i in range(n_seg)])
            outs.append(acc[:, 0:V_DIM] / acc[:, V_DIM:V_DIM + 1])
        o_ref[rows, :] = jnp.concatenate(outs, axis=1).astype(o_ref.dtype)


def _attention(q, segs, batch, seq_q, *, tq):
    t = q.shape[0]
    qt = seq_q // tq
    n_k = N_HEADS * HEAD_PAD
    in_specs = [pl.BlockSpec((tq, n_k), lambda b, j: (b * qt + j, 0))]
    args = [q]
    for k, v, m in segs:
        in_specs.append(pl.BlockSpec((m, n_k), lambda b, j: (b, 0)))
        in_specs.append(pl.BlockSpec((m, n_k), lambda b, j: (b, 0)))
        args += [k, v]
    return pl.pallas_call(
        functools.partial(_attn_kernel, n_seg=len(segs)),
        grid=(batch, qt),
        in_specs=in_specs,
        out_specs=pl.BlockSpec((tq, N_HEADS * V_DIM), lambda b, j: (b * qt + j, 0)),
        out_shape=jax.ShapeDtypeStruct((t, N_HEADS * V_DIM), BF16),
        compiler_params=_params(2),
        name="attention",
    )(*args)


def _four_kernel(cs_ref, ab_ref, o_ref, *, n, fw):
    o = _dot(cs_ref[:, 0:n], ab_ref[:, 0:fw]) + _dot(cs_ref[:, n:2 * n], ab_ref[:, fw:2 * fw])
    o_ref[...] = o.astype(o_ref.dtype)


def _fourier(ab, cs, batch, seq_len, *, tn):
    t, fw2 = ab.shape
    fw = fw2 // 2
    nt = seq_len // tn
    cs_spec = (_resident(cs.shape) if nt == 1
               else pl.BlockSpec((tn, 2 * seq_len), lambda b, j: (j, 0)))
    return pl.pallas_call(
        functools.partial(_four_kernel, n=seq_len, fw=fw),
        grid=(batch, nt),
        in_specs=[
            cs_spec,
            pl.BlockSpec((seq_len, fw2), lambda b, j: (b, 0)),
        ],
        out_specs=pl.BlockSpec((tn, fw), lambda b, j: (b * nt + j, 0)),
        out_shape=jax.ShapeDtypeStruct((t, fw), BF16),
        compiler_params=_params(2),
        name="fourier",
    )(cs, ab)


def _merge_kernel(x_ref, mod_ref, o_ref, cb_ref, cc_ref, cu_ref, ccp_ref, cup_ref, ccn_ref,
                  cun_ref, f_ref, gate_ref, convw_ref, wmo_ref, wco_ref, wfo_ref, wout_ref,
                  gpost_ref, out_ref, pad_ref, *, seq_len, tm, d_model):
    i = pl.program_id(0)
    has_prev = ((i * tm) % seq_len != 0).astype(F32)
    has_next = (((i + 1) * tm) % seq_len != 0).astype(F32)
    pad_ref[0:8, :] = ccp_ref[...].astype(F32) * cup_ref[...].astype(F32) * has_prev
    pad_ref[8:8 + tm, :] = cc_ref[...].astype(F32) * cu_ref[...].astype(F32)
    pad_ref[8 + tm:16 + tm, :] = ccn_ref[...].astype(F32) * cun_ref[...].astype(F32) * has_next
    d = d_model
    g1 = mod_ref[0, 2:3, :]
    chunk = min(tm, MERGE_ROWS)

    def conv_stage(r0):
        before = pad_ref[7 + r0:7 + r0 + chunk, :]
        after = pad_ref[9 + r0:9 + r0 + chunk, :]
        if tm > seq_len:
            at = (lax.broadcasted_iota(jnp.int32, (chunk, 1), 0) + r0) % seq_len
            before = jnp.where(at == 0, 0.0, before)
            after = jnp.where(at == seq_len - 1, 0.0, after)
        conv = (before * convw_ref[0:1, :] + pad_ref[8 + r0:8 + r0 + chunk, :] * convw_ref[1:2, :]
                + after * convw_ref[2:3, :])
        return (cb_ref[r0:r0 + chunk, :].astype(F32) * conv).astype(BF16)

    def branch_stage(r0, conv_in):
        return (_dot(o_ref[r0:r0 + chunk, :], wmo_ref[...]), _dot(conv_in, wco_ref[...]),
                _dot(f_ref[r0:r0 + chunk, :], wfo_ref[...]))

    def gate_stage(r0, ys):
        rows = slice(r0, r0 + chunk)
        return (gate_ref[rows, 0:d].astype(F32) * ys[0] + gate_ref[rows, d:2 * d].astype(F32) * ys[1]
                + gate_ref[rows, 2 * d:3 * d].astype(F32) * ys[2]).astype(BF16)

    def out_stage(r0, merged):
        return _dot(merged, wout_ref[...])

    def tail_stage(r0, y):
        rows = slice(r0, r0 + chunk)
        out_ref[rows, :] = x_ref[rows, :] + g1 * _rms(y, gpost_ref[...])

    _skewed((conv_stage, branch_stage, gate_stage, out_stage, tail_stage), tm // chunk, chunk)


def _merge(xs, mod, seq_len, pr, o, four, lw, *, tm):
    t, d = xs.shape
    assert tm % seq_len == 0 or seq_len % tm == 0
    cw = lw["conv_w"].shape[1]
    fw = four.shape[1]
    hb = tm // 8
    last_hb = t // 8 - 1

    def row(i):
        return (i, 0)

    def prev_map(i):
        return (jnp.maximum(i * hb - 1, 0), 0)

    def next_map(i):
        return (jnp.minimum((i + 1) * hb, last_hb), 0)

    in_specs = [
        pl.BlockSpec((tm, d), row),
        mod.spec(lambda i: i * tm // seq_len),
        pl.BlockSpec((tm, o.shape[1]), row),
        pl.BlockSpec((tm, cw), row),
        pl.BlockSpec((tm, cw), row),
        pl.BlockSpec((tm, cw), row),
        pl.BlockSpec((8, cw), prev_map),
        pl.BlockSpec((8, cw), prev_map),
        pl.BlockSpec((8, cw), next_map),
        pl.BlockSpec((8, cw), next_map),
        pl.BlockSpec((tm, fw), row),
        pl.BlockSpec((tm, N_BRANCHES * d), row),
        lw["conv_w"].spec(),
        lw["w_mla_out"].spec(),
        lw["w_conv_out"].spec(),
        lw["w_four_out"].spec(),
        lw["w_out"].spec(),
        lw["g_post_mix"].spec(),
    ]
    return pl.pallas_call(
        functools.partial(_merge_kernel, seq_len=seq_len, tm=tm, d_model=d),
        grid=(t // tm,),
        in_specs=in_specs,
        out_specs=pl.BlockSpec((tm, d), row),
        out_shape=jax.ShapeDtypeStruct((t, d), F32),
        scratch_shapes=[pltpu.VMEM((tm + 16, cw), F32)],
        compiler_params=_params(1),
        name="merge",
    )(xs, mod.table, o, pr["cb"], pr["cc"], pr["cu"], pr["cc"], pr["cu"], pr["cc"], pr["cu"], four,
      pr["gate"], lw["conv_w"].stacked, lw["w_mla_out"].stacked, lw["w_conv_out"].stacked,
      lw["w_four_out"].stacked, lw["w_out"].stacked, lw["g_post_mix"].stacked)


def _pack_rows(v):
    bits = lax.bitcast_convert_type(v.astype(BF16).astype(F32), jnp.uint32)
    rows = []
    for j in range(v.shape[1] // (2 * PACK_W)):
        lo = bits[:, (2 * j) * PACK_W:(2 * j + 1) * PACK_W]
        hi = bits[:, (2 * j + 1) * PACK_W:(2 * j + 2) * PACK_W]
        rows.append(lax.bitcast_convert_type((hi & jnp.uint32(0xFFFF0000)) | (lo >> 16), jnp.int32))
    return rows


def _unpack_rows(rows):
    parts = []
    for r in rows:
        u = lax.bitcast_convert_type(r, jnp.uint32)
        parts.append(lax.bitcast_convert_type(u << 16, F32))
        parts.append(lax.bitcast_convert_type(u & jnp.uint32(0xFFFF0000), F32))
    return jnp.concatenate(parts, axis=1)


def _route_kernel(x_ref, mod_ref, gpre_ref, wrt_ref, brt_ref, triu_ref,
                  tp_ref, sel_ref, rank_ref, cnt_ref, wgt_ref, *, n_experts):
    shift = mod_ref[0, 3:4, :]
    scale = mod_ref[0, 4:5, :]
    t = _rms(x_ref[...], gpre_ref[...]) * (1.0 + scale) + shift
    t_hi = t.astype(BF16)
    t_lo = (t - t_hi.astype(F32)).astype(BF16)
    for j, r in enumerate(_pack_rows(t)):
        tp_ref[j] = r
    tm = t.shape[0]
    hh = _dot_t(wrt_ref[...], t_hi)
    logits = hh[0:n_experts] + hh[n_experts:2 * n_experts] + _dot_t(wrt_ref[0:n_experts, :], t_lo)
    scores = _sigmoid(logits)
    work = scores + brt_ref[...]
    row = lax.broadcasted_iota(jnp.int32, scores.shape, 0)
    wide = lax.broadcasted_iota(jnp.int32, (LANE, tm), 0)
    firsts, picked, hits = [], [], []
    for k in range(TOP_K):
        best = jnp.max(work, axis=0, keepdims=True)
        first = jnp.min(jnp.where(work == best, row, n_experts), axis=0, keepdims=True)
        hit = row == first
        firsts.append(first)
        picked.append(jnp.sum(jnp.where(hit, scores, 0.0), axis=0, keepdims=True))
        hits.append(wide == first + k * n_experts)
        work = jnp.where(hit, -jnp.inf, work)
    total = functools.reduce(jnp.add, picked)
    onehot = functools.reduce(jnp.add, [jnp.where(h, 1.0, 0.0) for h in hits])
    earlier = _dot(onehot.astype(BF16), triu_ref[...])
    col = jnp.broadcast_to(jnp.sum(onehot, axis=1, keepdims=True), (LANE, LANE))
    row_c = lax.broadcasted_iota(jnp.int32, (LANE, LANE), 0)
    before = jnp.zeros((LANE, LANE), F32)
    for s in range(1, TOP_K):
        before = before + jnp.where(row_c >= s * n_experts, pltpu.roll(col, s * n_experts, 0), 0.0)
    ahead = earlier + before[:, 0:1]
    row8 = lax.broadcasted_iota(jnp.int32, (8, tm), 0)
    sel = jnp.zeros((8, tm), jnp.int32)
    rank = jnp.zeros((8, tm), F32)
    wgt_t = jnp.zeros((LANE, tm), F32)
    for k in range(TOP_K):
        sel = jnp.where(row8 == k, firsts[k], sel)
        rank = jnp.where(row8 == k, jnp.sum(jnp.where(hits[k], ahead, 0.0), axis=0, keepdims=True), rank)
        wgt_t = jnp.where(wide == k, picked[k] / total * ROUTED_SCALE, wgt_t)
    sel_ref[0] = sel
    rank_ref[0] = rank.astype(jnp.int32)
    cnt_ref[0] = col.astype(jnp.int32)
    wgt_ref[...] = wgt_t.T


def _route(x1, mod, seq_len, lw, *, tm):
    t, d = x1.shape
    n_tiles = t // tm
    tiles_per_seq = max(seq_len // tm, 1)
    n_experts = lw["b_router_t"].shape[0]
    rows = d // (2 * PACK_W)
    triu = jnp.tri(tm, tm, -1, dtype=BF16).T
    tile3 = lambda i: (i, 0, 0)
    return pl.pallas_call(
        functools.partial(_route_kernel, n_experts=n_experts),
        grid=(n_tiles,),
        in_specs=[
            pl.BlockSpec((tm, d), lambda i: (i, 0)),
            mod.spec(lambda i: i // tiles_per_seq),
            lw["g_pre_ffn"].spec(),
            lw["w_router_t"].spec(),
            lw["b_router_t"].spec(),
            _resident((tm, tm)),
        ],
        out_specs=[
            pl.BlockSpec((rows, tm, PACK_W), lambda i: (0, i, 0)),
            pl.BlockSpec((1, 8, tm), tile3),
            pl.BlockSpec((1, 8, tm), tile3),
            pl.BlockSpec((1, LANE, LANE), tile3),
            pl.BlockSpec((tm, LANE), lambda i: (i, 0)),
        ],
        out_shape=[
            jax.ShapeDtypeStruct((rows, t, PACK_W), jnp.int32),
            jax.ShapeDtypeStruct((n_tiles, 8, tm), jnp.int32),
            jax.ShapeDtypeStruct((n_tiles, 8, tm), jnp.int32),
            jax.ShapeDtypeStruct((n_tiles, LANE, LANE), jnp.int32),
            jax.ShapeDtypeStruct((t, LANE), F32),
        ],
        compiler_params=_params(1),
        name="moe_route",
    )(x1, mod.table, lw["g_pre_ffn"].stacked, lw["w_router_t"].stacked, lw["b_router_t"].stacked, triu)


def _expert_kernel(te_ref, meta_ref, xs_ref, wg_ref, wu_ref, wd_ref, ys_ref, wg_sc, wu_sc, wd_sc):
    i = pl.program_id(0)
    live = i < meta_ref[0]
    new_expert = (i == 0) | (te_ref[i] != te_ref[jnp.maximum(i - 1, 0)])

    @pl.when(live & new_expert)
    def _():
        wg_sc[...] = wg_ref[0, 0].astype(BF16)
        wu_sc[...] = wu_ref[0, 0].astype(BF16)
        wd_sc[...] = wd_ref[0, 0].astype(BF16)

    @pl.when(live)
    def _():
        x = _unpack_rows([xs_ref[j] for j in range(xs_ref.shape[0])]).astype(BF16)
        gate = _dot(x, wg_sc[...])
        act = (gate * _sigmoid(gate) * _dot(x, wu_sc[...])).astype(BF16)
        for j, r in enumerate(_pack_rows(_dot(act, wd_sc[...]))):
            ys_ref[j] = r


def _experts(xs, tile_expert, meta, w_gate_e, w_up_e, w_down_e, layer, *, tm):
    rows, p, _ = xs.shape
    n_experts, d, f = w_gate_e.shape[1:]

    def slot(i, te, meta):
        return (0, jnp.minimum(i, meta[0] - 1), 0)

    grid_spec = pltpu.PrefetchScalarGridSpec(
        num_scalar_prefetch=2,
        grid=(p // tm,),
        in_specs=[
            pl.BlockSpec((rows, tm, PACK_W), slot),
            pl.BlockSpec((1, 1, d, f), lambda i, te, meta: (layer, te[i], 0, 0)),
            pl.BlockSpec((1, 1, d, f), lambda i, te, meta: (layer, te[i], 0, 0)),
            pl.BlockSpec((1, 1, f, d), lambda i, te, meta: (layer, te[i], 0, 0)),
        ],
        out_specs=pl.BlockSpec((rows, tm, PACK_W), slot),
        scratch_shapes=[pltpu.VMEM((d, f), BF16), pltpu.VMEM((d, f), BF16), pltpu.VMEM((f, d), BF16)],
    )
    return pl.pallas_call(
        _expert_kernel,
        grid_spec=grid_spec,
        out_shape=jax.ShapeDtypeStruct(xs.shape, jnp.int32),
        compiler_params=_params(1),
        name="moe_experts",
    )(tile_expert, meta, xs, w_gate_e, w_up_e, w_down_e)


def _combine_kernel(x_ref, mod_ref, gpre_ref, gpost_ref, wgs_ref, wus_ref, wds_ref, yk_ref, wgt_ref,
                    out_ref):
    shift = mod_ref[0, 3:4, :]
    scale = mod_ref[0, 4:5, :]
    x = x_ref[...]
    t_hi = (_rms(x, gpre_ref[...]) * (1.0 + scale) + shift).astype(BF16)
    gate = _dot(t_hi, wgs_ref[...])
    act = (gate * _sigmoid(gate) * _dot(t_hi, wus_ref[...])).astype(BF16)
    acc = _dot(act, wds_ref[...])
    for k in range(TOP_K):
        y = _unpack_rows([yk_ref[k, j] for j in range(yk_ref.shape[1])])
        acc = acc + wgt_ref[:, k:k + 1] * y
    g2 = mod_ref[0, 5:6, :]
    out_ref[...] = x + g2 * _rms(acc, gpost_ref[...])


def _combine(x1, mod, seq_len, lw, yk, wgt, *, tm):
    t, d = x1.shape
    tiles_per_seq = max(seq_len // tm, 1)
    rows = yk.shape[1]
    row = lambda i: (i, 0)
    return pl.pallas_call(
        _combine_kernel,
        grid=(t // tm,),
        in_specs=[
            pl.BlockSpec((tm, d), row),
            mod.spec(lambda i: i // tiles_per_seq),
            lw["g_pre_ffn"].spec(),
            lw["g_post_ffn"].spec(),
            lw["w_gate_s"].spec(),
            lw["w_up_s"].spec(),
            lw["w_down_s"].spec(),
            pl.BlockSpec((TOP_K, rows, tm, PACK_W), lambda i: (0, 0, i, 0)),
            pl.BlockSpec((tm, LANE), row),
        ],
        out_specs=pl.BlockSpec((tm, d), row),
        out_shape=jax.ShapeDtypeStruct((t, d), F32),
        compiler_params=_params(1),
        name="moe_combine",
    )(x1, mod.table, lw["g_pre_ffn"].stacked, lw["g_post_ffn"].stacked, lw["w_gate_s"].stacked, lw["w_up_s"].stacked,
      lw["w_down_s"].stacked, yk, wgt)


def _sc_mesh():
    return plsc.VectorSubcoreMesh(core_axis_name="core", subcore_axis_name="subcore")


def _sc_scatter_rows(src, idx, n_out):
    n_lists, n = idx.shape
    width = src.shape[1]

    @pl.kernel(out_type=jax.ShapeDtypeStruct((n_out, width), src.dtype), mesh=_sc_mesh(), scratch_types=[])
    def scatter(x_hbm, *refs):
        i_hbms, o_hbm = refs[:n_lists], refs[n_lists]

        def body(x_vmem, *i_vmems):
            for i_vmem in i_vmems:
                pltpu.sync_copy(x_vmem, o_hbm.at[i_vmem.at[0]])

        pltpu.emit_pipeline(
            body,
            grid=(n // SC_WINDOW,),
            in_specs=[pl.BlockSpec((SC_WINDOW, width), lambda i: (i, 0))]
            + [pl.BlockSpec((1, SC_WINDOW), lambda i, r=r: (r, i)) for r in range(n_lists)],
            out_specs=[],
            core_axis_name=("core", "subcore"),
            dimension_semantics=(pltpu.PARALLEL,),
        )(x_hbm, *i_hbms)

    return scatter(src, *([idx] * n_lists))


def _sc_gather_rows(table, idx):
    n = idx.shape[0]
    width = table.shape[1]

    @pl.kernel(out_type=jax.ShapeDtypeStruct((n, width), table.dtype), mesh=_sc_mesh(), scratch_types=[])
    def gather(x_hbm, i_hbm, o_hbm):
        def body(i_vmem, o_vmem):
            pltpu.sync_copy(x_hbm.at[i_vmem.at[0]], o_vmem)

        pltpu.emit_pipeline(
            body,
            grid=(n // SC_WINDOW,),
            in_specs=[pl.BlockSpec((1, SC_WINDOW), lambda i: (0, i))],
            out_specs=[pl.BlockSpec((SC_WINDOW, width), lambda i: (i, 0))],
            core_axis_name=("core", "subcore"),
            dimension_semantics=(pltpu.PARALLEL,),
        )(i_hbm, o_hbm)

    return gather(table, idx.reshape(1, n))


def _moe_sparse(x1, mod, seq_len, lw, w_gate_e, w_up_e, w_down_e, layer, *, tm, tm_e):
    t = x1.shape[0]
    n_experts = lw["b_router_t"].shape[0]
    assert n_experts * TOP_K == LANE
    n_tok_tiles = t // tm
    n_row_tiles = (t * TOP_K) // tm_e + n_experts
    p = n_row_tiles * tm_e
    experts = jnp.arange(n_experts, dtype=jnp.int32)

    tp, sel, rank, cnt, wgt = _route(x1, mod, seq_len, lw, tm=tm)
    rows = tp.shape[0]
    cnt = cnt[:, :, 0].reshape(n_tok_tiles, TOP_K, n_experts).sum(axis=1)
    padded = (cnt.sum(axis=0) + tm_e - 1) // tm_e * tm_e
    group_end = jnp.cumsum(padded)
    base = (group_end - padded)[None, :] + jnp.cumsum(cnt, axis=0) - cnt
    chosen = sel[:, :TOP_K, :, None] == experts
    pos = jnp.sum(jnp.where(chosen, base[:, None, None, :], 0), axis=-1) + rank[:, :TOP_K, :]
    pos = pos.transpose(1, 0, 2).reshape(TOP_K, t)

    n_used = group_end[-1] // tm_e
    tile_start = jnp.arange(n_row_tiles, dtype=jnp.int32) * tm_e
    tile_expert = jnp.sum(tile_start[:, None] >= group_end[None, :], axis=1).astype(jnp.int32)
    tile_expert = jnp.minimum(tile_expert, n_experts - 1)
    tile_expert = jnp.where(tile_start < group_end[-1], tile_expert, tile_expert[n_used - 1])
    meta = jnp.stack([n_used, n_used]).astype(jnp.int32)

    idx = pos[:, None, :] + (jnp.arange(rows, dtype=jnp.int32) * p)[None, :, None]
    xs = _sc_scatter_rows(tp.reshape(rows * t, PACK_W), idx.reshape(TOP_K, rows * t), rows * p)
    yield None
    ys = _experts(xs.reshape(rows, p, PACK_W), tile_expert, meta, w_gate_e, w_up_e, w_down_e, layer, tm=tm_e)
    yk = _sc_gather_rows(ys.reshape(rows * p, PACK_W), idx.reshape(-1))
    yield None
    yield _combine(x1, mod, seq_len, lw, yk.reshape(TOP_K, rows, t, PACK_W), wgt, tm=tm)


def _rope_tables(n):
    rows = n // GRID_W
    r, col = jnp.meshgrid(jnp.arange(rows), jnp.arange(GRID_W), indexing="ij")
    r = r.reshape(-1).astype(F32)
    col = col.reshape(-1).astype(F32)
    pairs = QK_ROPE // 4
    inv = ROPE_BASE ** (-jnp.arange(pairs, dtype=F32) / pairs)
    ang = jnp.concatenate([r[:, None] * inv, col[:, None] * inv], axis=-1)
    cos, sin = jnp.cos(ang), jnp.sin(ang)
    pad = HEAD_PAD - QK_NOPE - QK_ROPE
    cos_t = jnp.concatenate([jnp.ones((n, QK_NOPE), F32), cos, cos, jnp.zeros((n, pad), F32)], axis=1)
    sin_t = jnp.concatenate([jnp.zeros((n, QK_NOPE), F32), sin, sin, jnp.zeros((n, pad), F32)], axis=1)
    return cos_t, sin_t


def _identity_tables(n):
    pad = HEAD_PAD - QK_NOPE - QK_ROPE
    cos_t = jnp.concatenate([jnp.ones((n, QK_NOPE + QK_ROPE), F32), jnp.zeros((n, pad), F32)], axis=1)
    return cos_t, jnp.zeros((n, HEAD_PAD), F32)


def _position_dft(n):
    nb = 64 if n % 64 == 0 else 1
    na = n // nb
    m = jnp.arange(n, dtype=jnp.int32)[None, :]
    ang_a = ((jnp.arange(na, dtype=jnp.int32)[:, None] * m) % na).astype(F32) * (2.0 * np.pi / na)
    ang_b = ((jnp.arange(nb, dtype=jnp.int32)[:, None] * m) % n).astype(F32) * (2.0 * np.pi / n)
    ca, sa = jnp.cos(ang_a)[:, None, :], jnp.sin(ang_a)[:, None, :]
    cb, sb = jnp.cos(ang_b)[None, :, :], jnp.sin(ang_b)[None, :, :]
    norm = 1.0 / np.sqrt(n)
    cos = ((ca * cb - sa * sb) * norm).reshape(n, n)
    sin = ((sa * cb + ca * sb) * (-norm)).reshape(n, n)
    return jnp.concatenate([cos, sin], axis=1).astype(BF16)


def _channel_dft(width):
    gc = width // FOURIER_GROUPS
    idx = (jnp.arange(gc, dtype=jnp.int32)[:, None] * jnp.arange(gc, dtype=jnp.int32)[None, :]) % gc
    ang = idx.astype(F32) * (2.0 * np.pi / gc)
    eye = jnp.eye(FOURIER_GROUPS, dtype=F32)
    norm = 1.0 / np.sqrt(gc)
    return jnp.concatenate([jnp.kron(eye, jnp.cos(ang) * norm), jnp.kron(eye, jnp.sin(ang) * norm)],
                           axis=1).astype(BF16)


def _w1_kernel(w_ref, o_ref, *, kv_rank):
    cols = w_ref.shape[2]
    kv_end = kv_rank + QK_ROPE
    half = QK_ROPE // 2
    tail = HEAD_PAD - QK_NOPE - QK_ROPE
    o_kpe = kv_rank
    o_rot = o_kpe + HEAD_PAD
    o_rest = o_rot + HEAD_PAD
    dt = o_ref.dtype
    o_ref[0, 0:kv_rank, :] = w_ref[0, 0:kv_rank, :].astype(dt)
    for base in (o_kpe, o_rot):
        o_ref[0, base:base + QK_NOPE, :] = jnp.zeros((QK_NOPE, cols), dt)
        o_ref[0, base + QK_NOPE + QK_ROPE:base + HEAD_PAD, :] = jnp.zeros((tail, cols), dt)
    o_ref[0, o_kpe + QK_NOPE:o_kpe + QK_NOPE + QK_ROPE, :] = w_ref[0, kv_rank:kv_end, :].astype(dt)
    o_ref[0, o_rot + QK_NOPE:o_rot + QK_NOPE + half, :] = (-w_ref[0, kv_rank + half:kv_end, :]).astype(dt)
    o_ref[0, o_rot + QK_NOPE + half:o_rot + QK_NOPE + QK_ROPE, :] = w_ref[0, kv_rank:kv_rank + half, :].astype(dt)
    o_ref[0, o_rest:, :] = w_ref[0, kv_end:, :].astype(dt)


def _prep_w1(w_in_t, kv_rank):
    n_layers, width, d = w_in_t.shape
    out_w = width - QK_ROPE + 2 * HEAD_PAD
    tc = _tile(d, 256)
    return pl.pallas_call(
        functools.partial(_w1_kernel, kv_rank=kv_rank),
        grid=(n_layers, d // tc),
        in_specs=[pl.BlockSpec((1, width, tc), lambda l, i: (l, 0, i))],
        out_specs=pl.BlockSpec((1, out_w, tc), lambda l, i: (l, 0, i)),
        out_shape=jax.ShapeDtypeStruct((n_layers, out_w, d), BF16),
        compiler_params=_params(2),
        name="prep_w1",
    )(w_in_t)


def _stacked_weights(g_pre_mix, g_post_mix, g_pre_ffn, g_post_ffn, b_gate, g_q, w_uq, g_kv, w_ukv,
                     w_mla_out, conv_w, w_conv_out, w_four_out, w_out, w_router, b_router, w_gate_s,
                     w_up_s, w_down_s):
    n_layers, kv_rank = g_kv.shape
    q_rank = g_q.shape[1]
    qk_dim = QK_NOPE + QK_ROPE
    uq = w_uq.reshape(n_layers, q_rank, N_HEADS, qk_dim) * (qk_dim ** -0.5 * np.log2(np.e))
    zq = jnp.zeros((n_layers, q_rank, N_HEADS, HEAD_PAD - qk_dim), F32)
    wuq = jnp.concatenate([uq, zq], axis=-1).reshape(n_layers, q_rank, N_HEADS * HEAD_PAD).astype(BF16)
    ukv = w_ukv.reshape(n_layers, kv_rank, N_HEADS, QK_NOPE + V_DIM)
    zk = jnp.zeros((n_layers, kv_rank, N_HEADS, HEAD_PAD - QK_NOPE), F32)
    zv = jnp.zeros((n_layers, kv_rank, N_HEADS, HEAD_PAD - V_DIM), F32)
    wuk = jnp.concatenate([ukv[..., :QK_NOPE], zk], axis=-1).reshape(n_layers, kv_rank, -1).astype(BF16)
    wuv = jnp.concatenate([ukv[..., QK_NOPE:], zv], axis=-1).reshape(n_layers, kv_rank, -1).astype(BF16)
    wr_hi = w_router.astype(BF16)
    wr_lo = (w_router - wr_hi.astype(F32)).astype(BF16)
    wr_t = jnp.concatenate([jnp.swapaxes(wr_hi, 1, 2), jnp.swapaxes(wr_lo, 1, 2)], axis=1)
    return {
        "g_pre_mix": g_pre_mix[:, None], "g_post_mix": g_post_mix[:, None],
        "g_pre_ffn": g_pre_ffn[:, None], "g_post_ffn": g_post_ffn[:, None],
        "b_gate": b_gate[:, None], "g_q": g_q[:, None], "g_kv": g_kv[:, None],
        "wuq": wuq, "wuk": wuk, "wuv": wuv, "conv_w": conv_w,
        "w_mla_out": w_mla_out.astype(BF16), "w_conv_out": w_conv_out.astype(BF16),
        "w_four_out": w_four_out.astype(BF16), "w_out": w_out.astype(BF16),
        "w_router_t": wr_t, "b_router_t": b_router[:, :, None],
        "w_gate_s": w_gate_s.astype(BF16), "w_up_s": w_up_s.astype(BF16), "w_down_s": w_down_s.astype(BF16),
    }


def _tile(n, pref):
    return pref if n % pref == 0 else n


def kernel(x, c, ctx, c_ctx, w_ada, b_ada, g_pre_mix, g_post_mix, g_pre_ffn, g_post_ffn, w_in, b_gate,
           g_q, w_uq, g_kv, w_ukv, w_mla_out, conv_w, w_conv_out, w_four_out, w_out, w_router, b_router,
           w_gate_e, w_up_e, w_down_e, w_gate_s, w_up_s, w_down_s):
    batch, seq, d = x.shape
    n_ctx = ctx.shape[1]
    n_layers = w_in.shape[0]
    xs = x.reshape(batch * seq, d)
    cs = ctx.reshape(batch * n_ctx, d)

    mod_rows = -(-(batch + 1) // 16) * 16
    c_all = jnp.concatenate([c, c_ctx[None], jnp.zeros((mod_rows - batch - 1, d), F32)], axis=0)
    ada = _ada(c_all, w_ada, b_ada).reshape(n_layers, mod_rows, 6, d)

    tm_x = _tile(seq, 512)
    tm_x_wide = _tile(seq, 2 * MERGE_ROWS)
    tq_x = _tile(seq, 2 * ATTN_ROWS)
    tm_c = _tile(n_ctx, 256)
    tm_c_mid = _tile(batch * n_ctx, 512)
    tm_c_wide = _tile(batch * n_ctx, 1024)
    te_x, te_c = 1024, 512

    tab_x = _rope_tables(seq)
    tab_c = _identity_tables(tm_c_mid)
    cs_x = _position_dft(seq)
    cs_c = _position_dft(n_ctx)
    dc = _channel_dft(w_four_out.shape[1])
    w1 = _prep_w1(jnp.swapaxes(w_in, 1, 2), g_kv.shape[1])
    stacked = _stacked_weights(g_pre_mix, g_post_mix, g_pre_ffn, g_post_ffn, b_gate, g_q, w_uq, g_kv, w_ukv,
                               w_mla_out, conv_w, w_conv_out, w_four_out, w_out, w_router, b_router,
                               w_gate_s, w_up_s, w_down_s)
    v_one = jnp.tile((jnp.arange(HEAD_PAD) == V_DIM).astype(F32), N_HEADS)[None]

    for l in range(n_layers):
        last = l == n_layers - 1
        lw = {name: _Layer(arr, l) for name, arr in stacked.items()}
        lw.update(w1=w1, layer=l, dc=dc, v_one=v_one)
        mod_x = _Mod(ada, l, 0, batch)
        mod_c = _Mod(ada, l, batch, 1)

        pc = _inproj(cs, mod_c, n_ctx, lw, tab_c, kv_only=last, tm=tm_c_mid)
        px = _inproj(xs, mod_x, seq, lw, tab_x, kv_only=False, tm=tm_x)
        o_x = _attention(px["q"], [(pc["k"], pc["v"], n_ctx), (px["k"], px["v"], seq)], batch, seq, tq=tq_x)
        f_x = _fourier(px["ab"], cs_x, batch, seq, tn=seq)
        x1 = _merge(xs, mod_x, seq, px, o_x, f_x, lw, tm=tm_x_wide)
        moe_x = _moe_sparse(x1, mod_x, seq, lw, w_gate_e, w_up_e, w_down_e, l, tm=tm_x_wide, tm_e=te_x)
        if last:
            xs = list(moe_x)[-1]
        else:
            next(moe_x)
            o_c = _attention(pc["q"], [(pc["k"], pc["v"], n_ctx)], batch, n_ctx, tq=tm_c)
            f_c = _fourier(pc["ab"], cs_c, batch, n_ctx, tn=tm_c)
            c1 = _merge(cs, mod_c, n_ctx, pc, o_c, f_c, lw, tm=tm_c_wide)
            moe_c = _moe_sparse(c1, mod_c, batch * n_ctx, lw, w_gate_e, w_up_e, w_down_e, l, tm=tm_c_wide,
                                tm_e=te_c)
            next(moe_c)
            next(moe_x)
            next(moe_c)
            xs = next(moe_x)
            cs = next(moe_c)
    return xs.reshape(batch, seq, d)
```

```python
import functools

import numpy as np
import jax
import jax.numpy as jnp
from jax import lax
from jax.experimental import pallas as pl
from jax.experimental.pallas import tpu as pltpu
from jax.experimental.pallas import tpu_sc as plsc

N_HEADS = 8
QK_NOPE = 64
QK_ROPE = 32
V_DIM = 64
GRID_W = 64
ROPE_BASE = 10000.0
FOURIER_GROUPS = 4
TOP_K = 4
ROUTED_SCALE = 2.5
N_BRANCHES = 3
EPS = 1e-6

LANE = 128
HEAD_PAD = LANE
VMEM_LIMIT = 56 * 1024 * 1024
PACK_W = 256
SC_WINDOW = 128
ATTN_ROWS = 512
MERGE_ROWS = 512

F32 = jnp.float32
BF16 = jnp.bfloat16


def _rms(x, g):
    return x * lax.rsqrt(jnp.mean(x * x, axis=-1, keepdims=True) + EPS) * g


def _sigmoid(x):
    return 1.0 / (1.0 + jnp.exp(-x))


def _dot(a, b):
    return jnp.dot(a, b, preferred_element_type=F32)


def _dot_t(a, b_t):
    return lax.dot_general(a, b_t, (((1,), (1,)), ((), ())), preferred_element_type=F32)


def _skewed(stages, n_chunks, chunk):
    live = [None] * n_chunks
    for step in range(n_chunks + len(stages) - 1):
        for c in range(n_chunks):
            s = step - c
            if 0 <= s < len(stages):
                live[c] = stages[s](c * chunk) if s == 0 else stages[s](c * chunk, live[c])


def _after(value, dependency):
    return lax.optimization_barrier((value, dependency))[0]


def _resident(shape):
    nd = len(shape)
    return pl.BlockSpec(shape, lambda *_: (0,) * nd, pipeline_mode=pl.Buffered(1))


class _Layer:
    def __init__(self, stacked, index):
        self.stacked, self.index = stacked, index
        self.shape = stacked.shape[1:]

    def spec(self):
        index = (self.index,) + (0,) * len(self.shape)
        return pl.BlockSpec((None,) + self.shape, lambda *_: index, pipeline_mode=pl.Buffered(1))


class _Mod:
    def __init__(self, table, layer, row0, n):
        self.table, self.layer, self.row0, self.n = table, layer, row0, n

    def spec(self, seq_of_step):
        layer, row0, n = self.layer, self.row0, self.n
        return pl.BlockSpec((None, 1) + self.table.shape[2:],
                            lambda i: (layer, row0 + seq_of_step(i) % n, 0, 0))


def _params(n_grid):
    return pltpu.CompilerParams(dimension_semantics=("arbitrary",) * n_grid,
                                vmem_limit_bytes=VMEM_LIMIT)


def _ada_kernel(c_ref, w_ref, b_ref, o_ref):
    c = c_ref[...]
    a = (c * _sigmoid(c)).astype(BF16)
    o_ref[0] = _dot(a, w_ref[0].astype(BF16)) + b_ref[0]


def _ada(c_all, w_ada, b_ada):
    n_layers, d, n_out = w_ada.shape
    rows = c_all.shape[0]
    tn = _tile(n_out, 1536)
    return pl.pallas_call(
        _ada_kernel,
        grid=(n_layers, n_out // tn),
        in_specs=[
            pl.BlockSpec((rows, d), lambda l, j: (0, 0)),
            pl.BlockSpec((1, d, tn), lambda l, j: (l, 0, j)),
            pl.BlockSpec((1, 1, tn), lambda l, j: (l, 0, j)),
        ],
        out_specs=pl.BlockSpec((1, rows, tn), lambda l, j: (l, 0, j)),
        out_shape=jax.ShapeDtypeStruct((n_layers, rows, n_out), F32),
        compiler_params=_params(2),
        name="ada",
    )(c_all, w_ada, b_ada.reshape(n_layers, 1, n_out))


def _inproj_kernel(*refs, kv_only, kv_rank, q_rank, conv_w, four_w, d_model):
    if kv_only:
        (x_ref, mod_ref, gpre_ref, w1_ref, gkv_ref, wuk_ref, wuv_ref, vone_ref, cos_ref, sin_ref,
         k_ref, v_ref) = refs
    else:
        (x_ref, mod_ref, gpre_ref, w1_ref, gkv_ref, wuk_ref, wuv_ref, vone_ref, cos_ref, sin_ref,
         bg_ref, gq_ref, wuq_ref, dc_ref,
         k_ref, v_ref, q_ref, cb_ref, cc_ref, cu_ref, ab_ref, gate_ref) = refs

    x = x_ref[...]
    shift = mod_ref[0, 0:1, :]
    scale = mod_ref[0, 1:2, :]
    h = (_rms(x, gpre_ref[...]) * (1.0 + scale) + shift).astype(BF16)
    cos = cos_ref[...]
    sin = sin_ref[...]

    o_kpe = kv_rank
    o_rot = o_kpe + HEAD_PAD
    o_q = o_rot + HEAD_PAD
    p = _dot_t(h, w1_ref[0, 0:o_q, :])
    ckv = _rms(p[:, 0:kv_rank], gkv_ref[...]).astype(BF16)
    kpe = p[:, o_kpe:o_rot] * cos + p[:, o_rot:o_q] * sin
    k = _dot(ckv, wuk_ref[...]) + jnp.concatenate([kpe] * N_HEADS, axis=1)
    k_ref[...] = k.astype(k_ref.dtype)
    v_ref[...] = (_dot(ckv, wuv_ref[...]) + vone_ref[...]).astype(v_ref.dtype)
    if kv_only:
        return

    o_cb = o_q + q_rank
    cq = _rms(_dot_t(h, w1_ref[0, o_q:o_cb, :]), gq_ref[...]).astype(BF16)
    half = QK_ROPE // 2
    lane = lax.broadcasted_iota(jnp.int32, (1, HEAD_PAD), 1)
    first = (lane >= QK_NOPE) & (lane < QK_NOPE + half)
    cos_h = jnp.concatenate([cos] * N_HEADS, axis=1)
    sin_h = jnp.concatenate([jnp.where(first, -sin, sin)] * N_HEADS, axis=1)
    first_h = jnp.concatenate([first] * N_HEADS, axis=1)
    lin = _dot(cq, wuq_ref[...])
    width = lin.shape[1]
    partner = jnp.where(first_h, pltpu.roll(lin, width - half, 1), pltpu.roll(lin, half, 1))
    q_ref[...] = (lin * cos_h + partner * sin_h).astype(q_ref.dtype)

    o_cc = o_cb + conv_w
    o_cu = o_cc + conv_w
    o_four = o_cu + conv_w
    cb_ref[...] = _dot_t(h, w1_ref[0, o_cb:o_cc, :]).astype(cb_ref.dtype)
    cc_ref[...] = _dot_t(h, w1_ref[0, o_cc:o_cu, :]).astype(cc_ref.dtype)
    cu_ref[...] = _dot_t(h, w1_ref[0, o_cu:o_four, :]).astype(cu_ref.dtype)

    o_gate = o_four + four_w
    uf = _dot_t(h, w1_ref[0, o_four:o_gate, :]).astype(BF16)
    ab_ref[...] = _dot(uf, dc_ref[...]).astype(ab_ref.dtype)

    for j in range(N_BRANCHES):
        lo = o_gate + j * d_model
        z = _dot_t(h, w1_ref[0, lo:lo + d_model, :]) + bg_ref[:, j * d_model:(j + 1) * d_model]
        gate_ref[:, j * d_model:(j + 1) * d_model] = _sigmoid(z).astype(gate_ref.dtype)


def _inproj(xs, mod, seq_len, lw, tables, *, kv_only, tm):
    t, d = xs.shape
    cos_t, sin_t = tables
    table_tiles = cos_t.shape[0] // tm
    kv_rank = lw["g_kv"].shape[1]
    q_rank = lw["g_q"].shape[1]
    conv_w = lw["conv_w"].shape[1]
    four_w = lw["dc"].shape[0]
    n_k = N_HEADS * HEAD_PAD

    def row(i):
        return (i, 0)

    def tab_map(i):
        return (i % table_tiles, 0)

    w1, layer = lw["w1"], lw["layer"]
    w1_rows = kv_rank + 2 * HEAD_PAD if kv_only else w1.shape[1]
    in_specs = [
        pl.BlockSpec((tm, d), row),
        mod.spec(lambda i: i * tm // seq_len),
        lw["g_pre_mix"].spec(),
        pl.BlockSpec((1, w1_rows, d), lambda i: (layer, 0, 0), pipeline_mode=pl.Buffered(1)),
        lw["g_kv"].spec(),
        lw["wuk"].spec(),
        lw["wuv"].spec(),
        _resident(lw["v_one"].shape),
        pl.BlockSpec((tm, HEAD_PAD), tab_map),
        pl.BlockSpec((tm, HEAD_PAD), tab_map),
    ]
    args = [xs, mod.table, lw["g_pre_mix"].stacked, w1, lw["g_kv"].stacked, lw["wuk"].stacked, lw["wuv"].stacked,
            lw["v_one"], cos_t, sin_t]
    out_shape = [jax.ShapeDtypeStruct((t, n_k), BF16), jax.ShapeDtypeStruct((t, n_k), BF16)]
    out_specs = [pl.BlockSpec((tm, n_k), row), pl.BlockSpec((tm, n_k), row)]
    if not kv_only:
        in_specs += [
            lw["b_gate"].spec(),
            lw["g_q"].spec(),
            lw["wuq"].spec(),
            _resident(lw["dc"].shape),
        ]
        args += [lw["b_gate"].stacked, lw["g_q"].stacked, lw["wuq"].stacked, lw["dc"]]
        widths = [n_k, conv_w, conv_w, conv_w, 2 * four_w, N_BRANCHES * d]
        out_shape += [jax.ShapeDtypeStruct((t, w), BF16) for w in widths]
        out_specs += [pl.BlockSpec((tm, w), row) for w in widths]
    outs = pl.pallas_call(
        functools.partial(_inproj_kernel, kv_only=kv_only, kv_rank=kv_rank, q_rank=q_rank,
                          conv_w=conv_w, four_w=four_w, d_model=d),
        grid=(t // tm,),
        in_specs=in_specs,
        out_specs=out_specs,
        out_shape=out_shape,
        compiler_params=_params(1),
        name="inproj_kv" if kv_only else "inproj",
    )(*args)
    names = ["k", "v", "q", "cb", "cc", "cu", "ab", "gate"]
    return dict(zip(names, outs))


def _attn_kernel(*refs, n_seg):
    q_ref = refs[0]
    o_ref = refs[-1]
    chunk = min(q_ref.shape[0], ATTN_ROWS)
    for c in range(q_ref.shape[0] // chunk):
        rows = slice(c * chunk, (c + 1) * chunk)
        outs = []
        for hh in range(N_HEADS):
            head = slice(hh * HEAD_PAD, (hh + 1) * HEAD_PAD)
            qh = q_ref[rows, head]
            s = [_dot_t(qh, refs[1 + 2 * i][:, head]) for i in range(n_seg)]
            m = functools.reduce(jnp.maximum, [jnp.max(si, axis=-1, keepdims=True) for si in s])
            acc = functools.reduce(jnp.add, [
                _dot(jnp.exp2((s[i] - m).astype(BF16)), refs[2 + 2 * i][:, head]) for i in range(n_seg)])
            outs.append(acc[:, 0:V_DIM] / acc[:, V_DIM:V_DIM + 1])
        o_ref[rows, :] = jnp.concatenate(outs, axis=1).astype(o_ref.dtype)


def _attention(q, segs, batch, seq_q, *, tq):
    t = q.shape[0]
    qt = seq_q // tq
    n_k = N_HEADS * HEAD_PAD
    in_specs = [pl.BlockSpec((tq, n_k), lambda b, j: (b * qt + j, 0))]
    args = [q]
    for k, v, m in segs:
        in_specs.append(pl.BlockSpec((m, n_k), lambda b, j: (b, 0)))
        in_specs.append(pl.BlockSpec((m, n_k), lambda b, j: (b, 0)))
        args += [k, v]
    return pl.pallas_call(
        functools.partial(_attn_kernel, n_seg=len(segs)),
        grid=(batch, qt),
        in_specs=in_specs,
        out_specs=pl.BlockSpec((tq, N_HEADS * V_DIM), lambda b, j: (b * qt + j, 0)),
        out_shape=jax.ShapeDtypeStruct((t, N_HEADS * V_DIM), BF16),
        compiler_params=_params(2),
        name="attention",
    )(*args)


def _four_kernel(cs_ref, ab_ref, o_ref, *, n, fw):
    o = _dot(cs_ref[:, 0:n], ab_ref[:, 0:fw]) + _dot(cs_ref[:, n:2 * n], ab_ref[:, fw:2 * fw])
    o_ref[...] = o.astype(o_ref.dtype)


def _fourier(ab, cs, batch, seq_len, *, tn):
    t, fw2 = ab.shape
    fw = fw2 // 2
    nt = seq_len // tn
    cs_spec = (_resident(cs.shape) if nt == 1
               else pl.BlockSpec((tn, 2 * seq_len), lambda b, j: (j, 0)))
    return pl.pallas_call(
        functools.partial(_four_kernel, n=seq_len, fw=fw),
        grid=(batch, nt),
        in_specs=[
            cs_spec,
            pl.BlockSpec((seq_len, fw2), lambda b, j: (b, 0)),
        ],
        out_specs=pl.BlockSpec((tn, fw), lambda b, j: (b * nt + j, 0)),
        out_shape=jax.ShapeDtypeStruct((t, fw), BF16),
        compiler_params=_params(2),
        name="fourier",
    )(cs, ab)


def _merge_kernel(x_ref, mod_ref, o_ref, cb_ref, cc_ref, cu_ref, ccp_ref, cup_ref, ccn_ref,
                  cun_ref, f_ref, gate_ref, convw_ref, wmo_ref, wco_ref, wfo_ref, wout_ref,
                  gpost_ref, out_ref, pad_ref, *, seq_len, tm, d_model):
    i = pl.program_id(0)
    has_prev = ((i * tm) % seq_len != 0).astype(F32)
    has_next = (((i + 1) * tm) % seq_len != 0).astype(F32)
    pad_ref[0:8, :] = ccp_ref[...].astype(F32) * cup_ref[...].astype(F32) * has_prev
    pad_ref[8:8 + tm, :] = cc_ref[...].astype(F32) * cu_ref[...].astype(F32)
    pad_ref[8 + tm:16 + tm, :] = ccn_ref[...].astype(F32) * cun_ref[...].astype(F32) * has_next
    d = d_model
    g1 = mod_ref[0, 2:3, :]
    chunk = min(tm, MERGE_ROWS)

    def conv_stage(r0):
        before = pad_ref[7 + r0:7 + r0 + chunk, :]
        after = pad_ref[9 + r0:9 + r0 + chunk, :]
        if tm > seq_len:
            at = (lax.broadcasted_iota(jnp.int32, (chunk, 1), 0) + r0) % seq_len
            before = jnp.where(at == 0, 0.0, before)
            after = jnp.where(at == seq_len - 1, 0.0, after)
        conv = (before * convw_ref[0:1, :] + pad_ref[8 + r0:8 + r0 + chunk, :] * convw_ref[1:2, :]
                + after * convw_ref[2:3, :])
        return (cb_ref[r0:r0 + chunk, :].astype(F32) * conv).astype(BF16)

    def branch_stage(r0, conv_in):
        return (_dot(o_ref[r0:r0 + chunk, :], wmo_ref[...]), _dot(conv_in, wco_ref[...]),
                _dot(f_ref[r0:r0 + chunk, :], wfo_ref[...]))

    def gate_stage(r0, ys):
        rows = slice(r0, r0 + chunk)
        return (gate_ref[rows, 0:d].astype(F32) * ys[0] + gate_ref[rows, d:2 * d].astype(F32) * ys[1]
                + gate_ref[rows, 2 * d:3 * d].astype(F32) * ys[2]).astype(BF16)

    def out_stage(r0, merged):
        return _dot(merged, wout_ref[...])

    def tail_stage(r0, y):
        rows = slice(r0, r0 + chunk)
        out_ref[rows, :] = x_ref[rows, :] + g1 * _rms(y, gpost_ref[...])

    _skewed((conv_stage, branch_stage, gate_stage, out_stage, tail_stage), tm // chunk, chunk)


def _merge(xs, mod, seq_len, pr, o, four, lw, *, tm):
    t, d = xs.shape
    assert tm % seq_len == 0 or seq_len % tm == 0
    cw = lw["conv_w"].shape[1]
    fw = four.shape[1]
    hb = tm // 8
    last_hb = t // 8 - 1

    def row(i):
        return (i, 0)

    def prev_map(i):
        return (jnp.maximum(i * hb - 1, 0), 0)

    def next_map(i):
        return (jnp.minimum((i + 1) * hb, last_hb), 0)

    in_specs = [
        pl.BlockSpec((tm, d), row),
        mod.spec(lambda i: i * tm // seq_len),
        pl.BlockSpec((tm, o.shape[1]), row),
        pl.BlockSpec((tm, cw), row),
        pl.BlockSpec((tm, cw), row),
        pl.BlockSpec((tm, cw), row),
        pl.BlockSpec((8, cw), prev_map),
        pl.BlockSpec((8, cw), prev_map),
        pl.BlockSpec((8, cw), next_map),
        pl.BlockSpec((8, cw), next_map),
        pl.BlockSpec((tm, fw), row),
        pl.BlockSpec((tm, N_BRANCHES * d), row),
        lw["conv_w"].spec(),
        lw["w_mla_out"].spec(),
        lw["w_conv_out"].spec(),
        lw["w_four_out"].spec(),
        lw["w_out"].spec(),
        lw["g_post_mix"].spec(),
    ]
    return pl.pallas_call(
        functools.partial(_merge_kernel, seq_len=seq_len, tm=tm, d_model=d),
        grid=(t // tm,),
        in_specs=in_specs,
        out_specs=pl.BlockSpec((tm, d), row),
        out_shape=jax.ShapeDtypeStruct((t, d), F32),
        scratch_shapes=[pltpu.VMEM((tm + 16, cw), F32)],
        compiler_params=_params(1),
        name="merge",
    )(xs, mod.table, o, pr["cb"], pr["cc"], pr["cu"], pr["cc"], pr["cu"], pr["cc"], pr["cu"], four,
      pr["gate"], lw["conv_w"].stacked, lw["w_mla_out"].stacked, lw["w_conv_out"].stacked,
      lw["w_four_out"].stacked, lw["w_out"].stacked, lw["g_post_mix"].stacked)


def _pack_rows(v):
    bits = lax.bitcast_convert_type(v.astype(BF16).astype(F32), jnp.uint32)
    rows = []
    for j in range(v.shape[1] // (2 * PACK_W)):
        lo = bits[:, (2 * j) * PACK_W:(2 * j + 1) * PACK_W]
        hi = bits[:, (2 * j + 1) * PACK_W:(2 * j + 2) * PACK_W]
        rows.append(lax.bitcast_convert_type((hi & jnp.uint32(0xFFFF0000)) | (lo >> 16), jnp.int32))
    return rows


def _unpack_rows(rows):
    parts = []
    for r in rows:
        u = lax.bitcast_convert_type(r, jnp.uint32)
        parts.append(lax.bitcast_convert_type(u << 16, F32))
        parts.append(lax.bitcast_convert_type(u & jnp.uint32(0xFFFF0000), F32))
    return jnp.concatenate(parts, axis=1)


def _route_kernel(x_ref, mod_ref, gpre_ref, wrt_ref, brt_ref, triu_ref,
                  tp_ref, sel_ref, rank_ref, cnt_ref, wgt_ref, *, n_experts):
    shift = mod_ref[0, 3:4, :]
    scale = mod_ref[0, 4:5, :]
    t = _rms(x_ref[...], gpre_ref[...]) * (1.0 + scale) + shift
    t_hi = t.astype(BF16)
    t_lo = (t - t_hi.astype(F32)).astype(BF16)
    for j, r in enumerate(_pack_rows(t)):
        tp_ref[j] = r
    tm = t.shape[0]
    hh = _dot_t(wrt_ref[...], t_hi)
    logits = hh[0:n_experts] + hh[n_experts:2 * n_experts] + _dot_t(wrt_ref[0:n_experts, :], t_lo)
    scores = _sigmoid(logits)
    work = scores + brt_ref[...]
    row = lax.broadcasted_iota(jnp.int32, scores.shape, 0)
    wide = lax.broadcasted_iota(jnp.int32, (LANE, tm), 0)
    firsts, picked, hits = [], [], []
    for k in range(TOP_K):
        best = jnp.max(work, axis=0, keepdims=True)
        first = jnp.min(jnp.where(work == best, row, n_experts), axis=0, keepdims=True)
        hit = row == first
        firsts.append(first)
        picked.append(jnp.sum(jnp.where(hit, scores, 0.0), axis=0, keepdims=True))
        hits.append(wide == first + k * n_experts)
        work = jnp.where(hit, -jnp.inf, work)
    total = functools.reduce(jnp.add, picked)
    onehot = functools.reduce(jnp.add, [jnp.where(h, 1.0, 0.0) for h in hits])
    earlier = _dot(onehot.astype(BF16), triu_ref[...])
    col = jnp.broadcast_to(jnp.sum(onehot, axis=1, keepdims=True), (LANE, LANE))
    row_c = lax.broadcasted_iota(jnp.int32, (LANE, LANE), 0)
    before = jnp.zeros((LANE, LANE), F32)
    for s in range(1, TOP_K):
        before = before + jnp.where(row_c >= s * n_experts, pltpu.roll(col, s * n_experts, 0), 0.0)
    ahead = earlier + before[:, 0:1]
    row8 = lax.broadcasted_iota(jnp.int32, (8, tm), 0)
    sel = jnp.zeros((8, tm), jnp.int32)
    rank = jnp.zeros((8, tm), F32)
    wgt_t = jnp.zeros((LANE, tm), F32)
    for k in range(TOP_K):
        sel = jnp.where(row8 == k, firsts[k], sel)
        rank = jnp.where(row8 == k, jnp.sum(jnp.where(hits[k], ahead, 0.0), axis=0, keepdims=True), rank)
        wgt_t = jnp.where(wide == k, picked[k] / total * ROUTED_SCALE, wgt_t)
    sel_ref[0] = sel
    rank_ref[0] = rank.astype(jnp.int32)
    cnt_ref[0] = col.astype(jnp.int32)
    wgt_ref[...] = wgt_t.T


def _route(x1, mod, seq_len, lw, *, tm):
    t, d = x1.shape
    n_tiles = t // tm
    tiles_per_seq = max(seq_len // tm, 1)
    n_experts = lw["b_router_t"].shape[0]
    rows = d // (2 * PACK_W)
    triu = jnp.tri(tm, tm, -1, dtype=BF16).T
    tile3 = lambda i: (i, 0, 0)
    return pl.pallas_call(
        functools.partial(_route_kernel, n_experts=n_experts),
        grid=(n_tiles,),
        in_specs=[
            pl.BlockSpec((tm, d), lambda i: (i, 0)),
            mod.spec(lambda i: i // tiles_per_seq),
            lw["g_pre_ffn"].spec(),
            lw["w_router_t"].spec(),
            lw["b_router_t"].spec(),
            _resident((tm, tm)),
        ],
        out_specs=[
            pl.BlockSpec((rows, tm, PACK_W), lambda i: (0, i, 0)),
            pl.BlockSpec((1, 8, tm), tile3),
            pl.BlockSpec((1, 8, tm), tile3),
            pl.BlockSpec((1, LANE, LANE), tile3),
            pl.BlockSpec((tm, LANE), lambda i: (i, 0)),
        ],
        out_shape=[
            jax.ShapeDtypeStruct((rows, t, PACK_W), jnp.int32),
            jax.ShapeDtypeStruct((n_tiles, 8, tm), jnp.int32),
            jax.ShapeDtypeStruct((n_tiles, 8, tm), jnp.int32),
            jax.ShapeDtypeStruct((n_tiles, LANE, LANE), jnp.int32),
            jax.ShapeDtypeStruct((t, LANE), F32),
        ],
        compiler_params=_params(1),
        name="moe_route",
    )(x1, mod.table, lw["g_pre_ffn"].stacked, lw["w_router_t"].stacked, lw["b_router_t"].stacked, triu)


def _expert_kernel(te_ref, meta_ref, xs_ref, wg_ref, wu_ref, wd_ref, ys_ref, wg_sc, wu_sc, wd_sc):
    i = pl.program_id(0)
    live = i < meta_ref[0]
    new_expert = (i == 0) | (te_ref[i] != te_ref[jnp.maximum(i - 1, 0)])

    @pl.when(live & new_expert)
    def _():
        wg_sc[...] = wg_ref[0, 0].astype(BF16)
        wu_sc[...] = wu_ref[0, 0].astype(BF16)
        wd_sc[...] = wd_ref[0, 0].astype(BF16)

    @pl.when(live)
    def _():
        x = _unpack_rows([xs_ref[j] for j in range(xs_ref.shape[0])]).astype(BF16)
        gate = _dot(x, wg_sc[...])
        act = (gate * _sigmoid(gate) * _dot(x, wu_sc[...])).astype(BF16)
        for j, r in enumerate(_pack_rows(_dot(act, wd_sc[...]))):
            ys_ref[j] = r


def _experts(xs, tile_expert, meta, w_gate_e, w_up_e, w_down_e, layer, *, tm):
    rows, p, _ = xs.shape
    n_experts, d, f = w_gate_e.shape[1:]

    def slot(i, te, meta):
        return (0, jnp.minimum(i, meta[0] - 1), 0)

    grid_spec = pltpu.PrefetchScalarGridSpec(
        num_scalar_prefetch=2,
        grid=(p // tm,),
        in_specs=[
            pl.BlockSpec((rows, tm, PACK_W), slot),
            pl.BlockSpec((1, 1, d, f), lambda i, te, meta: (layer, te[i], 0, 0)),
            pl.BlockSpec((1, 1, d, f), lambda i, te, meta: (layer, te[i], 0, 0)),
            pl.BlockSpec((1, 1, f, d), lambda i, te, meta: (layer, te[i], 0, 0)),
        ],
        out_specs=pl.BlockSpec((rows, tm, PACK_W), slot),
        scratch_shapes=[pltpu.VMEM((d, f), BF16), pltpu.VMEM((d, f), BF16), pltpu.VMEM((f, d), BF16)],
    )
    return pl.pallas_call(
        _expert_kernel,
        grid_spec=grid_spec,
        out_shape=jax.ShapeDtypeStruct(xs.shape, jnp.int32),
        compiler_params=_params(1),
        name="moe_experts",
    )(tile_expert, meta, xs, w_gate_e, w_up_e, w_down_e)


def _combine_kernel(x_ref, mod_ref, gpre_ref, gpost_ref, wgs_ref, wus_ref, wds_ref, yk_ref, wgt_ref,
                    out_ref):
    shift = mod_ref[0, 3:4, :]
    scale = mod_ref[0, 4:5, :]
    x = x_ref[...]
    t_hi = (_rms(x, gpre_ref[...]) * (1.0 + scale) + shift).astype(BF16)
    gate = _dot(t_hi, wgs_ref[...])
    act = (gate * _sigmoid(gate) * _dot(t_hi, wus_ref[...])).astype(BF16)
    acc = _dot(act, wds_ref[...])
    for k in range(TOP_K):
        y = _unpack_rows([yk_ref[k, j] for j in range(yk_ref.shape[1])])
        acc = acc + wgt_ref[:, k:k + 1] * y
    g2 = mod_ref[0, 5:6, :]
    out_ref[...] = x + g2 * _rms(acc, gpost_ref[...])


def _combine(x1, mod, seq_len, lw, yk, wgt, *, tm):
    t, d = x1.shape
    tiles_per_seq = max(seq_len // tm, 1)
    rows = yk.shape[1]
    row = lambda i: (i, 0)
    return pl.pallas_call(
        _combine_kernel,
        grid=(t // tm,),
        in_specs=[
            pl.BlockSpec((tm, d), row),
            mod.spec(lambda i: i // tiles_per_seq),
            lw["g_pre_ffn"].spec(),
            lw["g_post_ffn"].spec(),
            lw["w_gate_s"].spec(),
            lw["w_up_s"].spec(),
            lw["w_down_s"].spec(),
            pl.BlockSpec((TOP_K, rows, tm, PACK_W), lambda i: (0, 0, i, 0)),
            pl.BlockSpec((tm, LANE), row),
        ],
        out_specs=pl.BlockSpec((tm, d), row),
        out_shape=jax.ShapeDtypeStruct((t, d), F32),
        compiler_params=_params(1),
        name="moe_combine",
    )(x1, mod.table, lw["g_pre_ffn"].stacked, lw["g_post_ffn"].stacked, lw["w_gate_s"].stacked, lw["w_up_s"].stacked,
      lw["w_down_s"].stacked, yk, wgt)


def _sc_mesh():
    return plsc.VectorSubcoreMesh(core_axis_name="core", subcore_axis_name="subcore")


def _sc_scatter_rows(src, idx, n_out):
    n_lists, n = idx.shape
    width = src.shape[1]

    @pl.kernel(out_type=jax.ShapeDtypeStruct((n_out, width), src.dtype), mesh=_sc_mesh(), scratch_types=[])
    def scatter(x_hbm, *refs):
        i_hbms, o_hbm = refs[:n_lists], refs[n_lists]

        def body(x_vmem, *i_vmems):
            for i_vmem in i_vmems:
                pltpu.sync_copy(x_vmem, o_hbm.at[i_vmem.at[0]])

        pltpu.emit_pipeline(
            body,
            grid=(n // SC_WINDOW,),
            in_specs=[pl.BlockSpec((SC_WINDOW, width), lambda i: (i, 0))]
            + [pl.BlockSpec((1, SC_WINDOW), lambda i, r=r: (r, i)) for r in range(n_lists)],
            out_specs=[],
            core_axis_name=("core", "subcore"),
            dimension_semantics=(pltpu.PARALLEL,),
        )(x_hbm, *i_hbms)

    return scatter(src, *([idx] * n_lists))


def _sc_gather_rows(table, idx):
    n = idx.shape[0]
    width = table.shape[1]

    @pl.kernel(out_type=jax.ShapeDtypeStruct((n, width), table.dtype), mesh=_sc_mesh(), scratch_types=[])
    def gather(x_hbm, i_hbm, o_hbm):
        def body(i_vmem, o_vmem):
            pltpu.sync_copy(x_hbm.at[i_vmem.at[0]], o_vmem)

        pltpu.emit_pipeline(
            body,
            grid=(n // SC_WINDOW,),
            in_specs=[pl.BlockSpec((1, SC_WINDOW), lambda i: (0, i))],
            out_specs=[pl.BlockSpec((SC_WINDOW, width), lambda i: (i, 0))],
            core_axis_name=("core", "subcore"),
            dimension_semantics=(pltpu.PARALLEL,),
        )(i_hbm, o_hbm)

    return gather(table, idx.reshape(1, n))


def _moe_sparse(x1, mod, seq_len, lw, w_gate_e, w_up_e, w_down_e, layer, *, tm, tm_e):
    t = x1.shape[0]
    n_experts = lw["b_router_t"].shape[0]
    assert n_experts * TOP_K == LANE
    n_tok_tiles = t // tm
    n_row_tiles = (t * TOP_K) // tm_e + n_experts
    p = n_row_tiles * tm_e
    experts = jnp.arange(n_experts, dtype=jnp.int32)

    tp, sel, rank, cnt, wgt = _route(x1, mod, seq_len, lw, tm=tm)
    rows = tp.shape[0]
    cnt = cnt[:, :, 0].reshape(n_tok_tiles, TOP_K, n_experts).sum(axis=1)
    padded = (cnt.sum(axis=0) + tm_e - 1) // tm_e * tm_e
    group_end = jnp.cumsum(padded)
    base = (group_end - padded)[None, :] + jnp.cumsum(cnt, axis=0) - cnt
    chosen = sel[:, :TOP_K, :, None] == experts
    pos = jnp.sum(jnp.where(chosen, base[:, None, None, :], 0), axis=-1) + rank[:, :TOP_K, :]
    pos = pos.transpose(1, 0, 2).reshape(TOP_K, t)

    n_used = group_end[-1] // tm_e
    tile_start = jnp.arange(n_row_tiles, dtype=jnp.int32) * tm_e
    tile_expert = jnp.sum(tile_start[:, None] >= group_end[None, :], axis=1).astype(jnp.int32)
    tile_expert = jnp.minimum(tile_expert, n_experts - 1)
    tile_expert = jnp.where(tile_start < group_end[-1], tile_expert, tile_expert[n_used - 1])
    meta = jnp.stack([n_used, n_used]).astype(jnp.int32)

    idx = pos[:, None, :] + (jnp.arange(rows, dtype=jnp.int32) * p)[None, :, None]
    xs = _sc_scatter_rows(tp.reshape(rows * t, PACK_W), idx.reshape(TOP_K, rows * t), rows * p)
    after = yield wgt
    if after is not None:
        xs = _after(xs, after)
    ys = _experts(xs.reshape(rows, p, PACK_W), tile_expert, meta, w_gate_e, w_up_e, w_down_e, layer, tm=tm_e)
    yk = _sc_gather_rows(ys.reshape(rows * p, PACK_W), idx.reshape(-1))
    yield ys
    yield _combine(x1, mod, seq_len, lw, yk.reshape(TOP_K, rows, t, PACK_W), wgt, tm=tm)


def _rope_tables(n):
    rows = n // GRID_W
    r, col = jnp.meshgrid(jnp.arange(rows), jnp.arange(GRID_W), indexing="ij")
    r = r.reshape(-1).astype(F32)
    col = col.reshape(-1).astype(F32)
    pairs = QK_ROPE // 4
    inv = ROPE_BASE ** (-jnp.arange(pairs, dtype=F32) / pairs)
    ang = jnp.concatenate([r[:, None] * inv, col[:, None] * inv], axis=-1)
    cos, sin = jnp.cos(ang), jnp.sin(ang)
    pad = HEAD_PAD - QK_NOPE - QK_ROPE
    cos_t = jnp.concatenate([jnp.ones((n, QK_NOPE), F32), cos, cos, jnp.zeros((n, pad), F32)], axis=1)
    sin_t = jnp.concatenate([jnp.zeros((n, QK_NOPE), F32), sin, sin, jnp.zeros((n, pad), F32)], axis=1)
    return cos_t, sin_t


def _identity_tables(n):
    pad = HEAD_PAD - QK_NOPE - QK_ROPE
    cos_t = jnp.concatenate([jnp.ones((n, QK_NOPE + QK_ROPE), F32), jnp.zeros((n, pad), F32)], axis=1)
    return cos_t, jnp.zeros((n, HEAD_PAD), F32)


def _position_dft(n):
    nb = 64 if n % 64 == 0 else 1
    na = n // nb
    m = jnp.arange(n, dtype=jnp.int32)[None, :]
    ang_a = ((jnp.arange(na, dtype=jnp.int32)[:, None] * m) % na).astype(F32) * (2.0 * np.pi / na)
    ang_b = ((jnp.arange(nb, dtype=jnp.int32)[:, None] * m) % n).astype(F32) * (2.0 * np.pi / n)
    ca, sa = jnp.cos(ang_a)[:, None, :], jnp.sin(ang_a)[:, None, :]
    cb, sb = jnp.cos(ang_b)[None, :, :], jnp.sin(ang_b)[None, :, :]
    norm = 1.0 / np.sqrt(n)
    cos = ((ca * cb - sa * sb) * norm).reshape(n, n)
    sin = ((sa * cb + ca * sb) * (-norm)).reshape(n, n)
    return jnp.concatenate([cos, sin], axis=1).astype(BF16)


def _channel_dft(width):
    gc = width // FOURIER_GROUPS
    idx = (jnp.arange(gc, dtype=jnp.int32)[:, None] * jnp.arange(gc, dtype=jnp.int32)[None, :]) % gc
    ang = idx.astype(F32) * (2.0 * np.pi / gc)
    eye = jnp.eye(FOURIER_GROUPS, dtype=F32)
    norm = 1.0 / np.sqrt(gc)
    return jnp.concatenate([jnp.kron(eye, jnp.cos(ang) * norm), jnp.kron(eye, jnp.sin(ang) * norm)],
                           axis=1).astype(BF16)


def _w1_kernel(w_ref, o_ref, *, kv_rank):
    cols = w_ref.shape[2]
    kv_end = kv_rank + QK_ROPE
    half = QK_ROPE // 2
    tail = HEAD_PAD - QK_NOPE - QK_ROPE
    o_kpe = kv_rank
    o_rot = o_kpe + HEAD_PAD
    o_rest = o_rot + HEAD_PAD
    dt = o_ref.dtype
    o_ref[0, 0:kv_rank, :] = w_ref[0, 0:kv_rank, :].astype(dt)
    for base in (o_kpe, o_rot):
        o_ref[0, base:base + QK_NOPE, :] = jnp.zeros((QK_NOPE, cols), dt)
        o_ref[0, base + QK_NOPE + QK_ROPE:base + HEAD_PAD, :] = jnp.zeros((tail, cols), dt)
    o_ref[0, o_kpe + QK_NOPE:o_kpe + QK_NOPE + QK_ROPE, :] = w_ref[0, kv_rank:kv_end, :].astype(dt)
    o_ref[0, o_rot + QK_NOPE:o_rot + QK_NOPE + half, :] = (-w_ref[0, kv_rank + half:kv_end, :]).astype(dt)
    o_ref[0, o_rot + QK_NOPE + half:o_rot + QK_NOPE + QK_ROPE, :] = w_ref[0, kv_rank:kv_rank + half, :].astype(dt)
    o_ref[0, o_rest:, :] = w_ref[0, kv_end:, :].astype(dt)


def _prep_w1(w_in_t, kv_rank):
    n_layers, width, d = w_in_t.shape
    out_w = width - QK_ROPE + 2 * HEAD_PAD
    tc = _tile(d, 256)
    return pl.pallas_call(
        functools.partial(_w1_kernel, kv_rank=kv_rank),
        grid=(n_layers, d // tc),
        in_specs=[pl.BlockSpec((1, width, tc), lambda l, i: (l, 0, i))],
        out_specs=pl.BlockSpec((1, out_w, tc), lambda l, i: (l, 0, i)),
        out_shape=jax.ShapeDtypeStruct((n_layers, out_w, d), BF16),
        compiler_params=_params(2),
        name="prep_w1",
    )(w_in_t)


def _stacked_weights(g_pre_mix, g_post_mix, g_pre_ffn, g_post_ffn, b_gate, g_q, w_uq, g_kv, w_ukv,
                     w_mla_out, conv_w, w_conv_out, w_four_out, w_out, w_router, b_router, w_gate_s,
                     w_up_s, w_down_s):
    n_layers, kv_rank = g_kv.shape
    q_rank = g_q.shape[1]
    qk_dim = QK_NOPE + QK_ROPE
    uq = w_uq.reshape(n_layers, q_rank, N_HEADS, qk_dim) * (qk_dim ** -0.5 * np.log2(np.e))
    zq = jnp.zeros((n_layers, q_rank, N_HEADS, HEAD_PAD - qk_dim), F32)
    wuq = jnp.concatenate([uq, zq], axis=-1).reshape(n_layers, q_rank, N_HEADS * HEAD_PAD).astype(BF16)
    ukv = w_ukv.reshape(n_layers, kv_rank, N_HEADS, QK_NOPE + V_DIM)
    zk = jnp.zeros((n_layers, kv_rank, N_HEADS, HEAD_PAD - QK_NOPE), F32)
    zv = jnp.zeros((n_layers, kv_rank, N_HEADS, HEAD_PAD - V_DIM), F32)
    wuk = jnp.concatenate([ukv[..., :QK_NOPE], zk], axis=-1).reshape(n_layers, kv_rank, -1).astype(BF16)
    wuv = jnp.concatenate([ukv[..., QK_NOPE:], zv], axis=-1).reshape(n_layers, kv_rank, -1).astype(BF16)
    wr_hi = w_router.astype(BF16)
    wr_lo = (w_router - wr_hi.astype(F32)).astype(BF16)
    wr_t = jnp.concatenate([jnp.swapaxes(wr_hi, 1, 2), jnp.swapaxes(wr_lo, 1, 2)], axis=1)
    return {
        "g_pre_mix": g_pre_mix[:, None], "g_post_mix": g_post_mix[:, None],
        "g_pre_ffn": g_pre_ffn[:, None], "g_post_ffn": g_post_ffn[:, None],
        "b_gate": b_gate[:, None], "g_q": g_q[:, None], "g_kv": g_kv[:, None],
        "wuq": wuq, "wuk": wuk, "wuv": wuv, "conv_w": conv_w,
        "w_mla_out": w_mla_out.astype(BF16), "w_conv_out": w_conv_out.astype(BF16),
        "w_four_out": w_four_out.astype(BF16), "w_out": w_out.astype(BF16),
        "w_router_t": wr_t, "b_router_t": b_router[:, :, None],
        "w_gate_s": w_gate_s.astype(BF16), "w_up_s": w_up_s.astype(BF16), "w_down_s": w_down_s.astype(BF16),
    }


def _tile(n, pref):
    return pref if n % pref == 0 else n


def kernel(x, c, ctx, c_ctx, w_ada, b_ada, g_pre_mix, g_post_mix, g_pre_ffn, g_post_ffn, w_in, b_gate,
           g_q, w_uq, g_kv, w_ukv, w_mla_out, conv_w, w_conv_out, w_four_out, w_out, w_router, b_router,
           w_gate_e, w_up_e, w_down_e, w_gate_s, w_up_s, w_down_s):
    batch, seq, d = x.shape
    n_ctx = ctx.shape[1]
    n_layers = w_in.shape[0]
    xs = x.reshape(batch * seq, d)
    cs = ctx.reshape(batch * n_ctx, d)

    mod_rows = -(-(batch + 1) // 16) * 16
    c_all = jnp.concatenate([c, c_ctx[None], jnp.zeros((mod_rows - batch - 1, d), F32)], axis=0)
    ada = _ada(c_all, w_ada, b_ada).reshape(n_layers, mod_rows, 6, d)

    tm_x = _tile(seq, 512)
    tm_x_wide = _tile(seq, 2 * MERGE_ROWS)
    tq_x = _tile(seq, 2 * ATTN_ROWS)
    tm_c = _tile(n_ctx, 256)
    tm_c_mid = _tile(batch * n_ctx, 512)
    tm_c_wide = _tile(batch * n_ctx, 1024)
    te_x, te_c = 1024, 512

    tab_x = _rope_tables(seq)
    tab_c = _identity_tables(tm_c_mid)
    cs_x = _position_dft(seq)
    cs_c = _position_dft(n_ctx)
    dc = _channel_dft(w_four_out.shape[1])
    w1 = _prep_w1(jnp.swapaxes(w_in, 1, 2), g_kv.shape[1])
    stacked = _stacked_weights(g_pre_mix, g_post_mix, g_pre_ffn, g_post_ffn, b_gate, g_q, w_uq, g_kv, w_ukv,
                               w_mla_out, conv_w, w_conv_out, w_four_out, w_out, w_router, b_router,
                               w_gate_s, w_up_s, w_down_s)
    v_one = jnp.tile((jnp.arange(HEAD_PAD) == V_DIM).astype(F32), N_HEADS)[None]

    for l in range(n_layers):
        last = l == n_layers - 1
        lw = {name: _Layer(arr, l) for name, arr in stacked.items()}
        lw.update(w1=w1, layer=l, dc=dc, v_one=v_one)
        mod_x = _Mod(ada, l, 0, batch)
        mod_c = _Mod(ada, l, batch, 1)

        pc = _inproj(cs, mod_c, n_ctx, lw, tab_c, kv_only=last, tm=tm_c_mid)
        px = _inproj(xs, mod_x, seq, lw, tab_x, kv_only=False, tm=tm_x)
        o_x = _attention(px["q"], [(pc["k"], pc["v"], n_ctx), (px["k"], px["v"], seq)], batch, seq, tq=tq_x)
        f_x = _fourier(px["ab"], cs_x, batch, seq, tn=seq)
        x1 = _merge(xs, mod_x, seq, px, o_x, f_x, lw, tm=tm_x_wide)
        moe_x = _moe_sparse(x1, mod_x, seq, lw, w_gate_e, w_up_e, w_down_e, l, tm=tm_x_wide, tm_e=te_x)
        if last:
            xs = list(moe_x)[-1]
        else:
            routed_x = next(moe_x)
            q_c = _after(pc["q"], routed_x)
            o_c = _attention(q_c, [(pc["k"], pc["v"], n_ctx)], batch, n_ctx, tq=tm_c)
            f_c = _fourier(pc["ab"], cs_c, batch, n_ctx, tn=tm_c)
            c1 = _merge(cs, mod_c, n_ctx, pc, o_c, f_c, lw, tm=tm_c_wide)
            moe_c = _moe_sparse(c1, mod_c, batch * n_ctx, lw, w_gate_e, w_up_e, w_down_e, l, tm=tm_c_wide,
                                tm_e=te_c)
            next(moe_c)
            experts_x = moe_x.send(None)
            moe_c.send(experts_x)
            xs = next(moe_x)
            cs = next(moe_c)
    return xs.reshape(batch, seq, d)
```

```python
import functools

import numpy as np
import jax
import jax.numpy as jnp
from jax import lax
from jax.experimental import pallas as pl
from jax.experimental.pallas import tpu as pltpu
from jax.experimental.pallas import tpu_sc as plsc

N_HEADS = 8
QK_NOPE = 64
QK_ROPE = 32
V_DIM = 64
GRID_W = 64
ROPE_BASE = 10000.0
FOURIER_GROUPS = 4
TOP_K = 4
ROUTED_SCALE = 2.5
N_BRANCHES = 3
EPS = 1e-6

LANE = 128
SUBLANE = 8
HEAD_PAD = LANE
VMEM_LIMIT = 56 * 1024 * 1024
PACK_W = 256
SC_WINDOW = 128
ATTN_ROWS = 512
MERGE_ROWS = 512

F32 = jnp.float32
BF16 = jnp.bfloat16


def _rms(x, g):
    return x * lax.rsqrt(jnp.mean(x * x, axis=-1, keepdims=True) + EPS) * g


def _sigmoid(x):
    return 1.0 / (1.0 + jnp.exp(-x))


def _dot(a, b):
    return jnp.dot(a, b, preferred_element_type=F32)


def _dot_t(a, b_t):
    return lax.dot_general(a, b_t, (((1,), (1,)), ((), ())), preferred_element_type=F32)


def _skewed(stages, n_chunks, chunk):
    live = [None] * n_chunks
    for step in range(n_chunks + len(stages) - 1):
        for c in range(n_chunks):
            s = step - c
            if 0 <= s < len(stages):
                live[c] = stages[s](c * chunk) if s == 0 else stages[s](c * chunk, live[c])


def _after(value, dependency):
    return lax.optimization_barrier((value, dependency))[0]


def _resident(shape):
    nd = len(shape)
    return pl.BlockSpec(shape, lambda *_: (0,) * nd, pipeline_mode=pl.Buffered(1))


class _Layer:
    def __init__(self, stacked, index):
        self.stacked, self.index = stacked, index
        self.shape = stacked.shape[1:]

    def spec(self):
        index = (self.index,) + (0,) * len(self.shape)
        return pl.BlockSpec((None,) + self.shape, lambda *_: index, pipeline_mode=pl.Buffered(1))


class _Mod:
    def __init__(self, table, layer, row0, n):
        self.table, self.layer, self.row0, self.n = table, layer, row0, n

    def spec(self, seq_of_step):
        layer, row0, n = self.layer, self.row0, self.n
        return pl.BlockSpec((None, 1) + self.table.shape[2:],
                            lambda i: (layer, row0 + seq_of_step(i) % n, 0, 0))


def _params(n_grid):
    return pltpu.CompilerParams(dimension_semantics=("arbitrary",) * n_grid,
                                vmem_limit_bytes=VMEM_LIMIT)


def _ada_kernel(c_ref, w_ref, b_ref, o_ref):
    c = c_ref[...]
    a = (c * _sigmoid(c)).astype(BF16)
    o_ref[0] = _dot(a, w_ref[0].astype(BF16)) + b_ref[0]


def _ada(c_all, w_ada, b_ada):
    n_layers, d, n_out = w_ada.shape
    rows = c_all.shape[0]
    tn = _tile(n_out, 1536)
    return pl.pallas_call(
        _ada_kernel,
        grid=(n_layers, n_out // tn),
        in_specs=[
            pl.BlockSpec((rows, d), lambda l, j: (0, 0)),
            pl.BlockSpec((1, d, tn), lambda l, j: (l, 0, j)),
            pl.BlockSpec((1, 1, tn), lambda l, j: (l, 0, j)),
        ],
        out_specs=pl.BlockSpec((1, rows, tn), lambda l, j: (l, 0, j)),
        out_shape=jax.ShapeDtypeStruct((n_layers, rows, n_out), F32),
        compiler_params=_params(2),
        name="ada",
    )(c_all, w_ada, b_ada.reshape(n_layers, 1, n_out))


def _inproj_kernel(*refs, kv_only, kv_rank, q_rank, conv_w, four_w, d_model):
    if kv_only:
        (x_ref, mod_ref, gpre_ref, w1_ref, gkv_ref, wuk_ref, wuv_ref, vone_ref, cos_ref, sin_ref,
         k_ref, v_ref) = refs
    else:
        (x_ref, mod_ref, gpre_ref, w1_ref, gkv_ref, wuk_ref, wuv_ref, vone_ref, cos_ref, sin_ref,
         bg_ref, gq_ref, wuq_ref, dc_ref,
         k_ref, v_ref, q_ref, cb_ref, cc_ref, cu_ref, ab_ref, gate_ref) = refs

    x = x_ref[...]
    shift = mod_ref[0, 0:1, :]
    scale = mod_ref[0, 1:2, :]
    h = (_rms(x, gpre_ref[...]) * (1.0 + scale) + shift).astype(BF16)
    cos = cos_ref[...]
    sin = sin_ref[...]

    o_kpe = kv_rank
    o_rot = o_kpe + HEAD_PAD
    o_q = o_rot + HEAD_PAD
    p = _dot_t(h, w1_ref[0, 0:o_q, :])
    ckv = _rms(p[:, 0:kv_rank], gkv_ref[...]).astype(BF16)
    kpe = p[:, o_kpe:o_rot] * cos + p[:, o_rot:o_q] * sin
    k = _dot(ckv, wuk_ref[...]) + jnp.concatenate([kpe] * N_HEADS, axis=1)
    k_ref[...] = k.astype(k_ref.dtype)
    v_ref[...] = (_dot(ckv, wuv_ref[...]) + vone_ref[...]).astype(v_ref.dtype)
    if kv_only:
        return

    o_cb = o_q + q_rank
    cq = _rms(_dot_t(h, w1_ref[0, o_q:o_cb, :]), gq_ref[...]).astype(BF16)
    half = QK_ROPE // 2
    lane = lax.broadcasted_iota(jnp.int32, (1, HEAD_PAD), 1)
    first = (lane >= QK_NOPE) & (lane < QK_NOPE + half)
    cos_h = jnp.concatenate([cos] * N_HEADS, axis=1)
    sin_h = jnp.concatenate([jnp.where(first, -sin, sin)] * N_HEADS, axis=1)
    first_h = jnp.concatenate([first] * N_HEADS, axis=1)
    lin = _dot(cq, wuq_ref[...])
    width = lin.shape[1]
    partner = jnp.where(first_h, pltpu.roll(lin, width - half, 1), pltpu.roll(lin, half, 1))
    q_ref[...] = (lin * cos_h + partner * sin_h).astype(q_ref.dtype)

    o_cc = o_cb + conv_w
    o_cu = o_cc + conv_w
    o_four = o_cu + conv_w
    cb_ref[...] = _dot_t(h, w1_ref[0, o_cb:o_cc, :]).astype(cb_ref.dtype)
    cc_ref[...] = _dot_t(h, w1_ref[0, o_cc:o_cu, :]).astype(cc_ref.dtype)
    cu_ref[...] = _dot_t(h, w1_ref[0, o_cu:o_four, :]).astype(cu_ref.dtype)

    o_gate = o_four + four_w
    uf = _dot_t(h, w1_ref[0, o_four:o_gate, :]).astype(BF16)
    ab_ref[...] = _dot(uf, dc_ref[...]).astype(ab_ref.dtype)

    for j in range(N_BRANCHES):
        lo = o_gate + j * d_model
        z = _dot_t(h, w1_ref[0, lo:lo + d_model, :]) + bg_ref[:, j * d_model:(j + 1) * d_model]
        gate_ref[:, j * d_model:(j + 1) * d_model] = _sigmoid(z).astype(gate_ref.dtype)


def _inproj(xs, mod, seq_len, lw, tables, *, kv_only, tm):
    t, d = xs.shape
    cos_t, sin_t = tables
    table_tiles = cos_t.shape[0] // tm
    kv_rank = lw["g_kv"].shape[1]
    q_rank = lw["g_q"].shape[1]
    conv_w = lw["conv_w"].shape[1]
    four_w = lw["dc"].shape[0]
    n_k = N_HEADS * HEAD_PAD

    def row(i):
        return (i, 0)

    def tab_map(i):
        return (i % table_tiles, 0)

    w1, layer = lw["w1"], lw["layer"]
    w1_rows = kv_rank + 2 * HEAD_PAD if kv_only else w1.shape[1]
    in_specs = [
        pl.BlockSpec((tm, d), row),
        mod.spec(lambda i: i * tm // seq_len),
        lw["g_pre_mix"].spec(),
        pl.BlockSpec((1, w1_rows, d), lambda i: (layer, 0, 0), pipeline_mode=pl.Buffered(1)),
        lw["g_kv"].spec(),
        lw["wuk"].spec(),
        lw["wuv"].spec(),
        _resident(lw["v_one"].shape),
        pl.BlockSpec((tm, HEAD_PAD), tab_map),
        pl.BlockSpec((tm, HEAD_PAD), tab_map),
    ]
    args = [xs, mod.table, lw["g_pre_mix"].stacked, w1, lw["g_kv"].stacked, lw["wuk"].stacked, lw["wuv"].stacked,
            lw["v_one"], cos_t, sin_t]
    out_shape = [jax.ShapeDtypeStruct((t, n_k), BF16), jax.ShapeDtypeStruct((t, n_k), BF16)]
    out_specs = [pl.BlockSpec((tm, n_k), row), pl.BlockSpec((tm, n_k), row)]
    if not kv_only:
        in_specs += [
            lw["b_gate"].spec(),
            lw["g_q"].spec(),
            lw["wuq"].spec(),
            _resident(lw["dc"].shape),
        ]
        args += [lw["b_gate"].stacked, lw["g_q"].stacked, lw["wuq"].stacked, lw["dc"]]
        widths = [n_k, conv_w, conv_w, conv_w, 2 * four_w, N_BRANCHES * d]
        out_shape += [jax.ShapeDtypeStruct((t, w), BF16) for w in widths]
        out_specs += [pl.BlockSpec((tm, w), row) for w in widths]
    outs = pl.pallas_call(
        functools.partial(_inproj_kernel, kv_only=kv_only, kv_rank=kv_rank, q_rank=q_rank,
                          conv_w=conv_w, four_w=four_w, d_model=d),
        grid=(t // tm,),
        in_specs=in_specs,
        out_specs=out_specs,
        out_shape=out_shape,
        compiler_params=_params(1),
        name="inproj_kv" if kv_only else "inproj",
    )(*args)
    names = ["k", "v", "q", "cb", "cc", "cu", "ab", "gate"]
    return dict(zip(names, outs))


def _attn_kernel(*refs, n_seg):
    q_ref = refs[0]
    o_ref = refs[-1]
    chunk = min(q_ref.shape[0], ATTN_ROWS)
    for c in range(q_ref.shape[0] // chunk):
        rows = slice(c * chunk, (c + 1) * chunk)
        outs = []
        for hh in range(N_HEADS):
            head = slice(hh * HEAD_PAD, (hh + 1) * HEAD_PAD)
            qh = q_ref[rows, head]
            s = [_dot_t(qh, refs[1 + 2 * i][:, head]) for i in range(n_seg)]
            m = functools.reduce(jnp.maximum, [jnp.max(si, axis=-1, keepdims=True) for si in s])
            acc = functools.reduce(jnp.add, [
                _dot(jnp.exp2((s[i] - m).astype(BF16)), refs[2 + 2 * i][:, head]) for i in range(n_seg)])
            outs.append(acc[:, 0:V_DIM] / acc[:, V_DIM:V_DIM + 1])
        o_ref[rows, :] = jnp.concatenate(outs, axis=1).astype(o_ref.dtype)


def _attention(q, segs, batch, seq_q, *, tq):
    t = q.shape[0]
    qt = seq_q // tq
    n_k = N_HEADS * HEAD_PAD
    in_specs = [pl.BlockSpec((tq, n_k), lambda b, j: (b * qt + j, 0))]
    args = [q]
    for k, v, m in segs:
        in_specs.append(pl.BlockSpec((m, n_k), lambda b, j: (b, 0)))
        in_specs.append(pl.BlockSpec((m, n_k), lambda b, j: (b, 0)))
        args += [k, v]
    return pl.pallas_call(
        functools.partial(_attn_kernel, n_seg=len(segs)),
        grid=(batch, qt),
        in_specs=in_specs,
        out_specs=pl.BlockSpec((tq, N_HEADS * V_DIM), lambda b, j: (b * qt + j, 0)),
        out_shape=jax.ShapeDtypeStruct((t, N_HEADS * V_DIM), BF16),
        compiler_params=_params(2),
        name="attention",
    )(*args)


def _four_kernel(cs_ref, ab_ref, o_ref, *, n, fw):
    o = _dot(cs_ref[:, 0:n], ab_ref[:, 0:fw]) + _dot(cs_ref[:, n:2 * n], ab_ref[:, fw:2 * fw])
    o_ref[...] = o.astype(o_ref.dtype)


def _fourier(ab, cs, batch, seq_len, *, tn):
    t, fw2 = ab.shape
    fw = fw2 // 2
    nt = seq_len // tn
    cs_spec = (_resident(cs.shape) if nt == 1
               else pl.BlockSpec((tn, 2 * seq_len), lambda b, j: (j, 0)))
    return pl.pallas_call(
        functools.partial(_four_kernel, n=seq_len, fw=fw),
        grid=(batch, nt),
        in_specs=[
            cs_spec,
            pl.BlockSpec((seq_len, fw2), lambda b, j: (b, 0)),
        ],
        out_specs=pl.BlockSpec((tn, fw), lambda b, j: (b * nt + j, 0)),
        out_shape=jax.ShapeDtypeStruct((t, fw), BF16),
        compiler_params=_params(2),
        name="fourier",
    )(cs, ab)


def _merge_kernel(x_ref, mod_ref, o_ref, cb_ref, cc_ref, cu_ref, ccp_ref, cup_ref, ccn_ref,
                  cun_ref, f_ref, gate_ref, convw_ref, wmo_ref, wco_ref, wfo_ref, wout_ref,
                  gpost_ref, out_ref, pad_ref, *, seq_len, tm, d_model):
    i = pl.program_id(0)
    has_prev = ((i * tm) % seq_len != 0).astype(F32)
    has_next = (((i + 1) * tm) % seq_len != 0).astype(F32)
    lo = SUBLANE
    pad_ref[0:lo, :] = ccp_ref[...].astype(F32) * cup_ref[...].astype(F32) * has_prev
    pad_ref[lo:lo + tm, :] = cc_ref[...].astype(F32) * cu_ref[...].astype(F32)
    pad_ref[lo + tm:2 * lo + tm, :] = ccn_ref[...].astype(F32) * cun_ref[...].astype(F32) * has_next
    d = d_model
    g1 = mod_ref[0, 2:3, :]
    chunk = min(tm, MERGE_ROWS)

    def conv_stage(r0):
        before = pad_ref[lo - 1 + r0:lo - 1 + r0 + chunk, :]
        after = pad_ref[lo + 1 + r0:lo + 1 + r0 + chunk, :]
        if tm > seq_len:
            at = (lax.broadcasted_iota(jnp.int32, (chunk, 1), 0) + r0) % seq_len
            before = jnp.where(at == 0, 0.0, before)
            after = jnp.where(at == seq_len - 1, 0.0, after)
        conv = (before * convw_ref[0:1, :] + pad_ref[lo + r0:lo + r0 + chunk, :] * convw_ref[1:2, :]
                + after * convw_ref[2:3, :])
        return (cb_ref[r0:r0 + chunk, :].astype(F32) * conv).astype(BF16)

    def branch_stage(r0, conv_in):
        return (_dot(o_ref[r0:r0 + chunk, :], wmo_ref[...]), _dot(conv_in, wco_ref[...]),
                _dot(f_ref[r0:r0 + chunk, :], wfo_ref[...]))

    def gate_stage(r0, ys):
        rows = slice(r0, r0 + chunk)
        return (gate_ref[rows, 0:d].astype(F32) * ys[0] + gate_ref[rows, d:2 * d].astype(F32) * ys[1]
                + gate_ref[rows, 2 * d:3 * d].astype(F32) * ys[2]).astype(BF16)

    def out_stage(r0, merged):
        return _dot(merged, wout_ref[...])

    def tail_stage(r0, y):
        rows = slice(r0, r0 + chunk)
        out_ref[rows, :] = x_ref[rows, :] + g1 * _rms(y, gpost_ref[...])

    _skewed((conv_stage, branch_stage, gate_stage, out_stage, tail_stage), tm // chunk, chunk)


def _merge(xs, mod, seq_len, pr, o, four, lw, *, tm):
    t, d = xs.shape
    assert tm % seq_len == 0 or seq_len % tm == 0
    cw = lw["conv_w"].shape[1]
    fw = four.shape[1]
    hb = tm // SUBLANE
    last_hb = t // SUBLANE - 1

    def row(i):
        return (i, 0)

    def prev_map(i):
        return (jnp.maximum(i * hb - 1, 0), 0)

    def next_map(i):
        return (jnp.minimum((i + 1) * hb, last_hb), 0)

    in_specs = [
        pl.BlockSpec((tm, d), row),
        mod.spec(lambda i: i * tm // seq_len),
        pl.BlockSpec((tm, o.shape[1]), row),
        pl.BlockSpec((tm, cw), row),
        pl.BlockSpec((tm, cw), row),
        pl.BlockSpec((tm, cw), row),
        pl.BlockSpec((SUBLANE, cw), prev_map),
        pl.BlockSpec((SUBLANE, cw), prev_map),
        pl.BlockSpec((SUBLANE, cw), next_map),
        pl.BlockSpec((SUBLANE, cw), next_map),
        pl.BlockSpec((tm, fw), row),
        pl.BlockSpec((tm, N_BRANCHES * d), row),
        lw["conv_w"].spec(),
        lw["w_mla_out"].spec(),
        lw["w_conv_out"].spec(),
        lw["w_four_out"].spec(),
        lw["w_out"].spec(),
        lw["g_post_mix"].spec(),
    ]
    return pl.pallas_call(
        functools.partial(_merge_kernel, seq_len=seq_len, tm=tm, d_model=d),
        grid=(t // tm,),
        in_specs=in_specs,
        out_specs=pl.BlockSpec((tm, d), row),
        out_shape=jax.ShapeDtypeStruct((t, d), F32),
        scratch_shapes=[pltpu.VMEM((tm + 2 * SUBLANE, cw), F32)],
        compiler_params=_params(1),
        name="merge",
    )(xs, mod.table, o, pr["cb"], pr["cc"], pr["cu"], pr["cc"], pr["cu"], pr["cc"], pr["cu"], four,
      pr["gate"], lw["conv_w"].stacked, lw["w_mla_out"].stacked, lw["w_conv_out"].stacked,
      lw["w_four_out"].stacked, lw["w_out"].stacked, lw["g_post_mix"].stacked)


def _pack_rows(v):
    bits = lax.bitcast_convert_type(v.astype(BF16).astype(F32), jnp.uint32)
    rows = []
    for j in range(v.shape[1] // (2 * PACK_W)):
        lo = bits[:, (2 * j) * PACK_W:(2 * j + 1) * PACK_W]
        hi = bits[:, (2 * j + 1) * PACK_W:(2 * j + 2) * PACK_W]
        rows.append(lax.bitcast_convert_type((hi & jnp.uint32(0xFFFF0000)) | (lo >> 16), jnp.int32))
    return rows


def _unpack_rows(rows):
    parts = []
    for r in rows:
        u = lax.bitcast_convert_type(r, jnp.uint32)
        parts.append(lax.bitcast_convert_type(u << 16, F32))
        parts.append(lax.bitcast_convert_type(u & jnp.uint32(0xFFFF0000), F32))
    return jnp.concatenate(parts, axis=1)


def _route_kernel(x_ref, mod_ref, gpre_ref, wrt_ref, brt_ref, triu_ref,
                  tp_ref, sel_ref, rank_ref, cnt_ref, wgt_ref, *, n_experts):
    shift = mod_ref[0, 3:4, :]
    scale = mod_ref[0, 4:5, :]
    t = _rms(x_ref[...], gpre_ref[...]) * (1.0 + scale) + shift
    t_hi = t.astype(BF16)
    t_lo = (t - t_hi.astype(F32)).astype(BF16)
    for j, r in enumerate(_pack_rows(t)):
        tp_ref[j] = r
    tm = t.shape[0]
    hh = _dot_t(wrt_ref[...], t_hi)
    logits = hh[0:n_experts] + hh[n_experts:2 * n_experts] + _dot_t(wrt_ref[0:n_experts, :], t_lo)
    scores = _sigmoid(logits)
    work = scores + brt_ref[...]
    row = lax.broadcasted_iota(jnp.int32, scores.shape, 0)
    wide = lax.broadcasted_iota(jnp.int32, (LANE, tm), 0)
    firsts, picked, hits = [], [], []
    for k in range(TOP_K):
        best = jnp.max(work, axis=0, keepdims=True)
        first = jnp.min(jnp.where(work == best, row, n_experts), axis=0, keepdims=True)
        hit = row == first
        firsts.append(first)
        picked.append(jnp.sum(jnp.where(hit, scores, 0.0), axis=0, keepdims=True))
        hits.append(wide == first + k * n_experts)
        work = jnp.where(hit, -jnp.inf, work)
    total = functools.reduce(jnp.add, picked)
    onehot = functools.reduce(jnp.add, [jnp.where(h, 1.0, 0.0) for h in hits])
    earlier = _dot(onehot.astype(BF16), triu_ref[...])
    col = jnp.broadcast_to(jnp.sum(onehot, axis=1, keepdims=True), (LANE, LANE))
    row_c = lax.broadcasted_iota(jnp.int32, (LANE, LANE), 0)
    before = jnp.zeros((LANE, LANE), F32)
    for s in range(1, TOP_K):
        before = before + jnp.where(row_c >= s * n_experts, pltpu.roll(col, s * n_experts, 0), 0.0)
    ahead = earlier + before[:, 0:1]
    row8 = lax.broadcasted_iota(jnp.int32, (SUBLANE, tm), 0)
    sel = jnp.zeros((SUBLANE, tm), jnp.int32)
    rank = jnp.zeros((SUBLANE, tm), F32)
    wgt_t = jnp.zeros((LANE, tm), F32)
    for k in range(TOP_K):
        sel = jnp.where(row8 == k, firsts[k], sel)
        rank = jnp.where(row8 == k, jnp.sum(jnp.where(hits[k], ahead, 0.0), axis=0, keepdims=True), rank)
        wgt_t = jnp.where(wide == k, picked[k] / total * ROUTED_SCALE, wgt_t)
    sel_ref[0] = sel
    rank_ref[0] = rank.astype(jnp.int32)
    cnt_ref[0] = col.astype(jnp.int32)
    wgt_ref[...] = wgt_t.T


def _route(x1, mod, seq_len, lw, *, tm):
    t, d = x1.shape
    n_tiles = t // tm
    tiles_per_seq = max(seq_len // tm, 1)
    n_experts = lw["b_router_t"].shape[0]
    rows = d // (2 * PACK_W)
    triu = jnp.tri(tm, tm, -1, dtype=BF16).T
    tile3 = lambda i: (i, 0, 0)
    return pl.pallas_call(
        functools.partial(_route_kernel, n_experts=n_experts),
        grid=(n_tiles,),
        in_specs=[
            pl.BlockSpec((tm, d), lambda i: (i, 0)),
            mod.spec(lambda i: i // tiles_per_seq),
            lw["g_pre_ffn"].spec(),
            lw["w_router_t"].spec(),
            lw["b_router_t"].spec(),
            _resident((tm, tm)),
        ],
        out_specs=[
            pl.BlockSpec((rows, tm, PACK_W), lambda i: (0, i, 0)),
            pl.BlockSpec((1, SUBLANE, tm), tile3),
            pl.BlockSpec((1, SUBLANE, tm), tile3),
            pl.BlockSpec((1, LANE, LANE), tile3),
            pl.BlockSpec((tm, LANE), lambda i: (i, 0)),
        ],
        out_shape=[
            jax.ShapeDtypeStruct((rows, t, PACK_W), jnp.int32),
            jax.ShapeDtypeStruct((n_tiles, SUBLANE, tm), jnp.int32),
            jax.ShapeDtypeStruct((n_tiles, SUBLANE, tm), jnp.int32),
            jax.ShapeDtypeStruct((n_tiles, LANE, LANE), jnp.int32),
            jax.ShapeDtypeStruct((t, LANE), F32),
        ],
        compiler_params=_params(1),
        name="moe_route",
    )(x1, mod.table, lw["g_pre_ffn"].stacked, lw["w_router_t"].stacked, lw["b_router_t"].stacked, triu)


def _expert_kernel(te_ref, meta_ref, xs_ref, wg_ref, wu_ref, wd_ref, ys_ref, wg_sc, wu_sc, wd_sc):
    i = pl.program_id(0)
    live = i < meta_ref[0]
    new_expert = (i == 0) | (te_ref[i] != te_ref[jnp.maximum(i - 1, 0)])

    @pl.when(live & new_expert)
    def _():
        wg_sc[...] = wg_ref[0, 0].astype(BF16)
        wu_sc[...] = wu_ref[0, 0].astype(BF16)
        wd_sc[...] = wd_ref[0, 0].astype(BF16)

    @pl.when(live)
    def _():
        x = _unpack_rows([xs_ref[j] for j in range(xs_ref.shape[0])]).astype(BF16)
        gate = _dot(x, wg_sc[...])
        act = (gate * _sigmoid(gate) * _dot(x, wu_sc[...])).astype(BF16)
        for j, r in enumerate(_pack_rows(_dot(act, wd_sc[...]))):
            ys_ref[j] = r


def _experts(xs, tile_expert, meta, w_gate_e, w_up_e, w_down_e, layer, *, tm):
    rows, p, _ = xs.shape
    n_experts, d, f = w_gate_e.shape[1:]

    def slot(i, te, meta):
        return (0, jnp.minimum(i, meta[0] - 1), 0)

    grid_spec = pltpu.PrefetchScalarGridSpec(
        num_scalar_prefetch=2,
        grid=(p // tm,),
        in_specs=[
            pl.BlockSpec((rows, tm, PACK_W), slot),
            pl.BlockSpec((1, 1, d, f), lambda i, te, meta: (layer, te[i], 0, 0)),
            pl.BlockSpec((1, 1, d, f), lambda i, te, meta: (layer, te[i], 0, 0)),
            pl.BlockSpec((1, 1, f, d), lambda i, te, meta: (layer, te[i], 0, 0)),
        ],
        out_specs=pl.BlockSpec((rows, tm, PACK_W), slot),
        scratch_shapes=[pltpu.VMEM((d, f), BF16), pltpu.VMEM((d, f), BF16), pltpu.VMEM((f, d), BF16)],
    )
    return pl.pallas_call(
        _expert_kernel,
        grid_spec=grid_spec,
        out_shape=jax.ShapeDtypeStruct(xs.shape, jnp.int32),
        compiler_params=_params(1),
        name="moe_experts",
    )(tile_expert, meta, xs, w_gate_e, w_up_e, w_down_e)


def _combine_kernel(x_ref, mod_ref, gpre_ref, gpost_ref, wgs_ref, wus_ref, wds_ref, yk_ref, wgt_ref,
                    out_ref):
    shift = mod_ref[0, 3:4, :]
    scale = mod_ref[0, 4:5, :]
    x = x_ref[...]
    t_hi = (_rms(x, gpre_ref[...]) * (1.0 + scale) + shift).astype(BF16)
    gate = _dot(t_hi, wgs_ref[...])
    act = (gate * _sigmoid(gate) * _dot(t_hi, wus_ref[...])).astype(BF16)
    acc = _dot(act, wds_ref[...])
    for k in range(TOP_K):
        y = _unpack_rows([yk_ref[k, j] for j in range(yk_ref.shape[1])])
        acc = acc + wgt_ref[:, k:k + 1] * y
    g2 = mod_ref[0, 5:6, :]
    out_ref[...] = x + g2 * _rms(acc, gpost_ref[...])


def _combine(x1, mod, seq_len, lw, yk, wgt, *, tm):
    t, d = x1.shape
    tiles_per_seq = max(seq_len // tm, 1)
    rows = yk.shape[1]
    row = lambda i: (i, 0)
    return pl.pallas_call(
        _combine_kernel,
        grid=(t // tm,),
        in_specs=[
            pl.BlockSpec((tm, d), row),
            mod.spec(lambda i: i // tiles_per_seq),
            lw["g_pre_ffn"].spec(),
            lw["g_post_ffn"].spec(),
            lw["w_gate_s"].spec(),
            lw["w_up_s"].spec(),
            lw["w_down_s"].spec(),
            pl.BlockSpec((TOP_K, rows, tm, PACK_W), lambda i: (0, 0, i, 0)),
            pl.BlockSpec((tm, LANE), row),
        ],
        out_specs=pl.BlockSpec((tm, d), row),
        out_shape=jax.ShapeDtypeStruct((t, d), F32),
        compiler_params=_params(1),
        name="moe_combine",
    )(x1, mod.table, lw["g_pre_ffn"].stacked, lw["g_post_ffn"].stacked, lw["w_gate_s"].stacked, lw["w_up_s"].stacked,
      lw["w_down_s"].stacked, yk, wgt)


def _sc_mesh():
    return plsc.VectorSubcoreMesh(core_axis_name="core", subcore_axis_name="subcore")


def _sc_scatter_rows(src, idx, n_out):
    n_lists, n = idx.shape
    width = src.shape[1]

    @pl.kernel(out_type=jax.ShapeDtypeStruct((n_out, width), src.dtype), mesh=_sc_mesh(), scratch_types=[])
    def scatter(x_hbm, *refs):
        i_hbms, o_hbm = refs[:n_lists], refs[n_lists]

        def body(x_vmem, *i_vmems):
            for i_vmem in i_vmems:
                pltpu.sync_copy(x_vmem, o_hbm.at[i_vmem.at[0]])

        pltpu.emit_pipeline(
            body,
            grid=(n // SC_WINDOW,),
            in_specs=[pl.BlockSpec((SC_WINDOW, width), lambda i: (i, 0))]
            + [pl.BlockSpec((1, SC_WINDOW), lambda i, r=r: (r, i)) for r in range(n_lists)],
            out_specs=[],
            core_axis_name=("core", "subcore"),
            dimension_semantics=(pltpu.PARALLEL,),
        )(x_hbm, *i_hbms)

    return scatter(src, *([idx] * n_lists))


def _sc_gather_rows(table, idx):
    n = idx.shape[0]
    width = table.shape[1]

    @pl.kernel(out_type=jax.ShapeDtypeStruct((n, width), table.dtype), mesh=_sc_mesh(), scratch_types=[])
    def gather(x_hbm, i_hbm, o_hbm):
        def body(i_vmem, o_vmem):
            pltpu.sync_copy(x_hbm.at[i_vmem.at[0]], o_vmem)

        pltpu.emit_pipeline(
            body,
            grid=(n // SC_WINDOW,),
            in_specs=[pl.BlockSpec((1, SC_WINDOW), lambda i: (0, i))],
            out_specs=[pl.BlockSpec((SC_WINDOW, width), lambda i: (i, 0))],
            core_axis_name=("core", "subcore"),
            dimension_semantics=(pltpu.PARALLEL,),
        )(i_hbm, o_hbm)

    return gather(table, idx.reshape(1, n))


def _moe_sparse(x1, mod, seq_len, lw, w_gate_e, w_up_e, w_down_e, layer, *, tm, tm_e):
    t = x1.shape[0]
    n_experts = lw["b_router_t"].shape[0]
    assert n_experts * TOP_K == LANE
    n_tok_tiles = t // tm
    n_row_tiles = (t * TOP_K) // tm_e + n_experts
    p = n_row_tiles * tm_e
    experts = jnp.arange(n_experts, dtype=jnp.int32)

    tp, sel, rank, cnt, wgt = _route(x1, mod, seq_len, lw, tm=tm)
    rows = tp.shape[0]
    cnt = cnt[:, :, 0].reshape(n_tok_tiles, TOP_K, n_experts).sum(axis=1)
    padded = (cnt.sum(axis=0) + tm_e - 1) // tm_e * tm_e
    group_end = jnp.cumsum(padded)
    base = (group_end - padded)[None, :] + jnp.cumsum(cnt, axis=0) - cnt
    chosen = sel[:, :TOP_K, :, None] == experts
    pos = jnp.sum(jnp.where(chosen, base[:, None, None, :], 0), axis=-1) + rank[:, :TOP_K, :]
    pos = pos.transpose(1, 0, 2).reshape(TOP_K, t)

    n_used = group_end[-1] // tm_e
    tile_start = jnp.arange(n_row_tiles, dtype=jnp.int32) * tm_e
    tile_expert = jnp.sum(tile_start[:, None] >= group_end[None, :], axis=1).astype(jnp.int32)
    tile_expert = jnp.minimum(tile_expert, n_experts - 1)
    tile_expert = jnp.where(tile_start < group_end[-1], tile_expert, tile_expert[n_used - 1])
    meta = jnp.stack([n_used, n_used]).astype(jnp.int32)

    idx = pos[:, None, :] + (jnp.arange(rows, dtype=jnp.int32) * p)[None, :, None]
    xs = _sc_scatter_rows(tp.reshape(rows * t, PACK_W), idx.reshape(TOP_K, rows * t), rows * p)
    after = yield wgt
    if after is not None:
        xs = _after(xs, after)
    ys = _experts(xs.reshape(rows, p, PACK_W), tile_expert, meta, w_gate_e, w_up_e, w_down_e, layer, tm=tm_e)
    yk = _sc_gather_rows(ys.reshape(rows * p, PACK_W), idx.reshape(-1))
    yield ys
    yield _combine(x1, mod, seq_len, lw, yk.reshape(TOP_K, rows, t, PACK_W), wgt, tm=tm)


def _rope_tables(n):
    rows = n // GRID_W
    r, col = jnp.meshgrid(jnp.arange(rows), jnp.arange(GRID_W), indexing="ij")
    r = r.reshape(-1).astype(F32)
    col = col.reshape(-1).astype(F32)
    pairs = QK_ROPE // 4
    inv = ROPE_BASE ** (-jnp.arange(pairs, dtype=F32) / pairs)
    ang = jnp.concatenate([r[:, None] * inv, col[:, None] * inv], axis=-1)
    cos, sin = jnp.cos(ang), jnp.sin(ang)
    pad = HEAD_PAD - QK_NOPE - QK_ROPE
    cos_t = jnp.concatenate([jnp.ones((n, QK_NOPE), F32), cos, cos, jnp.zeros((n, pad), F32)], axis=1)
    sin_t = jnp.concatenate([jnp.zeros((n, QK_NOPE), F32), sin, sin, jnp.zeros((n, pad), F32)], axis=1)
    return cos_t, sin_t


def _identity_tables(n):
    pad = HEAD_PAD - QK_NOPE - QK_ROPE
    cos_t = jnp.concatenate([jnp.ones((n, QK_NOPE + QK_ROPE), F32), jnp.zeros((n, pad), F32)], axis=1)
    return cos_t, jnp.zeros((n, HEAD_PAD), F32)


def _position_dft(n):
    nb = 64 if n % 64 == 0 else 1
    na = n // nb
    m = jnp.arange(n, dtype=jnp.int32)[None, :]
    ang_a = ((jnp.arange(na, dtype=jnp.int32)[:, None] * m) % na).astype(F32) * (2.0 * np.pi / na)
    ang_b = ((jnp.arange(nb, dtype=jnp.int32)[:, None] * m) % n).astype(F32) * (2.0 * np.pi / n)
    ca, sa = jnp.cos(ang_a)[:, None, :], jnp.sin(ang_a)[:, None, :]
    cb, sb = jnp.cos(ang_b)[None, :, :], jnp.sin(ang_b)[None, :, :]
    norm = 1.0 / np.sqrt(n)
    cos = ((ca * cb - sa * sb) * norm).reshape(n, n)
    sin = ((sa * cb + ca * sb) * (-norm)).reshape(n, n)
    return jnp.concatenate([cos, sin], axis=1).astype(BF16)


def _channel_dft(width):
    gc = width // FOURIER_GROUPS
    idx = (jnp.arange(gc, dtype=jnp.int32)[:, None] * jnp.arange(gc, dtype=jnp.int32)[None, :]) % gc
    ang = idx.astype(F32) * (2.0 * np.pi / gc)
    eye = jnp.eye(FOURIER_GROUPS, dtype=F32)
    norm = 1.0 / np.sqrt(gc)
    return jnp.concatenate([jnp.kron(eye, jnp.cos(ang) * norm), jnp.kron(eye, jnp.sin(ang) * norm)],
                           axis=1).astype(BF16)


def _w1_kernel(w_ref, o_ref, *, kv_rank):
    cols = w_ref.shape[2]
    kv_end = kv_rank + QK_ROPE
    half = QK_ROPE // 2
    tail = HEAD_PAD - QK_NOPE - QK_ROPE
    o_kpe = kv_rank
    o_rot = o_kpe + HEAD_PAD
    o_rest = o_rot + HEAD_PAD
    dt = o_ref.dtype
    o_ref[0, 0:kv_rank, :] = w_ref[0, 0:kv_rank, :].astype(dt)
    for base in (o_kpe, o_rot):
        o_ref[0, base:base + QK_NOPE, :] = jnp.zeros((QK_NOPE, cols), dt)
        o_ref[0, base + QK_NOPE + QK_ROPE:base + HEAD_PAD, :] = jnp.zeros((tail, cols), dt)
    o_ref[0, o_kpe + QK_NOPE:o_kpe + QK_NOPE + QK_ROPE, :] = w_ref[0, kv_rank:kv_end, :].astype(dt)
    o_ref[0, o_rot + QK_NOPE:o_rot + QK_NOPE + half, :] = (-w_ref[0, kv_rank + half:kv_end, :]).astype(dt)
    o_ref[0, o_rot + QK_NOPE + half:o_rot + QK_NOPE + QK_ROPE, :] = w_ref[0, kv_rank:kv_rank + half, :].astype(dt)
    o_ref[0, o_rest:, :] = w_ref[0, kv_end:, :].astype(dt)


def _prep_w1(w_in_t, kv_rank):
    n_layers, width, d = w_in_t.shape
    out_w = width - QK_ROPE + 2 * HEAD_PAD
    tc = _tile(d, 256)
    return pl.pallas_call(
        functools.partial(_w1_kernel, kv_rank=kv_rank),
        grid=(n_layers, d // tc),
        in_specs=[pl.BlockSpec((1, width, tc), lambda l, i: (l, 0, i))],
        out_specs=pl.BlockSpec((1, out_w, tc), lambda l, i: (l, 0, i)),
        out_shape=jax.ShapeDtypeStruct((n_layers, out_w, d), BF16),
        compiler_params=_params(2),
        name="prep_w1",
    )(w_in_t)


def _stacked_weights(g_pre_mix, g_post_mix, g_pre_ffn, g_post_ffn, b_gate, g_q, w_uq, g_kv, w_ukv,
                     w_mla_out, conv_w, w_conv_out, w_four_out, w_out, w_router, b_router, w_gate_s,
                     w_up_s, w_down_s):
    n_layers, kv_rank = g_kv.shape
    q_rank = g_q.shape[1]
    qk_dim = QK_NOPE + QK_ROPE
    uq = w_uq.reshape(n_layers, q_rank, N_HEADS, qk_dim) * (qk_dim ** -0.5 * np.log2(np.e))
    zq = jnp.zeros((n_layers, q_rank, N_HEADS, HEAD_PAD - qk_dim), F32)
    wuq = jnp.concatenate([uq, zq], axis=-1).reshape(n_layers, q_rank, N_HEADS * HEAD_PAD).astype(BF16)
    ukv = w_ukv.reshape(n_layers, kv_rank, N_HEADS, QK_NOPE + V_DIM)
    zk = jnp.zeros((n_layers, kv_rank, N_HEADS, HEAD_PAD - QK_NOPE), F32)
    zv = jnp.zeros((n_layers, kv_rank, N_HEADS, HEAD_PAD - V_DIM), F32)
    wuk = jnp.concatenate([ukv[..., :QK_NOPE], zk], axis=-1).reshape(n_layers, kv_rank, -1).astype(BF16)
    wuv = jnp.concatenate([ukv[..., QK_NOPE:], zv], axis=-1).reshape(n_layers, kv_rank, -1).astype(BF16)
    wr_hi = w_router.astype(BF16)
    wr_lo = (w_router - wr_hi.astype(F32)).astype(BF16)
    wr_t = jnp.concatenate([jnp.swapaxes(wr_hi, 1, 2), jnp.swapaxes(wr_lo, 1, 2)], axis=1)
    return {
        "g_pre_mix": g_pre_mix[:, None], "g_post_mix": g_post_mix[:, None],
        "g_pre_ffn": g_pre_ffn[:, None], "g_post_ffn": g_post_ffn[:, None],
        "b_gate": b_gate[:, None], "g_q": g_q[:, None], "g_kv": g_kv[:, None],
        "wuq": wuq, "wuk": wuk, "wuv": wuv, "conv_w": conv_w,
        "w_mla_out": w_mla_out.astype(BF16), "w_conv_out": w_conv_out.astype(BF16),
        "w_four_out": w_four_out.astype(BF16), "w_out": w_out.astype(BF16),
        "w_router_t": wr_t, "b_router_t": b_router[:, :, None],
        "w_gate_s": w_gate_s.astype(BF16), "w_up_s": w_up_s.astype(BF16), "w_down_s": w_down_s.astype(BF16),
    }


def _tile(n, pref):
    return pref if n % pref == 0 else n


def kernel(x, c, ctx, c_ctx, w_ada, b_ada, g_pre_mix, g_post_mix, g_pre_ffn, g_post_ffn, w_in, b_gate,
           g_q, w_uq, g_kv, w_ukv, w_mla_out, conv_w, w_conv_out, w_four_out, w_out, w_router, b_router,
           w_gate_e, w_up_e, w_down_e, w_gate_s, w_up_s, w_down_s):
    batch, seq, d = x.shape
    n_ctx = ctx.shape[1]
    n_layers = w_in.shape[0]
    xs = x.reshape(batch * seq, d)
    cs = ctx.reshape(batch * n_ctx, d)

    mod_rows = -(-(batch + 1) // (2 * SUBLANE)) * (2 * SUBLANE)
    c_all = jnp.concatenate([c, c_ctx[None], jnp.zeros((mod_rows - batch - 1, d), F32)], axis=0)
    ada = _ada(c_all, w_ada, b_ada).reshape(n_layers, mod_rows, 6, d)

    tm_x = _tile(seq, 512)
    tm_x_wide = _tile(seq, 2 * MERGE_ROWS)
    tq_x = _tile(seq, 2 * ATTN_ROWS)
    tm_c = _tile(n_ctx, 256)
    tm_c_mid = _tile(batch * n_ctx, 512)
    tm_c_wide = _tile(batch * n_ctx, 1024)
    te_x, te_c = 1024, 512

    tab_x = _rope_tables(seq)
    tab_c = _identity_tables(tm_c_mid)
    cs_x = _position_dft(seq)
    cs_c = _position_dft(n_ctx)
    dc = _channel_dft(w_four_out.shape[1])
    w1 = _prep_w1(jnp.swapaxes(w_in, 1, 2), g_kv.shape[1])
    stacked = _stacked_weights(g_pre_mix, g_post_mix, g_pre_ffn, g_post_ffn, b_gate, g_q, w_uq, g_kv, w_ukv,
                               w_mla_out, conv_w, w_conv_out, w_four_out, w_out, w_router, b_router,
                               w_gate_s, w_up_s, w_down_s)
    v_one = jnp.tile((jnp.arange(HEAD_PAD) == V_DIM).astype(F32), N_HEADS)[None]

    for l in range(n_layers):
        last = l == n_layers - 1
        lw = {name: _Layer(arr, l) for name, arr in stacked.items()}
        lw.update(w1=w1, layer=l, dc=dc, v_one=v_one)
        mod_x = _Mod(ada, l, 0, batch)
        mod_c = _Mod(ada, l, batch, 1)

        pc = _inproj(cs, mod_c, n_ctx, lw, tab_c, kv_only=last, tm=tm_c_mid)
        px = _inproj(xs, mod_x, seq, lw, tab_x, kv_only=False, tm=tm_x)
        o_x = _attention(px["q"], [(pc["k"], pc["v"], n_ctx), (px["k"], px["v"], seq)], batch, seq, tq=tq_x)
        f_x = _fourier(px["ab"], cs_x, batch, seq, tn=seq)
        x1 = _merge(xs, mod_x, seq, px, o_x, f_x, lw, tm=tm_x_wide)
        moe_x = _moe_sparse(x1, mod_x, seq, lw, w_gate_e, w_up_e, w_down_e, l, tm=tm_x_wide, tm_e=te_x)
        if last:
            xs = list(moe_x)[-1]
        else:
            routed_x = next(moe_x)
            q_c = _after(pc["q"], routed_x)
            o_c = _attention(q_c, [(pc["k"], pc["v"], n_ctx)], batch, n_ctx, tq=tm_c)
            f_c = _fourier(pc["ab"], cs_c, batch, n_ctx, tn=tm_c)
            c1 = _merge(cs, mod_c, n_ctx, pc, o_c, f_c, lw, tm=tm_c_wide)
            moe_c = _moe_sparse(c1, mod_c, batch * n_ctx, lw, w_gate_e, w_up_e, w_down_e, l, tm=tm_c_wide,
                                tm_e=te_c)
            next(moe_c)
            experts_x = moe_x.send(None)
            moe_c.send(experts_x)
            xs = next(moe_x)
            cs = next(moe_c)
    return xs.reshape(batch, seq, d)
```

```python
import functools

import numpy as np
import jax
import jax.numpy as jnp
from jax import lax
from jax.experimental import pallas as pl
from jax.experimental.pallas import tpu as pltpu
from jax.experimental.pallas import tpu_sc as plsc

N_HEADS = 8
QK_NOPE = 64
QK_ROPE = 32
V_DIM = 64
GRID_W = 64
ROPE_BASE = 10000.0
FOURIER_GROUPS = 4
TOP_K = 4
ROUTED_SCALE = 2.5
N_BRANCHES = 3
EPS = 1e-6

LANE = 128
SUBLANE = 8
HEAD_PAD = LANE
VMEM_LIMIT = 56 * 1024 * 1024
PACK_W = 256
SC_WINDOW = 128
ATTN_ROWS = 512
MERGE_ROWS = 512

F32 = jnp.float32
BF16 = jnp.bfloat16


def _rms(x, g):
    return x * lax.rsqrt(jnp.mean(x * x, axis=-1, keepdims=True) + EPS) * g


def _sigmoid(x):
    return 1.0 / (1.0 + jnp.exp(-x))


def _dot(a, b):
    return jnp.dot(a, b, preferred_element_type=F32)


def _dot_t(a, b_t):
    return lax.dot_general(a, b_t, (((1,), (1,)), ((), ())), preferred_element_type=F32)


def _skewed(stages, n_chunks, chunk):
    live = [None] * n_chunks
    for step in range(n_chunks + len(stages) - 1):
        for c in range(n_chunks):
            s = step - c
            if 0 <= s < len(stages):
                live[c] = stages[s](c * chunk) if s == 0 else stages[s](c * chunk, live[c])


def _after(value, dependency):
    return lax.optimization_barrier((value, dependency))[0]


def _resident(shape):
    nd = len(shape)
    return pl.BlockSpec(shape, lambda *_: (0,) * nd, pipeline_mode=pl.Buffered(1))


class _Layer:
    def __init__(self, stacked, index):
        self.stacked, self.index = stacked, index
        self.shape = stacked.shape[1:]

    def spec(self):
        index = (self.index,) + (0,) * len(self.shape)
        return pl.BlockSpec((None,) + self.shape, lambda *_: index, pipeline_mode=pl.Buffered(1))


class _Mod:
    def __init__(self, table, layer, row0, n):
        self.table, self.layer, self.row0, self.n = table, layer, row0, n

    def spec(self, seq_of_step):
        layer, row0, n = self.layer, self.row0, self.n
        return pl.BlockSpec((None, 1) + self.table.shape[2:],
                            lambda i: (layer, row0 + seq_of_step(i) % n, 0, 0))


def _params(n_grid):
    return pltpu.CompilerParams(dimension_semantics=("arbitrary",) * n_grid,
                                vmem_limit_bytes=VMEM_LIMIT)


def _ada_kernel(c_ref, w_ref, b_ref, o_ref):
    c = c_ref[...]
    a = (c * _sigmoid(c)).astype(BF16)
    o_ref[0] = _dot(a, w_ref[0].astype(BF16)) + b_ref[0]


def _ada(c_all, w_ada, b_ada):
    n_layers, d, n_out = w_ada.shape
    rows = c_all.shape[0]
    tn = _tile(n_out, 1536)
    return pl.pallas_call(
        _ada_kernel,
        grid=(n_layers, n_out // tn),
        in_specs=[
            pl.BlockSpec((rows, d), lambda l, j: (0, 0)),
            pl.BlockSpec((1, d, tn), lambda l, j: (l, 0, j)),
            pl.BlockSpec((1, 1, tn), lambda l, j: (l, 0, j)),
        ],
        out_specs=pl.BlockSpec((1, rows, tn), lambda l, j: (l, 0, j)),
        out_shape=jax.ShapeDtypeStruct((n_layers, rows, n_out), F32),
        compiler_params=_params(2),
        name="ada",
    )(c_all, w_ada, b_ada.reshape(n_layers, 1, n_out))


def _inproj_kernel(*refs, kv_only, kv_rank, q_rank, conv_w, four_w, d_model):
    if kv_only:
        (x_ref, mod_ref, gpre_ref, w1_ref, gkv_ref, wuk_ref, wuv_ref, vone_ref, cos_ref, sin_ref,
         k_ref, v_ref) = refs
    else:
        (x_ref, mod_ref, gpre_ref, w1_ref, gkv_ref, wuk_ref, wuv_ref, vone_ref, cos_ref, sin_ref,
         bg_ref, gq_ref, wuq_ref, dc_ref,
         k_ref, v_ref, q_ref, cb_ref, cc_ref, cu_ref, ab_ref, gate_ref) = refs

    x = x_ref[...]
    shift = mod_ref[0, 0:1, :]
    scale = mod_ref[0, 1:2, :]
    h = (_rms(x, gpre_ref[...]) * (1.0 + scale) + shift).astype(BF16)
    cos = cos_ref[...]
    sin = sin_ref[...]

    o_kpe = kv_rank
    o_rot = o_kpe + HEAD_PAD
    o_q = o_rot + HEAD_PAD
    p = _dot_t(h, w1_ref[0, 0:o_q, :])
    ckv = _rms(p[:, 0:kv_rank], gkv_ref[...]).astype(BF16)
    kpe = p[:, o_kpe:o_rot] * cos + p[:, o_rot:o_q] * sin
    k = _dot(ckv, wuk_ref[...]) + jnp.concatenate([kpe] * N_HEADS, axis=1)
    k_ref[...] = k.astype(k_ref.dtype)
    v_ref[...] = (_dot(ckv, wuv_ref[...]) + vone_ref[...]).astype(v_ref.dtype)
    if kv_only:
        return

    o_cb = o_q + q_rank
    cq = _rms(_dot_t(h, w1_ref[0, o_q:o_cb, :]), gq_ref[...]).astype(BF16)
    half = QK_ROPE // 2
    lane = lax.broadcasted_iota(jnp.int32, (1, HEAD_PAD), 1)
    first = (lane >= QK_NOPE) & (lane < QK_NOPE + half)
    cos_h = jnp.concatenate([cos] * N_HEADS, axis=1)
    sin_h = jnp.concatenate([jnp.where(first, -sin, sin)] * N_HEADS, axis=1)
    first_h = jnp.concatenate([first] * N_HEADS, axis=1)
    lin = _dot(cq, wuq_ref[...])
    width = lin.shape[1]
    partner = jnp.where(first_h, pltpu.roll(lin, width - half, 1), pltpu.roll(lin, half, 1))
    q_ref[...] = (lin * cos_h + partner * sin_h).astype(q_ref.dtype)

    o_cc = o_cb + conv_w
    o_cu = o_cc + conv_w
    o_four = o_cu + conv_w
    cb_ref[...] = _dot_t(h, w1_ref[0, o_cb:o_cc, :]).astype(cb_ref.dtype)
    cc_ref[...] = _dot_t(h, w1_ref[0, o_cc:o_cu, :]).astype(cc_ref.dtype)
    cu_ref[...] = _dot_t(h, w1_ref[0, o_cu:o_four, :]).astype(cu_ref.dtype)

    o_gate = o_four + four_w
    uf = _dot_t(h, w1_ref[0, o_four:o_gate, :]).astype(BF16)
    ab_ref[...] = _dot(uf, dc_ref[...]).astype(ab_ref.dtype)

    for j in range(N_BRANCHES):
        lo = o_gate + j * d_model
        z = _dot_t(h, w1_ref[0, lo:lo + d_model, :]) + bg_ref[:, j * d_model:(j + 1) * d_model]
        gate_ref[:, j * d_model:(j + 1) * d_model] = _sigmoid(z).astype(gate_ref.dtype)


def _inproj(xs, mod, seq_len, lw, tables, *, kv_only, tm):
    t, d = xs.shape
    cos_t, sin_t = tables
    table_tiles = cos_t.shape[0] // tm
    kv_rank = lw["g_kv"].shape[1]
    q_rank = lw["g_q"].shape[1]
    conv_w = lw["conv_w"].shape[1]
    four_w = lw["dc"].shape[0]
    n_k = N_HEADS * HEAD_PAD

    def row(i):
        return (i, 0)

    def tab_map(i):
        return (i % table_tiles, 0)

    w1, layer = lw["w1"], lw["layer"]
    w1_rows = kv_rank + 2 * HEAD_PAD if kv_only else w1.shape[1]
    in_specs = [
        pl.BlockSpec((tm, d), row),
        mod.spec(lambda i: i * tm // seq_len),
        lw["g_pre_mix"].spec(),
        pl.BlockSpec((1, w1_rows, d), lambda i: (layer, 0, 0), pipeline_mode=pl.Buffered(1)),
        lw["g_kv"].spec(),
        lw["wuk"].spec(),
        lw["wuv"].spec(),
        _resident(lw["v_one"].shape),
        pl.BlockSpec((tm, HEAD_PAD), tab_map),
        pl.BlockSpec((tm, HEAD_PAD), tab_map),
    ]
    args = [xs, mod.table, lw["g_pre_mix"].stacked, w1, lw["g_kv"].stacked, lw["wuk"].stacked, lw["wuv"].stacked,
            lw["v_one"], cos_t, sin_t]
    out_shape = [jax.ShapeDtypeStruct((t, n_k), BF16), jax.ShapeDtypeStruct((t, n_k), BF16)]
    out_specs = [pl.BlockSpec((tm, n_k), row), pl.BlockSpec((tm, n_k), row)]
    if not kv_only:
        in_specs += [
            lw["b_gate"].spec(),
            lw["g_q"].spec(),
            lw["wuq"].spec(),
            _resident(lw["dc"].shape),
        ]
        args += [lw["b_gate"].stacked, lw["g_q"].stacked, lw["wuq"].stacked, lw["dc"]]
        widths = [n_k, conv_w, conv_w, conv_w, 2 * four_w, N_BRANCHES * d]
        out_shape += [jax.ShapeDtypeStruct((t, w), BF16) for w in widths]
        out_specs += [pl.BlockSpec((tm, w), row) for w in widths]
    outs = pl.pallas_call(
        functools.partial(_inproj_kernel, kv_only=kv_only, kv_rank=kv_rank, q_rank=q_rank,
                          conv_w=conv_w, four_w=four_w, d_model=d),
        grid=(t // tm,),
        in_specs=in_specs,
        out_specs=out_specs,
        out_shape=out_shape,
        compiler_params=_params(1),
        name="inproj_kv" if kv_only else "inproj",
    )(*args)
    names = ["k", "v", "q", "cb", "cc", "cu", "ab", "gate"]
    return dict(zip(names, outs))


def _attn_kernel(*refs, n_seg):
    q_ref = refs[0]
    o_ref = refs[-1]
    chunk = min(q_ref.shape[0], ATTN_ROWS)
    for c in range(q_ref.shape[0] // chunk):
        rows = slice(c * chunk, (c + 1) * chunk)
        outs = []
        for hh in range(N_HEADS):
            head = slice(hh * HEAD_PAD, (hh + 1) * HEAD_PAD)
            qh = q_ref[rows, head]
            s = [_dot_t(qh, refs[1 + 2 * i][:, head]) for i in range(n_seg)]
            m = functools.reduce(jnp.maximum, [jnp.max(si, axis=-1, keepdims=True) for si in s])
            acc = functools.reduce(jnp.add, [
                _dot(jnp.exp2((s[i] - m).astype(BF16)), refs[2 + 2 * i][:, head]) for i in range(n_seg)])
            outs.append(acc[:, 0:V_DIM] / acc[:, V_DIM:V_DIM + 1])
        o_ref[rows, :] = jnp.concatenate(outs, axis=1).astype(o_ref.dtype)


def _attention(q, segs, batch, seq_q, *, tq):
    t = q.shape[0]
    qt = seq_q // tq
    n_k = N_HEADS * HEAD_PAD
    in_specs = [pl.BlockSpec((tq, n_k), lambda b, j: (b * qt + j, 0))]
    args = [q]
    for k, v, m in segs:
        in_specs.append(pl.BlockSpec((m, n_k), lambda b, j: (b, 0)))
        in_specs.append(pl.BlockSpec((m, n_k), lambda b, j: (b, 0)))
        args += [k, v]
    return pl.pallas_call(
        functools.partial(_attn_kernel, n_seg=len(segs)),
        grid=(batch, qt),
        in_specs=in_specs,
        out_specs=pl.BlockSpec((tq, N_HEADS * V_DIM), lambda b, j: (b * qt + j, 0)),
        out_shape=jax.ShapeDtypeStruct((t, N_HEADS * V_DIM), BF16),
        compiler_params=_params(2),
        name="attention",
    )(*args)


def _four_kernel(cs_ref, ab_ref, o_ref, *, n, fw):
    o = _dot(cs_ref[:, 0:n], ab_ref[:, 0:fw]) + _dot(cs_ref[:, n:2 * n], ab_ref[:, fw:2 * fw])
    o_ref[...] = o.astype(o_ref.dtype)


def _fourier(ab, cs, batch, seq_len, *, tn):
    t, fw2 = ab.shape
    fw = fw2 // 2
    nt = seq_len // tn
    cs_spec = (_resident(cs.shape) if nt == 1
               else pl.BlockSpec((tn, 2 * seq_len), lambda b, j: (j, 0)))
    return pl.pallas_call(
        functools.partial(_four_kernel, n=seq_len, fw=fw),
        grid=(batch, nt),
        in_specs=[
            cs_spec,
            pl.BlockSpec((seq_len, fw2), lambda b, j: (b, 0)),
        ],
        out_specs=pl.BlockSpec((tn, fw), lambda b, j: (b * nt + j, 0)),
        out_shape=jax.ShapeDtypeStruct((t, fw), BF16),
        compiler_params=_params(2),
        name="fourier",
    )(cs, ab)


def _merge_kernel(x_ref, mod_ref, o_ref, cb_ref, cc_ref, cu_ref, ccp_ref, cup_ref, ccn_ref,
                  cun_ref, f_ref, gate_ref, convw_ref, wmo_ref, wco_ref, wfo_ref, wout_ref,
                  gpost_ref, out_ref, pad_ref, *, seq_len, tm, d_model):
    i = pl.program_id(0)
    has_prev = ((i * tm) % seq_len != 0).astype(F32)
    has_next = (((i + 1) * tm) % seq_len != 0).astype(F32)
    lo = SUBLANE
    pad_ref[0:lo, :] = ccp_ref[...].astype(F32) * cup_ref[...].astype(F32) * has_prev
    pad_ref[lo:lo + tm, :] = cc_ref[...].astype(F32) * cu_ref[...].astype(F32)
    pad_ref[lo + tm:2 * lo + tm, :] = ccn_ref[...].astype(F32) * cun_ref[...].astype(F32) * has_next
    d = d_model
    g1 = mod_ref[0, 2:3, :]
    chunk = min(tm, MERGE_ROWS)

    def conv_stage(r0):
        before = pad_ref[lo - 1 + r0:lo - 1 + r0 + chunk, :]
        after = pad_ref[lo + 1 + r0:lo + 1 + r0 + chunk, :]
        if tm > seq_len:
            at = (lax.broadcasted_iota(jnp.int32, (chunk, 1), 0) + r0) % seq_len
            before = jnp.where(at == 0, 0.0, before)
            after = jnp.where(at == seq_len - 1, 0.0, after)
        conv = (before * convw_ref[0:1, :] + pad_ref[lo + r0:lo + r0 + chunk, :] * convw_ref[1:2, :]
                + after * convw_ref[2:3, :])
        return (cb_ref[r0:r0 + chunk, :].astype(F32) * conv).astype(BF16)

    def branch_stage(r0, conv_in):
        return (_dot(o_ref[r0:r0 + chunk, :], wmo_ref[...]), _dot(conv_in, wco_ref[...]),
                _dot(f_ref[r0:r0 + chunk, :], wfo_ref[...]))

    def gate_stage(r0, ys):
        rows = slice(r0, r0 + chunk)
        return (gate_ref[rows, 0:d].astype(F32) * ys[0] + gate_ref[rows, d:2 * d].astype(F32) * ys[1]
                + gate_ref[rows, 2 * d:3 * d].astype(F32) * ys[2]).astype(BF16)

    def out_stage(r0, merged):
        return _dot(merged, wout_ref[...])

    def tail_stage(r0, y):
        rows = slice(r0, r0 + chunk)
        out_ref[rows, :] = x_ref[rows, :] + g1 * _rms(y, gpost_ref[...])

    _skewed((conv_stage, branch_stage, gate_stage, out_stage, tail_stage), tm // chunk, chunk)


def _merge(xs, mod, seq_len, pr, o, four, lw, *, tm):
    t, d = xs.shape
    assert tm % seq_len == 0 or seq_len % tm == 0
    cw = lw["conv_w"].shape[1]
    fw = four.shape[1]
    hb = tm // SUBLANE
    last_hb = t // SUBLANE - 1

    def row(i):
        return (i, 0)

    def prev_map(i):
        return (jnp.maximum(i * hb - 1, 0), 0)

    def next_map(i):
        return (jnp.minimum((i + 1) * hb, last_hb), 0)

    in_specs = [
        pl.BlockSpec((tm, d), row),
        mod.spec(lambda i: i * tm // seq_len),
        pl.BlockSpec((tm, o.shape[1]), row),
        pl.BlockSpec((tm, cw), row),
        pl.BlockSpec((tm, cw), row),
        pl.BlockSpec((tm, cw), row),
        pl.BlockSpec((SUBLANE, cw), prev_map),
        pl.BlockSpec((SUBLANE, cw), prev_map),
        pl.BlockSpec((SUBLANE, cw), next_map),
        pl.BlockSpec((SUBLANE, cw), next_map),
        pl.BlockSpec((tm, fw), row),
        pl.BlockSpec((tm, N_BRANCHES * d), row),
        lw["conv_w"].spec(),
        lw["w_mla_out"].spec(),
        lw["w_conv_out"].spec(),
        lw["w_four_out"].spec(),
        lw["w_out"].spec(),
        lw["g_post_mix"].spec(),
    ]
    return pl.pallas_call(
        functools.partial(_merge_kernel, seq_len=seq_len, tm=tm, d_model=d),
        grid=(t // tm,),
        in_specs=in_specs,
        out_specs=pl.BlockSpec((tm, d), row),
        out_shape=jax.ShapeDtypeStruct((t, d), F32),
        scratch_shapes=[pltpu.VMEM((tm + 2 * SUBLANE, cw), F32)],
        compiler_params=_params(1),
        name="merge",
    )(xs, mod.table, o, pr["cb"], pr["cc"], pr["cu"], pr["cc"], pr["cu"], pr["cc"], pr["cu"], four,
      pr["gate"], lw["conv_w"].stacked, lw["w_mla_out"].stacked, lw["w_conv_out"].stacked,
      lw["w_four_out"].stacked, lw["w_out"].stacked, lw["g_post_mix"].stacked)


def _pack_rows(v):
    bits = lax.bitcast_convert_type(v.astype(BF16).astype(F32), jnp.uint32)
    rows = []
    for j in range(v.shape[1] // (2 * PACK_W)):
        lo = bits[:, (2 * j) * PACK_W:(2 * j + 1) * PACK_W]
        hi = bits[:, (2 * j + 1) * PACK_W:(2 * j + 2) * PACK_W]
        rows.append(lax.bitcast_convert_type((hi & jnp.uint32(0xFFFF0000)) | (lo >> 16), jnp.int32))
    return rows


def _unpack_rows(rows):
    parts = []
    for r in rows:
        u = lax.bitcast_convert_type(r, jnp.uint32)
        parts.append(lax.bitcast_convert_type(u << 16, F32))
        parts.append(lax.bitcast_convert_type(u & jnp.uint32(0xFFFF0000), F32))
    return jnp.concatenate(parts, axis=1)


def _route_kernel(x_ref, mod_ref, gpre_ref, wrt_ref, brt_ref, triu_ref,
                  tp_ref, sel_ref, rank_ref, cnt_ref, wgt_ref, *, n_experts):
    shift = mod_ref[0, 3:4, :]
    scale = mod_ref[0, 4:5, :]
    t = _rms(x_ref[...], gpre_ref[...]) * (1.0 + scale) + shift
    t_hi = t.astype(BF16)
    t_lo = (t - t_hi.astype(F32)).astype(BF16)
    for j, r in enumerate(_pack_rows(t)):
        tp_ref[j] = r
    tm = t.shape[0]
    hh = _dot_t(wrt_ref[...], t_hi)
    logits = hh[0:n_experts] + hh[n_experts:2 * n_experts] + _dot_t(wrt_ref[0:n_experts, :], t_lo)
    scores = _sigmoid(logits)
    work = scores + brt_ref[...]
    row = lax.broadcasted_iota(jnp.int32, scores.shape, 0)
    wide = lax.broadcasted_iota(jnp.int32, (LANE, tm), 0)
    firsts, picked, hits = [], [], []
    for k in range(TOP_K):
        best = jnp.max(work, axis=0, keepdims=True)
        first = jnp.min(jnp.where(work == best, row, n_experts), axis=0, keepdims=True)
        hit = row == first
        firsts.append(first)
        picked.append(jnp.sum(jnp.where(hit, scores, 0.0), axis=0, keepdims=True))
        hits.append(wide == first + k * n_experts)
        work = jnp.where(hit, -jnp.inf, work)
    total = functools.reduce(jnp.add, picked)
    onehot = functools.reduce(jnp.add, [jnp.where(h, 1.0, 0.0) for h in hits])
    earlier = _dot(onehot.astype(BF16), triu_ref[...])
    col = jnp.broadcast_to(jnp.sum(onehot, axis=1, keepdims=True), (LANE, LANE))
    row_c = lax.broadcasted_iota(jnp.int32, (LANE, LANE), 0)
    before = jnp.zeros((LANE, LANE), F32)
    for s in range(1, TOP_K):
        before = before + jnp.where(row_c >= s * n_experts, pltpu.roll(col, s * n_experts, 0), 0.0)
    ahead = earlier + before[:, 0:1]
    row8 = lax.broadcasted_iota(jnp.int32, (SUBLANE, tm), 0)
    sel = jnp.zeros((SUBLANE, tm), jnp.int32)
    rank = jnp.zeros((SUBLANE, tm), F32)
    wgt_t = jnp.zeros((LANE, tm), F32)
    for k in range(TOP_K):
        sel = jnp.where(row8 == k, firsts[k], sel)
        rank = jnp.where(row8 == k, jnp.sum(jnp.where(hits[k], ahead, 0.0), axis=0, keepdims=True), rank)
        wgt_t = jnp.where(wide == k, picked[k] / total * ROUTED_SCALE, wgt_t)
    sel_ref[0] = sel
    rank_ref[0] = rank.astype(jnp.int32)
    cnt_ref[0] = col.astype(jnp.int32)
    wgt_ref[...] = wgt_t.T


def _route(x1, mod, seq_len, lw, *, tm):
    t, d = x1.shape
    n_tiles = t // tm
    tiles_per_seq = max(seq_len // tm, 1)
    n_experts = lw["b_router_t"].shape[0]
    rows = d // (2 * PACK_W)
    triu = jnp.tri(tm, tm, -1, dtype=BF16).T
    tile3 = lambda i: (i, 0, 0)
    return pl.pallas_call(
        functools.partial(_route_kernel, n_experts=n_experts),
        grid=(n_tiles,),
        in_specs=[
            pl.BlockSpec((tm, d), lambda i: (i, 0)),
            mod.spec(lambda i: i // tiles_per_seq),
            lw["g_pre_ffn"].spec(),
            lw["w_router_t"].spec(),
            lw["b_router_t"].spec(),
            _resident((tm, tm)),
        ],
        out_specs=[
            pl.BlockSpec((rows, tm, PACK_W), lambda i: (0, i, 0)),
            pl.BlockSpec((1, SUBLANE, tm), tile3),
            pl.BlockSpec((1, SUBLANE, tm), tile3),
            pl.BlockSpec((1, LANE, LANE), tile3),
            pl.BlockSpec((tm, LANE), lambda i: (i, 0)),
        ],
        out_shape=[
            jax.ShapeDtypeStruct((rows, t, PACK_W), jnp.int32),
            jax.ShapeDtypeStruct((n_tiles, SUBLANE, tm), jnp.int32),
            jax.ShapeDtypeStruct((n_tiles, SUBLANE, tm), jnp.int32),
            jax.ShapeDtypeStruct((n_tiles, LANE, LANE), jnp.int32),
            jax.ShapeDtypeStruct((t, LANE), F32),
        ],
        compiler_params=_params(1),
        name="moe_route",
    )(x1, mod.table, lw["g_pre_ffn"].stacked, lw["w_router_t"].stacked, lw["b_router_t"].stacked, triu)


def _expert_kernel(te_ref, meta_ref, xs_ref, wg_ref, wu_ref, wd_ref, ys_ref, wg_sc, wu_sc, wd_sc):
    i = pl.program_id(0)
    live = i < meta_ref[0]
    new_expert = (i == 0) | (te_ref[i] != te_ref[jnp.maximum(i - 1, 0)])

    @pl.when(live & new_expert)
    def _():
        wg_sc[...] = wg_ref[0, 0].astype(BF16)
        wu_sc[...] = wu_ref[0, 0].astype(BF16)
        wd_sc[...] = wd_ref[0, 0].astype(BF16)

    @pl.when(live)
    def _():
        x = _unpack_rows([xs_ref[j] for j in range(xs_ref.shape[0])]).astype(BF16)
        gate = _dot(x, wg_sc[...])
        act = (gate * _sigmoid(gate) * _dot(x, wu_sc[...])).astype(BF16)
        for j, r in enumerate(_pack_rows(_dot(act, wd_sc[...]))):
            ys_ref[j] = r


def _experts(xs, tile_expert, meta, w_gate_e, w_up_e, w_down_e, layer, *, tm):
    rows, p, _ = xs.shape
    n_experts, d, f = w_gate_e.shape[1:]

    def slot(i, te, meta):
        return (0, jnp.minimum(i, meta[0] - 1), 0)

    grid_spec = pltpu.PrefetchScalarGridSpec(
        num_scalar_prefetch=2,
        grid=(p // tm,),
        in_specs=[
            pl.BlockSpec((rows, tm, PACK_W), slot),
            pl.BlockSpec((1, 1, d, f), lambda i, te, meta: (layer, te[i], 0, 0)),
            pl.BlockSpec((1, 1, d, f), lambda i, te, meta: (layer, te[i], 0, 0)),
            pl.BlockSpec((1, 1, f, d), lambda i, te, meta: (layer, te[i], 0, 0)),
        ],
        out_specs=pl.BlockSpec((rows, tm, PACK_W), slot),
        scratch_shapes=[pltpu.VMEM((d, f), BF16), pltpu.VMEM((d, f), BF16), pltpu.VMEM((f, d), BF16)],
    )
    return pl.pallas_call(
        _expert_kernel,
        grid_spec=grid_spec,
        out_shape=jax.ShapeDtypeStruct(xs.shape, jnp.int32),
        compiler_params=_params(1),
        name="moe_experts",
    )(tile_expert, meta, xs, w_gate_e, w_up_e, w_down_e)


def _combine_kernel(x_ref, mod_ref, gpre_ref, gpost_ref, wgs_ref, wus_ref, wds_ref, yk_ref, wgt_ref,
                    out_ref):
    shift = mod_ref[0, 3:4, :]
    scale = mod_ref[0, 4:5, :]
    x = x_ref[...]
    t_hi = (_rms(x, gpre_ref[...]) * (1.0 + scale) + shift).astype(BF16)
    gate = _dot(t_hi, wgs_ref[...])
    act = (gate * _sigmoid(gate) * _dot(t_hi, wus_ref[...])).astype(BF16)
    acc = _dot(act, wds_ref[...])
    for k in range(TOP_K):
        y = _unpack_rows([yk_ref[k, j] for j in range(yk_ref.shape[1])])
        acc = acc + wgt_ref[:, k:k + 1] * y
    g2 = mod_ref[0, 5:6, :]
    out_ref[...] = x + g2 * _rms(acc, gpost_ref[...])


def _combine(x1, mod, seq_len, lw, yk, wgt, *, tm):
    t, d = x1.shape
    tiles_per_seq = max(seq_len // tm, 1)
    rows = yk.shape[1]
    row = lambda i: (i, 0)
    return pl.pallas_call(
        _combine_kernel,
        grid=(t // tm,),
        in_specs=[
            pl.BlockSpec((tm, d), row),
            mod.spec(lambda i: i // tiles_per_seq),
            lw["g_pre_ffn"].spec(),
            lw["g_post_ffn"].spec(),
            lw["w_gate_s"].spec(),
            lw["w_up_s"].spec(),
            lw["w_down_s"].spec(),
            pl.BlockSpec((TOP_K, rows, tm, PACK_W), lambda i: (0, 0, i, 0)),
            pl.BlockSpec((tm, LANE), row),
        ],
        out_specs=pl.BlockSpec((tm, d), row),
        out_shape=jax.ShapeDtypeStruct((t, d), F32),
        compiler_params=_params(1),
        name="moe_combine",
    )(x1, mod.table, lw["g_pre_ffn"].stacked, lw["g_post_ffn"].stacked, lw["w_gate_s"].stacked, lw["w_up_s"].stacked,
      lw["w_down_s"].stacked, yk, wgt)


def _sc_mesh():
    return plsc.VectorSubcoreMesh(core_axis_name="core", subcore_axis_name="subcore")


def _sc_scatter_rows(src, idx, n_out):
    n_lists, n = idx.shape
    width = src.shape[1]

    @pl.kernel(out_type=jax.ShapeDtypeStruct((n_out, width), src.dtype), mesh=_sc_mesh(), scratch_types=[])
    def scatter(x_hbm, *refs):
        i_hbms, o_hbm = refs[:n_lists], refs[n_lists]

        def body(x_vmem, *i_vmems):
            for i_vmem in i_vmems:
                pltpu.sync_copy(x_vmem, o_hbm.at[i_vmem.at[0]])

        pltpu.emit_pipeline(
            body,
            grid=(n // SC_WINDOW,),
            in_specs=[pl.BlockSpec((SC_WINDOW, width), lambda i: (i, 0))]
            + [pl.BlockSpec((1, SC_WINDOW), lambda i, r=r: (r, i)) for r in range(n_lists)],
            out_specs=[],
            core_axis_name=("core", "subcore"),
            dimension_semantics=(pltpu.PARALLEL,),
        )(x_hbm, *i_hbms)

    return scatter(src, *([idx] * n_lists))


def _sc_gather_rows(table, idx):
    n = idx.shape[0]
    width = table.shape[1]

    @pl.kernel(out_type=jax.ShapeDtypeStruct((n, width), table.dtype), mesh=_sc_mesh(), scratch_types=[])
    def gather(x_hbm, i_hbm, o_hbm):
        def body(i_vmem, o_vmem):
            pltpu.sync_copy(x_hbm.at[i_vmem.at[0]], o_vmem)

        pltpu.emit_pipeline(
            body,
            grid=(n // SC_WINDOW,),
            in_specs=[pl.BlockSpec((1, SC_WINDOW), lambda i: (0, i))],
            out_specs=[pl.BlockSpec((SC_WINDOW, width), lambda i: (i, 0))],
            core_axis_name=("core", "subcore"),
            dimension_semantics=(pltpu.PARALLEL,),
        )(i_hbm, o_hbm)

    return gather(table, idx.reshape(1, n))


def _moe_sparse(x1, mod, seq_len, lw, w_gate_e, w_up_e, w_down_e, layer, *, tm, tm_e):
    t = x1.shape[0]
    n_experts = lw["b_router_t"].shape[0]
    assert n_experts * TOP_K == LANE
    n_tok_tiles = t // tm
    n_row_tiles = (t * TOP_K) // tm_e + n_experts
    p = n_row_tiles * tm_e
    experts = jnp.arange(n_experts, dtype=jnp.int32)

    tp, sel, rank, cnt, wgt = _route(x1, mod, seq_len, lw, tm=tm)
    rows = tp.shape[0]
    cnt = cnt[:, :, 0].reshape(n_tok_tiles, TOP_K, n_experts).sum(axis=1)
    padded = (cnt.sum(axis=0) + tm_e - 1) // tm_e * tm_e
    group_end = jnp.cumsum(padded)
    base = (group_end - padded)[None, :] + jnp.cumsum(cnt, axis=0) - cnt
    chosen = sel[:, :TOP_K, :, None] == experts
    pos = jnp.sum(jnp.where(chosen, base[:, None, None, :], 0), axis=-1) + rank[:, :TOP_K, :]
    pos = pos.transpose(1, 0, 2).reshape(TOP_K, t)

    n_used = group_end[-1] // tm_e
    tile_start = jnp.arange(n_row_tiles, dtype=jnp.int32) * tm_e
    tile_expert = jnp.sum(tile_start[:, None] >= group_end[None, :], axis=1).astype(jnp.int32)
    tile_expert = jnp.minimum(tile_expert, n_experts - 1)
    tile_expert = jnp.where(tile_start < group_end[-1], tile_expert, tile_expert[n_used - 1])
    meta = jnp.stack([n_used, n_used]).astype(jnp.int32)

    idx = pos[:, None, :] + (jnp.arange(rows, dtype=jnp.int32) * p)[None, :, None]
    xs = _sc_scatter_rows(tp.reshape(rows * t, PACK_W), idx.reshape(TOP_K, rows * t), rows * p)
    after = yield wgt
    if after is not None:
        xs = _after(xs, after)
    ys = _experts(xs.reshape(rows, p, PACK_W), tile_expert, meta, w_gate_e, w_up_e, w_down_e, layer, tm=tm_e)
    yk = _sc_gather_rows(ys.reshape(rows * p, PACK_W), idx.reshape(-1))
    yield ys
    yield _combine(x1, mod, seq_len, lw, yk.reshape(TOP_K, rows, t, PACK_W), wgt, tm=tm)


def _rope_tables(n):
    rows = n // GRID_W
    r, col = jnp.meshgrid(jnp.arange(rows), jnp.arange(GRID_W), indexing="ij")
    r = r.reshape(-1).astype(F32)
    col = col.reshape(-1).astype(F32)
    pairs = QK_ROPE // 4
    inv = ROPE_BASE ** (-jnp.arange(pairs, dtype=F32) / pairs)
    ang = jnp.concatenate([r[:, None] * inv, col[:, None] * inv], axis=-1)
    cos, sin = jnp.cos(ang), jnp.sin(ang)
    pad = HEAD_PAD - QK_NOPE - QK_ROPE
    cos_t = jnp.concatenate([jnp.ones((n, QK_NOPE), F32), cos, cos, jnp.zeros((n, pad), F32)], axis=1)
    sin_t = jnp.concatenate([jnp.zeros((n, QK_NOPE), F32), sin, sin, jnp.zeros((n, pad), F32)], axis=1)
    return cos_t, sin_t


def _identity_tables(n):
    pad = HEAD_PAD - QK_NOPE - QK_ROPE
    cos_t = jnp.concatenate([jnp.ones((n, QK_NOPE + QK_ROPE), F32), jnp.zeros((n, pad), F32)], axis=1)
    return cos_t, jnp.zeros((n, HEAD_PAD), F32)


def _position_dft(n):
    nb = 64 if n % 64 == 0 else 1
    na = n // nb
    m = jnp.arange(n, dtype=jnp.int32)[None, :]
    ang_a = ((jnp.arange(na, dtype=jnp.int32)[:, None] * m) % na).astype(F32) * (2.0 * np.pi / na)
    ang_b = ((jnp.arange(nb, dtype=jnp.int32)[:, None] * m) % n).astype(F32) * (2.0 * np.pi / n)
    ca, sa = jnp.cos(ang_a)[:, None, :], jnp.sin(ang_a)[:, None, :]
    cb, sb = jnp.cos(ang_b)[None, :, :], jnp.sin(ang_b)[None, :, :]
    norm = 1.0 / np.sqrt(n)
    cos = ((ca * cb - sa * sb) * norm).reshape(n, n)
    sin = ((sa * cb + ca * sb) * (-norm)).reshape(n, n)
    return jnp.concatenate([cos, sin], axis=1).astype(BF16)


def _channel_dft(width):
    gc = width // FOURIER_GROUPS
    idx = (jnp.arange(gc, dtype=jnp.int32)[:, None] * jnp.arange(gc, dtype=jnp.int32)[None, :]) % gc
    ang = idx.astype(F32) * (2.0 * np.pi / gc)
    eye = jnp.eye(FOURIER_GROUPS, dtype=F32)
    norm = 1.0 / np.sqrt(gc)
    return jnp.concatenate([jnp.kron(eye, jnp.cos(ang) * norm), jnp.kron(eye, jnp.sin(ang) * norm)],
                           axis=1).astype(BF16)


def _w1_kernel(w_ref, o_ref, *, kv_rank):
    cols = w_ref.shape[2]
    kv_end = kv_rank + QK_ROPE
    half = QK_ROPE // 2
    tail = HEAD_PAD - QK_NOPE - QK_ROPE
    o_kpe = kv_rank
    o_rot = o_kpe + HEAD_PAD
    o_rest = o_rot + HEAD_PAD
    dt = o_ref.dtype
    o_ref[0, 0:kv_rank, :] = w_ref[0, 0:kv_rank, :].astype(dt)
    for base in (o_kpe, o_rot):
        o_ref[0, base:base + QK_NOPE, :] = jnp.zeros((QK_NOPE, cols), dt)
        o_ref[0, base + QK_NOPE + QK_ROPE:base + HEAD_PAD, :] = jnp.zeros((tail, cols), dt)
    o_ref[0, o_kpe + QK_NOPE:o_kpe + QK_NOPE + QK_ROPE, :] = w_ref[0, kv_rank:kv_end, :].astype(dt)
    o_ref[0, o_rot + QK_NOPE:o_rot + QK_NOPE + half, :] = (-w_ref[0, kv_rank + half:kv_end, :]).astype(dt)
    o_ref[0, o_rot + QK_NOPE + half:o_rot + QK_NOPE + QK_ROPE, :] = w_ref[0, kv_rank:kv_rank + half, :].astype(dt)
    o_ref[0, o_rest:, :] = w_ref[0, kv_end:, :].astype(dt)


def _prep_w1(w_in_t, kv_rank):
    n_layers, width, d = w_in_t.shape
    out_w = width - QK_ROPE + 2 * HEAD_PAD
    tc = _tile(d, 256)
    return pl.pallas_call(
        functools.partial(_w1_kernel, kv_rank=kv_rank),
        grid=(n_layers, d // tc),
        in_specs=[pl.BlockSpec((1, width, tc), lambda l, i: (l, 0, i))],
        out_specs=pl.BlockSpec((1, out_w, tc), lambda l, i: (l, 0, i)),
        out_shape=jax.ShapeDtypeStruct((n_layers, out_w, d), BF16),
        compiler_params=_params(2),
        name="prep_w1",
    )(w_in_t)


def _stacked_weights(g_pre_mix, g_post_mix, g_pre_ffn, g_post_ffn, b_gate, g_q, w_uq, g_kv, w_ukv,
                     w_mla_out, conv_w, w_conv_out, w_four_out, w_out, w_router, b_router, w_gate_s,
                     w_up_s, w_down_s):
    n_layers, kv_rank = g_kv.shape
    q_rank = g_q.shape[1]
    qk_dim = QK_NOPE + QK_ROPE
    uq = w_uq.reshape(n_layers, q_rank, N_HEADS, qk_dim) * (qk_dim ** -0.5 * np.log2(np.e))
    zq = jnp.zeros((n_layers, q_rank, N_HEADS, HEAD_PAD - qk_dim), F32)
    wuq = jnp.concatenate([uq, zq], axis=-1).reshape(n_layers, q_rank, N_HEADS * HEAD_PAD).astype(BF16)
    ukv = w_ukv.reshape(n_layers, kv_rank, N_HEADS, QK_NOPE + V_DIM)
    zk = jnp.zeros((n_layers, kv_rank, N_HEADS, HEAD_PAD - QK_NOPE), F32)
    zv = jnp.zeros((n_layers, kv_rank, N_HEADS, HEAD_PAD - V_DIM), F32)
    wuk = jnp.concatenate([ukv[..., :QK_NOPE], zk], axis=-1).reshape(n_layers, kv_rank, -1).astype(BF16)
    wuv = jnp.concatenate([ukv[..., QK_NOPE:], zv], axis=-1).reshape(n_layers, kv_rank, -1).astype(BF16)
    wr_hi = w_router.astype(BF16)
    wr_lo = (w_router - wr_hi.astype(F32)).astype(BF16)
    wr_t = jnp.concatenate([jnp.swapaxes(wr_hi, 1, 2), jnp.swapaxes(wr_lo, 1, 2)], axis=1)
    return {
        "g_pre_mix": g_pre_mix[:, None], "g_post_mix": g_post_mix[:, None],
        "g_pre_ffn": g_pre_ffn[:, None], "g_post_ffn": g_post_ffn[:, None],
        "b_gate": b_gate[:, None], "g_q": g_q[:, None], "g_kv": g_kv[:, None],
        "wuq": wuq, "wuk": wuk, "wuv": wuv, "conv_w": conv_w,
        "w_mla_out": w_mla_out.astype(BF16), "w_conv_out": w_conv_out.astype(BF16),
        "w_four_out": w_four_out.astype(BF16), "w_out": w_out.astype(BF16),
        "w_router_t": wr_t, "b_router_t": b_router[:, :, None],
        "w_gate_s": w_gate_s.astype(BF16), "w_up_s": w_up_s.astype(BF16), "w_down_s": w_down_s.astype(BF16),
    }


def _tile(n, pref):
    return pref if n % pref == 0 else n


def kernel(x, c, ctx, c_ctx, w_ada, b_ada, g_pre_mix, g_post_mix, g_pre_ffn, g_post_ffn, w_in, b_gate,
           g_q, w_uq, g_kv, w_ukv, w_mla_out, conv_w, w_conv_out, w_four_out, w_out, w_router, b_router,
           w_gate_e, w_up_e, w_down_e, w_gate_s, w_up_s, w_down_s):
    batch, seq, d = x.shape
    n_ctx = ctx.shape[1]
    n_layers = w_in.shape[0]
    xs = x.reshape(batch * seq, d)
    cs = ctx.reshape(batch * n_ctx, d)

    mod_rows = -(-(batch + 1) // (2 * SUBLANE)) * (2 * SUBLANE)
    c_all = jnp.concatenate([c, c_ctx[None], jnp.zeros((mod_rows - batch - 1, d), F32)], axis=0)
    ada = _ada(c_all, w_ada, b_ada).reshape(n_layers, mod_rows, 6, d)

    tm_x = _tile(seq, 512)
    tm_x_wide = _tile(seq, 2 * MERGE_ROWS)
    tq_x = _tile(seq, 2 * ATTN_ROWS)
    tm_c = _tile(n_ctx, 256)
    tm_c_mid = _tile(batch * n_ctx, 512)
    tm_c_wide = _tile(batch * n_ctx, 1024)
    te_x, te_c = 1024, 512

    tab_x = _rope_tables(seq)
    tab_c = _identity_tables(tm_c_mid)
    cs_x = _position_dft(seq)
    cs_c = _position_dft(n_ctx)
    dc = _channel_dft(w_four_out.shape[1])
    w1 = _prep_w1(jnp.swapaxes(w_in, 1, 2), g_kv.shape[1])
    stacked = _stacked_weights(g_pre_mix, g_post_mix, g_pre_ffn, g_post_ffn, b_gate, g_q, w_uq, g_kv, w_ukv,
                               w_mla_out, conv_w, w_conv_out, w_four_out, w_out, w_router, b_router,
                               w_gate_s, w_up_s, w_down_s)
    v_one = jnp.tile((jnp.arange(HEAD_PAD) == V_DIM).astype(F32), N_HEADS)[None]

    for l in range(n_layers):
        last = l == n_layers - 1
        lw = {name: _Layer(arr, l) for name, arr in stacked.items()}
        lw.update(w1=w1, layer=l, dc=dc, v_one=v_one)
        mod_x = _Mod(ada, l, 0, batch)
        mod_c = _Mod(ada, l, batch, 1)

        pc = _inproj(cs, mod_c, n_ctx, lw, tab_c, kv_only=last, tm=tm_c_mid)
        px = _inproj(xs, mod_x, seq, lw, tab_x, kv_only=False, tm=tm_x)
        o_x = _attention(px["q"], [(pc["k"], pc["v"], n_ctx), (px["k"], px["v"], seq)], batch, seq, tq=tq_x)
        f_x = _fourier(px["ab"], cs_x, batch, seq, tn=seq)
        x1 = _merge(xs, mod_x, seq, px, o_x, f_x, lw, tm=tm_x_wide)
        moe_x = _moe_sparse(x1, mod_x, seq, lw, w_gate_e, w_up_e, w_down_e, l, tm=tm_x_wide, tm_e=te_x)
        if last:
            xs = list(moe_x)[-1]
        else:
            routed_x = next(moe_x)
            q_c = _after(pc["q"], routed_x)
            o_c = _attention(q_c, [(pc["k"], pc["v"], n_ctx)], batch, n_ctx, tq=tm_c)
            f_c = _fourier(_after(pc["ab"], routed_x), cs_c, batch, n_ctx, tn=tm_c)
            experts_x = moe_x.send((o_c, f_c))
            c1 = _merge(cs, mod_c, n_ctx, pc, _after(o_c, experts_x), f_c, lw, tm=tm_c_wide)
            moe_c = _moe_sparse(c1, mod_c, batch * n_ctx, lw, w_gate_e, w_up_e, w_down_e, l, tm=tm_c_wide,
                                tm_e=te_c)
            next(moe_c)
            moe_c.send(None)
            xs = next(moe_x)
            cs = next(moe_c)
    return xs.reshape(batch, seq, d)
```

```python
import functools

import numpy as np
import jax
import jax.numpy as jnp
from jax import lax
from jax.experimental import pallas as pl
from jax.experimental.pallas import tpu as pltpu
from jax.experimental.pallas import tpu_sc as plsc

N_HEADS = 8
QK_NOPE = 64
QK_ROPE = 32
V_DIM = 64
GRID_W = 64
ROPE_BASE = 10000.0
FOURIER_GROUPS = 4
TOP_K = 4
ROUTED_SCALE = 2.5
N_BRANCHES = 3
EPS = 1e-6

LANE = 128
SUBLANE = 8
HEAD_PAD = LANE
VMEM_LIMIT = 56 * 1024 * 1024
PACK_W = 256
SC_WINDOW = 128
ATTN_ROWS = 512
MERGE_ROWS = 512

F32 = jnp.float32
BF16 = jnp.bfloat16


def _rms(x, g):
    return x * lax.rsqrt(jnp.mean(x * x, axis=-1, keepdims=True) + EPS) * g


def _sigmoid(x):
    return 1.0 / (1.0 + jnp.exp(-x))


def _dot(a, b):
    return jnp.dot(a, b, preferred_element_type=F32)


def _dot_t(a, b_t):
    return lax.dot_general(a, b_t, (((1,), (1,)), ((), ())), preferred_element_type=F32)


def _skewed(stages, n_chunks, chunk):
    live = [None] * n_chunks
    for step in range(n_chunks + len(stages) - 1):
        for c in range(n_chunks):
            s = step - c
            if 0 <= s < len(stages):
                live[c] = stages[s](c * chunk) if s == 0 else stages[s](c * chunk, live[c])


def _after(value, dependency):
    return lax.optimization_barrier((value, dependency))[0]


def _resident(shape):
    nd = len(shape)
    return pl.BlockSpec(shape, lambda *_: (0,) * nd, pipeline_mode=pl.Buffered(1))


class _Layer:
    def __init__(self, stacked, index):
        self.stacked, self.index = stacked, index
        self.shape = stacked.shape[1:]

    def spec(self):
        index = (self.index,) + (0,) * len(self.shape)
        return pl.BlockSpec((None,) + self.shape, lambda *_: index, pipeline_mode=pl.Buffered(1))


class _Mod:
    def __init__(self, table, layer, row0, n):
        self.table, self.layer, self.row0, self.n = table, layer, row0, n

    def spec(self, seq_of_step):
        layer, row0, n = self.layer, self.row0, self.n
        return pl.BlockSpec((None, 1) + self.table.shape[2:],
                            lambda i: (layer, row0 + seq_of_step(i) % n, 0, 0))


def _params(n_grid):
    return pltpu.CompilerParams(dimension_semantics=("arbitrary",) * n_grid,
                                vmem_limit_bytes=VMEM_LIMIT)


def _ada_kernel(c_ref, w_ref, b_ref, o_ref):
    c = c_ref[...]
    a = (c * _sigmoid(c)).astype(BF16)
    o_ref[0] = _dot(a, w_ref[0].astype(BF16)) + b_ref[0]


def _ada(c_all, w_ada, b_ada):
    n_layers, d, n_out = w_ada.shape
    rows = c_all.shape[0]
    tn = _tile(n_out, 1536)
    return pl.pallas_call(
        _ada_kernel,
        grid=(n_layers, n_out // tn),
        in_specs=[
            pl.BlockSpec((rows, d), lambda l, j: (0, 0)),
            pl.BlockSpec((1, d, tn), lambda l, j: (l, 0, j)),
            pl.BlockSpec((1, 1, tn), lambda l, j: (l, 0, j)),
        ],
        out_specs=pl.BlockSpec((1, rows, tn), lambda l, j: (l, 0, j)),
        out_shape=jax.ShapeDtypeStruct((n_layers, rows, n_out), F32),
        compiler_params=_params(2),
        name="ada",
    )(c_all, w_ada, b_ada.reshape(n_layers, 1, n_out))


def _inproj_kernel(*refs, kv_only, kv_rank, q_rank, conv_w, four_w, d_model):
    if kv_only:
        (x_ref, mod_ref, gpre_ref, w1_ref, gkv_ref, wuk_ref, wuv_ref, vone_ref, cos_ref, sin_ref,
         k_ref, v_ref) = refs
    else:
        (x_ref, mod_ref, gpre_ref, w1_ref, gkv_ref, wuk_ref, wuv_ref, vone_ref, cos_ref, sin_ref,
         bg_ref, gq_ref, wuq_ref, dc_ref,
         k_ref, v_ref, q_ref, cb_ref, cc_ref, cu_ref, ab_ref, gate_ref) = refs

    x = x_ref[...]
    shift = mod_ref[0, 0:1, :]
    scale = mod_ref[0, 1:2, :]
    h = (_rms(x, gpre_ref[...]) * (1.0 + scale) + shift).astype(BF16)
    cos = cos_ref[...]
    sin = sin_ref[...]

    o_kpe = kv_rank
    o_rot = o_kpe + HEAD_PAD
    o_q = o_rot + HEAD_PAD
    p = _dot_t(h, w1_ref[0, 0:o_q, :])
    ckv = _rms(p[:, 0:kv_rank], gkv_ref[...]).astype(BF16)
    kpe = p[:, o_kpe:o_rot] * cos + p[:, o_rot:o_q] * sin
    k = _dot(ckv, wuk_ref[...]) + jnp.concatenate([kpe] * N_HEADS, axis=1)
    k_ref[...] = k.astype(k_ref.dtype)
    v_ref[...] = (_dot(ckv, wuv_ref[...]) + vone_ref[...]).astype(v_ref.dtype)
    if kv_only:
        return

    o_cb = o_q + q_rank
    cq = _rms(_dot_t(h, w1_ref[0, o_q:o_cb, :]), gq_ref[...]).astype(BF16)
    half = QK_ROPE // 2
    lane = lax.broadcasted_iota(jnp.int32, (1, HEAD_PAD), 1)
    first = (lane >= QK_NOPE) & (lane < QK_NOPE + half)
    cos_h = jnp.concatenate([cos] * N_HEADS, axis=1)
    sin_h = jnp.concatenate([jnp.where(first, -sin, sin)] * N_HEADS, axis=1)
    first_h = jnp.concatenate([first] * N_HEADS, axis=1)
    lin = _dot(cq, wuq_ref[...])
    width = lin.shape[1]
    partner = jnp.where(first_h, pltpu.roll(lin, width - half, 1), pltpu.roll(lin, half, 1))
    q_ref[...] = (lin * cos_h + partner * sin_h).astype(q_ref.dtype)

    o_cc = o_cb + conv_w
    o_cu = o_cc + conv_w
    o_four = o_cu + conv_w
    cb_ref[...] = _dot_t(h, w1_ref[0, o_cb:o_cc, :]).astype(cb_ref.dtype)
    cc_ref[...] = _dot_t(h, w1_ref[0, o_cc:o_cu, :]).astype(cc_ref.dtype)
    cu_ref[...] = _dot_t(h, w1_ref[0, o_cu:o_four, :]).astype(cu_ref.dtype)

    o_gate = o_four + four_w
    uf = _dot_t(h, w1_ref[0, o_four:o_gate, :]).astype(BF16)
    ab_ref[...] = _dot(uf, dc_ref[...]).astype(ab_ref.dtype)

    for j in range(N_BRANCHES):
        lo = o_gate + j * d_model
        z = _dot_t(h, w1_ref[0, lo:lo + d_model, :]) + bg_ref[:, j * d_model:(j + 1) * d_model]
        gate_ref[:, j * d_model:(j + 1) * d_model] = _sigmoid(z).astype(gate_ref.dtype)


def _inproj(xs, mod, seq_len, lw, tables, *, kv_only, tm):
    t, d = xs.shape
    cos_t, sin_t = tables
    table_tiles = cos_t.shape[0] // tm
    kv_rank = lw["g_kv"].shape[1]
    q_rank = lw["g_q"].shape[1]
    conv_w = lw["conv_w"].shape[1]
    four_w = lw["dc"].shape[0]
    n_k = N_HEADS * HEAD_PAD

    def row(i):
        return (i, 0)

    def tab_map(i):
        return (i % table_tiles, 0)

    w1, layer = lw["w1"], lw["layer"]
    w1_rows = kv_rank + 2 * HEAD_PAD if kv_only else w1.shape[1]
    in_specs = [
        pl.BlockSpec((tm, d), row),
        mod.spec(lambda i: i * tm // seq_len),
        lw["g_pre_mix"].spec(),
        pl.BlockSpec((1, w1_rows, d), lambda i: (layer, 0, 0), pipeline_mode=pl.Buffered(1)),
        lw["g_kv"].spec(),
        lw["wuk"].spec(),
        lw["wuv"].spec(),
        _resident(lw["v_one"].shape),
        pl.BlockSpec((tm, HEAD_PAD), tab_map),
        pl.BlockSpec((tm, HEAD_PAD), tab_map),
    ]
    args = [xs, mod.table, lw["g_pre_mix"].stacked, w1, lw["g_kv"].stacked, lw["wuk"].stacked, lw["wuv"].stacked,
            lw["v_one"], cos_t, sin_t]
    out_shape = [jax.ShapeDtypeStruct((t, n_k), BF16), jax.ShapeDtypeStruct((t, n_k), BF16)]
    out_specs = [pl.BlockSpec((tm, n_k), row), pl.BlockSpec((tm, n_k), row)]
    if not kv_only:
        in_specs += [
            lw["b_gate"].spec(),
            lw["g_q"].spec(),
            lw["wuq"].spec(),
            _resident(lw["dc"].shape),
        ]
        args += [lw["b_gate"].stacked, lw["g_q"].stacked, lw["wuq"].stacked, lw["dc"]]
        widths = [n_k, conv_w, conv_w, conv_w, 2 * four_w, N_BRANCHES * d]
        out_shape += [jax.ShapeDtypeStruct((t, w), BF16) for w in widths]
        out_specs += [pl.BlockSpec((tm, w), row) for w in widths]
    outs = pl.pallas_call(
        functools.partial(_inproj_kernel, kv_only=kv_only, kv_rank=kv_rank, q_rank=q_rank,
                          conv_w=conv_w, four_w=four_w, d_model=d),
        grid=(t // tm,),
        in_specs=in_specs,
        out_specs=out_specs,
        out_shape=out_shape,
        compiler_params=_params(1),
        name="inproj_kv" if kv_only else "inproj",
    )(*args)
    names = ["k", "v", "q", "cb", "cc", "cu", "ab", "gate"]
    return dict(zip(names, outs))


def _attn_kernel(*refs, n_seg):
    q_ref = refs[0]
    o_ref = refs[-1]
    chunk = min(q_ref.shape[0], ATTN_ROWS)
    for c in range(q_ref.shape[0] // chunk):
        rows = slice(c * chunk, (c + 1) * chunk)
        outs = []
        for hh in range(N_HEADS):
            head = slice(hh * HEAD_PAD, (hh + 1) * HEAD_PAD)
            qh = q_ref[rows, head]
            s = [_dot_t(qh, refs[1 + 2 * i][:, head]) for i in range(n_seg)]
            m = functools.reduce(jnp.maximum, [jnp.max(si, axis=-1, keepdims=True) for si in s])
            acc = functools.reduce(jnp.add, [
                _dot(jnp.exp2((s[i] - m).astype(BF16)), refs[2 + 2 * i][:, head]) for i in range(n_seg)])
            outs.append(acc[:, 0:V_DIM] / acc[:, V_DIM:V_DIM + 1])
        o_ref[rows, :] = jnp.concatenate(outs, axis=1).astype(o_ref.dtype)


def _attention(q, segs, batch, seq_q, *, tq):
    t = q.shape[0]
    qt = seq_q // tq
    n_k = N_HEADS * HEAD_PAD
    in_specs = [pl.BlockSpec((tq, n_k), lambda b, j: (b * qt + j, 0))]
    args = [q]
    for k, v, m in segs:
        in_specs.append(pl.BlockSpec((m, n_k), lambda b, j: (b, 0)))
        in_specs.append(pl.BlockSpec((m, n_k), lambda b, j: (b, 0)))
        args += [k, v]
    return pl.pallas_call(
        functools.partial(_attn_kernel, n_seg=len(segs)),
        grid=(batch, qt),
        in_specs=in_specs,
        out_specs=pl.BlockSpec((tq, N_HEADS * V_DIM), lambda b, j: (b * qt + j, 0)),
        out_shape=jax.ShapeDtypeStruct((t, N_HEADS * V_DIM), BF16),
        compiler_params=_params(2),
        name="attention",
    )(*args)


def _four_kernel(cs_ref, ab_ref, o_ref, *, n, fw):
    o = _dot(cs_ref[:, 0:n], ab_ref[:, 0:fw]) + _dot(cs_ref[:, n:2 * n], ab_ref[:, fw:2 * fw])
    o_ref[...] = o.astype(o_ref.dtype)


def _fourier(ab, cs, batch, seq_len, *, tn):
    t, fw2 = ab.shape
    fw = fw2 // 2
    nt = seq_len // tn
    cs_spec = (_resident(cs.shape) if nt == 1
               else pl.BlockSpec((tn, 2 * seq_len), lambda b, j: (j, 0)))
    return pl.pallas_call(
        functools.partial(_four_kernel, n=seq_len, fw=fw),
        grid=(batch, nt),
        in_specs=[
            cs_spec,
            pl.BlockSpec((seq_len, fw2), lambda b, j: (b, 0)),
        ],
        out_specs=pl.BlockSpec((tn, fw), lambda b, j: (b * nt + j, 0)),
        out_shape=jax.ShapeDtypeStruct((t, fw), BF16),
        compiler_params=_params(2),
        name="fourier",
    )(cs, ab)


def _merge_kernel(x_ref, mod_ref, o_ref, cb_ref, cc_ref, cu_ref, ccp_ref, cup_ref, ccn_ref,
                  cun_ref, f_ref, gate_ref, convw_ref, wmo_ref, wco_ref, wfo_ref, wout_ref,
                  gpost_ref, gffn_ref, wrt_ref, brt_ref, triu_ref,
                  out_ref, tp_ref, sel_ref, rank_ref, cnt_ref, wgt_ref, pad_ref,
                  *, seq_len, tm, d_model, n_experts):
    i = pl.program_id(0)
    has_prev = ((i * tm) % seq_len != 0).astype(F32)
    has_next = (((i + 1) * tm) % seq_len != 0).astype(F32)
    lo = SUBLANE
    pad_ref[0:lo, :] = ccp_ref[...].astype(F32) * cup_ref[...].astype(F32) * has_prev
    pad_ref[lo:lo + tm, :] = cc_ref[...].astype(F32) * cu_ref[...].astype(F32)
    pad_ref[lo + tm:2 * lo + tm, :] = ccn_ref[...].astype(F32) * cun_ref[...].astype(F32) * has_next
    d = d_model
    g1 = mod_ref[0, 2:3, :]
    chunk = min(tm, MERGE_ROWS)

    def conv_stage(r0):
        before = pad_ref[lo - 1 + r0:lo - 1 + r0 + chunk, :]
        after = pad_ref[lo + 1 + r0:lo + 1 + r0 + chunk, :]
        if tm > seq_len:
            at = (lax.broadcasted_iota(jnp.int32, (chunk, 1), 0) + r0) % seq_len
            before = jnp.where(at == 0, 0.0, before)
            after = jnp.where(at == seq_len - 1, 0.0, after)
        conv = (before * convw_ref[0:1, :] + pad_ref[lo + r0:lo + r0 + chunk, :] * convw_ref[1:2, :]
                + after * convw_ref[2:3, :])
        return (cb_ref[r0:r0 + chunk, :].astype(F32) * conv).astype(BF16)

    def branch_stage(r0, conv_in):
        return (_dot(o_ref[r0:r0 + chunk, :], wmo_ref[...]), _dot(conv_in, wco_ref[...]),
                _dot(f_ref[r0:r0 + chunk, :], wfo_ref[...]))

    def gate_stage(r0, ys):
        rows = slice(r0, r0 + chunk)
        return (gate_ref[rows, 0:d].astype(F32) * ys[0] + gate_ref[rows, d:2 * d].astype(F32) * ys[1]
                + gate_ref[rows, 2 * d:3 * d].astype(F32) * ys[2]).astype(BF16)

    def out_stage(r0, merged):
        return _dot(merged, wout_ref[...])

    def tail_stage(r0, y):
        rows = slice(r0, r0 + chunk)
        out_ref[rows, :] = x_ref[rows, :] + g1 * _rms(y, gpost_ref[...])

    _skewed((conv_stage, branch_stage, gate_stage, out_stage, tail_stage), tm // chunk, chunk)
    _route_tile(out_ref[...], mod_ref, gffn_ref, wrt_ref, brt_ref, triu_ref,
                tp_ref, sel_ref, rank_ref, cnt_ref, wgt_ref, n_experts)


def _merge(xs, mod, seq_len, pr, o, four, lw, *, tm):
    t, d = xs.shape
    assert tm % seq_len == 0 or seq_len % tm == 0
    cw = lw["conv_w"].shape[1]
    fw = four.shape[1]
    hb = tm // SUBLANE
    last_hb = t // SUBLANE - 1

    def row(i):
        return (i, 0)

    def prev_map(i):
        return (jnp.maximum(i * hb - 1, 0), 0)

    def next_map(i):
        return (jnp.minimum((i + 1) * hb, last_hb), 0)

    in_specs = [
        pl.BlockSpec((tm, d), row),
        mod.spec(lambda i: i * tm // seq_len),
        pl.BlockSpec((tm, o.shape[1]), row),
        pl.BlockSpec((tm, cw), row),
        pl.BlockSpec((tm, cw), row),
        pl.BlockSpec((tm, cw), row),
        pl.BlockSpec((SUBLANE, cw), prev_map),
        pl.BlockSpec((SUBLANE, cw), prev_map),
        pl.BlockSpec((SUBLANE, cw), next_map),
        pl.BlockSpec((SUBLANE, cw), next_map),
        pl.BlockSpec((tm, fw), row),
        pl.BlockSpec((tm, N_BRANCHES * d), row),
        lw["conv_w"].spec(),
        lw["w_mla_out"].spec(),
        lw["w_conv_out"].spec(),
        lw["w_four_out"].spec(),
        lw["w_out"].spec(),
        lw["g_post_mix"].spec(),
        lw["g_pre_ffn"].spec(),
        lw["w_router_t"].spec(),
        lw["b_router_t"].spec(),
        _resident((tm, tm)),
    ]
    n_tiles = t // tm
    n_experts = lw["b_router_t"].shape[0]
    rows = d // (2 * PACK_W)
    triu = jnp.tri(tm, tm, -1, dtype=BF16).T
    tile3 = lambda i: (i, 0, 0)
    outs = pl.pallas_call(
        functools.partial(_merge_kernel, seq_len=seq_len, tm=tm, d_model=d, n_experts=n_experts),
        grid=(n_tiles,),
        in_specs=in_specs,
        out_specs=[
            pl.BlockSpec((tm, d), row),
            pl.BlockSpec((rows, tm, PACK_W), lambda i: (0, i, 0)),
            pl.BlockSpec((1, SUBLANE, tm), tile3),
            pl.BlockSpec((1, SUBLANE, tm), tile3),
            pl.BlockSpec((1, LANE, LANE), tile3),
            pl.BlockSpec((tm, LANE), row),
        ],
        out_shape=[
            jax.ShapeDtypeStruct((t, d), F32),
            jax.ShapeDtypeStruct((rows, t, PACK_W), jnp.int32),
            jax.ShapeDtypeStruct((n_tiles, SUBLANE, tm), jnp.int32),
            jax.ShapeDtypeStruct((n_tiles, SUBLANE, tm), jnp.int32),
            jax.ShapeDtypeStruct((n_tiles, LANE, LANE), jnp.int32),
            jax.ShapeDtypeStruct((t, LANE), F32),
        ],
        scratch_shapes=[pltpu.VMEM((tm + 2 * SUBLANE, cw), F32)],
        compiler_params=_params(1),
        name="merge",
    )(xs, mod.table, o, pr["cb"], pr["cc"], pr["cu"], pr["cc"], pr["cu"], pr["cc"], pr["cu"], four,
      pr["gate"], lw["conv_w"].stacked, lw["w_mla_out"].stacked, lw["w_conv_out"].stacked,
      lw["w_four_out"].stacked, lw["w_out"].stacked, lw["g_post_mix"].stacked,
      lw["g_pre_ffn"].stacked, lw["w_router_t"].stacked, lw["b_router_t"].stacked, triu)
    return outs[0], outs[1:]


def _pack_rows(v):
    bits = lax.bitcast_convert_type(v.astype(BF16).astype(F32), jnp.uint32)
    rows = []
    for j in range(v.shape[1] // (2 * PACK_W)):
        lo = bits[:, (2 * j) * PACK_W:(2 * j + 1) * PACK_W]
        hi = bits[:, (2 * j + 1) * PACK_W:(2 * j + 2) * PACK_W]
        rows.append(lax.bitcast_convert_type((hi & jnp.uint32(0xFFFF0000)) | (lo >> 16), jnp.int32))
    return rows


def _unpack_rows(rows):
    parts = []
    for r in rows:
        u = lax.bitcast_convert_type(r, jnp.uint32)
        parts.append(lax.bitcast_convert_type(u << 16, F32))
        parts.append(lax.bitcast_convert_type(u & jnp.uint32(0xFFFF0000), F32))
    return jnp.concatenate(parts, axis=1)


def _route_tile(x, mod_ref, gpre_ref, wrt_ref, brt_ref, triu_ref,
                tp_ref, sel_ref, rank_ref, cnt_ref, wgt_ref, n_experts):
    shift = mod_ref[0, 3:4, :]
    scale = mod_ref[0, 4:5, :]
    t = _rms(x, gpre_ref[...]) * (1.0 + scale) + shift
    t_hi = t.astype(BF16)
    t_lo = (t - t_hi.astype(F32)).astype(BF16)
    for j, r in enumerate(_pack_rows(t)):
        tp_ref[j] = r
    tm = t.shape[0]
    hh = _dot_t(wrt_ref[...], t_hi)
    logits = hh[0:n_experts] + hh[n_experts:2 * n_experts] + _dot_t(wrt_ref[0:n_experts, :], t_lo)
    scores = _sigmoid(logits)
    work = scores + brt_ref[...]
    row = lax.broadcasted_iota(jnp.int32, scores.shape, 0)
    wide = lax.broadcasted_iota(jnp.int32, (LANE, tm), 0)
    firsts, picked, hits = [], [], []
    for k in range(TOP_K):
        best = jnp.max(work, axis=0, keepdims=True)
        first = jnp.min(jnp.where(work == best, row, n_experts), axis=0, keepdims=True)
        hit = row == first
        firsts.append(first)
        picked.append(jnp.sum(jnp.where(hit, scores, 0.0), axis=0, keepdims=True))
        hits.append(wide == first + k * n_experts)
        work = jnp.where(hit, -jnp.inf, work)
    total = functools.reduce(jnp.add, picked)
    onehot = functools.reduce(jnp.add, [jnp.where(h, 1.0, 0.0) for h in hits])
    earlier = _dot(onehot.astype(BF16), triu_ref[...])
    col = jnp.broadcast_to(jnp.sum(onehot, axis=1, keepdims=True), (LANE, LANE))
    row_c = lax.broadcasted_iota(jnp.int32, (LANE, LANE), 0)
    before = jnp.zeros((LANE, LANE), F32)
    for s in range(1, TOP_K):
        before = before + jnp.where(row_c >= s * n_experts, pltpu.roll(col, s * n_experts, 0), 0.0)
    ahead = earlier + before[:, 0:1]
    row8 = lax.broadcasted_iota(jnp.int32, (SUBLANE, tm), 0)
    sel = jnp.zeros((SUBLANE, tm), jnp.int32)
    rank = jnp.zeros((SUBLANE, tm), F32)
    wgt_t = jnp.zeros((LANE, tm), F32)
    for k in range(TOP_K):
        sel = jnp.where(row8 == k, firsts[k], sel)
        rank = jnp.where(row8 == k, jnp.sum(jnp.where(hits[k], ahead, 0.0), axis=0, keepdims=True), rank)
        wgt_t = jnp.where(wide == k, picked[k] / total * ROUTED_SCALE, wgt_t)
    sel_ref[0] = sel
    rank_ref[0] = rank.astype(jnp.int32)
    cnt_ref[0] = col.astype(jnp.int32)
    wgt_ref[...] = wgt_t.T


def _expert_kernel(te_ref, meta_ref, xs_ref, wg_ref, wu_ref, wd_ref, ys_ref, wg_sc, wu_sc, wd_sc):
    i = pl.program_id(0)
    live = i < meta_ref[0]
    new_expert = (i == 0) | (te_ref[i] != te_ref[jnp.maximum(i - 1, 0)])

    @pl.when(live & new_expert)
    def _():
        wg_sc[...] = wg_ref[0, 0].astype(BF16)
        wu_sc[...] = wu_ref[0, 0].astype(BF16)
        wd_sc[...] = wd_ref[0, 0].astype(BF16)

    @pl.when(live)
    def _():
        x = _unpack_rows([xs_ref[j] for j in range(xs_ref.shape[0])]).astype(BF16)
        gate = _dot(x, wg_sc[...])
        act = (gate * _sigmoid(gate) * _dot(x, wu_sc[...])).astype(BF16)
        for j, r in enumerate(_pack_rows(_dot(act, wd_sc[...]))):
            ys_ref[j] = r


def _experts(xs, tile_expert, meta, w_gate_e, w_up_e, w_down_e, layer, *, tm):
    rows, p, _ = xs.shape
    n_experts, d, f = w_gate_e.shape[1:]

    def slot(i, te, meta):
        return (0, jnp.minimum(i, meta[0] - 1), 0)

    grid_spec = pltpu.PrefetchScalarGridSpec(
        num_scalar_prefetch=2,
        grid=(p // tm,),
        in_specs=[
            pl.BlockSpec((rows, tm, PACK_W), slot),
            pl.BlockSpec((1, 1, d, f), lambda i, te, meta: (layer, te[i], 0, 0)),
            pl.BlockSpec((1, 1, d, f), lambda i, te, meta: (layer, te[i], 0, 0)),
            pl.BlockSpec((1, 1, f, d), lambda i, te, meta: (layer, te[i], 0, 0)),
        ],
        out_specs=pl.BlockSpec((rows, tm, PACK_W), slot),
        scratch_shapes=[pltpu.VMEM((d, f), BF16), pltpu.VMEM((d, f), BF16), pltpu.VMEM((f, d), BF16)],
    )
    return pl.pallas_call(
        _expert_kernel,
        grid_spec=grid_spec,
        out_shape=jax.ShapeDtypeStruct(xs.shape, jnp.int32),
        compiler_params=_params(1),
        name="moe_experts",
    )(tile_expert, meta, xs, w_gate_e, w_up_e, w_down_e)


def _combine_kernel(x_ref, mod_ref, gpre_ref, gpost_ref, wgs_ref, wus_ref, wds_ref, yk_ref, wgt_ref,
                    out_ref):
    shift = mod_ref[0, 3:4, :]
    scale = mod_ref[0, 4:5, :]
    x = x_ref[...]
    t_hi = (_rms(x, gpre_ref[...]) * (1.0 + scale) + shift).astype(BF16)
    gate = _dot(t_hi, wgs_ref[...])
    act = (gate * _sigmoid(gate) * _dot(t_hi, wus_ref[...])).astype(BF16)
    acc = _dot(act, wds_ref[...])
    for k in range(TOP_K):
        y = _unpack_rows([yk_ref[k, j] for j in range(yk_ref.shape[1])])
        acc = acc + wgt_ref[:, k:k + 1] * y
    g2 = mod_ref[0, 5:6, :]
    out_ref[...] = x + g2 * _rms(acc, gpost_ref[...])


def _combine(x1, mod, seq_len, lw, yk, wgt, *, tm):
    t, d = x1.shape
    tiles_per_seq = max(seq_len // tm, 1)
    rows = yk.shape[1]
    row = lambda i: (i, 0)
    return pl.pallas_call(
        _combine_kernel,
        grid=(t // tm,),
        in_specs=[
            pl.BlockSpec((tm, d), row),
            mod.spec(lambda i: i // tiles_per_seq),
            lw["g_pre_ffn"].spec(),
            lw["g_post_ffn"].spec(),
            lw["w_gate_s"].spec(),
            lw["w_up_s"].spec(),
            lw["w_down_s"].spec(),
            pl.BlockSpec((TOP_K, rows, tm, PACK_W), lambda i: (0, 0, i, 0)),
            pl.BlockSpec((tm, LANE), row),
        ],
        out_specs=pl.BlockSpec((tm, d), row),
        out_shape=jax.ShapeDtypeStruct((t, d), F32),
        compiler_params=_params(1),
        name="moe_combine",
    )(x1, mod.table, lw["g_pre_ffn"].stacked, lw["g_post_ffn"].stacked, lw["w_gate_s"].stacked, lw["w_up_s"].stacked,
      lw["w_down_s"].stacked, yk, wgt)


def _sc_mesh():
    return plsc.VectorSubcoreMesh(core_axis_name="core", subcore_axis_name="subcore")


def _sc_scatter_rows(src, idx, n_out):
    n_lists, n = idx.shape
    width = src.shape[1]

    @pl.kernel(out_type=jax.ShapeDtypeStruct((n_out, width), src.dtype), mesh=_sc_mesh(), scratch_types=[])
    def scatter(x_hbm, *refs):
        i_hbms, o_hbm = refs[:n_lists], refs[n_lists]

        def body(x_vmem, *i_vmems):
            for i_vmem in i_vmems:
                pltpu.sync_copy(x_vmem, o_hbm.at[i_vmem.at[0]])

        pltpu.emit_pipeline(
            body,
            grid=(n // SC_WINDOW,),
            in_specs=[pl.BlockSpec((SC_WINDOW, width), lambda i: (i, 0))]
            + [pl.BlockSpec((1, SC_WINDOW), lambda i, r=r: (r, i)) for r in range(n_lists)],
            out_specs=[],
            core_axis_name=("core", "subcore"),
            dimension_semantics=(pltpu.PARALLEL,),
        )(x_hbm, *i_hbms)

    return scatter(src, *([idx] * n_lists))


def _sc_gather_rows(table, idx):
    n = idx.shape[0]
    width = table.shape[1]

    @pl.kernel(out_type=jax.ShapeDtypeStruct((n, width), table.dtype), mesh=_sc_mesh(), scratch_types=[])
    def gather(x_hbm, i_hbm, o_hbm):
        def body(i_vmem, o_vmem):
            pltpu.sync_copy(x_hbm.at[i_vmem.at[0]], o_vmem)

        pltpu.emit_pipeline(
            body,
            grid=(n // SC_WINDOW,),
            in_specs=[pl.BlockSpec((1, SC_WINDOW), lambda i: (0, i))],
            out_specs=[pl.BlockSpec((SC_WINDOW, width), lambda i: (i, 0))],
            core_axis_name=("core", "subcore"),
            dimension_semantics=(pltpu.PARALLEL,),
        )(i_hbm, o_hbm)

    return gather(table, idx.reshape(1, n))


def _moe_sparse(x1, routed, mod, seq_len, lw, w_gate_e, w_up_e, w_down_e, layer, *, tm, tm_e):
    t = x1.shape[0]
    n_experts = lw["b_router_t"].shape[0]
    assert n_experts * TOP_K == LANE
    n_row_tiles = (t * TOP_K) // tm_e + n_experts
    p = n_row_tiles * tm_e
    experts = jnp.arange(n_experts, dtype=jnp.int32)

    tp, sel, rank, cnt, wgt = routed
    rows = tp.shape[0]
    n_tok_tiles = sel.shape[0]
    cnt = cnt[:, :, 0].reshape(n_tok_tiles, TOP_K, n_experts).sum(axis=1)
    padded = (cnt.sum(axis=0) + tm_e - 1) // tm_e * tm_e
    group_end = jnp.cumsum(padded)
    base = (group_end - padded)[None, :] + jnp.cumsum(cnt, axis=0) - cnt
    chosen = sel[:, :TOP_K, :, None] == experts
    pos = jnp.sum(jnp.where(chosen, base[:, None, None, :], 0), axis=-1) + rank[:, :TOP_K, :]
    pos = pos.transpose(1, 0, 2).reshape(TOP_K, t)

    n_used = group_end[-1] // tm_e
    tile_start = jnp.arange(n_row_tiles, dtype=jnp.int32) * tm_e
    tile_expert = jnp.sum(tile_start[:, None] >= group_end[None, :], axis=1).astype(jnp.int32)
    tile_expert = jnp.minimum(tile_expert, n_experts - 1)
    tile_expert = jnp.where(tile_start < group_end[-1], tile_expert, tile_expert[n_used - 1])
    meta = jnp.stack([n_used, n_used]).astype(jnp.int32)

    idx = pos[:, None, :] + (jnp.arange(rows, dtype=jnp.int32) * p)[None, :, None]
    xs = _sc_scatter_rows(tp.reshape(rows * t, PACK_W), idx.reshape(TOP_K, rows * t), rows * p)
    after = yield wgt
    if after is not None:
        xs = _after(xs, after)
    ys = _experts(xs.reshape(rows, p, PACK_W), tile_expert, meta, w_gate_e, w_up_e, w_down_e, layer, tm=tm_e)
    yk = _sc_gather_rows(ys.reshape(rows * p, PACK_W), idx.reshape(-1))
    yield ys
    yield _combine(x1, mod, seq_len, lw, yk.reshape(TOP_K, rows, t, PACK_W), wgt, tm=tm)


def _rope_tables(n):
    rows = n // GRID_W
    r, col = jnp.meshgrid(jnp.arange(rows), jnp.arange(GRID_W), indexing="ij")
    r = r.reshape(-1).astype(F32)
    col = col.reshape(-1).astype(F32)
    pairs = QK_ROPE // 4
    inv = ROPE_BASE ** (-jnp.arange(pairs, dtype=F32) / pairs)
    ang = jnp.concatenate([r[:, None] * inv, col[:, None] * inv], axis=-1)
    cos, sin = jnp.cos(ang), jnp.sin(ang)
    pad = HEAD_PAD - QK_NOPE - QK_ROPE
    cos_t = jnp.concatenate([jnp.ones((n, QK_NOPE), F32), cos, cos, jnp.zeros((n, pad), F32)], axis=1)
    sin_t = jnp.concatenate([jnp.zeros((n, QK_NOPE), F32), sin, sin, jnp.zeros((n, pad), F32)], axis=1)
    return cos_t, sin_t


def _identity_tables(n):
    pad = HEAD_PAD - QK_NOPE - QK_ROPE
    cos_t = jnp.concatenate([jnp.ones((n, QK_NOPE + QK_ROPE), F32), jnp.zeros((n, pad), F32)], axis=1)
    return cos_t, jnp.zeros((n, HEAD_PAD), F32)


def _position_dft(n):
    nb = 64 if n % 64 == 0 else 1
    na = n // nb
    m = jnp.arange(n, dtype=jnp.int32)[None, :]
    ang_a = ((jnp.arange(na, dtype=jnp.int32)[:, None] * m) % na).astype(F32) * (2.0 * np.pi / na)
    ang_b = ((jnp.arange(nb, dtype=jnp.int32)[:, None] * m) % n).astype(F32) * (2.0 * np.pi / n)
    ca, sa = jnp.cos(ang_a)[:, None, :], jnp.sin(ang_a)[:, None, :]
    cb, sb = jnp.cos(ang_b)[None, :, :], jnp.sin(ang_b)[None, :, :]
    norm = 1.0 / np.sqrt(n)
    cos = ((ca * cb - sa * sb) * norm).reshape(n, n)
    sin = ((sa * cb + ca * sb) * (-norm)).reshape(n, n)
    return jnp.concatenate([cos, sin], axis=1).astype(BF16)


def _channel_dft(width):
    gc = width // FOURIER_GROUPS
    idx = (jnp.arange(gc, dtype=jnp.int32)[:, None] * jnp.arange(gc, dtype=jnp.int32)[None, :]) % gc
    ang = idx.astype(F32) * (2.0 * np.pi / gc)
    eye = jnp.eye(FOURIER_GROUPS, dtype=F32)
    norm = 1.0 / np.sqrt(gc)
    return jnp.concatenate([jnp.kron(eye, jnp.cos(ang) * norm), jnp.kron(eye, jnp.sin(ang) * norm)],
                           axis=1).astype(BF16)


def _w1_kernel(w_ref, o_ref, *, kv_rank):
    cols = w_ref.shape[2]
    kv_end = kv_rank + QK_ROPE
    half = QK_ROPE // 2
    tail = HEAD_PAD - QK_NOPE - QK_ROPE
    o_kpe = kv_rank
    o_rot = o_kpe + HEAD_PAD
    o_rest = o_rot + HEAD_PAD
    dt = o_ref.dtype
    o_ref[0, 0:kv_rank, :] = w_ref[0, 0:kv_rank, :].astype(dt)
    for base in (o_kpe, o_rot):
        o_ref[0, base:base + QK_NOPE, :] = jnp.zeros((QK_NOPE, cols), dt)
        o_ref[0, base + QK_NOPE + QK_ROPE:base + HEAD_PAD, :] = jnp.zeros((tail, cols), dt)
    o_ref[0, o_kpe + QK_NOPE:o_kpe + QK_NOPE + QK_ROPE, :] = w_ref[0, kv_rank:kv_end, :].astype(dt)
    o_ref[0, o_rot + QK_NOPE:o_rot + QK_NOPE + half, :] = (-w_ref[0, kv_rank + half:kv_end, :]).astype(dt)
    o_ref[0, o_rot + QK_NOPE + half:o_rot + QK_NOPE + QK_ROPE, :] = w_ref[0, kv_rank:kv_rank + half, :].astype(dt)
    o_ref[0, o_rest:, :] = w_ref[0, kv_end:, :].astype(dt)


def _prep_w1(w_in_t, kv_rank):
    n_layers, width, d = w_in_t.shape
    out_w = width - QK_ROPE + 2 * HEAD_PAD
    tc = _tile(d, 256)
    return pl.pallas_call(
        functools.partial(_w1_kernel, kv_rank=kv_rank),
        grid=(n_layers, d // tc),
        in_specs=[pl.BlockSpec((1, width, tc), lambda l, i: (l, 0, i))],
        out_specs=pl.BlockSpec((1, out_w, tc), lambda l, i: (l, 0, i)),
        out_shape=jax.ShapeDtypeStruct((n_layers, out_w, d), BF16),
        compiler_params=_params(2),
        name="prep_w1",
    )(w_in_t)


def _stacked_weights(g_pre_mix, g_post_mix, g_pre_ffn, g_post_ffn, b_gate, g_q, w_uq, g_kv, w_ukv,
                     w_mla_out, conv_w, w_conv_out, w_four_out, w_out, w_router, b_router, w_gate_s,
                     w_up_s, w_down_s):
    n_layers, kv_rank = g_kv.shape
    q_rank = g_q.shape[1]
    qk_dim = QK_NOPE + QK_ROPE
    uq = w_uq.reshape(n_layers, q_rank, N_HEADS, qk_dim) * (qk_dim ** -0.5 * np.log2(np.e))
    zq = jnp.zeros((n_layers, q_rank, N_HEADS, HEAD_PAD - qk_dim), F32)
    wuq = jnp.concatenate([uq, zq], axis=-1).reshape(n_layers, q_rank, N_HEADS * HEAD_PAD).astype(BF16)
    ukv = w_ukv.reshape(n_layers, kv_rank, N_HEADS, QK_NOPE + V_DIM)
    zk = jnp.zeros((n_layers, kv_rank, N_HEADS, HEAD_PAD - QK_NOPE), F32)
    zv = jnp.zeros((n_layers, kv_rank, N_HEADS, HEAD_PAD - V_DIM), F32)
    wuk = jnp.concatenate([ukv[..., :QK_NOPE], zk], axis=-1).reshape(n_layers, kv_rank, -1).astype(BF16)
    wuv = jnp.concatenate([ukv[..., QK_NOPE:], zv], axis=-1).reshape(n_layers, kv_rank, -1).astype(BF16)
    wr_hi = w_router.astype(BF16)
    wr_lo = (w_router - wr_hi.astype(F32)).astype(BF16)
    wr_t = jnp.concatenate([jnp.swapaxes(wr_hi, 1, 2), jnp.swapaxes(wr_lo, 1, 2)], axis=1)
    return {
        "g_pre_mix": g_pre_mix[:, None], "g_post_mix": g_post_mix[:, None],
        "g_pre_ffn": g_pre_ffn[:, None], "g_post_ffn": g_post_ffn[:, None],
        "b_gate": b_gate[:, None], "g_q": g_q[:, None], "g_kv": g_kv[:, None],
        "wuq": wuq, "wuk": wuk, "wuv": wuv, "conv_w": conv_w,
        "w_mla_out": w_mla_out.astype(BF16), "w_conv_out": w_conv_out.astype(BF16),
        "w_four_out": w_four_out.astype(BF16), "w_out": w_out.astype(BF16),
        "w_router_t": wr_t, "b_router_t": b_router[:, :, None],
        "w_gate_s": w_gate_s.astype(BF16), "w_up_s": w_up_s.astype(BF16), "w_down_s": w_down_s.astype(BF16),
    }


def _tile(n, pref):
    return pref if n % pref == 0 else n


def kernel(x, c, ctx, c_ctx, w_ada, b_ada, g_pre_mix, g_post_mix, g_pre_ffn, g_post_ffn, w_in, b_gate,
           g_q, w_uq, g_kv, w_ukv, w_mla_out, conv_w, w_conv_out, w_four_out, w_out, w_router, b_router,
           w_gate_e, w_up_e, w_down_e, w_gate_s, w_up_s, w_down_s):
    batch, seq, d = x.shape
    n_ctx = ctx.shape[1]
    n_layers = w_in.shape[0]
    xs = x.reshape(batch * seq, d)
    cs = ctx.reshape(batch * n_ctx, d)

    mod_rows = -(-(batch + 1) // (2 * SUBLANE)) * (2 * SUBLANE)
    c_all = jnp.concatenate([c, c_ctx[None], jnp.zeros((mod_rows - batch - 1, d), F32)], axis=0)
    ada = _ada(c_all, w_ada, b_ada).reshape(n_layers, mod_rows, 6, d)

    tm_x = _tile(seq, 512)
    tm_x_wide = _tile(seq, 1024)
    tq_x = _tile(seq, 2 * ATTN_ROWS)
    tm_c = _tile(n_ctx, 256)
    tm_c_mid = _tile(batch * n_ctx, 512)
    tm_c_wide = _tile(batch * n_ctx, 1024)
    te_x, te_c = 1024, 512

    tab_x = _rope_tables(seq)
    tab_c = _identity_tables(tm_c_mid)
    cs_x = _position_dft(seq)
    cs_c = _position_dft(n_ctx)
    dc = _channel_dft(w_four_out.shape[1])
    w1 = _prep_w1(jnp.swapaxes(w_in, 1, 2), g_kv.shape[1])
    stacked = _stacked_weights(g_pre_mix, g_post_mix, g_pre_ffn, g_post_ffn, b_gate, g_q, w_uq, g_kv, w_ukv,
                               w_mla_out, conv_w, w_conv_out, w_four_out, w_out, w_router, b_router,
                               w_gate_s, w_up_s, w_down_s)
    v_one = jnp.tile((jnp.arange(HEAD_PAD) == V_DIM).astype(F32), N_HEADS)[None]

    for l in range(n_layers):
        last = l == n_layers - 1
        lw = {name: _Layer(arr, l) for name, arr in stacked.items()}
        lw.update(w1=w1, layer=l, dc=dc, v_one=v_one)
        mod_x = _Mod(ada, l, 0, batch)
        mod_c = _Mod(ada, l, batch, 1)

        pc = _inproj(cs, mod_c, n_ctx, lw, tab_c, kv_only=last, tm=tm_c_mid)
        px = _inproj(xs, mod_x, seq, lw, tab_x, kv_only=False, tm=tm_x)
        o_x = _attention(px["q"], [(pc["k"], pc["v"], n_ctx), (px["k"], px["v"], seq)], batch, seq, tq=tq_x)
        f_x = _fourier(px["ab"], cs_x, batch, seq, tn=seq)
        x1, routed_x = _merge(xs, mod_x, seq, px, o_x, f_x, lw, tm=tm_x)
        moe_x = _moe_sparse(x1, routed_x, mod_x, seq, lw, w_gate_e, w_up_e, w_down_e, l, tm=tm_x_wide,
                            tm_e=te_x)
        if last:
            xs = list(moe_x)[-1]
        else:
            q_c = _after(pc["q"], next(moe_x))
            o_c = _attention(q_c, [(pc["k"], pc["v"], n_ctx)], batch, n_ctx, tq=tm_c)
            f_c = _fourier(pc["ab"], cs_c, batch, n_ctx, tn=tm_c)
            c1, routed_c = _merge(cs, mod_c, n_ctx, pc, o_c, f_c, lw, tm=tm_c_mid)
            moe_c = _moe_sparse(c1, routed_c, mod_c, batch * n_ctx, lw, w_gate_e, w_up_e, w_down_e, l,
                                tm=tm_c_wide, tm_e=te_c)
            next(moe_c)
            experts_x = moe_x.send(None)
            moe_c.send(experts_x)
            xs = next(moe_x)
            cs = next(moe_c)
    return xs.reshape(batch, seq, d)
```

```python
import functools

import numpy as np
import jax
import jax.numpy as jnp
from jax import lax
from jax.experimental import pallas as pl
from jax.experimental.pallas import tpu as pltpu
from jax.experimental.pallas import tpu_sc as plsc

N_HEADS = 8
QK_NOPE = 64
QK_ROPE = 32
V_DIM = 64
GRID_W = 64
ROPE_BASE = 10000.0
FOURIER_GROUPS = 4
TOP_K = 4
ROUTED_SCALE = 2.5
N_BRANCHES = 3
EPS = 1e-6

LANE = 128
SUBLANE = 8
HEAD_PAD = LANE
VMEM_LIMIT = 56 * 1024 * 1024
PACK_W = 256
SC_WINDOW = 128
ATTN_ROWS = 512
MERGE_ROWS = 256

F32 = jnp.float32
BF16 = jnp.bfloat16


def _rms(x, g):
    return x * lax.rsqrt(jnp.mean(x * x, axis=-1, keepdims=True) + EPS) * g


def _sigmoid(x):
    return 1.0 / (1.0 + jnp.exp(-x))


def _dot(a, b):
    return jnp.dot(a, b, preferred_element_type=F32)


def _dot_t(a, b_t):
    return lax.dot_general(a, b_t, (((1,), (1,)), ((), ())), preferred_element_type=F32)


def _skewed(stages, n_chunks, chunk):
    live = [None] * n_chunks
    for step in range(n_chunks + len(stages) - 1):
        for c in range(n_chunks):
            s = step - c
            if 0 <= s < len(stages):
                live[c] = stages[s](c * chunk) if s == 0 else stages[s](c * chunk, live[c])


def _after(value, dependency):
    return lax.optimization_barrier((value, dependency))[0]


def _resident(shape):
    nd = len(shape)
    return pl.BlockSpec(shape, lambda *_: (0,) * nd, pipeline_mode=pl.Buffered(1))


class _Layer:
    def __init__(self, stacked, index):
        self.stacked, self.index = stacked, index
        self.shape = stacked.shape[1:]

    def spec(self):
        index = (self.index,) + (0,) * len(self.shape)
        return pl.BlockSpec((None,) + self.shape, lambda *_: index, pipeline_mode=pl.Buffered(1))


class _Mod:
    def __init__(self, table, layer, row0, n):
        self.table, self.layer, self.row0, self.n = table, layer, row0, n

    def spec(self, seq_of_step):
        layer, row0, n = self.layer, self.row0, self.n
        return pl.BlockSpec((None, 1) + self.table.shape[2:],
                            lambda i: (layer, row0 + seq_of_step(i) % n, 0, 0))


def _params(n_grid):
    return pltpu.CompilerParams(dimension_semantics=("arbitrary",) * n_grid,
                                vmem_limit_bytes=VMEM_LIMIT)


def _ada_kernel(c_ref, w_ref, b_ref, o_ref):
    c = c_ref[...]
    a = (c * _sigmoid(c)).astype(BF16)
    o_ref[0] = _dot(a, w_ref[0].astype(BF16)) + b_ref[0]


def _ada(c_all, w_ada, b_ada):
    n_layers, d, n_out = w_ada.shape
    rows = c_all.shape[0]
    tn = _tile(n_out, 1536)
    return pl.pallas_call(
        _ada_kernel,
        grid=(n_layers, n_out // tn),
        in_specs=[
            pl.BlockSpec((rows, d), lambda l, j: (0, 0)),
            pl.BlockSpec((1, d, tn), lambda l, j: (l, 0, j)),
            pl.BlockSpec((1, 1, tn), lambda l, j: (l, 0, j)),
        ],
        out_specs=pl.BlockSpec((1, rows, tn), lambda l, j: (l, 0, j)),
        out_shape=jax.ShapeDtypeStruct((n_layers, rows, n_out), F32),
        compiler_params=_params(2),
        name="ada",
    )(c_all, w_ada, b_ada.reshape(n_layers, 1, n_out))


def _inproj_kernel(*refs, kv_only, kv_rank, q_rank, conv_w, four_w, d_model):
    if kv_only:
        (x_ref, mod_ref, gpre_ref, w1_ref, gkv_ref, wuk_ref, wuv_ref, vone_ref, cos_ref, sin_ref,
         k_ref, v_ref) = refs
    else:
        (x_ref, mod_ref, gpre_ref, w1_ref, gkv_ref, wuk_ref, wuv_ref, vone_ref, cos_ref, sin_ref,
         bg_ref, gq_ref, wuq_ref, dc_ref,
         k_ref, v_ref, q_ref, cb_ref, cc_ref, cu_ref, ab_ref, gate_ref) = refs

    x = x_ref[...]
    shift = mod_ref[0, 0:1, :]
    scale = mod_ref[0, 1:2, :]
    h = (_rms(x, gpre_ref[...]) * (1.0 + scale) + shift).astype(BF16)
    cos = cos_ref[...]
    sin = sin_ref[...]

    o_kpe = kv_rank
    o_rot = o_kpe + HEAD_PAD
    o_q = o_rot + HEAD_PAD
    p = _dot_t(h, w1_ref[0, 0:o_q, :])
    ckv = _rms(p[:, 0:kv_rank], gkv_ref[...]).astype(BF16)
    kpe = p[:, o_kpe:o_rot] * cos + p[:, o_rot:o_q] * sin
    k = _dot(ckv, wuk_ref[...]) + jnp.concatenate([kpe] * N_HEADS, axis=1)
    k_ref[...] = k.astype(k_ref.dtype)
    v_ref[...] = (_dot(ckv, wuv_ref[...]) + vone_ref[...]).astype(v_ref.dtype)
    if kv_only:
        return

    o_cb = o_q + q_rank
    cq = _rms(_dot_t(h, w1_ref[0, o_q:o_cb, :]), gq_ref[...]).astype(BF16)
    half = QK_ROPE // 2
    lane = lax.broadcasted_iota(jnp.int32, (1, HEAD_PAD), 1)
    first = (lane >= QK_NOPE) & (lane < QK_NOPE + half)
    cos_h = jnp.concatenate([cos] * N_HEADS, axis=1)
    sin_h = jnp.concatenate([jnp.where(first, -sin, sin)] * N_HEADS, axis=1)
    first_h = jnp.concatenate([first] * N_HEADS, axis=1)
    lin = _dot(cq, wuq_ref[...])
    width = lin.shape[1]
    partner = jnp.where(first_h, pltpu.roll(lin, width - half, 1), pltpu.roll(lin, half, 1))
    q_ref[...] = (lin * cos_h + partner * sin_h).astype(q_ref.dtype)

    o_cc = o_cb + conv_w
    o_cu = o_cc + conv_w
    o_four = o_cu + conv_w
    cb_ref[...] = _dot_t(h, w1_ref[0, o_cb:o_cc, :]).astype(cb_ref.dtype)
    cc_ref[...] = _dot_t(h, w1_ref[0, o_cc:o_cu, :]).astype(cc_ref.dtype)
    cu_ref[...] = _dot_t(h, w1_ref[0, o_cu:o_four, :]).astype(cu_ref.dtype)

    o_gate = o_four + four_w
    uf = _dot_t(h, w1_ref[0, o_four:o_gate, :]).astype(BF16)
    ab_ref[...] = _dot(uf, dc_ref[...]).astype(ab_ref.dtype)

    for j in range(N_BRANCHES):
        lo = o_gate + j * d_model
        z = _dot_t(h, w1_ref[0, lo:lo + d_model, :]) + bg_ref[:, j * d_model:(j + 1) * d_model]
        gate_ref[:, j * d_model:(j + 1) * d_model] = _sigmoid(z).astype(gate_ref.dtype)


def _inproj(xs, mod, seq_len, lw, tables, *, kv_only, tm):
    t, d = xs.shape
    cos_t, sin_t = tables
    table_tiles = cos_t.shape[0] // tm
    kv_rank = lw["g_kv"].shape[1]
    q_rank = lw["g_q"].shape[1]
    conv_w = lw["conv_w"].shape[1]
    four_w = lw["dc"].shape[0]
    n_k = N_HEADS * HEAD_PAD

    def row(i):
        return (i, 0)

    def tab_map(i):
        return (i % table_tiles, 0)

    w1, layer = lw["w1"], lw["layer"]
    w1_rows = kv_rank + 2 * HEAD_PAD if kv_only else w1.shape[1]
    in_specs = [
        pl.BlockSpec((tm, d), row),
        mod.spec(lambda i: i * tm // seq_len),
        lw["g_pre_mix"].spec(),
        pl.BlockSpec((1, w1_rows, d), lambda i: (layer, 0, 0), pipeline_mode=pl.Buffered(1)),
        lw["g_kv"].spec(),
        lw["wuk"].spec(),
        lw["wuv"].spec(),
        _resident(lw["v_one"].shape),
        pl.BlockSpec((tm, HEAD_PAD), tab_map),
        pl.BlockSpec((tm, HEAD_PAD), tab_map),
    ]
    args = [xs, mod.table, lw["g_pre_mix"].stacked, w1, lw["g_kv"].stacked, lw["wuk"].stacked, lw["wuv"].stacked,
            lw["v_one"], cos_t, sin_t]
    out_shape = [jax.ShapeDtypeStruct((t, n_k), BF16), jax.ShapeDtypeStruct((t, n_k), BF16)]
    out_specs = [pl.BlockSpec((tm, n_k), row), pl.BlockSpec((tm, n_k), row)]
    if not kv_only:
        in_specs += [
            lw["b_gate"].spec(),
            lw["g_q"].spec(),
            lw["wuq"].spec(),
            _resident(lw["dc"].shape),
        ]
        args += [lw["b_gate"].stacked, lw["g_q"].stacked, lw["wuq"].stacked, lw["dc"]]
        widths = [n_k, conv_w, conv_w, conv_w, 2 * four_w, N_BRANCHES * d]
        out_shape += [jax.ShapeDtypeStruct((t, w), BF16) for w in widths]
        out_specs += [pl.BlockSpec((tm, w), row) for w in widths]
    outs = pl.pallas_call(
        functools.partial(_inproj_kernel, kv_only=kv_only, kv_rank=kv_rank, q_rank=q_rank,
                          conv_w=conv_w, four_w=four_w, d_model=d),
        grid=(t // tm,),
        in_specs=in_specs,
        out_specs=out_specs,
        out_shape=out_shape,
        compiler_params=_params(1),
        name="inproj_kv" if kv_only else "inproj",
    )(*args)
    names = ["k", "v", "q", "cb", "cc", "cu", "ab", "gate"]
    return dict(zip(names, outs))


def _attn_kernel(*refs, n_seg):
    q_ref = refs[0]
    o_ref = refs[-1]
    chunk = min(q_ref.shape[0], ATTN_ROWS)
    for c in range(q_ref.shape[0] // chunk):
        rows = slice(c * chunk, (c + 1) * chunk)
        outs = []
        for hh in range(N_HEADS):
            head = slice(hh * HEAD_PAD, (hh + 1) * HEAD_PAD)
            qh = q_ref[rows, head]
            s = [_dot_t(qh, refs[1 + 2 * i][:, head]) for i in range(n_seg)]
            m = functools.reduce(jnp.maximum, [jnp.max(si, axis=-1, keepdims=True) for si in s])
            acc = functools.reduce(jnp.add, [
                _dot(jnp.exp2((s[i] - m).astype(BF16)), refs[2 + 2 * i][:, head]) for i in range(n_seg)])
            outs.append(acc[:, 0:V_DIM] / acc[:, V_DIM:V_DIM + 1])
        o_ref[rows, :] = jnp.concatenate(outs, axis=1).astype(o_ref.dtype)


def _attention(q, segs, batch, seq_q, *, tq):
    t = q.shape[0]
    qt = seq_q // tq
    n_k = N_HEADS * HEAD_PAD
    in_specs = [pl.BlockSpec((tq, n_k), lambda b, j: (b * qt + j, 0))]
    args = [q]
    for k, v, m in segs:
        in_specs.append(pl.BlockSpec((m, n_k), lambda b, j: (b, 0)))
        in_specs.append(pl.BlockSpec((m, n_k), lambda b, j: (b, 0)))
        args += [k, v]
    return pl.pallas_call(
        functools.partial(_attn_kernel, n_seg=len(segs)),
        grid=(batch, qt),
        in_specs=in_specs,
        out_specs=pl.BlockSpec((tq, N_HEADS * V_DIM), lambda b, j: (b * qt + j, 0)),
        out_shape=jax.ShapeDtypeStruct((t, N_HEADS * V_DIM), BF16),
        compiler_params=_params(2),
        name="attention",
    )(*args)


def _four_kernel(cs_ref, ab_ref, o_ref, *, n, fw):
    o = _dot(cs_ref[:, 0:n], ab_ref[:, 0:fw]) + _dot(cs_ref[:, n:2 * n], ab_ref[:, fw:2 * fw])
    o_ref[...] = o.astype(o_ref.dtype)


def _fourier(ab, cs, batch, seq_len, *, tn):
    t, fw2 = ab.shape
    fw = fw2 // 2
    nt = seq_len // tn
    cs_spec = (_resident(cs.shape) if nt == 1
               else pl.BlockSpec((tn, 2 * seq_len), lambda b, j: (j, 0)))
    return pl.pallas_call(
        functools.partial(_four_kernel, n=seq_len, fw=fw),
        grid=(batch, nt),
        in_specs=[
            cs_spec,
            pl.BlockSpec((seq_len, fw2), lambda b, j: (b, 0)),
        ],
        out_specs=pl.BlockSpec((tn, fw), lambda b, j: (b * nt + j, 0)),
        out_shape=jax.ShapeDtypeStruct((t, fw), BF16),
        compiler_params=_params(2),
        name="fourier",
    )(cs, ab)


def _merge_kernel(x_ref, mod_ref, o_ref, cb_ref, cc_ref, cu_ref, ccp_ref, cup_ref, ccn_ref,
                  cun_ref, f_ref, gate_ref, convw_ref, wmo_ref, wco_ref, wfo_ref, wout_ref,
                  gpost_ref, gffn_ref, wrt_ref, brt_ref, triu_ref,
                  out_ref, tp_ref, sel_ref, rank_ref, cnt_ref, wgt_ref, pad_ref,
                  *, seq_len, tm, d_model, n_experts):
    i = pl.program_id(0)
    has_prev = ((i * tm) % seq_len != 0).astype(F32)
    has_next = (((i + 1) * tm) % seq_len != 0).astype(F32)
    lo = SUBLANE
    pad_ref[0:lo, :] = ccp_ref[...].astype(F32) * cup_ref[...].astype(F32) * has_prev
    pad_ref[lo:lo + tm, :] = cc_ref[...].astype(F32) * cu_ref[...].astype(F32)
    pad_ref[lo + tm:2 * lo + tm, :] = ccn_ref[...].astype(F32) * cun_ref[...].astype(F32) * has_next
    d = d_model
    g1 = mod_ref[0, 2:3, :]
    chunk = min(tm, MERGE_ROWS)

    def conv_stage(r0):
        before = pad_ref[lo - 1 + r0:lo - 1 + r0 + chunk, :]
        after = pad_ref[lo + 1 + r0:lo + 1 + r0 + chunk, :]
        if tm > seq_len:
            at = (lax.broadcasted_iota(jnp.int32, (chunk, 1), 0) + r0) % seq_len
            before = jnp.where(at == 0, 0.0, before)
            after = jnp.where(at == seq_len - 1, 0.0, after)
        conv = (before * convw_ref[0:1, :] + pad_ref[lo + r0:lo + r0 + chunk, :] * convw_ref[1:2, :]
                + after * convw_ref[2:3, :])
        return (cb_ref[r0:r0 + chunk, :].astype(F32) * conv).astype(BF16)

    def branch_stage(r0, conv_in):
        return (_dot(o_ref[r0:r0 + chunk, :], wmo_ref[...]), _dot(conv_in, wco_ref[...]),
                _dot(f_ref[r0:r0 + chunk, :], wfo_ref[...]))

    def gate_stage(r0, ys):
        rows = slice(r0, r0 + chunk)
        return (gate_ref[rows, 0:d].astype(F32) * ys[0] + gate_ref[rows, d:2 * d].astype(F32) * ys[1]
                + gate_ref[rows, 2 * d:3 * d].astype(F32) * ys[2]).astype(BF16)

    def out_stage(r0, merged):
        return _dot(merged, wout_ref[...])

    def tail_stage(r0, y):
        rows = slice(r0, r0 + chunk)
        out_ref[rows, :] = x_ref[rows, :] + g1 * _rms(y, gpost_ref[...])

    _skewed((conv_stage, branch_stage, gate_stage, out_stage, tail_stage), tm // chunk, chunk)
    _route_tile(out_ref[...], mod_ref, gffn_ref, wrt_ref, brt_ref, triu_ref,
                tp_ref, sel_ref, rank_ref, cnt_ref, wgt_ref, n_experts)


def _merge(xs, mod, seq_len, pr, o, four, lw, *, tm):
    t, d = xs.shape
    assert tm % seq_len == 0 or seq_len % tm == 0
    cw = lw["conv_w"].shape[1]
    fw = four.shape[1]
    hb = tm // SUBLANE
    last_hb = t // SUBLANE - 1

    def row(i):
        return (i, 0)

    def prev_map(i):
        return (jnp.maximum(i * hb - 1, 0), 0)

    def next_map(i):
        return (jnp.minimum((i + 1) * hb, last_hb), 0)

    in_specs = [
        pl.BlockSpec((tm, d), row),
        mod.spec(lambda i: i * tm // seq_len),
        pl.BlockSpec((tm, o.shape[1]), row),
        pl.BlockSpec((tm, cw), row),
        pl.BlockSpec((tm, cw), row),
        pl.BlockSpec((tm, cw), row),
        pl.BlockSpec((SUBLANE, cw), prev_map),
        pl.BlockSpec((SUBLANE, cw), prev_map),
        pl.BlockSpec((SUBLANE, cw), next_map),
        pl.BlockSpec((SUBLANE, cw), next_map),
        pl.BlockSpec((tm, fw), row),
        pl.BlockSpec((tm, N_BRANCHES * d), row),
        lw["conv_w"].spec(),
        lw["w_mla_out"].spec(),
        lw["w_conv_out"].spec(),
        lw["w_four_out"].spec(),
        lw["w_out"].spec(),
        lw["g_post_mix"].spec(),
        lw["g_pre_ffn"].spec(),
        lw["w_router_t"].spec(),
        lw["b_router_t"].spec(),
        _resident((tm, tm)),
    ]
    n_tiles = t // tm
    n_experts = lw["b_router_t"].shape[0]
    rows = d // (2 * PACK_W)
    triu = jnp.tri(tm, tm, -1, dtype=BF16).T
    tile3 = lambda i: (i, 0, 0)
    outs = pl.pallas_call(
        functools.partial(_merge_kernel, seq_len=seq_len, tm=tm, d_model=d, n_experts=n_experts),
        grid=(n_tiles,),
        in_specs=in_specs,
        out_specs=[
            pl.BlockSpec((tm, d), row),
            pl.BlockSpec((rows, tm, PACK_W), lambda i: (0, i, 0)),
            pl.BlockSpec((1, SUBLANE, tm), tile3),
            pl.BlockSpec((1, SUBLANE, tm), tile3),
            pl.BlockSpec((1, LANE, LANE), tile3),
            pl.BlockSpec((tm, LANE), row),
        ],
        out_shape=[
            jax.ShapeDtypeStruct((t, d), F32),
            jax.ShapeDtypeStruct((rows, t, PACK_W), jnp.int32),
            jax.ShapeDtypeStruct((n_tiles, SUBLANE, tm), jnp.int32),
            jax.ShapeDtypeStruct((n_tiles, SUBLANE, tm), jnp.int32),
            jax.ShapeDtypeStruct((n_tiles, LANE, LANE), jnp.int32),
            jax.ShapeDtypeStruct((t, LANE), F32),
        ],
        scratch_shapes=[pltpu.VMEM((tm + 2 * SUBLANE, cw), F32)],
        compiler_params=_params(1),
        name="merge",
    )(xs, mod.table, o, pr["cb"], pr["cc"], pr["cu"], pr["cc"], pr["cu"], pr["cc"], pr["cu"], four,
      pr["gate"], lw["conv_w"].stacked, lw["w_mla_out"].stacked, lw["w_conv_out"].stacked,
      lw["w_four_out"].stacked, lw["w_out"].stacked, lw["g_post_mix"].stacked,
      lw["g_pre_ffn"].stacked, lw["w_router_t"].stacked, lw["b_router_t"].stacked, triu)
    return outs[0], outs[1:]


def _pack_rows(v):
    bits = lax.bitcast_convert_type(v.astype(BF16).astype(F32), jnp.uint32)
    rows = []
    for j in range(v.shape[1] // (2 * PACK_W)):
        lo = bits[:, (2 * j) * PACK_W:(2 * j + 1) * PACK_W]
        hi = bits[:, (2 * j + 1) * PACK_W:(2 * j + 2) * PACK_W]
        rows.append(lax.bitcast_convert_type((hi & jnp.uint32(0xFFFF0000)) | (lo >> 16), jnp.int32))
    return rows


def _unpack_rows(rows):
    parts = []
    for r in rows:
        u = lax.bitcast_convert_type(r, jnp.uint32)
        parts.append(lax.bitcast_convert_type(u << 16, F32))
        parts.append(lax.bitcast_convert_type(u & jnp.uint32(0xFFFF0000), F32))
    return jnp.concatenate(parts, axis=1)


def _route_tile(x, mod_ref, gpre_ref, wrt_ref, brt_ref, triu_ref,
                tp_ref, sel_ref, rank_ref, cnt_ref, wgt_ref, n_experts):
    shift = mod_ref[0, 3:4, :]
    scale = mod_ref[0, 4:5, :]
    t = _rms(x, gpre_ref[...]) * (1.0 + scale) + shift
    t_hi = t.astype(BF16)
    t_lo = (t - t_hi.astype(F32)).astype(BF16)
    for j, r in enumerate(_pack_rows(t)):
        tp_ref[j] = r
    tm = t.shape[0]
    hh = _dot_t(wrt_ref[...], t_hi)
    logits = hh[0:n_experts] + hh[n_experts:2 * n_experts] + _dot_t(wrt_ref[0:n_experts, :], t_lo)
    scores = _sigmoid(logits)
    work = scores + brt_ref[...]
    row = lax.broadcasted_iota(jnp.int32, scores.shape, 0)
    wide = lax.broadcasted_iota(jnp.int32, (LANE, tm), 0)
    firsts, picked, hits = [], [], []
    for k in range(TOP_K):
        best = jnp.max(work, axis=0, keepdims=True)
        first = jnp.min(jnp.where(work == best, row, n_experts), axis=0, keepdims=True)
        hit = row == first
        firsts.append(first)
        picked.append(jnp.sum(jnp.where(hit, scores, 0.0), axis=0, keepdims=True))
        hits.append(wide == first + k * n_experts)
        work = jnp.where(hit, -jnp.inf, work)
    total = functools.reduce(jnp.add, picked)
    onehot = functools.reduce(jnp.add, [jnp.where(h, 1.0, 0.0) for h in hits])
    earlier = _dot(onehot.astype(BF16), triu_ref[...])
    col = jnp.broadcast_to(jnp.sum(onehot, axis=1, keepdims=True), (LANE, LANE))
    row_c = lax.broadcasted_iota(jnp.int32, (LANE, LANE), 0)
    before = jnp.zeros((LANE, LANE), F32)
    for s in range(1, TOP_K):
        before = before + jnp.where(row_c >= s * n_experts, pltpu.roll(col, s * n_experts, 0), 0.0)
    ahead = earlier + before[:, 0:1]
    row8 = lax.broadcasted_iota(jnp.int32, (SUBLANE, tm), 0)
    sel = jnp.zeros((SUBLANE, tm), jnp.int32)
    rank = jnp.zeros((SUBLANE, tm), F32)
    wgt_t = jnp.zeros((LANE, tm), F32)
    for k in range(TOP_K):
        sel = jnp.where(row8 == k, firsts[k], sel)
        rank = jnp.where(row8 == k, jnp.sum(jnp.where(hits[k], ahead, 0.0), axis=0, keepdims=True), rank)
        wgt_t = jnp.where(wide == k, picked[k] / total * ROUTED_SCALE, wgt_t)
    sel_ref[0] = sel
    rank_ref[0] = rank.astype(jnp.int32)
    cnt_ref[0] = col.astype(jnp.int32)
    wgt_ref[...] = wgt_t.T


def _expert_kernel(te_ref, meta_ref, nxt_ref, par_ref, xs_ref, wg_hbm, wu_hbm, wd_hbm, ys_ref,
                   wg_sc, wu_sc, wd_sc, wg_buf, wu_buf, wd_buf, sems, *, layer):
    i = pl.program_id(0)
    live = i < meta_ref[0]
    new_expert = (i == 0) | (te_ref[i] != te_ref[jnp.maximum(i - 1, 0)])

    def fetch(expert, slot):
        return [pltpu.make_async_copy(hbm.at[layer, expert], buf.at[slot], sems.at[slot, m])
                for m, (hbm, buf) in enumerate(((wg_hbm, wg_buf), (wu_hbm, wu_buf), (wd_hbm, wd_buf)))]

    @pl.when(live & (i == 0))
    def _():
        for copy in fetch(te_ref[0], par_ref[0]):
            copy.start()

    @pl.when(live & new_expert)
    def _():
        slot = par_ref[i]
        for copy in fetch(te_ref[i], slot):
            copy.wait()
        wg_sc[...] = wg_buf[slot].astype(BF16)
        wu_sc[...] = wu_buf[slot].astype(BF16)
        wd_sc[...] = wd_buf[slot].astype(BF16)

        @pl.when(nxt_ref[i] >= 0)
        def _():
            for copy in fetch(nxt_ref[i], 1 - slot):
                copy.start()

    @pl.when(live)
    def _():
        x = _unpack_rows([xs_ref[j] for j in range(xs_ref.shape[0])]).astype(BF16)
        gate = _dot(x, wg_sc[...])
        act = (gate * _sigmoid(gate) * _dot(x, wu_sc[...])).astype(BF16)
        for j, r in enumerate(_pack_rows(_dot(act, wd_sc[...]))):
            ys_ref[j] = r


def _experts(xs, tile_expert, meta, w_gate_e, w_up_e, w_down_e, layer, *, tm):
    rows, p, _ = xs.shape
    n_experts, d, f = w_gate_e.shape[1:]

    def slot(i, te, meta, nxt, par):
        return (0, jnp.minimum(i, meta[0] - 1), 0)

    n_tiles = p // tm
    tile = jnp.arange(n_tiles, dtype=jnp.int32)
    live = tile < meta[0]
    later = live[None, :] & (tile_expert[None, :] > tile_expert[:, None])
    nxt = jnp.min(jnp.where(later, tile_expert[None, :], n_experts), axis=1)
    nxt = jnp.where(nxt < n_experts, nxt, -1).astype(jnp.int32)
    starts = jnp.concatenate([jnp.ones((1,), jnp.int32),
                              (tile_expert[1:] != tile_expert[:-1]).astype(jnp.int32)])
    par = ((jnp.cumsum(starts) - 1) % 2).astype(jnp.int32)

    grid_spec = pltpu.PrefetchScalarGridSpec(
        num_scalar_prefetch=4,
        grid=(n_tiles,),
        in_specs=[
            pl.BlockSpec((rows, tm, PACK_W), slot),
            pl.BlockSpec(memory_space=pl.ANY),
            pl.BlockSpec(memory_space=pl.ANY),
            pl.BlockSpec(memory_space=pl.ANY),
        ],
        out_specs=pl.BlockSpec((rows, tm, PACK_W), slot),
        scratch_shapes=[pltpu.VMEM((d, f), BF16), pltpu.VMEM((d, f), BF16), pltpu.VMEM((f, d), BF16),
                        pltpu.VMEM((2, d, f), F32), pltpu.VMEM((2, d, f), F32), pltpu.VMEM((2, f, d), F32),
                        pltpu.SemaphoreType.DMA((2, 3))],
    )
    return pl.pallas_call(
        functools.partial(_expert_kernel, layer=layer),
        grid_spec=grid_spec,
        out_shape=jax.ShapeDtypeStruct(xs.shape, jnp.int32),
        compiler_params=_params(1),
        name="moe_experts",
    )(tile_expert, meta, nxt, par, xs, w_gate_e, w_up_e, w_down_e)


def _combine_kernel(x_ref, mod_ref, gpre_ref, gpost_ref, wgs_ref, wus_ref, wds_ref, yk_ref, wgt_ref,
                    out_ref):
    shift = mod_ref[0, 3:4, :]
    scale = mod_ref[0, 4:5, :]
    x = x_ref[...]
    t_hi = (_rms(x, gpre_ref[...]) * (1.0 + scale) + shift).astype(BF16)
    gate = _dot(t_hi, wgs_ref[...])
    act = (gate * _sigmoid(gate) * _dot(t_hi, wus_ref[...])).astype(BF16)
    acc = _dot(act, wds_ref[...])
    for k in range(TOP_K):
        y = _unpack_rows([yk_ref[k, j] for j in range(yk_ref.shape[1])])
        acc = acc + wgt_ref[:, k:k + 1] * y
    g2 = mod_ref[0, 5:6, :]
    out_ref[...] = x + g2 * _rms(acc, gpost_ref[...])


def _combine(x1, mod, seq_len, lw, yk, wgt, *, tm):
    t, d = x1.shape
    tiles_per_seq = max(seq_len // tm, 1)
    rows = yk.shape[1]
    row = lambda i: (i, 0)
    return pl.pallas_call(
        _combine_kernel,
        grid=(t // tm,),
        in_specs=[
            pl.BlockSpec((tm, d), row),
            mod.spec(lambda i: i // tiles_per_seq),
            lw["g_pre_ffn"].spec(),
            lw["g_post_ffn"].spec(),
            lw["w_gate_s"].spec(),
            lw["w_up_s"].spec(),
            lw["w_down_s"].spec(),
            pl.BlockSpec((TOP_K, rows, tm, PACK_W), lambda i: (0, 0, i, 0)),
            pl.BlockSpec((tm, LANE), row),
        ],
        out_specs=pl.BlockSpec((tm, d), row),
        out_shape=jax.ShapeDtypeStruct((t, d), F32),
        compiler_params=_params(1),
        name="moe_combine",
    )(x1, mod.table, lw["g_pre_ffn"].stacked, lw["g_post_ffn"].stacked, lw["w_gate_s"].stacked, lw["w_up_s"].stacked,
      lw["w_down_s"].stacked, yk, wgt)


def _sc_mesh():
    return plsc.VectorSubcoreMesh(core_axis_name="core", subcore_axis_name="subcore")


def _sc_scatter_rows(src, idx, n_out):
    n_lists, n = idx.shape
    width = src.shape[1]

    @pl.kernel(out_type=jax.ShapeDtypeStruct((n_out, width), src.dtype), mesh=_sc_mesh(), scratch_types=[])
    def scatter(x_hbm, *refs):
        i_hbms, o_hbm = refs[:n_lists], refs[n_lists]

        def body(x_vmem, *i_vmems):
            for i_vmem in i_vmems:
                pltpu.sync_copy(x_vmem, o_hbm.at[i_vmem.at[0]])

        pltpu.emit_pipeline(
            body,
            grid=(n // SC_WINDOW,),
            in_specs=[pl.BlockSpec((SC_WINDOW, width), lambda i: (i, 0))]
            + [pl.BlockSpec((1, SC_WINDOW), lambda i, r=r: (r, i)) for r in range(n_lists)],
            out_specs=[],
            core_axis_name=("core", "subcore"),
            dimension_semantics=(pltpu.PARALLEL,),
        )(x_hbm, *i_hbms)

    return scatter(src, *([idx] * n_lists))


def _sc_gather_rows(table, idx):
    n = idx.shape[0]
    width = table.shape[1]

    @pl.kernel(out_type=jax.ShapeDtypeStruct((n, width), table.dtype), mesh=_sc_mesh(), scratch_types=[])
    def gather(x_hbm, i_hbm, o_hbm):
        def body(i_vmem, o_vmem):
            pltpu.sync_copy(x_hbm.at[i_vmem.at[0]], o_vmem)

        pltpu.emit_pipeline(
            body,
            grid=(n // SC_WINDOW,),
            in_specs=[pl.BlockSpec((1, SC_WINDOW), lambda i: (0, i))],
            out_specs=[pl.BlockSpec((SC_WINDOW, width), lambda i: (i, 0))],
            core_axis_name=("core", "subcore"),
            dimension_semantics=(pltpu.PARALLEL,),
        )(i_hbm, o_hbm)

    return gather(table, idx.reshape(1, n))


def _moe_sparse(x1, routed, mod, seq_len, lw, w_gate_e, w_up_e, w_down_e, layer, *, tm, tm_e):
    t = x1.shape[0]
    n_experts = lw["b_router_t"].shape[0]
    assert n_experts * TOP_K == LANE
    n_row_tiles = (t * TOP_K) // tm_e + n_experts
    p = n_row_tiles * tm_e
    experts = jnp.arange(n_experts, dtype=jnp.int32)

    tp, sel, rank, cnt, wgt = routed
    rows = tp.shape[0]
    n_tok_tiles = sel.shape[0]
    cnt = cnt[:, :, 0].reshape(n_tok_tiles, TOP_K, n_experts).sum(axis=1)
    padded = (cnt.sum(axis=0) + tm_e - 1) // tm_e * tm_e
    group_end = jnp.cumsum(padded)
    base = (group_end - padded)[None, :] + jnp.cumsum(cnt, axis=0) - cnt
    chosen = sel[:, :TOP_K, :, None] == experts
    pos = jnp.sum(jnp.where(chosen, base[:, None, None, :], 0), axis=-1) + rank[:, :TOP_K, :]
    pos = pos.transpose(1, 0, 2).reshape(TOP_K, t)

    n_used = group_end[-1] // tm_e
    tile_start = jnp.arange(n_row_tiles, dtype=jnp.int32) * tm_e
    tile_expert = jnp.sum(tile_start[:, None] >= group_end[None, :], axis=1).astype(jnp.int32)
    tile_expert = jnp.minimum(tile_expert, n_experts - 1)
    tile_expert = jnp.where(tile_start < group_end[-1], tile_expert, tile_expert[n_used - 1])
    meta = jnp.stack([n_used, n_used]).astype(jnp.int32)

    idx = pos[:, None, :] + (jnp.arange(rows, dtype=jnp.int32) * p)[None, :, None]
    xs = _sc_scatter_rows(tp.reshape(rows * t, PACK_W), idx.reshape(TOP_K, rows * t), rows * p)
    after = yield wgt
    if after is not None:
        xs = _after(xs, after)
    ys = _experts(xs.reshape(rows, p, PACK_W), tile_expert, meta, w_gate_e, w_up_e, w_down_e, layer, tm=tm_e)
    yk = _sc_gather_rows(ys.reshape(rows * p, PACK_W), idx.reshape(-1))
    yield ys
    yield _combine(x1, mod, seq_len, lw, yk.reshape(TOP_K, rows, t, PACK_W), wgt, tm=tm)


def _rope_tables(n):
    rows = n // GRID_W
    r, col = jnp.meshgrid(jnp.arange(rows), jnp.arange(GRID_W), indexing="ij")
    r = r.reshape(-1).astype(F32)
    col = col.reshape(-1).astype(F32)
    pairs = QK_ROPE // 4
    inv = ROPE_BASE ** (-jnp.arange(pairs, dtype=F32) / pairs)
    ang = jnp.concatenate([r[:, None] * inv, col[:, None] * inv], axis=-1)
    cos, sin = jnp.cos(ang), jnp.sin(ang)
    pad = HEAD_PAD - QK_NOPE - QK_ROPE
    cos_t = jnp.concatenate([jnp.ones((n, QK_NOPE), F32), cos, cos, jnp.zeros((n, pad), F32)], axis=1)
    sin_t = jnp.concatenate([jnp.zeros((n, QK_NOPE), F32), sin, sin, jnp.zeros((n, pad), F32)], axis=1)
    return cos_t, sin_t


def _identity_tables(n):
    pad = HEAD_PAD - QK_NOPE - QK_ROPE
    cos_t = jnp.concatenate([jnp.ones((n, QK_NOPE + QK_ROPE), F32), jnp.zeros((n, pad), F32)], axis=1)
    return cos_t, jnp.zeros((n, HEAD_PAD), F32)


def _position_dft(n):
    nb = 64 if n % 64 == 0 else 1
    na = n // nb
    m = jnp.arange(n, dtype=jnp.int32)[None, :]
    ang_a = ((jnp.arange(na, dtype=jnp.int32)[:, None] * m) % na).astype(F32) * (2.0 * np.pi / na)
    ang_b = ((jnp.arange(nb, dtype=jnp.int32)[:, None] * m) % n).astype(F32) * (2.0 * np.pi / n)
    ca, sa = jnp.cos(ang_a)[:, None, :], jnp.sin(ang_a)[:, None, :]
    cb, sb = jnp.cos(ang_b)[None, :, :], jnp.sin(ang_b)[None, :, :]
    norm = 1.0 / np.sqrt(n)
    cos = ((ca * cb - sa * sb) * norm).reshape(n, n)
    sin = ((sa * cb + ca * sb) * (-norm)).reshape(n, n)
    return jnp.concatenate([cos, sin], axis=1).astype(BF16)


def _channel_dft(width):
    gc = width // FOURIER_GROUPS
    idx = (jnp.arange(gc, dtype=jnp.int32)[:, None] * jnp.arange(gc, dtype=jnp.int32)[None, :]) % gc
    ang = idx.astype(F32) * (2.0 * np.pi / gc)
    eye = jnp.eye(FOURIER_GROUPS, dtype=F32)
    norm = 1.0 / np.sqrt(gc)
    return jnp.concatenate([jnp.kron(eye, jnp.cos(ang) * norm), jnp.kron(eye, jnp.sin(ang) * norm)],
                           axis=1).astype(BF16)


def _w1_kernel(w_ref, o_ref, *, kv_rank):
    cols = w_ref.shape[2]
    kv_end = kv_rank + QK_ROPE
    half = QK_ROPE // 2
    tail = HEAD_PAD - QK_NOPE - QK_ROPE
    o_kpe = kv_rank
    o_rot = o_kpe + HEAD_PAD
    o_rest = o_rot + HEAD_PAD
    dt = o_ref.dtype
    o_ref[0, 0:kv_rank, :] = w_ref[0, 0:kv_rank, :].astype(dt)
    for base in (o_kpe, o_rot):
        o_ref[0, base:base + QK_NOPE, :] = jnp.zeros((QK_NOPE, cols), dt)
        o_ref[0, base + QK_NOPE + QK_ROPE:base + HEAD_PAD, :] = jnp.zeros((tail, cols), dt)
    o_ref[0, o_kpe + QK_NOPE:o_kpe + QK_NOPE + QK_ROPE, :] = w_ref[0, kv_rank:kv_end, :].astype(dt)
    o_ref[0, o_rot + QK_NOPE:o_rot + QK_NOPE + half, :] = (-w_ref[0, kv_rank + half:kv_end, :]).astype(dt)
    o_ref[0, o_rot + QK_NOPE + half:o_rot + QK_NOPE + QK_ROPE, :] = w_ref[0, kv_rank:kv_rank + half, :].astype(dt)
    o_ref[0, o_rest:, :] = w_ref[0, kv_end:, :].astype(dt)


def _prep_w1(w_in_t, kv_rank):
    n_layers, width, d = w_in_t.shape
    out_w = width - QK_ROPE + 2 * HEAD_PAD
    tc = _tile(d, 256)
    return pl.pallas_call(
        functools.partial(_w1_kernel, kv_rank=kv_rank),
        grid=(n_layers, d // tc),
        in_specs=[pl.BlockSpec((1, width, tc), lambda l, i: (l, 0, i))],
        out_specs=pl.BlockSpec((1, out_w, tc), lambda l, i: (l, 0, i)),
        out_shape=jax.ShapeDtypeStruct((n_layers, out_w, d), BF16),
        compiler_params=_params(2),
        name="prep_w1",
    )(w_in_t)


def _stacked_weights(g_pre_mix, g_post_mix, g_pre_ffn, g_post_ffn, b_gate, g_q, w_uq, g_kv, w_ukv,
                     w_mla_out, conv_w, w_conv_out, w_four_out, w_out, w_router, b_router, w_gate_s,
                     w_up_s, w_down_s):
    n_layers, kv_rank = g_kv.shape
    q_rank = g_q.shape[1]
    qk_dim = QK_NOPE + QK_ROPE
    uq = w_uq.reshape(n_layers, q_rank, N_HEADS, qk_dim) * (qk_dim ** -0.5 * np.log2(np.e))
    zq = jnp.zeros((n_layers, q_rank, N_HEADS, HEAD_PAD - qk_dim), F32)
    wuq = jnp.concatenate([uq, zq], axis=-1).reshape(n_layers, q_rank, N_HEADS * HEAD_PAD).astype(BF16)
    ukv = w_ukv.reshape(n_layers, kv_rank, N_HEADS, QK_NOPE + V_DIM)
    zk = jnp.zeros((n_layers, kv_rank, N_HEADS, HEAD_PAD - QK_NOPE), F32)
    zv = jnp.zeros((n_layers, kv_rank, N_HEADS, HEAD_PAD - V_DIM), F32)
    wuk = jnp.concatenate([ukv[..., :QK_NOPE], zk], axis=-1).reshape(n_layers, kv_rank, -1).astype(BF16)
    wuv = jnp.concatenate([ukv[..., QK_NOPE:], zv], axis=-1).reshape(n_layers, kv_rank, -1).astype(BF16)
    wr_hi = w_router.astype(BF16)
    wr_lo = (w_router - wr_hi.astype(F32)).astype(BF16)
    wr_t = jnp.concatenate([jnp.swapaxes(wr_hi, 1, 2), jnp.swapaxes(wr_lo, 1, 2)], axis=1)
    return {
        "g_pre_mix": g_pre_mix[:, None], "g_post_mix": g_post_mix[:, None],
        "g_pre_ffn": g_pre_ffn[:, None], "g_post_ffn": g_post_ffn[:, None],
        "b_gate": b_gate[:, None], "g_q": g_q[:, None], "g_kv": g_kv[:, None],
        "wuq": wuq, "wuk": wuk, "wuv": wuv, "conv_w": conv_w,
        "w_mla_out": w_mla_out.astype(BF16), "w_conv_out": w_conv_out.astype(BF16),
        "w_four_out": w_four_out.astype(BF16), "w_out": w_out.astype(BF16),
        "w_router_t": wr_t, "b_router_t": b_router[:, :, None],
        "w_gate_s": w_gate_s.astype(BF16), "w_up_s": w_up_s.astype(BF16), "w_down_s": w_down_s.astype(BF16),
    }


def _tile(n, pref):
    return pref if n % pref == 0 else n


def kernel(x, c, ctx, c_ctx, w_ada, b_ada, g_pre_mix, g_post_mix, g_pre_ffn, g_post_ffn, w_in, b_gate,
           g_q, w_uq, g_kv, w_ukv, w_mla_out, conv_w, w_conv_out, w_four_out, w_out, w_router, b_router,
           w_gate_e, w_up_e, w_down_e, w_gate_s, w_up_s, w_down_s):
    batch, seq, d = x.shape
    n_ctx = ctx.shape[1]
    n_layers = w_in.shape[0]
    xs = x.reshape(batch * seq, d)
    cs = ctx.reshape(batch * n_ctx, d)

    mod_rows = -(-(batch + 1) // (2 * SUBLANE)) * (2 * SUBLANE)
    c_all = jnp.concatenate([c, c_ctx[None], jnp.zeros((mod_rows - batch - 1, d), F32)], axis=0)
    ada = _ada(c_all, w_ada, b_ada).reshape(n_layers, mod_rows, 6, d)

    tm_x = _tile(seq, 512)
    tm_x_wide = _tile(seq, 1024)
    tq_x = _tile(seq, 2 * ATTN_ROWS)
    tm_c = _tile(n_ctx, 256)
    tm_c_mid = _tile(batch * n_ctx, 512)
    tm_c_wide = _tile(batch * n_ctx, 1024)
    te_x, te_c = 1024, 512

    tab_x = _rope_tables(seq)
    tab_c = _identity_tables(tm_c_mid)
    cs_x = _position_dft(seq)
    cs_c = _position_dft(n_ctx)
    dc = _channel_dft(w_four_out.shape[1])
    w1 = _prep_w1(jnp.swapaxes(w_in, 1, 2), g_kv.shape[1])
    stacked = _stacked_weights(g_pre_mix, g_post_mix, g_pre_ffn, g_post_ffn, b_gate, g_q, w_uq, g_kv, w_ukv,
                               w_mla_out, conv_w, w_conv_out, w_four_out, w_out, w_router, b_router,
                               w_gate_s, w_up_s, w_down_s)
    v_one = jnp.tile((jnp.arange(HEAD_PAD) == V_DIM).astype(F32), N_HEADS)[None]

    for l in range(n_layers):
        last = l == n_layers - 1
        lw = {name: _Layer(arr, l) for name, arr in stacked.items()}
        lw.update(w1=w1, layer=l, dc=dc, v_one=v_one)
        mod_x = _Mod(ada, l, 0, batch)
        mod_c = _Mod(ada, l, batch, 1)

        pc = _inproj(cs, mod_c, n_ctx, lw, tab_c, kv_only=last, tm=tm_c_mid)
        px = _inproj(xs, mod_x, seq, lw, tab_x, kv_only=False, tm=tm_x)
        o_x = _attention(px["q"], [(pc["k"], pc["v"], n_ctx), (px["k"], px["v"], seq)], batch, seq, tq=tq_x)
        f_x = _fourier(px["ab"], cs_x, batch, seq, tn=seq)
        x1, routed_x = _merge(xs, mod_x, seq, px, o_x, f_x, lw, tm=tm_x)
        moe_x = _moe_sparse(x1, routed_x, mod_x, seq, lw, w_gate_e, w_up_e, w_down_e, l, tm=tm_x_wide,
                            tm_e=te_x)
        if last:
            xs = list(moe_x)[-1]
        else:
            q_c = _after(pc["q"], next(moe_x))
            o_c = _attention(q_c, [(pc["k"], pc["v"], n_ctx)], batch, n_ctx, tq=tm_c)
            f_c = _fourier(pc["ab"], cs_c, batch, n_ctx, tn=tm_c)
            c1, routed_c = _merge(cs, mod_c, n_ctx, pc, o_c, f_c, lw, tm=tm_c_mid)
            moe_c = _moe_sparse(c1, routed_c, mod_c, batch * n_ctx, lw, w_gate_e, w_up_e, w_down_e, l,
                                tm=tm_c_wide, tm_e=te_c)
            next(moe_c)
            experts_x = moe_x.send(None)
            moe_c.send(experts_x)
            xs = next(moe_x)
            cs = next(moe_c)
    return xs.reshape(batch, seq, d)
```

```python
import functools

import numpy as np
import jax
import jax.numpy as jnp
from jax import lax
from jax.experimental import pallas as pl
from jax.experimental.pallas import tpu as pltpu
from jax.experimental.pallas import tpu_sc as plsc

N_HEADS = 8
QK_NOPE = 64
QK_ROPE = 32
V_DIM = 64
GRID_W = 64
ROPE_BASE = 10000.0
FOURIER_GROUPS = 4
TOP_K = 4
ROUTED_SCALE = 2.5
N_BRANCHES = 3
EPS = 1e-6

LANE = 128
SUBLANE = 8
HEAD_PAD = LANE
VMEM_LIMIT = 56 * 1024 * 1024
PACK_W = 256
SC_WINDOW = 128
ATTN_ROWS = 512
MERGE_ROWS = 256

F32 = jnp.float32
BF16 = jnp.bfloat16


def _rms(x, g):
    return x * lax.rsqrt(jnp.mean(x * x, axis=-1, keepdims=True) + EPS) * g


def _sigmoid(x):
    return 1.0 / (1.0 + jnp.exp(-x))


def _dot(a, b):
    return jnp.dot(a, b, preferred_element_type=F32)


def _dot_t(a, b_t):
    return lax.dot_general(a, b_t, (((1,), (1,)), ((), ())), preferred_element_type=F32)


def _skewed(stages, n_chunks, chunk):
    live = [None] * n_chunks
    for step in range(n_chunks + len(stages) - 1):
        for c in range(n_chunks):
            s = step - c
            if 0 <= s < len(stages):
                live[c] = stages[s](c * chunk) if s == 0 else stages[s](c * chunk, live[c])


def _after(value, dependency):
    return lax.optimization_barrier((value, dependency))[0]


def _resident(shape):
    nd = len(shape)
    return pl.BlockSpec(shape, lambda *_: (0,) * nd, pipeline_mode=pl.Buffered(1))


class _Layer:
    def __init__(self, stacked, index):
        self.stacked, self.index = stacked, index
        self.shape = stacked.shape[1:]

    def spec(self):
        index = (self.index,) + (0,) * len(self.shape)
        return pl.BlockSpec((None,) + self.shape, lambda *_: index, pipeline_mode=pl.Buffered(1))


class _Mod:
    def __init__(self, table, layer, row0, n):
        self.table, self.layer, self.row0, self.n = table, layer, row0, n

    def spec(self, seq_of_step):
        layer, row0, n = self.layer, self.row0, self.n
        return pl.BlockSpec((None, 1) + self.table.shape[2:],
                            lambda i: (layer, row0 + seq_of_step(i) % n, 0, 0))


def _params(n_grid):
    return pltpu.CompilerParams(dimension_semantics=("arbitrary",) * n_grid,
                                vmem_limit_bytes=VMEM_LIMIT)


def _ada_kernel(c_ref, w_ref, b_ref, o_ref):
    c = c_ref[...]
    a = (c * _sigmoid(c)).astype(BF16)
    o_ref[0] = _dot(a, w_ref[0].astype(BF16)) + b_ref[0]


def _ada(c_all, w_ada, b_ada):
    n_layers, d, n_out = w_ada.shape
    rows = c_all.shape[0]
    tn = _tile(n_out, 1536)
    return pl.pallas_call(
        _ada_kernel,
        grid=(n_layers, n_out // tn),
        in_specs=[
            pl.BlockSpec((rows, d), lambda l, j: (0, 0)),
            pl.BlockSpec((1, d, tn), lambda l, j: (l, 0, j)),
            pl.BlockSpec((1, 1, tn), lambda l, j: (l, 0, j)),
        ],
        out_specs=pl.BlockSpec((1, rows, tn), lambda l, j: (l, 0, j)),
        out_shape=jax.ShapeDtypeStruct((n_layers, rows, n_out), F32),
        compiler_params=_params(2),
        name="ada",
    )(c_all, w_ada, b_ada.reshape(n_layers, 1, n_out))


def _inproj_kernel(*refs, kv_only, kv_rank, q_rank, conv_w, four_w, d_model):
    if kv_only:
        (x_ref, mod_ref, gpre_ref, w1_ref, gkv_ref, wuk_ref, wuv_ref, vone_ref, cos_ref, sin_ref,
         k_ref, v_ref) = refs
    else:
        (x_ref, mod_ref, gpre_ref, w1_ref, gkv_ref, wuk_ref, wuv_ref, vone_ref, cos_ref, sin_ref,
         bg_ref, gq_ref, wuq_ref, dc_ref,
         k_ref, v_ref, q_ref, cb_ref, cc_ref, cu_ref, ab_ref, gate_ref) = refs

    x = x_ref[...]
    shift = mod_ref[0, 0:1, :]
    scale = mod_ref[0, 1:2, :]
    h = (_rms(x, gpre_ref[...]) * (1.0 + scale) + shift).astype(BF16)
    cos = cos_ref[...]
    sin = sin_ref[...]

    o_kpe = kv_rank
    o_rot = o_kpe + HEAD_PAD
    o_q = o_rot + HEAD_PAD
    p = _dot_t(h, w1_ref[0, 0:o_q, :])
    ckv = _rms(p[:, 0:kv_rank], gkv_ref[...]).astype(BF16)
    kpe = p[:, o_kpe:o_rot] * cos + p[:, o_rot:o_q] * sin
    k = _dot(ckv, wuk_ref[...]) + jnp.concatenate([kpe] * N_HEADS, axis=1)
    k_ref[...] = k.astype(k_ref.dtype)
    v_ref[...] = (_dot(ckv, wuv_ref[...]) + vone_ref[...]).astype(v_ref.dtype)
    if kv_only:
        return

    o_cb = o_q + q_rank
    cq = _rms(_dot_t(h, w1_ref[0, o_q:o_cb, :]), gq_ref[...]).astype(BF16)
    half = QK_ROPE // 2
    lane = lax.broadcasted_iota(jnp.int32, (1, HEAD_PAD), 1)
    first = (lane >= QK_NOPE) & (lane < QK_NOPE + half)
    cos_h = jnp.concatenate([cos] * N_HEADS, axis=1)
    sin_h = jnp.concatenate([jnp.where(first, -sin, sin)] * N_HEADS, axis=1)
    first_h = jnp.concatenate([first] * N_HEADS, axis=1)
    lin = _dot(cq, wuq_ref[...])
    width = lin.shape[1]
    partner = jnp.where(first_h, pltpu.roll(lin, width - half, 1), pltpu.roll(lin, half, 1))
    q_ref[...] = (lin * cos_h + partner * sin_h).astype(q_ref.dtype)

    o_cc = o_cb + conv_w
    o_cu = o_cc + conv_w
    o_four = o_cu + conv_w
    cb_ref[...] = _dot_t(h, w1_ref[0, o_cb:o_cc, :]).astype(cb_ref.dtype)
    cc_ref[...] = _dot_t(h, w1_ref[0, o_cc:o_cu, :]).astype(cc_ref.dtype)
    cu_ref[...] = _dot_t(h, w1_ref[0, o_cu:o_four, :]).astype(cu_ref.dtype)

    o_gate = o_four + four_w
    uf = _dot_t(h, w1_ref[0, o_four:o_gate, :]).astype(BF16)
    ab_ref[...] = _dot(uf, dc_ref[...]).astype(ab_ref.dtype)

    for j in range(N_BRANCHES):
        lo = o_gate + j * d_model
        z = _dot_t(h, w1_ref[0, lo:lo + d_model, :]) + bg_ref[:, j * d_model:(j + 1) * d_model]
        gate_ref[:, j * d_model:(j + 1) * d_model] = _sigmoid(z).astype(gate_ref.dtype)


def _inproj(xs, mod, seq_len, lw, tables, *, kv_only, tm):
    t, d = xs.shape
    cos_t, sin_t = tables
    table_tiles = cos_t.shape[0] // tm
    kv_rank = lw["g_kv"].shape[1]
    q_rank = lw["g_q"].shape[1]
    conv_w = lw["conv_w"].shape[1]
    four_w = lw["dc"].shape[0]
    n_k = N_HEADS * HEAD_PAD

    def row(i):
        return (i, 0)

    def tab_map(i):
        return (i % table_tiles, 0)

    w1, layer = lw["w1"], lw["layer"]
    w1_rows = kv_rank + 2 * HEAD_PAD if kv_only else w1.shape[1]
    in_specs = [
        pl.BlockSpec((tm, d), row),
        mod.spec(lambda i: i * tm // seq_len),
        lw["g_pre_mix"].spec(),
        pl.BlockSpec((1, w1_rows, d), lambda i: (layer, 0, 0), pipeline_mode=pl.Buffered(1)),
        lw["g_kv"].spec(),
        lw["wuk"].spec(),
        lw["wuv"].spec(),
        _resident(lw["v_one"].shape),
        pl.BlockSpec((tm, HEAD_PAD), tab_map),
        pl.BlockSpec((tm, HEAD_PAD), tab_map),
    ]
    args = [xs, mod.table, lw["g_pre_mix"].stacked, w1, lw["g_kv"].stacked, lw["wuk"].stacked, lw["wuv"].stacked,
            lw["v_one"], cos_t, sin_t]
    out_shape = [jax.ShapeDtypeStruct((t, n_k), BF16), jax.ShapeDtypeStruct((t, n_k), BF16)]
    out_specs = [pl.BlockSpec((tm, n_k), row), pl.BlockSpec((tm, n_k), row)]
    if not kv_only:
        in_specs += [
            lw["b_gate"].spec(),
            lw["g_q"].spec(),
            lw["wuq"].spec(),
            _resident(lw["dc"].shape),
        ]
        args += [lw["b_gate"].stacked, lw["g_q"].stacked, lw["wuq"].stacked, lw["dc"]]
        widths = [n_k, conv_w, conv_w, conv_w, 2 * four_w, N_BRANCHES * d]
        out_shape += [jax.ShapeDtypeStruct((t, w), BF16) for w in widths]
        out_specs += [pl.BlockSpec((tm, w), row) for w in widths]
    outs = pl.pallas_call(
        functools.partial(_inproj_kernel, kv_only=kv_only, kv_rank=kv_rank, q_rank=q_rank,
                          conv_w=conv_w, four_w=four_w, d_model=d),
        grid=(t // tm,),
        in_specs=in_specs,
        out_specs=out_specs,
        out_shape=out_shape,
        compiler_params=_params(1),
        name="inproj_kv" if kv_only else "inproj",
    )(*args)
    names = ["k", "v", "q", "cb", "cc", "cu", "ab", "gate"]
    return dict(zip(names, outs))


def _attn_kernel(*refs, n_seg):
    q_ref = refs[0]
    o_ref = refs[-1]
    chunk = min(q_ref.shape[0], ATTN_ROWS)
    for c in range(q_ref.shape[0] // chunk):
        rows = slice(c * chunk, (c + 1) * chunk)
        outs = []
        for hh in range(N_HEADS):
            head = slice(hh * HEAD_PAD, (hh + 1) * HEAD_PAD)
            qh = q_ref[rows, head]
            s = [_dot_t(qh, refs[1 + 2 * i][:, head]) for i in range(n_seg)]
            m = functools.reduce(jnp.maximum, [jnp.max(si, axis=-1, keepdims=True) for si in s])
            acc = functools.reduce(jnp.add, [
                _dot(jnp.exp2((s[i] - m).astype(BF16)), refs[2 + 2 * i][:, head]) for i in range(n_seg)])
            outs.append(acc[:, 0:V_DIM] / acc[:, V_DIM:V_DIM + 1])
        o_ref[rows, :] = jnp.concatenate(outs, axis=1).astype(o_ref.dtype)


def _attention(q, segs, batch, seq_q, *, tq):
    t = q.shape[0]
    qt = seq_q // tq
    n_k = N_HEADS * HEAD_PAD
    in_specs = [pl.BlockSpec((tq, n_k), lambda b, j: (b * qt + j, 0))]
    args = [q]
    for k, v, m in segs:
        in_specs.append(pl.BlockSpec((m, n_k), lambda b, j: (b, 0)))
        in_specs.append(pl.BlockSpec((m, n_k), lambda b, j: (b, 0)))
        args += [k, v]
    return pl.pallas_call(
        functools.partial(_attn_kernel, n_seg=len(segs)),
        grid=(batch, qt),
        in_specs=in_specs,
        out_specs=pl.BlockSpec((tq, N_HEADS * V_DIM), lambda b, j: (b * qt + j, 0)),
        out_shape=jax.ShapeDtypeStruct((t, N_HEADS * V_DIM), BF16),
        compiler_params=_params(2),
        name="attention",
    )(*args)


def _four_kernel(cs_ref, ab_ref, o_ref, *, n, fw):
    o = _dot(cs_ref[:, 0:n], ab_ref[:, 0:fw]) + _dot(cs_ref[:, n:2 * n], ab_ref[:, fw:2 * fw])
    o_ref[...] = o.astype(o_ref.dtype)


def _fourier(ab, cs, batch, seq_len, *, tn):
    t, fw2 = ab.shape
    fw = fw2 // 2
    nt = seq_len // tn
    cs_spec = (_resident(cs.shape) if nt == 1
               else pl.BlockSpec((tn, 2 * seq_len), lambda b, j: (j, 0)))
    return pl.pallas_call(
        functools.partial(_four_kernel, n=seq_len, fw=fw),
        grid=(batch, nt),
        in_specs=[
            cs_spec,
            pl.BlockSpec((seq_len, fw2), lambda b, j: (b, 0)),
        ],
        out_specs=pl.BlockSpec((tn, fw), lambda b, j: (b * nt + j, 0)),
        out_shape=jax.ShapeDtypeStruct((t, fw), BF16),
        compiler_params=_params(2),
        name="fourier",
    )(cs, ab)


def _merge_kernel(x_ref, mod_ref, o_ref, cb_ref, cc_ref, cu_ref, ccp_ref, cup_ref, ccn_ref,
                  cun_ref, f_ref, gate_ref, convw_ref, wmo_ref, wco_ref, wfo_ref, wout_ref,
                  gpost_ref, gffn_ref, wrt_ref, brt_ref, triu_ref,
                  out_ref, tp_ref, sel_ref, rank_ref, cnt_ref, wgt_ref, pad_ref,
                  *, seq_len, tm, d_model, n_experts):
    i = pl.program_id(0)
    has_prev = ((i * tm) % seq_len != 0).astype(F32)
    has_next = (((i + 1) * tm) % seq_len != 0).astype(F32)
    lo = SUBLANE
    pad_ref[0:lo, :] = ccp_ref[...].astype(F32) * cup_ref[...].astype(F32) * has_prev
    pad_ref[lo:lo + tm, :] = cc_ref[...].astype(F32) * cu_ref[...].astype(F32)
    pad_ref[lo + tm:2 * lo + tm, :] = ccn_ref[...].astype(F32) * cun_ref[...].astype(F32) * has_next
    d = d_model
    g1 = mod_ref[0, 2:3, :]
    chunk = min(tm, MERGE_ROWS)

    def conv_stage(r0):
        before = pad_ref[lo - 1 + r0:lo - 1 + r0 + chunk, :]
        after = pad_ref[lo + 1 + r0:lo + 1 + r0 + chunk, :]
        if tm > seq_len:
            at = (lax.broadcasted_iota(jnp.int32, (chunk, 1), 0) + r0) % seq_len
            before = jnp.where(at == 0, 0.0, before)
            after = jnp.where(at == seq_len - 1, 0.0, after)
        conv = (before * convw_ref[0:1, :] + pad_ref[lo + r0:lo + r0 + chunk, :] * convw_ref[1:2, :]
                + after * convw_ref[2:3, :])
        return (cb_ref[r0:r0 + chunk, :].astype(F32) * conv).astype(BF16)

    def branch_stage(r0, conv_in):
        return (_dot(o_ref[r0:r0 + chunk, :], wmo_ref[...]), _dot(conv_in, wco_ref[...]),
                _dot(f_ref[r0:r0 + chunk, :], wfo_ref[...]))

    def gate_stage(r0, ys):
        rows = slice(r0, r0 + chunk)
        return (gate_ref[rows, 0:d].astype(F32) * ys[0] + gate_ref[rows, d:2 * d].astype(F32) * ys[1]
                + gate_ref[rows, 2 * d:3 * d].astype(F32) * ys[2]).astype(BF16)

    def out_stage(r0, merged):
        return _dot(merged, wout_ref[...])

    def tail_stage(r0, y):
        rows = slice(r0, r0 + chunk)
        out_ref[rows, :] = x_ref[rows, :] + g1 * _rms(y, gpost_ref[...])

    _skewed((conv_stage, branch_stage, gate_stage, out_stage, tail_stage), tm // chunk, chunk)
    _route_tile(out_ref[...], mod_ref, gffn_ref, wrt_ref, brt_ref, triu_ref,
                tp_ref, sel_ref, rank_ref, cnt_ref, wgt_ref, n_experts)


def _merge(xs, mod, seq_len, pr, o, four, lw, *, tm):
    t, d = xs.shape
    assert tm % seq_len == 0 or seq_len % tm == 0
    cw = lw["conv_w"].shape[1]
    fw = four.shape[1]
    hb = tm // SUBLANE
    last_hb = t // SUBLANE - 1

    def row(i):
        return (i, 0)

    def prev_map(i):
        return (jnp.maximum(i * hb - 1, 0), 0)

    def next_map(i):
        return (jnp.minimum((i + 1) * hb, last_hb), 0)

    in_specs = [
        pl.BlockSpec((tm, d), row),
        mod.spec(lambda i: i * tm // seq_len),
        pl.BlockSpec((tm, o.shape[1]), row),
        pl.BlockSpec((tm, cw), row),
        pl.BlockSpec((tm, cw), row),
        pl.BlockSpec((tm, cw), row),
        pl.BlockSpec((SUBLANE, cw), prev_map),
        pl.BlockSpec((SUBLANE, cw), prev_map),
        pl.BlockSpec((SUBLANE, cw), next_map),
        pl.BlockSpec((SUBLANE, cw), next_map),
        pl.BlockSpec((tm, fw), row),
        pl.BlockSpec((tm, N_BRANCHES * d), row),
        lw["conv_w"].spec(),
        lw["w_mla_out"].spec(),
        lw["w_conv_out"].spec(),
        lw["w_four_out"].spec(),
        lw["w_out"].spec(),
        lw["g_post_mix"].spec(),
        lw["g_pre_ffn"].spec(),
        lw["w_router_t"].spec(),
        lw["b_router_t"].spec(),
        _resident((tm, tm)),
    ]
    n_tiles = t // tm
    n_experts = lw["b_router_t"].shape[0]
    rows = d // (2 * PACK_W)
    triu = jnp.tri(tm, tm, -1, dtype=BF16).T
    tile3 = lambda i: (i, 0, 0)
    outs = pl.pallas_call(
        functools.partial(_merge_kernel, seq_len=seq_len, tm=tm, d_model=d, n_experts=n_experts),
        grid=(n_tiles,),
        in_specs=in_specs,
        out_specs=[
            pl.BlockSpec((tm, d), row),
            pl.BlockSpec((rows, tm, PACK_W), lambda i: (0, i, 0)),
            pl.BlockSpec((1, SUBLANE, tm), tile3),
            pl.BlockSpec((1, SUBLANE, tm), tile3),
            pl.BlockSpec((1, LANE, LANE), tile3),
            pl.BlockSpec((tm, LANE), row),
        ],
        out_shape=[
            jax.ShapeDtypeStruct((t, d), F32),
            jax.ShapeDtypeStruct((rows, t, PACK_W), jnp.int32),
            jax.ShapeDtypeStruct((n_tiles, SUBLANE, tm), jnp.int32),
            jax.ShapeDtypeStruct((n_tiles, SUBLANE, tm), jnp.int32),
            jax.ShapeDtypeStruct((n_tiles, LANE, LANE), jnp.int32),
            jax.ShapeDtypeStruct((t, LANE), F32),
        ],
        scratch_shapes=[pltpu.VMEM((tm + 2 * SUBLANE, cw), F32)],
        compiler_params=_params(1),
        name="merge",
    )(xs, mod.table, o, pr["cb"], pr["cc"], pr["cu"], pr["cc"], pr["cu"], pr["cc"], pr["cu"], four,
      pr["gate"], lw["conv_w"].stacked, lw["w_mla_out"].stacked, lw["w_conv_out"].stacked,
      lw["w_four_out"].stacked, lw["w_out"].stacked, lw["g_post_mix"].stacked,
      lw["g_pre_ffn"].stacked, lw["w_router_t"].stacked, lw["b_router_t"].stacked, triu)
    return outs[0], outs[1:]


def _pack_rows(v):
    bits = lax.bitcast_convert_type(v.astype(BF16).astype(F32), jnp.uint32)
    rows = []
    for j in range(v.shape[1] // (2 * PACK_W)):
        lo = bits[:, (2 * j) * PACK_W:(2 * j + 1) * PACK_W]
        hi = bits[:, (2 * j + 1) * PACK_W:(2 * j + 2) * PACK_W]
        rows.append(lax.bitcast_convert_type((hi & jnp.uint32(0xFFFF0000)) | (lo >> 16), jnp.int32))
    return rows


def _unpack_rows(rows):
    parts = []
    for r in rows:
        u = lax.bitcast_convert_type(r, jnp.uint32)
        parts.append(lax.bitcast_convert_type(u << 16, F32))
        parts.append(lax.bitcast_convert_type(u & jnp.uint32(0xFFFF0000), F32))
    return jnp.concatenate(parts, axis=1)


def _route_tile(x, mod_ref, gpre_ref, wrt_ref, brt_ref, triu_ref,
                tp_ref, sel_ref, rank_ref, cnt_ref, wgt_ref, n_experts):
    shift = mod_ref[0, 3:4, :]
    scale = mod_ref[0, 4:5, :]
    t = _rms(x, gpre_ref[...]) * (1.0 + scale) + shift
    t_hi = t.astype(BF16)
    t_lo = (t - t_hi.astype(F32)).astype(BF16)
    for j, r in enumerate(_pack_rows(t)):
        tp_ref[j] = r
    tm = t.shape[0]
    hh = _dot_t(wrt_ref[...], t_hi)
    logits = hh[0:n_experts] + hh[n_experts:2 * n_experts] + _dot_t(wrt_ref[0:n_experts, :], t_lo)
    scores = _sigmoid(logits)
    work = scores + brt_ref[...]
    row = lax.broadcasted_iota(jnp.int32, scores.shape, 0)
    wide = lax.broadcasted_iota(jnp.int32, (LANE, tm), 0)
    firsts, picked, hits = [], [], []
    for k in range(TOP_K):
        best = jnp.max(work, axis=0, keepdims=True)
        first = jnp.min(jnp.where(work == best, row, n_experts), axis=0, keepdims=True)
        hit = row == first
        firsts.append(first)
        picked.append(jnp.sum(jnp.where(hit, scores, 0.0), axis=0, keepdims=True))
        hits.append(wide == first + k * n_experts)
        work = jnp.where(hit, -jnp.inf, work)
    total = functools.reduce(jnp.add, picked)
    onehot = functools.reduce(jnp.add, [jnp.where(h, 1.0, 0.0) for h in hits])
    earlier = _dot(onehot.astype(BF16), triu_ref[...])
    col = jnp.broadcast_to(jnp.sum(onehot, axis=1, keepdims=True), (LANE, LANE))
    row_c = lax.broadcasted_iota(jnp.int32, (LANE, LANE), 0)
    before = jnp.zeros((LANE, LANE), F32)
    for s in range(1, TOP_K):
        before = before + jnp.where(row_c >= s * n_experts, pltpu.roll(col, s * n_experts, 0), 0.0)
    ahead = earlier + before[:, 0:1]
    row8 = lax.broadcasted_iota(jnp.int32, (SUBLANE, tm), 0)
    sel = jnp.zeros((SUBLANE, tm), jnp.int32)
    rank = jnp.zeros((SUBLANE, tm), F32)
    wgt_t = jnp.zeros((LANE, tm), F32)
    for k in range(TOP_K):
        sel = jnp.where(row8 == k, firsts[k], sel)
        rank = jnp.where(row8 == k, jnp.sum(jnp.where(hits[k], ahead, 0.0), axis=0, keepdims=True), rank)
        wgt_t = jnp.where(wide == k, picked[k] / total * ROUTED_SCALE, wgt_t)
    sel_ref[0] = sel
    rank_ref[0] = rank.astype(jnp.int32)
    cnt_ref[0] = col.astype(jnp.int32)
    wgt_ref[...] = wgt_t.T


def _expert_kernel(te_ref, meta_ref, nxt_ref, par_ref, xs_ref, wg_hbm, wu_hbm, wd_hbm, ys_ref,
                   wg_sc, wu_sc, wd_sc, wg_buf, wu_buf, wd_buf, sems, *, layer):
    i = pl.program_id(0)
    live = i < meta_ref[0]
    new_expert = (i == 0) | (te_ref[i] != te_ref[jnp.maximum(i - 1, 0)])

    def fetch(expert, slot):
        return [pltpu.make_async_copy(hbm.at[layer, expert], buf.at[slot], sems.at[slot, m])
                for m, (hbm, buf) in enumerate(((wg_hbm, wg_buf), (wu_hbm, wu_buf), (wd_hbm, wd_buf)))]

    @pl.when(live & (i == 0))
    def _():
        for copy in fetch(te_ref[0], par_ref[0]):
            copy.start()

    @pl.when(live & new_expert)
    def _():
        slot = par_ref[i]
        for copy in fetch(te_ref[i], slot):
            copy.wait()
        wg_sc[...] = wg_buf[slot].astype(BF16)
        wu_sc[...] = wu_buf[slot].astype(BF16)
        wd_sc[...] = wd_buf[slot].astype(BF16)

        @pl.when(nxt_ref[i] >= 0)
        def _():
            for copy in fetch(nxt_ref[i], 1 - slot):
                copy.start(priority=1)

    @pl.when(live)
    def _():
        x = _unpack_rows([xs_ref[j] for j in range(xs_ref.shape[0])]).astype(BF16)
        gate = _dot(x, wg_sc[...])
        act = (gate * _sigmoid(gate) * _dot(x, wu_sc[...])).astype(BF16)
        for j, r in enumerate(_pack_rows(_dot(act, wd_sc[...]))):
            ys_ref[j] = r


def _experts(xs, tile_expert, meta, w_gate_e, w_up_e, w_down_e, layer, *, tm):
    rows, p, _ = xs.shape
    n_experts, d, f = w_gate_e.shape[1:]

    def slot(i, te, meta, nxt, par):
        return (0, jnp.minimum(i, meta[0] - 1), 0)

    n_tiles = p // tm
    tile = jnp.arange(n_tiles, dtype=jnp.int32)
    live = tile < meta[0]
    later = live[None, :] & (tile_expert[None, :] > tile_expert[:, None])
    nxt = jnp.min(jnp.where(later, tile_expert[None, :], n_experts), axis=1)
    nxt = jnp.where(nxt < n_experts, nxt, -1).astype(jnp.int32)
    starts = jnp.concatenate([jnp.ones((1,), jnp.int32),
                              (tile_expert[1:] != tile_expert[:-1]).astype(jnp.int32)])
    par = ((jnp.cumsum(starts) - 1) % 2).astype(jnp.int32)

    grid_spec = pltpu.PrefetchScalarGridSpec(
        num_scalar_prefetch=4,
        grid=(n_tiles,),
        in_specs=[
            pl.BlockSpec((rows, tm, PACK_W), slot),
            pl.BlockSpec(memory_space=pl.ANY),
            pl.BlockSpec(memory_space=pl.ANY),
            pl.BlockSpec(memory_space=pl.ANY),
        ],
        out_specs=pl.BlockSpec((rows, tm, PACK_W), slot),
        scratch_shapes=[pltpu.VMEM((d, f), BF16), pltpu.VMEM((d, f), BF16), pltpu.VMEM((f, d), BF16),
                        pltpu.VMEM((2, d, f), F32), pltpu.VMEM((2, d, f), F32), pltpu.VMEM((2, f, d), F32),
                        pltpu.SemaphoreType.DMA((2, 3))],
    )
    return pl.pallas_call(
        functools.partial(_expert_kernel, layer=layer),
        grid_spec=grid_spec,
        out_shape=jax.ShapeDtypeStruct(xs.shape, jnp.int32),
        compiler_params=_params(1),
        name="moe_experts",
    )(tile_expert, meta, nxt, par, xs, w_gate_e, w_up_e, w_down_e)


def _combine_kernel(x_ref, mod_ref, gpre_ref, gpost_ref, wgs_ref, wus_ref, wds_ref, yk_ref, wgt_ref,
                    out_ref):
    shift = mod_ref[0, 3:4, :]
    scale = mod_ref[0, 4:5, :]
    x = x_ref[...]
    t_hi = (_rms(x, gpre_ref[...]) * (1.0 + scale) + shift).astype(BF16)
    gate = _dot(t_hi, wgs_ref[...])
    act = (gate * _sigmoid(gate) * _dot(t_hi, wus_ref[...])).astype(BF16)
    acc = _dot(act, wds_ref[...])
    for k in range(TOP_K):
        y = _unpack_rows([yk_ref[k, j] for j in range(yk_ref.shape[1])])
        acc = acc + wgt_ref[:, k:k + 1] * y
    g2 = mod_ref[0, 5:6, :]
    out_ref[...] = x + g2 * _rms(acc, gpost_ref[...])


def _combine(x1, mod, seq_len, lw, yk, wgt, *, tm):
    t, d = x1.shape
    tiles_per_seq = max(seq_len // tm, 1)
    rows = yk.shape[1]
    row = lambda i: (i, 0)
    return pl.pallas_call(
        _combine_kernel,
        grid=(t // tm,),
        in_specs=[
            pl.BlockSpec((tm, d), row),
            mod.spec(lambda i: i // tiles_per_seq),
            lw["g_pre_ffn"].spec(),
            lw["g_post_ffn"].spec(),
            lw["w_gate_s"].spec(),
            lw["w_up_s"].spec(),
            lw["w_down_s"].spec(),
            pl.BlockSpec((TOP_K, rows, tm, PACK_W), lambda i: (0, 0, i, 0)),
            pl.BlockSpec((tm, LANE), row),
        ],
        out_specs=pl.BlockSpec((tm, d), row),
        out_shape=jax.ShapeDtypeStruct((t, d), F32),
        compiler_params=_params(1),
        name="moe_combine",
    )(x1, mod.table, lw["g_pre_ffn"].stacked, lw["g_post_ffn"].stacked, lw["w_gate_s"].stacked, lw["w_up_s"].stacked,
      lw["w_down_s"].stacked, yk, wgt)


def _sc_mesh():
    return plsc.VectorSubcoreMesh(core_axis_name="core", subcore_axis_name="subcore")


def _sc_scatter_rows(src, idx, n_out):
    n_lists, n = idx.shape
    width = src.shape[1]

    @pl.kernel(out_type=jax.ShapeDtypeStruct((n_out, width), src.dtype), mesh=_sc_mesh(), scratch_types=[])
    def scatter(x_hbm, *refs):
        i_hbms, o_hbm = refs[:n_lists], refs[n_lists]

        def body(x_vmem, *i_vmems):
            for i_vmem in i_vmems:
                pltpu.sync_copy(x_vmem, o_hbm.at[i_vmem.at[0]])

        pltpu.emit_pipeline(
            body,
            grid=(n // SC_WINDOW,),
            in_specs=[pl.BlockSpec((SC_WINDOW, width), lambda i: (i, 0))]
            + [pl.BlockSpec((1, SC_WINDOW), lambda i, r=r: (r, i)) for r in range(n_lists)],
            out_specs=[],
            core_axis_name=("core", "subcore"),
            dimension_semantics=(pltpu.PARALLEL,),
        )(x_hbm, *i_hbms)

    return scatter(src, *([idx] * n_lists))


def _sc_gather_rows(table, idx):
    n = idx.shape[0]
    width = table.shape[1]

    @pl.kernel(out_type=jax.ShapeDtypeStruct((n, width), table.dtype), mesh=_sc_mesh(), scratch_types=[])
    def gather(x_hbm, i_hbm, o_hbm):
        def body(i_vmem, o_vmem):
            pltpu.sync_copy(x_hbm.at[i_vmem.at[0]], o_vmem)

        pltpu.emit_pipeline(
            body,
            grid=(n // SC_WINDOW,),
            in_specs=[pl.BlockSpec((1, SC_WINDOW), lambda i: (0, i))],
            out_specs=[pl.BlockSpec((SC_WINDOW, width), lambda i: (i, 0))],
            core_axis_name=("core", "subcore"),
            dimension_semantics=(pltpu.PARALLEL,),
        )(i_hbm, o_hbm)

    return gather(table, idx.reshape(1, n))


def _moe_sparse(x1, routed, mod, seq_len, lw, w_gate_e, w_up_e, w_down_e, layer, *, tm, tm_e):
    t = x1.shape[0]
    n_experts = lw["b_router_t"].shape[0]
    assert n_experts * TOP_K == LANE
    n_row_tiles = (t * TOP_K) // tm_e + n_experts
    p = n_row_tiles * tm_e
    experts = jnp.arange(n_experts, dtype=jnp.int32)

    tp, sel, rank, cnt, wgt = routed
    rows = tp.shape[0]
    n_tok_tiles = sel.shape[0]
    cnt = cnt[:, :, 0].reshape(n_tok_tiles, TOP_K, n_experts).sum(axis=1)
    padded = (cnt.sum(axis=0) + tm_e - 1) // tm_e * tm_e
    group_end = jnp.cumsum(padded)
    base = (group_end - padded)[None, :] + jnp.cumsum(cnt, axis=0) - cnt
    chosen = sel[:, :TOP_K, :, None] == experts
    pos = jnp.sum(jnp.where(chosen, base[:, None, None, :], 0), axis=-1) + rank[:, :TOP_K, :]
    pos = pos.transpose(1, 0, 2).reshape(TOP_K, t)

    n_used = group_end[-1] // tm_e
    tile_start = jnp.arange(n_row_tiles, dtype=jnp.int32) * tm_e
    tile_expert = jnp.sum(tile_start[:, None] >= group_end[None, :], axis=1).astype(jnp.int32)
    tile_expert = jnp.minimum(tile_expert, n_experts - 1)
    tile_expert = jnp.where(tile_start < group_end[-1], tile_expert, tile_expert[n_used - 1])
    meta = jnp.stack([n_used, n_used]).astype(jnp.int32)

    idx = pos[:, None, :] + (jnp.arange(rows, dtype=jnp.int32) * p)[None, :, None]
    xs = _sc_scatter_rows(tp.reshape(rows * t, PACK_W), idx.reshape(TOP_K, rows * t), rows * p)
    after = yield wgt
    if after is not None:
        xs = _after(xs, after)
    ys = _experts(xs.reshape(rows, p, PACK_W), tile_expert, meta, w_gate_e, w_up_e, w_down_e, layer, tm=tm_e)
    yk = _sc_gather_rows(ys.reshape(rows * p, PACK_W), idx.reshape(-1))
    yield ys
    yield _combine(x1, mod, seq_len, lw, yk.reshape(TOP_K, rows, t, PACK_W), wgt, tm=tm)


def _rope_tables(n):
    rows = n // GRID_W
    r, col = jnp.meshgrid(jnp.arange(rows), jnp.arange(GRID_W), indexing="ij")
    r = r.reshape(-1).astype(F32)
    col = col.reshape(-1).astype(F32)
    pairs = QK_ROPE // 4
    inv = ROPE_BASE ** (-jnp.arange(pairs, dtype=F32) / pairs)
    ang = jnp.concatenate([r[:, None] * inv, col[:, None] * inv], axis=-1)
    cos, sin = jnp.cos(ang), jnp.sin(ang)
    pad = HEAD_PAD - QK_NOPE - QK_ROPE
    cos_t = jnp.concatenate([jnp.ones((n, QK_NOPE), F32), cos, cos, jnp.zeros((n, pad), F32)], axis=1)
    sin_t = jnp.concatenate([jnp.zeros((n, QK_NOPE), F32), sin, sin, jnp.zeros((n, pad), F32)], axis=1)
    return cos_t, sin_t


def _identity_tables(n):
    pad = HEAD_PAD - QK_NOPE - QK_ROPE
    cos_t = jnp.concatenate([jnp.ones((n, QK_NOPE + QK_ROPE), F32), jnp.zeros((n, pad), F32)], axis=1)
    return cos_t, jnp.zeros((n, HEAD_PAD), F32)


def _position_dft(n):
    nb = 64 if n % 64 == 0 else 1
    na = n // nb
    m = jnp.arange(n, dtype=jnp.int32)[None, :]
    ang_a = ((jnp.arange(na, dtype=jnp.int32)[:, None] * m) % na).astype(F32) * (2.0 * np.pi / na)
    ang_b = ((jnp.arange(nb, dtype=jnp.int32)[:, None] * m) % n).astype(F32) * (2.0 * np.pi / n)
    ca, sa = jnp.cos(ang_a)[:, None, :], jnp.sin(ang_a)[:, None, :]
    cb, sb = jnp.cos(ang_b)[None, :, :], jnp.sin(ang_b)[None, :, :]
    norm = 1.0 / np.sqrt(n)
    cos = ((ca * cb - sa * sb) * norm).reshape(n, n)
    sin = ((sa * cb + ca * sb) * (-norm)).reshape(n, n)
    return jnp.concatenate([cos, sin], axis=1).astype(BF16)


def _channel_dft(width):
    gc = width // FOURIER_GROUPS
    idx = (jnp.arange(gc, dtype=jnp.int32)[:, None] * jnp.arange(gc, dtype=jnp.int32)[None, :]) % gc
    ang = idx.astype(F32) * (2.0 * np.pi / gc)
    eye = jnp.eye(FOURIER_GROUPS, dtype=F32)
    norm = 1.0 / np.sqrt(gc)
    return jnp.concatenate([jnp.kron(eye, jnp.cos(ang) * norm), jnp.kron(eye, jnp.sin(ang) * norm)],
                           axis=1).astype(BF16)


def _w1_kernel(w_ref, o_ref, *, kv_rank):
    cols = w_ref.shape[2]
    kv_end = kv_rank + QK_ROPE
    half = QK_ROPE // 2
    tail = HEAD_PAD - QK_NOPE - QK_ROPE
    o_kpe = kv_rank
    o_rot = o_kpe + HEAD_PAD
    o_rest = o_rot + HEAD_PAD
    dt = o_ref.dtype
    o_ref[0, 0:kv_rank, :] = w_ref[0, 0:kv_rank, :].astype(dt)
    for base in (o_kpe, o_rot):
        o_ref[0, base:base + QK_NOPE, :] = jnp.zeros((QK_NOPE, cols), dt)
        o_ref[0, base + QK_NOPE + QK_ROPE:base + HEAD_PAD, :] = jnp.zeros((tail, cols), dt)
    o_ref[0, o_kpe + QK_NOPE:o_kpe + QK_NOPE + QK_ROPE, :] = w_ref[0, kv_rank:kv_end, :].astype(dt)
    o_ref[0, o_rot + QK_NOPE:o_rot + QK_NOPE + half, :] = (-w_ref[0, kv_rank + half:kv_end, :]).astype(dt)
    o_ref[0, o_rot + QK_NOPE + half:o_rot + QK_NOPE + QK_ROPE, :] = w_ref[0, kv_rank:kv_rank + half, :].astype(dt)
    o_ref[0, o_rest:, :] = w_ref[0, kv_end:, :].astype(dt)


def _prep_w1(w_in_t, kv_rank):
    n_layers, width, d = w_in_t.shape
    out_w = width - QK_ROPE + 2 * HEAD_PAD
    tc = _tile(d, 256)
    return pl.pallas_call(
        functools.partial(_w1_kernel, kv_rank=kv_rank),
        grid=(n_layers, d // tc),
        in_specs=[pl.BlockSpec((1, width, tc), lambda l, i: (l, 0, i))],
        out_specs=pl.BlockSpec((1, out_w, tc), lambda l, i: (l, 0, i)),
        out_shape=jax.ShapeDtypeStruct((n_layers, out_w, d), BF16),
        compiler_params=_params(2),
        name="prep_w1",
    )(w_in_t)


def _stacked_weights(g_pre_mix, g_post_mix, g_pre_ffn, g_post_ffn, b_gate, g_q, w_uq, g_kv, w_ukv,
                     w_mla_out, conv_w, w_conv_out, w_four_out, w_out, w_router, b_router, w_gate_s,
                     w_up_s, w_down_s):
    n_layers, kv_rank = g_kv.shape
    q_rank = g_q.shape[1]
    qk_dim = QK_NOPE + QK_ROPE
    uq = w_uq.reshape(n_layers, q_rank, N_HEADS, qk_dim) * (qk_dim ** -0.5 * np.log2(np.e))
    zq = jnp.zeros((n_layers, q_rank, N_HEADS, HEAD_PAD - qk_dim), F32)
    wuq = jnp.concatenate([uq, zq], axis=-1).reshape(n_layers, q_rank, N_HEADS * HEAD_PAD).astype(BF16)
    ukv = w_ukv.reshape(n_layers, kv_rank, N_HEADS, QK_NOPE + V_DIM)
    zk = jnp.zeros((n_layers, kv_rank, N_HEADS, HEAD_PAD - QK_NOPE), F32)
    zv = jnp.zeros((n_layers, kv_rank, N_HEADS, HEAD_PAD - V_DIM), F32)
    wuk = jnp.concatenate([ukv[..., :QK_NOPE], zk], axis=-1).reshape(n_layers, kv_rank, -1).astype(BF16)
    wuv = jnp.concatenate([ukv[..., QK_NOPE:], zv], axis=-1).reshape(n_layers, kv_rank, -1).astype(BF16)
    wr_hi = w_router.astype(BF16)
    wr_lo = (w_router - wr_hi.astype(F32)).astype(BF16)
    wr_t = jnp.concatenate([jnp.swapaxes(wr_hi, 1, 2), jnp.swapaxes(wr_lo, 1, 2)], axis=1)
    return {
        "g_pre_mix": g_pre_mix[:, None], "g_post_mix": g_post_mix[:, None],
        "g_pre_ffn": g_pre_ffn[:, None], "g_post_ffn": g_post_ffn[:, None],
        "b_gate": b_gate[:, None], "g_q": g_q[:, None], "g_kv": g_kv[:, None],
        "wuq": wuq, "wuk": wuk, "wuv": wuv, "conv_w": conv_w,
        "w_mla_out": w_mla_out.astype(BF16), "w_conv_out": w_conv_out.astype(BF16),
        "w_four_out": w_four_out.astype(BF16), "w_out": w_out.astype(BF16),
        "w_router_t": wr_t, "b_router_t": b_router[:, :, None],
        "w_gate_s": w_gate_s.astype(BF16), "w_up_s": w_up_s.astype(BF16), "w_down_s": w_down_s.astype(BF16),
    }


def _tile(n, pref):
    return pref if n % pref == 0 else n


def kernel(x, c, ctx, c_ctx, w_ada, b_ada, g_pre_mix, g_post_mix, g_pre_ffn, g_post_ffn, w_in, b_gate,
           g_q, w_uq, g_kv, w_ukv, w_mla_out, conv_w, w_conv_out, w_four_out, w_out, w_router, b_router,
           w_gate_e, w_up_e, w_down_e, w_gate_s, w_up_s, w_down_s):
    batch, seq, d = x.shape
    n_ctx = ctx.shape[1]
    n_layers = w_in.shape[0]
    xs = x.reshape(batch * seq, d)
    cs = ctx.reshape(batch * n_ctx, d)

    mod_rows = -(-(batch + 1) // (2 * SUBLANE)) * (2 * SUBLANE)
    c_all = jnp.concatenate([c, c_ctx[None], jnp.zeros((mod_rows - batch - 1, d), F32)], axis=0)
    ada = _ada(c_all, w_ada, b_ada).reshape(n_layers, mod_rows, 6, d)

    tm_x = _tile(seq, 512)
    tm_x_wide = _tile(seq, 1024)
    tq_x = _tile(seq, 2 * ATTN_ROWS)
    tm_c = _tile(n_ctx, 256)
    tm_c_mid = _tile(batch * n_ctx, 512)
    tm_c_wide = _tile(batch * n_ctx, 1024)
    te_x, te_c = 1024, 512

    tab_x = _rope_tables(seq)
    tab_c = _identity_tables(tm_c_mid)
    cs_x = _position_dft(seq)
    cs_c = _position_dft(n_ctx)
    dc = _channel_dft(w_four_out.shape[1])
    w1 = _prep_w1(jnp.swapaxes(w_in, 1, 2), g_kv.shape[1])
    stacked = _stacked_weights(g_pre_mix, g_post_mix, g_pre_ffn, g_post_ffn, b_gate, g_q, w_uq, g_kv, w_ukv,
                               w_mla_out, conv_w, w_conv_out, w_four_out, w_out, w_router, b_router,
                               w_gate_s, w_up_s, w_down_s)
    v_one = jnp.tile((jnp.arange(HEAD_PAD) == V_DIM).astype(F32), N_HEADS)[None]

    for l in range(n_layers):
        last = l == n_layers - 1
        lw = {name: _Layer(arr, l) for name, arr in stacked.items()}
        lw.update(w1=w1, layer=l, dc=dc, v_one=v_one)
        mod_x = _Mod(ada, l, 0, batch)
        mod_c = _Mod(ada, l, batch, 1)

        pc = _inproj(cs, mod_c, n_ctx, lw, tab_c, kv_only=last, tm=tm_c_mid)
        px = _inproj(xs, mod_x, seq, lw, tab_x, kv_only=False, tm=tm_x)
        o_x = _attention(px["q"], [(pc["k"], pc["v"], n_ctx), (px["k"], px["v"], seq)], batch, seq, tq=tq_x)
        f_x = _fourier(px["ab"], cs_x, batch, seq, tn=seq)
        x1, routed_x = _merge(xs, mod_x, seq, px, o_x, f_x, lw, tm=tm_x)
        moe_x = _moe_sparse(x1, routed_x, mod_x, seq, lw, w_gate_e, w_up_e, w_down_e, l, tm=tm_x_wide,
                            tm_e=te_x)
        if last:
            xs = list(moe_x)[-1]
        else:
            q_c = _after(pc["q"], next(moe_x))
            o_c = _attention(q_c, [(pc["k"], pc["v"], n_ctx)], batch, n_ctx, tq=tm_c)
            f_c = _fourier(pc["ab"], cs_c, batch, n_ctx, tn=tm_c)
            c1, routed_c = _merge(cs, mod_c, n_ctx, pc, o_c, f_c, lw, tm=tm_c_mid)
            moe_c = _moe_sparse(c1, routed_c, mod_c, batch * n_ctx, lw, w_gate_e, w_up_e, w_down_e, l,
                                tm=tm_c_wide, tm_e=te_c)
            next(moe_c)
            experts_x = moe_x.send(None)
            moe_c.send(experts_x)
            xs = next(moe_x)
            cs = next(moe_c)
    return xs.reshape(batch, seq, d)
```
